```python
import jax, jax.numpy as jnp
from jax import lax
import numpy as np

D_MODEL = 1024
BATCH = 8
SEQ = 4096
DEPTH = 1

MIX_WIDTH = D_MODEL
C_CONV = MIX_WIDTH // 2
CONV_HEADS = 8
C_POOL = MIX_WIDTH - C_CONV
POOL_WINDOWS = (2, 4, 8, 16)
N_POOL_GROUPS = len(POOL_WINDOWS)
POOL_GROUP = C_POOL // N_POOL_GROUPS
CONV_K = 31
D_FF = ((8 * D_MODEL // 3 + 255) // 256) * 256
D_PLE = 256
EPS = 1e-6

kernel_name = "hybrid_conformer_conv_multiscale_pool_block"


def rmsnorm(x, g):
    xf = x.astype(jnp.float32)
    y = xf * lax.rsqrt(jnp.mean(xf * xf, axis=-1, keepdims=True) + EPS)
    return (y * g.astype(jnp.float32)).astype(x.dtype)


def layernorm(x, g, b):
    xf = x.astype(jnp.float32)
    mu = jnp.mean(xf, axis=-1, keepdims=True)
    var = jnp.mean(jnp.square(xf - mu), axis=-1, keepdims=True)
    y = (xf - mu) * lax.rsqrt(var + EPS)
    return (y * g.astype(jnp.float32) + b.astype(jnp.float32)).astype(x.dtype)


def causal_depthwise_conv(u, w, b):
    c = u.shape[-1]
    k = w.reshape(CONV_K, 1, c).astype(u.dtype)
    y = lax.conv_general_dilated(
        u, k, window_strides=(1,), padding=[(CONV_K - 1, 0)],
        dimension_numbers=("NWC", "WIO", "NWC"), feature_group_count=c)
    return y + b.astype(u.dtype)


def multiscale_pool(v, pool_w, pool_scale):
    s = v.shape[1]
    vf = v.astype(jnp.float32)
    groups = vf.reshape(vf.shape[0], s, N_POOL_GROUPS, POOL_GROUP)
    pos1 = jnp.arange(1, s + 1, dtype=jnp.float32)[None, :, None]
    outs = []
    for g, w in enumerate(POOL_WINDOWS):
        vg = groups[:, :, g, :]
        c = jnp.cumsum(vg, axis=1)
        c_shift = jnp.pad(c, ((0, 0), (w, 0), (0, 0)))[:, :s]
        cnt = jnp.minimum(pos1, float(w))
        outs.append((c - c_shift) / cnt - vg)
    pooled = jnp.stack(outs, axis=2)
    mixed = jnp.einsum("bsgc,gcd->bsgd", pooled, pool_w.astype(jnp.float32))
    mixed = mixed.reshape(vf.shape) * pool_scale.astype(jnp.float32)
    return mixed.astype(v.dtype)


def _fwd_setup_inputs(seed: int = 0) -> dict:
    key = jax.random.key(seed)
    ks = jax.random.split(key, 20)
    L, D = DEPTH, D_MODEL
    n = lambda k, shape, fan: jax.random.normal(k, shape, jnp.float32) * (fan ** -0.5)
    gain = lambda k, shape: 1.0 + 0.05 * jax.random.normal(k, shape, jnp.float32)
    return {
        "x": jax.random.normal(ks[0], (BATCH, SEQ, D), jnp.float32),
        "p": jax.random.normal(ks[1], (DEPTH, BATCH, SEQ, D_PLE), jnp.float32),
        "g_mix": gain(ks[2], (L, D)),
        "w_in": n(ks[3], (L, D, 2 * C_CONV + C_POOL), D),
        "conv_w": n(ks[4], (L, CONV_K, C_CONV), CONV_K),
        "conv_b": 0.02 * jax.random.normal(ks[5], (L, C_CONV), jnp.float32),
        "ln_g": gain(ks[6], (L, C_CONV)),
        "ln_b": 0.02 * jax.random.normal(ks[7], (L, C_CONV), jnp.float32),
        "pool_w": n(ks[8], (L, N_POOL_GROUPS, POOL_GROUP, POOL_GROUP), POOL_GROUP),
        "pool_scale": gain(ks[9], (L, C_POOL)),
        "w_out": n(ks[10], (L, MIX_WIDTH, D), MIX_WIDTH),
        "g_ffn": gain(ks[11], (L, D)),
        "w_gate_up": n(ks[12], (L, D, 2 * D_FF), D),
        "w_down": n(ks[13], (L, D_FF, D), D_FF),
        "g_ple_gate": gain(ks[14], (L, D)),
        "w_ple_gate": n(ks[15], (L, D, D), D),
        "w_ple_up": n(ks[16], (L, D_PLE, D), D_PLE),
        "g_ple_post": gain(ks[17], (L, D)),
        "g_final": gain(ks[18], (D,)),
    }


def _fwd_reference(x, p, g_mix, w_in, conv_w, conv_b, ln_g, ln_b, pool_w, pool_scale,
              w_out, g_ffn, w_gate_up, w_down, g_ple_gate, w_ple_gate, w_ple_up,
              g_ple_post, g_final):
    for i in range(DEPTH):
        h = rmsnorm(x, g_mix[i])
        z = h @ w_in[i]
        a = z[..., :C_CONV]
        b = z[..., C_CONV:2 * C_CONV]
        v = z[..., 2 * C_CONV:]
        u = a * jax.nn.sigmoid(b)
        u = causal_depthwise_conv(u, conv_w[i], conv_b[i])
        u = jax.nn.silu(layernorm(u, ln_g[i], ln_b[i]))
        q = multiscale_pool(v, pool_w[i], pool_scale[i])
        mix = jnp.concatenate([u, q], axis=-1)
        x = x + mix @ w_out[i]
        h = rmsnorm(x, g_ffn[i])
        gu = h @ w_gate_up[i]
        x = x + (jax.nn.silu(gu[..., :D_FF]) * gu[..., D_FF:]) @ w_down[i]
        gate = jax.nn.sigmoid(rmsnorm(x, g_ple_gate[i]) @ w_ple_gate[i])
        e = rmsnorm(p[i].astype(x.dtype) @ w_ple_up[i], g_ple_post[i])
        x = x + gate * e
    return rmsnorm(x, g_final)


import jax as _jax
import jax.numpy as _jnp

TWIN_FORMAT = 'train_step'
FWD_PARAMS = ['x', 'p', 'g_mix', 'w_in', 'conv_w', 'conv_b', 'ln_g', 'ln_b', 'pool_w', 'pool_scale', 'w_out', 'g_ffn', 'w_gate_up', 'w_down', 'g_ple_gate', 'w_ple_gate', 'w_ple_up', 'g_ple_post', 'g_final']
TWIN_WEIGHTS = ['g_mix', 'w_in', 'conv_w', 'conv_b', 'ln_g', 'ln_b', 'pool_w', 'pool_scale', 'w_out', 'g_ffn', 'w_gate_up', 'w_down', 'g_ple_gate', 'w_ple_gate', 'w_ple_up', 'g_ple_post', 'g_final']
TWIN_DIFF_INPUT = 'x'
TWIN_INPUTS = ['x', 'p', 'g_mix', 'w_in', 'conv_w', 'conv_b', 'ln_g', 'ln_b', 'pool_w', 'pool_scale', 'w_out', 'g_ffn', 'w_gate_up', 'w_down', 'g_ple_gate', 'w_ple_gate', 'w_ple_up', 'g_ple_post', 'g_final', 'loss_target', 'm_g_mix', 'm_w_in', 'm_conv_w', 'm_conv_b', 'm_ln_g', 'm_ln_b', 'm_pool_w', 'm_pool_scale', 'm_w_out', 'm_g_ffn', 'm_w_gate_up', 'm_w_down', 'm_g_ple_gate', 'm_w_ple_gate', 'm_w_ple_up', 'm_g_ple_post', 'm_g_final', 'v_g_mix', 'v_w_in', 'v_conv_w', 'v_conv_b', 'v_ln_g', 'v_ln_b', 'v_pool_w', 'v_pool_scale', 'v_w_out', 'v_g_ffn', 'v_w_gate_up', 'v_w_down', 'v_g_ple_gate', 'v_w_ple_gate', 'v_w_ple_up', 'v_g_ple_post', 'v_g_final']
TWIN_OUTPUTS = ['loss', 'grad_x', 'grad_g_mix', 'grad_w_in', 'grad_conv_w', 'grad_conv_b', 'grad_ln_g', 'grad_ln_b', 'grad_pool_w', 'grad_pool_scale', 'grad_w_out', 'grad_g_ffn', 'grad_w_gate_up', 'grad_w_down', 'grad_g_ple_gate', 'grad_w_ple_gate', 'grad_w_ple_up', 'grad_g_ple_post', 'grad_g_final', 'delta_g_mix', 'delta_w_in', 'delta_conv_w', 'delta_conv_b', 'delta_ln_g', 'delta_ln_b', 'delta_pool_w', 'delta_pool_scale', 'delta_w_out', 'delta_g_ffn', 'delta_w_gate_up', 'delta_w_down', 'delta_g_ple_gate', 'delta_w_ple_gate', 'delta_w_ple_up', 'delta_g_ple_post', 'delta_g_final', 'new_m_g_mix', 'new_m_w_in', 'new_m_conv_w', 'new_m_conv_b', 'new_m_ln_g', 'new_m_ln_b', 'new_m_pool_w', 'new_m_pool_scale', 'new_m_w_out', 'new_m_g_ffn', 'new_m_w_gate_up', 'new_m_w_down', 'new_m_g_ple_gate', 'new_m_w_ple_gate', 'new_m_w_ple_up', 'new_m_g_ple_post', 'new_m_g_final', 'new_v_g_mix', 'new_v_w_in', 'new_v_conv_w', 'new_v_conv_b', 'new_v_ln_g', 'new_v_ln_b', 'new_v_pool_w', 'new_v_pool_scale', 'new_v_w_out', 'new_v_g_ffn', 'new_v_w_gate_up', 'new_v_w_down', 'new_v_g_ple_gate', 'new_v_w_ple_gate', 'new_v_w_ple_up', 'new_v_g_ple_post', 'new_v_g_final']
TWIN_LEAF_KINDS = {'loss': 'loss', 'grad_x': 'grad_x', 'grad_g_mix': 'grad_w', 'grad_w_in': 'grad_w', 'grad_conv_w': 'grad_w', 'grad_conv_b': 'grad_w', 'grad_ln_g': 'grad_w', 'grad_ln_b': 'grad_w', 'grad_pool_w': 'grad_w', 'grad_pool_scale': 'grad_w', 'grad_w_out': 'grad_w', 'grad_g_ffn': 'grad_w', 'grad_w_gate_up': 'grad_w', 'grad_w_down': 'grad_w', 'grad_g_ple_gate': 'grad_w', 'grad_w_ple_gate': 'grad_w', 'grad_w_ple_up': 'grad_w', 'grad_g_ple_post': 'grad_w', 'grad_g_final': 'grad_w', 'delta_g_mix': 'delta_w', 'delta_w_in': 'delta_w', 'delta_conv_w': 'delta_w', 'delta_conv_b': 'delta_w', 'delta_ln_g': 'delta_w', 'delta_ln_b': 'delta_w', 'delta_pool_w': 'delta_w', 'delta_pool_scale': 'delta_w', 'delta_w_out': 'delta_w', 'delta_g_ffn': 'delta_w', 'delta_w_gate_up': 'delta_w', 'delta_w_down': 'delta_w', 'delta_g_ple_gate': 'delta_w', 'delta_w_ple_gate': 'delta_w', 'delta_w_ple_up': 'delta_w', 'delta_g_ple_post': 'delta_w', 'delta_g_final': 'delta_w', 'new_m_g_mix': 'new_m', 'new_m_w_in': 'new_m', 'new_m_conv_w': 'new_m', 'new_m_conv_b': 'new_m', 'new_m_ln_g': 'new_m', 'new_m_ln_b': 'new_m', 'new_m_pool_w': 'new_m', 'new_m_pool_scale': 'new_m', 'new_m_w_out': 'new_m', 'new_m_g_ffn': 'new_m', 'new_m_w_gate_up': 'new_m', 'new_m_w_down': 'new_m', 'new_m_g_ple_gate': 'new_m', 'new_m_w_ple_gate': 'new_m', 'new_m_w_ple_up': 'new_m', 'new_m_g_ple_post': 'new_m', 'new_m_g_final': 'new_m', 'new_v_g_mix': 'new_v', 'new_v_w_in': 'new_v', 'new_v_conv_w': 'new_v', 'new_v_conv_b': 'new_v', 'new_v_ln_g': 'new_v', 'new_v_ln_b': 'new_v', 'new_v_pool_w': 'new_v', 'new_v_pool_scale': 'new_v', 'new_v_w_out': 'new_v', 'new_v_g_ffn': 'new_v', 'new_v_w_gate_up': 'new_v', 'new_v_w_down': 'new_v', 'new_v_g_ple_gate': 'new_v', 'new_v_w_ple_gate': 'new_v', 'new_v_w_ple_up': 'new_v', 'new_v_g_ple_post': 'new_v', 'new_v_g_final': 'new_v'}


def _forward(args):
    return _fwd_reference(*[args[k] for k in FWD_PARAMS])


def _output_shape():
    out = _jax.eval_shape(lambda: _forward(_fwd_setup_inputs(0)))
    return out.shape, out.dtype

N_MICROBATCH = 1
ADAM_LR = 0.001
ADAM_B1 = 0.9
ADAM_B2 = 0.999
ADAM_EPS = 1e-08
ADAM_WD = 0.01
ADAM_STEP = 10
PER_EXAMPLE_BATCH_AXIS = {'x': 0, 'p': 1, 'loss_target': 0}
SHARED_INPUTS = []
_WEIGHT_DTYPES = {'g_mix': _jnp.float32, 'w_in': _jnp.float32, 'conv_w': _jnp.float32, 'conv_b': _jnp.float32, 'ln_g': _jnp.float32, 'ln_b': _jnp.float32, 'pool_w': _jnp.float32, 'pool_scale': _jnp.float32, 'w_out': _jnp.float32, 'g_ffn': _jnp.float32, 'w_gate_up': _jnp.float32, 'w_down': _jnp.float32, 'g_ple_gate': _jnp.float32, 'w_ple_gate': _jnp.float32, 'w_ple_up': _jnp.float32, 'g_ple_post': _jnp.float32, 'g_final': _jnp.float32}
MOMENT_SCALE = {'g_mix': 1.152800e-01, 'w_in': 9.490895e-02, 'conv_w': 9.180420e-02, 'conv_b': 2.080672e-01, 'ln_g': 1.229684e-01, 'ln_b': 1.200703e-01, 'pool_w': 1.313890e-01, 'pool_scale': 1.441996e-01, 'w_out': 1.143077e-01, 'g_ffn': 1.085408e-01, 'w_gate_up': 4.484510e-02, 'w_down': 7.359528e-02, 'g_ple_gate': 2.815435e-02, 'w_ple_gate': 2.668202e-02, 'w_ple_up': 6.577032e-02, 'g_ple_post': 3.259597e-01, 'g_final': 3.208202e+01}


def _to_microbatches(a, axis):
    t = _jnp.moveaxis(a, axis, 0)
    t = t.reshape((N_MICROBATCH, t.shape[0] // N_MICROBATCH) + t.shape[1:])
    return _jnp.moveaxis(t, 1, axis + 1)


def setup_inputs(seed: int = 0) -> dict:
    inp = _fwd_setup_inputs(seed)
    key = _jax.random.fold_in(_jax.random.key(seed), 7919)
    shape, _ = _output_shape()
    out = dict(inp)
    out["loss_target"] = _jax.random.normal(_jax.random.fold_in(key, 0), shape, _jnp.float32)
    for i, name in enumerate(TWIN_WEIGHTS):
        w = inp[name].astype(_jnp.float32)
        if MOMENT_SCALE is None:
            s = _jnp.sqrt(_jnp.mean(_jnp.square(w)) + 1e-30)
        else:
            s = MOMENT_SCALE[name]
        km, kv = _jax.random.split(_jax.random.fold_in(key, i + 1))
        out[name] = w
        out["m_" + name] = s * _jax.random.normal(km, w.shape, _jnp.float32)
        out["v_" + name] = (s * s) * _jax.random.uniform(kv, w.shape, _jnp.float32, 0.5, 1.5)
    if N_MICROBATCH > 1:
        for name, axis in PER_EXAMPLE_BATCH_AXIS.items():
            out[name] = _to_microbatches(out[name], axis)
    return {'x': out['x'], 'p': out['p'], 'g_mix': out['g_mix'], 'w_in': out['w_in'], 'conv_w': out['conv_w'], 'conv_b': out['conv_b'], 'ln_g': out['ln_g'], 'ln_b': out['ln_b'], 'pool_w': out['pool_w'], 'pool_scale': out['pool_scale'], 'w_out': out['w_out'], 'g_ffn': out['g_ffn'], 'w_gate_up': out['w_gate_up'], 'w_down': out['w_down'], 'g_ple_gate': out['g_ple_gate'], 'w_ple_gate': out['w_ple_gate'], 'w_ple_up': out['w_ple_up'], 'g_ple_post': out['g_ple_post'], 'g_final': out['g_final'], 'loss_target': out['loss_target'], 'm_g_mix': out['m_g_mix'], 'm_w_in': out['m_w_in'], 'm_conv_w': out['m_conv_w'], 'm_conv_b': out['m_conv_b'], 'm_ln_g': out['m_ln_g'], 'm_ln_b': out['m_ln_b'], 'm_pool_w': out['m_pool_w'], 'm_pool_scale': out['m_pool_scale'], 'm_w_out': out['m_w_out'], 'm_g_ffn': out['m_g_ffn'], 'm_w_gate_up': out['m_w_gate_up'], 'm_w_down': out['m_w_down'], 'm_g_ple_gate': out['m_g_ple_gate'], 'm_w_ple_gate': out['m_w_ple_gate'], 'm_w_ple_up': out['m_w_ple_up'], 'm_g_ple_post': out['m_g_ple_post'], 'm_g_final': out['m_g_final'], 'v_g_mix': out['v_g_mix'], 'v_w_in': out['v_w_in'], 'v_conv_w': out['v_conv_w'], 'v_conv_b': out['v_conv_b'], 'v_ln_g': out['v_ln_g'], 'v_ln_b': out['v_ln_b'], 'v_pool_w': out['v_pool_w'], 'v_pool_scale': out['v_pool_scale'], 'v_w_out': out['v_w_out'], 'v_g_ffn': out['v_g_ffn'], 'v_w_gate_up': out['v_w_gate_up'], 'v_w_down': out['v_w_down'], 'v_g_ple_gate': out['v_g_ple_gate'], 'v_w_ple_gate': out['v_w_ple_gate'], 'v_w_ple_up': out['v_w_ple_up'], 'v_g_ple_post': out['v_g_ple_post'], 'v_g_final': out['v_g_final']}


def _loss(weights, diff, rest, loss_target):
    with _jax.named_scope("forward"):
        args = {**rest, TWIN_DIFF_INPUT: diff, **{k: w.astype(_WEIGHT_DTYPES[k]) for k, w in weights.items()}}
        y = _forward(args)
    with _jax.named_scope("loss_head"):
        err = _jnp.square(y.astype(_jnp.float32) - loss_target)
        return 0.5 * _jnp.sum(_jnp.mean(err, axis=-1)) if err.ndim else 0.5 * err


def _adamw(w, g, m, v):
    m = ADAM_B1 * m + (1.0 - ADAM_B1) * g
    v = ADAM_B2 * v + (1.0 - ADAM_B2) * _jnp.square(g)
    m_hat = m / (1.0 - ADAM_B1 ** ADAM_STEP)
    v_hat = v / (1.0 - ADAM_B2 ** ADAM_STEP)
    delta = -ADAM_LR * (m_hat / (_jnp.sqrt(v_hat) + ADAM_EPS) + ADAM_WD * w)
    return delta, m, v


def reference(x, p, g_mix, w_in, conv_w, conv_b, ln_g, ln_b, pool_w, pool_scale, w_out, g_ffn, w_gate_up, w_down, g_ple_gate, w_ple_gate, w_ple_up, g_ple_post, g_final, loss_target, m_g_mix, m_w_in, m_conv_w, m_conv_b, m_ln_g, m_ln_b, m_pool_w, m_pool_scale, m_w_out, m_g_ffn, m_w_gate_up, m_w_down, m_g_ple_gate, m_w_ple_gate, m_w_ple_up, m_g_ple_post, m_g_final, v_g_mix, v_w_in, v_conv_w, v_conv_b, v_ln_g, v_ln_b, v_pool_w, v_pool_scale, v_w_out, v_g_ffn, v_w_gate_up, v_w_down, v_g_ple_gate, v_w_ple_gate, v_w_ple_up, v_g_ple_post, v_g_final):
    given = dict(x=x, p=p, g_mix=g_mix, w_in=w_in, conv_w=conv_w, conv_b=conv_b, ln_g=ln_g, ln_b=ln_b, pool_w=pool_w, pool_scale=pool_scale, w_out=w_out, g_ffn=g_ffn, w_gate_up=w_gate_up, w_down=w_down, g_ple_gate=g_ple_gate, w_ple_gate=w_ple_gate, w_ple_up=w_ple_up, g_ple_post=g_ple_post, g_final=g_final, loss_target=loss_target, m_g_mix=m_g_mix, m_w_in=m_w_in, m_conv_w=m_conv_w, m_conv_b=m_conv_b, m_ln_g=m_ln_g, m_ln_b=m_ln_b, m_pool_w=m_pool_w, m_pool_scale=m_pool_scale, m_w_out=m_w_out, m_g_ffn=m_g_ffn, m_w_gate_up=m_w_gate_up, m_w_down=m_w_down, m_g_ple_gate=m_g_ple_gate, m_w_ple_gate=m_w_ple_gate, m_w_ple_up=m_w_ple_up, m_g_ple_post=m_g_ple_post, m_g_final=m_g_final, v_g_mix=v_g_mix, v_w_in=v_w_in, v_conv_w=v_conv_w, v_conv_b=v_conv_b, v_ln_g=v_ln_g, v_ln_b=v_ln_b, v_pool_w=v_pool_w, v_pool_scale=v_pool_scale, v_w_out=v_w_out, v_g_ffn=v_g_ffn, v_w_gate_up=v_w_gate_up, v_w_down=v_w_down, v_g_ple_gate=v_g_ple_gate, v_w_ple_gate=v_w_ple_gate, v_w_ple_up=v_w_ple_up, v_g_ple_post=v_g_ple_post, v_g_final=v_g_final)
    weights = {n: given[n] for n in TWIN_WEIGHTS}
    shared = {n: given[n] for n in SHARED_INPUTS}
    per_example = {n: given[n] for n in ['x', 'p']}
    grad_fn = _jax.value_and_grad(_loss, argnums=(0, 1))

    def one_microbatch(ex, loss_target):
        ex = dict(ex)
        diff = ex.pop(TWIN_DIFF_INPUT)
        return grad_fn(weights, diff, {**shared, **ex}, loss_target)

    if N_MICROBATCH == 1:
        loss, (grad_w, grad_x) = one_microbatch(per_example, given["loss_target"])
    else:
        def body(carry, xs):
            loss_sum, grad_sum = carry
            l_k, (gw_k, gx_k) = one_microbatch(xs[0], xs[1])
            with _jax.named_scope("update"):
                return (loss_sum + l_k, _jax.tree.map(_jnp.add, grad_sum, gw_k)), gx_k

        init = (_jnp.zeros((), _jnp.float32), _jax.tree.map(_jnp.zeros_like, weights))
        (loss, grad_w), grad_x = _jax.lax.scan(body, init, (per_example, given["loss_target"]))
    with _jax.named_scope("update"):
        delta_w, new_m, new_v = {}, {}, {}
        for n in TWIN_WEIGHTS:
            delta_w[n], new_m[n], new_v[n] = _adamw(weights[n], grad_w[n], given["m_" + n], given["v_" + n])
    return (loss, grad_x, *[grad_w[n] for n in TWIN_WEIGHTS], *[delta_w[n] for n in TWIN_WEIGHTS],
            *[new_m[n] for n in TWIN_WEIGHTS], *[new_v[n] for n in TWIN_WEIGHTS])
```

```python
import functools

import jax
import jax.numpy as jnp
from jax import lax
from jax.experimental import pallas as pl
from jax.experimental.pallas import tpu as pltpu

D_MODEL = 1024
C_CONV = 512
C_POOL = 512
Z_WIDTH = 2 * C_CONV + C_POOL
POOL_WINDOWS = (2, 4, 8, 16)
POOL_GROUP = 128
CONV_K = 31
D_FF = 2816
D_PLE = 256
EPS = 1e-6
N_DEV = 8

ADAM_LR = 0.001
ADAM_B1 = 0.9
ADAM_B2 = 0.999
ADAM_EPS = 1e-08
ADAM_WD = 0.01
ADAM_STEP = 10

CONV_HALO = 32
POOL_HALO = 16
ROW_CHUNK = 32
V7X_VMEM_LIMIT = 56 * 1024 * 1024
FF_CHUNK = D_FF // 2

BF16 = jnp.bfloat16
F32 = jnp.float32
MESH = pl.DeviceIdType.MESH


def _dot(a, b):
    return lax.dot_general(a, b, (((1,), (0,)), ((), ())), preferred_element_type=F32)


def _dot_nt(a, b):
    return lax.dot_general(a, b, (((1,), (1,)), ((), ())), preferred_element_type=F32)


def _dot_tn(a, b):
    return lax.dot_general(a, b, (((0,), (0,)), ((), ())), preferred_element_type=F32)


def _rms_fwd(x, g):
    r = lax.rsqrt(jnp.mean(x * x, axis=-1, keepdims=True) + EPS)
    return x * r * g, r


def _rms_bwd(x, r, g, dy):
    xr = x * r
    dg = jnp.sum(dy * xr, axis=0, keepdims=True)
    dyg = dy * g
    dx = r * (dyg - xr * jnp.mean(dyg * xr, axis=-1, keepdims=True))
    return dx, dg


def _sigmoid(x):
    return jax.nn.sigmoid(x)


def _params(sem=None):
    return pltpu.CompilerParams(dimension_semantics=sem, vmem_limit_bytes=V7X_VMEM_LIMIT)


def _row_spec(tm, width):
    return pl.BlockSpec((tm, width), lambda i: (i, 0))


def _const_spec(shape):
    return pl.BlockSpec(shape, lambda i: (0,) * len(shape))


def _fwd_mix(x, w_in_t, w_out, g_mix, conv_w, conv_b, ln_g, ln_b, pool_w, pool_scale, tm):
    s = x.shape[0]
    nt = s // tm

    def body(x_ref, win_ref, wout_ref, gmix_ref, cw_ref, cb_ref, lng_ref, lnb_ref, pw_ref, ps_ref,
             z_ref, u1_ref, pooled_ref, x1_ref, ubuf, vbuf):
        i = pl.program_id(0)

        @pl.when(i == 0)
        def _():
            ubuf[0:CONV_HALO, :] = jnp.zeros((CONV_HALO, C_CONV), F32)
            vbuf[0:POOL_HALO, :] = jnp.zeros((POOL_HALO, C_POOL), F32)

        xt = x_ref[...]
        h, _ = _rms_fwd(xt, gmix_ref[...])
        z = _dot_nt(h.astype(BF16), win_ref[...])
        z_ref[...] = z
        a = z[:, :C_CONV]
        b = z[:, C_CONV:2 * C_CONV]
        v = z[:, 2 * C_CONV:]
        ubuf[CONV_HALO:CONV_HALO + tm, :] = a * _sigmoid(b)
        vbuf[POOL_HALO:POOL_HALO + tm, :] = v

        for rc in range(tm // ROW_CHUNK):
            base = rc * ROW_CHUNK + CONV_HALO - (CONV_K - 1)
            acc = jnp.broadcast_to(cb_ref[...], (ROW_CHUNK, C_CONV))
            for k in range(CONV_K):
                acc = acc + cw_ref[k:k + 1, :] * ubuf[base + k:base + k + ROW_CHUNK, :]
            u1_ref[rc * ROW_CHUNK:(rc + 1) * ROW_CHUNK, :] = acc

        u1 = u1_ref[...]
        mu = jnp.mean(u1, axis=-1, keepdims=True)
        cen = u1 - mu
        rstd = lax.rsqrt(jnp.mean(cen * cen, axis=-1, keepdims=True) + EPS)
        u2 = cen * rstd * lng_ref[...] + lnb_ref[...]
        u = u2 * _sigmoid(u2)

        pos1 = (i * tm + lax.broadcasted_iota(jnp.int32, (tm, 1), 0) + 1).astype(F32)
        parts = [u]
        for g, w in enumerate(POOL_WINDOWS):
            cols = slice(g * POOL_GROUP, (g + 1) * POOL_GROUP)
            vg = v[:, cols]
            tot = vg
            for j in range(1, w):
                tot = tot + vbuf[POOL_HALO - j:POOL_HALO - j + tm, cols]
            pooled = tot / jnp.minimum(pos1, float(w)) - vg
            pooled_b = pooled.astype(BF16)
            pooled_ref[:, cols] = pooled_b
            parts.append(_dot(pooled_b, pw_ref[g]) * ps_ref[:, cols])
        mix = jnp.concatenate(parts, axis=-1).astype(BF16)
        x1_ref[...] = xt + _dot(mix, wout_ref[...])

        ubuf[0:CONV_HALO, :] = ubuf[tm:tm + CONV_HALO, :]
        vbuf[0:POOL_HALO, :] = vbuf[tm:tm + POOL_HALO, :]

    return pl.pallas_call(
        body,
        grid=(nt,),
        in_specs=[
            _row_spec(tm, D_MODEL),
            _const_spec((Z_WIDTH, D_MODEL)),
            _const_spec((D_MODEL, D_MODEL)),
            _const_spec((1, D_MODEL)),
            _const_spec((CONV_HALO, C_CONV)),
            _const_spec((1, C_CONV)),
            _const_spec((1, C_CONV)),
            _const_spec((1, C_CONV)),
            _const_spec((len(POOL_WINDOWS), POOL_GROUP, POOL_GROUP)),
            _const_spec((1, C_POOL)),
        ],
        out_specs=[
            _row_spec(tm, Z_WIDTH),
            _row_spec(tm, C_CONV),
            _row_spec(tm, C_POOL),
            _row_spec(tm, D_MODEL),
        ],
        out_shape=[
            jax.ShapeDtypeStruct((s, Z_WIDTH), F32),
            jax.ShapeDtypeStruct((s, C_CONV), F32),
            jax.ShapeDtypeStruct((s, C_POOL), BF16),
            jax.ShapeDtypeStruct((s, D_MODEL), F32),
        ],
        scratch_shapes=[
            pltpu.VMEM((tm + CONV_HALO, C_CONV), F32),
            pltpu.VMEM((tm + POOL_HALO, C_POOL), F32),
        ],
        compiler_params=_params(("arbitrary",)),
        name="fwd_mix",
    )(x, w_in_t, w_out, g_mix, conv_w, conv_b, ln_g, ln_b, pool_w, pool_scale)


def _ffn_fwd(x1, w_gu_t, w_down, g_ffn, tm):
    s = x1.shape[0]
    nt = s // tm
    nc = D_FF // FF_CHUNK

    def body(x1_ref, wg_ref, wu_ref, wd_ref, gffn_ref, h2_ref, g_ref, u_ref, act_ref, x2_ref, h2_buf, acc):
        c = pl.program_id(1)

        @pl.when(c == 0)
        def _():
            h, _ = _rms_fwd(x1_ref[...], gffn_ref[...])
            h2_buf[...] = h.astype(BF16)
            h2_ref[...] = h2_buf[...]
            acc[...] = jnp.zeros_like(acc)

        h2 = h2_buf[...]
        g = _dot_nt(h2, wg_ref[...])
        u = _dot_nt(h2, wu_ref[...])
        act = (g * _sigmoid(g) * u).astype(BF16)
        g_ref[...] = g.astype(BF16)
        u_ref[...] = u.astype(BF16)
        act_ref[...] = act
        acc[...] += _dot(act, wd_ref[...])

        @pl.when(c == nc - 1)
        def _():
            x2_ref[...] = x1_ref[...] + acc[...]

    tok = lambda width: pl.BlockSpec((tm, width), lambda i, c: (i, 0))
    chunk = pl.BlockSpec((tm, FF_CHUNK), lambda i, c: (i, c))
    return pl.pallas_call(
        body,
        grid=(nt, nc),
        in_specs=[
            tok(D_MODEL),
            pl.BlockSpec((FF_CHUNK, D_MODEL), lambda i, c: (c, 0)),
            pl.BlockSpec((FF_CHUNK, D_MODEL), lambda i, c: (c + nc, 0)),
            pl.BlockSpec((FF_CHUNK, D_MODEL), lambda i, c: (c, 0)),
            pl.BlockSpec((1, D_MODEL), lambda i, c: (0, 0)),
        ],
        out_specs=[tok(D_MODEL), chunk, chunk, chunk, tok(D_MODEL)],
        out_shape=[
            jax.ShapeDtypeStruct((s, D_MODEL), BF16),
            jax.ShapeDtypeStruct((s, D_FF), BF16),
            jax.ShapeDtypeStruct((s, D_FF), BF16),
            jax.ShapeDtypeStruct((s, D_FF), BF16),
            jax.ShapeDtypeStruct((s, D_MODEL), F32),
        ],
        scratch_shapes=[pltpu.VMEM((tm, D_MODEL), BF16), pltpu.VMEM((tm, D_MODEL), F32)],
        compiler_params=_params(("arbitrary", "arbitrary")),
        name="ffn_fwd",
    )(x1, w_gu_t, w_gu_t, w_down, g_ffn)


def _ple_loss(x2, p, tgt, w_pg, w_pu_t, g_gate, g_post, g_final, tm):
    s = x2.shape[0]
    nt = s // tm

    def body(x2_ref, p_ref, t_ref, wpg_ref, wpu_ref, gg_ref, gp_ref, gf_ref,
             dx2_ref, dx2b_ref, hg_ref, ds_ref, dpe_ref, pb_ref, stats_ref):
        i = pl.program_id(0)

        @pl.when(i == 0)
        def _():
            stats_ref[...] = jnp.zeros_like(stats_ref)

        x2 = x2_ref[...]
        hg, rg = _rms_fwd(x2, gg_ref[...])
        hg_b = hg.astype(BF16)
        hg_ref[...] = hg_b
        gate = _sigmoid(_dot(hg_b, wpg_ref[...]))
        pb = p_ref[...].astype(BF16)
        pb_ref[...] = pb
        pe = _dot_nt(pb, wpu_ref[...])
        e, rp = _rms_fwd(pe, gp_ref[...])
        x3 = x2 + gate * e
        y, r3 = _rms_fwd(x3, gf_ref[...])
        diff = y - t_ref[...]
        loss = 0.5 * jnp.sum(jnp.sum(diff * diff, axis=-1, keepdims=True), axis=0, keepdims=True) / D_MODEL
        dy = diff * (1.0 / D_MODEL)

        dx3, dgf = _rms_bwd(x3, r3, gf_ref[...], dy)
        dpe, dgp = _rms_bwd(pe, rp, gp_ref[...], dx3 * gate)
        dpe_ref[...] = dpe.astype(BF16)
        ds = (dx3 * e * gate * (1.0 - gate)).astype(BF16)
        ds_ref[...] = ds
        dhg = _dot_nt(ds, wpg_ref[...])
        dxg, dgg = _rms_bwd(x2, rg, gg_ref[...], dhg)
        dx2 = dx3 + dxg
        dx2_ref[...] = dx2
        dx2b_ref[...] = dx2.astype(BF16)

        stats_ref[0:1, :] += dgf
        stats_ref[1:2, :] += dgp
        stats_ref[2:3, :] += dgg
        stats_ref[3:4, :] += jnp.broadcast_to(loss, (1, D_MODEL))

    return pl.pallas_call(
        body,
        grid=(nt,),
        in_specs=[
            _row_spec(tm, D_MODEL),
            _row_spec(tm, D_PLE),
            _row_spec(tm, D_MODEL),
            _const_spec((D_MODEL, D_MODEL)),
            _const_spec((D_MODEL, D_PLE)),
            _const_spec((1, D_MODEL)),
            _const_spec((1, D_MODEL)),
            _const_spec((1, D_MODEL)),
        ],
        out_specs=[
            _row_spec(tm, D_MODEL),
            _row_spec(tm, D_MODEL),
            _row_spec(tm, D_MODEL),
            _row_spec(tm, D_MODEL),
            _row_spec(tm, D_MODEL),
            _row_spec(tm, D_PLE),
            _const_spec((8, D_MODEL)),
        ],
        out_shape=[
            jax.ShapeDtypeStruct((s, D_MODEL), F32),
            jax.ShapeDtypeStruct((s, D_MODEL), BF16),
            jax.ShapeDtypeStruct((s, D_MODEL), BF16),
            jax.ShapeDtypeStruct((s, D_MODEL), BF16),
            jax.ShapeDtypeStruct((s, D_MODEL), BF16),
            jax.ShapeDtypeStruct((s, D_PLE), BF16),
            jax.ShapeDtypeStruct((8, D_MODEL), F32),
        ],
        compiler_params=_params(("arbitrary",)),
        name="ple_loss",
    )(x2, p, tgt, w_pg, w_pu_t, g_gate, g_post, g_final)


def _ffn_bwd(dx2, dx2b, x1, g_sav, u_sav, w_gu_t, w_down, g_ffn, tm):
    s = x1.shape[0]
    nt = s // tm
    nc = D_FF // FF_CHUNK

    def body(dx2_ref, dx2b_ref, x1_ref, g_ref, u_ref, wg_ref, wu_ref, wd_ref, gffn_ref,
             dg_ref, du_ref, dx1_ref, dx1b_ref, stats_ref, acc):
        i = pl.program_id(0)
        c = pl.program_id(1)

        @pl.when(jnp.logical_and(i == 0, c == 0))
        def _():
            stats_ref[...] = jnp.zeros_like(stats_ref)

        @pl.when(c == 0)
        def _():
            acc[...] = jnp.zeros_like(acc)

        dact = _dot_nt(dx2b_ref[...], wd_ref[...])
        g = g_ref[...].astype(F32)
        u = u_ref[...].astype(F32)
        sg = _sigmoid(g)
        dg = (dact * u * sg * (1.0 + g * (1.0 - sg))).astype(BF16)
        du = (dact * g * sg).astype(BF16)
        dg_ref[...] = dg
        du_ref[...] = du
        acc[...] += _dot(dg, wg_ref[...]) + _dot(du, wu_ref[...])

        @pl.when(c == nc - 1)
        def _():
            x1 = x1_ref[...]
            r2 = lax.rsqrt(jnp.mean(x1 * x1, axis=-1, keepdims=True) + EPS)
            dxn, dgf = _rms_bwd(x1, r2, gffn_ref[...], acc[...])
            dx1 = dx2_ref[...] + dxn
            dx1_ref[...] = dx1
            dx1b_ref[...] = dx1.astype(BF16)
            stats_ref[0:1, :] += dgf

    tok = lambda width: pl.BlockSpec((tm, width), lambda i, c: (i, 0))
    chunk = pl.BlockSpec((tm, FF_CHUNK), lambda i, c: (i, c))
    return pl.pallas_call(
        body,
        grid=(nt, nc),
        in_specs=[
            tok(D_MODEL), tok(D_MODEL), tok(D_MODEL), chunk, chunk,
            pl.BlockSpec((FF_CHUNK, D_MODEL), lambda i, c: (c, 0)),
            pl.BlockSpec((FF_CHUNK, D_MODEL), lambda i, c: (c + nc, 0)),
            pl.BlockSpec((FF_CHUNK, D_MODEL), lambda i, c: (c, 0)),
            pl.BlockSpec((1, D_MODEL), lambda i, c: (0, 0)),
        ],
        out_specs=[chunk, chunk, tok(D_MODEL), tok(D_MODEL), pl.BlockSpec((8, D_MODEL), lambda i, c: (0, 0))],
        out_shape=[
            jax.ShapeDtypeStruct((s, D_FF), BF16),
            jax.ShapeDtypeStruct((s, D_FF), BF16),
            jax.ShapeDtypeStruct((s, D_MODEL), F32),
            jax.ShapeDtypeStruct((s, D_MODEL), BF16),
            jax.ShapeDtypeStruct((8, D_MODEL), F32),
        ],
        scratch_shapes=[pltpu.VMEM((tm, D_MODEL), F32)],
        compiler_params=_params(("arbitrary", "arbitrary")),
        name="ffn_bwd",
    )(dx2, dx2b, x1, g_sav, u_sav, w_gu_t, w_gu_t, w_down, g_ffn)


def _bwd_mix(dx1, dx1b, x, z, u1, pooled, w_in_t, w_out, g_mix, conv_w, ln_g, ln_b, pool_w, pool_scale, tm):
    s = x.shape[0]
    nt = s // tm

    def body(dx1_ref, dx1b_ref, x_ref, z_ref, u1_ref, pooled_ref, win_ref, wout_ref, gmix_ref, cw_ref,
             lng_ref, lnb_ref, pw_ref, ps_ref,
             dx_ref, dz_ref, mix_ref, h1_ref, vec_ref, dcw_ref, dpw_ref, dubuf, dvbuf, u0buf, du0buf):
        i = pl.program_id(0)
        tile = nt - 1 - i

        @pl.when(i == 0)
        def _():
            vec_ref[...] = jnp.zeros_like(vec_ref)
            dcw_ref[...] = jnp.zeros_like(dcw_ref)
            dpw_ref[...] = jnp.zeros_like(dpw_ref)
            dubuf[tm:tm + CONV_HALO, :] = jnp.zeros((CONV_HALO, C_CONV), F32)
            dvbuf[tm:tm + POOL_HALO, :] = jnp.zeros((POOL_HALO, C_POOL), F32)

        dmix = _dot_nt(dx1b_ref[...], wout_ref[...])
        du = dmix[:, :C_CONV]
        dq = dmix[:, C_CONV:]

        pos1 = (tile * tm + lax.broadcasted_iota(jnp.int32, (tm, 1), 0) + 1).astype(F32)
        q_parts = []
        dpooled_parts = []
        dps_rows = []
        for g, w in enumerate(POOL_WINDOWS):
            cols = slice(g * POOL_GROUP, (g + 1) * POOL_GROUP)
            pooled_b = pooled_ref[:, cols]
            mixed = _dot(pooled_b, pw_ref[g])
            dqg = dq[:, cols]
            dps_rows.append(jnp.sum(dqg * mixed, axis=0, keepdims=True))
            q_parts.append(mixed * ps_ref[:, cols])
            dmixed = (dqg * ps_ref[:, cols]).astype(BF16)
            dpw_ref[g] += _dot_tn(pooled_b, dmixed)
            dpooled = _dot_nt(dmixed, pw_ref[g])
            dpooled_parts.append(dpooled)
            dvbuf[0:tm, cols] = dpooled / jnp.minimum(pos1, float(w))
        vec_ref[4:5, 0:C_POOL] += jnp.concatenate(dps_rows, axis=-1)
        dv_parts = []
        for g, w in enumerate(POOL_WINDOWS):
            cols = slice(g * POOL_GROUP, (g + 1) * POOL_GROUP)
            tot = dvbuf[0:tm, cols]
            for j in range(1, w):
                tot = tot + dvbuf[j:j + tm, cols]
            dv_parts.append(tot - dpooled_parts[g])

        u1 = u1_ref[...]
        mu = jnp.mean(u1, axis=-1, keepdims=True)
        cen = u1 - mu
        rstd = lax.rsqrt(jnp.mean(cen * cen, axis=-1, keepdims=True) + EPS)
        xhat = cen * rstd
        u2 = xhat * lng_ref[...] + lnb_ref[...]
        sg2 = _sigmoid(u2)
        du2 = du * sg2 * (1.0 + u2 * (1.0 - sg2))
        vec_ref[1:2, 0:C_CONV] += jnp.sum(du2 * xhat, axis=0, keepdims=True)
        vec_ref[2:3, 0:C_CONV] += jnp.sum(du2, axis=0, keepdims=True)
        t1 = du2 * lng_ref[...]
        du1 = rstd * (t1 - jnp.mean(t1, axis=-1, keepdims=True)
                      - xhat * jnp.mean(t1 * xhat, axis=-1, keepdims=True))
        vec_ref[3:4, 0:C_CONV] += jnp.sum(du1, axis=0, keepdims=True)
        dubuf[0:tm, :] = du1

        zt = z_ref[...]
        a = zt[:, :C_CONV]
        sgb = _sigmoid(zt[:, C_CONV:2 * C_CONV])
        u0buf[...] = a * sgb

        for rc in range(tm // ROW_CHUNK):
            r0 = rc * ROW_CHUNK
            acc = jnp.zeros((ROW_CHUNK, C_CONV), F32)
            for k in range(CONV_K):
                off = r0 + (CONV_K - 1) - k
                acc = acc + cw_ref[k:k + 1, :] * dubuf[off:off + ROW_CHUNK, :]
            du0buf[r0:r0 + ROW_CHUNK, :] = acc
        for k in range(CONV_K):
            acc = jnp.zeros((ROW_CHUNK, C_CONV), F32)
            for rc in range(tm // ROW_CHUNK):
                r0 = rc * ROW_CHUNK
                off = r0 + (CONV_K - 1) - k
                acc = acc + u0buf[r0:r0 + ROW_CHUNK, :] * dubuf[off:off + ROW_CHUNK, :]
            dcw_ref[k:k + 1, :] += jnp.sum(acc, axis=0, keepdims=True)
        du0 = du0buf[...]

        da = du0 * sgb
        db = du0 * a * sgb * (1.0 - sgb)
        dz = jnp.concatenate([da, db] + dv_parts, axis=-1).astype(BF16)
        dz_ref[...] = dz

        mix_ref[...] = jnp.concatenate([u2 * sg2] + q_parts, axis=-1).astype(BF16)

        xt = x_ref[...]
        h1, r1 = _rms_fwd(xt, gmix_ref[...])
        h1_ref[...] = h1.astype(BF16)
        dh1 = _dot(dz, win_ref[...])
        dxn, dgm = _rms_bwd(xt, r1, gmix_ref[...], dh1)
        dx_ref[...] = dx1_ref[...] + dxn
        vec_ref[0:1, :] += dgm

        dubuf[tm:tm + CONV_HALO, :] = dubuf[0:CONV_HALO, :]
        dvbuf[tm:tm + POOL_HALO, :] = dvbuf[0:POOL_HALO, :]

    rev = lambda width: pl.BlockSpec((tm, width), lambda i: (nt - 1 - i, 0))
    return pl.pallas_call(
        body,
        grid=(nt,),
        in_specs=[
            rev(D_MODEL), rev(D_MODEL), rev(D_MODEL), rev(Z_WIDTH), rev(C_CONV), rev(C_POOL),
            _const_spec((Z_WIDTH, D_MODEL)),
            _const_spec((D_MODEL, D_MODEL)),
            _const_spec((1, D_MODEL)),
            _const_spec((CONV_HALO, C_CONV)),
            _const_spec((1, C_CONV)),
            _const_spec((1, C_CONV)),
            _const_spec((len(POOL_WINDOWS), POOL_GROUP, POOL_GROUP)),
            _const_spec((1, C_POOL)),
        ],
        out_specs=[
            rev(D_MODEL), rev(Z_WIDTH), rev(D_MODEL), rev(D_MODEL),
            _const_spec((8, D_MODEL)),
            _const_spec((CONV_HALO, C_CONV)),
            _const_spec((len(POOL_WINDOWS), POOL_GROUP, POOL_GROUP)),
        ],
        out_shape=[
            jax.ShapeDtypeStruct((s, D_MODEL), F32),
            jax.ShapeDtypeStruct((s, Z_WIDTH), BF16),
            jax.ShapeDtypeStruct((s, D_MODEL), BF16),
            jax.ShapeDtypeStruct((s, D_MODEL), BF16),
            jax.ShapeDtypeStruct((8, D_MODEL), F32),
            jax.ShapeDtypeStruct((CONV_HALO, C_CONV), F32),
            jax.ShapeDtypeStruct((len(POOL_WINDOWS), POOL_GROUP, POOL_GROUP), F32),
        ],
        scratch_shapes=[
            pltpu.VMEM((tm + CONV_HALO, C_CONV), F32),
            pltpu.VMEM((tm + POOL_HALO, C_POOL), F32),
            pltpu.VMEM((tm, C_CONV), F32),
            pltpu.VMEM((tm, C_CONV), F32),
        ],
        compiler_params=_params(("arbitrary",)),
        name="bwd_mix",
    )(dx1, dx1b, x, z, u1, pooled, w_in_t, w_out, g_mix, conv_w, ln_g, ln_b, pool_w, pool_scale)


def _grad_matmul(a, b, bm, name):
    s, ma = a.shape
    nb = b.shape[1]

    def body(a_ref, b_ref, o_ref):
        o_ref[...] = _dot_tn(a_ref[...], b_ref[...]).astype(BF16)

    return pl.pallas_call(
        body,
        grid=(ma // bm,),
        in_specs=[pl.BlockSpec((s, bm), lambda i: (0, i)), pl.BlockSpec((s, nb), lambda i: (0, 0))],
        out_specs=pl.BlockSpec((bm, nb), lambda i: (i, 0)),
        out_shape=jax.ShapeDtypeStruct((ma, nb), BF16),
        compiler_params=_params(("arbitrary",)),
        name=name,
    )(a, b)


def _place():
    x, y, c = lax.axis_index("x"), lax.axis_index("y"), lax.axis_index("c")
    chips = [(1 - x, y), (x, 1 - y), (1 - x, 1 - y)]
    return (x, y, c), (x, y, 1 - c), chips


def _block(px, py, pc):
    return 4 * px + 2 * py + pc


def _all_gather(shards, name):
    n = len(shards)

    def body(*refs):
        src = refs[:n]
        dst = refs[n:2 * n]
        send_sems, recv_sems, local_sems = refs[2 * n:]
        me, sib, chips = _place()
        c = me[2]

        def copy(a, k, block, to, from_src=False):
            rows = dst[a].at[_block(*block)]
            return pltpu.make_async_remote_copy(
                src_ref=src[a] if from_src else rows, dst_ref=rows,
                send_sem=send_sems.at[a, k], recv_sem=recv_sems.at[a, k],
                device_id=to, device_id_type=MESH)

        started = []
        mine = []
        for a in range(n):
            cp = pltpu.make_async_copy(src[a], dst[a].at[_block(*me)], local_sems.at[a])
            cp.start()
            mine.append(cp)
            first = [copy(a, 0, me, sib, True)]
            first += [copy(a, 1 + j, me, (*chip, c), True) for j, chip in enumerate(chips)]
            for cp in first:
                cp.start()
            started += first
        for a in range(n):
            for j, chip in enumerate(chips):
                copy(a, 1 + j, (*chip, c), me).wait_recv()
                fwd = copy(a, 4 + j, (*chip, c), sib)
                fwd.start()
                started.append(fwd)
        for a in range(n):
            copy(a, 0, sib, me).wait_recv()
            for j, chip in enumerate(chips):
                copy(a, 4 + j, (*chip, 1 - c), me).wait_recv()
        for cp in started:
            cp.wait_send()
        for cp in mine:
            cp.wait()

    any_spec = pl.BlockSpec(memory_space=pl.ANY)
    return pl.pallas_call(
        body,
        in_specs=[any_spec] * n,
        out_specs=[any_spec] * n,
        out_shape=[jax.ShapeDtypeStruct((N_DEV,) + sh.shape, sh.dtype) for sh in shards],
        scratch_shapes=[
            pltpu.SemaphoreType.DMA((n, 7)),
            pltpu.SemaphoreType.DMA((n, 7)),
            pltpu.SemaphoreType.DMA((n,)),
        ],
        name=name,
    )(*shards)


def _reduce_scatter(grads, name):
    n = len(grads)
    shapes = [g.shape[1:] for g in grads]

    def body(*refs):
        g = refs[:n]
        out = refs[n:2 * n]
        own = refs[2 * n:3 * n]
        loc = refs[3 * n:4 * n]
        r1 = refs[4 * n:5 * n]
        r2 = refs[5 * n:6 * n]
        load_sems, s1, q1, s2, q2 = refs[6 * n:]
        me, sib, chips = _place()
        c = me[2]

        loads = []
        sends = []
        for a in range(n):
            ld = [pltpu.make_async_copy(g[a].at[_block(*chip, c)], loc[a].at[j], load_sems.at[a, j])
                  for j, chip in enumerate(chips)]
            ld.append(pltpu.make_async_copy(g[a].at[_block(*me)], own[a], load_sems.at[a, 3]))
            for cp in ld:
                cp.start()
            loads.append(ld)
            blocks = [(*chip, 1 - c) for chip in chips] + [sib]
            for j, blk in enumerate(blocks):
                cp = pltpu.make_async_remote_copy(
                    src_ref=g[a].at[_block(*blk)], dst_ref=r1[a].at[j],
                    send_sem=s1.at[a, j], recv_sem=q1.at[a, j], device_id=sib, device_id_type=MESH)
                cp.start()
                sends.append(cp)

        def from_sibling(a, j):
            return pltpu.make_async_remote_copy(
                src_ref=r1[a].at[j], dst_ref=r1[a].at[j], send_sem=s1.at[a, j], recv_sem=q1.at[a, j],
                device_id=sib, device_id_type=MESH)

        def partial(a, j, chip):
            return pltpu.make_async_remote_copy(
                src_ref=loc[a].at[j], dst_ref=r2[a].at[j], send_sem=s2.at[a, j], recv_sem=q2.at[a, j],
                device_id=(*chip, c), device_id_type=MESH)

        for a in range(n):
            for j, chip in enumerate(chips):
                loads[a][j].wait()
                from_sibling(a, j).wait_recv()
                loc[a][j] = (loc[a][j].astype(F32) + r1[a][j].astype(F32)).astype(BF16)
                cp = partial(a, j, chip)
                cp.start()
                sends.append(cp)
        for a in range(n):
            loads[a][3].wait()
            from_sibling(a, 3).wait_recv()
            acc = own[a][...].astype(F32) + r1[a][3].astype(F32)
            for j, chip in enumerate(chips):
                partial(a, j, chip).wait_recv()
                acc = acc + r2[a][j].astype(F32)
            out[a][...] = acc
        for cp in sends:
            cp.wait_send()

    any_spec = pl.BlockSpec(memory_space=pl.ANY)
    vmem_spec = pl.BlockSpec(memory_space=pltpu.VMEM)
    return pl.pallas_call(
        body,
        in_specs=[any_spec] * n,
        out_specs=[vmem_spec] * n,
        out_shape=[jax.ShapeDtypeStruct(sh, F32) for sh in shapes],
        scratch_shapes=(
            [pltpu.VMEM(sh, BF16) for sh in shapes]
            + [pltpu.VMEM((3,) + sh, BF16) for sh in shapes]
            + [pltpu.VMEM((4,) + sh, BF16) for sh in shapes]
            + [pltpu.VMEM((3,) + sh, BF16) for sh in shapes]
            + [pltpu.SemaphoreType.DMA((n, 4)),
               pltpu.SemaphoreType.DMA((n, 4)), pltpu.SemaphoreType.DMA((n, 4)),
               pltpu.SemaphoreType.DMA((n, 3)), pltpu.SemaphoreType.DMA((n, 3))]
        ),
        compiler_params=pltpu.CompilerParams(vmem_limit_bytes=V7X_VMEM_LIMIT),
        name=name,
    )(*grads)


def _sum_devices(parts, name):
    def body(p_ref, o_ref):
        acc = p_ref[0]
        for d in range(1, N_DEV):
            acc = acc + p_ref[d]
        o_ref[...] = acc

    return pl.pallas_call(
        body,
        out_shape=jax.ShapeDtypeStruct(parts.shape[1:], parts.dtype),
        name=name,
    )(parts)


def _adamw(w, g, m, v, name):
    rows, cols = w.shape
    br = rows
    for cand in (512, 256, 128, 64, 32, 16, 8):
        if rows % cand == 0 and rows > cand:
            br = cand
            break

    def body(w_ref, g_ref, m_ref, v_ref, d_ref, nm_ref, nv_ref):
        gt = g_ref[...]
        nm = ADAM_B1 * m_ref[...] + (1.0 - ADAM_B1) * gt
        nv = ADAM_B2 * v_ref[...] + (1.0 - ADAM_B2) * (gt * gt)
        m_hat = nm / (1.0 - ADAM_B1 ** ADAM_STEP)
        v_hat = nv / (1.0 - ADAM_B2 ** ADAM_STEP)
        d_ref[...] = -ADAM_LR * (m_hat / (jnp.sqrt(v_hat) + ADAM_EPS) + ADAM_WD * w_ref[...])
        nm_ref[...] = nm
        nv_ref[...] = nv

    spec = pl.BlockSpec((br, cols), lambda i: (i, 0))
    shape = jax.ShapeDtypeStruct((rows, cols), F32)
    return pl.pallas_call(
        body,
        grid=(rows // br,),
        in_specs=[spec] * 4,
        out_specs=[spec] * 3,
        out_shape=[shape] * 3,
        compiler_params=_params(("arbitrary",)),
        name=name,
    )(w, g, m, v)


def _lanes(v):
    flat = v.reshape(-1)
    return jnp.pad(flat, (0, (-flat.size) % 1024)).reshape(-1, 128)


def kernel(x, p, g_mix, w_in, conv_w, conv_b, ln_g, ln_b, pool_w, pool_scale, w_out, g_ffn, w_gate_up, w_down, g_ple_gate, w_ple_gate, w_ple_up, g_ple_post, g_final, loss_target, m_g_mix, m_w_in, m_conv_w, m_conv_b, m_ln_g, m_ln_b, m_pool_w, m_pool_scale, m_w_out, m_g_ffn, m_w_gate_up, m_w_down, m_g_ple_gate, m_w_ple_gate, m_w_ple_up, m_g_ple_post, m_g_final, v_g_mix, v_w_in, v_conv_w, v_conv_b, v_ln_g, v_ln_b, v_pool_w, v_pool_scale, v_w_out, v_g_ffn, v_w_gate_up, v_w_down, v_g_ple_gate, v_w_ple_gate, v_w_ple_up, v_g_ple_post, v_g_final):
    seq = x.shape[1]
    xs = x[0]
    ps = p[0, 0]
    tgt = loss_target[0]
    me = 4 * lax.axis_index("x") + 2 * lax.axis_index("y") + lax.axis_index("c")

    shards = [
        w_in[0].T.astype(BF16),
        w_out[0].astype(BF16),
        w_gate_up[0].T.astype(BF16),
        w_down[0].astype(BF16),
        w_ple_gate[0].astype(BF16),
        w_ple_up[0].T.astype(BF16),
        jnp.pad(conv_w[0].T, ((0, 0), (0, CONV_HALO - CONV_K))),
    ]
    gathered = _all_gather(shards, "gather_weights")
    w_in_t, w_out_f, w_gu_t, w_down_f, w_pg_f, w_pu_t, conv_w_t = [
        gth.reshape((-1,) + gth.shape[2:]) for gth in gathered]
    conv_w_f = conv_w_t.T
    pool_w_b = pool_w[0].astype(BF16)

    z, u1, pooled, x1 = _fwd_mix(xs, w_in_t, w_out_f, g_mix, conv_w_f, conv_b, ln_g, ln_b, pool_w_b,
                                 pool_scale, min(256, seq))
    h2, g_sav, u_sav, act, x2 = _ffn_fwd(x1, w_gu_t, w_down_f, g_ffn, min(512, seq))
    dx2, dx2b, hg, ds, dpe, pb, stats_ple = _ple_loss(x2, ps, tgt, w_pg_f, w_pu_t, g_ple_gate, g_ple_post,
                                                      g_final.reshape(1, D_MODEL), min(256, seq))
    dg, du, dx1, dx1b, stats_ffn = _ffn_bwd(dx2, dx2b, x1, g_sav, u_sav, w_gu_t, w_down_f, g_ffn, min(512, seq))
    dx, dz, mix, h1, vec_mix, dconv_w_full, dpool_w_part = _bwd_mix(
        dx1, dx1b, xs, z, u1, pooled, w_in_t, w_out_f, g_mix, conv_w_f, ln_g, ln_b, pool_w_b, pool_scale,
        min(256, seq))

    d_w_in_t = _grad_matmul(dz, h1, 256, "grad_w_in")
    d_w_out = _grad_matmul(mix, dx1b, 256, "grad_w_out")
    d_w_g_t = _grad_matmul(dg, h2, 256, "grad_w_gate")
    d_w_u_t = _grad_matmul(du, h2, 256, "grad_w_up")
    d_w_down = _grad_matmul(act, dx2b, 256, "grad_w_down")
    d_w_pg = _grad_matmul(hg, ds, 256, "grad_w_ple_gate")
    d_w_pu_t = _grad_matmul(dpe, pb, 256, "grad_w_ple_up")
    d_w_gu_t = jnp.concatenate([d_w_g_t, d_w_u_t], axis=0)
    full = [d_w_in_t, d_w_out, d_w_gu_t, d_w_down, d_w_pg, d_w_pu_t]
    summed = _reduce_scatter([f.reshape(N_DEV, f.shape[0] // N_DEV, f.shape[1]) for f in full], "scatter_grads")
    gr_w_in = summed[0].T
    gr_w_out = summed[1]
    gr_w_gu = summed[2].T
    gr_w_down = summed[3]
    gr_w_pg = summed[4]
    gr_w_pu = summed[5].T

    small = jnp.concatenate([
        vec_mix,
        stats_ple,
        stats_ffn,
        dconv_w_full.reshape(CONV_HALO // 2, D_MODEL),
        dpool_w_part.reshape(-1, D_MODEL),
    ], axis=0)
    (small_all,) = _all_gather([small], "gather_small")
    tot = _sum_devices(small_all, "sum_small")
    gr_g_mix = tot[0:1]
    gr_ln_g = tot[1:2, :C_CONV]
    gr_ln_b = tot[2:3, :C_CONV]
    gr_conv_b = tot[3:4, :C_CONV]
    gr_pool_scale = tot[4:5, :C_POOL]
    gr_g_final = tot[8]
    gr_g_post = tot[9:10]
    gr_g_gate = tot[10:11]
    loss = tot[11, 0]
    gr_g_ffn = tot[16:17]
    gr_conv_w_all = tot[24:40].reshape(CONV_HALO, C_CONV)[:CONV_K]
    gr_conv_w = lax.dynamic_slice_in_dim(gr_conv_w_all, me * (C_CONV // N_DEV), C_CONV // N_DEV, axis=1)[None]
    gr_pool_w = tot[40:104].reshape(1, len(POOL_WINDOWS), POOL_GROUP, POOL_GROUP)

    big = [
        (w_in[0], gr_w_in, m_w_in[0], v_w_in[0]),
        (w_out[0], gr_w_out, m_w_out[0], v_w_out[0]),
        (w_gate_up[0], gr_w_gu, m_w_gate_up[0], v_w_gate_up[0]),
        (w_down[0], gr_w_down, m_w_down[0], v_w_down[0]),
        (w_ple_gate[0], gr_w_pg, m_w_ple_gate[0], v_w_ple_gate[0]),
        (w_ple_up[0], gr_w_pu, m_w_ple_up[0], v_w_ple_up[0]),
        (conv_w[0], gr_conv_w[0], m_conv_w[0], v_conv_w[0]),
    ]
    names = ["w_in", "w_out", "w_gate_up", "w_down", "w_ple_gate", "w_ple_up", "conv_w"]
    upd = {}
    for nm, (w_, g_, m_, v_) in zip(names, big):
        d_, nm_, nv_ = _adamw(w_, g_, m_, v_, "adamw_" + nm)
        upd[nm] = (g_[None], d_[None], nm_[None], nv_[None])

    vec_names = ["g_mix", "conv_b", "ln_g", "ln_b", "pool_scale", "g_ffn", "g_ple_gate", "g_ple_post", "g_final",
                 "pool_w"]
    vec_w = [g_mix, conv_b, ln_g, ln_b, pool_scale, g_ffn, g_ple_gate, g_ple_post, g_final, pool_w]
    vec_g = [gr_g_mix, gr_conv_b, gr_ln_g, gr_ln_b, gr_pool_scale, gr_g_ffn, gr_g_gate, gr_g_post, gr_g_final,
             gr_pool_w]
    vec_m = [m_g_mix, m_conv_b, m_ln_g, m_ln_b, m_pool_scale, m_g_ffn, m_g_ple_gate, m_g_ple_post, m_g_final,
             m_pool_w]
    vec_v = [v_g_mix, v_conv_b, v_ln_g, v_ln_b, v_pool_scale, v_g_ffn, v_g_ple_gate, v_g_ple_post, v_g_final,
             v_pool_w]
    pack = lambda arrs: jnp.concatenate([_lanes(a_) for a_ in arrs], axis=0)
    d_pk, nm_pk, nv_pk = _adamw(pack(vec_w), pack(vec_g), pack(vec_m), pack(vec_v), "adamw_small")
    row = 0
    for nm, w_, g_ in zip(vec_names, vec_w, vec_g):
        nrow = -(-w_.size // 1024) * 8
        cut = lambda a_: a_[row:row + nrow].reshape(-1)[:w_.size].reshape(w_.shape)
        upd[nm] = (g_.reshape(w_.shape), cut(d_pk), cut(nm_pk), cut(nv_pk))
        row += nrow

    order = ["g_mix", "w_in", "conv_w", "conv_b", "ln_g", "ln_b", "pool_w", "pool_scale", "w_out", "g_ffn",
             "w_gate_up", "w_down", "g_ple_gate", "w_ple_gate", "w_ple_up", "g_ple_post", "g_final"]
    outs = [loss, dx[None]]
    for k in range(4):
        outs += [upd[nm][k] for nm in order]
    return tuple(outs)
```

```python
import functools

import jax
import jax.numpy as jnp
from jax import lax
from jax.experimental import pallas as pl
from jax.experimental.pallas import tpu as pltpu

D_MODEL = 1024
C_CONV = 512
C_POOL = 512
Z_WIDTH = 2 * C_CONV + C_POOL
POOL_WINDOWS = (2, 4, 8, 16)
POOL_GROUP = 128
CONV_K = 31
D_FF = 2816
D_PLE = 256
EPS = 1e-6
N_DEV = 8

ADAM_LR = 0.001
ADAM_B1 = 0.9
ADAM_B2 = 0.999
ADAM_EPS = 1e-08
ADAM_WD = 0.01
ADAM_STEP = 10

CONV_HALO = 32
POOL_HALO = 16
ROW_CHUNK = 32
V7X_VMEM_LIMIT = 56 * 1024 * 1024
FF_CHUNK = D_FF // 2

BF16 = jnp.bfloat16
F32 = jnp.float32
MESH = pl.DeviceIdType.MESH


def _dot(a, b):
    return lax.dot_general(a, b, (((1,), (0,)), ((), ())), preferred_element_type=F32)


def _dot_nt(a, b):
    return lax.dot_general(a, b, (((1,), (1,)), ((), ())), preferred_element_type=F32)


def _dot_tn(a, b):
    return lax.dot_general(a, b, (((0,), (0,)), ((), ())), preferred_element_type=F32)


def _rms_fwd(x, g):
    r = lax.rsqrt(jnp.mean(x * x, axis=-1, keepdims=True) + EPS)
    return x * r * g, r


def _rms_bwd(x, r, g, dy):
    xr = x * r
    dg = jnp.sum(dy * xr, axis=0, keepdims=True)
    dyg = dy * g
    dx = r * (dyg - xr * jnp.mean(dyg * xr, axis=-1, keepdims=True))
    return dx, dg


def _sigmoid(x):
    return jax.nn.sigmoid(x)


def _params(sem=None):
    return pltpu.CompilerParams(dimension_semantics=sem, vmem_limit_bytes=V7X_VMEM_LIMIT)


def _place():
    x, y, c = lax.axis_index("x"), lax.axis_index("y"), lax.axis_index("c")
    chips = [(1 - x, y), (x, 1 - y), (1 - x, 1 - y)]
    return (x, y, c), (x, y, 1 - c), chips


def _block(px, py, pc):
    return 4 * px + 2 * py + pc


class _Carry:
    def __init__(self):
        self.inputs = []
        self.out_shapes = []
        self.copies = []
        self.locals = []

    def add_input(self, arr):
        self.inputs.append(arr)
        return len(self.inputs) - 1

    def add_output(self, shape, dtype):
        self.out_shapes.append(jax.ShapeDtypeStruct(shape, dtype))
        return len(self.out_shapes) - 1

    def local(self, src_idx, dst_idx, dst_blk):
        self.locals.append((src_idx, dst_idx, dst_blk))

    def copy(self, src, dst_idx, dst_blk, got_blk, peer, step=0, after=()):
        self.copies.append(dict(src=src, dst_idx=dst_idx, dst_blk=dst_blk, got_blk=got_blk, peer=peer, step=step,
                                after=tuple(after)))
        return len(self.copies) - 1

    def sem_shapes(self):
        return [pltpu.SemaphoreType.DMA((max(1, len(self.copies)),)),
                pltpu.SemaphoreType.DMA((max(1, len(self.copies)),)),
                pltpu.SemaphoreType.DMA((max(1, len(self.locals)),))]

    def _desc(self, k, ins, outs, sems, place):
        cp = self.copies[k]
        me, sib, chips = place
        kind, idx, blk = cp["src"]
        src = (ins if kind == "in" else outs)[idx]
        if blk is not None:
            src = src.at[blk(*place)]
        to = sib if cp["peer"] == "sib" else (*chips[cp["peer"]], me[2])
        return pltpu.make_async_remote_copy(
            src_ref=src, dst_ref=outs[cp["dst_idx"]].at[cp["dst_blk"](*place)],
            send_sem=sems[0].at[k], recv_sem=sems[1].at[k], device_id=to, device_id_type=MESH)

    def _arrival(self, k, outs, sems, place):
        cp = self.copies[k]
        got = outs[cp["dst_idx"]].at[cp["got_blk"](*place)]
        return pltpu.make_async_remote_copy(
            src_ref=got, dst_ref=got, send_sem=sems[0].at[k], recv_sem=sems[1].at[k],
            device_id=place[0], device_id_type=MESH)

    def _local(self, n, ins, outs, sems, place):
        src_idx, dst_idx, blk = self.locals[n]
        return pltpu.make_async_copy(ins[src_idx], outs[dst_idx].at[blk(*place)], sems[2].at[n])

    def starts(self, step, nsteps, ins, outs, sems):
        place = _place()
        self._waited = set()
        for s in sorted({0} | {cp["step"] for cp in self.copies}):
            ks = [k for k, cp in enumerate(self.copies) if cp["step"] == s]

            @pl.when(step == min(s, nsteps - 1))
            def _(s=s, ks=ks):
                if s == 0:
                    for n in range(len(self.locals)):
                        self._local(n, ins, outs, sems, place).start()
                for k in ks:
                    for a in self.copies[k]["after"]:
                        self._arrival(a, outs, sems, place).wait_recv()
                        self._waited.add(a)
                    self._desc(k, ins, outs, sems, place).start()

    def finish(self, step, nsteps, ins, outs, sems):
        place = _place()

        @pl.when(step == nsteps - 1)
        def _():
            for k in range(len(self.copies)):
                if k not in self._waited:
                    self._arrival(k, outs, sems, place).wait_recv()
            for k in range(len(self.copies)):
                self._desc(k, ins, outs, sems, place).wait_send()
            for n in range(len(self.locals)):
                self._local(n, ins, outs, sems, place).wait()


def _const_blk(j):
    return lambda me, sib, chips: j


def _carry_gather(carry, shards, fwd_step):
    outs = []
    for sh in shards:
        i = carry.add_input(sh)
        o = carry.add_output((N_DEV,) + sh.shape, sh.dtype)
        mine = lambda me, sib, chips: _block(*me)
        carry.local(i, o, mine)
        carry.copy(("in", i, None), o, mine, lambda me, sib, chips: _block(*sib), "sib")
        first = []
        for j in range(3):
            theirs = lambda me, sib, chips, j=j: _block(*chips[j], me[2])
            first.append(carry.copy(("in", i, None), o, mine, theirs, j))
        for j in range(3):
            theirs = lambda me, sib, chips, j=j: _block(*chips[j], me[2])
            cousin = lambda me, sib, chips, j=j: _block(*chips[j], 1 - me[2])
            carry.copy(("out", o, theirs), o, theirs, cousin, "sib", step=fwd_step, after=(first[j],))
        outs.append(o)
    return outs


def _carry_pair(carry, grads):
    outs = []
    for g in grads:
        i = carry.add_input(g)
        o = carry.add_output((4,) + g.shape[1:], g.dtype)
        for j in range(4):
            if j < 3:
                blk = lambda me, sib, chips, j=j: _block(*chips[j], 1 - me[2])
            else:
                blk = lambda me, sib, chips: _block(*sib)
            carry.copy(("in", i, blk), o, _const_blk(j), _const_blk(j), "sib")
        outs.append(o)
    return outs


def _carry_chip(carry, parts):
    outs = []
    for p in parts:
        i = carry.add_input(p)
        o = carry.add_output((3,) + p.shape[1:], p.dtype)
        for j in range(3):
            carry.copy(("in", i, _const_blk(j)), o, _const_blk(j), _const_blk(j), j)
        outs.append(o)
    return outs


def _pcall(body, *, grid, in_specs, out_specs, out_shape, scratch_shapes, name, args, carry=None):
    sem = ("arbitrary",) * len(grid)
    if carry is None:
        res = pl.pallas_call(body, grid=grid, in_specs=in_specs, out_specs=out_specs, out_shape=out_shape,
                             scratch_shapes=scratch_shapes, compiler_params=_params(sem), name=name)(*args)
        return list(res), []
    n_in, n_out, n_scr = len(in_specs), len(out_specs), len(scratch_shapes)
    c_in, c_out = len(carry.inputs), len(carry.out_shapes)
    nsteps = 1
    for extent in grid:
        nsteps *= extent

    def wrapped(*refs):
        ins = refs[:n_in]
        cins = refs[n_in:n_in + c_in]
        o0 = n_in + c_in
        outs = refs[o0:o0 + n_out]
        couts = refs[o0 + n_out:o0 + n_out + c_out]
        s0 = o0 + n_out + c_out
        scr = refs[s0:s0 + n_scr]
        sems = refs[s0 + n_scr:]
        step = pl.program_id(0)
        for d in range(1, len(grid)):
            step = step * grid[d] + pl.program_id(d)
        carry.starts(step, nsteps, cins, couts, sems)
        body(*ins, *outs, *scr)
        carry.finish(step, nsteps, cins, couts, sems)

    any_spec = pl.BlockSpec(memory_space=pl.ANY)
    res = pl.pallas_call(
        wrapped, grid=grid,
        in_specs=list(in_specs) + [any_spec] * c_in,
        out_specs=list(out_specs) + [any_spec] * c_out,
        out_shape=list(out_shape) + carry.out_shapes,
        scratch_shapes=list(scratch_shapes) + carry.sem_shapes(),
        compiler_params=_params(sem), name=name)(*args, *carry.inputs)
    return list(res[:n_out]), list(res[n_out:])


def _row_spec(tm, width):
    return pl.BlockSpec((tm, width), lambda i: (i, 0))


def _const_spec(shape):
    return pl.BlockSpec(shape, lambda i: (0,) * len(shape))


def _fwd_mix(x, w_in_t, w_out, g_mix, conv_w, conv_b, ln_g, ln_b, pool_w, pool_scale, tm, carry=None):
    s = x.shape[0]
    nt = s // tm

    def body(x_ref, win_ref, wout_ref, gmix_ref, cw_ref, cb_ref, lng_ref, lnb_ref, pw_ref, ps_ref,
             z_ref, u1_ref, pooled_ref, x1_ref, ubuf, vbuf):
        i = pl.program_id(0)

        @pl.when(i == 0)
        def _():
            ubuf[0:CONV_HALO, :] = jnp.zeros((CONV_HALO, C_CONV), F32)
            vbuf[0:POOL_HALO, :] = jnp.zeros((POOL_HALO, C_POOL), F32)

        xt = x_ref[...]
        h, _ = _rms_fwd(xt, gmix_ref[...])
        z = _dot_nt(h.astype(BF16), win_ref[...])
        z_ref[...] = z
        a = z[:, :C_CONV]
        b = z[:, C_CONV:2 * C_CONV]
        v = z[:, 2 * C_CONV:]
        ubuf[CONV_HALO:CONV_HALO + tm, :] = a * _sigmoid(b)
        vbuf[POOL_HALO:POOL_HALO + tm, :] = v

        for rc in range(tm // ROW_CHUNK):
            base = rc * ROW_CHUNK + CONV_HALO - (CONV_K - 1)
            acc = jnp.broadcast_to(cb_ref[...], (ROW_CHUNK, C_CONV))
            for k in range(CONV_K):
                acc = acc + cw_ref[k:k + 1, :] * ubuf[base + k:base + k + ROW_CHUNK, :]
            u1_ref[rc * ROW_CHUNK:(rc + 1) * ROW_CHUNK, :] = acc

        u1 = u1_ref[...]
        mu = jnp.mean(u1, axis=-1, keepdims=True)
        cen = u1 - mu
        rstd = lax.rsqrt(jnp.mean(cen * cen, axis=-1, keepdims=True) + EPS)
        u2 = cen * rstd * lng_ref[...] + lnb_ref[...]
        u = u2 * _sigmoid(u2)

        pos1 = (i * tm + lax.broadcasted_iota(jnp.int32, (tm, 1), 0) + 1).astype(F32)
        parts = [u]
        for g, w in enumerate(POOL_WINDOWS):
            cols = slice(g * POOL_GROUP, (g + 1) * POOL_GROUP)
            vg = v[:, cols]
            tot = vg
            for j in range(1, w):
                tot = tot + vbuf[POOL_HALO - j:POOL_HALO - j + tm, cols]
            pooled = tot / jnp.minimum(pos1, float(w)) - vg
            pooled_b = pooled.astype(BF16)
            pooled_ref[:, cols] = pooled_b
            parts.append(_dot(pooled_b, pw_ref[g]) * ps_ref[:, cols])
        mix = jnp.concatenate(parts, axis=-1).astype(BF16)
        x1_ref[...] = xt + _dot(mix, wout_ref[...])

        ubuf[0:CONV_HALO, :] = ubuf[tm:tm + CONV_HALO, :]
        vbuf[0:POOL_HALO, :] = vbuf[tm:tm + POOL_HALO, :]

    return _pcall(
        body,
        grid=(nt,),
        in_specs=[
            _row_spec(tm, D_MODEL),
            _const_spec((Z_WIDTH, D_MODEL)),
            _const_spec((D_MODEL, D_MODEL)),
            _const_spec((1, D_MODEL)),
            _const_spec((CONV_HALO, C_CONV)),
            _const_spec((1, C_CONV)),
            _const_spec((1, C_CONV)),
            _const_spec((1, C_CONV)),
            _const_spec((len(POOL_WINDOWS), POOL_GROUP, POOL_GROUP)),
            _const_spec((1, C_POOL)),
        ],
        out_specs=[
            _row_spec(tm, Z_WIDTH),
            _row_spec(tm, C_CONV),
            _row_spec(tm, C_POOL),
            _row_spec(tm, D_MODEL),
        ],
        out_shape=[
            jax.ShapeDtypeStruct((s, Z_WIDTH), F32),
            jax.ShapeDtypeStruct((s, C_CONV), F32),
            jax.ShapeDtypeStruct((s, C_POOL), BF16),
            jax.ShapeDtypeStruct((s, D_MODEL), F32),
        ],
        scratch_shapes=[
            pltpu.VMEM((tm + CONV_HALO, C_CONV), F32),
            pltpu.VMEM((tm + POOL_HALO, C_POOL), F32),
        ],
        name="fwd_mix",
        args=(x, w_in_t, w_out, g_mix, conv_w, conv_b, ln_g, ln_b, pool_w, pool_scale),
        carry=carry,
    )


def _ffn_up(x1, w_gu_t, g_ffn, tm, carry=None):
    s = x1.shape[0]
    nt = s // tm
    nc = D_FF // FF_CHUNK

    def body(x1_ref, wg_ref, wu_ref, gffn_ref, h2_ref, g_ref, u_ref, act_ref, h2_buf):
        c = pl.program_id(1)

        @pl.when(c == 0)
        def _():
            h, _ = _rms_fwd(x1_ref[...], gffn_ref[...])
            h2_buf[...] = h.astype(BF16)
            h2_ref[...] = h2_buf[...]

        h2 = h2_buf[...]
        g = _dot_nt(h2, wg_ref[...])
        u = _dot_nt(h2, wu_ref[...])
        g_ref[...] = g.astype(BF16)
        u_ref[...] = u.astype(BF16)
        act_ref[...] = (g * _sigmoid(g) * u).astype(BF16)

    tok = lambda width: pl.BlockSpec((tm, width), lambda i, c: (i, 0))
    chunk = pl.BlockSpec((tm, FF_CHUNK), lambda i, c: (i, c))
    return _pcall(
        body,
        grid=(nt, nc),
        in_specs=[
            tok(D_MODEL),
            pl.BlockSpec((FF_CHUNK, D_MODEL), lambda i, c: (c, 0)),
            pl.BlockSpec((FF_CHUNK, D_MODEL), lambda i, c: (c + nc, 0)),
            pl.BlockSpec((1, D_MODEL), lambda i, c: (0, 0)),
        ],
        out_specs=[tok(D_MODEL), chunk, chunk, chunk],
        out_shape=[
            jax.ShapeDtypeStruct((s, D_MODEL), BF16),
            jax.ShapeDtypeStruct((s, D_FF), BF16),
            jax.ShapeDtypeStruct((s, D_FF), BF16),
            jax.ShapeDtypeStruct((s, D_FF), BF16),
        ],
        scratch_shapes=[pltpu.VMEM((tm, D_MODEL), BF16)],
        name="ffn_up",
        args=(x1, w_gu_t, w_gu_t, g_ffn),
        carry=carry,
    )


def _ffn_down(x1, act, w_down, tm):
    s = x1.shape[0]

    def body(x1_ref, act_ref, wd_ref, x2_ref):
        x2_ref[...] = x1_ref[...] + _dot(act_ref[...], wd_ref[...])

    return pl.pallas_call(
        body,
        grid=(s // tm,),
        in_specs=[_row_spec(tm, D_MODEL), _row_spec(tm, D_FF), _const_spec((D_FF, D_MODEL))],
        out_specs=_row_spec(tm, D_MODEL),
        out_shape=jax.ShapeDtypeStruct((s, D_MODEL), F32),
        compiler_params=_params(("arbitrary",)),
        name="ffn_down",
    )(x1, act, w_down)


def _ple_loss(x2, p, tgt, w_pg, w_pu_t, g_gate, g_post, g_final, tm):
    s = x2.shape[0]
    nt = s // tm

    def body(x2_ref, p_ref, t_ref, wpg_ref, wpu_ref, gg_ref, gp_ref, gf_ref,
             dx2_ref, dx2b_ref, hg_ref, ds_ref, dpe_ref, pb_ref, stats_ref):
        i = pl.program_id(0)

        @pl.when(i == 0)
        def _():
            stats_ref[...] = jnp.zeros_like(stats_ref)

        x2 = x2_ref[...]
        hg, rg = _rms_fwd(x2, gg_ref[...])
        hg_b = hg.astype(BF16)
        hg_ref[...] = hg_b
        gate = _sigmoid(_dot(hg_b, wpg_ref[...]))
        pb = p_ref[...].astype(BF16)
        pb_ref[...] = pb
        pe = _dot_nt(pb, wpu_ref[...])
        e, rp = _rms_fwd(pe, gp_ref[...])
        x3 = x2 + gate * e
        y, r3 = _rms_fwd(x3, gf_ref[...])
        diff = y - t_ref[...]
        loss = 0.5 * jnp.sum(jnp.sum(diff * diff, axis=-1, keepdims=True), axis=0, keepdims=True) / D_MODEL
        dy = diff * (1.0 / D_MODEL)

        dx3, dgf = _rms_bwd(x3, r3, gf_ref[...], dy)
        dpe, dgp = _rms_bwd(pe, rp, gp_ref[...], dx3 * gate)
        dpe_ref[...] = dpe.astype(BF16)
        ds = (dx3 * e * gate * (1.0 - gate)).astype(BF16)
        ds_ref[...] = ds
        dhg = _dot_nt(ds, wpg_ref[...])
        dxg, dgg = _rms_bwd(x2, rg, gg_ref[...], dhg)
        dx2 = dx3 + dxg
        dx2_ref[...] = dx2
        dx2b_ref[...] = dx2.astype(BF16)

        stats_ref[0:1, :] += dgf
        stats_ref[1:2, :] += dgp
        stats_ref[2:3, :] += dgg
        stats_ref[3:4, :] += jnp.broadcast_to(loss, (1, D_MODEL))

    return pl.pallas_call(
        body,
        grid=(nt,),
        in_specs=[
            _row_spec(tm, D_MODEL),
            _row_spec(tm, D_PLE),
            _row_spec(tm, D_MODEL),
            _const_spec((D_MODEL, D_MODEL)),
            _const_spec((D_MODEL, D_PLE)),
            _const_spec((1, D_MODEL)),
            _const_spec((1, D_MODEL)),
            _const_spec((1, D_MODEL)),
        ],
        out_specs=[
            _row_spec(tm, D_MODEL),
            _row_spec(tm, D_MODEL),
            _row_spec(tm, D_MODEL),
            _row_spec(tm, D_MODEL),
            _row_spec(tm, D_MODEL),
            _row_spec(tm, D_PLE),
            _const_spec((8, D_MODEL)),
        ],
        out_shape=[
            jax.ShapeDtypeStruct((s, D_MODEL), F32),
            jax.ShapeDtypeStruct((s, D_MODEL), BF16),
            jax.ShapeDtypeStruct((s, D_MODEL), BF16),
            jax.ShapeDtypeStruct((s, D_MODEL), BF16),
            jax.ShapeDtypeStruct((s, D_MODEL), BF16),
            jax.ShapeDtypeStruct((s, D_PLE), BF16),
            jax.ShapeDtypeStruct((8, D_MODEL), F32),
        ],
        compiler_params=_params(("arbitrary",)),
        name="ple_loss",
    )(x2, p, tgt, w_pg, w_pu_t, g_gate, g_post, g_final)


def _ffn_bwd(dx2, dx2b, x1, g_sav, u_sav, w_gu_t, w_down, g_ffn, tm):
    s = x1.shape[0]
    nt = s // tm
    nc = D_FF // FF_CHUNK

    def body(dx2_ref, dx2b_ref, x1_ref, g_ref, u_ref, wg_ref, wu_ref, wd_ref, gffn_ref,
             dg_ref, du_ref, dx1_ref, dx1b_ref, stats_ref, acc):
        i = pl.program_id(0)
        c = pl.program_id(1)

        @pl.when(jnp.logical_and(i == 0, c == 0))
        def _():
            stats_ref[...] = jnp.zeros_like(stats_ref)

        @pl.when(c == 0)
        def _():
            acc[...] = jnp.zeros_like(acc)

        dact = _dot_nt(dx2b_ref[...], wd_ref[...])
        g = g_ref[...].astype(F32)
        u = u_ref[...].astype(F32)
        sg = _sigmoid(g)
        dg = (dact * u * sg * (1.0 + g * (1.0 - sg))).astype(BF16)
        du = (dact * g * sg).astype(BF16)
        dg_ref[...] = dg
        du_ref[...] = du
        acc[...] += _dot(dg, wg_ref[...]) + _dot(du, wu_ref[...])

        @pl.when(c == nc - 1)
        def _():
            x1 = x1_ref[...]
            r2 = lax.rsqrt(jnp.mean(x1 * x1, axis=-1, keepdims=True) + EPS)
            dxn, dgf = _rms_bwd(x1, r2, gffn_ref[...], acc[...])
            dx1 = dx2_ref[...] + dxn
            dx1_ref[...] = dx1
            dx1b_ref[...] = dx1.astype(BF16)
            stats_ref[0:1, :] += dgf

    tok = lambda width: pl.BlockSpec((tm, width), lambda i, c: (i, 0))
    chunk = pl.BlockSpec((tm, FF_CHUNK), lambda i, c: (i, c))
    return pl.pallas_call(
        body,
        grid=(nt, nc),
        in_specs=[
            tok(D_MODEL), tok(D_MODEL), tok(D_MODEL), chunk, chunk,
            pl.BlockSpec((FF_CHUNK, D_MODEL), lambda i, c: (c, 0)),
            pl.BlockSpec((FF_CHUNK, D_MODEL), lambda i, c: (c + nc, 0)),
            pl.BlockSpec((FF_CHUNK, D_MODEL), lambda i, c: (c, 0)),
            pl.BlockSpec((1, D_MODEL), lambda i, c: (0, 0)),
        ],
        out_specs=[chunk, chunk, tok(D_MODEL), tok(D_MODEL), pl.BlockSpec((8, D_MODEL), lambda i, c: (0, 0))],
        out_shape=[
            jax.ShapeDtypeStruct((s, D_FF), BF16),
            jax.ShapeDtypeStruct((s, D_FF), BF16),
            jax.ShapeDtypeStruct((s, D_MODEL), F32),
            jax.ShapeDtypeStruct((s, D_MODEL), BF16),
            jax.ShapeDtypeStruct((8, D_MODEL), F32),
        ],
        scratch_shapes=[pltpu.VMEM((tm, D_MODEL), F32)],
        compiler_params=_params(("arbitrary", "arbitrary")),
        name="ffn_bwd",
    )(dx2, dx2b, x1, g_sav, u_sav, w_gu_t, w_gu_t, w_down, g_ffn)


def _bwd_mix(dx1, dx1b, x, z, u1, pooled, w_in_t, w_out, g_mix, conv_w, ln_g, ln_b, pool_w, pool_scale, tm,
             carry=None):
    s = x.shape[0]
    nt = s // tm

    def body(dx1_ref, dx1b_ref, x_ref, z_ref, u1_ref, pooled_ref, win_ref, wout_ref, gmix_ref, cw_ref,
             lng_ref, lnb_ref, pw_ref, ps_ref,
             dx_ref, dz_ref, mix_ref, h1_ref, vec_ref, dcw_ref, dpw_ref, dubuf, dvbuf, u0buf, du0buf):
        i = pl.program_id(0)
        tile = nt - 1 - i

        @pl.when(i == 0)
        def _():
            vec_ref[...] = jnp.zeros_like(vec_ref)
            dcw_ref[...] = jnp.zeros_like(dcw_ref)
            dpw_ref[...] = jnp.zeros_like(dpw_ref)
            dubuf[tm:tm + CONV_HALO, :] = jnp.zeros((CONV_HALO, C_CONV), F32)
            dvbuf[tm:tm + POOL_HALO, :] = jnp.zeros((POOL_HALO, C_POOL), F32)

        dmix = _dot_nt(dx1b_ref[...], wout_ref[...])
        du = dmix[:, :C_CONV]
        dq = dmix[:, C_CONV:]

        pos1 = (tile * tm + lax.broadcasted_iota(jnp.int32, (tm, 1), 0) + 1).astype(F32)
        q_parts = []
        dpooled_parts = []
        dps_rows = []
        for g, w in enumerate(POOL_WINDOWS):
            cols = slice(g * POOL_GROUP, (g + 1) * POOL_GROUP)
            pooled_b = pooled_ref[:, cols]
            mixed = _dot(pooled_b, pw_ref[g])
            dqg = dq[:, cols]
            dps_rows.append(jnp.sum(dqg * mixed, axis=0, keepdims=True))
            q_parts.append(mixed * ps_ref[:, cols])
            dmixed = (dqg * ps_ref[:, cols]).astype(BF16)
            dpw_ref[g] += _dot_tn(pooled_b, dmixed)
            dpooled = _dot_nt(dmixed, pw_ref[g])
            dpooled_parts.append(dpooled)
            dvbuf[0:tm, cols] = dpooled / jnp.minimum(pos1, float(w))
        vec_ref[4:5, 0:C_POOL] += jnp.concatenate(dps_rows, axis=-1)
        dv_parts = []
        for g, w in enumerate(POOL_WINDOWS):
            cols = slice(g * POOL_GROUP, (g + 1) * POOL_GROUP)
            tot = dvbuf[0:tm, cols]
            for j in range(1, w):
                tot = tot + dvbuf[j:j + tm, cols]
            dv_parts.append(tot - dpooled_parts[g])

        u1 = u1_ref[...]
        mu = jnp.mean(u1, axis=-1, keepdims=True)
        cen = u1 - mu
        rstd = lax.rsqrt(jnp.mean(cen * cen, axis=-1, keepdims=True) + EPS)
        xhat = cen * rstd
        u2 = xhat * lng_ref[...] + lnb_ref[...]
        sg2 = _sigmoid(u2)
        du2 = du * sg2 * (1.0 + u2 * (1.0 - sg2))
        vec_ref[1:2, 0:C_CONV] += jnp.sum(du2 * xhat, axis=0, keepdims=True)
        vec_ref[2:3, 0:C_CONV] += jnp.sum(du2, axis=0, keepdims=True)
        t1 = du2 * lng_ref[...]
        du1 = rstd * (t1 - jnp.mean(t1, axis=-1, keepdims=True)
                      - xhat * jnp.mean(t1 * xhat, axis=-1, keepdims=True))
        vec_ref[3:4, 0:C_CONV] += jnp.sum(du1, axis=0, keepdims=True)
        dubuf[0:tm, :] = du1

        zt = z_ref[...]
        a = zt[:, :C_CONV]
        sgb = _sigmoid(zt[:, C_CONV:2 * C_CONV])
        u0buf[...] = a * sgb

        for rc in range(tm // ROW_CHUNK):
            r0 = rc * ROW_CHUNK
            acc = jnp.zeros((ROW_CHUNK, C_CONV), F32)
            for k in range(CONV_K):
                off = r0 + (CONV_K - 1) - k
                acc = acc + cw_ref[k:k + 1, :] * dubuf[off:off + ROW_CHUNK, :]
            du0buf[r0:r0 + ROW_CHUNK, :] = acc
        for k in range(CONV_K):
            acc = jnp.zeros((ROW_CHUNK, C_CONV), F32)
            for rc in range(tm // ROW_CHUNK):
                r0 = rc * ROW_CHUNK
                off = r0 + (CONV_K - 1) - k
                acc = acc + u0buf[r0:r0 + ROW_CHUNK, :] * dubuf[off:off + ROW_CHUNK, :]
            dcw_ref[k:k + 1, :] += jnp.sum(acc, axis=0, keepdims=True)
        du0 = du0buf[...]

        da = du0 * sgb
        db = du0 * a * sgb * (1.0 - sgb)
        dz = jnp.concatenate([da, db] + dv_parts, axis=-1).astype(BF16)
        dz_ref[...] = dz

        mix_ref[...] = jnp.concatenate([u2 * sg2] + q_parts, axis=-1).astype(BF16)

        xt = x_ref[...]
        h1, r1 = _rms_fwd(xt, gmix_ref[...])
        h1_ref[...] = h1.astype(BF16)
        dh1 = _dot(dz, win_ref[...])
        dxn, dgm = _rms_bwd(xt, r1, gmix_ref[...], dh1)
        dx_ref[...] = dx1_ref[...] + dxn
        vec_ref[0:1, :] += dgm

        dubuf[tm:tm + CONV_HALO, :] = dubuf[0:CONV_HALO, :]
        dvbuf[tm:tm + POOL_HALO, :] = dvbuf[0:POOL_HALO, :]

    rev = lambda width: pl.BlockSpec((tm, width), lambda i: (nt - 1 - i, 0))
    return _pcall(
        body,
        grid=(nt,),
        in_specs=[
            rev(D_MODEL), rev(D_MODEL), rev(D_MODEL), rev(Z_WIDTH), rev(C_CONV), rev(C_POOL),
            _const_spec((Z_WIDTH, D_MODEL)),
            _const_spec((D_MODEL, D_MODEL)),
            _const_spec((1, D_MODEL)),
            _const_spec((CONV_HALO, C_CONV)),
            _const_spec((1, C_CONV)),
            _const_spec((1, C_CONV)),
            _const_spec((len(POOL_WINDOWS), POOL_GROUP, POOL_GROUP)),
            _const_spec((1, C_POOL)),
        ],
        out_specs=[
            rev(D_MODEL), rev(Z_WIDTH), rev(D_MODEL), rev(D_MODEL),
            _const_spec((8, D_MODEL)),
            _const_spec((CONV_HALO, C_CONV)),
            _const_spec((len(POOL_WINDOWS), POOL_GROUP, POOL_GROUP)),
        ],
        out_shape=[
            jax.ShapeDtypeStruct((s, D_MODEL), F32),
            jax.ShapeDtypeStruct((s, Z_WIDTH), BF16),
            jax.ShapeDtypeStruct((s, D_MODEL), BF16),
            jax.ShapeDtypeStruct((s, D_MODEL), BF16),
            jax.ShapeDtypeStruct((8, D_MODEL), F32),
            jax.ShapeDtypeStruct((CONV_HALO, C_CONV), F32),
            jax.ShapeDtypeStruct((len(POOL_WINDOWS), POOL_GROUP, POOL_GROUP), F32),
        ],
        scratch_shapes=[
            pltpu.VMEM((tm + CONV_HALO, C_CONV), F32),
            pltpu.VMEM((tm + POOL_HALO, C_POOL), F32),
            pltpu.VMEM((tm, C_CONV), F32),
            pltpu.VMEM((tm, C_CONV), F32),
        ],
        name="bwd_mix",
        args=(dx1, dx1b, x, z, u1, pooled, w_in_t, w_out, g_mix, conv_w, ln_g, ln_b, pool_w, pool_scale),
        carry=carry,
    )


def _grad_matmul(a, b, bm, name, a2=None, carry=None):
    s, ma = a.shape
    nb = b.shape[1]
    na = ma // bm
    if a2 is None:
        def body(a_ref, b_ref, o_ref):
            o_ref[...] = _dot_tn(a_ref[...], b_ref[...]).astype(BF16)

        lhs_specs = [pl.BlockSpec((s, bm), lambda i: (0, i))]
        lhs = (a,)
        steps = na
    else:
        def body(a_ref, a2_ref, b_ref, o_ref):
            i = pl.program_id(0)

            @pl.when(i < na)
            def _():
                o_ref[...] = _dot_tn(a_ref[...], b_ref[...]).astype(BF16)

            @pl.when(i >= na)
            def _():
                o_ref[...] = _dot_tn(a2_ref[...], b_ref[...]).astype(BF16)

        lhs_specs = [pl.BlockSpec((s, bm), lambda i: (0, jnp.minimum(i, na - 1))),
                     pl.BlockSpec((s, bm), lambda i: (0, jnp.maximum(i - na, 0)))]
        lhs = (a, a2)
        steps = 2 * na

    outs, carried = _pcall(
        body,
        grid=(steps,),
        in_specs=lhs_specs + [pl.BlockSpec((s, nb), lambda i: (0, 0))],
        out_specs=[pl.BlockSpec((bm, nb), lambda i: (i, 0))],
        out_shape=[jax.ShapeDtypeStruct((steps * bm, nb), BF16)],
        scratch_shapes=[],
        name=name,
        args=lhs + (b,),
        carry=carry,
    )
    return outs[0], carried


def _all_gather(shards, name):
    n = len(shards)

    def body(*refs):
        src = refs[:n]
        dst = refs[n:2 * n]
        send_sems, recv_sems, local_sems = refs[2 * n:]
        me, sib, chips = _place()
        c = me[2]

        def copy(a, k, block, to, from_src=False):
            rows = dst[a].at[_block(*block)]
            return pltpu.make_async_remote_copy(
                src_ref=src[a] if from_src else rows, dst_ref=rows,
                send_sem=send_sems.at[a, k], recv_sem=recv_sems.at[a, k],
                device_id=to, device_id_type=MESH)

        started = []
        mine = []
        for a in range(n):
            cp = pltpu.make_async_copy(src[a], dst[a].at[_block(*me)], local_sems.at[a])
            cp.start()
            mine.append(cp)
            first = [copy(a, 0, me, sib, True)]
            first += [copy(a, 1 + j, me, (*chip, c), True) for j, chip in enumerate(chips)]
            for cp in first:
                cp.start()
            started += first
        for a in range(n):
            for j, chip in enumerate(chips):
                copy(a, 1 + j, (*chip, c), me).wait_recv()
                fwd = copy(a, 4 + j, (*chip, c), sib)
                fwd.start()
                started.append(fwd)
        for a in range(n):
            copy(a, 0, sib, me).wait_recv()
            for j, chip in enumerate(chips):
                copy(a, 4 + j, (*chip, 1 - c), me).wait_recv()
        for cp in started:
            cp.wait_send()
        for cp in mine:
            cp.wait()

    any_spec = pl.BlockSpec(memory_space=pl.ANY)
    return pl.pallas_call(
        body,
        in_specs=[any_spec] * n,
        out_specs=[any_spec] * n,
        out_shape=[jax.ShapeDtypeStruct((N_DEV,) + sh.shape, sh.dtype) for sh in shards],
        scratch_shapes=[
            pltpu.SemaphoreType.DMA((n, 7)),
            pltpu.SemaphoreType.DMA((n, 7)),
            pltpu.SemaphoreType.DMA((n,)),
        ],
        name=name,
    )(*shards)


def _reduce_scatter(grads, name):
    n = len(grads)
    shapes = [g.shape[1:] for g in grads]

    def body(*refs):
        g = refs[:n]
        out = refs[n:2 * n]
        own = refs[2 * n:3 * n]
        loc = refs[3 * n:4 * n]
        r1 = refs[4 * n:5 * n]
        r2 = refs[5 * n:6 * n]
        load_sems, s1, q1, s2, q2 = refs[6 * n:]
        me, sib, chips = _place()
        c = me[2]

        loads = []
        sends = []
        for a in range(n):
            ld = [pltpu.make_async_copy(g[a].at[_block(*chip, c)], loc[a].at[j], load_sems.at[a, j])
                  for j, chip in enumerate(chips)]
            ld.append(pltpu.make_async_copy(g[a].at[_block(*me)], own[a], load_sems.at[a, 3]))
            for cp in ld:
                cp.start()
            loads.append(ld)
            blocks = [(*chip, 1 - c) for chip in chips] + [sib]
            for j, blk in enumerate(blocks):
                cp = pltpu.make_async_remote_copy(
                    src_ref=g[a].at[_block(*blk)], dst_ref=r1[a].at[j],
                    send_sem=s1.at[a, j], recv_sem=q1.at[a, j], device_id=sib, device_id_type=MESH)
                cp.start()
                sends.append(cp)

        def from_sibling(a, j):
            return pltpu.make_async_remote_copy(
                src_ref=r1[a].at[j], dst_ref=r1[a].at[j], send_sem=s1.at[a, j], recv_sem=q1.at[a, j],
                device_id=sib, device_id_type=MESH)

        def partial(a, j, chip):
            return pltpu.make_async_remote_copy(
                src_ref=loc[a].at[j], dst_ref=r2[a].at[j], send_sem=s2.at[a, j], recv_sem=q2.at[a, j],
                device_id=(*chip, c), device_id_type=MESH)

        for a in range(n):
            for j, chip in enumerate(chips):
                loads[a][j].wait()
                from_sibling(a, j).wait_recv()
                loc[a][j] = (loc[a][j].astype(F32) + r1[a][j].astype(F32)).astype(BF16)
                cp = partial(a, j, chip)
                cp.start()
                sends.append(cp)
        for a in range(n):
            loads[a][3].wait()
            from_sibling(a, 3).wait_recv()
            acc = own[a][...].astype(F32) + r1[a][3].astype(F32)
            for j, chip in enumerate(chips):
                partial(a, j, chip).wait_recv()
                acc = acc + r2[a][j].astype(F32)
            out[a][...] = acc
        for cp in sends:
            cp.wait_send()

    any_spec = pl.BlockSpec(memory_space=pl.ANY)
    vmem_spec = pl.BlockSpec(memory_space=pltpu.VMEM)
    return pl.pallas_call(
        body,
        in_specs=[any_spec] * n,
        out_specs=[vmem_spec] * n,
        out_shape=[jax.ShapeDtypeStruct(sh, F32) for sh in shapes],
        scratch_shapes=(
            [pltpu.VMEM(sh, BF16) for sh in shapes]
            + [pltpu.VMEM((3,) + sh, BF16) for sh in shapes]
            + [pltpu.VMEM((4,) + sh, BF16) for sh in shapes]
            + [pltpu.VMEM((3,) + sh, BF16) for sh in shapes]
            + [pltpu.SemaphoreType.DMA((n, 4)),
               pltpu.SemaphoreType.DMA((n, 4)), pltpu.SemaphoreType.DMA((n, 4)),
               pltpu.SemaphoreType.DMA((n, 3)), pltpu.SemaphoreType.DMA((n, 3))]
        ),
        compiler_params=pltpu.CompilerParams(vmem_limit_bytes=V7X_VMEM_LIMIT),
        name=name,
    )(*grads)


def _pair_add(grads, from_sib, blks, name):
    n = len(grads)

    def body(blk_ref, *refs):
        for a in range(n):
            refs[2 * n + a][...] = (refs[a][...].astype(F32) + refs[n + a][...].astype(F32)).astype(BF16)

    mine = [pl.BlockSpec((None,) + g.shape[1:], lambda j, b: (b[j], 0, 0)) for g in grads]
    same = [pl.BlockSpec((None,) + g.shape[1:], lambda j, b: (j, 0, 0)) for g in grads]
    return pl.pallas_call(
        body,
        grid_spec=pltpu.PrefetchScalarGridSpec(
            num_scalar_prefetch=1, grid=(4,), in_specs=mine + same, out_specs=same),
        out_shape=[jax.ShapeDtypeStruct((4,) + g.shape[1:], BF16) for g in grads],
        compiler_params=_params(("arbitrary",)),
        name=name,
    )(blks, *grads, *from_sib)


def _chip_sum(parts, from_chips, name):
    n = len(parts)

    def body(*refs):
        for a in range(n):
            acc = refs[a][...].astype(F32)
            for j in range(3):
                acc = acc + refs[n + a][j].astype(F32)
            refs[2 * n + a][...] = acc

    half = [p.shape[1] // 2 for p in parts]
    return pl.pallas_call(
        body,
        grid=(2,),
        in_specs=([pl.BlockSpec((None, h, p.shape[2]), lambda i: (3, i, 0)) for p, h in zip(parts, half)]
                  + [pl.BlockSpec((3, h, p.shape[2]), lambda i: (0, i, 0)) for p, h in zip(parts, half)]),
        out_specs=[pl.BlockSpec((h, p.shape[2]), lambda i: (i, 0)) for p, h in zip(parts, half)],
        out_shape=[jax.ShapeDtypeStruct(p.shape[1:], F32) for p in parts],
        compiler_params=_params(("arbitrary",)),
        name=name,
    )(*parts, *from_chips)


def _adam_math(w, g, m, v):
    nm = ADAM_B1 * m + (1.0 - ADAM_B1) * g
    nv = ADAM_B2 * v + (1.0 - ADAM_B2) * (g * g)
    m_hat = nm / (1.0 - ADAM_B1 ** ADAM_STEP)
    v_hat = nv / (1.0 - ADAM_B2 ** ADAM_STEP)
    return -ADAM_LR * (m_hat / (jnp.sqrt(v_hat) + ADAM_EPS) + ADAM_WD * w), nm, nv


def _small_update(gathered, layout, params, name):
    ng, npar = len(gathered), len(params)

    def body(*refs):
        parts = refs[:ng]
        prm = refs[ng:ng + 3 * npar]
        tot_refs = refs[ng + 3 * npar:2 * ng + 3 * npar]
        out = refs[2 * ng + 3 * npar:]
        tots = []
        for a in range(ng):
            acc = parts[a][0]
            for d in range(1, N_DEV):
                acc = acc + parts[a][d]
            tot_refs[a][...] = acc
            tots.append(acc)
        for i, (a, row, width) in enumerate(layout):
            g = tots[a] if row is None else tots[a][row:row + 1, :width]
            delta, nm, nv = _adam_math(prm[3 * i][...], g, prm[3 * i + 1][...], prm[3 * i + 2][...])
            out[4 * i][...] = g
            out[4 * i + 1][...] = delta
            out[4 * i + 2][...] = nm
            out[4 * i + 3][...] = nv

    flat = [t for prm in params for t in prm]
    res = pl.pallas_call(
        body,
        out_shape=([jax.ShapeDtypeStruct(g.shape[1:], F32) for g in gathered]
                   + [jax.ShapeDtypeStruct(prm[0].shape, F32) for prm in params for _ in range(4)]),
        compiler_params=pltpu.CompilerParams(vmem_limit_bytes=V7X_VMEM_LIMIT),
        name=name,
    )(*gathered, *flat)
    return res[:ng], [tuple(res[ng + 4 * i:ng + 4 * i + 4]) for i in range(npar)]


def _adamw(w, g, m, v, name):
    rows, cols = w.shape
    br = rows
    for cand in (512, 256, 128):
        if rows % cand == 0 and rows > cand:
            br = cand
            break

    def body(w_ref, g_ref, m_ref, v_ref, d_ref, nm_ref, nv_ref):
        d_ref[...], nm_ref[...], nv_ref[...] = _adam_math(w_ref[...], g_ref[...], m_ref[...], v_ref[...])

    spec = pl.BlockSpec((br, cols), lambda i: (i, 0))
    shape = jax.ShapeDtypeStruct((rows, cols), F32)
    return pl.pallas_call(
        body,
        grid=(rows // br,),
        in_specs=[spec] * 4,
        out_specs=[spec] * 3,
        out_shape=[shape] * 3,
        compiler_params=_params(("arbitrary",)),
        name=name,
    )(w, g, m, v)


def _by_device(full):
    return full.reshape(N_DEV, full.shape[0] // N_DEV, full.shape[1])


def kernel(x, p, g_mix, w_in, conv_w, conv_b, ln_g, ln_b, pool_w, pool_scale, w_out, g_ffn, w_gate_up, w_down, g_ple_gate, w_ple_gate, w_ple_up, g_ple_post, g_final, loss_target, m_g_mix, m_w_in, m_conv_w, m_conv_b, m_ln_g, m_ln_b, m_pool_w, m_pool_scale, m_w_out, m_g_ffn, m_w_gate_up, m_w_down, m_g_ple_gate, m_w_ple_gate, m_w_ple_up, m_g_ple_post, m_g_final, v_g_mix, v_w_in, v_conv_w, v_conv_b, v_ln_g, v_ln_b, v_pool_w, v_pool_scale, v_w_out, v_g_ffn, v_w_gate_up, v_w_down, v_g_ple_gate, v_w_ple_gate, v_w_ple_up, v_g_ple_post, v_g_final):
    seq = x.shape[1]
    xs = x[0]
    ps = p[0, 0]
    tgt = loss_target[0]
    ax, ay, ac = lax.axis_index("x"), lax.axis_index("y"), lax.axis_index("c")
    me = _block(ax, ay, ac)
    blks = jnp.stack([_block(1 - ax, ay, ac), _block(ax, 1 - ay, ac), _block(1 - ax, 1 - ay, ac), me]).astype(jnp.int32)
    rows = lambda gth: gth.reshape((-1,) + gth.shape[2:])

    w_in_t, w_out_f, conv_w_t = [rows(gth) for gth in _all_gather([
        w_in[0].T.astype(BF16),
        w_out[0].astype(BF16),
        jnp.pad(conv_w[0].T, ((0, 0), (0, CONV_HALO - CONV_K))),
    ], "gather_first")]
    conv_w_f = conv_w_t.T
    pool_w_b = pool_w[0].astype(BF16)

    carry = _Carry()
    _carry_gather(carry, [w_gate_up[0].T.astype(BF16)], 12)
    (z, u1, pooled, x1), (w_gu_all,) = _fwd_mix(
        xs, w_in_t, w_out_f, g_mix, conv_w_f, conv_b, ln_g, ln_b, pool_w_b, pool_scale, min(256, seq), carry)
    w_gu_t = rows(w_gu_all)

    carry = _Carry()
    _carry_gather(carry, [w_down[0].astype(BF16),
                          w_ple_gate[0].astype(BF16),
                          w_ple_up[0].T.astype(BF16)], 12)
    (h2, g_sav, u_sav, act), late = _ffn_up(x1, w_gu_t, g_ffn, min(512, seq), carry)
    w_down_f, w_pg_f, w_pu_t = [rows(gth) for gth in late]
    x2 = _ffn_down(x1, act, w_down_f, min(512, seq))
    dx2, dx2b, hg, ds, dpe, pb, stats_ple = _ple_loss(x2, ps, tgt, w_pg_f, w_pu_t, g_ple_gate, g_ple_post,
                                                      g_final.reshape(1, D_MODEL), min(256, seq))
    dg, du, dx1, dx1b, stats_ffn = _ffn_bwd(dx2, dx2b, x1, g_sav, u_sav, w_gu_t, w_down_f, g_ffn, min(512, seq))

    d_w_pg, _ = _grad_matmul(hg, ds, 256, "grad_w_ple_gate")
    d_w_pu_t, _ = _grad_matmul(dpe, pb, 256, "grad_w_ple_up")
    d_w_gu_t, _ = _grad_matmul(dg, h2, 256, "grad_w_gate_up", a2=du)
    early = [_by_device(d_w_pg), _by_device(d_w_pu_t), _by_device(d_w_gu_t)]
    carry = _Carry()
    _carry_pair(carry, early)
    d_w_down, from_sib = _grad_matmul(act, dx2b, 256, "grad_w_down", carry=carry)
    early_parts = _pair_add(early, from_sib, blks, "pair_add_early")

    carry = _Carry()
    _carry_chip(carry, early_parts)
    _carry_pair(carry, [_by_device(d_w_down)])
    (dx, dz, mix, h1, vec_mix, dconv_w_part, dpool_w_part), carried = _bwd_mix(
        dx1, dx1b, xs, z, u1, pooled, w_in_t, w_out_f, g_mix, conv_w_f, ln_g, ln_b, pool_w_b, pool_scale,
        min(256, seq), carry)
    early_chips, down_sib = carried[:3], carried[3:]
    down_parts = _pair_add([_by_device(d_w_down)], down_sib, blks, "pair_add_down")

    carry = _Carry()
    _carry_chip(carry, down_parts)
    d_w_in_t, down_chips = _grad_matmul(dz, h1, 256, "grad_w_in", carry=carry)
    d_w_out, _ = _grad_matmul(mix, dx1b, 256, "grad_w_out")
    gr_w_in_t, gr_w_out = _reduce_scatter([_by_device(d_w_in_t), _by_device(d_w_out)], "scatter_last")
    gr_w_pg, gr_w_pu_t, gr_w_gu_t, gr_w_down = _chip_sum(early_parts + down_parts, early_chips + down_chips,
                                                          "chip_sum")

    small = _all_gather([vec_mix, stats_ple, stats_ffn, dconv_w_part, dpool_w_part], "gather_small")
    vec_names = ["g_mix", "ln_g", "ln_b", "conv_b", "pool_scale", "g_final", "g_ple_post", "g_ple_gate", "g_ffn",
                 "pool_w"]
    layout = [(0, 0, D_MODEL), (0, 1, C_CONV), (0, 2, C_CONV), (0, 3, C_CONV), (0, 4, C_POOL),
              (1, 0, D_MODEL), (1, 1, D_MODEL), (1, 2, D_MODEL), (2, 0, D_MODEL), (4, None, None)]
    as_row = lambda t: t.reshape(1, D_MODEL)
    params = [(g_mix, m_g_mix, v_g_mix), (ln_g, m_ln_g, v_ln_g), (ln_b, m_ln_b, v_ln_b),
              (conv_b, m_conv_b, v_conv_b), (pool_scale, m_pool_scale, v_pool_scale),
              (as_row(g_final), as_row(m_g_final), as_row(v_g_final)),
              (g_ple_post, m_g_ple_post, v_g_ple_post), (g_ple_gate, m_g_ple_gate, v_g_ple_gate),
              (g_ffn, m_g_ffn, v_g_ffn), (pool_w[0], m_pool_w[0], v_pool_w[0])]
    tots, small_upd = _small_update(small, layout, params, "small_update")
    loss = tots[1][3, 0]
    upd = {}
    for nm, res, prm in zip(vec_names, small_upd, [g_mix, ln_g, ln_b, conv_b, pool_scale, g_final, g_ple_post,
                                                    g_ple_gate, g_ffn, pool_w]):
        upd[nm] = tuple(t.reshape(prm.shape) for t in res)
    gr_conv_w = lax.dynamic_slice_in_dim(tots[3][:CONV_K], me * (C_CONV // N_DEV), C_CONV // N_DEV, axis=1)

    big = [
        ("w_in", w_in[0], gr_w_in_t.T, m_w_in[0], v_w_in[0]),
        ("w_out", w_out[0], gr_w_out, m_w_out[0], v_w_out[0]),
        ("w_gate_up", w_gate_up[0], gr_w_gu_t.T, m_w_gate_up[0], v_w_gate_up[0]),
        ("w_down", w_down[0], gr_w_down, m_w_down[0], v_w_down[0]),
        ("w_ple_gate", w_ple_gate[0], gr_w_pg, m_w_ple_gate[0], v_w_ple_gate[0]),
        ("w_ple_up", w_ple_up[0], gr_w_pu_t.T, m_w_ple_up[0], v_w_ple_up[0]),
        ("conv_w", conv_w[0], gr_conv_w, m_conv_w[0], v_conv_w[0]),
    ]
    for nm, w_, g_, m_, v_ in big:
        d_, nm_, nv_ = _adamw(w_, g_, m_, v_, "adamw_" + nm)
        upd[nm] = (g_[None], d_[None], nm_[None], nv_[None])

    order = ["g_mix", "w_in", "conv_w", "conv_b", "ln_g", "ln_b", "pool_w", "pool_scale", "w_out", "g_ffn",
             "w_gate_up", "w_down", "g_ple_gate", "w_ple_gate", "w_ple_up", "g_ple_post", "g_final"]
    outs = [loss, dx[None]]
    for k in range(4):
        outs += [upd[nm][k] for nm in order]
    return tuple(outs)
```

```python
import functools

import jax
import jax.numpy as jnp
from jax import lax
from jax.experimental import pallas as pl
from jax.experimental.pallas import tpu as pltpu

D_MODEL = 1024
C_CONV = 512
C_POOL = 512
Z_WIDTH = 2 * C_CONV + C_POOL
POOL_WINDOWS = (2, 4, 8, 16)
POOL_GROUP = 128
CONV_K = 31
D_FF = 2816
D_PLE = 256
EPS = 1e-6
N_DEV = 8

ADAM_LR = 0.001
ADAM_B1 = 0.9
ADAM_B2 = 0.999
ADAM_EPS = 1e-08
ADAM_WD = 0.01
ADAM_STEP = 10

CONV_HALO = 32
POOL_HALO = 16
ROW_CHUNK = 32
V7X_VMEM_LIMIT = 56 * 1024 * 1024
FF_CHUNK = D_FF // 2

BF16 = jnp.bfloat16
F32 = jnp.float32
MESH = pl.DeviceIdType.MESH


def _dot(a, b):
    return lax.dot_general(a, b, (((1,), (0,)), ((), ())), preferred_element_type=F32)


def _dot_nt(a, b):
    return lax.dot_general(a, b, (((1,), (1,)), ((), ())), preferred_element_type=F32)


def _dot_tn(a, b):
    return lax.dot_general(a, b, (((0,), (0,)), ((), ())), preferred_element_type=F32)


def _rms_fwd(x, g):
    r = lax.rsqrt(jnp.mean(x * x, axis=-1, keepdims=True) + EPS)
    return x * r * g, r


def _rms_bwd(x, r, g, dy):
    xr = x * r
    dg = jnp.sum(dy * xr, axis=0, keepdims=True)
    dyg = dy * g
    dx = r * (dyg - xr * jnp.mean(dyg * xr, axis=-1, keepdims=True))
    return dx, dg


def _sigmoid(x):
    return jax.nn.sigmoid(x)


def _params(sem=None):
    return pltpu.CompilerParams(dimension_semantics=sem, vmem_limit_bytes=V7X_VMEM_LIMIT)


def _place():
    x, y, c = lax.axis_index("x"), lax.axis_index("y"), lax.axis_index("c")
    chips = [(1 - x, y), (x, 1 - y), (1 - x, 1 - y)]
    return (x, y, c), (x, y, 1 - c), chips


def _block(px, py, pc):
    return 4 * px + 2 * py + pc


class _Carry:
    def __init__(self):
        self.inputs = []
        self.out_shapes = []
        self.copies = []
        self.locals = []

    def add_input(self, arr):
        self.inputs.append(arr)
        return len(self.inputs) - 1

    def add_output(self, shape, dtype):
        self.out_shapes.append(jax.ShapeDtypeStruct(shape, dtype))
        return len(self.out_shapes) - 1

    def local(self, src_idx, dst_idx, dst_blk):
        self.locals.append((src_idx, dst_idx, dst_blk))

    def copy(self, src, dst_idx, dst_blk, got_blk, peer, step=0, after=()):
        self.copies.append(dict(src=src, dst_idx=dst_idx, dst_blk=dst_blk, got_blk=got_blk, peer=peer, step=step,
                                after=tuple(after)))
        return len(self.copies) - 1

    def sem_shapes(self):
        return [pltpu.SemaphoreType.DMA((max(1, len(self.copies)),)),
                pltpu.SemaphoreType.DMA((max(1, len(self.copies)),)),
                pltpu.SemaphoreType.DMA((max(1, len(self.locals)),))]

    def _desc(self, k, ins, outs, sems, place):
        cp = self.copies[k]
        me, sib, chips = place
        kind, idx, blk = cp["src"]
        src = (ins if kind == "in" else outs)[idx]
        if blk is not None:
            src = src.at[blk(*place)]
        to = sib if cp["peer"] == "sib" else (*chips[cp["peer"]], me[2])
        return pltpu.make_async_remote_copy(
            src_ref=src, dst_ref=outs[cp["dst_idx"]].at[cp["dst_blk"](*place)],
            send_sem=sems[0].at[k], recv_sem=sems[1].at[k], device_id=to, device_id_type=MESH)

    def _arrival(self, k, outs, sems, place):
        cp = self.copies[k]
        got = outs[cp["dst_idx"]].at[cp["got_blk"](*place)]
        return pltpu.make_async_remote_copy(
            src_ref=got, dst_ref=got, send_sem=sems[0].at[k], recv_sem=sems[1].at[k],
            device_id=place[0], device_id_type=MESH)

    def _local(self, n, ins, outs, sems, place):
        src_idx, dst_idx, blk = self.locals[n]
        return pltpu.make_async_copy(ins[src_idx], outs[dst_idx].at[blk(*place)], sems[2].at[n])

    def starts(self, step, nsteps, ins, outs, sems):
        place = _place()
        self._waited = set()
        for s in sorted({0} | {cp["step"] for cp in self.copies}):
            ks = [k for k, cp in enumerate(self.copies) if cp["step"] == s]

            @pl.when(step == min(s, nsteps - 1))
            def _(s=s, ks=ks):
                if s == 0:
                    for n in range(len(self.locals)):
                        self._local(n, ins, outs, sems, place).start()
                for k in ks:
                    for a in self.copies[k]["after"]:
                        self._arrival(a, outs, sems, place).wait_recv()
                        self._waited.add(a)
                    self._desc(k, ins, outs, sems, place).start()

    def finish(self, step, nsteps, ins, outs, sems):
        place = _place()

        @pl.when(step == nsteps - 1)
        def _():
            for k in range(len(self.copies)):
                if k not in self._waited:
                    self._arrival(k, outs, sems, place).wait_recv()
            for k in range(len(self.copies)):
                self._desc(k, ins, outs, sems, place).wait_send()
            for n in range(len(self.locals)):
                self._local(n, ins, outs, sems, place).wait()


def _const_blk(j):
    return lambda me, sib, chips: j


def _carry_gather(carry, shards, fwd_step):
    outs = []
    for sh in shards:
        i = carry.add_input(sh)
        o = carry.add_output((N_DEV,) + sh.shape, sh.dtype)
        mine = lambda me, sib, chips: _block(*me)
        carry.local(i, o, mine)
        carry.copy(("in", i, None), o, mine, lambda me, sib, chips: _block(*sib), "sib")
        first = []
        for j in range(3):
            theirs = lambda me, sib, chips, j=j: _block(*chips[j], me[2])
            first.append(carry.copy(("in", i, None), o, mine, theirs, j))
        for j in range(3):
            theirs = lambda me, sib, chips, j=j: _block(*chips[j], me[2])
            cousin = lambda me, sib, chips, j=j: _block(*chips[j], 1 - me[2])
            carry.copy(("out", o, theirs), o, theirs, cousin, "sib", step=fwd_step, after=(first[j],))
        outs.append(o)
    return outs


def _carry_pair(carry, grads):
    outs = []
    for g in grads:
        i = carry.add_input(g)
        o = carry.add_output((4,) + g.shape[1:], g.dtype)
        for j in range(4):
            if j < 3:
                blk = lambda me, sib, chips, j=j: _block(*chips[j], 1 - me[2])
            else:
                blk = lambda me, sib, chips: _block(*sib)
            carry.copy(("in", i, blk), o, _const_blk(j), _const_blk(j), "sib")
        outs.append(o)
    return outs


def _carry_chip(carry, parts):
    outs = []
    for p in parts:
        i = carry.add_input(p)
        o = carry.add_output((3,) + p.shape[1:], p.dtype)
        for j in range(3):
            carry.copy(("in", i, _const_blk(j)), o, _const_blk(j), _const_blk(j), j)
        outs.append(o)
    return outs


def _pcall(body, *, grid, in_specs, out_specs, out_shape, scratch_shapes, name, args, carry=None):
    sem = ("arbitrary",) * len(grid)
    if carry is None:
        res = pl.pallas_call(body, grid=grid, in_specs=in_specs, out_specs=out_specs, out_shape=out_shape,
                             scratch_shapes=scratch_shapes, compiler_params=_params(sem), name=name)(*args)
        return list(res), []
    n_in, n_out, n_scr = len(in_specs), len(out_specs), len(scratch_shapes)
    c_in, c_out = len(carry.inputs), len(carry.out_shapes)
    nsteps = 1
    for extent in grid:
        nsteps *= extent

    def wrapped(*refs):
        ins = refs[:n_in]
        cins = refs[n_in:n_in + c_in]
        o0 = n_in + c_in
        outs = refs[o0:o0 + n_out]
        couts = refs[o0 + n_out:o0 + n_out + c_out]
        s0 = o0 + n_out + c_out
        scr = refs[s0:s0 + n_scr]
        sems = refs[s0 + n_scr:]
        step = pl.program_id(0)
        for d in range(1, len(grid)):
            step = step * grid[d] + pl.program_id(d)
        carry.starts(step, nsteps, cins, couts, sems)
        body(*ins, *outs, *scr)
        carry.finish(step, nsteps, cins, couts, sems)

    any_spec = pl.BlockSpec(memory_space=pl.ANY)
    res = pl.pallas_call(
        wrapped, grid=grid,
        in_specs=list(in_specs) + [any_spec] * c_in,
        out_specs=list(out_specs) + [any_spec] * c_out,
        out_shape=list(out_shape) + carry.out_shapes,
        scratch_shapes=list(scratch_shapes) + carry.sem_shapes(),
        compiler_params=_params(sem), name=name)(*args, *carry.inputs)
    return list(res[:n_out]), list(res[n_out:])


def _shifted_copies(buf, shifted, tm):
    span = tm + CONV_HALO - 8
    for r in range(1, 8):
        shifted[r - 1, 0:span, :] = buf[r:r + span, :]


def _rows_at(buf, shifted, start):
    aligned, r = (start // 8) * 8, start % 8
    if r == 0:
        return buf[aligned:aligned + ROW_CHUNK, :]
    return shifted[r - 1, aligned:aligned + ROW_CHUNK, :]


def _row_spec(tm, width):
    return pl.BlockSpec((tm, width), lambda i: (i, 0))


def _const_spec(shape):
    return pl.BlockSpec(shape, lambda i: (0,) * len(shape))


def _fwd_mix(x, w_in_t, w_out, g_mix, conv_w, conv_b, ln_g, ln_b, pool_w, pool_scale, tm, carry=None):
    s = x.shape[0]
    nt = s // tm

    def body(x_ref, win_ref, wout_ref, gmix_ref, cw_ref, cb_ref, lng_ref, lnb_ref, pw_ref, ps_ref,
             z_ref, u1_ref, pooled_ref, x1_ref, ubuf, vbuf, ush):
        i = pl.program_id(0)

        @pl.when(i == 0)
        def _():
            ubuf[0:CONV_HALO, :] = jnp.zeros((CONV_HALO, C_CONV), F32)
            vbuf[0:POOL_HALO, :] = jnp.zeros((POOL_HALO, C_POOL), F32)

        xt = x_ref[...]
        h, _ = _rms_fwd(xt, gmix_ref[...])
        z = _dot_nt(h.astype(BF16), win_ref[...])
        z_ref[...] = z
        a = z[:, :C_CONV]
        b = z[:, C_CONV:2 * C_CONV]
        v = z[:, 2 * C_CONV:]
        ubuf[CONV_HALO:CONV_HALO + tm, :] = a * _sigmoid(b)
        vbuf[POOL_HALO:POOL_HALO + tm, :] = v

        _shifted_copies(ubuf, ush, tm)
        for rc in range(tm // ROW_CHUNK):
            base = rc * ROW_CHUNK + CONV_HALO - (CONV_K - 1)
            acc = jnp.broadcast_to(cb_ref[...], (ROW_CHUNK, C_CONV))
            for k in range(CONV_K):
                acc = acc + cw_ref[k:k + 1, :] * _rows_at(ubuf, ush, base + k)
            u1_ref[rc * ROW_CHUNK:(rc + 1) * ROW_CHUNK, :] = acc

        u1 = u1_ref[...]
        mu = jnp.mean(u1, axis=-1, keepdims=True)
        cen = u1 - mu
        rstd = lax.rsqrt(jnp.mean(cen * cen, axis=-1, keepdims=True) + EPS)
        u2 = cen * rstd * lng_ref[...] + lnb_ref[...]
        u = u2 * _sigmoid(u2)

        pos1 = (i * tm + lax.broadcasted_iota(jnp.int32, (tm, 1), 0) + 1).astype(F32)
        parts = [u]
        for g, w in enumerate(POOL_WINDOWS):
            cols = slice(g * POOL_GROUP, (g + 1) * POOL_GROUP)
            vg = v[:, cols]
            tot = vg
            for j in range(1, w):
                tot = tot + vbuf[POOL_HALO - j:POOL_HALO - j + tm, cols]
            pooled = tot / jnp.minimum(pos1, float(w)) - vg
            pooled_b = pooled.astype(BF16)
            pooled_ref[:, cols] = pooled_b
            parts.append(_dot(pooled_b, pw_ref[g]) * ps_ref[:, cols])
        mix = jnp.concatenate(parts, axis=-1).astype(BF16)
        x1_ref[...] = xt + _dot(mix, wout_ref[...])

        ubuf[0:CONV_HALO, :] = ubuf[tm:tm + CONV_HALO, :]
        vbuf[0:POOL_HALO, :] = vbuf[tm:tm + POOL_HALO, :]

    return _pcall(
        body,
        grid=(nt,),
        in_specs=[
            _row_spec(tm, D_MODEL),
            _const_spec((Z_WIDTH, D_MODEL)),
            _const_spec((D_MODEL, D_MODEL)),
            _const_spec((1, D_MODEL)),
            _const_spec((CONV_HALO, C_CONV)),
            _const_spec((1, C_CONV)),
            _const_spec((1, C_CONV)),
            _const_spec((1, C_CONV)),
            _const_spec((len(POOL_WINDOWS), POOL_GROUP, POOL_GROUP)),
            _const_spec((1, C_POOL)),
        ],
        out_specs=[
            _row_spec(tm, Z_WIDTH),
            _row_spec(tm, C_CONV),
            _row_spec(tm, C_POOL),
            _row_spec(tm, D_MODEL),
        ],
        out_shape=[
            jax.ShapeDtypeStruct((s, Z_WIDTH), F32),
            jax.ShapeDtypeStruct((s, C_CONV), F32),
            jax.ShapeDtypeStruct((s, C_POOL), BF16),
            jax.ShapeDtypeStruct((s, D_MODEL), F32),
        ],
        scratch_shapes=[
            pltpu.VMEM((tm + CONV_HALO, C_CONV), F32),
            pltpu.VMEM((tm + POOL_HALO, C_POOL), F32),
            pltpu.VMEM((7, tm + CONV_HALO, C_CONV), F32),
        ],
        name="fwd_mix",
        args=(x, w_in_t, w_out, g_mix, conv_w, conv_b, ln_g, ln_b, pool_w, pool_scale),
        carry=carry,
    )


def _ffn_up(x1, w_gu_t, g_ffn, tm, carry=None):
    s = x1.shape[0]
    nt = s // tm
    nc = D_FF // FF_CHUNK

    def body(x1_ref, wg_ref, wu_ref, gffn_ref, h2_ref, g_ref, u_ref, act_ref, h2_buf):
        c = pl.program_id(1)

        @pl.when(c == 0)
        def _():
            h, _ = _rms_fwd(x1_ref[...], gffn_ref[...])
            h2_buf[...] = h.astype(BF16)
            h2_ref[...] = h2_buf[...]

        h2 = h2_buf[...]
        g = _dot_nt(h2, wg_ref[...])
        u = _dot_nt(h2, wu_ref[...])
        g_ref[...] = g.astype(BF16)
        u_ref[...] = u.astype(BF16)
        act_ref[...] = (g * _sigmoid(g) * u).astype(BF16)

    tok = lambda width: pl.BlockSpec((tm, width), lambda i, c: (i, 0))
    chunk = pl.BlockSpec((tm, FF_CHUNK), lambda i, c: (i, c))
    return _pcall(
        body,
        grid=(nt, nc),
        in_specs=[
            tok(D_MODEL),
            pl.BlockSpec((FF_CHUNK, D_MODEL), lambda i, c: (c, 0)),
            pl.BlockSpec((FF_CHUNK, D_MODEL), lambda i, c: (c + nc, 0)),
            pl.BlockSpec((1, D_MODEL), lambda i, c: (0, 0)),
        ],
        out_specs=[tok(D_MODEL), chunk, chunk, chunk],
        out_shape=[
            jax.ShapeDtypeStruct((s, D_MODEL), BF16),
            jax.ShapeDtypeStruct((s, D_FF), BF16),
            jax.ShapeDtypeStruct((s, D_FF), BF16),
            jax.ShapeDtypeStruct((s, D_FF), BF16),
        ],
        scratch_shapes=[pltpu.VMEM((tm, D_MODEL), BF16)],
        name="ffn_up",
        args=(x1, w_gu_t, w_gu_t, g_ffn),
        carry=carry,
    )


def _ffn_down(x1, act, w_down, tm):
    s = x1.shape[0]

    def body(x1_ref, act_ref, wd_ref, x2_ref):
        x2_ref[...] = x1_ref[...] + _dot(act_ref[...], wd_ref[...])

    return pl.pallas_call(
        body,
        grid=(s // tm,),
        in_specs=[_row_spec(tm, D_MODEL), _row_spec(tm, D_FF), _const_spec((D_FF, D_MODEL))],
        out_specs=_row_spec(tm, D_MODEL),
        out_shape=jax.ShapeDtypeStruct((s, D_MODEL), F32),
        compiler_params=_params(("arbitrary",)),
        name="ffn_down",
    )(x1, act, w_down)


def _ple_loss(x2, p, tgt, w_pg, w_pu_t, g_gate, g_post, g_final, tm):
    s = x2.shape[0]
    nt = s // tm

    def body(x2_ref, p_ref, t_ref, wpg_ref, wpu_ref, gg_ref, gp_ref, gf_ref,
             dx2_ref, dx2b_ref, hg_ref, ds_ref, dpe_ref, pb_ref, stats_ref):
        i = pl.program_id(0)

        @pl.when(i == 0)
        def _():
            stats_ref[...] = jnp.zeros_like(stats_ref)

        x2 = x2_ref[...]
        hg, rg = _rms_fwd(x2, gg_ref[...])
        hg_b = hg.astype(BF16)
        hg_ref[...] = hg_b
        gate = _sigmoid(_dot(hg_b, wpg_ref[...]))
        pb = p_ref[...].astype(BF16)
        pb_ref[...] = pb
        pe = _dot_nt(pb, wpu_ref[...])
        e, rp = _rms_fwd(pe, gp_ref[...])
        x3 = x2 + gate * e
        y, r3 = _rms_fwd(x3, gf_ref[...])
        diff = y - t_ref[...]
        loss = 0.5 * jnp.sum(jnp.sum(diff * diff, axis=-1, keepdims=True), axis=0, keepdims=True) / D_MODEL
        dy = diff * (1.0 / D_MODEL)

        dx3, dgf = _rms_bwd(x3, r3, gf_ref[...], dy)
        dpe, dgp = _rms_bwd(pe, rp, gp_ref[...], dx3 * gate)
        dpe_ref[...] = dpe.astype(BF16)
        ds = (dx3 * e * gate * (1.0 - gate)).astype(BF16)
        ds_ref[...] = ds
        dhg = _dot_nt(ds, wpg_ref[...])
        dxg, dgg = _rms_bwd(x2, rg, gg_ref[...], dhg)
        dx2 = dx3 + dxg
        dx2_ref[...] = dx2
        dx2b_ref[...] = dx2.astype(BF16)

        stats_ref[0:1, :] += dgf
        stats_ref[1:2, :] += dgp
        stats_ref[2:3, :] += dgg
        stats_ref[3:4, :] += jnp.broadcast_to(loss, (1, D_MODEL))

    return pl.pallas_call(
        body,
        grid=(nt,),
        in_specs=[
            _row_spec(tm, D_MODEL),
            _row_spec(tm, D_PLE),
            _row_spec(tm, D_MODEL),
            _const_spec((D_MODEL, D_MODEL)),
            _const_spec((D_MODEL, D_PLE)),
            _const_spec((1, D_MODEL)),
            _const_spec((1, D_MODEL)),
            _const_spec((1, D_MODEL)),
        ],
        out_specs=[
            _row_spec(tm, D_MODEL),
            _row_spec(tm, D_MODEL),
            _row_spec(tm, D_MODEL),
            _row_spec(tm, D_MODEL),
            _row_spec(tm, D_MODEL),
            _row_spec(tm, D_PLE),
            _const_spec((8, D_MODEL)),
        ],
        out_shape=[
            jax.ShapeDtypeStruct((s, D_MODEL), F32),
            jax.ShapeDtypeStruct((s, D_MODEL), BF16),
            jax.ShapeDtypeStruct((s, D_MODEL), BF16),
            jax.ShapeDtypeStruct((s, D_MODEL), BF16),
            jax.ShapeDtypeStruct((s, D_MODEL), BF16),
            jax.ShapeDtypeStruct((s, D_PLE), BF16),
            jax.ShapeDtypeStruct((8, D_MODEL), F32),
        ],
        compiler_params=_params(("arbitrary",)),
        name="ple_loss",
    )(x2, p, tgt, w_pg, w_pu_t, g_gate, g_post, g_final)


def _ffn_bwd(dx2, dx2b, x1, g_sav, u_sav, w_gu_t, w_down, g_ffn, tm):
    s = x1.shape[0]
    nt = s // tm
    nc = D_FF // FF_CHUNK

    def body(dx2_ref, dx2b_ref, x1_ref, g_ref, u_ref, wg_ref, wu_ref, wd_ref, gffn_ref,
             dg_ref, du_ref, dx1_ref, dx1b_ref, stats_ref, acc):
        i = pl.program_id(0)
        c = pl.program_id(1)

        @pl.when(jnp.logical_and(i == 0, c == 0))
        def _():
            stats_ref[...] = jnp.zeros_like(stats_ref)

        @pl.when(c == 0)
        def _():
            acc[...] = jnp.zeros_like(acc)

        dact = _dot_nt(dx2b_ref[...], wd_ref[...])
        g = g_ref[...].astype(F32)
        u = u_ref[...].astype(F32)
        sg = _sigmoid(g)
        dg = (dact * u * sg * (1.0 + g * (1.0 - sg))).astype(BF16)
        du = (dact * g * sg).astype(BF16)
        dg_ref[...] = dg
        du_ref[...] = du
        acc[...] += _dot(dg, wg_ref[...]) + _dot(du, wu_ref[...])

        @pl.when(c == nc - 1)
        def _():
            x1 = x1_ref[...]
            r2 = lax.rsqrt(jnp.mean(x1 * x1, axis=-1, keepdims=True) + EPS)
            dxn, dgf = _rms_bwd(x1, r2, gffn_ref[...], acc[...])
            dx1 = dx2_ref[...] + dxn
            dx1_ref[...] = dx1
            dx1b_ref[...] = dx1.astype(BF16)
            stats_ref[0:1, :] += dgf

    tok = lambda width: pl.BlockSpec((tm, width), lambda i, c: (i, 0))
    chunk = pl.BlockSpec((tm, FF_CHUNK), lambda i, c: (i, c))
    return pl.pallas_call(
        body,
        grid=(nt, nc),
        in_specs=[
            tok(D_MODEL), tok(D_MODEL), tok(D_MODEL), chunk, chunk,
            pl.BlockSpec((FF_CHUNK, D_MODEL), lambda i, c: (c, 0)),
            pl.BlockSpec((FF_CHUNK, D_MODEL), lambda i, c: (c + nc, 0)),
            pl.BlockSpec((FF_CHUNK, D_MODEL), lambda i, c: (c, 0)),
            pl.BlockSpec((1, D_MODEL), lambda i, c: (0, 0)),
        ],
        out_specs=[chunk, chunk, tok(D_MODEL), tok(D_MODEL), pl.BlockSpec((8, D_MODEL), lambda i, c: (0, 0))],
        out_shape=[
            jax.ShapeDtypeStruct((s, D_FF), BF16),
            jax.ShapeDtypeStruct((s, D_FF), BF16),
            jax.ShapeDtypeStruct((s, D_MODEL), F32),
            jax.ShapeDtypeStruct((s, D_MODEL), BF16),
            jax.ShapeDtypeStruct((8, D_MODEL), F32),
        ],
        scratch_shapes=[pltpu.VMEM((tm, D_MODEL), F32)],
        compiler_params=_params(("arbitrary", "arbitrary")),
        name="ffn_bwd",
    )(dx2, dx2b, x1, g_sav, u_sav, w_gu_t, w_gu_t, w_down, g_ffn)


def _bwd_mix(dx1, dx1b, x, z, u1, pooled, w_in_t, w_out, g_mix, conv_w, ln_g, ln_b, pool_w, pool_scale, tm,
             carry=None):
    s = x.shape[0]
    nt = s // tm

    def body(dx1_ref, dx1b_ref, x_ref, z_ref, u1_ref, pooled_ref, win_ref, wout_ref, gmix_ref, cw_ref,
             lng_ref, lnb_ref, pw_ref, ps_ref,
             dx_ref, dz_ref, mix_ref, h1_ref, vec_ref, dcw_ref, dpw_ref, dubuf, dvbuf, u0buf, du0buf, dush):
        i = pl.program_id(0)
        tile = nt - 1 - i

        @pl.when(i == 0)
        def _():
            vec_ref[...] = jnp.zeros_like(vec_ref)
            dcw_ref[...] = jnp.zeros_like(dcw_ref)
            dpw_ref[...] = jnp.zeros_like(dpw_ref)
            dubuf[tm:tm + CONV_HALO, :] = jnp.zeros((CONV_HALO, C_CONV), F32)
            dvbuf[tm:tm + POOL_HALO, :] = jnp.zeros((POOL_HALO, C_POOL), F32)

        dmix = _dot_nt(dx1b_ref[...], wout_ref[...])
        du = dmix[:, :C_CONV]
        dq = dmix[:, C_CONV:]

        pos1 = (tile * tm + lax.broadcasted_iota(jnp.int32, (tm, 1), 0) + 1).astype(F32)
        q_parts = []
        dpooled_parts = []
        dps_rows = []
        for g, w in enumerate(POOL_WINDOWS):
            cols = slice(g * POOL_GROUP, (g + 1) * POOL_GROUP)
            pooled_b = pooled_ref[:, cols]
            mixed = _dot(pooled_b, pw_ref[g])
            dqg = dq[:, cols]
            dps_rows.append(jnp.sum(dqg * mixed, axis=0, keepdims=True))
            q_parts.append(mixed * ps_ref[:, cols])
            dmixed = (dqg * ps_ref[:, cols]).astype(BF16)
            dpw_ref[g] += _dot_tn(pooled_b, dmixed)
            dpooled = _dot_nt(dmixed, pw_ref[g])
            dpooled_parts.append(dpooled)
            dvbuf[0:tm, cols] = dpooled / jnp.minimum(pos1, float(w))
        vec_ref[4:5, 0:C_POOL] += jnp.concatenate(dps_rows, axis=-1)
        dv_parts = []
        for g, w in enumerate(POOL_WINDOWS):
            cols = slice(g * POOL_GROUP, (g + 1) * POOL_GROUP)
            tot = dvbuf[0:tm, cols]
            for j in range(1, w):
                tot = tot + dvbuf[j:j + tm, cols]
            dv_parts.append(tot - dpooled_parts[g])

        u1 = u1_ref[...]
        mu = jnp.mean(u1, axis=-1, keepdims=True)
        cen = u1 - mu
        rstd = lax.rsqrt(jnp.mean(cen * cen, axis=-1, keepdims=True) + EPS)
        xhat = cen * rstd
        u2 = xhat * lng_ref[...] + lnb_ref[...]
        sg2 = _sigmoid(u2)
        du2 = du * sg2 * (1.0 + u2 * (1.0 - sg2))
        vec_ref[1:2, 0:C_CONV] += jnp.sum(du2 * xhat, axis=0, keepdims=True)
        vec_ref[2:3, 0:C_CONV] += jnp.sum(du2, axis=0, keepdims=True)
        t1 = du2 * lng_ref[...]
        du1 = rstd * (t1 - jnp.mean(t1, axis=-1, keepdims=True)
                      - xhat * jnp.mean(t1 * xhat, axis=-1, keepdims=True))
        vec_ref[3:4, 0:C_CONV] += jnp.sum(du1, axis=0, keepdims=True)
        dubuf[0:tm, :] = du1

        zt = z_ref[...]
        a = zt[:, :C_CONV]
        sgb = _sigmoid(zt[:, C_CONV:2 * C_CONV])
        u0buf[...] = a * sgb

        _shifted_copies(dubuf, dush, tm)
        for rc in range(tm // ROW_CHUNK):
            r0 = rc * ROW_CHUNK
            acc = jnp.zeros((ROW_CHUNK, C_CONV), F32)
            for k in range(CONV_K):
                acc = acc + cw_ref[k:k + 1, :] * _rows_at(dubuf, dush, r0 + (CONV_K - 1) - k)
            du0buf[r0:r0 + ROW_CHUNK, :] = acc
        for k in range(CONV_K):
            acc = jnp.zeros((ROW_CHUNK, C_CONV), F32)
            for rc in range(tm // ROW_CHUNK):
                r0 = rc * ROW_CHUNK
                acc = acc + u0buf[r0:r0 + ROW_CHUNK, :] * _rows_at(dubuf, dush, r0 + (CONV_K - 1) - k)
            dcw_ref[k:k + 1, :] += jnp.sum(acc, axis=0, keepdims=True)
        du0 = du0buf[...]

        da = du0 * sgb
        db = du0 * a * sgb * (1.0 - sgb)
        dz = jnp.concatenate([da, db] + dv_parts, axis=-1).astype(BF16)
        dz_ref[...] = dz

        mix_ref[...] = jnp.concatenate([u2 * sg2] + q_parts, axis=-1).astype(BF16)

        xt = x_ref[...]
        h1, r1 = _rms_fwd(xt, gmix_ref[...])
        h1_ref[...] = h1.astype(BF16)
        dh1 = _dot(dz, win_ref[...])
        dxn, dgm = _rms_bwd(xt, r1, gmix_ref[...], dh1)
        dx_ref[...] = dx1_ref[...] + dxn
        vec_ref[0:1, :] += dgm

        dubuf[tm:tm + CONV_HALO, :] = dubuf[0:CONV_HALO, :]
        dvbuf[tm:tm + POOL_HALO, :] = dvbuf[0:POOL_HALO, :]

    rev = lambda width: pl.BlockSpec((tm, width), lambda i: (nt - 1 - i, 0))
    return _pcall(
        body,
        grid=(nt,),
        in_specs=[
            rev(D_MODEL), rev(D_MODEL), rev(D_MODEL), rev(Z_WIDTH), rev(C_CONV), rev(C_POOL),
            _const_spec((Z_WIDTH, D_MODEL)),
            _const_spec((D_MODEL, D_MODEL)),
            _const_spec((1, D_MODEL)),
            _const_spec((CONV_HALO, C_CONV)),
            _const_spec((1, C_CONV)),
            _const_spec((1, C_CONV)),
            _const_spec((len(POOL_WINDOWS), POOL_GROUP, POOL_GROUP)),
            _const_spec((1, C_POOL)),
        ],
        out_specs=[
            rev(D_MODEL), rev(Z_WIDTH), rev(D_MODEL), rev(D_MODEL),
            _const_spec((8, D_MODEL)),
            _const_spec((CONV_HALO, C_CONV)),
            _const_spec((len(POOL_WINDOWS), POOL_GROUP, POOL_GROUP)),
        ],
        out_shape=[
            jax.ShapeDtypeStruct((s, D_MODEL), F32),
            jax.ShapeDtypeStruct((s, Z_WIDTH), BF16),
            jax.ShapeDtypeStruct((s, D_MODEL), BF16),
            jax.ShapeDtypeStruct((s, D_MODEL), BF16),
            jax.ShapeDtypeStruct((8, D_MODEL), F32),
            jax.ShapeDtypeStruct((CONV_HALO, C_CONV), F32),
            jax.ShapeDtypeStruct((len(POOL_WINDOWS), POOL_GROUP, POOL_GROUP), F32),
        ],
        scratch_shapes=[
            pltpu.VMEM((tm + CONV_HALO, C_CONV), F32),
            pltpu.VMEM((tm + POOL_HALO, C_POOL), F32),
            pltpu.VMEM((tm, C_CONV), F32),
            pltpu.VMEM((tm, C_CONV), F32),
            pltpu.VMEM((7, tm + CONV_HALO, C_CONV), F32),
        ],
        name="bwd_mix",
        args=(dx1, dx1b, x, z, u1, pooled, w_in_t, w_out, g_mix, conv_w, ln_g, ln_b, pool_w, pool_scale),
        carry=carry,
    )


def _grad_matmul(a, b, bm, name, a2=None, carry=None):
    s, ma = a.shape
    nb = b.shape[1]
    na = ma // bm
    if a2 is None:
        def body(a_ref, b_ref, o_ref):
            o_ref[...] = _dot_tn(a_ref[...], b_ref[...]).astype(BF16)

        lhs_specs = [pl.BlockSpec((s, bm), lambda i: (0, i))]
        lhs = (a,)
        steps = na
    else:
        def body(a_ref, a2_ref, b_ref, o_ref):
            i = pl.program_id(0)

            @pl.when(i < na)
            def _():
                o_ref[...] = _dot_tn(a_ref[...], b_ref[...]).astype(BF16)

            @pl.when(i >= na)
            def _():
                o_ref[...] = _dot_tn(a2_ref[...], b_ref[...]).astype(BF16)

        lhs_specs = [pl.BlockSpec((s, bm), lambda i: (0, jnp.minimum(i, na - 1))),
                     pl.BlockSpec((s, bm), lambda i: (0, jnp.maximum(i - na, 0)))]
        lhs = (a, a2)
        steps = 2 * na

    outs, carried = _pcall(
        body,
        grid=(steps,),
        in_specs=lhs_specs + [pl.BlockSpec((s, nb), lambda i: (0, 0))],
        out_specs=[pl.BlockSpec((bm, nb), lambda i: (i, 0))],
        out_shape=[jax.ShapeDtypeStruct((steps * bm, nb), BF16)],
        scratch_shapes=[],
        name=name,
        args=lhs + (b,),
        carry=carry,
    )
    return outs[0], carried


def _gather_steps(src, dst, send_sems, recv_sems, local_sems):
    n = len(src)
    me, sib, chips = _place()
    c = me[2]
    started = []
    mine = []

    def copy(a, k, block, to, from_src=False):
        rows = dst[a].at[_block(*block)]
        return pltpu.make_async_remote_copy(
            src_ref=src[a] if from_src else rows, dst_ref=rows,
            send_sem=send_sems.at[a, k], recv_sem=recv_sems.at[a, k],
            device_id=to, device_id_type=MESH)

    def start():
        for a in range(n):
            cp = pltpu.make_async_copy(src[a], dst[a].at[_block(*me)], local_sems.at[a])
            cp.start()
            mine.append(cp)
            first = [copy(a, 0, me, sib, True)]
            first += [copy(a, 1 + j, me, (*chip, c), True) for j, chip in enumerate(chips)]
            for cp in first:
                cp.start()
            started.extend(first)

    def forward():
        for a in range(n):
            for j, chip in enumerate(chips):
                copy(a, 1 + j, (*chip, c), me).wait_recv()
                fwd = copy(a, 4 + j, (*chip, c), sib)
                fwd.start()
                started.append(fwd)

    def finish():
        for a in range(n):
            copy(a, 0, sib, me).wait_recv()
            for j, chip in enumerate(chips):
                copy(a, 4 + j, (*chip, 1 - c), me).wait_recv()
        for cp in started:
            cp.wait_send()
        for cp in mine:
            cp.wait()

    return start, forward, finish


def _gather_sems(n):
    return [pltpu.SemaphoreType.DMA((n, 7)), pltpu.SemaphoreType.DMA((n, 7)), pltpu.SemaphoreType.DMA((n,))]


def _all_gather(shards, name):
    n = len(shards)

    def body(*refs):
        start, forward, finish = _gather_steps(refs[:n], refs[n:2 * n], *refs[2 * n:])
        start()
        forward()
        finish()

    any_spec = pl.BlockSpec(memory_space=pl.ANY)
    return pl.pallas_call(
        body,
        in_specs=[any_spec] * n,
        out_specs=[any_spec] * n,
        out_shape=[jax.ShapeDtypeStruct((N_DEV,) + sh.shape, sh.dtype) for sh in shards],
        scratch_shapes=_gather_sems(n),
        name=name,
    )(*shards)


def _reduce_scatter(grads, small, name):
    n, ns = len(grads), len(small)
    shapes = [g.shape[1:] for g in grads]

    def body(*refs):
        g = refs[:n]
        out = refs[n + ns:2 * n + ns]
        scr = refs[2 * (n + ns):]
        own, loc, r1, r2 = scr[:n], scr[n:2 * n], scr[2 * n:3 * n], scr[3 * n:4 * n]
        load_sems, s1, q1, s2, q2 = scr[4 * n:4 * n + 5]
        gather_start, gather_forward, gather_finish = _gather_steps(
            refs[n:n + ns], refs[2 * n + ns:2 * (n + ns)], *scr[4 * n + 5:])
        me, sib, chips = _place()
        c = me[2]

        gather_start()
        loads = []
        sends = []
        for a in range(n):
            ld = [pltpu.make_async_copy(g[a].at[_block(*chip, c)], loc[a].at[j], load_sems.at[a, j])
                  for j, chip in enumerate(chips)]
            ld.append(pltpu.make_async_copy(g[a].at[_block(*me)], own[a], load_sems.at[a, 3]))
            for cp in ld:
                cp.start()
            loads.append(ld)
            blocks = [(*chip, 1 - c) for chip in chips] + [sib]
            for j, blk in enumerate(blocks):
                cp = pltpu.make_async_remote_copy(
                    src_ref=g[a].at[_block(*blk)], dst_ref=r1[a].at[j],
                    send_sem=s1.at[a, j], recv_sem=q1.at[a, j], device_id=sib, device_id_type=MESH)
                cp.start()
                sends.append(cp)

        def from_sibling(a, j):
            return pltpu.make_async_remote_copy(
                src_ref=r1[a].at[j], dst_ref=r1[a].at[j], send_sem=s1.at[a, j], recv_sem=q1.at[a, j],
                device_id=sib, device_id_type=MESH)

        def partial(a, j, chip):
            return pltpu.make_async_remote_copy(
                src_ref=loc[a].at[j], dst_ref=r2[a].at[j], send_sem=s2.at[a, j], recv_sem=q2.at[a, j],
                device_id=(*chip, c), device_id_type=MESH)

        gather_forward()
        for a in range(n):
            for j, chip in enumerate(chips):
                loads[a][j].wait()
                from_sibling(a, j).wait_recv()
                loc[a][j] = (loc[a][j].astype(F32) + r1[a][j].astype(F32)).astype(BF16)
                cp = partial(a, j, chip)
                cp.start()
                sends.append(cp)
        for a in range(n):
            loads[a][3].wait()
            from_sibling(a, 3).wait_recv()
            acc = own[a][...].astype(F32) + r1[a][3].astype(F32)
            for j, chip in enumerate(chips):
                partial(a, j, chip).wait_recv()
                acc = acc + r2[a][j].astype(F32)
            out[a][...] = acc
        for cp in sends:
            cp.wait_send()
        gather_finish()

    any_spec = pl.BlockSpec(memory_space=pl.ANY)
    vmem_spec = pl.BlockSpec(memory_space=pltpu.VMEM)
    res = pl.pallas_call(
        body,
        in_specs=[any_spec] * (n + ns),
        out_specs=[vmem_spec] * n + [any_spec] * ns,
        out_shape=([jax.ShapeDtypeStruct(sh, F32) for sh in shapes]
                   + [jax.ShapeDtypeStruct((N_DEV,) + sm.shape, sm.dtype) for sm in small]),
        scratch_shapes=(
            [pltpu.VMEM(sh, BF16) for sh in shapes]
            + [pltpu.VMEM((3,) + sh, BF16) for sh in shapes]
            + [pltpu.VMEM((4,) + sh, BF16) for sh in shapes]
            + [pltpu.VMEM((3,) + sh, BF16) for sh in shapes]
            + [pltpu.SemaphoreType.DMA((n, 4)),
               pltpu.SemaphoreType.DMA((n, 4)), pltpu.SemaphoreType.DMA((n, 4)),
               pltpu.SemaphoreType.DMA((n, 3)), pltpu.SemaphoreType.DMA((n, 3))]
            + _gather_sems(ns)
        ),
        compiler_params=pltpu.CompilerParams(vmem_limit_bytes=V7X_VMEM_LIMIT),
        name=name,
    )(*grads, *small)
    return res[:n], res[n:]


def _pair_add(grads, from_sib, blks, name):
    n = len(grads)

    def body(blk_ref, *refs):
        for a in range(n):
            refs[2 * n + a][...] = (refs[a][...].astype(F32) + refs[n + a][...].astype(F32)).astype(BF16)

    mine = [pl.BlockSpec((None,) + g.shape[1:], lambda j, b: (b[j], 0, 0)) for g in grads]
    same = [pl.BlockSpec((None,) + g.shape[1:], lambda j, b: (j, 0, 0)) for g in grads]
    return pl.pallas_call(
        body,
        grid_spec=pltpu.PrefetchScalarGridSpec(
            num_scalar_prefetch=1, grid=(4,), in_specs=mine + same, out_specs=same),
        out_shape=[jax.ShapeDtypeStruct((4,) + g.shape[1:], BF16) for g in grads],
        compiler_params=_params(("arbitrary",)),
        name=name,
    )(blks, *grads, *from_sib)


def _chip_sum(parts, from_chips, name):
    n = len(parts)

    def body(*refs):
        for a in range(n):
            acc = refs[a][...].astype(F32)
            for j in range(3):
                acc = acc + refs[n + a][j].astype(F32)
            refs[2 * n + a][...] = acc

    half = [p.shape[1] // 2 for p in parts]
    return pl.pallas_call(
        body,
        grid=(2,),
        in_specs=([pl.BlockSpec((None, h, p.shape[2]), lambda i: (3, i, 0)) for p, h in zip(parts, half)]
                  + [pl.BlockSpec((3, h, p.shape[2]), lambda i: (0, i, 0)) for p, h in zip(parts, half)]),
        out_specs=[pl.BlockSpec((h, p.shape[2]), lambda i: (i, 0)) for p, h in zip(parts, half)],
        out_shape=[jax.ShapeDtypeStruct(p.shape[1:], F32) for p in parts],
        compiler_params=_params(("arbitrary",)),
        name=name,
    )(*parts, *from_chips)


def _adam_math(w, g, m, v):
    nm = ADAM_B1 * m + (1.0 - ADAM_B1) * g
    nv = ADAM_B2 * v + (1.0 - ADAM_B2) * (g * g)
    m_hat = nm / (1.0 - ADAM_B1 ** ADAM_STEP)
    v_hat = nv / (1.0 - ADAM_B2 ** ADAM_STEP)
    return -ADAM_LR * (m_hat / (jnp.sqrt(v_hat) + ADAM_EPS) + ADAM_WD * w), nm, nv


def _small_update(gathered, layout, params, name):
    ng, npar = len(gathered), len(params)

    def body(*refs):
        parts = refs[:ng]
        prm = refs[ng:ng + 3 * npar]
        tot_refs = refs[ng + 3 * npar:2 * ng + 3 * npar]
        out = refs[2 * ng + 3 * npar:]
        tots = []
        for a in range(ng):
            acc = parts[a][0]
            for d in range(1, N_DEV):
                acc = acc + parts[a][d]
            tot_refs[a][...] = acc
            tots.append(acc)
        for i, (a, row, width) in enumerate(layout):
            g = tots[a] if row is None else tots[a][row:row + 1, :width]
            delta, nm, nv = _adam_math(prm[3 * i][...], g, prm[3 * i + 1][...], prm[3 * i + 2][...])
            out[4 * i][...] = g
            out[4 * i + 1][...] = delta
            out[4 * i + 2][...] = nm
            out[4 * i + 3][...] = nv

    flat = [t for prm in params for t in prm]
    res = pl.pallas_call(
        body,
        out_shape=([jax.ShapeDtypeStruct(g.shape[1:], F32) for g in gathered]
                   + [jax.ShapeDtypeStruct(prm[0].shape, F32) for prm in params for _ in range(4)]),
        compiler_params=pltpu.CompilerParams(vmem_limit_bytes=V7X_VMEM_LIMIT),
        name=name,
    )(*gathered, *flat)
    return res[:ng], [tuple(res[ng + 4 * i:ng + 4 * i + 4]) for i in range(npar)]


def _adamw(w, g, m, v, name):
    rows, cols = w.shape
    br = rows
    for cand in (512, 256, 128):
        if rows % cand == 0 and rows > cand:
            br = cand
            break

    def body(w_ref, g_ref, m_ref, v_ref, d_ref, nm_ref, nv_ref):
        d_ref[...], nm_ref[...], nv_ref[...] = _adam_math(w_ref[...], g_ref[...], m_ref[...], v_ref[...])

    spec = pl.BlockSpec((br, cols), lambda i: (i, 0))
    shape = jax.ShapeDtypeStruct((rows, cols), F32)
    return pl.pallas_call(
        body,
        grid=(rows // br,),
        in_specs=[spec] * 4,
        out_specs=[spec] * 3,
        out_shape=[shape] * 3,
        compiler_params=_params(("arbitrary",)),
        name=name,
    )(w, g, m, v)


def _by_device(full):
    return full.reshape(N_DEV, full.shape[0] // N_DEV, full.shape[1])


def kernel(x, p, g_mix, w_in, conv_w, conv_b, ln_g, ln_b, pool_w, pool_scale, w_out, g_ffn, w_gate_up, w_down, g_ple_gate, w_ple_gate, w_ple_up, g_ple_post, g_final, loss_target, m_g_mix, m_w_in, m_conv_w, m_conv_b, m_ln_g, m_ln_b, m_pool_w, m_pool_scale, m_w_out, m_g_ffn, m_w_gate_up, m_w_down, m_g_ple_gate, m_w_ple_gate, m_w_ple_up, m_g_ple_post, m_g_final, v_g_mix, v_w_in, v_conv_w, v_conv_b, v_ln_g, v_ln_b, v_pool_w, v_pool_scale, v_w_out, v_g_ffn, v_w_gate_up, v_w_down, v_g_ple_gate, v_w_ple_gate, v_w_ple_up, v_g_ple_post, v_g_final):
    seq = x.shape[1]
    xs = x[0]
    ps = p[0, 0]
    tgt = loss_target[0]
    ax, ay, ac = lax.axis_index("x"), lax.axis_index("y"), lax.axis_index("c")
    me = _block(ax, ay, ac)
    blks = jnp.stack([_block(1 - ax, ay, ac), _block(ax, 1 - ay, ac), _block(1 - ax, 1 - ay, ac), me]).astype(jnp.int32)
    rows = lambda gth: gth.reshape((-1,) + gth.shape[2:])

    w_in_t, w_out_f, conv_w_t = [rows(gth) for gth in _all_gather([
        w_in[0].T.astype(BF16),
        w_out[0].astype(BF16),
        jnp.pad(conv_w[0].T, ((0, 0), (0, CONV_HALO - CONV_K))),
    ], "gather_first")]
    conv_w_f = conv_w_t.T
    pool_w_b = pool_w[0].astype(BF16)

    carry = _Carry()
    _carry_gather(carry, [w_gate_up[0].T.astype(BF16)], 12)
    (z, u1, pooled, x1), (w_gu_all,) = _fwd_mix(
        xs, w_in_t, w_out_f, g_mix, conv_w_f, conv_b, ln_g, ln_b, pool_w_b, pool_scale, min(256, seq), carry)
    w_gu_t = rows(w_gu_all)

    carry = _Carry()
    _carry_gather(carry, [w_down[0].astype(BF16),
                          w_ple_gate[0].astype(BF16),
                          w_ple_up[0].T.astype(BF16)], 12)
    (h2, g_sav, u_sav, act), late = _ffn_up(x1, w_gu_t, g_ffn, min(512, seq), carry)
    w_down_f, w_pg_f, w_pu_t = [rows(gth) for gth in late]
    x2 = _ffn_down(x1, act, w_down_f, min(512, seq))
    dx2, dx2b, hg, ds, dpe, pb, stats_ple = _ple_loss(x2, ps, tgt, w_pg_f, w_pu_t, g_ple_gate, g_ple_post,
                                                      g_final.reshape(1, D_MODEL), min(256, seq))
    dg, du, dx1, dx1b, stats_ffn = _ffn_bwd(dx2, dx2b, x1, g_sav, u_sav, w_gu_t, w_down_f, g_ffn, min(512, seq))

    d_w_pg, _ = _grad_matmul(hg, ds, 256, "grad_w_ple_gate")
    d_w_pu_t, _ = _grad_matmul(dpe, pb, 256, "grad_w_ple_up")
    d_w_gu_t, _ = _grad_matmul(dg, h2, 256, "grad_w_gate_up", a2=du)
    early = [_by_device(d_w_pg), _by_device(d_w_pu_t), _by_device(d_w_gu_t)]
    carry = _Carry()
    _carry_pair(carry, early)
    d_w_down, from_sib = _grad_matmul(act, dx2b, 256, "grad_w_down", carry=carry)
    early_parts = _pair_add(early, from_sib, blks, "pair_add_early")

    carry = _Carry()
    _carry_chip(carry, early_parts)
    _carry_pair(carry, [_by_device(d_w_down)])
    (dx, dz, mix, h1, vec_mix, dconv_w_part, dpool_w_part), carried = _bwd_mix(
        dx1, dx1b, xs, z, u1, pooled, w_in_t, w_out_f, g_mix, conv_w_f, ln_g, ln_b, pool_w_b, pool_scale,
        min(256, seq), carry)
    early_chips, down_sib = carried[:3], carried[3:]
    down_parts = _pair_add([_by_device(d_w_down)], down_sib, blks, "pair_add_down")

    carry = _Carry()
    _carry_chip(carry, down_parts)
    d_w_in_t, down_chips = _grad_matmul(dz, h1, 256, "grad_w_in", carry=carry)
    d_w_out, _ = _grad_matmul(mix, dx1b, 256, "grad_w_out")
    (gr_w_in_t, gr_w_out), small = _reduce_scatter(
        [_by_device(d_w_in_t), _by_device(d_w_out)],
        [vec_mix, stats_ple, stats_ffn, dconv_w_part, dpool_w_part], "scatter_last")
    gr_w_pg, gr_w_pu_t, gr_w_gu_t, gr_w_down = _chip_sum(early_parts + down_parts, early_chips + down_chips,
                                                          "chip_sum")

    vec_names = ["g_mix", "ln_g", "ln_b", "conv_b", "pool_scale", "g_final", "g_ple_post", "g_ple_gate", "g_ffn",
                 "pool_w"]
    layout = [(0, 0, D_MODEL), (0, 1, C_CONV), (0, 2, C_CONV), (0, 3, C_CONV), (0, 4, C_POOL),
              (1, 0, D_MODEL), (1, 1, D_MODEL), (1, 2, D_MODEL), (2, 0, D_MODEL), (4, None, None)]
    as_row = lambda t: t.reshape(1, D_MODEL)
    params = [(g_mix, m_g_mix, v_g_mix), (ln_g, m_ln_g, v_ln_g), (ln_b, m_ln_b, v_ln_b),
              (conv_b, m_conv_b, v_conv_b), (pool_scale, m_pool_scale, v_pool_scale),
              (as_row(g_final), as_row(m_g_final), as_row(v_g_final)),
              (g_ple_post, m_g_ple_post, v_g_ple_post), (g_ple_gate, m_g_ple_gate, v_g_ple_gate),
              (g_ffn, m_g_ffn, v_g_ffn), (pool_w[0], m_pool_w[0], v_pool_w[0])]
    tots, small_upd = _small_update(small, layout, params, "small_update")
    loss = tots[1][3, 0]
    upd = {}
    for nm, res, prm in zip(vec_names, small_upd, [g_mix, ln_g, ln_b, conv_b, pool_scale, g_final, g_ple_post,
                                                    g_ple_gate, g_ffn, pool_w]):
        upd[nm] = tuple(t.reshape(prm.shape) for t in res)
    gr_conv_w = lax.dynamic_slice_in_dim(tots[3][:CONV_K], me * (C_CONV // N_DEV), C_CONV // N_DEV, axis=1)

    big = [
        ("w_in", w_in[0], gr_w_in_t.T, m_w_in[0], v_w_in[0]),
        ("w_out", w_out[0], gr_w_out, m_w_out[0], v_w_out[0]),
        ("w_gate_up", w_gate_up[0], gr_w_gu_t.T, m_w_gate_up[0], v_w_gate_up[0]),
        ("w_down", w_down[0], gr_w_down, m_w_down[0], v_w_down[0]),
        ("w_ple_gate", w_ple_gate[0], gr_w_pg, m_w_ple_gate[0], v_w_ple_gate[0]),
        ("w_ple_up", w_ple_up[0], gr_w_pu_t.T, m_w_ple_up[0], v_w_ple_up[0]),
        ("conv_w", conv_w[0], gr_conv_w, m_conv_w[0], v_conv_w[0]),
    ]
    for nm, w_, g_, m_, v_ in big:
        d_, nm_, nv_ = _adamw(w_, g_, m_, v_, "adamw_" + nm)
        upd[nm] = (g_[None], d_[None], nm_[None], nv_[None])

    order = ["g_mix", "w_in", "conv_w", "conv_b", "ln_g", "ln_b", "pool_w", "pool_scale", "w_out", "g_ffn",
             "w_gate_up", "w_down", "g_ple_gate", "w_ple_gate", "w_ple_up", "g_ple_post", "g_final"]
    outs = [loss, dx[None]]
    for k in range(4):
        outs += [upd[nm][k] for nm in order]
    return tuple(outs)
```

```python
import functools

import jax
import jax.numpy as jnp
from jax import lax
from jax.experimental import pallas as pl
from jax.experimental.pallas import tpu as pltpu

D_MODEL = 1024
C_CONV = 512
C_POOL = 512
Z_WIDTH = 2 * C_CONV + C_POOL
POOL_WINDOWS = (2, 4, 8, 16)
POOL_GROUP = 128
CONV_K = 31
D_FF = 2816
D_PLE = 256
EPS = 1e-6
N_DEV = 8

ADAM_LR = 0.001
ADAM_B1 = 0.9
ADAM_B2 = 0.999
ADAM_EPS = 1e-08
ADAM_WD = 0.01
ADAM_STEP = 10

CONV_HALO = 32
POOL_HALO = 16
ROW_CHUNK = 32
V7X_VMEM_LIMIT = 56 * 1024 * 1024
FF_CHUNK = D_FF // 2

BF16 = jnp.bfloat16
F32 = jnp.float32
MESH = pl.DeviceIdType.MESH


def _dot(a, b):
    return lax.dot_general(a, b, (((1,), (0,)), ((), ())), preferred_element_type=F32)


def _dot_nt(a, b):
    return lax.dot_general(a, b, (((1,), (1,)), ((), ())), preferred_element_type=F32)


def _dot_tn(a, b):
    return lax.dot_general(a, b, (((0,), (0,)), ((), ())), preferred_element_type=F32)


def _rms_fwd(x, g):
    r = lax.rsqrt(jnp.mean(x * x, axis=-1, keepdims=True) + EPS)
    return x * r * g, r


def _rms_bwd(x, r, g, dy):
    xr = x * r
    dg = jnp.sum(dy * xr, axis=0, keepdims=True)
    dyg = dy * g
    dx = r * (dyg - xr * jnp.mean(dyg * xr, axis=-1, keepdims=True))
    return dx, dg


def _sigmoid(x):
    return jax.nn.sigmoid(x)


def _params(sem=None):
    return pltpu.CompilerParams(dimension_semantics=sem, vmem_limit_bytes=V7X_VMEM_LIMIT)


def _place():
    x, y, c = lax.axis_index("x"), lax.axis_index("y"), lax.axis_index("c")
    chips = [(1 - x, y), (x, 1 - y), (1 - x, 1 - y)]
    return (x, y, c), (x, y, 1 - c), chips


def _block(px, py, pc):
    return 4 * px + 2 * py + pc


class _Carry:
    def __init__(self):
        self.inputs = []
        self.out_shapes = []
        self.copies = []
        self.locals = []

    def add_input(self, arr):
        self.inputs.append(arr)
        return len(self.inputs) - 1

    def add_output(self, shape, dtype):
        self.out_shapes.append(jax.ShapeDtypeStruct(shape, dtype))
        return len(self.out_shapes) - 1

    def local(self, src_idx, dst_idx, dst_blk):
        self.locals.append((src_idx, dst_idx, dst_blk))

    def copy(self, src, dst_idx, dst_blk, got_blk, peer, step=0, after=()):
        self.copies.append(dict(src=src, dst_idx=dst_idx, dst_blk=dst_blk, got_blk=got_blk, peer=peer, step=step,
                                after=tuple(after)))
        return len(self.copies) - 1

    def sem_shapes(self):
        return [pltpu.SemaphoreType.DMA((max(1, len(self.copies)),)),
                pltpu.SemaphoreType.DMA((max(1, len(self.copies)),)),
                pltpu.SemaphoreType.DMA((max(1, len(self.locals)),))]

    def _desc(self, k, ins, outs, sems, place):
        cp = self.copies[k]
        me, sib, chips = place
        kind, idx, blk = cp["src"]
        src = (ins if kind == "in" else outs)[idx]
        if blk is not None:
            src = src.at[blk(*place)]
        to = sib if cp["peer"] == "sib" else (*chips[cp["peer"]], me[2])
        return pltpu.make_async_remote_copy(
            src_ref=src, dst_ref=outs[cp["dst_idx"]].at[cp["dst_blk"](*place)],
            send_sem=sems[0].at[k], recv_sem=sems[1].at[k], device_id=to, device_id_type=MESH)

    def _arrival(self, k, outs, sems, place):
        cp = self.copies[k]
        got = outs[cp["dst_idx"]].at[cp["got_blk"](*place)]
        return pltpu.make_async_remote_copy(
            src_ref=got, dst_ref=got, send_sem=sems[0].at[k], recv_sem=sems[1].at[k],
            device_id=place[0], device_id_type=MESH)

    def _local(self, n, ins, outs, sems, place):
        src_idx, dst_idx, blk = self.locals[n]
        return pltpu.make_async_copy(ins[src_idx], outs[dst_idx].at[blk(*place)], sems[2].at[n])

    def starts(self, step, nsteps, ins, outs, sems):
        place = _place()
        self._waited = set()
        for s in sorted({0} | {cp["step"] for cp in self.copies}):
            ks = [k for k, cp in enumerate(self.copies) if cp["step"] == s]

            @pl.when(step == min(s, nsteps - 1))
            def _(s=s, ks=ks):
                if s == 0:
                    for n in range(len(self.locals)):
                        self._local(n, ins, outs, sems, place).start()
                for k in ks:
                    for a in self.copies[k]["after"]:
                        self._arrival(a, outs, sems, place).wait_recv()
                        self._waited.add(a)
                    self._desc(k, ins, outs, sems, place).start()

    def finish(self, step, nsteps, ins, outs, sems):
        place = _place()

        @pl.when(step == nsteps - 1)
        def _():
            for k in range(len(self.copies)):
                if k not in self._waited:
                    self._arrival(k, outs, sems, place).wait_recv()
            for k in range(len(self.copies)):
                self._desc(k, ins, outs, sems, place).wait_send()
            for n in range(len(self.locals)):
                self._local(n, ins, outs, sems, place).wait()


def _const_blk(j):
    return lambda me, sib, chips: j


def _carry_gather(carry, shards, fwd_step):
    outs = []
    for sh in shards:
        i = carry.add_input(sh)
        o = carry.add_output((N_DEV,) + sh.shape, sh.dtype)
        mine = lambda me, sib, chips: _block(*me)
        carry.local(i, o, mine)
        carry.copy(("in", i, None), o, mine, lambda me, sib, chips: _block(*sib), "sib")
        first = []
        for j in range(3):
            theirs = lambda me, sib, chips, j=j: _block(*chips[j], me[2])
            first.append(carry.copy(("in", i, None), o, mine, theirs, j))
        for j in range(3):
            theirs = lambda me, sib, chips, j=j: _block(*chips[j], me[2])
            cousin = lambda me, sib, chips, j=j: _block(*chips[j], 1 - me[2])
            carry.copy(("out", o, theirs), o, theirs, cousin, "sib", step=fwd_step, after=(first[j],))
        outs.append(o)
    return outs


def _carry_pair(carry, grads):
    outs = []
    for g in grads:
        i = carry.add_input(g)
        o = carry.add_output((4,) + g.shape[1:], g.dtype)
        for j in range(4):
            if j < 3:
                blk = lambda me, sib, chips, j=j: _block(*chips[j], 1 - me[2])
            else:
                blk = lambda me, sib, chips: _block(*sib)
            carry.copy(("in", i, blk), o, _const_blk(j), _const_blk(j), "sib")
        outs.append(o)
    return outs


def _carry_chip(carry, parts):
    outs = []
    for p in parts:
        i = carry.add_input(p)
        o = carry.add_output((3,) + p.shape[1:], p.dtype)
        for j in range(3):
            carry.copy(("in", i, _const_blk(j)), o, _const_blk(j), _const_blk(j), j)
        outs.append(o)
    return outs


def _pcall(body, *, grid, in_specs, out_specs, out_shape, scratch_shapes, name, args, carry=None):
    sem = ("arbitrary",) * len(grid)
    if carry is None:
        res = pl.pallas_call(body, grid=grid, in_specs=in_specs, out_specs=out_specs, out_shape=out_shape,
                             scratch_shapes=scratch_shapes, compiler_params=_params(sem), name=name)(*args)
        return list(res), []
    n_in, n_out, n_scr = len(in_specs), len(out_specs), len(scratch_shapes)
    c_in, c_out = len(carry.inputs), len(carry.out_shapes)
    nsteps = 1
    for extent in grid:
        nsteps *= extent

    def wrapped(*refs):
        ins = refs[:n_in]
        cins = refs[n_in:n_in + c_in]
        o0 = n_in + c_in
        outs = refs[o0:o0 + n_out]
        couts = refs[o0 + n_out:o0 + n_out + c_out]
        s0 = o0 + n_out + c_out
        scr = refs[s0:s0 + n_scr]
        sems = refs[s0 + n_scr:]
        step = pl.program_id(0)
        for d in range(1, len(grid)):
            step = step * grid[d] + pl.program_id(d)
        carry.starts(step, nsteps, cins, couts, sems)
        body(*ins, *outs, *scr)
        carry.finish(step, nsteps, cins, couts, sems)

    any_spec = pl.BlockSpec(memory_space=pl.ANY)
    res = pl.pallas_call(
        wrapped, grid=grid,
        in_specs=list(in_specs) + [any_spec] * c_in,
        out_specs=list(out_specs) + [any_spec] * c_out,
        out_shape=list(out_shape) + carry.out_shapes,
        scratch_shapes=list(scratch_shapes) + carry.sem_shapes(),
        compiler_params=_params(sem), name=name)(*args, *carry.inputs)
    return list(res[:n_out]), list(res[n_out:])


def _shifted_copies(buf, shifted, tm):
    span = tm + CONV_HALO - 8
    for r in range(1, 8):
        shifted[r - 1, 0:span, :] = buf[r:r + span, :]


def _rows_at(buf, shifted, start):
    aligned, r = (start // 8) * 8, start % 8
    if r == 0:
        return buf[aligned:aligned + ROW_CHUNK, :]
    return shifted[r - 1, aligned:aligned + ROW_CHUNK, :]


def _row_spec(tm, width):
    return pl.BlockSpec((tm, width), lambda i: (i, 0))


def _const_spec(shape):
    return pl.BlockSpec(shape, lambda i: (0,) * len(shape))


def _fwd_mix(x, w_in_t, w_out, g_mix, conv_w, conv_b, ln_g, ln_b, pool_w, pool_scale, tm, carry=None):
    s = x.shape[0]
    nt = s // tm

    def body(x_ref, win_ref, wout_ref, gmix_ref, cw_ref, cb_ref, lng_ref, lnb_ref, pw_ref, ps_ref,
             z_ref, u1_ref, pooled_ref, x1_ref, ubuf, vbuf, ush):
        i = pl.program_id(0)

        @pl.when(i == 0)
        def _():
            ubuf[0:CONV_HALO, :] = jnp.zeros((CONV_HALO, C_CONV), F32)
            vbuf[0:POOL_HALO, :] = jnp.zeros((POOL_HALO, C_POOL), F32)

        xt = x_ref[...]
        h, _ = _rms_fwd(xt, gmix_ref[...])
        z = _dot_nt(h.astype(BF16), win_ref[...])
        z_ref[...] = z
        a = z[:, :C_CONV]
        b = z[:, C_CONV:2 * C_CONV]
        v = z[:, 2 * C_CONV:]
        ubuf[CONV_HALO:CONV_HALO + tm, :] = a * _sigmoid(b)
        vbuf[POOL_HALO:POOL_HALO + tm, :] = v

        _shifted_copies(ubuf, ush, tm)
        for rc in range(tm // ROW_CHUNK):
            base = rc * ROW_CHUNK + CONV_HALO - (CONV_K - 1)
            acc = jnp.broadcast_to(cb_ref[...], (ROW_CHUNK, C_CONV))
            for k in range(CONV_K):
                acc = acc + cw_ref[k:k + 1, :] * _rows_at(ubuf, ush, base + k)
            u1_ref[rc * ROW_CHUNK:(rc + 1) * ROW_CHUNK, :] = acc

        u1 = u1_ref[...]
        mu = jnp.mean(u1, axis=-1, keepdims=True)
        cen = u1 - mu
        rstd = lax.rsqrt(jnp.mean(cen * cen, axis=-1, keepdims=True) + EPS)
        u2 = cen * rstd * lng_ref[...] + lnb_ref[...]
        u = u2 * _sigmoid(u2)

        pos1 = (i * tm + lax.broadcasted_iota(jnp.int32, (tm, 1), 0) + 1).astype(F32)
        parts = [u]
        for g, w in enumerate(POOL_WINDOWS):
            cols = slice(g * POOL_GROUP, (g + 1) * POOL_GROUP)
            vg = v[:, cols]
            tot = vg
            for j in range(1, w):
                tot = tot + vbuf[POOL_HALO - j:POOL_HALO - j + tm, cols]
            pooled = tot / jnp.minimum(pos1, float(w)) - vg
            pooled_b = pooled.astype(BF16)
            pooled_ref[:, cols] = pooled_b
            parts.append(_dot(pooled_b, pw_ref[g]) * ps_ref[:, cols])
        mix = jnp.concatenate(parts, axis=-1).astype(BF16)
        x1_ref[...] = xt + _dot(mix, wout_ref[...])

        ubuf[0:CONV_HALO, :] = ubuf[tm:tm + CONV_HALO, :]
        vbuf[0:POOL_HALO, :] = vbuf[tm:tm + POOL_HALO, :]

    return _pcall(
        body,
        grid=(nt,),
        in_specs=[
            _row_spec(tm, D_MODEL),
            _const_spec((Z_WIDTH, D_MODEL)),
            _const_spec((D_MODEL, D_MODEL)),
            _const_spec((1, D_MODEL)),
            _const_spec((CONV_HALO, C_CONV)),
            _const_spec((1, C_CONV)),
            _const_spec((1, C_CONV)),
            _const_spec((1, C_CONV)),
            _const_spec((len(POOL_WINDOWS), POOL_GROUP, POOL_GROUP)),
            _const_spec((1, C_POOL)),
        ],
        out_specs=[
            _row_spec(tm, Z_WIDTH),
            _row_spec(tm, C_CONV),
            _row_spec(tm, C_POOL),
            _row_spec(tm, D_MODEL),
        ],
        out_shape=[
            jax.ShapeDtypeStruct((s, Z_WIDTH), F32),
            jax.ShapeDtypeStruct((s, C_CONV), F32),
            jax.ShapeDtypeStruct((s, C_POOL), BF16),
            jax.ShapeDtypeStruct((s, D_MODEL), F32),
        ],
        scratch_shapes=[
            pltpu.VMEM((tm + CONV_HALO, C_CONV), F32),
            pltpu.VMEM((tm + POOL_HALO, C_POOL), F32),
            pltpu.VMEM((7, tm + CONV_HALO, C_CONV), F32),
        ],
        name="fwd_mix",
        args=(x, w_in_t, w_out, g_mix, conv_w, conv_b, ln_g, ln_b, pool_w, pool_scale),
        carry=carry,
    )


def _ffn_up(x1, w_gu_t, g_ffn, tm, carry=None):
    s = x1.shape[0]
    nt = s // tm
    nc = D_FF // FF_CHUNK

    def body(x1_ref, wg_ref, wu_ref, gffn_ref, h2_ref, g_ref, u_ref, act_ref, h2_buf):
        c = pl.program_id(1)

        @pl.when(c == 0)
        def _():
            h, _ = _rms_fwd(x1_ref[...], gffn_ref[...])
            h2_buf[...] = h.astype(BF16)
            h2_ref[...] = h2_buf[...]

        h2 = h2_buf[...]
        g = _dot_nt(h2, wg_ref[...])
        u = _dot_nt(h2, wu_ref[...])
        g_ref[...] = g.astype(BF16)
        u_ref[...] = u.astype(BF16)
        act_ref[...] = (g * _sigmoid(g) * u).astype(BF16)

    tok = lambda width: pl.BlockSpec((tm, width), lambda i, c: (i, 0))
    chunk = pl.BlockSpec((tm, FF_CHUNK), lambda i, c: (i, c))
    return _pcall(
        body,
        grid=(nt, nc),
        in_specs=[
            tok(D_MODEL),
            pl.BlockSpec((FF_CHUNK, D_MODEL), lambda i, c: (c, 0)),
            pl.BlockSpec((FF_CHUNK, D_MODEL), lambda i, c: (c + nc, 0)),
            pl.BlockSpec((1, D_MODEL), lambda i, c: (0, 0)),
        ],
        out_specs=[tok(D_MODEL), chunk, chunk, chunk],
        out_shape=[
            jax.ShapeDtypeStruct((s, D_MODEL), BF16),
            jax.ShapeDtypeStruct((s, D_FF), BF16),
            jax.ShapeDtypeStruct((s, D_FF), BF16),
            jax.ShapeDtypeStruct((s, D_FF), BF16),
        ],
        scratch_shapes=[pltpu.VMEM((tm, D_MODEL), BF16)],
        name="ffn_up",
        args=(x1, w_gu_t, w_gu_t, g_ffn),
        carry=carry,
    )


def _ffn_down(x1, act, w_down, tm):
    s = x1.shape[0]

    def body(x1_ref, act_ref, wd_ref, x2_ref):
        x2_ref[...] = x1_ref[...] + _dot(act_ref[...], wd_ref[...])

    return pl.pallas_call(
        body,
        grid=(s // tm,),
        in_specs=[_row_spec(tm, D_MODEL), _row_spec(tm, D_FF), _const_spec((D_FF, D_MODEL))],
        out_specs=_row_spec(tm, D_MODEL),
        out_shape=jax.ShapeDtypeStruct((s, D_MODEL), F32),
        compiler_params=_params(("arbitrary",)),
        name="ffn_down",
    )(x1, act, w_down)


def _ple_loss(x2, p, tgt, w_pg, w_pu_t, g_gate, g_post, g_final, tm):
    s = x2.shape[0]
    nt = s // tm

    def body(x2_ref, p_ref, t_ref, wpg_ref, wpu_ref, gg_ref, gp_ref, gf_ref,
             dx2_ref, dx2b_ref, hg_ref, ds_ref, dpe_ref, pb_ref, stats_ref):
        i = pl.program_id(0)

        @pl.when(i == 0)
        def _():
            stats_ref[...] = jnp.zeros_like(stats_ref)

        x2 = x2_ref[...]
        hg, rg = _rms_fwd(x2, gg_ref[...])
        hg_b = hg.astype(BF16)
        hg_ref[...] = hg_b
        gate = _sigmoid(_dot(hg_b, wpg_ref[...]))
        pb = p_ref[...].astype(BF16)
        pb_ref[...] = pb
        pe = _dot_nt(pb, wpu_ref[...])
        e, rp = _rms_fwd(pe, gp_ref[...])
        x3 = x2 + gate * e
        y, r3 = _rms_fwd(x3, gf_ref[...])
        diff = y - t_ref[...]
        loss = 0.5 * jnp.sum(jnp.sum(diff * diff, axis=-1, keepdims=True), axis=0, keepdims=True) / D_MODEL
        dy = diff * (1.0 / D_MODEL)

        dx3, dgf = _rms_bwd(x3, r3, gf_ref[...], dy)
        dpe, dgp = _rms_bwd(pe, rp, gp_ref[...], dx3 * gate)
        dpe_ref[...] = dpe.astype(BF16)
        ds = (dx3 * e * gate * (1.0 - gate)).astype(BF16)
        ds_ref[...] = ds
        dhg = _dot_nt(ds, wpg_ref[...])
        dxg, dgg = _rms_bwd(x2, rg, gg_ref[...], dhg)
        dx2 = dx3 + dxg
        dx2_ref[...] = dx2
        dx2b_ref[...] = dx2.astype(BF16)

        stats_ref[0:1, :] += dgf
        stats_ref[1:2, :] += dgp
        stats_ref[2:3, :] += dgg
        stats_ref[3:4, :] += jnp.broadcast_to(loss, (1, D_MODEL))

    return pl.pallas_call(
        body,
        grid=(nt,),
        in_specs=[
            _row_spec(tm, D_MODEL),
            _row_spec(tm, D_PLE),
            _row_spec(tm, D_MODEL),
            _const_spec((D_MODEL, D_MODEL)),
            _const_spec((D_MODEL, D_PLE)),
            _const_spec((1, D_MODEL)),
            _const_spec((1, D_MODEL)),
            _const_spec((1, D_MODEL)),
        ],
        out_specs=[
            _row_spec(tm, D_MODEL),
            _row_spec(tm, D_MODEL),
            _row_spec(tm, D_MODEL),
            _row_spec(tm, D_MODEL),
            _row_spec(tm, D_MODEL),
            _row_spec(tm, D_PLE),
            _const_spec((8, D_MODEL)),
        ],
        out_shape=[
            jax.ShapeDtypeStruct((s, D_MODEL), F32),
            jax.ShapeDtypeStruct((s, D_MODEL), BF16),
            jax.ShapeDtypeStruct((s, D_MODEL), BF16),
            jax.ShapeDtypeStruct((s, D_MODEL), BF16),
            jax.ShapeDtypeStruct((s, D_MODEL), BF16),
            jax.ShapeDtypeStruct((s, D_PLE), BF16),
            jax.ShapeDtypeStruct((8, D_MODEL), F32),
        ],
        compiler_params=_params(("arbitrary",)),
        name="ple_loss",
    )(x2, p, tgt, w_pg, w_pu_t, g_gate, g_post, g_final)


def _ffn_bwd(dx2, dx2b, x1, g_sav, u_sav, w_gu_t, w_down, g_ffn, tm):
    s = x1.shape[0]
    nt = s // tm
    nc = D_FF // FF_CHUNK

    def body(dx2_ref, dx2b_ref, x1_ref, g_ref, u_ref, wg_ref, wu_ref, wd_ref, gffn_ref,
             dg_ref, du_ref, dx1_ref, dx1b_ref, stats_ref, acc):
        i = pl.program_id(0)
        c = pl.program_id(1)

        @pl.when(jnp.logical_and(i == 0, c == 0))
        def _():
            stats_ref[...] = jnp.zeros_like(stats_ref)

        @pl.when(c == 0)
        def _():
            acc[...] = jnp.zeros_like(acc)

        dact = _dot_nt(dx2b_ref[...], wd_ref[...])
        g = g_ref[...].astype(F32)
        u = u_ref[...].astype(F32)
        sg = _sigmoid(g)
        dg = (dact * u * sg * (1.0 + g * (1.0 - sg))).astype(BF16)
        du = (dact * g * sg).astype(BF16)
        dg_ref[...] = dg
        du_ref[...] = du
        acc[...] += _dot(dg, wg_ref[...]) + _dot(du, wu_ref[...])

        @pl.when(c == nc - 1)
        def _():
            x1 = x1_ref[...]
            r2 = lax.rsqrt(jnp.mean(x1 * x1, axis=-1, keepdims=True) + EPS)
            dxn, dgf = _rms_bwd(x1, r2, gffn_ref[...], acc[...])
            dx1 = dx2_ref[...] + dxn
            dx1_ref[...] = dx1
            dx1b_ref[...] = dx1.astype(BF16)
            stats_ref[0:1, :] += dgf

    tok = lambda width: pl.BlockSpec((tm, width), lambda i, c: (i, 0))
    chunk = pl.BlockSpec((tm, FF_CHUNK), lambda i, c: (i, c))
    return pl.pallas_call(
        body,
        grid=(nt, nc),
        in_specs=[
            tok(D_MODEL), tok(D_MODEL), tok(D_MODEL), chunk, chunk,
            pl.BlockSpec((FF_CHUNK, D_MODEL), lambda i, c: (c, 0)),
            pl.BlockSpec((FF_CHUNK, D_MODEL), lambda i, c: (c + nc, 0)),
            pl.BlockSpec((FF_CHUNK, D_MODEL), lambda i, c: (c, 0)),
            pl.BlockSpec((1, D_MODEL), lambda i, c: (0, 0)),
        ],
        out_specs=[chunk, chunk, tok(D_MODEL), tok(D_MODEL), pl.BlockSpec((8, D_MODEL), lambda i, c: (0, 0))],
        out_shape=[
            jax.ShapeDtypeStruct((s, D_FF), BF16),
            jax.ShapeDtypeStruct((s, D_FF), BF16),
            jax.ShapeDtypeStruct((s, D_MODEL), F32),
            jax.ShapeDtypeStruct((s, D_MODEL), BF16),
            jax.ShapeDtypeStruct((8, D_MODEL), F32),
        ],
        scratch_shapes=[pltpu.VMEM((tm, D_MODEL), F32)],
        compiler_params=_params(("arbitrary", "arbitrary")),
        name="ffn_bwd",
    )(dx2, dx2b, x1, g_sav, u_sav, w_gu_t, w_gu_t, w_down, g_ffn)


def _bwd_mix(dx1, dx1b, x, z, u1, pooled, w_in_t, w_out, g_mix, conv_w, ln_g, ln_b, pool_w, pool_scale, tm,
             carry=None):
    s = x.shape[0]
    nt = s // tm

    def body(dx1_ref, dx1b_ref, x_ref, z_ref, u1_ref, pooled_ref, win_ref, wout_ref, gmix_ref, cw_ref,
             lng_ref, lnb_ref, pw_ref, ps_ref,
             dx_ref, dz_ref, mix_ref, h1_ref, vec_ref, dcw_ref, dpw_ref, dubuf, dvbuf, u0buf, du0buf, dush):
        i = pl.program_id(0)
        tile = nt - 1 - i

        @pl.when(i == 0)
        def _():
            vec_ref[...] = jnp.zeros_like(vec_ref)
            dcw_ref[...] = jnp.zeros_like(dcw_ref)
            dpw_ref[...] = jnp.zeros_like(dpw_ref)
            dubuf[tm:tm + CONV_HALO, :] = jnp.zeros((CONV_HALO, C_CONV), F32)
            dvbuf[tm:tm + POOL_HALO, :] = jnp.zeros((POOL_HALO, C_POOL), F32)

        dmix = _dot_nt(dx1b_ref[...], wout_ref[...])
        du = dmix[:, :C_CONV]
        dq = dmix[:, C_CONV:]

        pos1 = (tile * tm + lax.broadcasted_iota(jnp.int32, (tm, 1), 0) + 1).astype(F32)
        q_parts = []
        dpooled_parts = []
        dps_rows = []
        for g, w in enumerate(POOL_WINDOWS):
            cols = slice(g * POOL_GROUP, (g + 1) * POOL_GROUP)
            pooled_b = pooled_ref[:, cols]
            mixed = _dot(pooled_b, pw_ref[g])
            dqg = dq[:, cols]
            dps_rows.append(jnp.sum(dqg * mixed, axis=0, keepdims=True))
            q_parts.append(mixed * ps_ref[:, cols])
            dmixed = (dqg * ps_ref[:, cols]).astype(BF16)
            dpw_ref[g] += _dot_tn(pooled_b, dmixed)
            dpooled = _dot_nt(dmixed, pw_ref[g])
            dpooled_parts.append(dpooled)
            dvbuf[0:tm, cols] = dpooled / jnp.minimum(pos1, float(w))
        vec_ref[4:5, 0:C_POOL] += jnp.concatenate(dps_rows, axis=-1)
        dv_parts = []
        for g, w in enumerate(POOL_WINDOWS):
            cols = slice(g * POOL_GROUP, (g + 1) * POOL_GROUP)
            tot = dvbuf[0:tm, cols]
            for j in range(1, w):
                tot = tot + dvbuf[j:j + tm, cols]
            dv_parts.append(tot - dpooled_parts[g])

        u1 = u1_ref[...]
        mu = jnp.mean(u1, axis=-1, keepdims=True)
        cen = u1 - mu
        rstd = lax.rsqrt(jnp.mean(cen * cen, axis=-1, keepdims=True) + EPS)
        xhat = cen * rstd
        u2 = xhat * lng_ref[...] + lnb_ref[...]
        sg2 = _sigmoid(u2)
        du2 = du * sg2 * (1.0 + u2 * (1.0 - sg2))
        vec_ref[1:2, 0:C_CONV] += jnp.sum(du2 * xhat, axis=0, keepdims=True)
        vec_ref[2:3, 0:C_CONV] += jnp.sum(du2, axis=0, keepdims=True)
        t1 = du2 * lng_ref[...]
        du1 = rstd * (t1 - jnp.mean(t1, axis=-1, keepdims=True)
                      - xhat * jnp.mean(t1 * xhat, axis=-1, keepdims=True))
        vec_ref[3:4, 0:C_CONV] += jnp.sum(du1, axis=0, keepdims=True)
        dubuf[0:tm, :] = du1

        zt = z_ref[...]
        a = zt[:, :C_CONV]
        sgb = _sigmoid(zt[:, C_CONV:2 * C_CONV])
        u0buf[...] = a * sgb

        _shifted_copies(dubuf, dush, tm)
        for rc in range(tm // ROW_CHUNK):
            r0 = rc * ROW_CHUNK
            acc = jnp.zeros((ROW_CHUNK, C_CONV), F32)
            for k in range(CONV_K):
                acc = acc + cw_ref[k:k + 1, :] * _rows_at(dubuf, dush, r0 + (CONV_K - 1) - k)
            du0buf[r0:r0 + ROW_CHUNK, :] = acc
        for k in range(CONV_K):
            acc = jnp.zeros((ROW_CHUNK, C_CONV), F32)
            for rc in range(tm // ROW_CHUNK):
                r0 = rc * ROW_CHUNK
                acc = acc + u0buf[r0:r0 + ROW_CHUNK, :] * _rows_at(dubuf, dush, r0 + (CONV_K - 1) - k)
            dcw_ref[k:k + 1, :] += jnp.sum(acc, axis=0, keepdims=True)
        du0 = du0buf[...]

        da = du0 * sgb
        db = du0 * a * sgb * (1.0 - sgb)
        dz = jnp.concatenate([da, db] + dv_parts, axis=-1).astype(BF16)
        dz_ref[...] = dz

        mix_ref[...] = jnp.concatenate([u2 * sg2] + q_parts, axis=-1).astype(BF16)

        xt = x_ref[...]
        h1, r1 = _rms_fwd(xt, gmix_ref[...])
        h1_ref[...] = h1.astype(BF16)
        dh1 = _dot(dz, win_ref[...])
        dxn, dgm = _rms_bwd(xt, r1, gmix_ref[...], dh1)
        dx_ref[...] = dx1_ref[...] + dxn
        vec_ref[0:1, :] += dgm

        dubuf[tm:tm + CONV_HALO, :] = dubuf[0:CONV_HALO, :]
        dvbuf[tm:tm + POOL_HALO, :] = dvbuf[0:POOL_HALO, :]

    rev = lambda width: pl.BlockSpec((tm, width), lambda i: (nt - 1 - i, 0))
    return _pcall(
        body,
        grid=(nt,),
        in_specs=[
            rev(D_MODEL), rev(D_MODEL), rev(D_MODEL), rev(Z_WIDTH), rev(C_CONV), rev(C_POOL),
            _const_spec((Z_WIDTH, D_MODEL)),
            _const_spec((D_MODEL, D_MODEL)),
            _const_spec((1, D_MODEL)),
            _const_spec((CONV_HALO, C_CONV)),
            _const_spec((1, C_CONV)),
            _const_spec((1, C_CONV)),
            _const_spec((len(POOL_WINDOWS), POOL_GROUP, POOL_GROUP)),
            _const_spec((1, C_POOL)),
        ],
        out_specs=[
            rev(D_MODEL), rev(Z_WIDTH), rev(D_MODEL), rev(D_MODEL),
            _const_spec((8, D_MODEL)),
            _const_spec((CONV_HALO, C_CONV)),
            _const_spec((len(POOL_WINDOWS), POOL_GROUP, POOL_GROUP)),
        ],
        out_shape=[
            jax.ShapeDtypeStruct((s, D_MODEL), F32),
            jax.ShapeDtypeStruct((s, Z_WIDTH), BF16),
            jax.ShapeDtypeStruct((s, D_MODEL), BF16),
            jax.ShapeDtypeStruct((s, D_MODEL), BF16),
            jax.ShapeDtypeStruct((8, D_MODEL), F32),
            jax.ShapeDtypeStruct((CONV_HALO, C_CONV), F32),
            jax.ShapeDtypeStruct((len(POOL_WINDOWS), POOL_GROUP, POOL_GROUP), F32),
        ],
        scratch_shapes=[
            pltpu.VMEM((tm + CONV_HALO, C_CONV), F32),
            pltpu.VMEM((tm + POOL_HALO, C_POOL), F32),
            pltpu.VMEM((tm, C_CONV), F32),
            pltpu.VMEM((tm, C_CONV), F32),
            pltpu.VMEM((7, tm + CONV_HALO, C_CONV), F32),
        ],
        name="bwd_mix",
        args=(dx1, dx1b, x, z, u1, pooled, w_in_t, w_out, g_mix, conv_w, ln_g, ln_b, pool_w, pool_scale),
        carry=carry,
    )


def _grad_matmul(a, b, bm, name, a2=None, carry=None):
    s, ma = a.shape
    nb = b.shape[1]
    na = ma // bm
    if a2 is None:
        def body(a_ref, b_ref, o_ref):
            o_ref[...] = _dot_tn(a_ref[...], b_ref[...]).astype(BF16)

        lhs_specs = [pl.BlockSpec((s, bm), lambda i: (0, i))]
        lhs = (a,)
        steps = na
    else:
        def body(a_ref, a2_ref, b_ref, o_ref):
            i = pl.program_id(0)

            @pl.when(i < na)
            def _():
                o_ref[...] = _dot_tn(a_ref[...], b_ref[...]).astype(BF16)

            @pl.when(i >= na)
            def _():
                o_ref[...] = _dot_tn(a2_ref[...], b_ref[...]).astype(BF16)

        lhs_specs = [pl.BlockSpec((s, bm), lambda i: (0, jnp.minimum(i, na - 1))),
                     pl.BlockSpec((s, bm), lambda i: (0, jnp.maximum(i - na, 0)))]
        lhs = (a, a2)
        steps = 2 * na

    outs, carried = _pcall(
        body,
        grid=(steps,),
        in_specs=lhs_specs + [pl.BlockSpec((s, nb), lambda i: (0, 0))],
        out_specs=[pl.BlockSpec((bm, nb), lambda i: (i, 0))],
        out_shape=[jax.ShapeDtypeStruct((steps * bm, nb), BF16)],
        scratch_shapes=[],
        name=name,
        args=lhs + (b,),
        carry=carry,
    )
    return outs[0], carried


def _gather_steps(src, dst, send_sems, recv_sems, local_sems):
    n = len(src)
    me, sib, chips = _place()
    c = me[2]
    started = []
    mine = []

    def copy(a, k, block, to, from_src=False):
        rows = dst[a].at[_block(*block)]
        return pltpu.make_async_remote_copy(
            src_ref=src[a] if from_src else rows, dst_ref=rows,
            send_sem=send_sems.at[a, k], recv_sem=recv_sems.at[a, k],
            device_id=to, device_id_type=MESH)

    def start():
        for a in range(n):
            cp = pltpu.make_async_copy(src[a], dst[a].at[_block(*me)], local_sems.at[a])
            cp.start()
            mine.append(cp)
            first = [copy(a, 0, me, sib, True)]
            first += [copy(a, 1 + j, me, (*chip, c), True) for j, chip in enumerate(chips)]
            for cp in first:
                cp.start()
            started.extend(first)

    def forward():
        for a in range(n):
            for j, chip in enumerate(chips):
                copy(a, 1 + j, (*chip, c), me).wait_recv()
                fwd = copy(a, 4 + j, (*chip, c), sib)
                fwd.start()
                started.append(fwd)

    def finish():
        for a in range(n):
            copy(a, 0, sib, me).wait_recv()
            for j, chip in enumerate(chips):
                copy(a, 4 + j, (*chip, 1 - c), me).wait_recv()
        for cp in started:
            cp.wait_send()
        for cp in mine:
            cp.wait()

    return start, forward, finish


def _gather_sems(n):
    return [pltpu.SemaphoreType.DMA((n, 7)), pltpu.SemaphoreType.DMA((n, 7)), pltpu.SemaphoreType.DMA((n,))]


def _all_gather(shards, name):
    n = len(shards)

    def body(*refs):
        start, forward, finish = _gather_steps(refs[:n], refs[n:2 * n], *refs[2 * n:])
        start()
        forward()
        finish()

    any_spec = pl.BlockSpec(memory_space=pl.ANY)
    return pl.pallas_call(
        body,
        in_specs=[any_spec] * n,
        out_specs=[any_spec] * n,
        out_shape=[jax.ShapeDtypeStruct((N_DEV,) + sh.shape, sh.dtype) for sh in shards],
        scratch_shapes=_gather_sems(n),
        name=name,
    )(*shards)


def _reduce_scatter(grads, small, name):
    n, ns = len(grads), len(small)
    shapes = [g.shape[1:] for g in grads]

    def body(*refs):
        g = refs[:n]
        out = refs[n + ns:2 * n + ns]
        scr = refs[2 * (n + ns):]
        own, loc, r1, r2 = scr[:n], scr[n:2 * n], scr[2 * n:3 * n], scr[3 * n:4 * n]
        load_sems, s1, q1, s2, q2 = scr[4 * n:4 * n + 5]
        gather_start, gather_forward, gather_finish = _gather_steps(
            refs[n:n + ns], refs[2 * n + ns:2 * (n + ns)], *scr[4 * n + 5:])
        me, sib, chips = _place()
        c = me[2]

        gather_start()
        loads = []
        sends = []
        for a in range(n):
            ld = [pltpu.make_async_copy(g[a].at[_block(*chip, c)], loc[a].at[j], load_sems.at[a, j])
                  for j, chip in enumerate(chips)]
            ld.append(pltpu.make_async_copy(g[a].at[_block(*me)], own[a], load_sems.at[a, 3]))
            for cp in ld:
                cp.start()
            loads.append(ld)
            blocks = [(*chip, 1 - c) for chip in chips] + [sib]
            for j, blk in enumerate(blocks):
                cp = pltpu.make_async_remote_copy(
                    src_ref=g[a].at[_block(*blk)], dst_ref=r1[a].at[j],
                    send_sem=s1.at[a, j], recv_sem=q1.at[a, j], device_id=sib, device_id_type=MESH)
                cp.start()
                sends.append(cp)

        def from_sibling(a, j):
            return pltpu.make_async_remote_copy(
                src_ref=r1[a].at[j], dst_ref=r1[a].at[j], send_sem=s1.at[a, j], recv_sem=q1.at[a, j],
                device_id=sib, device_id_type=MESH)

        def partial(a, j, chip):
            return pltpu.make_async_remote_copy(
                src_ref=loc[a].at[j], dst_ref=r2[a].at[j], send_sem=s2.at[a, j], recv_sem=q2.at[a, j],
                device_id=(*chip, c), device_id_type=MESH)

        gather_forward()
        for a in range(n):
            for j, chip in enumerate(chips):
                loads[a][j].wait()
                from_sibling(a, j).wait_recv()
                loc[a][j] = (loc[a][j].astype(F32) + r1[a][j].astype(F32)).astype(BF16)
                cp = partial(a, j, chip)
                cp.start()
                sends.append(cp)
        for a in range(n):
            loads[a][3].wait()
            from_sibling(a, 3).wait_recv()
            acc = own[a][...].astype(F32) + r1[a][3].astype(F32)
            for j, chip in enumerate(chips):
                partial(a, j, chip).wait_recv()
                acc = acc + r2[a][j].astype(F32)
            out[a][...] = acc
        for cp in sends:
            cp.wait_send()
        gather_finish()

    any_spec = pl.BlockSpec(memory_space=pl.ANY)
    vmem_spec = pl.BlockSpec(memory_space=pltpu.VMEM)
    res = pl.pallas_call(
        body,
        in_specs=[any_spec] * (n + ns),
        out_specs=[vmem_spec] * n + [any_spec] * ns,
        out_shape=([jax.ShapeDtypeStruct(sh, F32) for sh in shapes]
                   + [jax.ShapeDtypeStruct((N_DEV,) + sm.shape, sm.dtype) for sm in small]),
        scratch_shapes=(
            [pltpu.VMEM(sh, BF16) for sh in shapes]
            + [pltpu.VMEM((3,) + sh, BF16) for sh in shapes]
            + [pltpu.VMEM((4,) + sh, BF16) for sh in shapes]
            + [pltpu.VMEM((3,) + sh, BF16) for sh in shapes]
            + [pltpu.SemaphoreType.DMA((n, 4)),
               pltpu.SemaphoreType.DMA((n, 4)), pltpu.SemaphoreType.DMA((n, 4)),
               pltpu.SemaphoreType.DMA((n, 3)), pltpu.SemaphoreType.DMA((n, 3))]
            + _gather_sems(ns)
        ),
        compiler_params=pltpu.CompilerParams(vmem_limit_bytes=V7X_VMEM_LIMIT),
        name=name,
    )(*grads, *small)
    return res[:n], res[n:]


def _pair_add(grads, from_sib, blks, name):
    n = len(grads)

    def body(blk_ref, *refs):
        for a in range(n):
            refs[2 * n + a][...] = (refs[a][...].astype(F32) + refs[n + a][...].astype(F32)).astype(BF16)

    mine = [pl.BlockSpec((None,) + g.shape[1:], lambda j, b: (b[j], 0, 0)) for g in grads]
    same = [pl.BlockSpec((None,) + g.shape[1:], lambda j, b: (j, 0, 0)) for g in grads]
    return pl.pallas_call(
        body,
        grid_spec=pltpu.PrefetchScalarGridSpec(
            num_scalar_prefetch=1, grid=(4,), in_specs=mine + same, out_specs=same),
        out_shape=[jax.ShapeDtypeStruct((4,) + g.shape[1:], BF16) for g in grads],
        compiler_params=_params(("arbitrary",)),
        name=name,
    )(blks, *grads, *from_sib)


def _chip_sum(parts, from_chips, name):
    n = len(parts)

    def body(*refs):
        for a in range(n):
            acc = refs[a][...].astype(F32)
            for j in range(3):
                acc = acc + refs[n + a][j].astype(F32)
            refs[2 * n + a][...] = acc

    half = [p.shape[1] // 2 for p in parts]
    return pl.pallas_call(
        body,
        grid=(2,),
        in_specs=([pl.BlockSpec((None, h, p.shape[2]), lambda i: (3, i, 0)) for p, h in zip(parts, half)]
                  + [pl.BlockSpec((3, h, p.shape[2]), lambda i: (0, i, 0)) for p, h in zip(parts, half)]),
        out_specs=[pl.BlockSpec((h, p.shape[2]), lambda i: (i, 0)) for p, h in zip(parts, half)],
        out_shape=[jax.ShapeDtypeStruct(p.shape[1:], F32) for p in parts],
        compiler_params=_params(("arbitrary",)),
        name=name,
    )(*parts, *from_chips)


def _adam_math(w, g, m, v):
    nm = ADAM_B1 * m + (1.0 - ADAM_B1) * g
    nv = ADAM_B2 * v + (1.0 - ADAM_B2) * (g * g)
    m_hat = nm / (1.0 - ADAM_B1 ** ADAM_STEP)
    v_hat = nv / (1.0 - ADAM_B2 ** ADAM_STEP)
    return -ADAM_LR * (m_hat / (jnp.sqrt(v_hat) + ADAM_EPS) + ADAM_WD * w), nm, nv


def _sum_adamw(part, from_chips, w, m, v, name):
    r, c = w.shape
    half = r // 2

    def body(p_ref, f_ref, w_ref, m_ref, v_ref, g_ref, d_ref, nm_ref, nv_ref):
        g = p_ref[...].astype(F32)
        for j in range(3):
            g = g + f_ref[j].astype(F32)
        g_ref[...] = g
        d_ref[...], nm_ref[...], nv_ref[...] = _adam_math(w_ref[...], g, m_ref[...], v_ref[...])

    spec = pl.BlockSpec((half, c), lambda i: (i, 0))
    return pl.pallas_call(
        body,
        grid=(2,),
        in_specs=[pl.BlockSpec((None, half, c), lambda i: (3, i, 0)),
                  pl.BlockSpec((3, half, c), lambda i: (0, i, 0)), spec, spec, spec],
        out_specs=[spec] * 4,
        out_shape=[jax.ShapeDtypeStruct((r, c), F32)] * 4,
        compiler_params=_params(("arbitrary",)),
        name=name,
    )(part, from_chips, w, m, v)


def _small_update(gathered, layout, params, name):
    ng, npar = len(gathered), len(params)

    def body(*refs):
        parts = refs[:ng]
        prm = refs[ng:ng + 3 * npar]
        tot_refs = refs[ng + 3 * npar:2 * ng + 3 * npar]
        out = refs[2 * ng + 3 * npar:]
        tots = []
        for a in range(ng):
            acc = parts[a][0]
            for d in range(1, N_DEV):
                acc = acc + parts[a][d]
            tot_refs[a][...] = acc
            tots.append(acc)
        for i, (a, row, width) in enumerate(layout):
            g = tots[a] if row is None else tots[a][row:row + 1, :width]
            delta, nm, nv = _adam_math(prm[3 * i][...], g, prm[3 * i + 1][...], prm[3 * i + 2][...])
            out[4 * i][...] = g
            out[4 * i + 1][...] = delta
            out[4 * i + 2][...] = nm
            out[4 * i + 3][...] = nv

    flat = [t for prm in params for t in prm]
    res = pl.pallas_call(
        body,
        out_shape=([jax.ShapeDtypeStruct(g.shape[1:], F32) for g in gathered]
                   + [jax.ShapeDtypeStruct(prm[0].shape, F32) for prm in params for _ in range(4)]),
        compiler_params=pltpu.CompilerParams(vmem_limit_bytes=V7X_VMEM_LIMIT),
        name=name,
    )(*gathered, *flat)
    return res[:ng], [tuple(res[ng + 4 * i:ng + 4 * i + 4]) for i in range(npar)]


def _adamw(w, g, m, v, name):
    rows, cols = w.shape
    br = rows
    for cand in (512, 256, 128):
        if rows % cand == 0 and rows > cand:
            br = cand
            break

    def body(w_ref, g_ref, m_ref, v_ref, d_ref, nm_ref, nv_ref):
        d_ref[...], nm_ref[...], nv_ref[...] = _adam_math(w_ref[...], g_ref[...], m_ref[...], v_ref[...])

    spec = pl.BlockSpec((br, cols), lambda i: (i, 0))
    shape = jax.ShapeDtypeStruct((rows, cols), F32)
    return pl.pallas_call(
        body,
        grid=(rows // br,),
        in_specs=[spec] * 4,
        out_specs=[spec] * 3,
        out_shape=[shape] * 3,
        compiler_params=_params(("arbitrary",)),
        name=name,
    )(w, g, m, v)


def _by_device(full):
    return full.reshape(N_DEV, full.shape[0] // N_DEV, full.shape[1])


def kernel(x, p, g_mix, w_in, conv_w, conv_b, ln_g, ln_b, pool_w, pool_scale, w_out, g_ffn, w_gate_up, w_down, g_ple_gate, w_ple_gate, w_ple_up, g_ple_post, g_final, loss_target, m_g_mix, m_w_in, m_conv_w, m_conv_b, m_ln_g, m_ln_b, m_pool_w, m_pool_scale, m_w_out, m_g_ffn, m_w_gate_up, m_w_down, m_g_ple_gate, m_w_ple_gate, m_w_ple_up, m_g_ple_post, m_g_final, v_g_mix, v_w_in, v_conv_w, v_conv_b, v_ln_g, v_ln_b, v_pool_w, v_pool_scale, v_w_out, v_g_ffn, v_w_gate_up, v_w_down, v_g_ple_gate, v_w_ple_gate, v_w_ple_up, v_g_ple_post, v_g_final):
    seq = x.shape[1]
    xs = x[0]
    ps = p[0, 0]
    tgt = loss_target[0]
    ax, ay, ac = lax.axis_index("x"), lax.axis_index("y"), lax.axis_index("c")
    me = _block(ax, ay, ac)
    blks = jnp.stack([_block(1 - ax, ay, ac), _block(ax, 1 - ay, ac), _block(1 - ax, 1 - ay, ac), me]).astype(jnp.int32)
    rows = lambda gth: gth.reshape((-1,) + gth.shape[2:])

    w_in_t, w_out_f, conv_w_t = [rows(gth) for gth in _all_gather([
        w_in[0].T.astype(BF16),
        w_out[0].astype(BF16),
        jnp.pad(conv_w[0].T, ((0, 0), (0, CONV_HALO - CONV_K))),
    ], "gather_first")]
    conv_w_f = conv_w_t.T
    pool_w_b = pool_w[0].astype(BF16)

    carry = _Carry()
    _carry_gather(carry, [w_gate_up[0].T.astype(BF16)], 12)
    (z, u1, pooled, x1), (w_gu_all,) = _fwd_mix(
        xs, w_in_t, w_out_f, g_mix, conv_w_f, conv_b, ln_g, ln_b, pool_w_b, pool_scale, min(256, seq), carry)
    w_gu_t = rows(w_gu_all)

    carry = _Carry()
    _carry_gather(carry, [w_down[0].astype(BF16),
                          w_ple_gate[0].astype(BF16),
                          w_ple_up[0].T.astype(BF16)], 12)
    (h2, g_sav, u_sav, act), late = _ffn_up(x1, w_gu_t, g_ffn, min(512, seq), carry)
    w_down_f, w_pg_f, w_pu_t = [rows(gth) for gth in late]
    x2 = _ffn_down(x1, act, w_down_f, min(512, seq))
    dx2, dx2b, hg, ds, dpe, pb, stats_ple = _ple_loss(x2, ps, tgt, w_pg_f, w_pu_t, g_ple_gate, g_ple_post,
                                                      g_final.reshape(1, D_MODEL), min(256, seq))
    dg, du, dx1, dx1b, stats_ffn = _ffn_bwd(dx2, dx2b, x1, g_sav, u_sav, w_gu_t, w_down_f, g_ffn, min(512, seq))

    d_w_pg, _ = _grad_matmul(hg, ds, 256, "grad_w_ple_gate")
    d_w_pu_t, _ = _grad_matmul(dpe, pb, 256, "grad_w_ple_up")
    d_w_gu_t, _ = _grad_matmul(dg, h2, 256, "grad_w_gate_up", a2=du)
    early = [_by_device(d_w_pg), _by_device(d_w_pu_t), _by_device(d_w_gu_t)]
    carry = _Carry()
    _carry_pair(carry, early)
    d_w_down, from_sib = _grad_matmul(act, dx2b, 256, "grad_w_down", carry=carry)
    early_parts = _pair_add(early, from_sib, blks, "pair_add_early")

    carry = _Carry()
    _carry_chip(carry, early_parts)
    _carry_pair(carry, [_by_device(d_w_down)])
    (dx, dz, mix, h1, vec_mix, dconv_w_part, dpool_w_part), carried = _bwd_mix(
        dx1, dx1b, xs, z, u1, pooled, w_in_t, w_out_f, g_mix, conv_w_f, ln_g, ln_b, pool_w_b, pool_scale,
        min(256, seq), carry)
    early_chips, down_sib = carried[:3], carried[3:]
    down_parts = _pair_add([_by_device(d_w_down)], down_sib, blks, "pair_add_down")

    carry = _Carry()
    _carry_chip(carry, down_parts)
    d_w_in_t, down_chips = _grad_matmul(dz, h1, 256, "grad_w_in", carry=carry)
    d_w_out, _ = _grad_matmul(mix, dx1b, 256, "grad_w_out")
    (gr_w_in_t, gr_w_out), small = _reduce_scatter(
        [_by_device(d_w_in_t), _by_device(d_w_out)],
        [vec_mix, stats_ple, stats_ffn, dconv_w_part, dpool_w_part], "scatter_last")
    (gr_w_pu_t,) = _chip_sum(early_parts[1:2], early_chips[1:2], "chip_sum")

    vec_names = ["g_mix", "ln_g", "ln_b", "conv_b", "pool_scale", "g_final", "g_ple_post", "g_ple_gate", "g_ffn",
                 "pool_w"]
    layout = [(0, 0, D_MODEL), (0, 1, C_CONV), (0, 2, C_CONV), (0, 3, C_CONV), (0, 4, C_POOL),
              (1, 0, D_MODEL), (1, 1, D_MODEL), (1, 2, D_MODEL), (2, 0, D_MODEL), (4, None, None)]
    as_row = lambda t: t.reshape(1, D_MODEL)
    params = [(g_mix, m_g_mix, v_g_mix), (ln_g, m_ln_g, v_ln_g), (ln_b, m_ln_b, v_ln_b),
              (conv_b, m_conv_b, v_conv_b), (pool_scale, m_pool_scale, v_pool_scale),
              (as_row(g_final), as_row(m_g_final), as_row(v_g_final)),
              (g_ple_post, m_g_ple_post, v_g_ple_post), (g_ple_gate, m_g_ple_gate, v_g_ple_gate),
              (g_ffn, m_g_ffn, v_g_ffn), (pool_w[0], m_pool_w[0], v_pool_w[0])]
    tots, small_upd = _small_update(small, layout, params, "small_update")
    loss = tots[1][3, 0]
    upd = {}
    for nm, res, prm in zip(vec_names, small_upd, [g_mix, ln_g, ln_b, conv_b, pool_scale, g_final, g_ple_post,
                                                    g_ple_gate, g_ffn, pool_w]):
        upd[nm] = tuple(t.reshape(prm.shape) for t in res)
    gr_conv_w = lax.dynamic_slice_in_dim(tots[3][:CONV_K], me * (C_CONV // N_DEV), C_CONV // N_DEV, axis=1)

    fused = [
        ("w_ple_gate", early_parts[0], early_chips[0], w_ple_gate[0], m_w_ple_gate[0], v_w_ple_gate[0], False),
        ("w_gate_up", early_parts[2], early_chips[2], w_gate_up[0].T, m_w_gate_up[0].T, v_w_gate_up[0].T, True),
        ("w_down", down_parts[0], down_chips[0], w_down[0], m_w_down[0], v_w_down[0], False),
    ]
    for nm, part, chips3, w_, m_, v_, transposed in fused:
        res = _sum_adamw(part, chips3, w_, m_, v_, "adamw_" + nm)
        upd[nm] = tuple((t.T if transposed else t)[None] for t in res)
    plain = [
        ("w_in", w_in[0].T, gr_w_in_t, m_w_in[0].T, v_w_in[0].T, True),
        ("w_out", w_out[0], gr_w_out, m_w_out[0], v_w_out[0], False),
        ("w_ple_up", w_ple_up[0], gr_w_pu_t.T, m_w_ple_up[0], v_w_ple_up[0], False),
        ("conv_w", conv_w[0], gr_conv_w, m_conv_w[0], v_conv_w[0], False),
    ]
    for nm, w_, g_, m_, v_, transposed in plain:
        res = (g_,) + tuple(_adamw(w_, g_, m_, v_, "adamw_" + nm))
        upd[nm] = tuple((t.T if transposed else t)[None] for t in res)

    order = ["g_mix", "w_in", "conv_w", "conv_b", "ln_g", "ln_b", "pool_w", "pool_scale", "w_out", "g_ffn",
             "w_gate_up", "w_down", "g_ple_gate", "w_ple_gate", "w_ple_up", "g_ple_post", "g_final"]
    outs = [loss, dx[None]]
    for k in range(4):
        outs += [upd[nm][k] for nm in order]
    return tuple(outs)
```

```python
import functools

import jax
import jax.numpy as jnp
from jax import lax
from jax.experimental import pallas as pl
from jax.experimental.pallas import tpu as pltpu

D_MODEL = 1024
C_CONV = 512
C_POOL = 512
Z_WIDTH = 2 * C_CONV + C_POOL
POOL_WINDOWS = (2, 4, 8, 16)
POOL_GROUP = 128
CONV_K = 31
D_FF = 2816
D_PLE = 256
EPS = 1e-6
N_DEV = 8

ADAM_LR = 0.001
ADAM_B1 = 0.9
ADAM_B2 = 0.999
ADAM_EPS = 1e-08
ADAM_WD = 0.01
ADAM_STEP = 10

CONV_HALO = 32
POOL_HALO = 16
ROW_CHUNK = 32
V7X_VMEM_LIMIT = 56 * 1024 * 1024
FF_CHUNK = D_FF // 2

BF16 = jnp.bfloat16
F32 = jnp.float32
MESH = pl.DeviceIdType.MESH


def _dot(a, b):
    return lax.dot_general(a, b, (((1,), (0,)), ((), ())), preferred_element_type=F32)


def _dot_nt(a, b):
    return lax.dot_general(a, b, (((1,), (1,)), ((), ())), preferred_element_type=F32)


def _dot_tn(a, b):
    return lax.dot_general(a, b, (((0,), (0,)), ((), ())), preferred_element_type=F32)


def _rms_fwd(x, g):
    r = lax.rsqrt(jnp.mean(x * x, axis=-1, keepdims=True) + EPS)
    return x * r * g, r


def _rms_bwd(x, r, g, dy):
    xr = x * r
    dg = jnp.sum(dy * xr, axis=0, keepdims=True)
    dyg = dy * g
    dx = r * (dyg - xr * jnp.mean(dyg * xr, axis=-1, keepdims=True))
    return dx, dg


def _sigmoid(x):
    return jax.nn.sigmoid(x)


def _params(sem=None):
    return pltpu.CompilerParams(dimension_semantics=sem, vmem_limit_bytes=V7X_VMEM_LIMIT)


def _place():
    x, y, c = lax.axis_index("x"), lax.axis_index("y"), lax.axis_index("c")
    chips = [(1 - x, y), (x, 1 - y), (1 - x, 1 - y)]
    return (x, y, c), (x, y, 1 - c), chips


def _block(px, py, pc):
    return 4 * px + 2 * py + pc


class _Carry:
    def __init__(self):
        self.inputs = []
        self.out_shapes = []
        self.copies = []
        self.locals = []

    def add_input(self, arr):
        self.inputs.append(arr)
        return len(self.inputs) - 1

    def add_output(self, shape, dtype):
        self.out_shapes.append(jax.ShapeDtypeStruct(shape, dtype))
        return len(self.out_shapes) - 1

    def local(self, src_idx, dst_idx, dst_blk):
        self.locals.append((src_idx, dst_idx, dst_blk))

    def copy(self, src, dst_idx, dst_blk, got_blk, peer, step=0, after=()):
        self.copies.append(dict(src=src, dst_idx=dst_idx, dst_blk=dst_blk, got_blk=got_blk, peer=peer, step=step,
                                after=tuple(after)))
        return len(self.copies) - 1

    def sem_shapes(self):
        return [pltpu.SemaphoreType.DMA((max(1, len(self.copies)),)),
                pltpu.SemaphoreType.DMA((max(1, len(self.copies)),)),
                pltpu.SemaphoreType.DMA((max(1, len(self.locals)),))]

    @staticmethod
    def _view(ref, where):
        if isinstance(where, tuple):
            blk, row0, nrows = where
            return ref.at[blk, pl.ds(row0, nrows)]
        return ref.at[where]

    def _desc(self, k, ins, outs, sems, place):
        cp = self.copies[k]
        me, sib, chips = place
        kind, idx, blk = cp["src"]
        src = (ins if kind == "in" else outs)[idx]
        if blk is not None:
            src = self._view(src, blk(*place))
        to = sib if cp["peer"] == "sib" else (*chips[cp["peer"]], me[2])
        return pltpu.make_async_remote_copy(
            src_ref=src, dst_ref=self._view(outs[cp["dst_idx"]], cp["dst_blk"](*place)),
            send_sem=sems[0].at[k], recv_sem=sems[1].at[k], device_id=to, device_id_type=MESH)

    def _arrival(self, k, outs, sems, place):
        cp = self.copies[k]
        got = self._view(outs[cp["dst_idx"]], cp["got_blk"](*place))
        return pltpu.make_async_remote_copy(
            src_ref=got, dst_ref=got, send_sem=sems[0].at[k], recv_sem=sems[1].at[k],
            device_id=place[0], device_id_type=MESH)

    def _local(self, n, ins, outs, sems, place):
        src_idx, dst_idx, blk = self.locals[n]
        return pltpu.make_async_copy(ins[src_idx], outs[dst_idx].at[blk(*place)], sems[2].at[n])

    def starts(self, step, nsteps, ins, outs, sems):
        place = _place()
        self._waited = set()
        for s in sorted({0} | {cp["step"] for cp in self.copies}):
            ks = [k for k, cp in enumerate(self.copies) if cp["step"] == s]

            @pl.when(step == min(s, nsteps - 1))
            def _(s=s, ks=ks):
                if s == 0:
                    for n in range(len(self.locals)):
                        self._local(n, ins, outs, sems, place).start()
                for k in ks:
                    for a in self.copies[k]["after"]:
                        if a not in self._waited:
                            self._arrival(a, outs, sems, place).wait_recv()
                            self._waited.add(a)
                    self._desc(k, ins, outs, sems, place).start()

    def finish(self, step, nsteps, ins, outs, sems):
        place = _place()

        @pl.when(step == nsteps - 1)
        def _():
            for k in range(len(self.copies)):
                if k not in self._waited:
                    self._arrival(k, outs, sems, place).wait_recv()
            for k in range(len(self.copies)):
                self._desc(k, ins, outs, sems, place).wait_send()
            for n in range(len(self.locals)):
                self._local(n, ins, outs, sems, place).wait()


def _const_blk(j):
    return lambda me, sib, chips: j


def _carry_gather(carry, shards, relay_step, last_step):
    outs = []
    for sh in shards:
        i = carry.add_input(sh)
        o = carry.add_output((N_DEV,) + sh.shape, sh.dtype)
        half = sh.shape[0] // 2
        tile = 16 if sh.dtype == BF16 else 8
        split = half % tile == 0
        rows = [(0, half), (half, sh.shape[0] - half)] if split else [(0, sh.shape[0]), None]

        def whole(j, core):
            return lambda me, sib, chips, j=j, core=core: _block(*chips[j], me[2] if core == 0 else 1 - me[2])

        def part(j, core, h, rows=rows):
            return lambda me, sib, chips: (_block(*chips[j], me[2] if core == 0 else 1 - me[2]),) + rows[h]

        mine = lambda me, sib, chips: _block(*me)
        carry.local(i, o, mine)
        carry.copy(("in", i, None), o, mine, lambda me, sib, chips: _block(*sib), "sib")
        near = [carry.copy(("in", i, None), o, mine, whole(j, 0), j) for j in range(2)]
        for j in range(2):
            carry.copy(("out", o, whole(j, 0)), o, whole(j, 0), whole(j, 1), "sib", step=relay_step, after=(near[j],))
        for j in range(2):
            if rows[j] is None:
                continue
            far = carry.copy(("out", o, part(j, 0, j)), o, part(j, 0, j), part(2, 0, j), 1 - j,
                             step=relay_step, after=(near[j],))
            carry.copy(("out", o, part(2, 0, j)), o, part(2, 0, j), part(2, 1, j), "sib", step=last_step, after=(far,))
        outs.append(o)
    return outs


def _carry_pair(carry, grads):
    outs = []
    for g in grads:
        i = carry.add_input(g)
        o = carry.add_output((4,) + g.shape[1:], g.dtype)
        for j in range(4):
            if j < 3:
                blk = lambda me, sib, chips, j=j: _block(*chips[j], 1 - me[2])
            else:
                blk = lambda me, sib, chips: _block(*sib)
            carry.copy(("in", i, blk), o, _const_blk(j), _const_blk(j), "sib")
        outs.append(o)
    return outs


def _carry_chip(carry, parts):
    outs = []
    for p in parts:
        i = carry.add_input(p)
        o = carry.add_output((3,) + p.shape[1:], p.dtype)
        for j in range(3):
            carry.copy(("in", i, _const_blk(j)), o, _const_blk(j), _const_blk(j), j)
        outs.append(o)
    return outs


def _pcall(body, *, grid, in_specs, out_specs, out_shape, scratch_shapes, name, args, carry=None):
    sem = ("arbitrary",) * len(grid)
    if carry is None:
        res = pl.pallas_call(body, grid=grid, in_specs=in_specs, out_specs=out_specs, out_shape=out_shape,
                             scratch_shapes=scratch_shapes, compiler_params=_params(sem), name=name)(*args)
        return list(res), []
    n_in, n_out, n_scr = len(in_specs), len(out_specs), len(scratch_shapes)
    c_in, c_out = len(carry.inputs), len(carry.out_shapes)
    nsteps = 1
    for extent in grid:
        nsteps *= extent

    def wrapped(*refs):
        ins = refs[:n_in]
        cins = refs[n_in:n_in + c_in]
        o0 = n_in + c_in
        outs = refs[o0:o0 + n_out]
        couts = refs[o0 + n_out:o0 + n_out + c_out]
        s0 = o0 + n_out + c_out
        scr = refs[s0:s0 + n_scr]
        sems = refs[s0 + n_scr:]
        step = pl.program_id(0)
        for d in range(1, len(grid)):
            step = step * grid[d] + pl.program_id(d)
        carry.starts(step, nsteps, cins, couts, sems)
        body(*ins, *outs, *scr)
        carry.finish(step, nsteps, cins, couts, sems)

    any_spec = pl.BlockSpec(memory_space=pl.ANY)
    res = pl.pallas_call(
        wrapped, grid=grid,
        in_specs=list(in_specs) + [any_spec] * c_in,
        out_specs=list(out_specs) + [any_spec] * c_out,
        out_shape=list(out_shape) + carry.out_shapes,
        scratch_shapes=list(scratch_shapes) + carry.sem_shapes(),
        compiler_params=_params(sem), name=name)(*args, *carry.inputs)
    return list(res[:n_out]), list(res[n_out:])


def _shifted_copies(buf, shifted, tm):
    span = tm + CONV_HALO - 8
    for r in range(1, 8):
        shifted[r - 1, 0:span, :] = buf[r:r + span, :]


def _rows_at(buf, shifted, start):
    aligned, r = (start // 8) * 8, start % 8
    if r == 0:
        return buf[aligned:aligned + ROW_CHUNK, :]
    return shifted[r - 1, aligned:aligned + ROW_CHUNK, :]


def _row_spec(tm, width):
    return pl.BlockSpec((tm, width), lambda i: (i, 0))


def _const_spec(shape):
    return pl.BlockSpec(shape, lambda i: (0,) * len(shape))


def _fwd_mix(x, w_in_t, w_out, g_mix, conv_w, conv_b, ln_g, ln_b, pool_w, pool_scale, tm, carry=None):
    s = x.shape[0]
    nt = s // tm

    def body(x_ref, win_ref, wout_ref, gmix_ref, cw_ref, cb_ref, lng_ref, lnb_ref, pw_ref, ps_ref,
             z_ref, u1_ref, pooled_ref, x1_ref, ubuf, vbuf, ush):
        i = pl.program_id(0)

        @pl.when(i == 0)
        def _():
            ubuf[0:CONV_HALO, :] = jnp.zeros((CONV_HALO, C_CONV), F32)
            vbuf[0:POOL_HALO, :] = jnp.zeros((POOL_HALO, C_POOL), F32)

        xt = x_ref[...]
        h, _ = _rms_fwd(xt, gmix_ref[...])
        z = _dot_nt(h.astype(BF16), win_ref[...])
        z_ref[...] = z
        a = z[:, :C_CONV]
        b = z[:, C_CONV:2 * C_CONV]
        v = z[:, 2 * C_CONV:]
        ubuf[CONV_HALO:CONV_HALO + tm, :] = a * _sigmoid(b)
        vbuf[POOL_HALO:POOL_HALO + tm, :] = v

        _shifted_copies(ubuf, ush, tm)
        for rc in range(tm // ROW_CHUNK):
            base = rc * ROW_CHUNK + CONV_HALO - (CONV_K - 1)
            acc = jnp.broadcast_to(cb_ref[...], (ROW_CHUNK, C_CONV))
            for k in range(CONV_K):
                acc = acc + cw_ref[k:k + 1, :] * _rows_at(ubuf, ush, base + k)
            u1_ref[rc * ROW_CHUNK:(rc + 1) * ROW_CHUNK, :] = acc

        u1 = u1_ref[...]
        mu = jnp.mean(u1, axis=-1, keepdims=True)
        cen = u1 - mu
        rstd = lax.rsqrt(jnp.mean(cen * cen, axis=-1, keepdims=True) + EPS)
        u2 = cen * rstd * lng_ref[...] + lnb_ref[...]
        u = u2 * _sigmoid(u2)

        pos1 = (i * tm + lax.broadcasted_iota(jnp.int32, (tm, 1), 0) + 1).astype(F32)
        parts = [u]
        for g, w in enumerate(POOL_WINDOWS):
            cols = slice(g * POOL_GROUP, (g + 1) * POOL_GROUP)
            vg = v[:, cols]
            tot = vg
            for j in range(1, w):
                tot = tot + vbuf[POOL_HALO - j:POOL_HALO - j + tm, cols]
            pooled = tot / jnp.minimum(pos1, float(w)) - vg
            pooled_b = pooled.astype(BF16)
            pooled_ref[:, cols] = pooled_b
            parts.append(_dot(pooled_b, pw_ref[g]) * ps_ref[:, cols])
        mix = jnp.concatenate(parts, axis=-1).astype(BF16)
        x1_ref[...] = xt + _dot(mix, wout_ref[...])

        ubuf[0:CONV_HALO, :] = ubuf[tm:tm + CONV_HALO, :]
        vbuf[0:POOL_HALO, :] = vbuf[tm:tm + POOL_HALO, :]

    return _pcall(
        body,
        grid=(nt,),
        in_specs=[
            _row_spec(tm, D_MODEL),
            _const_spec((Z_WIDTH, D_MODEL)),
            _const_spec((D_MODEL, D_MODEL)),
            _const_spec((1, D_MODEL)),
            _const_spec((CONV_HALO, C_CONV)),
            _const_spec((1, C_CONV)),
            _const_spec((1, C_CONV)),
            _const_spec((1, C_CONV)),
            _const_spec((len(POOL_WINDOWS), POOL_GROUP, POOL_GROUP)),
            _const_spec((1, C_POOL)),
        ],
        out_specs=[
            _row_spec(tm, Z_WIDTH),
            _row_spec(tm, C_CONV),
            _row_spec(tm, C_POOL),
            _row_spec(tm, D_MODEL),
        ],
        out_shape=[
            jax.ShapeDtypeStruct((s, Z_WIDTH), F32),
            jax.ShapeDtypeStruct((s, C_CONV), F32),
            jax.ShapeDtypeStruct((s, C_POOL), BF16),
            jax.ShapeDtypeStruct((s, D_MODEL), F32),
        ],
        scratch_shapes=[
            pltpu.VMEM((tm + CONV_HALO, C_CONV), F32),
            pltpu.VMEM((tm + POOL_HALO, C_POOL), F32),
            pltpu.VMEM((7, tm + CONV_HALO, C_CONV), F32),
        ],
        name="fwd_mix",
        args=(x, w_in_t, w_out, g_mix, conv_w, conv_b, ln_g, ln_b, pool_w, pool_scale),
        carry=carry,
    )


def _ffn_up(x1, w_gu_t, g_ffn, tm, carry=None):
    s = x1.shape[0]
    nt = s // tm
    nc = D_FF // FF_CHUNK

    def body(x1_ref, wg_ref, wu_ref, gffn_ref, h2_ref, g_ref, u_ref, act_ref, h2_buf):
        c = pl.program_id(1)

        @pl.when(c == 0)
        def _():
            h, _ = _rms_fwd(x1_ref[...], gffn_ref[...])
            h2_buf[...] = h.astype(BF16)
            h2_ref[...] = h2_buf[...]

        h2 = h2_buf[...]
        g = _dot_nt(h2, wg_ref[...])
        u = _dot_nt(h2, wu_ref[...])
        g_ref[...] = g.astype(BF16)
        u_ref[...] = u.astype(BF16)
        act_ref[...] = (g * _sigmoid(g) * u).astype(BF16)

    tok = lambda width: pl.BlockSpec((tm, width), lambda i, c: (i, 0))
    chunk = pl.BlockSpec((tm, FF_CHUNK), lambda i, c: (i, c))
    return _pcall(
        body,
        grid=(nt, nc),
        in_specs=[
            tok(D_MODEL),
            pl.BlockSpec((FF_CHUNK, D_MODEL), lambda i, c: (c, 0)),
            pl.BlockSpec((FF_CHUNK, D_MODEL), lambda i, c: (c + nc, 0)),
            pl.BlockSpec((1, D_MODEL), lambda i, c: (0, 0)),
        ],
        out_specs=[tok(D_MODEL), chunk, chunk, chunk],
        out_shape=[
            jax.ShapeDtypeStruct((s, D_MODEL), BF16),
            jax.ShapeDtypeStruct((s, D_FF), BF16),
            jax.ShapeDtypeStruct((s, D_FF), BF16),
            jax.ShapeDtypeStruct((s, D_FF), BF16),
        ],
        scratch_shapes=[pltpu.VMEM((tm, D_MODEL), BF16)],
        name="ffn_up",
        args=(x1, w_gu_t, w_gu_t, g_ffn),
        carry=carry,
    )


def _ffn_down(x1, act, w_down, tm, carry=None):
    s = x1.shape[0]

    def body(x1_ref, act_ref, wd_ref, x2_ref):
        x2_ref[...] = x1_ref[...] + _dot(act_ref[...], wd_ref[...])

    return _pcall(
        body,
        grid=(s // tm,),
        in_specs=[_row_spec(tm, D_MODEL), _row_spec(tm, D_FF), _const_spec((D_FF, D_MODEL))],
        out_specs=[_row_spec(tm, D_MODEL)],
        out_shape=[jax.ShapeDtypeStruct((s, D_MODEL), F32)],
        scratch_shapes=[],
        name="ffn_down",
        args=(x1, act, w_down),
        carry=carry,
    )


def _ple_loss(x2, p, tgt, w_pg, w_pu_t, g_gate, g_post, g_final, tm):
    s = x2.shape[0]
    nt = s // tm

    def body(x2_ref, p_ref, t_ref, wpg_ref, wpu_ref, gg_ref, gp_ref, gf_ref,
             dx2_ref, dx2b_ref, hg_ref, ds_ref, dpe_ref, pb_ref, stats_ref):
        i = pl.program_id(0)

        @pl.when(i == 0)
        def _():
            stats_ref[...] = jnp.zeros_like(stats_ref)

        x2 = x2_ref[...]
        hg, rg = _rms_fwd(x2, gg_ref[...])
        hg_b = hg.astype(BF16)
        hg_ref[...] = hg_b
        gate = _sigmoid(_dot(hg_b, wpg_ref[...]))
        pb = p_ref[...].astype(BF16)
        pb_ref[...] = pb
        pe = _dot_nt(pb, wpu_ref[...])
        e, rp = _rms_fwd(pe, gp_ref[...])
        x3 = x2 + gate * e
        y, r3 = _rms_fwd(x3, gf_ref[...])
        diff = y - t_ref[...]
        loss = 0.5 * jnp.sum(jnp.sum(diff * diff, axis=-1, keepdims=True), axis=0, keepdims=True) / D_MODEL
        dy = diff * (1.0 / D_MODEL)

        dx3, dgf = _rms_bwd(x3, r3, gf_ref[...], dy)
        dpe, dgp = _rms_bwd(pe, rp, gp_ref[...], dx3 * gate)
        dpe_ref[...] = dpe.astype(BF16)
        ds = (dx3 * e * gate * (1.0 - gate)).astype(BF16)
        ds_ref[...] = ds
        dhg = _dot_nt(ds, wpg_ref[...])
        dxg, dgg = _rms_bwd(x2, rg, gg_ref[...], dhg)
        dx2 = dx3 + dxg
        dx2_ref[...] = dx2
        dx2b_ref[...] = dx2.astype(BF16)

        stats_ref[0:1, :] += dgf
        stats_ref[1:2, :] += dgp
        stats_ref[2:3, :] += dgg
        stats_ref[3:4, :] += jnp.broadcast_to(loss, (1, D_MODEL))

    return pl.pallas_call(
        body,
        grid=(nt,),
        in_specs=[
            _row_spec(tm, D_MODEL),
            _row_spec(tm, D_PLE),
            _row_spec(tm, D_MODEL),
            _const_spec((D_MODEL, D_MODEL)),
            _const_spec((D_MODEL, D_PLE)),
            _const_spec((1, D_MODEL)),
            _const_spec((1, D_MODEL)),
            _const_spec((1, D_MODEL)),
        ],
        out_specs=[
            _row_spec(tm, D_MODEL),
            _row_spec(tm, D_MODEL),
            _row_spec(tm, D_MODEL),
            _row_spec(tm, D_MODEL),
            _row_spec(tm, D_MODEL),
            _row_spec(tm, D_PLE),
            _const_spec((8, D_MODEL)),
        ],
        out_shape=[
            jax.ShapeDtypeStruct((s, D_MODEL), F32),
            jax.ShapeDtypeStruct((s, D_MODEL), BF16),
            jax.ShapeDtypeStruct((s, D_MODEL), BF16),
            jax.ShapeDtypeStruct((s, D_MODEL), BF16),
            jax.ShapeDtypeStruct((s, D_MODEL), BF16),
            jax.ShapeDtypeStruct((s, D_PLE), BF16),
            jax.ShapeDtypeStruct((8, D_MODEL), F32),
        ],
        compiler_params=_params(("arbitrary",)),
        name="ple_loss",
    )(x2, p, tgt, w_pg, w_pu_t, g_gate, g_post, g_final)


def _ffn_bwd(dx2, dx2b, x1, g_sav, u_sav, w_gu_t, w_down, g_ffn, tm):
    s = x1.shape[0]
    nt = s // tm
    nc = D_FF // FF_CHUNK

    def body(dx2_ref, dx2b_ref, x1_ref, g_ref, u_ref, wg_ref, wu_ref, wd_ref, gffn_ref,
             dg_ref, du_ref, dx1_ref, dx1b_ref, stats_ref, acc):
        i = pl.program_id(0)
        c = pl.program_id(1)

        @pl.when(jnp.logical_and(i == 0, c == 0))
        def _():
            stats_ref[...] = jnp.zeros_like(stats_ref)

        @pl.when(c == 0)
        def _():
            acc[...] = jnp.zeros_like(acc)

        dact = _dot_nt(dx2b_ref[...], wd_ref[...])
        g = g_ref[...].astype(F32)
        u = u_ref[...].astype(F32)
        sg = _sigmoid(g)
        dg = (dact * u * sg * (1.0 + g * (1.0 - sg))).astype(BF16)
        du = (dact * g * sg).astype(BF16)
        dg_ref[...] = dg
        du_ref[...] = du
        acc[...] += _dot(dg, wg_ref[...]) + _dot(du, wu_ref[...])

        @pl.when(c == nc - 1)
        def _():
            x1 = x1_ref[...]
            r2 = lax.rsqrt(jnp.mean(x1 * x1, axis=-1, keepdims=True) + EPS)
            dxn, dgf = _rms_bwd(x1, r2, gffn_ref[...], acc[...])
            dx1 = dx2_ref[...] + dxn
            dx1_ref[...] = dx1
            dx1b_ref[...] = dx1.astype(BF16)
            stats_ref[0:1, :] += dgf

    tok = lambda width: pl.BlockSpec((tm, width), lambda i, c: (i, 0))
    chunk = pl.BlockSpec((tm, FF_CHUNK), lambda i, c: (i, c))
    return pl.pallas_call(
        body,
        grid=(nt, nc),
        in_specs=[
            tok(D_MODEL), tok(D_MODEL), tok(D_MODEL), chunk, chunk,
            pl.BlockSpec((FF_CHUNK, D_MODEL), lambda i, c: (c, 0)),
            pl.BlockSpec((FF_CHUNK, D_MODEL), lambda i, c: (c + nc, 0)),
            pl.BlockSpec((FF_CHUNK, D_MODEL), lambda i, c: (c, 0)),
            pl.BlockSpec((1, D_MODEL), lambda i, c: (0, 0)),
        ],
        out_specs=[chunk, chunk, tok(D_MODEL), tok(D_MODEL), pl.BlockSpec((8, D_MODEL), lambda i, c: (0, 0))],
        out_shape=[
            jax.ShapeDtypeStruct((s, D_FF), BF16),
            jax.ShapeDtypeStruct((s, D_FF), BF16),
            jax.ShapeDtypeStruct((s, D_MODEL), F32),
            jax.ShapeDtypeStruct((s, D_MODEL), BF16),
            jax.ShapeDtypeStruct((8, D_MODEL), F32),
        ],
        scratch_shapes=[pltpu.VMEM((tm, D_MODEL), F32)],
        compiler_params=_params(("arbitrary", "arbitrary")),
        name="ffn_bwd",
    )(dx2, dx2b, x1, g_sav, u_sav, w_gu_t, w_gu_t, w_down, g_ffn)


def _bwd_mix(dx1, dx1b, x, z, u1, pooled, w_in_t, w_out, g_mix, conv_w, ln_g, ln_b, pool_w, pool_scale, tm,
             carry=None):
    s = x.shape[0]
    nt = s // tm

    def body(dx1_ref, dx1b_ref, x_ref, z_ref, u1_ref, pooled_ref, win_ref, wout_ref, gmix_ref, cw_ref,
             lng_ref, lnb_ref, pw_ref, ps_ref,
             dx_ref, dz_ref, mix_ref, h1_ref, vec_ref, dcw_ref, dpw_ref, dubuf, dvbuf, u0buf, du0buf, dush):
        i = pl.program_id(0)
        tile = nt - 1 - i

        @pl.when(i == 0)
        def _():
            vec_ref[...] = jnp.zeros_like(vec_ref)
            dcw_ref[...] = jnp.zeros_like(dcw_ref)
            dpw_ref[...] = jnp.zeros_like(dpw_ref)
            dubuf[tm:tm + CONV_HALO, :] = jnp.zeros((CONV_HALO, C_CONV), F32)
            dvbuf[tm:tm + POOL_HALO, :] = jnp.zeros((POOL_HALO, C_POOL), F32)

        dmix = _dot_nt(dx1b_ref[...], wout_ref[...])
        du = dmix[:, :C_CONV]
        dq = dmix[:, C_CONV:]

        pos1 = (tile * tm + lax.broadcasted_iota(jnp.int32, (tm, 1), 0) + 1).astype(F32)
        q_parts = []
        dpooled_parts = []
        dps_rows = []
        for g, w in enumerate(POOL_WINDOWS):
            cols = slice(g * POOL_GROUP, (g + 1) * POOL_GROUP)
            pooled_b = pooled_ref[:, cols]
            mixed = _dot(pooled_b, pw_ref[g])
            dqg = dq[:, cols]
            dps_rows.append(jnp.sum(dqg * mixed, axis=0, keepdims=True))
            q_parts.append(mixed * ps_ref[:, cols])
            dmixed = (dqg * ps_ref[:, cols]).astype(BF16)
            dpw_ref[g] += _dot_tn(pooled_b, dmixed)
            dpooled = _dot_nt(dmixed, pw_ref[g])
            dpooled_parts.append(dpooled)
            dvbuf[0:tm, cols] = dpooled / jnp.minimum(pos1, float(w))
        vec_ref[4:5, 0:C_POOL] += jnp.concatenate(dps_rows, axis=-1)
        dv_parts = []
        for g, w in enumerate(POOL_WINDOWS):
            cols = slice(g * POOL_GROUP, (g + 1) * POOL_GROUP)
            tot = dvbuf[0:tm, cols]
            for j in range(1, w):
                tot = tot + dvbuf[j:j + tm, cols]
            dv_parts.append(tot - dpooled_parts[g])

        u1 = u1_ref[...]
        mu = jnp.mean(u1, axis=-1, keepdims=True)
        cen = u1 - mu
        rstd = lax.rsqrt(jnp.mean(cen * cen, axis=-1, keepdims=True) + EPS)
        xhat = cen * rstd
        u2 = xhat * lng_ref[...] + lnb_ref[...]
        sg2 = _sigmoid(u2)
        du2 = du * sg2 * (1.0 + u2 * (1.0 - sg2))
        vec_ref[1:2, 0:C_CONV] += jnp.sum(du2 * xhat, axis=0, keepdims=True)
        vec_ref[2:3, 0:C_CONV] += jnp.sum(du2, axis=0, keepdims=True)
        t1 = du2 * lng_ref[...]
        du1 = rstd * (t1 - jnp.mean(t1, axis=-1, keepdims=True)
                      - xhat * jnp.mean(t1 * xhat, axis=-1, keepdims=True))
        vec_ref[3:4, 0:C_CONV] += jnp.sum(du1, axis=0, keepdims=True)
        dubuf[0:tm, :] = du1

        zt = z_ref[...]
        a = zt[:, :C_CONV]
        sgb = _sigmoid(zt[:, C_CONV:2 * C_CONV])
        u0buf[...] = a * sgb

        _shifted_copies(dubuf, dush, tm)
        for rc in range(tm // ROW_CHUNK):
            r0 = rc * ROW_CHUNK
            acc = jnp.zeros((ROW_CHUNK, C_CONV), F32)
            for k in range(CONV_K):
                acc = acc + cw_ref[k:k + 1, :] * _rows_at(dubuf, dush, r0 + (CONV_K - 1) - k)
            du0buf[r0:r0 + ROW_CHUNK, :] = acc
        for k in range(CONV_K):
            acc = jnp.zeros((ROW_CHUNK, C_CONV), F32)
            for rc in range(tm // ROW_CHUNK):
                r0 = rc * ROW_CHUNK
                acc = acc + u0buf[r0:r0 + ROW_CHUNK, :] * _rows_at(dubuf, dush, r0 + (CONV_K - 1) - k)
            dcw_ref[k:k + 1, :] += jnp.sum(acc, axis=0, keepdims=True)
        du0 = du0buf[...]

        da = du0 * sgb
        db = du0 * a * sgb * (1.0 - sgb)
        dz = jnp.concatenate([da, db] + dv_parts, axis=-1).astype(BF16)
        dz_ref[...] = dz

        mix_ref[...] = jnp.concatenate([u2 * sg2] + q_parts, axis=-1).astype(BF16)

        xt = x_ref[...]
        h1, r1 = _rms_fwd(xt, gmix_ref[...])
        h1_ref[...] = h1.astype(BF16)
        dh1 = _dot(dz, win_ref[...])
        dxn, dgm = _rms_bwd(xt, r1, gmix_ref[...], dh1)
        dx_ref[...] = dx1_ref[...] + dxn
        vec_ref[0:1, :] += dgm

        dubuf[tm:tm + CONV_HALO, :] = dubuf[0:CONV_HALO, :]
        dvbuf[tm:tm + POOL_HALO, :] = dvbuf[0:POOL_HALO, :]

    rev = lambda width: pl.BlockSpec((tm, width), lambda i: (nt - 1 - i, 0))
    return _pcall(
        body,
        grid=(nt,),
        in_specs=[
            rev(D_MODEL), rev(D_MODEL), rev(D_MODEL), rev(Z_WIDTH), rev(C_CONV), rev(C_POOL),
            _const_spec((Z_WIDTH, D_MODEL)),
            _const_spec((D_MODEL, D_MODEL)),
            _const_spec((1, D_MODEL)),
            _const_spec((CONV_HALO, C_CONV)),
            _const_spec((1, C_CONV)),
            _const_spec((1, C_CONV)),
            _const_spec((len(POOL_WINDOWS), POOL_GROUP, POOL_GROUP)),
            _const_spec((1, C_POOL)),
        ],
        out_specs=[
            rev(D_MODEL), rev(Z_WIDTH), rev(D_MODEL), rev(D_MODEL),
            _const_spec((8, D_MODEL)),
            _const_spec((CONV_HALO, C_CONV)),
            _const_spec((len(POOL_WINDOWS), POOL_GROUP, POOL_GROUP)),
        ],
        out_shape=[
            jax.ShapeDtypeStruct((s, D_MODEL), F32),
            jax.ShapeDtypeStruct((s, Z_WIDTH), BF16),
            jax.ShapeDtypeStruct((s, D_MODEL), BF16),
            jax.ShapeDtypeStruct((s, D_MODEL), BF16),
            jax.ShapeDtypeStruct((8, D_MODEL), F32),
            jax.ShapeDtypeStruct((CONV_HALO, C_CONV), F32),
            jax.ShapeDtypeStruct((len(POOL_WINDOWS), POOL_GROUP, POOL_GROUP), F32),
        ],
        scratch_shapes=[
            pltpu.VMEM((tm + CONV_HALO, C_CONV), F32),
            pltpu.VMEM((tm + POOL_HALO, C_POOL), F32),
            pltpu.VMEM((tm, C_CONV), F32),
            pltpu.VMEM((tm, C_CONV), F32),
            pltpu.VMEM((7, tm + CONV_HALO, C_CONV), F32),
        ],
        name="bwd_mix",
        args=(dx1, dx1b, x, z, u1, pooled, w_in_t, w_out, g_mix, conv_w, ln_g, ln_b, pool_w, pool_scale),
        carry=carry,
    )


def _grad_matmul(a, b, bm, name, a2=None, carry=None):
    s, ma = a.shape
    nb = b.shape[1]
    na = ma // bm
    if a2 is None:
        def body(a_ref, b_ref, o_ref):
            o_ref[...] = _dot_tn(a_ref[...], b_ref[...]).astype(BF16)

        lhs_specs = [pl.BlockSpec((s, bm), lambda i: (0, i))]
        lhs = (a,)
        steps = na
    else:
        def body(a_ref, a2_ref, b_ref, o_ref):
            i = pl.program_id(0)

            @pl.when(i < na)
            def _():
                o_ref[...] = _dot_tn(a_ref[...], b_ref[...]).astype(BF16)

            @pl.when(i >= na)
            def _():
                o_ref[...] = _dot_tn(a2_ref[...], b_ref[...]).astype(BF16)

        lhs_specs = [pl.BlockSpec((s, bm), lambda i: (0, jnp.minimum(i, na - 1))),
                     pl.BlockSpec((s, bm), lambda i: (0, jnp.maximum(i - na, 0)))]
        lhs = (a, a2)
        steps = 2 * na

    outs, carried = _pcall(
        body,
        grid=(steps,),
        in_specs=lhs_specs + [pl.BlockSpec((s, nb), lambda i: (0, 0))],
        out_specs=[pl.BlockSpec((bm, nb), lambda i: (i, 0))],
        out_shape=[jax.ShapeDtypeStruct((steps * bm, nb), BF16)],
        scratch_shapes=[],
        name=name,
        args=lhs + (b,),
        carry=carry,
    )
    return outs[0], carried


def _gather_steps(src, dst, send_sems, recv_sems, local_sems):
    n = len(src)
    me, sib, chips = _place()
    c = me[2]
    started = []
    mine = []

    def copy(a, k, block, to, from_src=False):
        rows = dst[a].at[_block(*block)]
        return pltpu.make_async_remote_copy(
            src_ref=src[a] if from_src else rows, dst_ref=rows,
            send_sem=send_sems.at[a, k], recv_sem=recv_sems.at[a, k],
            device_id=to, device_id_type=MESH)

    def start():
        for a in range(n):
            cp = pltpu.make_async_copy(src[a], dst[a].at[_block(*me)], local_sems.at[a])
            cp.start()
            mine.append(cp)
            first = [copy(a, 0, me, sib, True)]
            first += [copy(a, 1 + j, me, (*chip, c), True) for j, chip in enumerate(chips)]
            for cp in first:
                cp.start()
            started.extend(first)

    def forward():
        for a in range(n):
            for j, chip in enumerate(chips):
                copy(a, 1 + j, (*chip, c), me).wait_recv()
                fwd = copy(a, 4 + j, (*chip, c), sib)
                fwd.start()
                started.append(fwd)

    def finish():
        for a in range(n):
            copy(a, 0, sib, me).wait_recv()
            for j, chip in enumerate(chips):
                copy(a, 4 + j, (*chip, 1 - c), me).wait_recv()
        for cp in started:
            cp.wait_send()
        for cp in mine:
            cp.wait()

    return start, forward, finish


def _gather_sems(n):
    return [pltpu.SemaphoreType.DMA((n, 7)), pltpu.SemaphoreType.DMA((n, 7)), pltpu.SemaphoreType.DMA((n,))]


def _all_gather(shards, name):
    n = len(shards)
    carry = _Carry()
    _carry_gather(carry, shards, 1, 2)

    def body(*refs):
        step = jnp.int32(0)
        carry.starts(step, 1, refs[:n], refs[n:2 * n], refs[2 * n:])
        carry.finish(step, 1, refs[:n], refs[n:2 * n], refs[2 * n:])

    any_spec = pl.BlockSpec(memory_space=pl.ANY)
    return pl.pallas_call(
        body,
        in_specs=[any_spec] * n,
        out_specs=[any_spec] * n,
        out_shape=carry.out_shapes,
        scratch_shapes=carry.sem_shapes(),
        name=name,
    )(*shards)


def _reduce_scatter(grads, small, name):
    n, ns = len(grads), len(small)
    shapes = [g.shape[1:] for g in grads]

    def body(*refs):
        g = refs[:n]
        out = refs[n + ns:2 * n + ns]
        scr = refs[2 * (n + ns):]
        own, loc, r1, r2 = scr[:n], scr[n:2 * n], scr[2 * n:3 * n], scr[3 * n:4 * n]
        load_sems, s1, q1, s2, q2 = scr[4 * n:4 * n + 5]
        gather_start, gather_forward, gather_finish = _gather_steps(
            refs[n:n + ns], refs[2 * n + ns:2 * (n + ns)], *scr[4 * n + 5:])
        me, sib, chips = _place()
        c = me[2]

        gather_start()
        loads = []
        sends = []
        for a in range(n):
            ld = [pltpu.make_async_copy(g[a].at[_block(*chip, c)], loc[a].at[j], load_sems.at[a, j])
                  for j, chip in enumerate(chips)]
            ld.append(pltpu.make_async_copy(g[a].at[_block(*me)], own[a], load_sems.at[a, 3]))
            for cp in ld:
                cp.start()
            loads.append(ld)
            blocks = [(*chip, 1 - c) for chip in chips] + [sib]
            for j, blk in enumerate(blocks):
                cp = pltpu.make_async_remote_copy(
                    src_ref=g[a].at[_block(*blk)], dst_ref=r1[a].at[j],
                    send_sem=s1.at[a, j], recv_sem=q1.at[a, j], device_id=sib, device_id_type=MESH)
                cp.start()
                sends.append(cp)

        def from_sibling(a, j):
            return pltpu.make_async_remote_copy(
                src_ref=r1[a].at[j], dst_ref=r1[a].at[j], send_sem=s1.at[a, j], recv_sem=q1.at[a, j],
                device_id=sib, device_id_type=MESH)

        def partial(a, j, chip):
            return pltpu.make_async_remote_copy(
                src_ref=loc[a].at[j], dst_ref=r2[a].at[j], send_sem=s2.at[a, j], recv_sem=q2.at[a, j],
                device_id=(*chip, c), device_id_type=MESH)

        gather_forward()
        for a in range(n):
            for j, chip in enumerate(chips):
                loads[a][j].wait()
                from_sibling(a, j).wait_recv()
                loc[a][j] = (loc[a][j].astype(F32) + r1[a][j].astype(F32)).astype(BF16)
                cp = partial(a, j, chip)
                cp.start()
                sends.append(cp)
        for a in range(n):
            loads[a][3].wait()
            from_sibling(a, 3).wait_recv()
            acc = own[a][...].astype(F32) + r1[a][3].astype(F32)
            for j, chip in enumerate(chips):
                partial(a, j, chip).wait_recv()
                acc = acc + r2[a][j].astype(F32)
            out[a][...] = acc
        for cp in sends:
            cp.wait_send()
        gather_finish()

    any_spec = pl.BlockSpec(memory_space=pl.ANY)
    vmem_spec = pl.BlockSpec(memory_space=pltpu.VMEM)
    res = pl.pallas_call(
        body,
        in_specs=[any_spec] * (n + ns),
        out_specs=[vmem_spec] * n + [any_spec] * ns,
        out_shape=([jax.ShapeDtypeStruct(sh, F32) for sh in shapes]
                   + [jax.ShapeDtypeStruct((N_DEV,) + sm.shape, sm.dtype) for sm in small]),
        scratch_shapes=(
            [pltpu.VMEM(sh, BF16) for sh in shapes]
            + [pltpu.VMEM((3,) + sh, BF16) for sh in shapes]
            + [pltpu.VMEM((4,) + sh, BF16) for sh in shapes]
            + [pltpu.VMEM((3,) + sh, BF16) for sh in shapes]
            + [pltpu.SemaphoreType.DMA((n, 4)),
               pltpu.SemaphoreType.DMA((n, 4)), pltpu.SemaphoreType.DMA((n, 4)),
               pltpu.SemaphoreType.DMA((n, 3)), pltpu.SemaphoreType.DMA((n, 3))]
            + _gather_sems(ns)
        ),
        compiler_params=pltpu.CompilerParams(vmem_limit_bytes=V7X_VMEM_LIMIT),
        name=name,
    )(*grads, *small)
    return res[:n], res[n:]


def _pair_add(grads, from_sib, blks, name):
    n = len(grads)

    def body(blk_ref, *refs):
        for a in range(n):
            refs[2 * n + a][...] = (refs[a][...].astype(F32) + refs[n + a][...].astype(F32)).astype(BF16)

    mine = [pl.BlockSpec((None,) + g.shape[1:], lambda j, b: (b[j], 0, 0)) for g in grads]
    same = [pl.BlockSpec((None,) + g.shape[1:], lambda j, b: (j, 0, 0)) for g in grads]
    return pl.pallas_call(
        body,
        grid_spec=pltpu.PrefetchScalarGridSpec(
            num_scalar_prefetch=1, grid=(4,), in_specs=mine + same, out_specs=same),
        out_shape=[jax.ShapeDtypeStruct((4,) + g.shape[1:], BF16) for g in grads],
        compiler_params=_params(("arbitrary",)),
        name=name,
    )(blks, *grads, *from_sib)


def _chip_sum(parts, from_chips, name):
    n = len(parts)

    def body(*refs):
        for a in range(n):
            acc = refs[a][...].astype(F32)
            for j in range(3):
                acc = acc + refs[n + a][j].astype(F32)
            refs[2 * n + a][...] = acc

    half = [p.shape[1] // 2 for p in parts]
    return pl.pallas_call(
        body,
        grid=(2,),
        in_specs=([pl.BlockSpec((None, h, p.shape[2]), lambda i: (3, i, 0)) for p, h in zip(parts, half)]
                  + [pl.BlockSpec((3, h, p.shape[2]), lambda i: (0, i, 0)) for p, h in zip(parts, half)]),
        out_specs=[pl.BlockSpec((h, p.shape[2]), lambda i: (i, 0)) for p, h in zip(parts, half)],
        out_shape=[jax.ShapeDtypeStruct(p.shape[1:], F32) for p in parts],
        compiler_params=_params(("arbitrary",)),
        name=name,
    )(*parts, *from_chips)


def _adam_math(w, g, m, v):
    nm = ADAM_B1 * m + (1.0 - ADAM_B1) * g
    nv = ADAM_B2 * v + (1.0 - ADAM_B2) * (g * g)
    m_hat = nm / (1.0 - ADAM_B1 ** ADAM_STEP)
    v_hat = nv / (1.0 - ADAM_B2 ** ADAM_STEP)
    return -ADAM_LR * (m_hat / (jnp.sqrt(v_hat) + ADAM_EPS) + ADAM_WD * w), nm, nv


def _sum_adamw(part, from_chips, w, m, v, name):
    r, c = w.shape
    half = r // 2

    def body(p_ref, f_ref, w_ref, m_ref, v_ref, g_ref, d_ref, nm_ref, nv_ref):
        g = p_ref[...].astype(F32)
        for j in range(3):
            g = g + f_ref[j].astype(F32)
        g_ref[...] = g
        d_ref[...], nm_ref[...], nv_ref[...] = _adam_math(w_ref[...], g, m_ref[...], v_ref[...])

    spec = pl.BlockSpec((half, c), lambda i: (i, 0))
    return pl.pallas_call(
        body,
        grid=(2,),
        in_specs=[pl.BlockSpec((None, half, c), lambda i: (3, i, 0)),
                  pl.BlockSpec((3, half, c), lambda i: (0, i, 0)), spec, spec, spec],
        out_specs=[spec] * 4,
        out_shape=[jax.ShapeDtypeStruct((r, c), F32)] * 4,
        compiler_params=_params(("arbitrary",)),
        name=name,
    )(part, from_chips, w, m, v)


def _small_update(gathered, layout, params, name):
    ng, npar = len(gathered), len(params)

    def body(*refs):
        parts = refs[:ng]
        prm = refs[ng:ng + 3 * npar]
        tot_refs = refs[ng + 3 * npar:2 * ng + 3 * npar]
        out = refs[2 * ng + 3 * npar:]
        tots = []
        for a in range(ng):
            acc = parts[a][0]
            for d in range(1, N_DEV):
                acc = acc + parts[a][d]
            tot_refs[a][...] = acc
            tots.append(acc)
        for i, (a, row, width) in enumerate(layout):
            g = tots[a] if row is None else tots[a][row:row + 1, :width]
            delta, nm, nv = _adam_math(prm[3 * i][...], g, prm[3 * i + 1][...], prm[3 * i + 2][...])
            out[4 * i][...] = g
            out[4 * i + 1][...] = delta
            out[4 * i + 2][...] = nm
            out[4 * i + 3][...] = nv

    flat = [t for prm in params for t in prm]
    res = pl.pallas_call(
        body,
        out_shape=([jax.ShapeDtypeStruct(g.shape[1:], F32) for g in gathered]
                   + [jax.ShapeDtypeStruct(prm[0].shape, F32) for prm in params for _ in range(4)]),
        compiler_params=pltpu.CompilerParams(vmem_limit_bytes=V7X_VMEM_LIMIT),
        name=name,
    )(*gathered, *flat)
    return res[:ng], [tuple(res[ng + 4 * i:ng + 4 * i + 4]) for i in range(npar)]


def _adamw(w, g, m, v, name):
    rows, cols = w.shape
    br = rows
    for cand in (512, 256, 128):
        if rows % cand == 0 and rows > cand:
            br = cand
            break

    def body(w_ref, g_ref, m_ref, v_ref, d_ref, nm_ref, nv_ref):
        d_ref[...], nm_ref[...], nv_ref[...] = _adam_math(w_ref[...], g_ref[...], m_ref[...], v_ref[...])

    spec = pl.BlockSpec((br, cols), lambda i: (i, 0))
    shape = jax.ShapeDtypeStruct((rows, cols), F32)
    return pl.pallas_call(
        body,
        grid=(rows // br,),
        in_specs=[spec] * 4,
        out_specs=[spec] * 3,
        out_shape=[shape] * 3,
        compiler_params=_params(("arbitrary",)),
        name=name,
    )(w, g, m, v)


def _by_device(full):
    return full.reshape(N_DEV, full.shape[0] // N_DEV, full.shape[1])


def kernel(x, p, g_mix, w_in, conv_w, conv_b, ln_g, ln_b, pool_w, pool_scale, w_out, g_ffn, w_gate_up, w_down, g_ple_gate, w_ple_gate, w_ple_up, g_ple_post, g_final, loss_target, m_g_mix, m_w_in, m_conv_w, m_conv_b, m_ln_g, m_ln_b, m_pool_w, m_pool_scale, m_w_out, m_g_ffn, m_w_gate_up, m_w_down, m_g_ple_gate, m_w_ple_gate, m_w_ple_up, m_g_ple_post, m_g_final, v_g_mix, v_w_in, v_conv_w, v_conv_b, v_ln_g, v_ln_b, v_pool_w, v_pool_scale, v_w_out, v_g_ffn, v_w_gate_up, v_w_down, v_g_ple_gate, v_w_ple_gate, v_w_ple_up, v_g_ple_post, v_g_final):
    seq = x.shape[1]
    xs = x[0]
    ps = p[0, 0]
    tgt = loss_target[0]
    ax, ay, ac = lax.axis_index("x"), lax.axis_index("y"), lax.axis_index("c")
    me = _block(ax, ay, ac)
    blks = jnp.stack([_block(1 - ax, ay, ac), _block(ax, 1 - ay, ac), _block(1 - ax, 1 - ay, ac), me]).astype(jnp.int32)
    rows = lambda gth: gth.reshape((-1,) + gth.shape[2:])

    w_in_t, w_out_f, conv_w_t = [rows(gth) for gth in _all_gather([
        w_in[0].T.astype(BF16),
        w_out[0].astype(BF16),
        jnp.pad(conv_w[0].T, ((0, 0), (0, CONV_HALO - CONV_K))),
    ], "gather_first")]
    conv_w_f = conv_w_t.T
    pool_w_b = pool_w[0].astype(BF16)

    carry = _Carry()
    _carry_gather(carry, [w_gate_up[0].T.astype(BF16)], 8, 12)
    (z, u1, pooled, x1), (w_gu_all,) = _fwd_mix(
        xs, w_in_t, w_out_f, g_mix, conv_w_f, conv_b, ln_g, ln_b, pool_w_b, pool_scale, min(256, seq), carry)
    w_gu_t = rows(w_gu_all)

    carry = _Carry()
    _carry_gather(carry, [w_down[0].astype(BF16)], 5, 8)
    (h2, g_sav, u_sav, act), (w_down_all,) = _ffn_up(x1, w_gu_t, g_ffn, min(512, seq), carry)
    w_down_f = rows(w_down_all)

    carry = _Carry()
    _carry_gather(carry, [w_ple_gate[0].astype(BF16),
                          w_ple_up[0].T.astype(BF16)], 3, 5)
    (x2,), late = _ffn_down(x1, act, w_down_f, min(512, seq), carry)
    w_pg_f, w_pu_t = [rows(gth) for gth in late]
    dx2, dx2b, hg, ds, dpe, pb, stats_ple = _ple_loss(x2, ps, tgt, w_pg_f, w_pu_t, g_ple_gate, g_ple_post,
                                                      g_final.reshape(1, D_MODEL), min(256, seq))
    dg, du, dx1, dx1b, stats_ffn = _ffn_bwd(dx2, dx2b, x1, g_sav, u_sav, w_gu_t, w_down_f, g_ffn, min(512, seq))

    d_w_pg, _ = _grad_matmul(hg, ds, 256, "grad_w_ple_gate")
    d_w_pu_t, _ = _grad_matmul(dpe, pb, 256, "grad_w_ple_up")
    d_w_gu_t, _ = _grad_matmul(dg, h2, 256, "grad_w_gate_up", a2=du)
    early = [_by_device(d_w_pg), _by_device(d_w_pu_t), _by_device(d_w_gu_t)]
    carry = _Carry()
    _carry_pair(carry, early)
    d_w_down, from_sib = _grad_matmul(act, dx2b, 256, "grad_w_down", carry=carry)
    early_parts = _pair_add(early, from_sib, blks, "pair_add_early")

    carry = _Carry()
    _carry_chip(carry, early_parts)
    _carry_pair(carry, [_by_device(d_w_down)])
    (dx, dz, mix, h1, vec_mix, dconv_w_part, dpool_w_part), carried = _bwd_mix(
        dx1, dx1b, xs, z, u1, pooled, w_in_t, w_out_f, g_mix, conv_w_f, ln_g, ln_b, pool_w_b, pool_scale,
        min(256, seq), carry)
    early_chips, down_sib = carried[:3], carried[3:]
    down_parts = _pair_add([_by_device(d_w_down)], down_sib, blks, "pair_add_down")

    carry = _Carry()
    _carry_chip(carry, down_parts)
    d_w_in_t, down_chips = _grad_matmul(dz, h1, 256, "grad_w_in", carry=carry)
    d_w_out, _ = _grad_matmul(mix, dx1b, 256, "grad_w_out")
    (gr_w_in_t, gr_w_out), small = _reduce_scatter(
        [_by_device(d_w_in_t), _by_device(d_w_out)],
        [vec_mix, stats_ple, stats_ffn, dconv_w_part, dpool_w_part], "scatter_last")
    (gr_w_pu_t,) = _chip_sum(early_parts[1:2], early_chips[1:2], "chip_sum")

    vec_names = ["g_mix", "ln_g", "ln_b", "conv_b", "pool_scale", "g_final", "g_ple_post", "g_ple_gate", "g_ffn",
                 "pool_w"]
    layout = [(0, 0, D_MODEL), (0, 1, C_CONV), (0, 2, C_CONV), (0, 3, C_CONV), (0, 4, C_POOL),
              (1, 0, D_MODEL), (1, 1, D_MODEL), (1, 2, D_MODEL), (2, 0, D_MODEL), (4, None, None)]
    as_row = lambda t: t.reshape(1, D_MODEL)
    params = [(g_mix, m_g_mix, v_g_mix), (ln_g, m_ln_g, v_ln_g), (ln_b, m_ln_b, v_ln_b),
              (conv_b, m_conv_b, v_conv_b), (pool_scale, m_pool_scale, v_pool_scale),
              (as_row(g_final), as_row(m_g_final), as_row(v_g_final)),
              (g_ple_post, m_g_ple_post, v_g_ple_post), (g_ple_gate, m_g_ple_gate, v_g_ple_gate),
              (g_ffn, m_g_ffn, v_g_ffn), (pool_w[0], m_pool_w[0], v_pool_w[0])]
    tots, small_upd = _small_update(small, layout, params, "small_update")
    loss = tots[1][3, 0]
    upd = {}
    for nm, res, prm in zip(vec_names, small_upd, [g_mix, ln_g, ln_b, conv_b, pool_scale, g_final, g_ple_post,
                                                    g_ple_gate, g_ffn, pool_w]):
        upd[nm] = tuple(t.reshape(prm.shape) for t in res)
    gr_conv_w = lax.dynamic_slice_in_dim(tots[3][:CONV_K], me * (C_CONV // N_DEV), C_CONV // N_DEV, axis=1)

    fused = [
        ("w_ple_gate", early_parts[0], early_chips[0], w_ple_gate[0], m_w_ple_gate[0], v_w_ple_gate[0], False),
        ("w_gate_up", early_parts[2], early_chips[2], w_gate_up[0].T, m_w_gate_up[0].T, v_w_gate_up[0].T, True),
        ("w_down", down_parts[0], down_chips[0], w_down[0], m_w_down[0], v_w_down[0], False),
    ]
    for nm, part, chips3, w_, m_, v_, transposed in fused:
        res = _sum_adamw(part, chips3, w_, m_, v_, "adamw_" + nm)
        upd[nm] = tuple((t.T if transposed else t)[None] for t in res)
    plain = [
        ("w_in", w_in[0].T, gr_w_in_t, m_w_in[0].T, v_w_in[0].T, True),
        ("w_out", w_out[0], gr_w_out, m_w_out[0], v_w_out[0], False),
        ("w_ple_up", w_ple_up[0], gr_w_pu_t.T, m_w_ple_up[0], v_w_ple_up[0], False),
        ("conv_w", conv_w[0], gr_conv_w, m_conv_w[0], v_conv_w[0], False),
    ]
    for nm, w_, g_, m_, v_, transposed in plain:
        res = (g_,) + tuple(_adamw(w_, g_, m_, v_, "adamw_" + nm))
        upd[nm] = tuple((t.T if transposed else t)[None] for t in res)

    order = ["g_mix", "w_in", "conv_w", "conv_b", "ln_g", "ln_b", "pool_w", "pool_scale", "w_out", "g_ffn",
             "w_gate_up", "w_down", "g_ple_gate", "w_ple_gate", "w_ple_up", "g_ple_post", "g_final"]
    outs = [loss, dx[None]]
    for k in range(4):
        outs += [upd[nm][k] for nm in order]
    return tuple(outs)
```

```python
import functools

import jax
import jax.numpy as jnp
from jax import lax
from jax.experimental import pallas as pl
from jax.experimental.pallas import tpu as pltpu

D_MODEL = 1024
C_CONV = 512
C_POOL = 512
Z_WIDTH = 2 * C_CONV + C_POOL
POOL_WINDOWS = (2, 4, 8, 16)
POOL_GROUP = 128
CONV_K = 31
D_FF = 2816
D_PLE = 256
EPS = 1e-6
N_DEV = 8

ADAM_LR = 0.001
ADAM_B1 = 0.9
ADAM_B2 = 0.999
ADAM_EPS = 1e-08
ADAM_WD = 0.01
ADAM_STEP = 10

CONV_HALO = 32
POOL_HALO = 16
ROW_CHUNK = 32
V7X_VMEM_LIMIT = 56 * 1024 * 1024
FF_CHUNK = D_FF // 2

BF16 = jnp.bfloat16
F32 = jnp.float32
MESH = pl.DeviceIdType.MESH


def _dot(a, b):
    return lax.dot_general(a, b, (((1,), (0,)), ((), ())), preferred_element_type=F32)


def _dot_nt(a, b):
    return lax.dot_general(a, b, (((1,), (1,)), ((), ())), preferred_element_type=F32)


def _dot_tn(a, b):
    return lax.dot_general(a, b, (((0,), (0,)), ((), ())), preferred_element_type=F32)


def _rms_fwd(x, g):
    r = lax.rsqrt(jnp.mean(x * x, axis=-1, keepdims=True) + EPS)
    return x * r * g, r


def _rms_bwd(x, r, g, dy):
    xr = x * r
    dg = jnp.sum(dy * xr, axis=0, keepdims=True)
    dyg = dy * g
    dx = r * (dyg - xr * jnp.mean(dyg * xr, axis=-1, keepdims=True))
    return dx, dg


def _sigmoid(x):
    return jax.nn.sigmoid(x)


def _params(sem=None):
    return pltpu.CompilerParams(dimension_semantics=sem, vmem_limit_bytes=V7X_VMEM_LIMIT)


def _place():
    x, y, c = lax.axis_index("x"), lax.axis_index("y"), lax.axis_index("c")
    chips = [(1 - x, y), (x, 1 - y), (1 - x, 1 - y)]
    return (x, y, c), (x, y, 1 - c), chips


def _block(px, py, pc):
    return 4 * px + 2 * py + pc


class _Carry:
    def __init__(self):
        self.inputs = []
        self.out_shapes = []
        self.copies = []
        self.locals = []

    def add_input(self, arr):
        self.inputs.append(arr)
        return len(self.inputs) - 1

    def add_output(self, shape, dtype):
        self.out_shapes.append(jax.ShapeDtypeStruct(shape, dtype))
        return len(self.out_shapes) - 1

    def local(self, src_idx, dst_idx, dst_blk):
        self.locals.append((src_idx, dst_idx, dst_blk))

    def copy(self, src, dst_idx, dst_blk, got_blk, peer, step=0, after=()):
        self.copies.append(dict(src=src, dst_idx=dst_idx, dst_blk=dst_blk, got_blk=got_blk, peer=peer, step=step,
                                after=tuple(after)))
        return len(self.copies) - 1

    def sem_shapes(self):
        return [pltpu.SemaphoreType.DMA((max(1, len(self.copies)),)),
                pltpu.SemaphoreType.DMA((max(1, len(self.copies)),)),
                pltpu.SemaphoreType.DMA((max(1, len(self.locals)),))]

    @staticmethod
    def _view(ref, where):
        if isinstance(where, tuple):
            blk, row0, nrows = where
            return ref.at[blk, pl.ds(row0, nrows)]
        return ref.at[where]

    def _desc(self, k, ins, outs, sems, place):
        cp = self.copies[k]
        me, sib, chips = place
        kind, idx, blk = cp["src"]
        src = (ins if kind == "in" else outs)[idx]
        if blk is not None:
            src = self._view(src, blk(*place))
        to = sib if cp["peer"] == "sib" else (*chips[cp["peer"]], me[2])
        return pltpu.make_async_remote_copy(
            src_ref=src, dst_ref=self._view(outs[cp["dst_idx"]], cp["dst_blk"](*place)),
            send_sem=sems[0].at[k], recv_sem=sems[1].at[k], device_id=to, device_id_type=MESH)

    def _arrival(self, k, outs, sems, place):
        cp = self.copies[k]
        got = self._view(outs[cp["dst_idx"]], cp["got_blk"](*place))
        return pltpu.make_async_remote_copy(
            src_ref=got, dst_ref=got, send_sem=sems[0].at[k], recv_sem=sems[1].at[k],
            device_id=place[0], device_id_type=MESH)

    def _local(self, n, ins, outs, sems, place):
        src_idx, dst_idx, blk = self.locals[n]
        return pltpu.make_async_copy(ins[src_idx], outs[dst_idx].at[blk(*place)], sems[2].at[n])

    def starts(self, step, nsteps, ins, outs, sems):
        place = _place()
        self._waited = set()
        for s in sorted({0} | {cp["step"] for cp in self.copies}):
            ks = [k for k, cp in enumerate(self.copies) if cp["step"] == s]

            @pl.when(step == min(s, nsteps - 1))
            def _(s=s, ks=ks):
                if s == 0:
                    for n in range(len(self.locals)):
                        self._local(n, ins, outs, sems, place).start()
                for k in ks:
                    for a in self.copies[k]["after"]:
                        if a not in self._waited:
                            self._arrival(a, outs, sems, place).wait_recv()
                            self._waited.add(a)
                    self._desc(k, ins, outs, sems, place).start()

    def finish(self, step, nsteps, ins, outs, sems):
        place = _place()

        @pl.when(step == nsteps - 1)
        def _():
            for k in range(len(self.copies)):
                if k not in self._waited:
                    self._arrival(k, outs, sems, place).wait_recv()
            for k in range(len(self.copies)):
                self._desc(k, ins, outs, sems, place).wait_send()
            for n in range(len(self.locals)):
                self._local(n, ins, outs, sems, place).wait()


def _const_blk(j):
    return lambda me, sib, chips: j


def _carry_gather(carry, shards, relay_step, last_step):
    outs = []
    for sh in shards:
        i = carry.add_input(sh)
        o = carry.add_output((N_DEV,) + sh.shape, sh.dtype)
        half = sh.shape[0] // 2
        tile = 16 if sh.dtype == BF16 else 8
        split = half % tile == 0
        rows = [(0, half), (half, sh.shape[0] - half)] if split else [(0, sh.shape[0]), None]

        def whole(j, core):
            return lambda me, sib, chips, j=j, core=core: _block(*chips[j], me[2] if core == 0 else 1 - me[2])

        def part(j, core, h, rows=rows):
            return lambda me, sib, chips: (_block(*chips[j], me[2] if core == 0 else 1 - me[2]),) + rows[h]

        mine = lambda me, sib, chips: _block(*me)
        carry.local(i, o, mine)
        carry.copy(("in", i, None), o, mine, lambda me, sib, chips: _block(*sib), "sib")
        near = [carry.copy(("in", i, None), o, mine, whole(j, 0), j) for j in range(2)]
        for j in range(2):
            carry.copy(("out", o, whole(j, 0)), o, whole(j, 0), whole(j, 1), "sib", step=relay_step, after=(near[j],))
        for j in range(2):
            if rows[j] is None:
                continue
            far = carry.copy(("out", o, part(j, 0, j)), o, part(j, 0, j), part(2, 0, j), 1 - j,
                             step=relay_step, after=(near[j],))
            carry.copy(("out", o, part(2, 0, j)), o, part(2, 0, j), part(2, 1, j), "sib", step=last_step, after=(far,))
        outs.append(o)
    return outs


def _carry_pair(carry, grads):
    outs = []
    for g in grads:
        i = carry.add_input(g)
        o = carry.add_output((4,) + g.shape[1:], g.dtype)
        for j in range(4):
            if j < 3:
                blk = lambda me, sib, chips, j=j: _block(*chips[j], 1 - me[2])
            else:
                blk = lambda me, sib, chips: _block(*sib)
            carry.copy(("in", i, blk), o, _const_blk(j), _const_blk(j), "sib")
        outs.append(o)
    return outs


def _carry_chip(carry, parts):
    outs = []
    for p in parts:
        i = carry.add_input(p)
        o = carry.add_output((3,) + p.shape[1:], p.dtype)
        for j in range(3):
            carry.copy(("in", i, _const_blk(j)), o, _const_blk(j), _const_blk(j), j)
        outs.append(o)
    return outs


def _pcall(body, *, grid, in_specs, out_specs, out_shape, scratch_shapes, name, args, carry=None):
    sem = ("arbitrary",) * len(grid)
    if carry is None:
        res = pl.pallas_call(body, grid=grid, in_specs=in_specs, out_specs=out_specs, out_shape=out_shape,
                             scratch_shapes=scratch_shapes, compiler_params=_params(sem), name=name)(*args)
        return list(res), []
    n_in, n_out, n_scr = len(in_specs), len(out_specs), len(scratch_shapes)
    c_in, c_out = len(carry.inputs), len(carry.out_shapes)
    nsteps = 1
    for extent in grid:
        nsteps *= extent

    def wrapped(*refs):
        ins = refs[:n_in]
        cins = refs[n_in:n_in + c_in]
        o0 = n_in + c_in
        outs = refs[o0:o0 + n_out]
        couts = refs[o0 + n_out:o0 + n_out + c_out]
        s0 = o0 + n_out + c_out
        scr = refs[s0:s0 + n_scr]
        sems = refs[s0 + n_scr:]
        step = pl.program_id(0)
        for d in range(1, len(grid)):
            step = step * grid[d] + pl.program_id(d)
        carry.starts(step, nsteps, cins, couts, sems)
        body(*ins, *outs, *scr)
        carry.finish(step, nsteps, cins, couts, sems)

    any_spec = pl.BlockSpec(memory_space=pl.ANY)
    res = pl.pallas_call(
        wrapped, grid=grid,
        in_specs=list(in_specs) + [any_spec] * c_in,
        out_specs=list(out_specs) + [any_spec] * c_out,
        out_shape=list(out_shape) + carry.out_shapes,
        scratch_shapes=list(scratch_shapes) + carry.sem_shapes(),
        compiler_params=_params(sem), name=name)(*args, *carry.inputs)
    return list(res[:n_out]), list(res[n_out:])


def _shifted_copies(buf, shifted, tm):
    span = tm + CONV_HALO - 8
    for r in range(1, 8):
        shifted[r - 1, 0:span, :] = buf[r:r + span, :]


def _rows_at(buf, shifted, start):
    aligned, r = (start // 8) * 8, start % 8
    if r == 0:
        return buf[aligned:aligned + ROW_CHUNK, :]
    return shifted[r - 1, aligned:aligned + ROW_CHUNK, :]


def _row_spec(tm, width):
    return pl.BlockSpec((tm, width), lambda i: (i, 0))


def _const_spec(shape):
    return pl.BlockSpec(shape, lambda i: (0,) * len(shape))


def _weight_spec(shape):
    return pl.BlockSpec(shape, lambda i: (0,) * len(shape), pipeline_mode=pl.Buffered(1))


def _fwd_mix(x, w_in_t, w_out, g_mix, conv_w, conv_b, ln_g, ln_b, pool_w, pool_scale, tm, carry=None):
    s = x.shape[0]
    nt = s // tm

    def body(x_ref, win_ref, wout_ref, gmix_ref, cw_ref, cb_ref, lng_ref, lnb_ref, pw_ref, ps_ref,
             z_ref, u1_ref, pooled_ref, x1_ref, ubuf, vbuf, ush):
        i = pl.program_id(0)

        @pl.when(i == 0)
        def _():
            ubuf[0:CONV_HALO, :] = jnp.zeros((CONV_HALO, C_CONV), F32)
            vbuf[0:POOL_HALO, :] = jnp.zeros((POOL_HALO, C_POOL), F32)

        xt = x_ref[...]
        h, _ = _rms_fwd(xt, gmix_ref[...])
        z = _dot_nt(h.astype(BF16), win_ref[...])
        z_ref[...] = z
        a = z[:, :C_CONV]
        b = z[:, C_CONV:2 * C_CONV]
        v = z[:, 2 * C_CONV:]
        ubuf[CONV_HALO:CONV_HALO + tm, :] = a * _sigmoid(b)
        vbuf[POOL_HALO:POOL_HALO + tm, :] = v

        _shifted_copies(ubuf, ush, tm)
        for rc in range(tm // ROW_CHUNK):
            base = rc * ROW_CHUNK + CONV_HALO - (CONV_K - 1)
            acc = jnp.broadcast_to(cb_ref[...], (ROW_CHUNK, C_CONV))
            for k in range(CONV_K):
                acc = acc + cw_ref[k:k + 1, :] * _rows_at(ubuf, ush, base + k)
            u1_ref[rc * ROW_CHUNK:(rc + 1) * ROW_CHUNK, :] = acc

        u1 = u1_ref[...]
        mu = jnp.mean(u1, axis=-1, keepdims=True)
        cen = u1 - mu
        rstd = lax.rsqrt(jnp.mean(cen * cen, axis=-1, keepdims=True) + EPS)
        u2 = cen * rstd * lng_ref[...] + lnb_ref[...]
        u = u2 * _sigmoid(u2)

        pos1 = (i * tm + lax.broadcasted_iota(jnp.int32, (tm, 1), 0) + 1).astype(F32)
        parts = [u]
        for g, w in enumerate(POOL_WINDOWS):
            cols = slice(g * POOL_GROUP, (g + 1) * POOL_GROUP)
            vg = v[:, cols]
            tot = vg
            for j in range(1, w):
                tot = tot + vbuf[POOL_HALO - j:POOL_HALO - j + tm, cols]
            pooled = tot / jnp.minimum(pos1, float(w)) - vg
            pooled_b = pooled.astype(BF16)
            pooled_ref[:, cols] = pooled_b
            parts.append(_dot(pooled_b, pw_ref[g]) * ps_ref[:, cols])
        mix = jnp.concatenate(parts, axis=-1).astype(BF16)
        x1_ref[...] = xt + _dot(mix, wout_ref[...])

        ubuf[0:CONV_HALO, :] = ubuf[tm:tm + CONV_HALO, :]
        vbuf[0:POOL_HALO, :] = vbuf[tm:tm + POOL_HALO, :]

    return _pcall(
        body,
        grid=(nt,),
        in_specs=[
            _row_spec(tm, D_MODEL),
            _const_spec((Z_WIDTH, D_MODEL)),
            _const_spec((D_MODEL, D_MODEL)),
            _const_spec((1, D_MODEL)),
            _const_spec((CONV_HALO, C_CONV)),
            _const_spec((1, C_CONV)),
            _const_spec((1, C_CONV)),
            _const_spec((1, C_CONV)),
            _const_spec((len(POOL_WINDOWS), POOL_GROUP, POOL_GROUP)),
            _const_spec((1, C_POOL)),
        ],
        out_specs=[
            _row_spec(tm, Z_WIDTH),
            _row_spec(tm, C_CONV),
            _row_spec(tm, C_POOL),
            _row_spec(tm, D_MODEL),
        ],
        out_shape=[
            jax.ShapeDtypeStruct((s, Z_WIDTH), F32),
            jax.ShapeDtypeStruct((s, C_CONV), F32),
            jax.ShapeDtypeStruct((s, C_POOL), BF16),
            jax.ShapeDtypeStruct((s, D_MODEL), F32),
        ],
        scratch_shapes=[
            pltpu.VMEM((tm + CONV_HALO, C_CONV), F32),
            pltpu.VMEM((tm + POOL_HALO, C_POOL), F32),
            pltpu.VMEM((7, tm + CONV_HALO, C_CONV), F32),
        ],
        name="fwd_mix",
        args=(x, w_in_t, w_out, g_mix, conv_w, conv_b, ln_g, ln_b, pool_w, pool_scale),
        carry=carry,
    )


def _ffn_up(x1, w_gu_t, g_ffn, tm, carry=None):
    s = x1.shape[0]

    def body(x1_ref, w_ref, gffn_ref, h2_ref, g_ref, u_ref, act_ref):
        h, _ = _rms_fwd(x1_ref[...], gffn_ref[...])
        h2 = h.astype(BF16)
        h2_ref[...] = h2
        for c in range(D_FF // FF_CHUNK):
            cols = slice(c * FF_CHUNK, (c + 1) * FF_CHUNK)
            g = _dot_nt(h2, w_ref[c * FF_CHUNK:(c + 1) * FF_CHUNK, :])
            u = _dot_nt(h2, w_ref[D_FF + c * FF_CHUNK:D_FF + (c + 1) * FF_CHUNK, :])
            g_ref[:, cols] = g.astype(BF16)
            u_ref[:, cols] = u.astype(BF16)
            act_ref[:, cols] = (g * _sigmoid(g) * u).astype(BF16)

    return _pcall(
        body,
        grid=(s // tm,),
        in_specs=[_row_spec(tm, D_MODEL), _weight_spec((2 * D_FF, D_MODEL)), _const_spec((1, D_MODEL))],
        out_specs=[_row_spec(tm, D_MODEL), _row_spec(tm, D_FF), _row_spec(tm, D_FF), _row_spec(tm, D_FF)],
        out_shape=[
            jax.ShapeDtypeStruct((s, D_MODEL), BF16),
            jax.ShapeDtypeStruct((s, D_FF), BF16),
            jax.ShapeDtypeStruct((s, D_FF), BF16),
            jax.ShapeDtypeStruct((s, D_FF), BF16),
        ],
        scratch_shapes=[],
        name="ffn_up",
        args=(x1, w_gu_t, g_ffn),
        carry=carry,
    )


def _ffn_down(x1, act, w_down, tm, carry=None):
    s = x1.shape[0]

    def body(x1_ref, act_ref, wd_ref, x2_ref):
        x2_ref[...] = x1_ref[...] + _dot(act_ref[...], wd_ref[...])

    return _pcall(
        body,
        grid=(s // tm,),
        in_specs=[_row_spec(tm, D_MODEL), _row_spec(tm, D_FF), _weight_spec((D_FF, D_MODEL))],
        out_specs=[_row_spec(tm, D_MODEL)],
        out_shape=[jax.ShapeDtypeStruct((s, D_MODEL), F32)],
        scratch_shapes=[],
        name="ffn_down",
        args=(x1, act, w_down),
        carry=carry,
    )


def _ple_loss(x2, p, tgt, w_pg, w_pu_t, g_gate, g_post, g_final, tm):
    s = x2.shape[0]
    nt = s // tm

    def body(x2_ref, p_ref, t_ref, wpg_ref, wpu_ref, gg_ref, gp_ref, gf_ref,
             dx2_ref, dx2b_ref, hg_ref, ds_ref, dpe_ref, pb_ref, stats_ref):
        i = pl.program_id(0)

        @pl.when(i == 0)
        def _():
            stats_ref[...] = jnp.zeros_like(stats_ref)

        x2 = x2_ref[...]
        hg, rg = _rms_fwd(x2, gg_ref[...])
        hg_b = hg.astype(BF16)
        hg_ref[...] = hg_b
        gate = _sigmoid(_dot(hg_b, wpg_ref[...]))
        pb = p_ref[...].astype(BF16)
        pb_ref[...] = pb
        pe = _dot_nt(pb, wpu_ref[...])
        e, rp = _rms_fwd(pe, gp_ref[...])
        x3 = x2 + gate * e
        y, r3 = _rms_fwd(x3, gf_ref[...])
        diff = y - t_ref[...]
        loss = 0.5 * jnp.sum(jnp.sum(diff * diff, axis=-1, keepdims=True), axis=0, keepdims=True) / D_MODEL
        dy = diff * (1.0 / D_MODEL)

        dx3, dgf = _rms_bwd(x3, r3, gf_ref[...], dy)
        dpe, dgp = _rms_bwd(pe, rp, gp_ref[...], dx3 * gate)
        dpe_ref[...] = dpe.astype(BF16)
        ds = (dx3 * e * gate * (1.0 - gate)).astype(BF16)
        ds_ref[...] = ds
        dhg = _dot_nt(ds, wpg_ref[...])
        dxg, dgg = _rms_bwd(x2, rg, gg_ref[...], dhg)
        dx2 = dx3 + dxg
        dx2_ref[...] = dx2
        dx2b_ref[...] = dx2.astype(BF16)

        stats_ref[0:1, :] += dgf
        stats_ref[1:2, :] += dgp
        stats_ref[2:3, :] += dgg
        stats_ref[3:4, :] += jnp.broadcast_to(loss, (1, D_MODEL))

    return pl.pallas_call(
        body,
        grid=(nt,),
        in_specs=[
            _row_spec(tm, D_MODEL),
            _row_spec(tm, D_PLE),
            _row_spec(tm, D_MODEL),
            _const_spec((D_MODEL, D_MODEL)),
            _const_spec((D_MODEL, D_PLE)),
            _const_spec((1, D_MODEL)),
            _const_spec((1, D_MODEL)),
            _const_spec((1, D_MODEL)),
        ],
        out_specs=[
            _row_spec(tm, D_MODEL),
            _row_spec(tm, D_MODEL),
            _row_spec(tm, D_MODEL),
            _row_spec(tm, D_MODEL),
            _row_spec(tm, D_MODEL),
            _row_spec(tm, D_PLE),
            _const_spec((8, D_MODEL)),
        ],
        out_shape=[
            jax.ShapeDtypeStruct((s, D_MODEL), F32),
            jax.ShapeDtypeStruct((s, D_MODEL), BF16),
            jax.ShapeDtypeStruct((s, D_MODEL), BF16),
            jax.ShapeDtypeStruct((s, D_MODEL), BF16),
            jax.ShapeDtypeStruct((s, D_MODEL), BF16),
            jax.ShapeDtypeStruct((s, D_PLE), BF16),
            jax.ShapeDtypeStruct((8, D_MODEL), F32),
        ],
        compiler_params=_params(("arbitrary",)),
        name="ple_loss",
    )(x2, p, tgt, w_pg, w_pu_t, g_gate, g_post, g_final)


def _ffn_bwd(dx2, dx2b, x1, g_sav, u_sav, w_gu_t, w_down, g_ffn, tm):
    s = x1.shape[0]

    def body(dx2_ref, dx2b_ref, x1_ref, g_ref, u_ref, w_ref, wd_ref, gffn_ref,
             dg_ref, du_ref, dx1_ref, dx1b_ref, stats_ref):
        @pl.when(pl.program_id(0) == 0)
        def _():
            stats_ref[...] = jnp.zeros_like(stats_ref)

        dx2b = dx2b_ref[...]
        dh2 = jnp.zeros((tm, D_MODEL), F32)
        for c in range(D_FF // FF_CHUNK):
            cols = slice(c * FF_CHUNK, (c + 1) * FF_CHUNK)
            dact = _dot_nt(dx2b, wd_ref[c * FF_CHUNK:(c + 1) * FF_CHUNK, :])
            g = g_ref[:, cols].astype(F32)
            u = u_ref[:, cols].astype(F32)
            sg = _sigmoid(g)
            dg = (dact * u * sg * (1.0 + g * (1.0 - sg))).astype(BF16)
            du = (dact * g * sg).astype(BF16)
            dg_ref[:, cols] = dg
            du_ref[:, cols] = du
            dh2 = dh2 + _dot(dg, w_ref[c * FF_CHUNK:(c + 1) * FF_CHUNK, :])
            dh2 = dh2 + _dot(du, w_ref[D_FF + c * FF_CHUNK:D_FF + (c + 1) * FF_CHUNK, :])

        x1 = x1_ref[...]
        r2 = lax.rsqrt(jnp.mean(x1 * x1, axis=-1, keepdims=True) + EPS)
        dxn, dgf = _rms_bwd(x1, r2, gffn_ref[...], dh2)
        dx1 = dx2_ref[...] + dxn
        dx1_ref[...] = dx1
        dx1b_ref[...] = dx1.astype(BF16)
        stats_ref[0:1, :] += dgf

    return pl.pallas_call(
        body,
        grid=(s // tm,),
        in_specs=[
            _row_spec(tm, D_MODEL), _row_spec(tm, D_MODEL), _row_spec(tm, D_MODEL),
            _row_spec(tm, D_FF), _row_spec(tm, D_FF),
            _weight_spec((2 * D_FF, D_MODEL)), _weight_spec((D_FF, D_MODEL)), _const_spec((1, D_MODEL)),
        ],
        out_specs=[_row_spec(tm, D_FF), _row_spec(tm, D_FF), _row_spec(tm, D_MODEL), _row_spec(tm, D_MODEL),
                   _const_spec((8, D_MODEL))],
        out_shape=[
            jax.ShapeDtypeStruct((s, D_FF), BF16),
            jax.ShapeDtypeStruct((s, D_FF), BF16),
            jax.ShapeDtypeStruct((s, D_MODEL), F32),
            jax.ShapeDtypeStruct((s, D_MODEL), BF16),
            jax.ShapeDtypeStruct((8, D_MODEL), F32),
        ],
        compiler_params=_params(("arbitrary",)),
        name="ffn_bwd",
    )(dx2, dx2b, x1, g_sav, u_sav, w_gu_t, w_down, g_ffn)


def _bwd_mix(dx1, dx1b, x, z, u1, pooled, w_in_t, w_out, g_mix, conv_w, ln_g, ln_b, pool_w, pool_scale, tm,
             carry=None):
    s = x.shape[0]
    nt = s // tm

    def body(dx1_ref, dx1b_ref, x_ref, z_ref, u1_ref, pooled_ref, win_ref, wout_ref, gmix_ref, cw_ref,
             lng_ref, lnb_ref, pw_ref, ps_ref,
             dx_ref, dz_ref, mix_ref, h1_ref, vec_ref, dcw_ref, dpw_ref, dubuf, dvbuf, u0buf, du0buf, dush):
        i = pl.program_id(0)
        tile = nt - 1 - i

        @pl.when(i == 0)
        def _():
            vec_ref[...] = jnp.zeros_like(vec_ref)
            dcw_ref[...] = jnp.zeros_like(dcw_ref)
            dpw_ref[...] = jnp.zeros_like(dpw_ref)
            dubuf[tm:tm + CONV_HALO, :] = jnp.zeros((CONV_HALO, C_CONV), F32)
            dvbuf[tm:tm + POOL_HALO, :] = jnp.zeros((POOL_HALO, C_POOL), F32)

        dmix = _dot_nt(dx1b_ref[...], wout_ref[...])
        du = dmix[:, :C_CONV]
        dq = dmix[:, C_CONV:]

        pos1 = (tile * tm + lax.broadcasted_iota(jnp.int32, (tm, 1), 0) + 1).astype(F32)
        q_parts = []
        dpooled_parts = []
        dps_rows = []
        for g, w in enumerate(POOL_WINDOWS):
            cols = slice(g * POOL_GROUP, (g + 1) * POOL_GROUP)
            pooled_b = pooled_ref[:, cols]
            mixed = _dot(pooled_b, pw_ref[g])
            dqg = dq[:, cols]
            dps_rows.append(jnp.sum(dqg * mixed, axis=0, keepdims=True))
            q_parts.append(mixed * ps_ref[:, cols])
            dmixed = (dqg * ps_ref[:, cols]).astype(BF16)
            dpw_ref[g] += _dot_tn(pooled_b, dmixed)
            dpooled = _dot_nt(dmixed, pw_ref[g])
            dpooled_parts.append(dpooled)
            dvbuf[0:tm, cols] = dpooled / jnp.minimum(pos1, float(w))
        vec_ref[4:5, 0:C_POOL] += jnp.concatenate(dps_rows, axis=-1)
        dv_parts = []
        for g, w in enumerate(POOL_WINDOWS):
            cols = slice(g * POOL_GROUP, (g + 1) * POOL_GROUP)
            tot = dvbuf[0:tm, cols]
            for j in range(1, w):
                tot = tot + dvbuf[j:j + tm, cols]
            dv_parts.append(tot - dpooled_parts[g])

        u1 = u1_ref[...]
        mu = jnp.mean(u1, axis=-1, keepdims=True)
        cen = u1 - mu
        rstd = lax.rsqrt(jnp.mean(cen * cen, axis=-1, keepdims=True) + EPS)
        xhat = cen * rstd
        u2 = xhat * lng_ref[...] + lnb_ref[...]
        sg2 = _sigmoid(u2)
        du2 = du * sg2 * (1.0 + u2 * (1.0 - sg2))
        vec_ref[1:2, 0:C_CONV] += jnp.sum(du2 * xhat, axis=0, keepdims=True)
        vec_ref[2:3, 0:C_CONV] += jnp.sum(du2, axis=0, keepdims=True)
        t1 = du2 * lng_ref[...]
        du1 = rstd * (t1 - jnp.mean(t1, axis=-1, keepdims=True)
                      - xhat * jnp.mean(t1 * xhat, axis=-1, keepdims=True))
        vec_ref[3:4, 0:C_CONV] += jnp.sum(du1, axis=0, keepdims=True)
        dubuf[0:tm, :] = du1

        zt = z_ref[...]
        a = zt[:, :C_CONV]
        sgb = _sigmoid(zt[:, C_CONV:2 * C_CONV])
        u0buf[...] = a * sgb

        _shifted_copies(dubuf, dush, tm)
        for rc in range(tm // ROW_CHUNK):
            r0 = rc * ROW_CHUNK
            acc = jnp.zeros((ROW_CHUNK, C_CONV), F32)
            for k in range(CONV_K):
                acc = acc + cw_ref[k:k + 1, :] * _rows_at(dubuf, dush, r0 + (CONV_K - 1) - k)
            du0buf[r0:r0 + ROW_CHUNK, :] = acc
        for k in range(CONV_K):
            acc = jnp.zeros((ROW_CHUNK, C_CONV), F32)
            for rc in range(tm // ROW_CHUNK):
                r0 = rc * ROW_CHUNK
                acc = acc + u0buf[r0:r0 + ROW_CHUNK, :] * _rows_at(dubuf, dush, r0 + (CONV_K - 1) - k)
            dcw_ref[k:k + 1, :] += jnp.sum(acc, axis=0, keepdims=True)
        du0 = du0buf[...]

        da = du0 * sgb
        db = du0 * a * sgb * (1.0 - sgb)
        dz = jnp.concatenate([da, db] + dv_parts, axis=-1).astype(BF16)
        dz_ref[...] = dz

        mix_ref[...] = jnp.concatenate([u2 * sg2] + q_parts, axis=-1).astype(BF16)

        xt = x_ref[...]
        h1, r1 = _rms_fwd(xt, gmix_ref[...])
        h1_ref[...] = h1.astype(BF16)
        dh1 = _dot(dz, win_ref[...])
        dxn, dgm = _rms_bwd(xt, r1, gmix_ref[...], dh1)
        dx_ref[...] = dx1_ref[...] + dxn
        vec_ref[0:1, :] += dgm

        dubuf[tm:tm + CONV_HALO, :] = dubuf[0:CONV_HALO, :]
        dvbuf[tm:tm + POOL_HALO, :] = dvbuf[0:POOL_HALO, :]

    rev = lambda width: pl.BlockSpec((tm, width), lambda i: (nt - 1 - i, 0))
    return _pcall(
        body,
        grid=(nt,),
        in_specs=[
            rev(D_MODEL), rev(D_MODEL), rev(D_MODEL), rev(Z_WIDTH), rev(C_CONV), rev(C_POOL),
            _const_spec((Z_WIDTH, D_MODEL)),
            _const_spec((D_MODEL, D_MODEL)),
            _const_spec((1, D_MODEL)),
            _const_spec((CONV_HALO, C_CONV)),
            _const_spec((1, C_CONV)),
            _const_spec((1, C_CONV)),
            _const_spec((len(POOL_WINDOWS), POOL_GROUP, POOL_GROUP)),
            _const_spec((1, C_POOL)),
        ],
        out_specs=[
            rev(D_MODEL), rev(Z_WIDTH), rev(D_MODEL), rev(D_MODEL),
            _const_spec((8, D_MODEL)),
            _const_spec((CONV_HALO, C_CONV)),
            _const_spec((len(POOL_WINDOWS), POOL_GROUP, POOL_GROUP)),
        ],
        out_shape=[
            jax.ShapeDtypeStruct((s, D_MODEL), F32),
            jax.ShapeDtypeStruct((s, Z_WIDTH), BF16),
            jax.ShapeDtypeStruct((s, D_MODEL), BF16),
            jax.ShapeDtypeStruct((s, D_MODEL), BF16),
            jax.ShapeDtypeStruct((8, D_MODEL), F32),
            jax.ShapeDtypeStruct((CONV_HALO, C_CONV), F32),
            jax.ShapeDtypeStruct((len(POOL_WINDOWS), POOL_GROUP, POOL_GROUP), F32),
        ],
        scratch_shapes=[
            pltpu.VMEM((tm + CONV_HALO, C_CONV), F32),
            pltpu.VMEM((tm + POOL_HALO, C_POOL), F32),
            pltpu.VMEM((tm, C_CONV), F32),
            pltpu.VMEM((tm, C_CONV), F32),
            pltpu.VMEM((7, tm + CONV_HALO, C_CONV), F32),
        ],
        name="bwd_mix",
        args=(dx1, dx1b, x, z, u1, pooled, w_in_t, w_out, g_mix, conv_w, ln_g, ln_b, pool_w, pool_scale),
        carry=carry,
    )


def _grad_matmul(a, b, bm, name, a2=None, carry=None):
    s, ma = a.shape
    nb = b.shape[1]
    na = ma // bm
    if a2 is None:
        def body(a_ref, b_ref, o_ref):
            o_ref[...] = _dot_tn(a_ref[...], b_ref[...]).astype(BF16)

        lhs_specs = [pl.BlockSpec((s, bm), lambda i: (0, i))]
        lhs = (a,)
        steps = na
    else:
        def body(a_ref, a2_ref, b_ref, o_ref):
            i = pl.program_id(0)

            @pl.when(i < na)
            def _():
                o_ref[...] = _dot_tn(a_ref[...], b_ref[...]).astype(BF16)

            @pl.when(i >= na)
            def _():
                o_ref[...] = _dot_tn(a2_ref[...], b_ref[...]).astype(BF16)

        lhs_specs = [pl.BlockSpec((s, bm), lambda i: (0, jnp.minimum(i, na - 1))),
                     pl.BlockSpec((s, bm), lambda i: (0, jnp.maximum(i - na, 0)))]
        lhs = (a, a2)
        steps = 2 * na

    outs, carried = _pcall(
        body,
        grid=(steps,),
        in_specs=lhs_specs + [pl.BlockSpec((s, nb), lambda i: (0, 0))],
        out_specs=[pl.BlockSpec((bm, nb), lambda i: (i, 0))],
        out_shape=[jax.ShapeDtypeStruct((steps * bm, nb), BF16)],
        scratch_shapes=[],
        name=name,
        args=lhs + (b,),
        carry=carry,
    )
    return outs[0], carried


def _gather_steps(src, dst, send_sems, recv_sems, local_sems):
    n = len(src)
    me, sib, chips = _place()
    c = me[2]
    started = []
    mine = []

    def copy(a, k, block, to, from_src=False):
        rows = dst[a].at[_block(*block)]
        return pltpu.make_async_remote_copy(
            src_ref=src[a] if from_src else rows, dst_ref=rows,
            send_sem=send_sems.at[a, k], recv_sem=recv_sems.at[a, k],
            device_id=to, device_id_type=MESH)

    def start():
        for a in range(n):
            cp = pltpu.make_async_copy(src[a], dst[a].at[_block(*me)], local_sems.at[a])
            cp.start()
            mine.append(cp)
            first = [copy(a, 0, me, sib, True)]
            first += [copy(a, 1 + j, me, (*chip, c), True) for j, chip in enumerate(chips)]
            for cp in first:
                cp.start()
            started.extend(first)

    def forward():
        for a in range(n):
            for j, chip in enumerate(chips):
                copy(a, 1 + j, (*chip, c), me).wait_recv()
                fwd = copy(a, 4 + j, (*chip, c), sib)
                fwd.start()
                started.append(fwd)

    def finish():
        for a in range(n):
            copy(a, 0, sib, me).wait_recv()
            for j, chip in enumerate(chips):
                copy(a, 4 + j, (*chip, 1 - c), me).wait_recv()
        for cp in started:
            cp.wait_send()
        for cp in mine:
            cp.wait()

    return start, forward, finish


def _gather_sems(n):
    return [pltpu.SemaphoreType.DMA((n, 7)), pltpu.SemaphoreType.DMA((n, 7)), pltpu.SemaphoreType.DMA((n,))]


def _all_gather(shards, name):
    n = len(shards)
    carry = _Carry()
    _carry_gather(carry, shards, 1, 2)

    def body(*refs):
        step = jnp.int32(0)
        carry.starts(step, 1, refs[:n], refs[n:2 * n], refs[2 * n:])
        carry.finish(step, 1, refs[:n], refs[n:2 * n], refs[2 * n:])

    any_spec = pl.BlockSpec(memory_space=pl.ANY)
    return pl.pallas_call(
        body,
        in_specs=[any_spec] * n,
        out_specs=[any_spec] * n,
        out_shape=carry.out_shapes,
        scratch_shapes=carry.sem_shapes(),
        name=name,
    )(*shards)


def _reduce_scatter(grads, small, name):
    n, ns = len(grads), len(small)
    shapes = [g.shape[1:] for g in grads]

    def body(*refs):
        g = refs[:n]
        out = refs[n + ns:2 * n + ns]
        scr = refs[2 * (n + ns):]
        own, loc, r1, r2 = scr[:n], scr[n:2 * n], scr[2 * n:3 * n], scr[3 * n:4 * n]
        load_sems, s1, q1, s2, q2 = scr[4 * n:4 * n + 5]
        gather_start, gather_forward, gather_finish = _gather_steps(
            refs[n:n + ns], refs[2 * n + ns:2 * (n + ns)], *scr[4 * n + 5:])
        me, sib, chips = _place()
        c = me[2]

        gather_start()
        loads = []
        sends = []
        for a in range(n):
            ld = [pltpu.make_async_copy(g[a].at[_block(*chip, c)], loc[a].at[j], load_sems.at[a, j])
                  for j, chip in enumerate(chips)]
            ld.append(pltpu.make_async_copy(g[a].at[_block(*me)], own[a], load_sems.at[a, 3]))
            for cp in ld:
                cp.start()
            loads.append(ld)
            blocks = [(*chip, 1 - c) for chip in chips] + [sib]
            for j, blk in enumerate(blocks):
                cp = pltpu.make_async_remote_copy(
                    src_ref=g[a].at[_block(*blk)], dst_ref=r1[a].at[j],
                    send_sem=s1.at[a, j], recv_sem=q1.at[a, j], device_id=sib, device_id_type=MESH)
                cp.start()
                sends.append(cp)

        def from_sibling(a, j):
            return pltpu.make_async_remote_copy(
                src_ref=r1[a].at[j], dst_ref=r1[a].at[j], send_sem=s1.at[a, j], recv_sem=q1.at[a, j],
                device_id=sib, device_id_type=MESH)

        def partial(a, j, chip):
            return pltpu.make_async_remote_copy(
                src_ref=loc[a].at[j], dst_ref=r2[a].at[j], send_sem=s2.at[a, j], recv_sem=q2.at[a, j],
                device_id=(*chip, c), device_id_type=MESH)

        gather_forward()
        for a in range(n):
            for j, chip in enumerate(chips):
                loads[a][j].wait()
                from_sibling(a, j).wait_recv()
                loc[a][j] = (loc[a][j].astype(F32) + r1[a][j].astype(F32)).astype(BF16)
                cp = partial(a, j, chip)
                cp.start()
                sends.append(cp)
        for a in range(n):
            loads[a][3].wait()
            from_sibling(a, 3).wait_recv()
            acc = own[a][...].astype(F32) + r1[a][3].astype(F32)
            for j, chip in enumerate(chips):
                partial(a, j, chip).wait_recv()
                acc = acc + r2[a][j].astype(F32)
            out[a][...] = acc
        for cp in sends:
            cp.wait_send()
        gather_finish()

    any_spec = pl.BlockSpec(memory_space=pl.ANY)
    vmem_spec = pl.BlockSpec(memory_space=pltpu.VMEM)
    res = pl.pallas_call(
        body,
        in_specs=[any_spec] * (n + ns),
        out_specs=[vmem_spec] * n + [any_spec] * ns,
        out_shape=([jax.ShapeDtypeStruct(sh, F32) for sh in shapes]
                   + [jax.ShapeDtypeStruct((N_DEV,) + sm.shape, sm.dtype) for sm in small]),
        scratch_shapes=(
            [pltpu.VMEM(sh, BF16) for sh in shapes]
            + [pltpu.VMEM((3,) + sh, BF16) for sh in shapes]
            + [pltpu.VMEM((4,) + sh, BF16) for sh in shapes]
            + [pltpu.VMEM((3,) + sh, BF16) for sh in shapes]
            + [pltpu.SemaphoreType.DMA((n, 4)),
               pltpu.SemaphoreType.DMA((n, 4)), pltpu.SemaphoreType.DMA((n, 4)),
               pltpu.SemaphoreType.DMA((n, 3)), pltpu.SemaphoreType.DMA((n, 3))]
            + _gather_sems(ns)
        ),
        compiler_params=pltpu.CompilerParams(vmem_limit_bytes=V7X_VMEM_LIMIT),
        name=name,
    )(*grads, *small)
    return res[:n], res[n:]


def _pair_add(grads, from_sib, blks, name):
    n = len(grads)

    def body(blk_ref, *refs):
        for a in range(n):
            refs[2 * n + a][...] = (refs[a][...].astype(F32) + refs[n + a][...].astype(F32)).astype(BF16)

    mine = [pl.BlockSpec((None,) + g.shape[1:], lambda j, b: (b[j], 0, 0)) for g in grads]
    same = [pl.BlockSpec((None,) + g.shape[1:], lambda j, b: (j, 0, 0)) for g in grads]
    return pl.pallas_call(
        body,
        grid_spec=pltpu.PrefetchScalarGridSpec(
            num_scalar_prefetch=1, grid=(4,), in_specs=mine + same, out_specs=same),
        out_shape=[jax.ShapeDtypeStruct((4,) + g.shape[1:], BF16) for g in grads],
        compiler_params=_params(("arbitrary",)),
        name=name,
    )(blks, *grads, *from_sib)


def _chip_sum(parts, from_chips, name):
    n = len(parts)

    def body(*refs):
        for a in range(n):
            acc = refs[a][...].astype(F32)
            for j in range(3):
                acc = acc + refs[n + a][j].astype(F32)
            refs[2 * n + a][...] = acc

    half = [p.shape[1] // 2 for p in parts]
    return pl.pallas_call(
        body,
        grid=(2,),
        in_specs=([pl.BlockSpec((None, h, p.shape[2]), lambda i: (3, i, 0)) for p, h in zip(parts, half)]
                  + [pl.BlockSpec((3, h, p.shape[2]), lambda i: (0, i, 0)) for p, h in zip(parts, half)]),
        out_specs=[pl.BlockSpec((h, p.shape[2]), lambda i: (i, 0)) for p, h in zip(parts, half)],
        out_shape=[jax.ShapeDtypeStruct(p.shape[1:], F32) for p in parts],
        compiler_params=_params(("arbitrary",)),
        name=name,
    )(*parts, *from_chips)


def _adam_math(w, g, m, v):
    nm = ADAM_B1 * m + (1.0 - ADAM_B1) * g
    nv = ADAM_B2 * v + (1.0 - ADAM_B2) * (g * g)
    m_hat = nm / (1.0 - ADAM_B1 ** ADAM_STEP)
    v_hat = nv / (1.0 - ADAM_B2 ** ADAM_STEP)
    return -ADAM_LR * (m_hat / (jnp.sqrt(v_hat) + ADAM_EPS) + ADAM_WD * w), nm, nv


def _sum_adamw(part, from_chips, w, m, v, name):
    r, c = w.shape
    half = r // 2

    def body(p_ref, f_ref, w_ref, m_ref, v_ref, g_ref, d_ref, nm_ref, nv_ref):
        g = p_ref[...].astype(F32)
        for j in range(3):
            g = g + f_ref[j].astype(F32)
        g_ref[...] = g
        d_ref[...], nm_ref[...], nv_ref[...] = _adam_math(w_ref[...], g, m_ref[...], v_ref[...])

    spec = pl.BlockSpec((half, c), lambda i: (i, 0))
    return pl.pallas_call(
        body,
        grid=(2,),
        in_specs=[pl.BlockSpec((None, half, c), lambda i: (3, i, 0)),
                  pl.BlockSpec((3, half, c), lambda i: (0, i, 0)), spec, spec, spec],
        out_specs=[spec] * 4,
        out_shape=[jax.ShapeDtypeStruct((r, c), F32)] * 4,
        compiler_params=_params(("arbitrary",)),
        name=name,
    )(part, from_chips, w, m, v)


def _small_update(gathered, layout, params, name):
    ng, npar = len(gathered), len(params)

    def body(*refs):
        parts = refs[:ng]
        prm = refs[ng:ng + 3 * npar]
        tot_refs = refs[ng + 3 * npar:2 * ng + 3 * npar]
        out = refs[2 * ng + 3 * npar:]
        tots = []
        for a in range(ng):
            acc = parts[a][0]
            for d in range(1, N_DEV):
                acc = acc + parts[a][d]
            tot_refs[a][...] = acc
            tots.append(acc)
        for i, (a, row, width) in enumerate(layout):
            g = tots[a] if row is None else tots[a][row:row + 1, :width]
            delta, nm, nv = _adam_math(prm[3 * i][...], g, prm[3 * i + 1][...], prm[3 * i + 2][...])
            out[4 * i][...] = g
            out[4 * i + 1][...] = delta
            out[4 * i + 2][...] = nm
            out[4 * i + 3][...] = nv

    flat = [t for prm in params for t in prm]
    res = pl.pallas_call(
        body,
        out_shape=([jax.ShapeDtypeStruct(g.shape[1:], F32) for g in gathered]
                   + [jax.ShapeDtypeStruct(prm[0].shape, F32) for prm in params for _ in range(4)]),
        compiler_params=pltpu.CompilerParams(vmem_limit_bytes=V7X_VMEM_LIMIT),
        name=name,
    )(*gathered, *flat)
    return res[:ng], [tuple(res[ng + 4 * i:ng + 4 * i + 4]) for i in range(npar)]


def _adamw(w, g, m, v, name):
    rows, cols = w.shape
    br = rows
    for cand in (512, 256, 128):
        if rows % cand == 0 and rows > cand:
            br = cand
            break

    def body(w_ref, g_ref, m_ref, v_ref, d_ref, nm_ref, nv_ref):
        d_ref[...], nm_ref[...], nv_ref[...] = _adam_math(w_ref[...], g_ref[...], m_ref[...], v_ref[...])

    spec = pl.BlockSpec((br, cols), lambda i: (i, 0))
    shape = jax.ShapeDtypeStruct((rows, cols), F32)
    return pl.pallas_call(
        body,
        grid=(rows // br,),
        in_specs=[spec] * 4,
        out_specs=[spec] * 3,
        out_shape=[shape] * 3,
        compiler_params=_params(("arbitrary",)),
        name=name,
    )(w, g, m, v)


def _by_device(full):
    return full.reshape(N_DEV, full.shape[0] // N_DEV, full.shape[1])


def kernel(x, p, g_mix, w_in, conv_w, conv_b, ln_g, ln_b, pool_w, pool_scale, w_out, g_ffn, w_gate_up, w_down, g_ple_gate, w_ple_gate, w_ple_up, g_ple_post, g_final, loss_target, m_g_mix, m_w_in, m_conv_w, m_conv_b, m_ln_g, m_ln_b, m_pool_w, m_pool_scale, m_w_out, m_g_ffn, m_w_gate_up, m_w_down, m_g_ple_gate, m_w_ple_gate, m_w_ple_up, m_g_ple_post, m_g_final, v_g_mix, v_w_in, v_conv_w, v_conv_b, v_ln_g, v_ln_b, v_pool_w, v_pool_scale, v_w_out, v_g_ffn, v_w_gate_up, v_w_down, v_g_ple_gate, v_w_ple_gate, v_w_ple_up, v_g_ple_post, v_g_final):
    seq = x.shape[1]
    xs = x[0]
    ps = p[0, 0]
    tgt = loss_target[0]
    ax, ay, ac = lax.axis_index("x"), lax.axis_index("y"), lax.axis_index("c")
    me = _block(ax, ay, ac)
    blks = jnp.stack([_block(1 - ax, ay, ac), _block(ax, 1 - ay, ac), _block(1 - ax, 1 - ay, ac), me]).astype(jnp.int32)
    rows = lambda gth: gth.reshape((-1,) + gth.shape[2:])

    w_in_t, w_out_f, conv_w_t = [rows(gth) for gth in _all_gather([
        w_in[0].T.astype(BF16),
        w_out[0].astype(BF16),
        jnp.pad(conv_w[0].T, ((0, 0), (0, CONV_HALO - CONV_K))),
    ], "gather_first")]
    conv_w_f = conv_w_t.T
    pool_w_b = pool_w[0].astype(BF16)

    carry = _Carry()
    _carry_gather(carry, [w_gate_up[0].T.astype(BF16)], 8, 12)
    (z, u1, pooled, x1), (w_gu_all,) = _fwd_mix(
        xs, w_in_t, w_out_f, g_mix, conv_w_f, conv_b, ln_g, ln_b, pool_w_b, pool_scale, min(256, seq), carry)
    w_gu_t = rows(w_gu_all)

    carry = _Carry()
    _carry_gather(carry, [w_down[0].astype(BF16)], 5, 8)
    (h2, g_sav, u_sav, act), (w_down_all,) = _ffn_up(x1, w_gu_t, g_ffn, min(256, seq), carry)
    w_down_f = rows(w_down_all)

    carry = _Carry()
    _carry_gather(carry, [w_ple_gate[0].astype(BF16),
                          w_ple_up[0].T.astype(BF16)], 3, 5)
    (x2,), late = _ffn_down(x1, act, w_down_f, min(512, seq), carry)
    w_pg_f, w_pu_t = [rows(gth) for gth in late]
    dx2, dx2b, hg, ds, dpe, pb, stats_ple = _ple_loss(x2, ps, tgt, w_pg_f, w_pu_t, g_ple_gate, g_ple_post,
                                                      g_final.reshape(1, D_MODEL), min(256, seq))
    dg, du, dx1, dx1b, stats_ffn = _ffn_bwd(dx2, dx2b, x1, g_sav, u_sav, w_gu_t, w_down_f, g_ffn, min(256, seq))

    d_w_pg, _ = _grad_matmul(hg, ds, 256, "grad_w_ple_gate")
    d_w_pu_t, _ = _grad_matmul(dpe, pb, 256, "grad_w_ple_up")
    d_w_gu_t, _ = _grad_matmul(dg, h2, 256, "grad_w_gate_up", a2=du)
    early = [_by_device(d_w_pg), _by_device(d_w_pu_t), _by_device(d_w_gu_t)]
    carry = _Carry()
    _carry_pair(carry, early)
    d_w_down, from_sib = _grad_matmul(act, dx2b, 256, "grad_w_down", carry=carry)
    early_parts = _pair_add(early, from_sib, blks, "pair_add_early")

    carry = _Carry()
    _carry_chip(carry, early_parts)
    _carry_pair(carry, [_by_device(d_w_down)])
    (dx, dz, mix, h1, vec_mix, dconv_w_part, dpool_w_part), carried = _bwd_mix(
        dx1, dx1b, xs, z, u1, pooled, w_in_t, w_out_f, g_mix, conv_w_f, ln_g, ln_b, pool_w_b, pool_scale,
        min(256, seq), carry)
    early_chips, down_sib = carried[:3], carried[3:]
    down_parts = _pair_add([_by_device(d_w_down)], down_sib, blks, "pair_add_down")

    carry = _Carry()
    _carry_chip(carry, down_parts)
    d_w_in_t, down_chips = _grad_matmul(dz, h1, 256, "grad_w_in", carry=carry)
    d_w_out, _ = _grad_matmul(mix, dx1b, 256, "grad_w_out")
    (gr_w_in_t, gr_w_out), small = _reduce_scatter(
        [_by_device(d_w_in_t), _by_device(d_w_out)],
        [vec_mix, stats_ple, stats_ffn, dconv_w_part, dpool_w_part], "scatter_last")
    (gr_w_pu_t,) = _chip_sum(early_parts[1:2], early_chips[1:2], "chip_sum")

    vec_names = ["g_mix", "ln_g", "ln_b", "conv_b", "pool_scale", "g_final", "g_ple_post", "g_ple_gate", "g_ffn",
                 "pool_w"]
    layout = [(0, 0, D_MODEL), (0, 1, C_CONV), (0, 2, C_CONV), (0, 3, C_CONV), (0, 4, C_POOL),
              (1, 0, D_MODEL), (1, 1, D_MODEL), (1, 2, D_MODEL), (2, 0, D_MODEL), (4, None, None)]
    as_row = lambda t: t.reshape(1, D_MODEL)
    params = [(g_mix, m_g_mix, v_g_mix), (ln_g, m_ln_g, v_ln_g), (ln_b, m_ln_b, v_ln_b),
              (conv_b, m_conv_b, v_conv_b), (pool_scale, m_pool_scale, v_pool_scale),
              (as_row(g_final), as_row(m_g_final), as_row(v_g_final)),
              (g_ple_post, m_g_ple_post, v_g_ple_post), (g_ple_gate, m_g_ple_gate, v_g_ple_gate),
              (g_ffn, m_g_ffn, v_g_ffn), (pool_w[0], m_pool_w[0], v_pool_w[0])]
    tots, small_upd = _small_update(small, layout, params, "small_update")
    loss = tots[1][3, 0]
    upd = {}
    for nm, res, prm in zip(vec_names, small_upd, [g_mix, ln_g, ln_b, conv_b, pool_scale, g_final, g_ple_post,
                                                    g_ple_gate, g_ffn, pool_w]):
        upd[nm] = tuple(t.reshape(prm.shape) for t in res)
    gr_conv_w = lax.dynamic_slice_in_dim(tots[3][:CONV_K], me * (C_CONV // N_DEV), C_CONV // N_DEV, axis=1)

    fused = [
        ("w_ple_gate", early_parts[0], early_chips[0], w_ple_gate[0], m_w_ple_gate[0], v_w_ple_gate[0], False),
        ("w_gate_up", early_parts[2], early_chips[2], w_gate_up[0].T, m_w_gate_up[0].T, v_w_gate_up[0].T, True),
        ("w_down", down_parts[0], down_chips[0], w_down[0], m_w_down[0], v_w_down[0], False),
    ]
    for nm, part, chips3, w_, m_, v_, transposed in fused:
        res = _sum_adamw(part, chips3, w_, m_, v_, "adamw_" + nm)
        upd[nm] = tuple((t.T if transposed else t)[None] for t in res)
    plain = [
        ("w_in", w_in[0].T, gr_w_in_t, m_w_in[0].T, v_w_in[0].T, True),
        ("w_out", w_out[0], gr_w_out, m_w_out[0], v_w_out[0], False),
        ("w_ple_up", w_ple_up[0], gr_w_pu_t.T, m_w_ple_up[0], v_w_ple_up[0], False),
        ("conv_w", conv_w[0], gr_conv_w, m_conv_w[0], v_conv_w[0], False),
    ]
    for nm, w_, g_, m_, v_, transposed in plain:
        res = (g_,) + tuple(_adamw(w_, g_, m_, v_, "adamw_" + nm))
        upd[nm] = tuple((t.T if transposed else t)[None] for t in res)

    order = ["g_mix", "w_in", "conv_w", "conv_b", "ln_g", "ln_b", "pool_w", "pool_scale", "w_out", "g_ffn",
             "w_gate_up", "w_down", "g_ple_gate", "w_ple_gate", "w_ple_up", "g_ple_post", "g_final"]
    outs = [loss, dx[None]]
    for k in range(4):
        outs += [upd[nm][k] for nm in order]
    return tuple(outs)
```

```python
import functools

import jax
import jax.numpy as jnp
from jax import lax
from jax.experimental import pallas as pl
from jax.experimental.pallas import tpu as pltpu

D_MODEL = 1024
C_CONV = 512
C_POOL = 512
Z_WIDTH = 2 * C_CONV + C_POOL
POOL_WINDOWS = (2, 4, 8, 16)
POOL_GROUP = 128
CONV_K = 31
D_FF = 2816
D_PLE = 256
EPS = 1e-6
N_DEV = 8

ADAM_LR = 0.001
ADAM_B1 = 0.9
ADAM_B2 = 0.999
ADAM_EPS = 1e-08
ADAM_WD = 0.01
ADAM_STEP = 10

CONV_HALO = 32
POOL_HALO = 16
ROW_CHUNK = 32
V7X_VMEM_LIMIT = 56 * 1024 * 1024
FF_CHUNK = D_FF // 2

BF16 = jnp.bfloat16
F32 = jnp.float32
MESH = pl.DeviceIdType.MESH


def _dot(a, b):
    return lax.dot_general(a, b, (((1,), (0,)), ((), ())), preferred_element_type=F32)


def _dot_nt(a, b):
    return lax.dot_general(a, b, (((1,), (1,)), ((), ())), preferred_element_type=F32)


def _dot_tn(a, b):
    return lax.dot_general(a, b, (((0,), (0,)), ((), ())), preferred_element_type=F32)


def _rms_fwd(x, g):
    r = lax.rsqrt(jnp.mean(x * x, axis=-1, keepdims=True) + EPS)
    return x * r * g, r


def _rms_bwd(x, r, g, dy):
    xr = x * r
    dg = jnp.sum(dy * xr, axis=0, keepdims=True)
    dyg = dy * g
    dx = r * (dyg - xr * jnp.mean(dyg * xr, axis=-1, keepdims=True))
    return dx, dg


def _sigmoid(x):
    return jax.nn.sigmoid(x)


def _params(sem=None):
    return pltpu.CompilerParams(dimension_semantics=sem, vmem_limit_bytes=V7X_VMEM_LIMIT)


def _place():
    x, y, c = lax.axis_index("x"), lax.axis_index("y"), lax.axis_index("c")
    chips = [(1 - x, y), (x, 1 - y), (1 - x, 1 - y)]
    return (x, y, c), (x, y, 1 - c), chips


def _block(px, py, pc):
    return 4 * px + 2 * py + pc


class _Carry:
    def __init__(self):
        self.inputs = []
        self.out_shapes = []
        self.copies = []
        self.locals = []

    def add_input(self, arr):
        self.inputs.append(arr)
        return len(self.inputs) - 1

    def add_output(self, shape, dtype):
        self.out_shapes.append(jax.ShapeDtypeStruct(shape, dtype))
        return len(self.out_shapes) - 1

    def local(self, src_idx, dst_idx, dst_blk):
        self.locals.append((src_idx, dst_idx, dst_blk))

    def copy(self, src, dst_idx, dst_blk, got_blk, peer, step=0, after=()):
        self.copies.append(dict(src=src, dst_idx=dst_idx, dst_blk=dst_blk, got_blk=got_blk, peer=peer, step=step,
                                after=tuple(after)))
        return len(self.copies) - 1

    def sem_shapes(self):
        return [pltpu.SemaphoreType.DMA((max(1, len(self.copies)),)),
                pltpu.SemaphoreType.DMA((max(1, len(self.copies)),)),
                pltpu.SemaphoreType.DMA((max(1, len(self.locals)),))]

    @staticmethod
    def _view(ref, where):
        if isinstance(where, tuple):
            blk, row0, nrows = where
            return ref.at[blk, pl.ds(row0, nrows)]
        return ref.at[where]

    def _desc(self, k, ins, outs, sems, place):
        cp = self.copies[k]
        me, sib, chips = place
        kind, idx, blk = cp["src"]
        src = (ins if kind == "in" else outs)[idx]
        if blk is not None:
            src = self._view(src, blk(*place))
        to = sib if cp["peer"] == "sib" else (*chips[cp["peer"]], me[2])
        return pltpu.make_async_remote_copy(
            src_ref=src, dst_ref=self._view(outs[cp["dst_idx"]], cp["dst_blk"](*place)),
            send_sem=sems[0].at[k], recv_sem=sems[1].at[k], device_id=to, device_id_type=MESH)

    def _arrival(self, k, outs, sems, place):
        cp = self.copies[k]
        got = self._view(outs[cp["dst_idx"]], cp["got_blk"](*place))
        return pltpu.make_async_remote_copy(
            src_ref=got, dst_ref=got, send_sem=sems[0].at[k], recv_sem=sems[1].at[k],
            device_id=place[0], device_id_type=MESH)

    def _local(self, n, ins, outs, sems, place):
        src_idx, dst_idx, blk = self.locals[n]
        return pltpu.make_async_copy(ins[src_idx], outs[dst_idx].at[blk(*place)], sems[2].at[n])

    def starts(self, step, nsteps, ins, outs, sems):
        place = _place()
        self._waited = set()
        for s in sorted({0} | {cp["step"] for cp in self.copies}):
            ks = [k for k, cp in enumerate(self.copies) if cp["step"] == s]

            @pl.when(step == min(s, nsteps - 1))
            def _(s=s, ks=ks):
                if s == 0:
                    for n in range(len(self.locals)):
                        self._local(n, ins, outs, sems, place).start()
                for k in ks:
                    for a in self.copies[k]["after"]:
                        if a not in self._waited:
                            self._arrival(a, outs, sems, place).wait_recv()
                            self._waited.add(a)
                    self._desc(k, ins, outs, sems, place).start()

    def finish(self, step, nsteps, ins, outs, sems):
        place = _place()

        @pl.when(step == nsteps - 1)
        def _():
            for k in range(len(self.copies)):
                if k not in self._waited:
                    self._arrival(k, outs, sems, place).wait_recv()
            for k in range(len(self.copies)):
                self._desc(k, ins, outs, sems, place).wait_send()
            for n in range(len(self.locals)):
                self._local(n, ins, outs, sems, place).wait()


def _const_blk(j):
    return lambda me, sib, chips: j


def _carry_gather(carry, shards, relay_step, last_step):
    outs = []
    for sh in shards:
        i = carry.add_input(sh)
        o = carry.add_output((N_DEV,) + sh.shape, sh.dtype)
        half = sh.shape[0] // 2
        tile = 16 if sh.dtype == BF16 else 8
        split = half % tile == 0
        rows = [(0, half), (half, sh.shape[0] - half)] if split else [(0, sh.shape[0]), None]

        def whole(j, core):
            return lambda me, sib, chips, j=j, core=core: _block(*chips[j], me[2] if core == 0 else 1 - me[2])

        def part(j, core, h, rows=rows):
            return lambda me, sib, chips: (_block(*chips[j], me[2] if core == 0 else 1 - me[2]),) + rows[h]

        mine = lambda me, sib, chips: _block(*me)
        carry.local(i, o, mine)
        carry.copy(("in", i, None), o, mine, lambda me, sib, chips: _block(*sib), "sib")
        near = [carry.copy(("in", i, None), o, mine, whole(j, 0), j) for j in range(2)]
        for j in range(2):
            carry.copy(("out", o, whole(j, 0)), o, whole(j, 0), whole(j, 1), "sib", step=relay_step, after=(near[j],))
        for j in range(2):
            if rows[j] is None:
                continue
            far = carry.copy(("out", o, part(j, 0, j)), o, part(j, 0, j), part(2, 0, j), 1 - j,
                             step=relay_step, after=(near[j],))
            carry.copy(("out", o, part(2, 0, j)), o, part(2, 0, j), part(2, 1, j), "sib", step=last_step, after=(far,))
        outs.append(o)
    return outs


def _carry_pair(carry, grads):
    outs = []
    for g in grads:
        i = carry.add_input(g)
        o = carry.add_output((4,) + g.shape[1:], g.dtype)
        for j in range(4):
            if j < 3:
                blk = lambda me, sib, chips, j=j: _block(*chips[j], 1 - me[2])
            else:
                blk = lambda me, sib, chips: _block(*sib)
            carry.copy(("in", i, blk), o, _const_blk(j), _const_blk(j), "sib")
        outs.append(o)
    return outs


def _carry_chip(carry, parts):
    outs = []
    for p in parts:
        i = carry.add_input(p)
        o = carry.add_output((3,) + p.shape[1:], p.dtype)
        for j in range(3):
            carry.copy(("in", i, _const_blk(j)), o, _const_blk(j), _const_blk(j), j)
        outs.append(o)
    return outs


def _pcall(body, *, grid, in_specs, out_specs, out_shape, scratch_shapes, name, args, carry=None):
    sem = ("arbitrary",) * len(grid)
    if carry is None:
        res = pl.pallas_call(body, grid=grid, in_specs=in_specs, out_specs=out_specs, out_shape=out_shape,
                             scratch_shapes=scratch_shapes, compiler_params=_params(sem), name=name)(*args)
        return list(res), []
    n_in, n_out, n_scr = len(in_specs), len(out_specs), len(scratch_shapes)
    c_in, c_out = len(carry.inputs), len(carry.out_shapes)
    nsteps = 1
    for extent in grid:
        nsteps *= extent

    def wrapped(*refs):
        ins = refs[:n_in]
        cins = refs[n_in:n_in + c_in]
        o0 = n_in + c_in
        outs = refs[o0:o0 + n_out]
        couts = refs[o0 + n_out:o0 + n_out + c_out]
        s0 = o0 + n_out + c_out
        scr = refs[s0:s0 + n_scr]
        sems = refs[s0 + n_scr:]
        step = pl.program_id(0)
        for d in range(1, len(grid)):
            step = step * grid[d] + pl.program_id(d)
        carry.starts(step, nsteps, cins, couts, sems)
        body(*ins, *outs, *scr)
        carry.finish(step, nsteps, cins, couts, sems)

    any_spec = pl.BlockSpec(memory_space=pl.ANY)
    res = pl.pallas_call(
        wrapped, grid=grid,
        in_specs=list(in_specs) + [any_spec] * c_in,
        out_specs=list(out_specs) + [any_spec] * c_out,
        out_shape=list(out_shape) + carry.out_shapes,
        scratch_shapes=list(scratch_shapes) + carry.sem_shapes(),
        compiler_params=_params(sem), name=name)(*args, *carry.inputs)
    return list(res[:n_out]), list(res[n_out:])


def _shifted_copies(buf, shifted, tm):
    span = tm + CONV_HALO - 8
    for r in range(1, 8):
        shifted[r - 1, 0:span, :] = buf[r:r + span, :]


def _rows_at(buf, shifted, start):
    aligned, r = (start // 8) * 8, start % 8
    if r == 0:
        return buf[aligned:aligned + ROW_CHUNK, :]
    return shifted[r - 1, aligned:aligned + ROW_CHUNK, :]


def _row_spec(tm, width):
    return pl.BlockSpec((tm, width), lambda i: (i, 0))


def _const_spec(shape):
    return pl.BlockSpec(shape, lambda i: (0,) * len(shape))


def _weight_spec(shape):
    return pl.BlockSpec(shape, lambda i: (0,) * len(shape), pipeline_mode=pl.Buffered(1))


def _fwd_mix(x, w_in_t, w_out, g_mix, conv_w, conv_b, ln_g, ln_b, pool_w, pool_scale, tm, carry=None):
    s = x.shape[0]
    nt = s // tm

    def body(x_ref, win_ref, wout_ref, gmix_ref, cw_ref, cb_ref, lng_ref, lnb_ref, pw_ref, ps_ref,
             z_ref, u1_ref, pooled_ref, x1_ref, mix_ref, ubuf, vbuf, ush):
        i = pl.program_id(0)

        @pl.when(i == 0)
        def _():
            ubuf[0:CONV_HALO, :] = jnp.zeros((CONV_HALO, C_CONV), F32)
            vbuf[0:POOL_HALO, :] = jnp.zeros((POOL_HALO, C_POOL), F32)

        xt = x_ref[...]
        h, _ = _rms_fwd(xt, gmix_ref[...])
        z = _dot_nt(h.astype(BF16), win_ref[...])
        z_ref[...] = z
        a = z[:, :C_CONV]
        b = z[:, C_CONV:2 * C_CONV]
        v = z[:, 2 * C_CONV:]
        ubuf[CONV_HALO:CONV_HALO + tm, :] = a * _sigmoid(b)
        vbuf[POOL_HALO:POOL_HALO + tm, :] = v

        _shifted_copies(ubuf, ush, tm)
        for rc in range(tm // ROW_CHUNK):
            base = rc * ROW_CHUNK + CONV_HALO - (CONV_K - 1)
            acc = jnp.broadcast_to(cb_ref[...], (ROW_CHUNK, C_CONV))
            for k in range(CONV_K):
                acc = acc + cw_ref[k:k + 1, :] * _rows_at(ubuf, ush, base + k)
            u1_ref[rc * ROW_CHUNK:(rc + 1) * ROW_CHUNK, :] = acc

        u1 = u1_ref[...]
        mu = jnp.mean(u1, axis=-1, keepdims=True)
        cen = u1 - mu
        rstd = lax.rsqrt(jnp.mean(cen * cen, axis=-1, keepdims=True) + EPS)
        u2 = cen * rstd * lng_ref[...] + lnb_ref[...]
        u = u2 * _sigmoid(u2)

        pos1 = (i * tm + lax.broadcasted_iota(jnp.int32, (tm, 1), 0) + 1).astype(F32)
        parts = [u]
        for g, w in enumerate(POOL_WINDOWS):
            cols = slice(g * POOL_GROUP, (g + 1) * POOL_GROUP)
            vg = v[:, cols]
            tot = vg
            for j in range(1, w):
                tot = tot + vbuf[POOL_HALO - j:POOL_HALO - j + tm, cols]
            pooled = tot / jnp.minimum(pos1, float(w)) - vg
            pooled_b = pooled.astype(BF16)
            pooled_ref[:, cols] = pooled_b
            parts.append(_dot(pooled_b, pw_ref[g]) * ps_ref[:, cols])
        mix = jnp.concatenate(parts, axis=-1).astype(BF16)
        mix_ref[...] = mix
        x1_ref[...] = xt + _dot(mix, wout_ref[...])

        ubuf[0:CONV_HALO, :] = ubuf[tm:tm + CONV_HALO, :]
        vbuf[0:POOL_HALO, :] = vbuf[tm:tm + POOL_HALO, :]

    return _pcall(
        body,
        grid=(nt,),
        in_specs=[
            _row_spec(tm, D_MODEL),
            _const_spec((Z_WIDTH, D_MODEL)),
            _const_spec((D_MODEL, D_MODEL)),
            _const_spec((1, D_MODEL)),
            _const_spec((CONV_HALO, C_CONV)),
            _const_spec((1, C_CONV)),
            _const_spec((1, C_CONV)),
            _const_spec((1, C_CONV)),
            _const_spec((len(POOL_WINDOWS), POOL_GROUP, POOL_GROUP)),
            _const_spec((1, C_POOL)),
        ],
        out_specs=[
            _row_spec(tm, Z_WIDTH),
            _row_spec(tm, C_CONV),
            _row_spec(tm, C_POOL),
            _row_spec(tm, D_MODEL),
            _row_spec(tm, D_MODEL),
        ],
        out_shape=[
            jax.ShapeDtypeStruct((s, Z_WIDTH), F32),
            jax.ShapeDtypeStruct((s, C_CONV), F32),
            jax.ShapeDtypeStruct((s, C_POOL), BF16),
            jax.ShapeDtypeStruct((s, D_MODEL), F32),
            jax.ShapeDtypeStruct((s, D_MODEL), BF16),
        ],
        scratch_shapes=[
            pltpu.VMEM((tm + CONV_HALO, C_CONV), F32),
            pltpu.VMEM((tm + POOL_HALO, C_POOL), F32),
            pltpu.VMEM((7, tm + CONV_HALO, C_CONV), F32),
        ],
        name="fwd_mix",
        args=(x, w_in_t, w_out, g_mix, conv_w, conv_b, ln_g, ln_b, pool_w, pool_scale),
        carry=carry,
    )


def _ffn_up(x1, w_gu_t, g_ffn, tm, carry=None):
    s = x1.shape[0]

    def body(x1_ref, w_ref, gffn_ref, h2_ref, g_ref, u_ref, act_ref):
        h, _ = _rms_fwd(x1_ref[...], gffn_ref[...])
        h2 = h.astype(BF16)
        h2_ref[...] = h2
        for c in range(D_FF // FF_CHUNK):
            cols = slice(c * FF_CHUNK, (c + 1) * FF_CHUNK)
            g = _dot_nt(h2, w_ref[c * FF_CHUNK:(c + 1) * FF_CHUNK, :])
            u = _dot_nt(h2, w_ref[D_FF + c * FF_CHUNK:D_FF + (c + 1) * FF_CHUNK, :])
            g_ref[:, cols] = g.astype(BF16)
            u_ref[:, cols] = u.astype(BF16)
            act_ref[:, cols] = (g * _sigmoid(g) * u).astype(BF16)

    return _pcall(
        body,
        grid=(s // tm,),
        in_specs=[_row_spec(tm, D_MODEL), _weight_spec((2 * D_FF, D_MODEL)), _const_spec((1, D_MODEL))],
        out_specs=[_row_spec(tm, D_MODEL), _row_spec(tm, D_FF), _row_spec(tm, D_FF), _row_spec(tm, D_FF)],
        out_shape=[
            jax.ShapeDtypeStruct((s, D_MODEL), BF16),
            jax.ShapeDtypeStruct((s, D_FF), BF16),
            jax.ShapeDtypeStruct((s, D_FF), BF16),
            jax.ShapeDtypeStruct((s, D_FF), BF16),
        ],
        scratch_shapes=[],
        name="ffn_up",
        args=(x1, w_gu_t, g_ffn),
        carry=carry,
    )


def _ffn_down(x1, act, w_down, tm, carry=None):
    s = x1.shape[0]

    def body(x1_ref, act_ref, wd_ref, x2_ref):
        x2_ref[...] = x1_ref[...] + _dot(act_ref[...], wd_ref[...])

    return _pcall(
        body,
        grid=(s // tm,),
        in_specs=[_row_spec(tm, D_MODEL), _row_spec(tm, D_FF), _weight_spec((D_FF, D_MODEL))],
        out_specs=[_row_spec(tm, D_MODEL)],
        out_shape=[jax.ShapeDtypeStruct((s, D_MODEL), F32)],
        scratch_shapes=[],
        name="ffn_down",
        args=(x1, act, w_down),
        carry=carry,
    )


def _ple_loss(x2, p, tgt, w_pg, w_pu_t, g_gate, g_post, g_final, tm):
    s = x2.shape[0]
    nt = s // tm

    def body(x2_ref, p_ref, t_ref, wpg_ref, wpu_ref, gg_ref, gp_ref, gf_ref,
             dx2_ref, dx2b_ref, hg_ref, ds_ref, dpe_ref, pb_ref, stats_ref):
        i = pl.program_id(0)

        @pl.when(i == 0)
        def _():
            stats_ref[...] = jnp.zeros_like(stats_ref)

        x2 = x2_ref[...]
        hg, rg = _rms_fwd(x2, gg_ref[...])
        hg_b = hg.astype(BF16)
        hg_ref[...] = hg_b
        gate = _sigmoid(_dot(hg_b, wpg_ref[...]))
        pb = p_ref[...].astype(BF16)
        pb_ref[...] = pb
        pe = _dot_nt(pb, wpu_ref[...])
        e, rp = _rms_fwd(pe, gp_ref[...])
        x3 = x2 + gate * e
        y, r3 = _rms_fwd(x3, gf_ref[...])
        diff = y - t_ref[...]
        loss = 0.5 * jnp.sum(jnp.sum(diff * diff, axis=-1, keepdims=True), axis=0, keepdims=True) / D_MODEL
        dy = diff * (1.0 / D_MODEL)

        dx3, dgf = _rms_bwd(x3, r3, gf_ref[...], dy)
        dpe, dgp = _rms_bwd(pe, rp, gp_ref[...], dx3 * gate)
        dpe_ref[...] = dpe.astype(BF16)
        ds = (dx3 * e * gate * (1.0 - gate)).astype(BF16)
        ds_ref[...] = ds
        dhg = _dot_nt(ds, wpg_ref[...])
        dxg, dgg = _rms_bwd(x2, rg, gg_ref[...], dhg)
        dx2 = dx3 + dxg
        dx2_ref[...] = dx2
        dx2b_ref[...] = dx2.astype(BF16)

        stats_ref[0:1, :] += dgf
        stats_ref[1:2, :] += dgp
        stats_ref[2:3, :] += dgg
        stats_ref[3:4, :] += jnp.broadcast_to(loss, (1, D_MODEL))

    return pl.pallas_call(
        body,
        grid=(nt,),
        in_specs=[
            _row_spec(tm, D_MODEL),
            _row_spec(tm, D_PLE),
            _row_spec(tm, D_MODEL),
            _const_spec((D_MODEL, D_MODEL)),
            _const_spec((D_MODEL, D_PLE)),
            _const_spec((1, D_MODEL)),
            _const_spec((1, D_MODEL)),
            _const_spec((1, D_MODEL)),
        ],
        out_specs=[
            _row_spec(tm, D_MODEL),
            _row_spec(tm, D_MODEL),
            _row_spec(tm, D_MODEL),
            _row_spec(tm, D_MODEL),
            _row_spec(tm, D_MODEL),
            _row_spec(tm, D_PLE),
            _const_spec((8, D_MODEL)),
        ],
        out_shape=[
            jax.ShapeDtypeStruct((s, D_MODEL), F32),
            jax.ShapeDtypeStruct((s, D_MODEL), BF16),
            jax.ShapeDtypeStruct((s, D_MODEL), BF16),
            jax.ShapeDtypeStruct((s, D_MODEL), BF16),
            jax.ShapeDtypeStruct((s, D_MODEL), BF16),
            jax.ShapeDtypeStruct((s, D_PLE), BF16),
            jax.ShapeDtypeStruct((8, D_MODEL), F32),
        ],
        compiler_params=_params(("arbitrary",)),
        name="ple_loss",
    )(x2, p, tgt, w_pg, w_pu_t, g_gate, g_post, g_final)


def _ffn_bwd(dx2, dx2b, x1, g_sav, u_sav, w_gu_t, w_down, g_ffn, tm):
    s = x1.shape[0]

    def body(dx2_ref, dx2b_ref, x1_ref, g_ref, u_ref, w_ref, wd_ref, gffn_ref,
             dg_ref, du_ref, dx1_ref, dx1b_ref, stats_ref):
        @pl.when(pl.program_id(0) == 0)
        def _():
            stats_ref[...] = jnp.zeros_like(stats_ref)

        dx2b = dx2b_ref[...]
        dh2 = jnp.zeros((tm, D_MODEL), F32)
        for c in range(D_FF // FF_CHUNK):
            cols = slice(c * FF_CHUNK, (c + 1) * FF_CHUNK)
            dact = _dot_nt(dx2b, wd_ref[c * FF_CHUNK:(c + 1) * FF_CHUNK, :])
            g = g_ref[:, cols].astype(F32)
            u = u_ref[:, cols].astype(F32)
            sg = _sigmoid(g)
            dg = (dact * u * sg * (1.0 + g * (1.0 - sg))).astype(BF16)
            du = (dact * g * sg).astype(BF16)
            dg_ref[:, cols] = dg
            du_ref[:, cols] = du
            dh2 = dh2 + _dot(dg, w_ref[c * FF_CHUNK:(c + 1) * FF_CHUNK, :])
            dh2 = dh2 + _dot(du, w_ref[D_FF + c * FF_CHUNK:D_FF + (c + 1) * FF_CHUNK, :])

        x1 = x1_ref[...]
        r2 = lax.rsqrt(jnp.mean(x1 * x1, axis=-1, keepdims=True) + EPS)
        dxn, dgf = _rms_bwd(x1, r2, gffn_ref[...], dh2)
        dx1 = dx2_ref[...] + dxn
        dx1_ref[...] = dx1
        dx1b_ref[...] = dx1.astype(BF16)
        stats_ref[0:1, :] += dgf

    return pl.pallas_call(
        body,
        grid=(s // tm,),
        in_specs=[
            _row_spec(tm, D_MODEL), _row_spec(tm, D_MODEL), _row_spec(tm, D_MODEL),
            _row_spec(tm, D_FF), _row_spec(tm, D_FF),
            _weight_spec((2 * D_FF, D_MODEL)), _weight_spec((D_FF, D_MODEL)), _const_spec((1, D_MODEL)),
        ],
        out_specs=[_row_spec(tm, D_FF), _row_spec(tm, D_FF), _row_spec(tm, D_MODEL), _row_spec(tm, D_MODEL),
                   _const_spec((8, D_MODEL))],
        out_shape=[
            jax.ShapeDtypeStruct((s, D_FF), BF16),
            jax.ShapeDtypeStruct((s, D_FF), BF16),
            jax.ShapeDtypeStruct((s, D_MODEL), F32),
            jax.ShapeDtypeStruct((s, D_MODEL), BF16),
            jax.ShapeDtypeStruct((8, D_MODEL), F32),
        ],
        compiler_params=_params(("arbitrary",)),
        name="ffn_bwd",
    )(dx2, dx2b, x1, g_sav, u_sav, w_gu_t, w_down, g_ffn)


def _bwd_mix(dx1, dx1b, x, z, u1, pooled, w_in_t, w_out, g_mix, conv_w, ln_g, ln_b, pool_w, pool_scale, tm,
             carry=None):
    s = x.shape[0]
    nt = s // tm

    def body(dx1_ref, dx1b_ref, x_ref, z_ref, u1_ref, pooled_ref, win_ref, wout_ref, gmix_ref, cw_ref,
             lng_ref, lnb_ref, pw_ref, ps_ref,
             dx_ref, dz_ref, h1_ref, vec_ref, dcw_ref, dpw_ref, dubuf, dvbuf, u0buf, du0buf, dush):
        i = pl.program_id(0)
        tile = nt - 1 - i

        @pl.when(i == 0)
        def _():
            vec_ref[...] = jnp.zeros_like(vec_ref)
            dcw_ref[...] = jnp.zeros_like(dcw_ref)
            dpw_ref[...] = jnp.zeros_like(dpw_ref)
            dubuf[tm:tm + CONV_HALO, :] = jnp.zeros((CONV_HALO, C_CONV), F32)
            dvbuf[tm:tm + POOL_HALO, :] = jnp.zeros((POOL_HALO, C_POOL), F32)

        dmix = _dot_nt(dx1b_ref[...], wout_ref[...])
        du = dmix[:, :C_CONV]
        dq = dmix[:, C_CONV:]

        pos1 = (tile * tm + lax.broadcasted_iota(jnp.int32, (tm, 1), 0) + 1).astype(F32)
        dpooled_parts = []
        dps_rows = []
        for g, w in enumerate(POOL_WINDOWS):
            cols = slice(g * POOL_GROUP, (g + 1) * POOL_GROUP)
            pooled_b = pooled_ref[:, cols]
            mixed = _dot(pooled_b, pw_ref[g])
            dqg = dq[:, cols]
            dps_rows.append(jnp.sum(dqg * mixed, axis=0, keepdims=True))
            dmixed = (dqg * ps_ref[:, cols]).astype(BF16)
            dpw_ref[g] += _dot_tn(pooled_b, dmixed)
            dpooled = _dot_nt(dmixed, pw_ref[g])
            dpooled_parts.append(dpooled)
            dvbuf[0:tm, cols] = dpooled / jnp.minimum(pos1, float(w))
        vec_ref[4:5, 0:C_POOL] += jnp.concatenate(dps_rows, axis=-1)
        dv_parts = []
        for g, w in enumerate(POOL_WINDOWS):
            cols = slice(g * POOL_GROUP, (g + 1) * POOL_GROUP)
            tot = dvbuf[0:tm, cols]
            for j in range(1, w):
                tot = tot + dvbuf[j:j + tm, cols]
            dv_parts.append(tot - dpooled_parts[g])

        u1 = u1_ref[...]
        mu = jnp.mean(u1, axis=-1, keepdims=True)
        cen = u1 - mu
        rstd = lax.rsqrt(jnp.mean(cen * cen, axis=-1, keepdims=True) + EPS)
        xhat = cen * rstd
        u2 = xhat * lng_ref[...] + lnb_ref[...]
        sg2 = _sigmoid(u2)
        du2 = du * sg2 * (1.0 + u2 * (1.0 - sg2))
        vec_ref[1:2, 0:C_CONV] += jnp.sum(du2 * xhat, axis=0, keepdims=True)
        vec_ref[2:3, 0:C_CONV] += jnp.sum(du2, axis=0, keepdims=True)
        t1 = du2 * lng_ref[...]
        du1 = rstd * (t1 - jnp.mean(t1, axis=-1, keepdims=True)
                      - xhat * jnp.mean(t1 * xhat, axis=-1, keepdims=True))
        vec_ref[3:4, 0:C_CONV] += jnp.sum(du1, axis=0, keepdims=True)
        dubuf[0:tm, :] = du1

        zt = z_ref[...]
        a = zt[:, :C_CONV]
        sgb = _sigmoid(zt[:, C_CONV:2 * C_CONV])
        u0buf[...] = a * sgb

        _shifted_copies(dubuf, dush, tm)
        for rc in range(tm // ROW_CHUNK):
            r0 = rc * ROW_CHUNK
            acc = jnp.zeros((ROW_CHUNK, C_CONV), F32)
            for k in range(CONV_K):
                acc = acc + cw_ref[k:k + 1, :] * _rows_at(dubuf, dush, r0 + (CONV_K - 1) - k)
            du0buf[r0:r0 + ROW_CHUNK, :] = acc
        for k in range(CONV_K):
            acc = jnp.zeros((ROW_CHUNK, C_CONV), F32)
            for rc in range(tm // ROW_CHUNK):
                r0 = rc * ROW_CHUNK
                acc = acc + u0buf[r0:r0 + ROW_CHUNK, :] * _rows_at(dubuf, dush, r0 + (CONV_K - 1) - k)
            dcw_ref[k:k + 1, :] += jnp.sum(acc, axis=0, keepdims=True)
        du0 = du0buf[...]

        da = du0 * sgb
        db = du0 * a * sgb * (1.0 - sgb)
        dz = jnp.concatenate([da, db] + dv_parts, axis=-1).astype(BF16)
        dz_ref[...] = dz

        xt = x_ref[...]
        h1, r1 = _rms_fwd(xt, gmix_ref[...])
        h1_ref[...] = h1.astype(BF16)
        dh1 = _dot(dz, win_ref[...])
        dxn, dgm = _rms_bwd(xt, r1, gmix_ref[...], dh1)
        dx_ref[...] = dx1_ref[...] + dxn
        vec_ref[0:1, :] += dgm

        dubuf[tm:tm + CONV_HALO, :] = dubuf[0:CONV_HALO, :]
        dvbuf[tm:tm + POOL_HALO, :] = dvbuf[0:POOL_HALO, :]

    rev = lambda width: pl.BlockSpec((tm, width), lambda i: (nt - 1 - i, 0))
    return _pcall(
        body,
        grid=(nt,),
        in_specs=[
            rev(D_MODEL), rev(D_MODEL), rev(D_MODEL), rev(Z_WIDTH), rev(C_CONV), rev(C_POOL),
            _const_spec((Z_WIDTH, D_MODEL)),
            _const_spec((D_MODEL, D_MODEL)),
            _const_spec((1, D_MODEL)),
            _const_spec((CONV_HALO, C_CONV)),
            _const_spec((1, C_CONV)),
            _const_spec((1, C_CONV)),
            _const_spec((len(POOL_WINDOWS), POOL_GROUP, POOL_GROUP)),
            _const_spec((1, C_POOL)),
        ],
        out_specs=[
            rev(D_MODEL), rev(Z_WIDTH), rev(D_MODEL),
            _const_spec((8, D_MODEL)),
            _const_spec((CONV_HALO, C_CONV)),
            _const_spec((len(POOL_WINDOWS), POOL_GROUP, POOL_GROUP)),
        ],
        out_shape=[
            jax.ShapeDtypeStruct((s, D_MODEL), F32),
            jax.ShapeDtypeStruct((s, Z_WIDTH), BF16),
            jax.ShapeDtypeStruct((s, D_MODEL), BF16),
            jax.ShapeDtypeStruct((8, D_MODEL), F32),
            jax.ShapeDtypeStruct((CONV_HALO, C_CONV), F32),
            jax.ShapeDtypeStruct((len(POOL_WINDOWS), POOL_GROUP, POOL_GROUP), F32),
        ],
        scratch_shapes=[
            pltpu.VMEM((tm + CONV_HALO, C_CONV), F32),
            pltpu.VMEM((tm + POOL_HALO, C_POOL), F32),
            pltpu.VMEM((tm, C_CONV), F32),
            pltpu.VMEM((tm, C_CONV), F32),
            pltpu.VMEM((7, tm + CONV_HALO, C_CONV), F32),
        ],
        name="bwd_mix",
        args=(dx1, dx1b, x, z, u1, pooled, w_in_t, w_out, g_mix, conv_w, ln_g, ln_b, pool_w, pool_scale),
        carry=carry,
    )


def _grad_matmul(a, b, bm, name, a2=None, carry=None):
    s, ma = a.shape
    nb = b.shape[1]
    na = ma // bm
    if a2 is None:
        def body(a_ref, b_ref, o_ref):
            o_ref[...] = _dot_tn(a_ref[...], b_ref[...]).astype(BF16)

        lhs_specs = [pl.BlockSpec((s, bm), lambda i: (0, i))]
        lhs = (a,)
        steps = na
    else:
        def body(a_ref, a2_ref, b_ref, o_ref):
            i = pl.program_id(0)

            @pl.when(i < na)
            def _():
                o_ref[...] = _dot_tn(a_ref[...], b_ref[...]).astype(BF16)

            @pl.when(i >= na)
            def _():
                o_ref[...] = _dot_tn(a2_ref[...], b_ref[...]).astype(BF16)

        lhs_specs = [pl.BlockSpec((s, bm), lambda i: (0, jnp.minimum(i, na - 1))),
                     pl.BlockSpec((s, bm), lambda i: (0, jnp.maximum(i - na, 0)))]
        lhs = (a, a2)
        steps = 2 * na

    outs, carried = _pcall(
        body,
        grid=(steps,),
        in_specs=lhs_specs + [pl.BlockSpec((s, nb), lambda i: (0, 0))],
        out_specs=[pl.BlockSpec((bm, nb), lambda i: (i, 0))],
        out_shape=[jax.ShapeDtypeStruct((steps * bm, nb), BF16)],
        scratch_shapes=[],
        name=name,
        args=lhs + (b,),
        carry=carry,
    )
    return outs[0], carried


def _gather_steps(src, dst, send_sems, recv_sems, local_sems):
    n = len(src)
    me, sib, chips = _place()
    c = me[2]
    started = []
    mine = []

    def copy(a, k, block, to, from_src=False):
        rows = dst[a].at[_block(*block)]
        return pltpu.make_async_remote_copy(
            src_ref=src[a] if from_src else rows, dst_ref=rows,
            send_sem=send_sems.at[a, k], recv_sem=recv_sems.at[a, k],
            device_id=to, device_id_type=MESH)

    def start():
        for a in range(n):
            cp = pltpu.make_async_copy(src[a], dst[a].at[_block(*me)], local_sems.at[a])
            cp.start()
            mine.append(cp)
            first = [copy(a, 0, me, sib, True)]
            first += [copy(a, 1 + j, me, (*chip, c), True) for j, chip in enumerate(chips)]
            for cp in first:
                cp.start()
            started.extend(first)

    def forward():
        for a in range(n):
            for j, chip in enumerate(chips):
                copy(a, 1 + j, (*chip, c), me).wait_recv()
                fwd = copy(a, 4 + j, (*chip, c), sib)
                fwd.start()
                started.append(fwd)

    def finish():
        for a in range(n):
            copy(a, 0, sib, me).wait_recv()
            for j, chip in enumerate(chips):
                copy(a, 4 + j, (*chip, 1 - c), me).wait_recv()
        for cp in started:
            cp.wait_send()
        for cp in mine:
            cp.wait()

    return start, forward, finish


def _gather_sems(n):
    return [pltpu.SemaphoreType.DMA((n, 7)), pltpu.SemaphoreType.DMA((n, 7)), pltpu.SemaphoreType.DMA((n,))]


def _all_gather(shards, name):
    n = len(shards)
    carry = _Carry()
    _carry_gather(carry, shards, 1, 2)

    def body(*refs):
        step = jnp.int32(0)
        carry.starts(step, 1, refs[:n], refs[n:2 * n], refs[2 * n:])
        carry.finish(step, 1, refs[:n], refs[n:2 * n], refs[2 * n:])

    any_spec = pl.BlockSpec(memory_space=pl.ANY)
    return pl.pallas_call(
        body,
        in_specs=[any_spec] * n,
        out_specs=[any_spec] * n,
        out_shape=carry.out_shapes,
        scratch_shapes=carry.sem_shapes(),
        name=name,
    )(*shards)


def _reduce_scatter(grads, small, name):
    n, ns = len(grads), len(small)
    shapes = [g.shape[1:] for g in grads]

    def body(*refs):
        g = refs[:n]
        out = refs[n + ns:2 * n + ns]
        scr = refs[2 * (n + ns):]
        own, loc, r1, r2 = scr[:n], scr[n:2 * n], scr[2 * n:3 * n], scr[3 * n:4 * n]
        load_sems, s1, q1, s2, q2 = scr[4 * n:4 * n + 5]
        gather_start, gather_forward, gather_finish = _gather_steps(
            refs[n:n + ns], refs[2 * n + ns:2 * (n + ns)], *scr[4 * n + 5:])
        me, sib, chips = _place()
        c = me[2]

        gather_start()
        loads = []
        sends = []
        for a in range(n):
            ld = [pltpu.make_async_copy(g[a].at[_block(*chip, c)], loc[a].at[j], load_sems.at[a, j])
                  for j, chip in enumerate(chips)]
            ld.append(pltpu.make_async_copy(g[a].at[_block(*me)], own[a], load_sems.at[a, 3]))
            for cp in ld:
                cp.start()
            loads.append(ld)
            blocks = [(*chip, 1 - c) for chip in chips] + [sib]
            for j, blk in enumerate(blocks):
                cp = pltpu.make_async_remote_copy(
                    src_ref=g[a].at[_block(*blk)], dst_ref=r1[a].at[j],
                    send_sem=s1.at[a, j], recv_sem=q1.at[a, j], device_id=sib, device_id_type=MESH)
                cp.start()
                sends.append(cp)

        def from_sibling(a, j):
            return pltpu.make_async_remote_copy(
                src_ref=r1[a].at[j], dst_ref=r1[a].at[j], send_sem=s1.at[a, j], recv_sem=q1.at[a, j],
                device_id=sib, device_id_type=MESH)

        def partial(a, j, chip):
            return pltpu.make_async_remote_copy(
                src_ref=loc[a].at[j], dst_ref=r2[a].at[j], send_sem=s2.at[a, j], recv_sem=q2.at[a, j],
                device_id=(*chip, c), device_id_type=MESH)

        gather_forward()
        for a in range(n):
            for j, chip in enumerate(chips):
                loads[a][j].wait()
                from_sibling(a, j).wait_recv()
                loc[a][j] = (loc[a][j].astype(F32) + r1[a][j].astype(F32)).astype(BF16)
                cp = partial(a, j, chip)
                cp.start()
                sends.append(cp)
        for a in range(n):
            loads[a][3].wait()
            from_sibling(a, 3).wait_recv()
            acc = own[a][...].astype(F32) + r1[a][3].astype(F32)
            for j, chip in enumerate(chips):
                partial(a, j, chip).wait_recv()
                acc = acc + r2[a][j].astype(F32)
            out[a][...] = acc
        for cp in sends:
            cp.wait_send()
        gather_finish()

    any_spec = pl.BlockSpec(memory_space=pl.ANY)
    vmem_spec = pl.BlockSpec(memory_space=pltpu.VMEM)
    res = pl.pallas_call(
        body,
        in_specs=[any_spec] * (n + ns),
        out_specs=[vmem_spec] * n + [any_spec] * ns,
        out_shape=([jax.ShapeDtypeStruct(sh, F32) for sh in shapes]
                   + [jax.ShapeDtypeStruct((N_DEV,) + sm.shape, sm.dtype) for sm in small]),
        scratch_shapes=(
            [pltpu.VMEM(sh, BF16) for sh in shapes]
            + [pltpu.VMEM((3,) + sh, BF16) for sh in shapes]
            + [pltpu.VMEM((4,) + sh, BF16) for sh in shapes]
            + [pltpu.VMEM((3,) + sh, BF16) for sh in shapes]
            + [pltpu.SemaphoreType.DMA((n, 4)),
               pltpu.SemaphoreType.DMA((n, 4)), pltpu.SemaphoreType.DMA((n, 4)),
               pltpu.SemaphoreType.DMA((n, 3)), pltpu.SemaphoreType.DMA((n, 3))]
            + _gather_sems(ns)
        ),
        compiler_params=pltpu.CompilerParams(vmem_limit_bytes=V7X_VMEM_LIMIT),
        name=name,
    )(*grads, *small)
    return res[:n], res[n:]


def _pair_add(grads, from_sib, blks, name):
    n = len(grads)

    def body(blk_ref, *refs):
        for a in range(n):
            refs[2 * n + a][...] = (refs[a][...].astype(F32) + refs[n + a][...].astype(F32)).astype(BF16)

    mine = [pl.BlockSpec((None,) + g.shape[1:], lambda j, b: (b[j], 0, 0)) for g in grads]
    same = [pl.BlockSpec((None,) + g.shape[1:], lambda j, b: (j, 0, 0)) for g in grads]
    return pl.pallas_call(
        body,
        grid_spec=pltpu.PrefetchScalarGridSpec(
            num_scalar_prefetch=1, grid=(4,), in_specs=mine + same, out_specs=same),
        out_shape=[jax.ShapeDtypeStruct((4,) + g.shape[1:], BF16) for g in grads],
        compiler_params=_params(("arbitrary",)),
        name=name,
    )(blks, *grads, *from_sib)


def _chip_sum(parts, from_chips, name):
    n = len(parts)

    def body(*refs):
        for a in range(n):
            acc = refs[a][...].astype(F32)
            for j in range(3):
                acc = acc + refs[n + a][j].astype(F32)
            refs[2 * n + a][...] = acc

    half = [p.shape[1] // 2 for p in parts]
    return pl.pallas_call(
        body,
        grid=(2,),
        in_specs=([pl.BlockSpec((None, h, p.shape[2]), lambda i: (3, i, 0)) for p, h in zip(parts, half)]
                  + [pl.BlockSpec((3, h, p.shape[2]), lambda i: (0, i, 0)) for p, h in zip(parts, half)]),
        out_specs=[pl.BlockSpec((h, p.shape[2]), lambda i: (i, 0)) for p, h in zip(parts, half)],
        out_shape=[jax.ShapeDtypeStruct(p.shape[1:], F32) for p in parts],
        compiler_params=_params(("arbitrary",)),
        name=name,
    )(*parts, *from_chips)


def _adam_math(w, g, m, v):
    nm = ADAM_B1 * m + (1.0 - ADAM_B1) * g
    nv = ADAM_B2 * v + (1.0 - ADAM_B2) * (g * g)
    m_hat = nm / (1.0 - ADAM_B1 ** ADAM_STEP)
    v_hat = nv / (1.0 - ADAM_B2 ** ADAM_STEP)
    return -ADAM_LR * (m_hat / (jnp.sqrt(v_hat) + ADAM_EPS) + ADAM_WD * w), nm, nv


def _sum_adamw(part, from_chips, w, m, v, name):
    r, c = w.shape
    half = r // 2

    def body(p_ref, f_ref, w_ref, m_ref, v_ref, g_ref, d_ref, nm_ref, nv_ref):
        g = p_ref[...].astype(F32)
        for j in range(3):
            g = g + f_ref[j].astype(F32)
        g_ref[...] = g
        d_ref[...], nm_ref[...], nv_ref[...] = _adam_math(w_ref[...], g, m_ref[...], v_ref[...])

    spec = pl.BlockSpec((half, c), lambda i: (i, 0))
    return pl.pallas_call(
        body,
        grid=(2,),
        in_specs=[pl.BlockSpec((None, half, c), lambda i: (3, i, 0)),
                  pl.BlockSpec((3, half, c), lambda i: (0, i, 0)), spec, spec, spec],
        out_specs=[spec] * 4,
        out_shape=[jax.ShapeDtypeStruct((r, c), F32)] * 4,
        compiler_params=_params(("arbitrary",)),
        name=name,
    )(part, from_chips, w, m, v)


def _small_update(gathered, layout, params, name):
    ng, npar = len(gathered), len(params)

    def body(*refs):
        parts = refs[:ng]
        prm = refs[ng:ng + 3 * npar]
        tot_refs = refs[ng + 3 * npar:2 * ng + 3 * npar]
        out = refs[2 * ng + 3 * npar:]
        tots = []
        for a in range(ng):
            acc = parts[a][0]
            for d in range(1, N_DEV):
                acc = acc + parts[a][d]
            tot_refs[a][...] = acc
            tots.append(acc)
        for i, (a, row, width) in enumerate(layout):
            g = tots[a] if row is None else tots[a][row:row + 1, :width]
            delta, nm, nv = _adam_math(prm[3 * i][...], g, prm[3 * i + 1][...], prm[3 * i + 2][...])
            out[4 * i][...] = g
            out[4 * i + 1][...] = delta
            out[4 * i + 2][...] = nm
            out[4 * i + 3][...] = nv

    flat = [t for prm in params for t in prm]
    res = pl.pallas_call(
        body,
        out_shape=([jax.ShapeDtypeStruct(g.shape[1:], F32) for g in gathered]
                   + [jax.ShapeDtypeStruct(prm[0].shape, F32) for prm in params for _ in range(4)]),
        compiler_params=pltpu.CompilerParams(vmem_limit_bytes=V7X_VMEM_LIMIT),
        name=name,
    )(*gathered, *flat)
    return res[:ng], [tuple(res[ng + 4 * i:ng + 4 * i + 4]) for i in range(npar)]


def _adamw(w, g, m, v, name):
    rows, cols = w.shape
    br = rows
    for cand in (512, 256, 128):
        if rows % cand == 0 and rows > cand:
            br = cand
            break

    def body(w_ref, g_ref, m_ref, v_ref, d_ref, nm_ref, nv_ref):
        d_ref[...], nm_ref[...], nv_ref[...] = _adam_math(w_ref[...], g_ref[...], m_ref[...], v_ref[...])

    spec = pl.BlockSpec((br, cols), lambda i: (i, 0))
    shape = jax.ShapeDtypeStruct((rows, cols), F32)
    return pl.pallas_call(
        body,
        grid=(rows // br,),
        in_specs=[spec] * 4,
        out_specs=[spec] * 3,
        out_shape=[shape] * 3,
        compiler_params=_params(("arbitrary",)),
        name=name,
    )(w, g, m, v)


def _by_device(full):
    return full.reshape(N_DEV, full.shape[0] // N_DEV, full.shape[1])


def kernel(x, p, g_mix, w_in, conv_w, conv_b, ln_g, ln_b, pool_w, pool_scale, w_out, g_ffn, w_gate_up, w_down, g_ple_gate, w_ple_gate, w_ple_up, g_ple_post, g_final, loss_target, m_g_mix, m_w_in, m_conv_w, m_conv_b, m_ln_g, m_ln_b, m_pool_w, m_pool_scale, m_w_out, m_g_ffn, m_w_gate_up, m_w_down, m_g_ple_gate, m_w_ple_gate, m_w_ple_up, m_g_ple_post, m_g_final, v_g_mix, v_w_in, v_conv_w, v_conv_b, v_ln_g, v_ln_b, v_pool_w, v_pool_scale, v_w_out, v_g_ffn, v_w_gate_up, v_w_down, v_g_ple_gate, v_w_ple_gate, v_w_ple_up, v_g_ple_post, v_g_final):
    seq = x.shape[1]
    xs = x[0]
    ps = p[0, 0]
    tgt = loss_target[0]
    ax, ay, ac = lax.axis_index("x"), lax.axis_index("y"), lax.axis_index("c")
    me = _block(ax, ay, ac)
    blks = jnp.stack([_block(1 - ax, ay, ac), _block(ax, 1 - ay, ac), _block(1 - ax, 1 - ay, ac), me]).astype(jnp.int32)
    rows = lambda gth: gth.reshape((-1,) + gth.shape[2:])

    w_in_t, w_out_f, conv_w_t = [rows(gth) for gth in _all_gather([
        w_in[0].T.astype(BF16),
        w_out[0].astype(BF16),
        jnp.pad(conv_w[0].T, ((0, 0), (0, CONV_HALO - CONV_K))),
    ], "gather_first")]
    conv_w_f = conv_w_t.T
    pool_w_b = pool_w[0].astype(BF16)

    carry = _Carry()
    _carry_gather(carry, [w_gate_up[0].T.astype(BF16)], 8, 12)
    (z, u1, pooled, x1, mix), (w_gu_all,) = _fwd_mix(
        xs, w_in_t, w_out_f, g_mix, conv_w_f, conv_b, ln_g, ln_b, pool_w_b, pool_scale, min(256, seq), carry)
    w_gu_t = rows(w_gu_all)

    carry = _Carry()
    _carry_gather(carry, [w_down[0].astype(BF16)], 5, 8)
    (h2, g_sav, u_sav, act), (w_down_all,) = _ffn_up(x1, w_gu_t, g_ffn, min(256, seq), carry)
    w_down_f = rows(w_down_all)

    carry = _Carry()
    _carry_gather(carry, [w_ple_gate[0].astype(BF16),
                          w_ple_up[0].T.astype(BF16)], 3, 5)
    (x2,), late = _ffn_down(x1, act, w_down_f, min(512, seq), carry)
    w_pg_f, w_pu_t = [rows(gth) for gth in late]
    dx2, dx2b, hg, ds, dpe, pb, stats_ple = _ple_loss(x2, ps, tgt, w_pg_f, w_pu_t, g_ple_gate, g_ple_post,
                                                      g_final.reshape(1, D_MODEL), min(256, seq))
    dg, du, dx1, dx1b, stats_ffn = _ffn_bwd(dx2, dx2b, x1, g_sav, u_sav, w_gu_t, w_down_f, g_ffn, min(256, seq))

    d_w_pg, _ = _grad_matmul(hg, ds, 256, "grad_w_ple_gate")
    d_w_pu_t, _ = _grad_matmul(dpe, pb, 256, "grad_w_ple_up")
    d_w_gu_t, _ = _grad_matmul(dg, h2, 256, "grad_w_gate_up", a2=du)
    d_w_out, _ = _grad_matmul(mix, dx1b, 256, "grad_w_out")
    early = [_by_device(d_w_pg), _by_device(d_w_pu_t), _by_device(d_w_gu_t), _by_device(d_w_out)]
    carry = _Carry()
    _carry_pair(carry, early)
    d_w_down, from_sib = _grad_matmul(act, dx2b, 256, "grad_w_down", carry=carry)
    early_parts = _pair_add(early, from_sib, blks, "pair_add_early")

    carry = _Carry()
    _carry_chip(carry, early_parts)
    _carry_pair(carry, [_by_device(d_w_down)])
    (dx, dz, h1, vec_mix, dconv_w_part, dpool_w_part), carried = _bwd_mix(
        dx1, dx1b, xs, z, u1, pooled, w_in_t, w_out_f, g_mix, conv_w_f, ln_g, ln_b, pool_w_b, pool_scale,
        min(256, seq), carry)
    early_chips, down_sib = carried[:4], carried[4:]
    down_parts = _pair_add([_by_device(d_w_down)], down_sib, blks, "pair_add_down")

    carry = _Carry()
    _carry_chip(carry, down_parts)
    d_w_in_t, down_chips = _grad_matmul(dz, h1, 256, "grad_w_in", carry=carry)
    (gr_w_in_t,), small = _reduce_scatter(
        [_by_device(d_w_in_t)], [vec_mix, stats_ple, stats_ffn, dconv_w_part, dpool_w_part], "scatter_last")
    (gr_w_pu_t,) = _chip_sum(early_parts[1:2], early_chips[1:2], "chip_sum")

    vec_names = ["g_mix", "ln_g", "ln_b", "conv_b", "pool_scale", "g_final", "g_ple_post", "g_ple_gate", "g_ffn",
                 "pool_w"]
    layout = [(0, 0, D_MODEL), (0, 1, C_CONV), (0, 2, C_CONV), (0, 3, C_CONV), (0, 4, C_POOL),
              (1, 0, D_MODEL), (1, 1, D_MODEL), (1, 2, D_MODEL), (2, 0, D_MODEL), (4, None, None)]
    as_row = lambda t: t.reshape(1, D_MODEL)
    params = [(g_mix, m_g_mix, v_g_mix), (ln_g, m_ln_g, v_ln_g), (ln_b, m_ln_b, v_ln_b),
              (conv_b, m_conv_b, v_conv_b), (pool_scale, m_pool_scale, v_pool_scale),
              (as_row(g_final), as_row(m_g_final), as_row(v_g_final)),
              (g_ple_post, m_g_ple_post, v_g_ple_post), (g_ple_gate, m_g_ple_gate, v_g_ple_gate),
              (g_ffn, m_g_ffn, v_g_ffn), (pool_w[0], m_pool_w[0], v_pool_w[0])]
    tots, small_upd = _small_update(small, layout, params, "small_update")
    loss = tots[1][3, 0]
    upd = {}
    for nm, res, prm in zip(vec_names, small_upd, [g_mix, ln_g, ln_b, conv_b, pool_scale, g_final, g_ple_post,
                                                    g_ple_gate, g_ffn, pool_w]):
        upd[nm] = tuple(t.reshape(prm.shape) for t in res)
    gr_conv_w = lax.dynamic_slice_in_dim(tots[3][:CONV_K], me * (C_CONV // N_DEV), C_CONV // N_DEV, axis=1)

    fused = [
        ("w_ple_gate", early_parts[0], early_chips[0], w_ple_gate[0], m_w_ple_gate[0], v_w_ple_gate[0], False),
        ("w_gate_up", early_parts[2], early_chips[2], w_gate_up[0].T, m_w_gate_up[0].T, v_w_gate_up[0].T, True),
        ("w_down", down_parts[0], down_chips[0], w_down[0], m_w_down[0], v_w_down[0], False),
        ("w_out", early_parts[3], early_chips[3], w_out[0], m_w_out[0], v_w_out[0], False),
    ]
    for nm, part, chips3, w_, m_, v_, transposed in fused:
        res = _sum_adamw(part, chips3, w_, m_, v_, "adamw_" + nm)
        upd[nm] = tuple((t.T if transposed else t)[None] for t in res)
    plain = [
        ("w_in", w_in[0].T, gr_w_in_t, m_w_in[0].T, v_w_in[0].T, True),
        ("w_ple_up", w_ple_up[0], gr_w_pu_t.T, m_w_ple_up[0], v_w_ple_up[0], False),
        ("conv_w", conv_w[0], gr_conv_w, m_conv_w[0], v_conv_w[0], False),
    ]
    for nm, w_, g_, m_, v_, transposed in plain:
        res = (g_,) + tuple(_adamw(w_, g_, m_, v_, "adamw_" + nm))
        upd[nm] = tuple((t.T if transposed else t)[None] for t in res)

    order = ["g_mix", "w_in", "conv_w", "conv_b", "ln_g", "ln_b", "pool_w", "pool_scale", "w_out", "g_ffn",
             "w_gate_up", "w_down", "g_ple_gate", "w_ple_gate", "w_ple_up", "g_ple_post", "g_final"]
    outs = [loss, dx[None]]
    for k in range(4):
        outs += [upd[nm][k] for nm in order]
    return tuple(outs)
```

```python
import functools

import jax
import jax.numpy as jnp
from jax import lax
from jax.experimental import pallas as pl
from jax.experimental.pallas import tpu as pltpu

D_MODEL = 1024
C_CONV = 512
C_POOL = 512
Z_WIDTH = 2 * C_CONV + C_POOL
POOL_WINDOWS = (2, 4, 8, 16)
POOL_GROUP = 128
CONV_K = 31
D_FF = 2816
D_PLE = 256
EPS = 1e-6
N_DEV = 8

ADAM_LR = 0.001
ADAM_B1 = 0.9
ADAM_B2 = 0.999
ADAM_EPS = 1e-08
ADAM_WD = 0.01
ADAM_STEP = 10

CONV_HALO = 32
POOL_HALO = 16
ROW_CHUNK = 32
V7X_VMEM_LIMIT = 56 * 1024 * 1024
FF_CHUNK = D_FF // 2

BF16 = jnp.bfloat16
F32 = jnp.float32
MESH = pl.DeviceIdType.MESH


def _dot(a, b):
    return lax.dot_general(a, b, (((1,), (0,)), ((), ())), preferred_element_type=F32)


def _dot_nt(a, b):
    return lax.dot_general(a, b, (((1,), (1,)), ((), ())), preferred_element_type=F32)


def _dot_tn(a, b):
    return lax.dot_general(a, b, (((0,), (0,)), ((), ())), preferred_element_type=F32)


def _rms_fwd(x, g):
    r = lax.rsqrt(jnp.mean(x * x, axis=-1, keepdims=True) + EPS)
    return x * r * g, r


def _rms_bwd(x, r, g, dy):
    xr = x * r
    dg = jnp.sum(dy * xr, axis=0, keepdims=True)
    dyg = dy * g
    dx = r * (dyg - xr * jnp.mean(dyg * xr, axis=-1, keepdims=True))
    return dx, dg


def _sigmoid(x):
    return jax.nn.sigmoid(x)


def _params(sem=None):
    return pltpu.CompilerParams(dimension_semantics=sem, vmem_limit_bytes=V7X_VMEM_LIMIT)


def _place():
    x, y, c = lax.axis_index("x"), lax.axis_index("y"), lax.axis_index("c")
    chips = [(1 - x, y), (x, 1 - y), (1 - x, 1 - y)]
    return (x, y, c), (x, y, 1 - c), chips


def _block(px, py, pc):
    return 4 * px + 2 * py + pc


class _Carry:
    def __init__(self):
        self.inputs = []
        self.out_shapes = []
        self.copies = []
        self.locals = []

    def add_input(self, arr):
        self.inputs.append(arr)
        return len(self.inputs) - 1

    def add_output(self, shape, dtype):
        self.out_shapes.append(jax.ShapeDtypeStruct(shape, dtype))
        return len(self.out_shapes) - 1

    def local(self, src_idx, dst_idx, dst_blk):
        self.locals.append((src_idx, dst_idx, dst_blk))

    def copy(self, src, dst_idx, dst_blk, got_blk, peer, step=0, after=()):
        self.copies.append(dict(src=src, dst_idx=dst_idx, dst_blk=dst_blk, got_blk=got_blk, peer=peer, step=step,
                                after=tuple(after)))
        return len(self.copies) - 1

    def sem_shapes(self):
        return [pltpu.SemaphoreType.DMA((max(1, len(self.copies)),)),
                pltpu.SemaphoreType.DMA((max(1, len(self.copies)),)),
                pltpu.SemaphoreType.DMA((max(1, len(self.locals)),))]

    @staticmethod
    def _view(ref, where):
        if isinstance(where, tuple):
            blk, row0, nrows = where
            return ref.at[blk, pl.ds(row0, nrows)]
        return ref.at[where]

    def _desc(self, k, ins, outs, sems, place):
        cp = self.copies[k]
        me, sib, chips = place
        kind, idx, blk = cp["src"]
        src = (ins if kind == "in" else outs)[idx]
        if blk is not None:
            src = self._view(src, blk(*place))
        to = sib if cp["peer"] == "sib" else (*chips[cp["peer"]], me[2])
        return pltpu.make_async_remote_copy(
            src_ref=src, dst_ref=self._view(outs[cp["dst_idx"]], cp["dst_blk"](*place)),
            send_sem=sems[0].at[k], recv_sem=sems[1].at[k], device_id=to, device_id_type=MESH)

    def _arrival(self, k, outs, sems, place):
        cp = self.copies[k]
        got = self._view(outs[cp["dst_idx"]], cp["got_blk"](*place))
        return pltpu.make_async_remote_copy(
            src_ref=got, dst_ref=got, send_sem=sems[0].at[k], recv_sem=sems[1].at[k],
            device_id=place[0], device_id_type=MESH)

    def _local(self, n, ins, outs, sems, place):
        src_idx, dst_idx, blk = self.locals[n]
        return pltpu.make_async_copy(ins[src_idx], outs[dst_idx].at[blk(*place)], sems[2].at[n])

    def starts(self, step, nsteps, ins, outs, sems):
        place = _place()
        self._waited = set()
        for s in sorted({0} | {cp["step"] for cp in self.copies}):
            ks = [k for k, cp in enumerate(self.copies) if cp["step"] == s]

            @pl.when(step == min(s, nsteps - 1))
            def _(s=s, ks=ks):
                if s == 0:
                    for n in range(len(self.locals)):
                        self._local(n, ins, outs, sems, place).start()
                for k in ks:
                    for a in self.copies[k]["after"]:
                        if a not in self._waited:
                            self._arrival(a, outs, sems, place).wait_recv()
                            self._waited.add(a)
                    self._desc(k, ins, outs, sems, place).start()

    def finish(self, step, nsteps, ins, outs, sems):
        place = _place()

        @pl.when(step == nsteps - 1)
        def _():
            for k in range(len(self.copies)):
                if k not in self._waited:
                    self._arrival(k, outs, sems, place).wait_recv()
            for k in range(len(self.copies)):
                self._desc(k, ins, outs, sems, place).wait_send()
            for n in range(len(self.locals)):
                self._local(n, ins, outs, sems, place).wait()


def _const_blk(j):
    return lambda me, sib, chips: j


def _carry_gather(carry, shards, relay_step, last_step):
    outs = []
    for sh in shards:
        i = carry.add_input(sh)
        o = carry.add_output((N_DEV,) + sh.shape, sh.dtype)
        half = sh.shape[0] // 2
        tile = 16 if sh.dtype == BF16 else 8
        split = half % tile == 0
        rows = [(0, half), (half, sh.shape[0] - half)] if split else [(0, sh.shape[0]), None]

        def whole(j, core):
            return lambda me, sib, chips, j=j, core=core: _block(*chips[j], me[2] if core == 0 else 1 - me[2])

        def part(j, core, h, rows=rows):
            return lambda me, sib, chips: (_block(*chips[j], me[2] if core == 0 else 1 - me[2]),) + rows[h]

        mine = lambda me, sib, chips: _block(*me)
        carry.local(i, o, mine)
        carry.copy(("in", i, None), o, mine, lambda me, sib, chips: _block(*sib), "sib")
        near = [carry.copy(("in", i, None), o, mine, whole(j, 0), j) for j in range(2)]
        for j in range(2):
            carry.copy(("out", o, whole(j, 0)), o, whole(j, 0), whole(j, 1), "sib", step=relay_step, after=(near[j],))
        for j in range(2):
            if rows[j] is None:
                continue
            far = carry.copy(("out", o, part(j, 0, j)), o, part(j, 0, j), part(2, 0, j), 1 - j,
                             step=relay_step, after=(near[j],))
            carry.copy(("out", o, part(2, 0, j)), o, part(2, 0, j), part(2, 1, j), "sib", step=last_step, after=(far,))
        outs.append(o)
    return outs


def _carry_pair(carry, grads):
    outs = []
    for g in grads:
        i = carry.add_input(g)
        o = carry.add_output((4,) + g.shape[1:], g.dtype)
        for j in range(4):
            if j < 3:
                blk = lambda me, sib, chips, j=j: _block(*chips[j], 1 - me[2])
            else:
                blk = lambda me, sib, chips: _block(*sib)
            carry.copy(("in", i, blk), o, _const_blk(j), _const_blk(j), "sib")
        outs.append(o)
    return outs


def _carry_chip(carry, parts):
    outs = []
    for p in parts:
        i = carry.add_input(p)
        o = carry.add_output((3,) + p.shape[1:], p.dtype)
        for j in range(3):
            carry.copy(("in", i, _const_blk(j)), o, _const_blk(j), _const_blk(j), j)
        outs.append(o)
    return outs


def _pcall(body, *, grid, in_specs, out_specs, out_shape, scratch_shapes, name, args, carry=None):
    sem = ("arbitrary",) * len(grid)
    if carry is None:
        res = pl.pallas_call(body, grid=grid, in_specs=in_specs, out_specs=out_specs, out_shape=out_shape,
                             scratch_shapes=scratch_shapes, compiler_params=_params(sem), name=name)(*args)
        return list(res), []
    n_in, n_out, n_scr = len(in_specs), len(out_specs), len(scratch_shapes)
    c_in, c_out = len(carry.inputs), len(carry.out_shapes)
    nsteps = 1
    for extent in grid:
        nsteps *= extent

    def wrapped(*refs):
        ins = refs[:n_in]
        cins = refs[n_in:n_in + c_in]
        o0 = n_in + c_in
        outs = refs[o0:o0 + n_out]
        couts = refs[o0 + n_out:o0 + n_out + c_out]
        s0 = o0 + n_out + c_out
        scr = refs[s0:s0 + n_scr]
        sems = refs[s0 + n_scr:]
        step = pl.program_id(0)
        for d in range(1, len(grid)):
            step = step * grid[d] + pl.program_id(d)
        carry.starts(step, nsteps, cins, couts, sems)
        body(*ins, *outs, *scr)
        carry.finish(step, nsteps, cins, couts, sems)

    any_spec = pl.BlockSpec(memory_space=pl.ANY)
    res = pl.pallas_call(
        wrapped, grid=grid,
        in_specs=list(in_specs) + [any_spec] * c_in,
        out_specs=list(out_specs) + [any_spec] * c_out,
        out_shape=list(out_shape) + carry.out_shapes,
        scratch_shapes=list(scratch_shapes) + carry.sem_shapes(),
        compiler_params=_params(sem), name=name)(*args, *carry.inputs)
    return list(res[:n_out]), list(res[n_out:])


def _shifted_copies(buf, shifted, tm):
    span = tm + CONV_HALO - 8
    for r in range(1, 8):
        shifted[r - 1, 0:span, :] = buf[r:r + span, :]


def _rows_at(buf, shifted, start):
    aligned, r = (start // 8) * 8, start % 8
    if r == 0:
        return buf[aligned:aligned + ROW_CHUNK, :]
    return shifted[r - 1, aligned:aligned + ROW_CHUNK, :]


def _row_spec(tm, width):
    return pl.BlockSpec((tm, width), lambda i: (i, 0))


def _const_spec(shape):
    return pl.BlockSpec(shape, lambda i: (0,) * len(shape))


def _weight_spec(shape):
    return pl.BlockSpec(shape, lambda i: (0,) * len(shape), pipeline_mode=pl.Buffered(1))


def _fwd_mix(x, w_in_t, w_out, g_mix, conv_w, conv_b, ln_g, ln_b, pool_w, pool_scale, tm, carry=None):
    s = x.shape[0]
    nt = s // tm

    def body(x_ref, win_ref, wout_ref, gmix_ref, cw_ref, cb_ref, lng_ref, lnb_ref, pw_ref, ps_ref,
             z_ref, u1_ref, pooled_ref, x1_ref, mix_ref, ubuf, vbuf, ush):
        i = pl.program_id(0)

        @pl.when(i == 0)
        def _():
            ubuf[0:CONV_HALO, :] = jnp.zeros((CONV_HALO, C_CONV), F32)
            vbuf[0:POOL_HALO, :] = jnp.zeros((POOL_HALO, C_POOL), F32)

        xt = x_ref[...]
        h, _ = _rms_fwd(xt, gmix_ref[...])
        z = _dot_nt(h.astype(BF16), win_ref[...])
        z_ref[...] = z
        a = z[:, :C_CONV]
        b = z[:, C_CONV:2 * C_CONV]
        v = z[:, 2 * C_CONV:]
        ubuf[CONV_HALO:CONV_HALO + tm, :] = a * _sigmoid(b)
        vbuf[POOL_HALO:POOL_HALO + tm, :] = v

        _shifted_copies(ubuf, ush, tm)
        for rc in range(tm // ROW_CHUNK):
            base = rc * ROW_CHUNK + CONV_HALO - (CONV_K - 1)
            acc = jnp.broadcast_to(cb_ref[...], (ROW_CHUNK, C_CONV))
            for k in range(CONV_K):
                acc = acc + cw_ref[k:k + 1, :] * _rows_at(ubuf, ush, base + k)
            u1_ref[rc * ROW_CHUNK:(rc + 1) * ROW_CHUNK, :] = acc

        u1 = u1_ref[...]
        mu = jnp.mean(u1, axis=-1, keepdims=True)
        cen = u1 - mu
        rstd = lax.rsqrt(jnp.mean(cen * cen, axis=-1, keepdims=True) + EPS)
        u2 = cen * rstd * lng_ref[...] + lnb_ref[...]
        u = u2 * _sigmoid(u2)

        pos1 = (i * tm + lax.broadcasted_iota(jnp.int32, (tm, 1), 0) + 1).astype(F32)
        parts = [u]
        for g, w in enumerate(POOL_WINDOWS):
            cols = slice(g * POOL_GROUP, (g + 1) * POOL_GROUP)
            vg = v[:, cols]
            tot = vg
            for j in range(1, w):
                tot = tot + vbuf[POOL_HALO - j:POOL_HALO - j + tm, cols]
            pooled = tot / jnp.minimum(pos1, float(w)) - vg
            pooled_b = pooled.astype(BF16)
            pooled_ref[:, cols] = pooled_b
            parts.append(_dot(pooled_b, pw_ref[g]) * ps_ref[:, cols])
        mix = jnp.concatenate(parts, axis=-1).astype(BF16)
        mix_ref[...] = mix
        x1_ref[...] = xt + _dot(mix, wout_ref[...])

        ubuf[0:CONV_HALO, :] = ubuf[tm:tm + CONV_HALO, :]
        vbuf[0:POOL_HALO, :] = vbuf[tm:tm + POOL_HALO, :]

    return _pcall(
        body,
        grid=(nt,),
        in_specs=[
            _row_spec(tm, D_MODEL),
            _const_spec((Z_WIDTH, D_MODEL)),
            _const_spec((D_MODEL, D_MODEL)),
            _const_spec((1, D_MODEL)),
            _const_spec((CONV_HALO, C_CONV)),
            _const_spec((1, C_CONV)),
            _const_spec((1, C_CONV)),
            _const_spec((1, C_CONV)),
            _const_spec((len(POOL_WINDOWS), POOL_GROUP, POOL_GROUP)),
            _const_spec((1, C_POOL)),
        ],
        out_specs=[
            _row_spec(tm, Z_WIDTH),
            _row_spec(tm, C_CONV),
            _row_spec(tm, C_POOL),
            _row_spec(tm, D_MODEL),
            _row_spec(tm, D_MODEL),
        ],
        out_shape=[
            jax.ShapeDtypeStruct((s, Z_WIDTH), F32),
            jax.ShapeDtypeStruct((s, C_CONV), F32),
            jax.ShapeDtypeStruct((s, C_POOL), BF16),
            jax.ShapeDtypeStruct((s, D_MODEL), F32),
            jax.ShapeDtypeStruct((s, D_MODEL), BF16),
        ],
        scratch_shapes=[
            pltpu.VMEM((tm + CONV_HALO, C_CONV), F32),
            pltpu.VMEM((tm + POOL_HALO, C_POOL), F32),
            pltpu.VMEM((7, tm + CONV_HALO, C_CONV), F32),
        ],
        name="fwd_mix",
        args=(x, w_in_t, w_out, g_mix, conv_w, conv_b, ln_g, ln_b, pool_w, pool_scale),
        carry=carry,
    )


def _ffn_up(x1, w_gu_t, g_ffn, tm, carry=None):
    s = x1.shape[0]

    def body(x1_ref, w_ref, gffn_ref, h2_ref, g_ref, u_ref, act_ref):
        h, _ = _rms_fwd(x1_ref[...], gffn_ref[...])
        h2 = h.astype(BF16)
        h2_ref[...] = h2
        for c in range(D_FF // FF_CHUNK):
            cols = slice(c * FF_CHUNK, (c + 1) * FF_CHUNK)
            g = _dot_nt(h2, w_ref[c * FF_CHUNK:(c + 1) * FF_CHUNK, :])
            u = _dot_nt(h2, w_ref[D_FF + c * FF_CHUNK:D_FF + (c + 1) * FF_CHUNK, :])
            g_ref[:, cols] = g.astype(BF16)
            u_ref[:, cols] = u.astype(BF16)
            act_ref[:, cols] = (g * _sigmoid(g) * u).astype(BF16)

    return _pcall(
        body,
        grid=(s // tm,),
        in_specs=[_row_spec(tm, D_MODEL), _weight_spec((2 * D_FF, D_MODEL)), _const_spec((1, D_MODEL))],
        out_specs=[_row_spec(tm, D_MODEL), _row_spec(tm, D_FF), _row_spec(tm, D_FF), _row_spec(tm, D_FF)],
        out_shape=[
            jax.ShapeDtypeStruct((s, D_MODEL), BF16),
            jax.ShapeDtypeStruct((s, D_FF), BF16),
            jax.ShapeDtypeStruct((s, D_FF), BF16),
            jax.ShapeDtypeStruct((s, D_FF), BF16),
        ],
        scratch_shapes=[],
        name="ffn_up",
        args=(x1, w_gu_t, g_ffn),
        carry=carry,
    )


def _ffn_down(x1, act, w_down, tm, carry=None):
    s = x1.shape[0]

    def body(x1_ref, act_ref, wd_ref, x2_ref):
        x2_ref[...] = x1_ref[...] + _dot(act_ref[...], wd_ref[...])

    return _pcall(
        body,
        grid=(s // tm,),
        in_specs=[_row_spec(tm, D_MODEL), _row_spec(tm, D_FF), _weight_spec((D_FF, D_MODEL))],
        out_specs=[_row_spec(tm, D_MODEL)],
        out_shape=[jax.ShapeDtypeStruct((s, D_MODEL), F32)],
        scratch_shapes=[],
        name="ffn_down",
        args=(x1, act, w_down),
        carry=carry,
    )


def _ple_loss(x2, p, tgt, w_pg, w_pu_t, g_gate, g_post, g_final, tm):
    s = x2.shape[0]
    nt = s // tm

    def body(x2_ref, p_ref, t_ref, wpg_ref, wpu_ref, gg_ref, gp_ref, gf_ref,
             dx2_ref, dx2b_ref, hg_ref, ds_ref, dpe_ref, pb_ref, stats_ref):
        i = pl.program_id(0)

        @pl.when(i == 0)
        def _():
            stats_ref[...] = jnp.zeros_like(stats_ref)

        x2 = x2_ref[...]
        hg, rg = _rms_fwd(x2, gg_ref[...])
        hg_b = hg.astype(BF16)
        hg_ref[...] = hg_b
        gate = _sigmoid(_dot(hg_b, wpg_ref[...]))
        pb = p_ref[...].astype(BF16)
        pb_ref[...] = pb
        pe = _dot_nt(pb, wpu_ref[...])
        e, rp = _rms_fwd(pe, gp_ref[...])
        x3 = x2 + gate * e
        y, r3 = _rms_fwd(x3, gf_ref[...])
        diff = y - t_ref[...]
        loss = 0.5 * jnp.sum(jnp.sum(diff * diff, axis=-1, keepdims=True), axis=0, keepdims=True) / D_MODEL
        dy = diff * (1.0 / D_MODEL)

        dx3, dgf = _rms_bwd(x3, r3, gf_ref[...], dy)
        dpe, dgp = _rms_bwd(pe, rp, gp_ref[...], dx3 * gate)
        dpe_ref[...] = dpe.astype(BF16)
        ds = (dx3 * e * gate * (1.0 - gate)).astype(BF16)
        ds_ref[...] = ds
        dhg = _dot_nt(ds, wpg_ref[...])
        dxg, dgg = _rms_bwd(x2, rg, gg_ref[...], dhg)
        dx2 = dx3 + dxg
        dx2_ref[...] = dx2
        dx2b_ref[...] = dx2.astype(BF16)

        stats_ref[0:1, :] += dgf
        stats_ref[1:2, :] += dgp
        stats_ref[2:3, :] += dgg
        stats_ref[3:4, :] += jnp.broadcast_to(loss, (1, D_MODEL))

    return pl.pallas_call(
        body,
        grid=(nt,),
        in_specs=[
            _row_spec(tm, D_MODEL),
            _row_spec(tm, D_PLE),
            _row_spec(tm, D_MODEL),
            _const_spec((D_MODEL, D_MODEL)),
            _const_spec((D_MODEL, D_PLE)),
            _const_spec((1, D_MODEL)),
            _const_spec((1, D_MODEL)),
            _const_spec((1, D_MODEL)),
        ],
        out_specs=[
            _row_spec(tm, D_MODEL),
            _row_spec(tm, D_MODEL),
            _row_spec(tm, D_MODEL),
            _row_spec(tm, D_MODEL),
            _row_spec(tm, D_MODEL),
            _row_spec(tm, D_PLE),
            _const_spec((8, D_MODEL)),
        ],
        out_shape=[
            jax.ShapeDtypeStruct((s, D_MODEL), F32),
            jax.ShapeDtypeStruct((s, D_MODEL), BF16),
            jax.ShapeDtypeStruct((s, D_MODEL), BF16),
            jax.ShapeDtypeStruct((s, D_MODEL), BF16),
            jax.ShapeDtypeStruct((s, D_MODEL), BF16),
            jax.ShapeDtypeStruct((s, D_PLE), BF16),
            jax.ShapeDtypeStruct((8, D_MODEL), F32),
        ],
        compiler_params=_params(("arbitrary",)),
        name="ple_loss",
    )(x2, p, tgt, w_pg, w_pu_t, g_gate, g_post, g_final)


def _ffn_bwd(dx2, dx2b, x1, g_sav, u_sav, w_gu_t, w_down, g_ffn, tm):
    s = x1.shape[0]

    def body(dx2_ref, dx2b_ref, x1_ref, g_ref, u_ref, w_ref, wd_ref, gffn_ref,
             dg_ref, du_ref, dx1_ref, dx1b_ref, stats_ref):
        @pl.when(pl.program_id(0) == 0)
        def _():
            stats_ref[...] = jnp.zeros_like(stats_ref)

        dx2b = dx2b_ref[...]
        dh2 = jnp.zeros((tm, D_MODEL), F32)
        for c in range(D_FF // FF_CHUNK):
            cols = slice(c * FF_CHUNK, (c + 1) * FF_CHUNK)
            dact = _dot_nt(dx2b, wd_ref[c * FF_CHUNK:(c + 1) * FF_CHUNK, :])
            g = g_ref[:, cols].astype(F32)
            u = u_ref[:, cols].astype(F32)
            sg = _sigmoid(g)
            dg = (dact * u * sg * (1.0 + g * (1.0 - sg))).astype(BF16)
            du = (dact * g * sg).astype(BF16)
            dg_ref[:, cols] = dg
            du_ref[:, cols] = du
            dh2 = dh2 + _dot(dg, w_ref[c * FF_CHUNK:(c + 1) * FF_CHUNK, :])
            dh2 = dh2 + _dot(du, w_ref[D_FF + c * FF_CHUNK:D_FF + (c + 1) * FF_CHUNK, :])

        x1 = x1_ref[...]
        r2 = lax.rsqrt(jnp.mean(x1 * x1, axis=-1, keepdims=True) + EPS)
        dxn, dgf = _rms_bwd(x1, r2, gffn_ref[...], dh2)
        dx1 = dx2_ref[...] + dxn
        dx1_ref[...] = dx1
        dx1b_ref[...] = dx1.astype(BF16)
        stats_ref[0:1, :] += dgf

    return pl.pallas_call(
        body,
        grid=(s // tm,),
        in_specs=[
            _row_spec(tm, D_MODEL), _row_spec(tm, D_MODEL), _row_spec(tm, D_MODEL),
            _row_spec(tm, D_FF), _row_spec(tm, D_FF),
            _weight_spec((2 * D_FF, D_MODEL)), _weight_spec((D_FF, D_MODEL)), _const_spec((1, D_MODEL)),
        ],
        out_specs=[_row_spec(tm, D_FF), _row_spec(tm, D_FF), _row_spec(tm, D_MODEL), _row_spec(tm, D_MODEL),
                   _const_spec((8, D_MODEL))],
        out_shape=[
            jax.ShapeDtypeStruct((s, D_FF), BF16),
            jax.ShapeDtypeStruct((s, D_FF), BF16),
            jax.ShapeDtypeStruct((s, D_MODEL), F32),
            jax.ShapeDtypeStruct((s, D_MODEL), BF16),
            jax.ShapeDtypeStruct((8, D_MODEL), F32),
        ],
        compiler_params=_params(("arbitrary",)),
        name="ffn_bwd",
    )(dx2, dx2b, x1, g_sav, u_sav, w_gu_t, w_down, g_ffn)


def _bwd_mix(dx1, dx1b, x, z, u1, pooled, w_in_t, w_out, g_mix, conv_w, ln_g, ln_b, pool_w, pool_scale, tm,
             carry=None):
    s = x.shape[0]
    nt = s // tm

    def body(dx1_ref, dx1b_ref, x_ref, z_ref, u1_ref, pooled_ref, win_ref, wout_ref, gmix_ref, cw_ref,
             lng_ref, lnb_ref, pw_ref, ps_ref,
             dx_ref, dz_ref, h1_ref, vec_ref, dcw_ref, dpw_ref, dubuf, dvbuf, u0buf, du0buf, dush):
        i = pl.program_id(0)
        tile = nt - 1 - i

        @pl.when(i == 0)
        def _():
            vec_ref[...] = jnp.zeros_like(vec_ref)
            dcw_ref[...] = jnp.zeros_like(dcw_ref)
            dpw_ref[...] = jnp.zeros_like(dpw_ref)
            dubuf[tm:tm + CONV_HALO, :] = jnp.zeros((CONV_HALO, C_CONV), F32)
            dvbuf[tm:tm + POOL_HALO, :] = jnp.zeros((POOL_HALO, C_POOL), F32)

        dmix = _dot_nt(dx1b_ref[...], wout_ref[...])
        du = dmix[:, :C_CONV]
        dq = dmix[:, C_CONV:]

        pos1 = (tile * tm + lax.broadcasted_iota(jnp.int32, (tm, 1), 0) + 1).astype(F32)
        dpooled_parts = []
        dps_rows = []
        for g, w in enumerate(POOL_WINDOWS):
            cols = slice(g * POOL_GROUP, (g + 1) * POOL_GROUP)
            pooled_b = pooled_ref[:, cols]
            mixed = _dot(pooled_b, pw_ref[g])
            dqg = dq[:, cols]
            dps_rows.append(jnp.sum(dqg * mixed, axis=0, keepdims=True))
            dmixed = (dqg * ps_ref[:, cols]).astype(BF16)
            dpw_ref[g] += _dot_tn(pooled_b, dmixed)
            dpooled = _dot_nt(dmixed, pw_ref[g])
            dpooled_parts.append(dpooled)
            dvbuf[0:tm, cols] = dpooled / jnp.minimum(pos1, float(w))
        vec_ref[4:5, 0:C_POOL] += jnp.concatenate(dps_rows, axis=-1)
        dv_parts = []
        for g, w in enumerate(POOL_WINDOWS):
            cols = slice(g * POOL_GROUP, (g + 1) * POOL_GROUP)
            tot = dvbuf[0:tm, cols]
            for j in range(1, w):
                tot = tot + dvbuf[j:j + tm, cols]
            dv_parts.append(tot - dpooled_parts[g])

        u1 = u1_ref[...]
        mu = jnp.mean(u1, axis=-1, keepdims=True)
        cen = u1 - mu
        rstd = lax.rsqrt(jnp.mean(cen * cen, axis=-1, keepdims=True) + EPS)
        xhat = cen * rstd
        u2 = xhat * lng_ref[...] + lnb_ref[...]
        sg2 = _sigmoid(u2)
        du2 = du * sg2 * (1.0 + u2 * (1.0 - sg2))
        vec_ref[1:2, 0:C_CONV] += jnp.sum(du2 * xhat, axis=0, keepdims=True)
        vec_ref[2:3, 0:C_CONV] += jnp.sum(du2, axis=0, keepdims=True)
        t1 = du2 * lng_ref[...]
        du1 = rstd * (t1 - jnp.mean(t1, axis=-1, keepdims=True)
                      - xhat * jnp.mean(t1 * xhat, axis=-1, keepdims=True))
        vec_ref[3:4, 0:C_CONV] += jnp.sum(du1, axis=0, keepdims=True)
        dubuf[0:tm, :] = du1

        zt = z_ref[...]
        a = zt[:, :C_CONV]
        sgb = _sigmoid(zt[:, C_CONV:2 * C_CONV])
        u0buf[...] = a * sgb

        _shifted_copies(dubuf, dush, tm)
        for rc in range(tm // ROW_CHUNK):
            r0 = rc * ROW_CHUNK
            acc = jnp.zeros((ROW_CHUNK, C_CONV), F32)
            for k in range(CONV_K):
                acc = acc + cw_ref[k:k + 1, :] * _rows_at(dubuf, dush, r0 + (CONV_K - 1) - k)
            du0buf[r0:r0 + ROW_CHUNK, :] = acc
        for k in range(CONV_K):
            acc = jnp.zeros((ROW_CHUNK, C_CONV), F32)
            for rc in range(tm // ROW_CHUNK):
                r0 = rc * ROW_CHUNK
                acc = acc + u0buf[r0:r0 + ROW_CHUNK, :] * _rows_at(dubuf, dush, r0 + (CONV_K - 1) - k)
            dcw_ref[k:k + 1, :] += jnp.sum(acc, axis=0, keepdims=True)
        du0 = du0buf[...]

        da = du0 * sgb
        db = du0 * a * sgb * (1.0 - sgb)
        dz = jnp.concatenate([da, db] + dv_parts, axis=-1).astype(BF16)
        dz_ref[...] = dz

        xt = x_ref[...]
        h1, r1 = _rms_fwd(xt, gmix_ref[...])
        h1_ref[...] = h1.astype(BF16)
        dh1 = _dot(dz, win_ref[...])
        dxn, dgm = _rms_bwd(xt, r1, gmix_ref[...], dh1)
        dx_ref[...] = dx1_ref[...] + dxn
        vec_ref[0:1, :] += dgm

        dubuf[tm:tm + CONV_HALO, :] = dubuf[0:CONV_HALO, :]
        dvbuf[tm:tm + POOL_HALO, :] = dvbuf[0:POOL_HALO, :]

    rev = lambda width: pl.BlockSpec((tm, width), lambda i: (nt - 1 - i, 0))
    return _pcall(
        body,
        grid=(nt,),
        in_specs=[
            rev(D_MODEL), rev(D_MODEL), rev(D_MODEL), rev(Z_WIDTH), rev(C_CONV), rev(C_POOL),
            _const_spec((Z_WIDTH, D_MODEL)),
            _const_spec((D_MODEL, D_MODEL)),
            _const_spec((1, D_MODEL)),
            _const_spec((CONV_HALO, C_CONV)),
            _const_spec((1, C_CONV)),
            _const_spec((1, C_CONV)),
            _const_spec((len(POOL_WINDOWS), POOL_GROUP, POOL_GROUP)),
            _const_spec((1, C_POOL)),
        ],
        out_specs=[
            rev(D_MODEL), rev(Z_WIDTH), rev(D_MODEL),
            _const_spec((8, D_MODEL)),
            _const_spec((CONV_HALO, C_CONV)),
            _const_spec((len(POOL_WINDOWS), POOL_GROUP, POOL_GROUP)),
        ],
        out_shape=[
            jax.ShapeDtypeStruct((s, D_MODEL), F32),
            jax.ShapeDtypeStruct((s, Z_WIDTH), BF16),
            jax.ShapeDtypeStruct((s, D_MODEL), BF16),
            jax.ShapeDtypeStruct((8, D_MODEL), F32),
            jax.ShapeDtypeStruct((CONV_HALO, C_CONV), F32),
            jax.ShapeDtypeStruct((len(POOL_WINDOWS), POOL_GROUP, POOL_GROUP), F32),
        ],
        scratch_shapes=[
            pltpu.VMEM((tm + CONV_HALO, C_CONV), F32),
            pltpu.VMEM((tm + POOL_HALO, C_POOL), F32),
            pltpu.VMEM((tm, C_CONV), F32),
            pltpu.VMEM((tm, C_CONV), F32),
            pltpu.VMEM((7, tm + CONV_HALO, C_CONV), F32),
        ],
        name="bwd_mix",
        args=(dx1, dx1b, x, z, u1, pooled, w_in_t, w_out, g_mix, conv_w, ln_g, ln_b, pool_w, pool_scale),
        carry=carry,
    )


def _grad_matmul(a, b, bm, name, a2=None, carry=None):
    s, ma = a.shape
    nb = b.shape[1]
    na = ma // bm
    if a2 is None:
        def body(a_ref, b_ref, o_ref):
            o_ref[...] = _dot_tn(a_ref[...], b_ref[...]).astype(BF16)

        lhs_specs = [pl.BlockSpec((s, bm), lambda i: (0, i))]
        lhs = (a,)
        steps = na
    else:
        def body(a_ref, a2_ref, b_ref, o_ref):
            i = pl.program_id(0)

            @pl.when(i < na)
            def _():
                o_ref[...] = _dot_tn(a_ref[...], b_ref[...]).astype(BF16)

            @pl.when(i >= na)
            def _():
                o_ref[...] = _dot_tn(a2_ref[...], b_ref[...]).astype(BF16)

        lhs_specs = [pl.BlockSpec((s, bm), lambda i: (0, jnp.minimum(i, na - 1))),
                     pl.BlockSpec((s, bm), lambda i: (0, jnp.maximum(i - na, 0)))]
        lhs = (a, a2)
        steps = 2 * na

    outs, carried = _pcall(
        body,
        grid=(steps,),
        in_specs=lhs_specs + [pl.BlockSpec((s, nb), lambda i: (0, 0))],
        out_specs=[pl.BlockSpec((bm, nb), lambda i: (i, 0))],
        out_shape=[jax.ShapeDtypeStruct((steps * bm, nb), BF16)],
        scratch_shapes=[],
        name=name,
        args=lhs + (b,),
        carry=carry,
    )
    return outs[0], carried


def _gather_steps(src, dst, send_sems, recv_sems, local_sems):
    n = len(src)
    me, sib, chips = _place()
    c = me[2]
    started = []
    mine = []

    def copy(a, k, block, to, from_src=False):
        rows = dst[a].at[_block(*block)]
        return pltpu.make_async_remote_copy(
            src_ref=src[a] if from_src else rows, dst_ref=rows,
            send_sem=send_sems.at[a, k], recv_sem=recv_sems.at[a, k],
            device_id=to, device_id_type=MESH)

    def start():
        for a in range(n):
            cp = pltpu.make_async_copy(src[a], dst[a].at[_block(*me)], local_sems.at[a])
            cp.start()
            mine.append(cp)
            first = [copy(a, 0, me, sib, True)]
            first += [copy(a, 1 + j, me, (*chip, c), True) for j, chip in enumerate(chips)]
            for cp in first:
                cp.start()
            started.extend(first)

    def forward():
        for a in range(n):
            for j, chip in enumerate(chips):
                copy(a, 1 + j, (*chip, c), me).wait_recv()
                fwd = copy(a, 4 + j, (*chip, c), sib)
                fwd.start()
                started.append(fwd)

    def finish():
        for a in range(n):
            copy(a, 0, sib, me).wait_recv()
            for j, chip in enumerate(chips):
                copy(a, 4 + j, (*chip, 1 - c), me).wait_recv()
        for cp in started:
            cp.wait_send()
        for cp in mine:
            cp.wait()

    return start, forward, finish


def _gather_sems(n):
    return [pltpu.SemaphoreType.DMA((n, 7)), pltpu.SemaphoreType.DMA((n, 7)), pltpu.SemaphoreType.DMA((n,))]


def _all_gather(shards, name):
    n = len(shards)
    carry = _Carry()
    _carry_gather(carry, shards, 1, 2)

    def body(*refs):
        step = jnp.int32(0)
        carry.starts(step, 1, refs[:n], refs[n:2 * n], refs[2 * n:])
        carry.finish(step, 1, refs[:n], refs[n:2 * n], refs[2 * n:])

    any_spec = pl.BlockSpec(memory_space=pl.ANY)
    return pl.pallas_call(
        body,
        in_specs=[any_spec] * n,
        out_specs=[any_spec] * n,
        out_shape=carry.out_shapes,
        scratch_shapes=carry.sem_shapes(),
        name=name,
    )(*shards)


def _reduce_scatter(grads, small, name):
    n, ns = len(grads), len(small)
    shapes = [g.shape[1:] for g in grads]

    def body(*refs):
        g = refs[:n]
        out = refs[n + ns:2 * n + ns]
        scr = refs[2 * (n + ns):]
        own, loc, r1, r2 = scr[:n], scr[n:2 * n], scr[2 * n:3 * n], scr[3 * n:4 * n]
        load_sems, s1, q1, s2, q2 = scr[4 * n:4 * n + 5]
        gather_start = gather_forward = gather_finish = lambda: None
        if ns:
            gather_start, gather_forward, gather_finish = _gather_steps(
                refs[n:n + ns], refs[2 * n + ns:2 * (n + ns)], *scr[4 * n + 5:])
        me, sib, chips = _place()
        c = me[2]

        gather_start()
        loads = []
        sends = []
        for a in range(n):
            ld = [pltpu.make_async_copy(g[a].at[_block(*chip, c)], loc[a].at[j], load_sems.at[a, j])
                  for j, chip in enumerate(chips)]
            ld.append(pltpu.make_async_copy(g[a].at[_block(*me)], own[a], load_sems.at[a, 3]))
            for cp in ld:
                cp.start()
            loads.append(ld)
            blocks = [(*chip, 1 - c) for chip in chips] + [sib]
            for j, blk in enumerate(blocks):
                cp = pltpu.make_async_remote_copy(
                    src_ref=g[a].at[_block(*blk)], dst_ref=r1[a].at[j],
                    send_sem=s1.at[a, j], recv_sem=q1.at[a, j], device_id=sib, device_id_type=MESH)
                cp.start()
                sends.append(cp)

        def from_sibling(a, j):
            return pltpu.make_async_remote_copy(
                src_ref=r1[a].at[j], dst_ref=r1[a].at[j], send_sem=s1.at[a, j], recv_sem=q1.at[a, j],
                device_id=sib, device_id_type=MESH)

        def partial(a, j, chip):
            return pltpu.make_async_remote_copy(
                src_ref=loc[a].at[j], dst_ref=r2[a].at[j], send_sem=s2.at[a, j], recv_sem=q2.at[a, j],
                device_id=(*chip, c), device_id_type=MESH)

        gather_forward()
        for a in range(n):
            for j, chip in enumerate(chips):
                loads[a][j].wait()
                from_sibling(a, j).wait_recv()
                loc[a][j] = (loc[a][j].astype(F32) + r1[a][j].astype(F32)).astype(BF16)
                cp = partial(a, j, chip)
                cp.start()
                sends.append(cp)
        for a in range(n):
            loads[a][3].wait()
            from_sibling(a, 3).wait_recv()
            acc = own[a][...].astype(F32) + r1[a][3].astype(F32)
            for j, chip in enumerate(chips):
                partial(a, j, chip).wait_recv()
                acc = acc + r2[a][j].astype(F32)
            out[a][...] = acc
        for cp in sends:
            cp.wait_send()
        gather_finish()

    any_spec = pl.BlockSpec(memory_space=pl.ANY)
    vmem_spec = pl.BlockSpec(memory_space=pltpu.VMEM)
    res = pl.pallas_call(
        body,
        in_specs=[any_spec] * (n + ns),
        out_specs=[vmem_spec] * n + [any_spec] * ns,
        out_shape=([jax.ShapeDtypeStruct(sh, F32) for sh in shapes]
                   + [jax.ShapeDtypeStruct((N_DEV,) + sm.shape, sm.dtype) for sm in small]),
        scratch_shapes=(
            [pltpu.VMEM(sh, BF16) for sh in shapes]
            + [pltpu.VMEM((3,) + sh, BF16) for sh in shapes]
            + [pltpu.VMEM((4,) + sh, BF16) for sh in shapes]
            + [pltpu.VMEM((3,) + sh, BF16) for sh in shapes]
            + [pltpu.SemaphoreType.DMA((n, 4)),
               pltpu.SemaphoreType.DMA((n, 4)), pltpu.SemaphoreType.DMA((n, 4)),
               pltpu.SemaphoreType.DMA((n, 3)), pltpu.SemaphoreType.DMA((n, 3))]
            + (_gather_sems(ns) if ns else [])
        ),
        compiler_params=pltpu.CompilerParams(vmem_limit_bytes=V7X_VMEM_LIMIT),
        name=name,
    )(*grads, *small)
    return res[:n], res[n:]


def _pair_add(grads, from_sib, blks, name):
    n = len(grads)

    def body(blk_ref, *refs):
        for a in range(n):
            refs[2 * n + a][...] = (refs[a][...].astype(F32) + refs[n + a][...].astype(F32)).astype(BF16)

    mine = [pl.BlockSpec((None,) + g.shape[1:], lambda j, b: (b[j], 0, 0)) for g in grads]
    same = [pl.BlockSpec((None,) + g.shape[1:], lambda j, b: (j, 0, 0)) for g in grads]
    return pl.pallas_call(
        body,
        grid_spec=pltpu.PrefetchScalarGridSpec(
            num_scalar_prefetch=1, grid=(4,), in_specs=mine + same, out_specs=same),
        out_shape=[jax.ShapeDtypeStruct((4,) + g.shape[1:], BF16) for g in grads],
        compiler_params=_params(("arbitrary",)),
        name=name,
    )(blks, *grads, *from_sib)


def _chip_sum(parts, from_chips, name):
    n = len(parts)

    def body(*refs):
        for a in range(n):
            acc = refs[a][...].astype(F32)
            for j in range(3):
                acc = acc + refs[n + a][j].astype(F32)
            refs[2 * n + a][...] = acc

    half = [p.shape[1] // 2 for p in parts]
    return pl.pallas_call(
        body,
        grid=(2,),
        in_specs=([pl.BlockSpec((None, h, p.shape[2]), lambda i: (3, i, 0)) for p, h in zip(parts, half)]
                  + [pl.BlockSpec((3, h, p.shape[2]), lambda i: (0, i, 0)) for p, h in zip(parts, half)]),
        out_specs=[pl.BlockSpec((h, p.shape[2]), lambda i: (i, 0)) for p, h in zip(parts, half)],
        out_shape=[jax.ShapeDtypeStruct(p.shape[1:], F32) for p in parts],
        compiler_params=_params(("arbitrary",)),
        name=name,
    )(*parts, *from_chips)


def _adam_math(w, g, m, v):
    nm = ADAM_B1 * m + (1.0 - ADAM_B1) * g
    nv = ADAM_B2 * v + (1.0 - ADAM_B2) * (g * g)
    m_hat = nm / (1.0 - ADAM_B1 ** ADAM_STEP)
    v_hat = nv / (1.0 - ADAM_B2 ** ADAM_STEP)
    return -ADAM_LR * (m_hat / (jnp.sqrt(v_hat) + ADAM_EPS) + ADAM_WD * w), nm, nv


def _sum_adamw(part, from_chips, w, m, v, name):
    r, c = w.shape
    half = r // 2

    def body(p_ref, f_ref, w_ref, m_ref, v_ref, g_ref, d_ref, nm_ref, nv_ref):
        g = p_ref[...].astype(F32)
        for j in range(3):
            g = g + f_ref[j].astype(F32)
        g_ref[...] = g
        d_ref[...], nm_ref[...], nv_ref[...] = _adam_math(w_ref[...], g, m_ref[...], v_ref[...])

    spec = pl.BlockSpec((half, c), lambda i: (i, 0))
    return pl.pallas_call(
        body,
        grid=(2,),
        in_specs=[pl.BlockSpec((None, half, c), lambda i: (3, i, 0)),
                  pl.BlockSpec((3, half, c), lambda i: (0, i, 0)), spec, spec, spec],
        out_specs=[spec] * 4,
        out_shape=[jax.ShapeDtypeStruct((r, c), F32)] * 4,
        compiler_params=_params(("arbitrary",)),
        name=name,
    )(part, from_chips, w, m, v)


def _small_update(gathered, layout, params, name):
    ng, npar = len(gathered), len(params)

    def body(*refs):
        parts = refs[:ng]
        prm = refs[ng:ng + 3 * npar]
        tot_refs = refs[ng + 3 * npar:2 * ng + 3 * npar]
        out = refs[2 * ng + 3 * npar:]
        tots = []
        for a in range(ng):
            acc = parts[a][0]
            for d in range(1, N_DEV):
                acc = acc + parts[a][d]
            tot_refs[a][...] = acc
            tots.append(acc)
        for i, (a, row, width) in enumerate(layout):
            g = tots[a] if row is None else tots[a][row:row + 1, :width]
            delta, nm, nv = _adam_math(prm[3 * i][...], g, prm[3 * i + 1][...], prm[3 * i + 2][...])
            out[4 * i][...] = g
            out[4 * i + 1][...] = delta
            out[4 * i + 2][...] = nm
            out[4 * i + 3][...] = nv

    flat = [t for prm in params for t in prm]
    res = pl.pallas_call(
        body,
        out_shape=([jax.ShapeDtypeStruct(g.shape[1:], F32) for g in gathered]
                   + [jax.ShapeDtypeStruct(prm[0].shape, F32) for prm in params for _ in range(4)]),
        compiler_params=pltpu.CompilerParams(vmem_limit_bytes=V7X_VMEM_LIMIT),
        name=name,
    )(*gathered, *flat)
    return res[:ng], [tuple(res[ng + 4 * i:ng + 4 * i + 4]) for i in range(npar)]


def _adamw(w, g, m, v, name):
    rows, cols = w.shape
    br = rows
    for cand in (512, 256, 128):
        if rows % cand == 0 and rows > cand:
            br = cand
            break

    def body(w_ref, g_ref, m_ref, v_ref, d_ref, nm_ref, nv_ref):
        d_ref[...], nm_ref[...], nv_ref[...] = _adam_math(w_ref[...], g_ref[...], m_ref[...], v_ref[...])

    spec = pl.BlockSpec((br, cols), lambda i: (i, 0))
    shape = jax.ShapeDtypeStruct((rows, cols), F32)
    return pl.pallas_call(
        body,
        grid=(rows // br,),
        in_specs=[spec] * 4,
        out_specs=[spec] * 3,
        out_shape=[shape] * 3,
        compiler_params=_params(("arbitrary",)),
        name=name,
    )(w, g, m, v)


def _by_device(full):
    return full.reshape(N_DEV, full.shape[0] // N_DEV, full.shape[1])


def kernel(x, p, g_mix, w_in, conv_w, conv_b, ln_g, ln_b, pool_w, pool_scale, w_out, g_ffn, w_gate_up, w_down, g_ple_gate, w_ple_gate, w_ple_up, g_ple_post, g_final, loss_target, m_g_mix, m_w_in, m_conv_w, m_conv_b, m_ln_g, m_ln_b, m_pool_w, m_pool_scale, m_w_out, m_g_ffn, m_w_gate_up, m_w_down, m_g_ple_gate, m_w_ple_gate, m_w_ple_up, m_g_ple_post, m_g_final, v_g_mix, v_w_in, v_conv_w, v_conv_b, v_ln_g, v_ln_b, v_pool_w, v_pool_scale, v_w_out, v_g_ffn, v_w_gate_up, v_w_down, v_g_ple_gate, v_w_ple_gate, v_w_ple_up, v_g_ple_post, v_g_final):
    seq = x.shape[1]
    xs = x[0]
    ps = p[0, 0]
    tgt = loss_target[0]
    ax, ay, ac = lax.axis_index("x"), lax.axis_index("y"), lax.axis_index("c")
    me = _block(ax, ay, ac)
    blks = jnp.stack([_block(1 - ax, ay, ac), _block(ax, 1 - ay, ac), _block(1 - ax, 1 - ay, ac), me]).astype(jnp.int32)
    rows = lambda gth: gth.reshape((-1,) + gth.shape[2:])

    w_in_t, w_out_f, conv_w_t = [rows(gth) for gth in _all_gather([
        w_in[0].T.astype(BF16),
        w_out[0].astype(BF16),
        jnp.pad(conv_w[0].T, ((0, 0), (0, CONV_HALO - CONV_K))),
    ], "gather_first")]
    conv_w_f = conv_w_t.T
    pool_w_b = pool_w[0].astype(BF16)

    carry = _Carry()
    _carry_gather(carry, [w_gate_up[0].T.astype(BF16)], 8, 12)
    (z, u1, pooled, x1, mix), (w_gu_all,) = _fwd_mix(
        xs, w_in_t, w_out_f, g_mix, conv_w_f, conv_b, ln_g, ln_b, pool_w_b, pool_scale, min(256, seq), carry)
    w_gu_t = rows(w_gu_all)

    carry = _Carry()
    _carry_gather(carry, [w_down[0].astype(BF16)], 5, 8)
    (h2, g_sav, u_sav, act), (w_down_all,) = _ffn_up(x1, w_gu_t, g_ffn, min(256, seq), carry)
    w_down_f = rows(w_down_all)

    carry = _Carry()
    _carry_gather(carry, [w_ple_gate[0].astype(BF16),
                          w_ple_up[0].T.astype(BF16)], 3, 5)
    (x2,), late = _ffn_down(x1, act, w_down_f, min(512, seq), carry)
    w_pg_f, w_pu_t = [rows(gth) for gth in late]
    dx2, dx2b, hg, ds, dpe, pb, stats_ple = _ple_loss(x2, ps, tgt, w_pg_f, w_pu_t, g_ple_gate, g_ple_post,
                                                      g_final.reshape(1, D_MODEL), min(256, seq))
    dg, du, dx1, dx1b, stats_ffn = _ffn_bwd(dx2, dx2b, x1, g_sav, u_sav, w_gu_t, w_down_f, g_ffn, min(256, seq))

    d_w_pg, _ = _grad_matmul(hg, ds, 256, "grad_w_ple_gate")
    d_w_pu_t, _ = _grad_matmul(dpe, pb, 256, "grad_w_ple_up")
    d_w_gu_t, _ = _grad_matmul(dg, h2, 256, "grad_w_gate_up", a2=du)
    d_w_down, _ = _grad_matmul(act, dx2b, 256, "grad_w_down")
    early = [_by_device(d_w_pg), _by_device(d_w_pu_t), _by_device(d_w_gu_t), _by_device(d_w_down)]
    carry = _Carry()
    _carry_pair(carry, early)
    d_w_out, from_sib = _grad_matmul(mix, dx1b, 256, "grad_w_out", carry=carry)
    early_parts = _pair_add(early, from_sib, blks, "pair_add_early")

    carry = _Carry()
    _carry_chip(carry, early_parts)
    _carry_pair(carry, [_by_device(d_w_out)])
    (dx, dz, h1, vec_mix, dconv_w_part, dpool_w_part), carried = _bwd_mix(
        dx1, dx1b, xs, z, u1, pooled, w_in_t, w_out_f, g_mix, conv_w_f, ln_g, ln_b, pool_w_b, pool_scale,
        min(256, seq), carry)
    early_chips, out_sib = carried[:4], carried[4:]
    out_parts = _pair_add([_by_device(d_w_out)], out_sib, blks, "pair_add_out")

    carry = _Carry()
    _carry_chip(carry, out_parts)
    _carry_gather(carry, [vec_mix, stats_ple, stats_ffn, dconv_w_part, dpool_w_part], 2, 4)
    d_w_in_t, carried = _grad_matmul(dz, h1, 256, "grad_w_in", carry=carry)
    out_chips, small = carried[:1], carried[1:]
    (gr_w_in_t,), _ = _reduce_scatter([_by_device(d_w_in_t)], [], "scatter_last")
    (gr_w_pu_t,) = _chip_sum(early_parts[1:2], early_chips[1:2], "chip_sum")

    vec_names = ["g_mix", "ln_g", "ln_b", "conv_b", "pool_scale", "g_final", "g_ple_post", "g_ple_gate", "g_ffn",
                 "pool_w"]
    layout = [(0, 0, D_MODEL), (0, 1, C_CONV), (0, 2, C_CONV), (0, 3, C_CONV), (0, 4, C_POOL),
              (1, 0, D_MODEL), (1, 1, D_MODEL), (1, 2, D_MODEL), (2, 0, D_MODEL), (4, None, None)]
    as_row = lambda t: t.reshape(1, D_MODEL)
    params = [(g_mix, m_g_mix, v_g_mix), (ln_g, m_ln_g, v_ln_g), (ln_b, m_ln_b, v_ln_b),
              (conv_b, m_conv_b, v_conv_b), (pool_scale, m_pool_scale, v_pool_scale),
              (as_row(g_final), as_row(m_g_final), as_row(v_g_final)),
              (g_ple_post, m_g_ple_post, v_g_ple_post), (g_ple_gate, m_g_ple_gate, v_g_ple_gate),
              (g_ffn, m_g_ffn, v_g_ffn), (pool_w[0], m_pool_w[0], v_pool_w[0])]
    tots, small_upd = _small_update(small, layout, params, "small_update")
    loss = tots[1][3, 0]
    upd = {}
    for nm, res, prm in zip(vec_names, small_upd, [g_mix, ln_g, ln_b, conv_b, pool_scale, g_final, g_ple_post,
                                                    g_ple_gate, g_ffn, pool_w]):
        upd[nm] = tuple(t.reshape(prm.shape) for t in res)
    gr_conv_w = lax.dynamic_slice_in_dim(tots[3][:CONV_K], me * (C_CONV // N_DEV), C_CONV // N_DEV, axis=1)

    fused = [
        ("w_ple_gate", early_parts[0], early_chips[0], w_ple_gate[0], m_w_ple_gate[0], v_w_ple_gate[0], False),
        ("w_gate_up", early_parts[2], early_chips[2], w_gate_up[0].T, m_w_gate_up[0].T, v_w_gate_up[0].T, True),
        ("w_down", early_parts[3], early_chips[3], w_down[0], m_w_down[0], v_w_down[0], False),
        ("w_out", out_parts[0], out_chips[0], w_out[0], m_w_out[0], v_w_out[0], False),
    ]
    for nm, part, chips3, w_, m_, v_, transposed in fused:
        res = _sum_adamw(part, chips3, w_, m_, v_, "adamw_" + nm)
        upd[nm] = tuple((t.T if transposed else t)[None] for t in res)
    plain = [
        ("w_in", w_in[0].T, gr_w_in_t, m_w_in[0].T, v_w_in[0].T, True),
        ("w_ple_up", w_ple_up[0], gr_w_pu_t.T, m_w_ple_up[0], v_w_ple_up[0], False),
        ("conv_w", conv_w[0], gr_conv_w, m_conv_w[0], v_conv_w[0], False),
    ]
    for nm, w_, g_, m_, v_, transposed in plain:
        res = (g_,) + tuple(_adamw(w_, g_, m_, v_, "adamw_" + nm))
        upd[nm] = tuple((t.T if transposed else t)[None] for t in res)

    order = ["g_mix", "w_in", "conv_w", "conv_b", "ln_g", "ln_b", "pool_w", "pool_scale", "w_out", "g_ffn",
             "w_gate_up", "w_down", "g_ple_gate", "w_ple_gate", "w_ple_up", "g_ple_post", "g_final"]
    outs = [loss, dx[None]]
    for k in range(4):
        outs += [upd[nm][k] for nm in order]
    return tuple(outs)
```

```python
import functools

import jax
import jax.numpy as jnp
from jax import lax
from jax.experimental import pallas as pl
from jax.experimental.pallas import tpu as pltpu

D_MODEL = 1024
C_CONV = 512
C_POOL = 512
Z_WIDTH = 2 * C_CONV + C_POOL
POOL_WINDOWS = (2, 4, 8, 16)
POOL_GROUP = 128
CONV_K = 31
D_FF = 2816
D_PLE = 256
EPS = 1e-6
N_DEV = 8

ADAM_LR = 0.001
ADAM_B1 = 0.9
ADAM_B2 = 0.999
ADAM_EPS = 1e-08
ADAM_WD = 0.01
ADAM_STEP = 10

CONV_HALO = 32
POOL_HALO = 16
ROW_CHUNK = 32
V7X_VMEM_LIMIT = 56 * 1024 * 1024
FF_CHUNK = D_FF // 2

BF16 = jnp.bfloat16
F32 = jnp.float32
MESH = pl.DeviceIdType.MESH


def _dot(a, b):
    return lax.dot_general(a, b, (((1,), (0,)), ((), ())), preferred_element_type=F32)


def _dot_nt(a, b):
    return lax.dot_general(a, b, (((1,), (1,)), ((), ())), preferred_element_type=F32)


def _dot_tn(a, b):
    return lax.dot_general(a, b, (((0,), (0,)), ((), ())), preferred_element_type=F32)


def _rms_fwd(x, g):
    r = lax.rsqrt(jnp.mean(x * x, axis=-1, keepdims=True) + EPS)
    return x * r * g, r


def _rms_bwd(x, r, g, dy):
    xr = x * r
    dg = jnp.sum(dy * xr, axis=0, keepdims=True)
    dyg = dy * g
    dx = r * (dyg - xr * jnp.mean(dyg * xr, axis=-1, keepdims=True))
    return dx, dg


def _sigmoid(x):
    return jax.nn.sigmoid(x)


def _params(sem=None):
    return pltpu.CompilerParams(dimension_semantics=sem, vmem_limit_bytes=V7X_VMEM_LIMIT)


def _place():
    x, y, c = lax.axis_index("x"), lax.axis_index("y"), lax.axis_index("c")
    chips = [(1 - x, y), (x, 1 - y), (1 - x, 1 - y)]
    return (x, y, c), (x, y, 1 - c), chips


def _block(px, py, pc):
    return 4 * px + 2 * py + pc


class _Carry:
    def __init__(self):
        self.inputs = []
        self.out_shapes = []
        self.copies = []
        self.locals = []

    def add_input(self, arr):
        self.inputs.append(arr)
        return len(self.inputs) - 1

    def add_output(self, shape, dtype):
        self.out_shapes.append(jax.ShapeDtypeStruct(shape, dtype))
        return len(self.out_shapes) - 1

    def local(self, src_idx, dst_idx, dst_blk):
        self.locals.append((src_idx, dst_idx, dst_blk))

    def copy(self, src, dst_idx, dst_blk, got_blk, peer, step=0, after=()):
        self.copies.append(dict(src=src, dst_idx=dst_idx, dst_blk=dst_blk, got_blk=got_blk, peer=peer, step=step,
                                after=tuple(after)))
        return len(self.copies) - 1

    def sem_shapes(self):
        return [pltpu.SemaphoreType.DMA((max(1, len(self.copies)),)),
                pltpu.SemaphoreType.DMA((max(1, len(self.copies)),)),
                pltpu.SemaphoreType.DMA((max(1, len(self.locals)),))]

    @staticmethod
    def _view(ref, where):
        if isinstance(where, tuple):
            blk, row0, nrows = where
            return ref.at[blk, pl.ds(row0, nrows)]
        return ref.at[where]

    def _desc(self, k, ins, outs, sems, place):
        cp = self.copies[k]
        me, sib, chips = place
        kind, idx, blk = cp["src"]
        src = (ins if kind == "in" else outs)[idx]
        if blk is not None:
            src = self._view(src, blk(*place))
        to = sib if cp["peer"] == "sib" else (*chips[cp["peer"]], me[2])
        return pltpu.make_async_remote_copy(
            src_ref=src, dst_ref=self._view(outs[cp["dst_idx"]], cp["dst_blk"](*place)),
            send_sem=sems[0].at[k], recv_sem=sems[1].at[k], device_id=to, device_id_type=MESH)

    def _arrival(self, k, outs, sems, place):
        cp = self.copies[k]
        got = self._view(outs[cp["dst_idx"]], cp["got_blk"](*place))
        return pltpu.make_async_remote_copy(
            src_ref=got, dst_ref=got, send_sem=sems[0].at[k], recv_sem=sems[1].at[k],
            device_id=place[0], device_id_type=MESH)

    def _local(self, n, ins, outs, sems, place):
        src_idx, dst_idx, blk = self.locals[n]
        return pltpu.make_async_copy(ins[src_idx], outs[dst_idx].at[blk(*place)], sems[2].at[n])

    def starts(self, step, nsteps, ins, outs, sems):
        place = _place()
        self._waited = set()
        for s in sorted({0} | {cp["step"] for cp in self.copies}):
            ks = [k for k, cp in enumerate(self.copies) if cp["step"] == s]

            @pl.when(step == min(s, nsteps - 1))
            def _(s=s, ks=ks):
                if s == 0:
                    for n in range(len(self.locals)):
                        self._local(n, ins, outs, sems, place).start()
                for k in ks:
                    for a in self.copies[k]["after"]:
                        if a not in self._waited:
                            self._arrival(a, outs, sems, place).wait_recv()
                            self._waited.add(a)
                    self._desc(k, ins, outs, sems, place).start()

    def finish(self, step, nsteps, ins, outs, sems):
        place = _place()

        @pl.when(step == nsteps - 1)
        def _():
            for k in range(len(self.copies)):
                if k not in self._waited:
                    self._arrival(k, outs, sems, place).wait_recv()
            for k in range(len(self.copies)):
                self._desc(k, ins, outs, sems, place).wait_send()
            for n in range(len(self.locals)):
                self._local(n, ins, outs, sems, place).wait()


def _const_blk(j):
    return lambda me, sib, chips: j


def _carry_gather(carry, shards, relay_step, last_step):
    outs = []
    for sh in shards:
        i = carry.add_input(sh)
        o = carry.add_output((N_DEV,) + sh.shape, sh.dtype)
        half = sh.shape[0] // 2
        tile = 16 if sh.dtype == BF16 else 8
        split = half % tile == 0
        rows = [(0, half), (half, sh.shape[0] - half)] if split else [(0, sh.shape[0]), None]

        def whole(j, core):
            return lambda me, sib, chips, j=j, core=core: _block(*chips[j], me[2] if core == 0 else 1 - me[2])

        def part(j, core, h, rows=rows):
            return lambda me, sib, chips: (_block(*chips[j], me[2] if core == 0 else 1 - me[2]),) + rows[h]

        mine = lambda me, sib, chips: _block(*me)
        carry.local(i, o, mine)
        carry.copy(("in", i, None), o, mine, lambda me, sib, chips: _block(*sib), "sib")
        near = [carry.copy(("in", i, None), o, mine, whole(j, 0), j) for j in range(2)]
        for j in range(2):
            carry.copy(("out", o, whole(j, 0)), o, whole(j, 0), whole(j, 1), "sib", step=relay_step, after=(near[j],))
        for j in range(2):
            if rows[j] is None:
                continue
            far = carry.copy(("out", o, part(j, 0, j)), o, part(j, 0, j), part(2, 0, j), 1 - j,
                             step=relay_step, after=(near[j],))
            carry.copy(("out", o, part(2, 0, j)), o, part(2, 0, j), part(2, 1, j), "sib", step=last_step, after=(far,))
        outs.append(o)
    return outs


def _carry_pair(carry, grads):
    outs = []
    for g in grads:
        i = carry.add_input(g)
        o = carry.add_output((4,) + g.shape[1:], g.dtype)
        for j in range(4):
            if j < 3:
                blk = lambda me, sib, chips, j=j: _block(*chips[j], 1 - me[2])
            else:
                blk = lambda me, sib, chips: _block(*sib)
            carry.copy(("in", i, blk), o, _const_blk(j), _const_blk(j), "sib")
        outs.append(o)
    return outs


def _carry_chip(carry, parts):
    outs = []
    for p in parts:
        i = carry.add_input(p)
        o = carry.add_output((3,) + p.shape[1:], p.dtype)
        for j in range(3):
            carry.copy(("in", i, _const_blk(j)), o, _const_blk(j), _const_blk(j), j)
        outs.append(o)
    return outs


def _pcall(body, *, grid, in_specs, out_specs, out_shape, scratch_shapes, name, args, carry=None):
    sem = ("arbitrary",) * len(grid)
    if carry is None:
        res = pl.pallas_call(body, grid=grid, in_specs=in_specs, out_specs=out_specs, out_shape=out_shape,
                             scratch_shapes=scratch_shapes, compiler_params=_params(sem), name=name)(*args)
        return list(res), []
    n_in, n_out, n_scr = len(in_specs), len(out_specs), len(scratch_shapes)
    c_in, c_out = len(carry.inputs), len(carry.out_shapes)
    nsteps = 1
    for extent in grid:
        nsteps *= extent

    def wrapped(*refs):
        ins = refs[:n_in]
        cins = refs[n_in:n_in + c_in]
        o0 = n_in + c_in
        outs = refs[o0:o0 + n_out]
        couts = refs[o0 + n_out:o0 + n_out + c_out]
        s0 = o0 + n_out + c_out
        scr = refs[s0:s0 + n_scr]
        sems = refs[s0 + n_scr:]
        step = pl.program_id(0)
        for d in range(1, len(grid)):
            step = step * grid[d] + pl.program_id(d)
        carry.starts(step, nsteps, cins, couts, sems)
        body(*ins, *outs, *scr)
        carry.finish(step, nsteps, cins, couts, sems)

    any_spec = pl.BlockSpec(memory_space=pl.ANY)
    res = pl.pallas_call(
        wrapped, grid=grid,
        in_specs=list(in_specs) + [any_spec] * c_in,
        out_specs=list(out_specs) + [any_spec] * c_out,
        out_shape=list(out_shape) + carry.out_shapes,
        scratch_shapes=list(scratch_shapes) + carry.sem_shapes(),
        compiler_params=_params(sem), name=name)(*args, *carry.inputs)
    return list(res[:n_out]), list(res[n_out:])


def _shifted_copies(buf, shifted, tm):
    span = tm + CONV_HALO - 8
    for r in range(1, 8):
        shifted[r - 1, 0:span, :] = buf[r:r + span, :]


def _rows_at(buf, shifted, start):
    aligned, r = (start // 8) * 8, start % 8
    if r == 0:
        return buf[aligned:aligned + ROW_CHUNK, :]
    return shifted[r - 1, aligned:aligned + ROW_CHUNK, :]


def _row_spec(tm, width):
    return pl.BlockSpec((tm, width), lambda i: (i, 0))


def _const_spec(shape):
    return pl.BlockSpec(shape, lambda i: (0,) * len(shape))


def _weight_spec(shape):
    return pl.BlockSpec(shape, lambda i: (0,) * len(shape), pipeline_mode=pl.Buffered(1))


def _fwd_mix(x, w_in_t, w_out, g_mix, conv_w, conv_b, ln_g, ln_b, pool_w, pool_scale, tm, carry=None):
    s = x.shape[0]
    nt = s // tm

    def body(x_ref, win_ref, wout_ref, gmix_ref, cw_ref, cb_ref, lng_ref, lnb_ref, pw_ref, ps_ref,
             z_ref, u1_ref, pooled_ref, x1_ref, mix_ref, ubuf, vbuf, ush):
        i = pl.program_id(0)

        @pl.when(i == 0)
        def _():
            ubuf[0:CONV_HALO, :] = jnp.zeros((CONV_HALO, C_CONV), F32)
            vbuf[0:POOL_HALO, :] = jnp.zeros((POOL_HALO, C_POOL), F32)

        xt = x_ref[...]
        h, _ = _rms_fwd(xt, gmix_ref[...])
        z = _dot_nt(h.astype(BF16), win_ref[...])
        z_ref[...] = z
        a = z[:, :C_CONV]
        b = z[:, C_CONV:2 * C_CONV]
        v = z[:, 2 * C_CONV:]
        ubuf[CONV_HALO:CONV_HALO + tm, :] = a * _sigmoid(b)
        vbuf[POOL_HALO:POOL_HALO + tm, :] = v

        _shifted_copies(ubuf, ush, tm)
        for rc in range(tm // ROW_CHUNK):
            base = rc * ROW_CHUNK + CONV_HALO - (CONV_K - 1)
            acc = jnp.broadcast_to(cb_ref[...], (ROW_CHUNK, C_CONV))
            for k in range(CONV_K):
                acc = acc + cw_ref[k:k + 1, :] * _rows_at(ubuf, ush, base + k)
            u1_ref[rc * ROW_CHUNK:(rc + 1) * ROW_CHUNK, :] = acc

        u1 = u1_ref[...]
        mu = jnp.mean(u1, axis=-1, keepdims=True)
        cen = u1 - mu
        rstd = lax.rsqrt(jnp.mean(cen * cen, axis=-1, keepdims=True) + EPS)
        u2 = cen * rstd * lng_ref[...] + lnb_ref[...]
        u = u2 * _sigmoid(u2)

        pos1 = (i * tm + lax.broadcasted_iota(jnp.int32, (tm, 1), 0) + 1).astype(F32)
        parts = [u]
        for g, w in enumerate(POOL_WINDOWS):
            cols = slice(g * POOL_GROUP, (g + 1) * POOL_GROUP)
            vg = v[:, cols]
            tot = vg
            for j in range(1, w):
                tot = tot + vbuf[POOL_HALO - j:POOL_HALO - j + tm, cols]
            pooled = tot / jnp.minimum(pos1, float(w)) - vg
            pooled_b = pooled.astype(BF16)
            pooled_ref[:, cols] = pooled_b
            parts.append(_dot(pooled_b, pw_ref[g]) * ps_ref[:, cols])
        mix = jnp.concatenate(parts, axis=-1).astype(BF16)
        mix_ref[...] = mix
        x1_ref[...] = xt + _dot(mix, wout_ref[...])

        ubuf[0:CONV_HALO, :] = ubuf[tm:tm + CONV_HALO, :]
        vbuf[0:POOL_HALO, :] = vbuf[tm:tm + POOL_HALO, :]

    return _pcall(
        body,
        grid=(nt,),
        in_specs=[
            _row_spec(tm, D_MODEL),
            _const_spec((Z_WIDTH, D_MODEL)),
            _const_spec((D_MODEL, D_MODEL)),
            _const_spec((1, D_MODEL)),
            _const_spec((CONV_HALO, C_CONV)),
            _const_spec((1, C_CONV)),
            _const_spec((1, C_CONV)),
            _const_spec((1, C_CONV)),
            _const_spec((len(POOL_WINDOWS), POOL_GROUP, POOL_GROUP)),
            _const_spec((1, C_POOL)),
        ],
        out_specs=[
            _row_spec(tm, Z_WIDTH),
            _row_spec(tm, C_CONV),
            _row_spec(tm, C_POOL),
            _row_spec(tm, D_MODEL),
            _row_spec(tm, D_MODEL),
        ],
        out_shape=[
            jax.ShapeDtypeStruct((s, Z_WIDTH), F32),
            jax.ShapeDtypeStruct((s, C_CONV), F32),
            jax.ShapeDtypeStruct((s, C_POOL), BF16),
            jax.ShapeDtypeStruct((s, D_MODEL), F32),
            jax.ShapeDtypeStruct((s, D_MODEL), BF16),
        ],
        scratch_shapes=[
            pltpu.VMEM((tm + CONV_HALO, C_CONV), F32),
            pltpu.VMEM((tm + POOL_HALO, C_POOL), F32),
            pltpu.VMEM((7, tm + CONV_HALO, C_CONV), F32),
        ],
        name="fwd_mix",
        args=(x, w_in_t, w_out, g_mix, conv_w, conv_b, ln_g, ln_b, pool_w, pool_scale),
        carry=carry,
    )


def _ffn_up(x1, w_gu_t, g_ffn, tm, carry=None):
    s = x1.shape[0]

    def body(x1_ref, w_ref, gffn_ref, h2_ref, g_ref, u_ref, act_ref):
        h, _ = _rms_fwd(x1_ref[...], gffn_ref[...])
        h2 = h.astype(BF16)
        h2_ref[...] = h2
        for c in range(D_FF // FF_CHUNK):
            cols = slice(c * FF_CHUNK, (c + 1) * FF_CHUNK)
            g = _dot_nt(h2, w_ref[c * FF_CHUNK:(c + 1) * FF_CHUNK, :])
            u = _dot_nt(h2, w_ref[D_FF + c * FF_CHUNK:D_FF + (c + 1) * FF_CHUNK, :])
            g_ref[:, cols] = g.astype(BF16)
            u_ref[:, cols] = u.astype(BF16)
            act_ref[:, cols] = (g * _sigmoid(g) * u).astype(BF16)

    return _pcall(
        body,
        grid=(s // tm,),
        in_specs=[_row_spec(tm, D_MODEL), _weight_spec((2 * D_FF, D_MODEL)), _const_spec((1, D_MODEL))],
        out_specs=[_row_spec(tm, D_MODEL), _row_spec(tm, D_FF), _row_spec(tm, D_FF), _row_spec(tm, D_FF)],
        out_shape=[
            jax.ShapeDtypeStruct((s, D_MODEL), BF16),
            jax.ShapeDtypeStruct((s, D_FF), BF16),
            jax.ShapeDtypeStruct((s, D_FF), BF16),
            jax.ShapeDtypeStruct((s, D_FF), BF16),
        ],
        scratch_shapes=[],
        name="ffn_up",
        args=(x1, w_gu_t, g_ffn),
        carry=carry,
    )


def _ffn_down(x1, act, w_down, tm, carry=None):
    s = x1.shape[0]

    def body(x1_ref, act_ref, wd_ref, x2_ref):
        x2_ref[...] = x1_ref[...] + _dot(act_ref[...], wd_ref[...])

    return _pcall(
        body,
        grid=(s // tm,),
        in_specs=[_row_spec(tm, D_MODEL), _row_spec(tm, D_FF), _weight_spec((D_FF, D_MODEL))],
        out_specs=[_row_spec(tm, D_MODEL)],
        out_shape=[jax.ShapeDtypeStruct((s, D_MODEL), F32)],
        scratch_shapes=[],
        name="ffn_down",
        args=(x1, act, w_down),
        carry=carry,
    )


def _ple_loss(x2, p, tgt, w_pg, w_pu_t, g_gate, g_post, g_final, tm):
    s = x2.shape[0]
    nt = s // tm

    def body(x2_ref, p_ref, t_ref, wpg_ref, wpu_ref, gg_ref, gp_ref, gf_ref,
             dx2_ref, dx2b_ref, hg_ref, ds_ref, dpe_ref, pb_ref, stats_ref):
        i = pl.program_id(0)

        @pl.when(i == 0)
        def _():
            stats_ref[...] = jnp.zeros_like(stats_ref)

        x2 = x2_ref[...]
        hg, rg = _rms_fwd(x2, gg_ref[...])
        hg_b = hg.astype(BF16)
        hg_ref[...] = hg_b
        gate = _sigmoid(_dot(hg_b, wpg_ref[...]))
        pb = p_ref[...].astype(BF16)
        pb_ref[...] = pb
        pe = _dot_nt(pb, wpu_ref[...])
        e, rp = _rms_fwd(pe, gp_ref[...])
        x3 = x2 + gate * e
        y, r3 = _rms_fwd(x3, gf_ref[...])
        diff = y - t_ref[...]
        loss = 0.5 * jnp.sum(jnp.sum(diff * diff, axis=-1, keepdims=True), axis=0, keepdims=True) / D_MODEL
        dy = diff * (1.0 / D_MODEL)

        dx3, dgf = _rms_bwd(x3, r3, gf_ref[...], dy)
        dpe, dgp = _rms_bwd(pe, rp, gp_ref[...], dx3 * gate)
        dpe_ref[...] = dpe.astype(BF16)
        ds = (dx3 * e * gate * (1.0 - gate)).astype(BF16)
        ds_ref[...] = ds
        dhg = _dot_nt(ds, wpg_ref[...])
        dxg, dgg = _rms_bwd(x2, rg, gg_ref[...], dhg)
        dx2 = dx3 + dxg
        dx2_ref[...] = dx2
        dx2b_ref[...] = dx2.astype(BF16)

        stats_ref[0:1, :] += dgf
        stats_ref[1:2, :] += dgp
        stats_ref[2:3, :] += dgg
        stats_ref[3:4, :] += jnp.broadcast_to(loss, (1, D_MODEL))

    return pl.pallas_call(
        body,
        grid=(nt,),
        in_specs=[
            _row_spec(tm, D_MODEL),
            _row_spec(tm, D_PLE),
            _row_spec(tm, D_MODEL),
            _const_spec((D_MODEL, D_MODEL)),
            _const_spec((D_MODEL, D_PLE)),
            _const_spec((1, D_MODEL)),
            _const_spec((1, D_MODEL)),
            _const_spec((1, D_MODEL)),
        ],
        out_specs=[
            _row_spec(tm, D_MODEL),
            _row_spec(tm, D_MODEL),
            _row_spec(tm, D_MODEL),
            _row_spec(tm, D_MODEL),
            _row_spec(tm, D_MODEL),
            _row_spec(tm, D_PLE),
            _const_spec((8, D_MODEL)),
        ],
        out_shape=[
            jax.ShapeDtypeStruct((s, D_MODEL), F32),
            jax.ShapeDtypeStruct((s, D_MODEL), BF16),
            jax.ShapeDtypeStruct((s, D_MODEL), BF16),
            jax.ShapeDtypeStruct((s, D_MODEL), BF16),
            jax.ShapeDtypeStruct((s, D_MODEL), BF16),
            jax.ShapeDtypeStruct((s, D_PLE), BF16),
            jax.ShapeDtypeStruct((8, D_MODEL), F32),
        ],
        compiler_params=_params(("arbitrary",)),
        name="ple_loss",
    )(x2, p, tgt, w_pg, w_pu_t, g_gate, g_post, g_final)


def _ffn_bwd(dx2, dx2b, x1, g_sav, u_sav, w_gu_t, w_down, g_ffn, tm, carry=None):
    s = x1.shape[0]

    def body(dx2_ref, dx2b_ref, x1_ref, g_ref, u_ref, w_ref, wd_ref, gffn_ref,
             dg_ref, du_ref, dx1_ref, dx1b_ref, stats_ref):
        @pl.when(pl.program_id(0) == 0)
        def _():
            stats_ref[...] = jnp.zeros_like(stats_ref)

        dx2b = dx2b_ref[...]
        dh2 = jnp.zeros((tm, D_MODEL), F32)
        for c in range(D_FF // FF_CHUNK):
            cols = slice(c * FF_CHUNK, (c + 1) * FF_CHUNK)
            dact = _dot_nt(dx2b, wd_ref[c * FF_CHUNK:(c + 1) * FF_CHUNK, :])
            g = g_ref[:, cols].astype(F32)
            u = u_ref[:, cols].astype(F32)
            sg = _sigmoid(g)
            dg = (dact * u * sg * (1.0 + g * (1.0 - sg))).astype(BF16)
            du = (dact * g * sg).astype(BF16)
            dg_ref[:, cols] = dg
            du_ref[:, cols] = du
            dh2 = dh2 + _dot(dg, w_ref[c * FF_CHUNK:(c + 1) * FF_CHUNK, :])
            dh2 = dh2 + _dot(du, w_ref[D_FF + c * FF_CHUNK:D_FF + (c + 1) * FF_CHUNK, :])

        x1 = x1_ref[...]
        r2 = lax.rsqrt(jnp.mean(x1 * x1, axis=-1, keepdims=True) + EPS)
        dxn, dgf = _rms_bwd(x1, r2, gffn_ref[...], dh2)
        dx1 = dx2_ref[...] + dxn
        dx1_ref[...] = dx1
        dx1b_ref[...] = dx1.astype(BF16)
        stats_ref[0:1, :] += dgf

    return _pcall(
        body,
        grid=(s // tm,),
        in_specs=[
            _row_spec(tm, D_MODEL), _row_spec(tm, D_MODEL), _row_spec(tm, D_MODEL),
            _row_spec(tm, D_FF), _row_spec(tm, D_FF),
            _weight_spec((2 * D_FF, D_MODEL)), _weight_spec((D_FF, D_MODEL)), _const_spec((1, D_MODEL)),
        ],
        out_specs=[_row_spec(tm, D_FF), _row_spec(tm, D_FF), _row_spec(tm, D_MODEL), _row_spec(tm, D_MODEL),
                   _const_spec((8, D_MODEL))],
        out_shape=[
            jax.ShapeDtypeStruct((s, D_FF), BF16),
            jax.ShapeDtypeStruct((s, D_FF), BF16),
            jax.ShapeDtypeStruct((s, D_MODEL), F32),
            jax.ShapeDtypeStruct((s, D_MODEL), BF16),
            jax.ShapeDtypeStruct((8, D_MODEL), F32),
        ],
        scratch_shapes=[],
        name="ffn_bwd",
        args=(dx2, dx2b, x1, g_sav, u_sav, w_gu_t, w_down, g_ffn),
        carry=carry,
    )


def _bwd_mix(dx1, dx1b, x, z, u1, pooled, w_in_t, w_out, g_mix, conv_w, ln_g, ln_b, pool_w, pool_scale, tm,
             carry=None):
    s = x.shape[0]
    nt = s // tm

    def body(dx1_ref, dx1b_ref, x_ref, z_ref, u1_ref, pooled_ref, win_ref, wout_ref, gmix_ref, cw_ref,
             lng_ref, lnb_ref, pw_ref, ps_ref,
             dx_ref, dz_ref, h1_ref, vec_ref, dcw_ref, dpw_ref, dubuf, dvbuf, u0buf, du0buf, dush):
        i = pl.program_id(0)
        tile = nt - 1 - i

        @pl.when(i == 0)
        def _():
            vec_ref[...] = jnp.zeros_like(vec_ref)
            dcw_ref[...] = jnp.zeros_like(dcw_ref)
            dpw_ref[...] = jnp.zeros_like(dpw_ref)
            dubuf[tm:tm + CONV_HALO, :] = jnp.zeros((CONV_HALO, C_CONV), F32)
            dvbuf[tm:tm + POOL_HALO, :] = jnp.zeros((POOL_HALO, C_POOL), F32)

        dmix = _dot_nt(dx1b_ref[...], wout_ref[...])
        du = dmix[:, :C_CONV]
        dq = dmix[:, C_CONV:]

        pos1 = (tile * tm + lax.broadcasted_iota(jnp.int32, (tm, 1), 0) + 1).astype(F32)
        dpooled_parts = []
        dps_rows = []
        for g, w in enumerate(POOL_WINDOWS):
            cols = slice(g * POOL_GROUP, (g + 1) * POOL_GROUP)
            pooled_b = pooled_ref[:, cols]
            mixed = _dot(pooled_b, pw_ref[g])
            dqg = dq[:, cols]
            dps_rows.append(jnp.sum(dqg * mixed, axis=0, keepdims=True))
            dmixed = (dqg * ps_ref[:, cols]).astype(BF16)
            dpw_ref[g] += _dot_tn(pooled_b, dmixed)
            dpooled = _dot_nt(dmixed, pw_ref[g])
            dpooled_parts.append(dpooled)
            dvbuf[0:tm, cols] = dpooled / jnp.minimum(pos1, float(w))
        vec_ref[4:5, 0:C_POOL] += jnp.concatenate(dps_rows, axis=-1)
        dv_parts = []
        for g, w in enumerate(POOL_WINDOWS):
            cols = slice(g * POOL_GROUP, (g + 1) * POOL_GROUP)
            tot = dvbuf[0:tm, cols]
            for j in range(1, w):
                tot = tot + dvbuf[j:j + tm, cols]
            dv_parts.append(tot - dpooled_parts[g])

        u1 = u1_ref[...]
        mu = jnp.mean(u1, axis=-1, keepdims=True)
        cen = u1 - mu
        rstd = lax.rsqrt(jnp.mean(cen * cen, axis=-1, keepdims=True) + EPS)
        xhat = cen * rstd
        u2 = xhat * lng_ref[...] + lnb_ref[...]
        sg2 = _sigmoid(u2)
        du2 = du * sg2 * (1.0 + u2 * (1.0 - sg2))
        vec_ref[1:2, 0:C_CONV] += jnp.sum(du2 * xhat, axis=0, keepdims=True)
        vec_ref[2:3, 0:C_CONV] += jnp.sum(du2, axis=0, keepdims=True)
        t1 = du2 * lng_ref[...]
        du1 = rstd * (t1 - jnp.mean(t1, axis=-1, keepdims=True)
                      - xhat * jnp.mean(t1 * xhat, axis=-1, keepdims=True))
        vec_ref[3:4, 0:C_CONV] += jnp.sum(du1, axis=0, keepdims=True)
        dubuf[0:tm, :] = du1

        zt = z_ref[...]
        a = zt[:, :C_CONV]
        sgb = _sigmoid(zt[:, C_CONV:2 * C_CONV])
        u0buf[...] = a * sgb

        _shifted_copies(dubuf, dush, tm)
        for rc in range(tm // ROW_CHUNK):
            r0 = rc * ROW_CHUNK
            acc = jnp.zeros((ROW_CHUNK, C_CONV), F32)
            for k in range(CONV_K):
                acc = acc + cw_ref[k:k + 1, :] * _rows_at(dubuf, dush, r0 + (CONV_K - 1) - k)
            du0buf[r0:r0 + ROW_CHUNK, :] = acc
        for k in range(CONV_K):
            acc = jnp.zeros((ROW_CHUNK, C_CONV), F32)
            for rc in range(tm // ROW_CHUNK):
                r0 = rc * ROW_CHUNK
                acc = acc + u0buf[r0:r0 + ROW_CHUNK, :] * _rows_at(dubuf, dush, r0 + (CONV_K - 1) - k)
            dcw_ref[k:k + 1, :] += jnp.sum(acc, axis=0, keepdims=True)
        du0 = du0buf[...]

        da = du0 * sgb
        db = du0 * a * sgb * (1.0 - sgb)
        dz = jnp.concatenate([da, db] + dv_parts, axis=-1).astype(BF16)
        dz_ref[...] = dz

        xt = x_ref[...]
        h1, r1 = _rms_fwd(xt, gmix_ref[...])
        h1_ref[...] = h1.astype(BF16)
        dh1 = _dot(dz, win_ref[...])
        dxn, dgm = _rms_bwd(xt, r1, gmix_ref[...], dh1)
        dx_ref[...] = dx1_ref[...] + dxn
        vec_ref[0:1, :] += dgm

        dubuf[tm:tm + CONV_HALO, :] = dubuf[0:CONV_HALO, :]
        dvbuf[tm:tm + POOL_HALO, :] = dvbuf[0:POOL_HALO, :]

    rev = lambda width: pl.BlockSpec((tm, width), lambda i: (nt - 1 - i, 0))
    return _pcall(
        body,
        grid=(nt,),
        in_specs=[
            rev(D_MODEL), rev(D_MODEL), rev(D_MODEL), rev(Z_WIDTH), rev(C_CONV), rev(C_POOL),
            _const_spec((Z_WIDTH, D_MODEL)),
            _const_spec((D_MODEL, D_MODEL)),
            _const_spec((1, D_MODEL)),
            _const_spec((CONV_HALO, C_CONV)),
            _const_spec((1, C_CONV)),
            _const_spec((1, C_CONV)),
            _const_spec((len(POOL_WINDOWS), POOL_GROUP, POOL_GROUP)),
            _const_spec((1, C_POOL)),
        ],
        out_specs=[
            rev(D_MODEL), rev(Z_WIDTH), rev(D_MODEL),
            _const_spec((8, D_MODEL)),
            _const_spec((CONV_HALO, C_CONV)),
            _const_spec((len(POOL_WINDOWS), POOL_GROUP, POOL_GROUP)),
        ],
        out_shape=[
            jax.ShapeDtypeStruct((s, D_MODEL), F32),
            jax.ShapeDtypeStruct((s, Z_WIDTH), BF16),
            jax.ShapeDtypeStruct((s, D_MODEL), BF16),
            jax.ShapeDtypeStruct((8, D_MODEL), F32),
            jax.ShapeDtypeStruct((CONV_HALO, C_CONV), F32),
            jax.ShapeDtypeStruct((len(POOL_WINDOWS), POOL_GROUP, POOL_GROUP), F32),
        ],
        scratch_shapes=[
            pltpu.VMEM((tm + CONV_HALO, C_CONV), F32),
            pltpu.VMEM((tm + POOL_HALO, C_POOL), F32),
            pltpu.VMEM((tm, C_CONV), F32),
            pltpu.VMEM((tm, C_CONV), F32),
            pltpu.VMEM((7, tm + CONV_HALO, C_CONV), F32),
        ],
        name="bwd_mix",
        args=(dx1, dx1b, x, z, u1, pooled, w_in_t, w_out, g_mix, conv_w, ln_g, ln_b, pool_w, pool_scale),
        carry=carry,
    )


def _grad_matmul(a, b, bm, name, a2=None, carry=None):
    s, ma = a.shape
    nb = b.shape[1]
    na = ma // bm
    if a2 is None:
        def body(a_ref, b_ref, o_ref):
            o_ref[...] = _dot_tn(a_ref[...], b_ref[...]).astype(BF16)

        lhs_specs = [pl.BlockSpec((s, bm), lambda i: (0, i))]
        lhs = (a,)
        steps = na
    else:
        def body(a_ref, a2_ref, b_ref, o_ref):
            i = pl.program_id(0)

            @pl.when(i < na)
            def _():
                o_ref[...] = _dot_tn(a_ref[...], b_ref[...]).astype(BF16)

            @pl.when(i >= na)
            def _():
                o_ref[...] = _dot_tn(a2_ref[...], b_ref[...]).astype(BF16)

        lhs_specs = [pl.BlockSpec((s, bm), lambda i: (0, jnp.minimum(i, na - 1))),
                     pl.BlockSpec((s, bm), lambda i: (0, jnp.maximum(i - na, 0)))]
        lhs = (a, a2)
        steps = 2 * na

    outs, carried = _pcall(
        body,
        grid=(steps,),
        in_specs=lhs_specs + [pl.BlockSpec((s, nb), lambda i: (0, 0))],
        out_specs=[pl.BlockSpec((bm, nb), lambda i: (i, 0))],
        out_shape=[jax.ShapeDtypeStruct((steps * bm, nb), BF16)],
        scratch_shapes=[],
        name=name,
        args=lhs + (b,),
        carry=carry,
    )
    return outs[0], carried


def _gather_steps(src, dst, send_sems, recv_sems, local_sems):
    n = len(src)
    me, sib, chips = _place()
    c = me[2]
    started = []
    mine = []

    def copy(a, k, block, to, from_src=False):
        rows = dst[a].at[_block(*block)]
        return pltpu.make_async_remote_copy(
            src_ref=src[a] if from_src else rows, dst_ref=rows,
            send_sem=send_sems.at[a, k], recv_sem=recv_sems.at[a, k],
            device_id=to, device_id_type=MESH)

    def start():
        for a in range(n):
            cp = pltpu.make_async_copy(src[a], dst[a].at[_block(*me)], local_sems.at[a])
            cp.start()
            mine.append(cp)
            first = [copy(a, 0, me, sib, True)]
            first += [copy(a, 1 + j, me, (*chip, c), True) for j, chip in enumerate(chips)]
            for cp in first:
                cp.start()
            started.extend(first)

    def forward():
        for a in range(n):
            for j, chip in enumerate(chips):
                copy(a, 1 + j, (*chip, c), me).wait_recv()
                fwd = copy(a, 4 + j, (*chip, c), sib)
                fwd.start()
                started.append(fwd)

    def finish():
        for a in range(n):
            copy(a, 0, sib, me).wait_recv()
            for j, chip in enumerate(chips):
                copy(a, 4 + j, (*chip, 1 - c), me).wait_recv()
        for cp in started:
            cp.wait_send()
        for cp in mine:
            cp.wait()

    return start, forward, finish


def _gather_sems(n):
    return [pltpu.SemaphoreType.DMA((n, 7)), pltpu.SemaphoreType.DMA((n, 7)), pltpu.SemaphoreType.DMA((n,))]


def _all_gather(shards, name):
    n = len(shards)
    carry = _Carry()
    _carry_gather(carry, shards, 1, 2)

    def body(*refs):
        step = jnp.int32(0)
        carry.starts(step, 1, refs[:n], refs[n:2 * n], refs[2 * n:])
        carry.finish(step, 1, refs[:n], refs[n:2 * n], refs[2 * n:])

    any_spec = pl.BlockSpec(memory_space=pl.ANY)
    return pl.pallas_call(
        body,
        in_specs=[any_spec] * n,
        out_specs=[any_spec] * n,
        out_shape=carry.out_shapes,
        scratch_shapes=carry.sem_shapes(),
        name=name,
    )(*shards)


def _reduce_scatter(grads, small, name):
    n, ns = len(grads), len(small)
    shapes = [g.shape[1:] for g in grads]

    def body(*refs):
        g = refs[:n]
        out = refs[n + ns:2 * n + ns]
        scr = refs[2 * (n + ns):]
        own, loc, r1, r2 = scr[:n], scr[n:2 * n], scr[2 * n:3 * n], scr[3 * n:4 * n]
        load_sems, s1, q1, s2, q2 = scr[4 * n:4 * n + 5]
        gather_start = gather_forward = gather_finish = lambda: None
        if ns:
            gather_start, gather_forward, gather_finish = _gather_steps(
                refs[n:n + ns], refs[2 * n + ns:2 * (n + ns)], *scr[4 * n + 5:])
        me, sib, chips = _place()
        c = me[2]

        gather_start()
        loads = []
        sends = []
        for a in range(n):
            ld = [pltpu.make_async_copy(g[a].at[_block(*chip, c)], loc[a].at[j], load_sems.at[a, j])
                  for j, chip in enumerate(chips)]
            ld.append(pltpu.make_async_copy(g[a].at[_block(*me)], own[a], load_sems.at[a, 3]))
            for cp in ld:
                cp.start()
            loads.append(ld)
            blocks = [(*chip, 1 - c) for chip in chips] + [sib]
            for j, blk in enumerate(blocks):
                cp = pltpu.make_async_remote_copy(
                    src_ref=g[a].at[_block(*blk)], dst_ref=r1[a].at[j],
                    send_sem=s1.at[a, j], recv_sem=q1.at[a, j], device_id=sib, device_id_type=MESH)
                cp.start()
                sends.append(cp)

        def from_sibling(a, j):
            return pltpu.make_async_remote_copy(
                src_ref=r1[a].at[j], dst_ref=r1[a].at[j], send_sem=s1.at[a, j], recv_sem=q1.at[a, j],
                device_id=sib, device_id_type=MESH)

        def partial(a, j, chip):
            return pltpu.make_async_remote_copy(
                src_ref=loc[a].at[j], dst_ref=r2[a].at[j], send_sem=s2.at[a, j], recv_sem=q2.at[a, j],
                device_id=(*chip, c), device_id_type=MESH)

        gather_forward()
        for a in range(n):
            for j, chip in enumerate(chips):
                loads[a][j].wait()
                from_sibling(a, j).wait_recv()
                loc[a][j] = (loc[a][j].astype(F32) + r1[a][j].astype(F32)).astype(BF16)
                cp = partial(a, j, chip)
                cp.start()
                sends.append(cp)
        for a in range(n):
            loads[a][3].wait()
            from_sibling(a, 3).wait_recv()
            acc = own[a][...].astype(F32) + r1[a][3].astype(F32)
            for j, chip in enumerate(chips):
                partial(a, j, chip).wait_recv()
                acc = acc + r2[a][j].astype(F32)
            out[a][...] = acc
        for cp in sends:
            cp.wait_send()
        gather_finish()

    any_spec = pl.BlockSpec(memory_space=pl.ANY)
    vmem_spec = pl.BlockSpec(memory_space=pltpu.VMEM)
    res = pl.pallas_call(
        body,
        in_specs=[any_spec] * (n + ns),
        out_specs=[vmem_spec] * n + [any_spec] * ns,
        out_shape=([jax.ShapeDtypeStruct(sh, F32) for sh in shapes]
                   + [jax.ShapeDtypeStruct((N_DEV,) + sm.shape, sm.dtype) for sm in small]),
        scratch_shapes=(
            [pltpu.VMEM(sh, BF16) for sh in shapes]
            + [pltpu.VMEM((3,) + sh, BF16) for sh in shapes]
            + [pltpu.VMEM((4,) + sh, BF16) for sh in shapes]
            + [pltpu.VMEM((3,) + sh, BF16) for sh in shapes]
            + [pltpu.SemaphoreType.DMA((n, 4)),
               pltpu.SemaphoreType.DMA((n, 4)), pltpu.SemaphoreType.DMA((n, 4)),
               pltpu.SemaphoreType.DMA((n, 3)), pltpu.SemaphoreType.DMA((n, 3))]
            + (_gather_sems(ns) if ns else [])
        ),
        compiler_params=pltpu.CompilerParams(vmem_limit_bytes=V7X_VMEM_LIMIT),
        name=name,
    )(*grads, *small)
    return res[:n], res[n:]


def _pair_add(grads, from_sib, blks, name):
    n = len(grads)

    def body(blk_ref, *refs):
        for a in range(n):
            refs[2 * n + a][...] = (refs[a][...].astype(F32) + refs[n + a][...].astype(F32)).astype(BF16)

    mine = [pl.BlockSpec((None,) + g.shape[1:], lambda j, b: (b[j], 0, 0)) for g in grads]
    same = [pl.BlockSpec((None,) + g.shape[1:], lambda j, b: (j, 0, 0)) for g in grads]
    return pl.pallas_call(
        body,
        grid_spec=pltpu.PrefetchScalarGridSpec(
            num_scalar_prefetch=1, grid=(4,), in_specs=mine + same, out_specs=same),
        out_shape=[jax.ShapeDtypeStruct((4,) + g.shape[1:], BF16) for g in grads],
        compiler_params=_params(("arbitrary",)),
        name=name,
    )(blks, *grads, *from_sib)


def _chip_sum(parts, from_chips, name):
    n = len(parts)

    def body(*refs):
        for a in range(n):
            acc = refs[a][...].astype(F32)
            for j in range(3):
                acc = acc + refs[n + a][j].astype(F32)
            refs[2 * n + a][...] = acc

    half = [p.shape[1] // 2 for p in parts]
    return pl.pallas_call(
        body,
        grid=(2,),
        in_specs=([pl.BlockSpec((None, h, p.shape[2]), lambda i: (3, i, 0)) for p, h in zip(parts, half)]
                  + [pl.BlockSpec((3, h, p.shape[2]), lambda i: (0, i, 0)) for p, h in zip(parts, half)]),
        out_specs=[pl.BlockSpec((h, p.shape[2]), lambda i: (i, 0)) for p, h in zip(parts, half)],
        out_shape=[jax.ShapeDtypeStruct(p.shape[1:], F32) for p in parts],
        compiler_params=_params(("arbitrary",)),
        name=name,
    )(*parts, *from_chips)


def _adam_math(w, g, m, v):
    nm = ADAM_B1 * m + (1.0 - ADAM_B1) * g
    nv = ADAM_B2 * v + (1.0 - ADAM_B2) * (g * g)
    m_hat = nm / (1.0 - ADAM_B1 ** ADAM_STEP)
    v_hat = nv / (1.0 - ADAM_B2 ** ADAM_STEP)
    return -ADAM_LR * (m_hat / (jnp.sqrt(v_hat) + ADAM_EPS) + ADAM_WD * w), nm, nv


def _sum_adamw(items, name, carry=None):
    n = len(items)

    def body(*refs):
        for a in range(n):
            p_ref, f_ref, w_ref, m_ref, v_ref = refs[5 * a:5 * a + 5]
            g_ref, d_ref, nm_ref, nv_ref = refs[5 * n + 4 * a:5 * n + 4 * a + 4]
            g = p_ref[...].astype(F32)
            for j in range(3):
                g = g + f_ref[j].astype(F32)
            g_ref[...] = g
            d_ref[...], nm_ref[...], nv_ref[...] = _adam_math(w_ref[...], g, m_ref[...], v_ref[...])

    in_specs, out_specs, out_shape, args = [], [], [], []
    for part, from_chips, w, m, v in items:
        r, c = w.shape
        spec = pl.BlockSpec((r // 2, c), lambda i: (i, 0))
        in_specs += [pl.BlockSpec((None, r // 2, c), lambda i: (3, i, 0)),
                     pl.BlockSpec((3, r // 2, c), lambda i: (0, i, 0)), spec, spec, spec]
        out_specs += [spec] * 4
        out_shape += [jax.ShapeDtypeStruct((r, c), F32)] * 4
        args += [part, from_chips, w, m, v]
    outs, carried = _pcall(body, grid=(2,), in_specs=in_specs, out_specs=out_specs, out_shape=out_shape,
                           scratch_shapes=[], name=name, args=tuple(args), carry=carry)
    return [tuple(outs[4 * a:4 * a + 4]) for a in range(n)], carried


def _small_update(gathered, layout, params, name):
    ng, npar = len(gathered), len(params)

    def body(*refs):
        parts = refs[:ng]
        prm = refs[ng:ng + 3 * npar]
        tot_refs = refs[ng + 3 * npar:2 * ng + 3 * npar]
        out = refs[2 * ng + 3 * npar:]
        tots = []
        for a in range(ng):
            acc = parts[a][0]
            for d in range(1, N_DEV):
                acc = acc + parts[a][d]
            tot_refs[a][...] = acc
            tots.append(acc)
        for i, (a, row, width) in enumerate(layout):
            g = tots[a] if row is None else tots[a][row:row + 1, :width]
            delta, nm, nv = _adam_math(prm[3 * i][...], g, prm[3 * i + 1][...], prm[3 * i + 2][...])
            out[4 * i][...] = g
            out[4 * i + 1][...] = delta
            out[4 * i + 2][...] = nm
            out[4 * i + 3][...] = nv

    flat = [t for prm in params for t in prm]
    res = pl.pallas_call(
        body,
        out_shape=([jax.ShapeDtypeStruct(g.shape[1:], F32) for g in gathered]
                   + [jax.ShapeDtypeStruct(prm[0].shape, F32) for prm in params for _ in range(4)]),
        compiler_params=pltpu.CompilerParams(vmem_limit_bytes=V7X_VMEM_LIMIT),
        name=name,
    )(*gathered, *flat)
    return res[:ng], [tuple(res[ng + 4 * i:ng + 4 * i + 4]) for i in range(npar)]


def _adamw(w, g, m, v, name):
    rows, cols = w.shape
    br = rows
    for cand in (512, 256, 128):
        if rows % cand == 0 and rows > cand:
            br = cand
            break

    def body(w_ref, g_ref, m_ref, v_ref, d_ref, nm_ref, nv_ref):
        d_ref[...], nm_ref[...], nv_ref[...] = _adam_math(w_ref[...], g_ref[...], m_ref[...], v_ref[...])

    spec = pl.BlockSpec((br, cols), lambda i: (i, 0))
    shape = jax.ShapeDtypeStruct((rows, cols), F32)
    return pl.pallas_call(
        body,
        grid=(rows // br,),
        in_specs=[spec] * 4,
        out_specs=[spec] * 3,
        out_shape=[shape] * 3,
        compiler_params=_params(("arbitrary",)),
        name=name,
    )(w, g, m, v)


def _by_device(full):
    return full.reshape(N_DEV, full.shape[0] // N_DEV, full.shape[1])


def kernel(x, p, g_mix, w_in, conv_w, conv_b, ln_g, ln_b, pool_w, pool_scale, w_out, g_ffn, w_gate_up, w_down, g_ple_gate, w_ple_gate, w_ple_up, g_ple_post, g_final, loss_target, m_g_mix, m_w_in, m_conv_w, m_conv_b, m_ln_g, m_ln_b, m_pool_w, m_pool_scale, m_w_out, m_g_ffn, m_w_gate_up, m_w_down, m_g_ple_gate, m_w_ple_gate, m_w_ple_up, m_g_ple_post, m_g_final, v_g_mix, v_w_in, v_conv_w, v_conv_b, v_ln_g, v_ln_b, v_pool_w, v_pool_scale, v_w_out, v_g_ffn, v_w_gate_up, v_w_down, v_g_ple_gate, v_w_ple_gate, v_w_ple_up, v_g_ple_post, v_g_final):
    seq = x.shape[1]
    xs = x[0]
    ps = p[0, 0]
    tgt = loss_target[0]
    ax, ay, ac = lax.axis_index("x"), lax.axis_index("y"), lax.axis_index("c")
    me = _block(ax, ay, ac)
    blks = jnp.stack([_block(1 - ax, ay, ac), _block(ax, 1 - ay, ac), _block(1 - ax, 1 - ay, ac), me]).astype(jnp.int32)
    rows = lambda gth: gth.reshape((-1,) + gth.shape[2:])

    w_in_t, w_out_f, conv_w_t = [rows(gth) for gth in _all_gather([
        w_in[0].T.astype(BF16),
        w_out[0].astype(BF16),
        jnp.pad(conv_w[0].T, ((0, 0), (0, CONV_HALO - CONV_K))),
    ], "gather_first")]
    conv_w_f = conv_w_t.T
    pool_w_b = pool_w[0].astype(BF16)

    carry = _Carry()
    _carry_gather(carry, [w_gate_up[0].T.astype(BF16)], 8, 12)
    (z, u1, pooled, x1, mix), (w_gu_all,) = _fwd_mix(
        xs, w_in_t, w_out_f, g_mix, conv_w_f, conv_b, ln_g, ln_b, pool_w_b, pool_scale, min(256, seq), carry)
    w_gu_t = rows(w_gu_all)

    carry = _Carry()
    _carry_gather(carry, [w_down[0].astype(BF16)], 5, 8)
    (h2, g_sav, u_sav, act), (w_down_all,) = _ffn_up(x1, w_gu_t, g_ffn, min(256, seq), carry)
    w_down_f = rows(w_down_all)

    carry = _Carry()
    _carry_gather(carry, [w_ple_gate[0].astype(BF16),
                          w_ple_up[0].T.astype(BF16)], 3, 5)
    (x2,), late = _ffn_down(x1, act, w_down_f, min(512, seq), carry)
    w_pg_f, w_pu_t = [rows(gth) for gth in late]
    dx2, dx2b, hg, ds, dpe, pb, stats_ple = _ple_loss(x2, ps, tgt, w_pg_f, w_pu_t, g_ple_gate, g_ple_post,
                                                      g_final.reshape(1, D_MODEL), min(256, seq))

    d_w_pg, _ = _grad_matmul(hg, ds, 256, "grad_w_ple_gate")
    d_w_pu_t, _ = _grad_matmul(dpe, pb, 256, "grad_w_ple_up")
    ple = [_by_device(d_w_pg), _by_device(d_w_pu_t)]
    carry = _Carry()
    _carry_pair(carry, ple)
    (dg, du, dx1, dx1b, stats_ffn), ple_sib = _ffn_bwd(dx2, dx2b, x1, g_sav, u_sav, w_gu_t, w_down_f, g_ffn,
                                                       min(256, seq), carry)
    ple_parts = _pair_add(ple, ple_sib, blks, "pair_add_ple")

    carry = _Carry()
    _carry_chip(carry, ple_parts)
    d_w_gu_t, ple_chips = _grad_matmul(dg, h2, 256, "grad_w_gate_up", a2=du, carry=carry)
    d_w_down, _ = _grad_matmul(act, dx2b, 256, "grad_w_down")
    ffn = [_by_device(d_w_gu_t), _by_device(d_w_down)]
    carry = _Carry()
    _carry_pair(carry, ffn)
    d_w_out, ffn_sib = _grad_matmul(mix, dx1b, 256, "grad_w_out", carry=carry)
    ffn_parts = _pair_add(ffn, ffn_sib, blks, "pair_add_ffn")

    carry = _Carry()
    _carry_chip(carry, ffn_parts)
    _carry_pair(carry, [_by_device(d_w_out)])
    (dx, dz, h1, vec_mix, dconv_w_part, dpool_w_part), carried = _bwd_mix(
        dx1, dx1b, xs, z, u1, pooled, w_in_t, w_out_f, g_mix, conv_w_f, ln_g, ln_b, pool_w_b, pool_scale,
        min(256, seq), carry)
    ffn_chips, out_sib = carried[:2], carried[2:]
    out_parts = _pair_add([_by_device(d_w_out)], out_sib, blks, "pair_add_out")

    carry = _Carry()
    _carry_chip(carry, out_parts)
    _carry_gather(carry, [vec_mix, stats_ple, stats_ffn, dconv_w_part,
                          dpool_w_part.reshape(-1, POOL_GROUP)], 2, 4)
    d_w_in_t, carried = _grad_matmul(dz, h1, 256, "grad_w_in", carry=carry)
    out_chips, small = carried[:1], carried[1:]
    small[4] = small[4].reshape((N_DEV,) + dpool_w_part.shape)

    vec_names = ["g_mix", "ln_g", "ln_b", "conv_b", "pool_scale", "g_final", "g_ple_post", "g_ple_gate", "g_ffn",
                 "pool_w"]
    layout = [(0, 0, D_MODEL), (0, 1, C_CONV), (0, 2, C_CONV), (0, 3, C_CONV), (0, 4, C_POOL),
              (1, 0, D_MODEL), (1, 1, D_MODEL), (1, 2, D_MODEL), (2, 0, D_MODEL), (4, None, None)]
    as_row = lambda t: t.reshape(1, D_MODEL)
    params = [(g_mix, m_g_mix, v_g_mix), (ln_g, m_ln_g, v_ln_g), (ln_b, m_ln_b, v_ln_b),
              (conv_b, m_conv_b, v_conv_b), (pool_scale, m_pool_scale, v_pool_scale),
              (as_row(g_final), as_row(m_g_final), as_row(v_g_final)),
              (g_ple_post, m_g_ple_post, v_g_ple_post), (g_ple_gate, m_g_ple_gate, v_g_ple_gate),
              (g_ffn, m_g_ffn, v_g_ffn), (pool_w[0], m_pool_w[0], v_pool_w[0])]
    tots, small_upd = _small_update(small, layout, params, "small_update")
    loss = tots[1][3, 0]
    upd = {}
    for nm, res, prm in zip(vec_names, small_upd, [g_mix, ln_g, ln_b, conv_b, pool_scale, g_final, g_ple_post,
                                                    g_ple_gate, g_ffn, pool_w]):
        upd[nm] = tuple(t.reshape(prm.shape) for t in res)
    gr_conv_w = lax.dynamic_slice_in_dim(tots[3][:CONV_K], me * (C_CONV // N_DEV), C_CONV // N_DEV, axis=1)

    natural = lambda t: t[None]
    turned = lambda t: t.T[None]
    carry = _Carry()
    _carry_pair(carry, [_by_device(d_w_in_t)])
    (res_pg, res_out), in_sib = _sum_adamw([
        (ple_parts[0], ple_chips[0], w_ple_gate[0], m_w_ple_gate[0], v_w_ple_gate[0]),
        (out_parts[0], out_chips[0], w_out[0], m_w_out[0], v_w_out[0])], "adamw_ple_gate_out", carry)
    upd["w_ple_gate"] = tuple(natural(t) for t in res_pg)
    upd["w_out"] = tuple(natural(t) for t in res_out)
    in_parts = _pair_add([_by_device(d_w_in_t)], in_sib, blks, "pair_add_in")
    carry = _Carry()
    _carry_chip(carry, in_parts)
    (res_gu, res_down), in_chips = _sum_adamw([
        (ffn_parts[0], ffn_chips[0], w_gate_up[0].T, m_w_gate_up[0].T, v_w_gate_up[0].T),
        (ffn_parts[1], ffn_chips[1], w_down[0], m_w_down[0], v_w_down[0])], "adamw_ffn", carry)
    upd["w_gate_up"] = tuple(turned(t) for t in res_gu)
    upd["w_down"] = tuple(natural(t) for t in res_down)
    (res_in,), _ = _sum_adamw([(in_parts[0], in_chips[0], w_in[0].T, m_w_in[0].T, v_w_in[0].T)], "adamw_w_in")
    upd["w_in"] = tuple(turned(t) for t in res_in)
    (gr_w_pu_t,) = _chip_sum(ple_parts[1:2], ple_chips[1:2], "chip_sum")
    plain = [
        ("w_ple_up", w_ple_up[0], gr_w_pu_t.T, m_w_ple_up[0], v_w_ple_up[0], False),
        ("conv_w", conv_w[0], gr_conv_w, m_conv_w[0], v_conv_w[0], False),
    ]
    for nm, w_, g_, m_, v_, transposed in plain:
        res = (g_,) + tuple(_adamw(w_, g_, m_, v_, "adamw_" + nm))
        upd[nm] = tuple((t.T if transposed else t)[None] for t in res)

    order = ["g_mix", "w_in", "conv_w", "conv_b", "ln_g", "ln_b", "pool_w", "pool_scale", "w_out", "g_ffn",
             "w_gate_up", "w_down", "g_ple_gate", "w_ple_gate", "w_ple_up", "g_ple_post", "g_final"]
    outs = [loss, dx[None]]
    for k in range(4):
        outs += [upd[nm][k] for nm in order]
    return tuple(outs)
```

```python
import functools

import jax
import jax.numpy as jnp
from jax import lax
from jax.experimental import pallas as pl
from jax.experimental.pallas import tpu as pltpu

D_MODEL = 1024
C_CONV = 512
C_POOL = 512
Z_WIDTH = 2 * C_CONV + C_POOL
POOL_WINDOWS = (2, 4, 8, 16)
POOL_GROUP = 128
CONV_K = 31
D_FF = 2816
D_PLE = 256
EPS = 1e-6
N_DEV = 8

ADAM_LR = 0.001
ADAM_B1 = 0.9
ADAM_B2 = 0.999
ADAM_EPS = 1e-08
ADAM_WD = 0.01
ADAM_STEP = 10

CONV_HALO = 32
POOL_HALO = 16
ROW_CHUNK = 32
V7X_VMEM_LIMIT = 56 * 1024 * 1024
FF_CHUNK = D_FF // 2

BF16 = jnp.bfloat16
F32 = jnp.float32
MESH = pl.DeviceIdType.MESH


def _dot(a, b):
    return lax.dot_general(a, b, (((1,), (0,)), ((), ())), preferred_element_type=F32)


def _dot_nt(a, b):
    return lax.dot_general(a, b, (((1,), (1,)), ((), ())), preferred_element_type=F32)


def _dot_tn(a, b):
    return lax.dot_general(a, b, (((0,), (0,)), ((), ())), preferred_element_type=F32)


def _rms_fwd(x, g):
    r = lax.rsqrt(jnp.mean(x * x, axis=-1, keepdims=True) + EPS)
    return x * r * g, r


def _rms_bwd(x, r, g, dy):
    xr = x * r
    dg = jnp.sum(dy * xr, axis=0, keepdims=True)
    dyg = dy * g
    dx = r * (dyg - xr * jnp.mean(dyg * xr, axis=-1, keepdims=True))
    return dx, dg


def _sigmoid(x):
    return jax.nn.sigmoid(x)


def _params(sem=None):
    return pltpu.CompilerParams(dimension_semantics=sem, vmem_limit_bytes=V7X_VMEM_LIMIT)


def _place():
    x, y, c = lax.axis_index("x"), lax.axis_index("y"), lax.axis_index("c")
    chips = [(1 - x, y), (x, 1 - y), (1 - x, 1 - y)]
    return (x, y, c), (x, y, 1 - c), chips


def _block(px, py, pc):
    return 4 * px + 2 * py + pc


class _Carry:
    def __init__(self):
        self.inputs = []
        self.out_shapes = []
        self.copies = []
        self.locals = []

    def add_input(self, arr):
        self.inputs.append(arr)
        return len(self.inputs) - 1

    def add_output(self, shape, dtype):
        self.out_shapes.append(jax.ShapeDtypeStruct(shape, dtype))
        return len(self.out_shapes) - 1

    def local(self, src_idx, dst_idx, dst_blk):
        self.locals.append((src_idx, dst_idx, dst_blk))

    def copy(self, src, dst_idx, dst_blk, got_blk, peer, step=0, after=()):
        self.copies.append(dict(src=src, dst_idx=dst_idx, dst_blk=dst_blk, got_blk=got_blk, peer=peer, step=step,
                                after=tuple(after)))
        return len(self.copies) - 1

    def sem_shapes(self):
        return [pltpu.SemaphoreType.DMA((max(1, len(self.copies)),)),
                pltpu.SemaphoreType.DMA((max(1, len(self.copies)),)),
                pltpu.SemaphoreType.DMA((max(1, len(self.locals)),))]

    @staticmethod
    def _view(ref, where):
        if isinstance(where, tuple):
            blk, row0, nrows = where
            return ref.at[blk, pl.ds(row0, nrows)]
        return ref.at[where]

    def _desc(self, k, ins, outs, sems, place):
        cp = self.copies[k]
        me, sib, chips = place
        kind, idx, blk = cp["src"]
        src = (ins if kind == "in" else outs)[idx]
        if blk is not None:
            src = self._view(src, blk(*place))
        to = sib if cp["peer"] == "sib" else (*chips[cp["peer"]], me[2])
        return pltpu.make_async_remote_copy(
            src_ref=src, dst_ref=self._view(outs[cp["dst_idx"]], cp["dst_blk"](*place)),
            send_sem=sems[0].at[k], recv_sem=sems[1].at[k], device_id=to, device_id_type=MESH)

    def _arrival(self, k, outs, sems, place):
        cp = self.copies[k]
        got = self._view(outs[cp["dst_idx"]], cp["got_blk"](*place))
        return pltpu.make_async_remote_copy(
            src_ref=got, dst_ref=got, send_sem=sems[0].at[k], recv_sem=sems[1].at[k],
            device_id=place[0], device_id_type=MESH)

    def _local(self, n, ins, outs, sems, place):
        src_idx, dst_idx, blk = self.locals[n]
        return pltpu.make_async_copy(ins[src_idx], outs[dst_idx].at[blk(*place)], sems[2].at[n])

    def starts(self, step, nsteps, ins, outs, sems):
        place = _place()
        self._waited = set()
        for s in sorted({0} | {cp["step"] for cp in self.copies}):
            ks = [k for k, cp in enumerate(self.copies) if cp["step"] == s]

            @pl.when(step == min(s, nsteps - 1))
            def _(s=s, ks=ks):
                if s == 0:
                    for n in range(len(self.locals)):
                        self._local(n, ins, outs, sems, place).start()
                for k in ks:
                    for a in self.copies[k]["after"]:
                        if a not in self._waited:
                            self._arrival(a, outs, sems, place).wait_recv()
                            self._waited.add(a)
                    self._desc(k, ins, outs, sems, place).start()

    def finish(self, step, nsteps, ins, outs, sems):
        place = _place()

        @pl.when(step == nsteps - 1)
        def _():
            for k in range(len(self.copies)):
                if k not in self._waited:
                    self._arrival(k, outs, sems, place).wait_recv()
            for k in range(len(self.copies)):
                self._desc(k, ins, outs, sems, place).wait_send()
            for n in range(len(self.locals)):
                self._local(n, ins, outs, sems, place).wait()


def _const_blk(j):
    return lambda me, sib, chips: j


def _carry_gather(carry, shards, relay_step, last_step):
    outs = []
    for sh in shards:
        i = carry.add_input(sh)
        o = carry.add_output((N_DEV,) + sh.shape, sh.dtype)
        half = sh.shape[0] // 2
        tile = 16 if sh.dtype == BF16 else 8
        split = half % tile == 0
        rows = [(0, half), (half, sh.shape[0] - half)] if split else [(0, sh.shape[0]), None]

        def whole(j, core):
            return lambda me, sib, chips, j=j, core=core: _block(*chips[j], me[2] if core == 0 else 1 - me[2])

        def part(j, core, h, rows=rows):
            return lambda me, sib, chips: (_block(*chips[j], me[2] if core == 0 else 1 - me[2]),) + rows[h]

        mine = lambda me, sib, chips: _block(*me)
        carry.local(i, o, mine)
        carry.copy(("in", i, None), o, mine, lambda me, sib, chips: _block(*sib), "sib")
        near = [carry.copy(("in", i, None), o, mine, whole(j, 0), j) for j in range(2)]
        for j in range(2):
            carry.copy(("out", o, whole(j, 0)), o, whole(j, 0), whole(j, 1), "sib", step=relay_step, after=(near[j],))
        for j in range(2):
            if rows[j] is None:
                continue
            far = carry.copy(("out", o, part(j, 0, j)), o, part(j, 0, j), part(2, 0, j), 1 - j,
                             step=relay_step, after=(near[j],))
            carry.copy(("out", o, part(2, 0, j)), o, part(2, 0, j), part(2, 1, j), "sib", step=last_step, after=(far,))
        outs.append(o)
    return outs


def _carry_pair(carry, grads):
    outs = []
    for g in grads:
        i = carry.add_input(g)
        o = carry.add_output((4,) + g.shape[1:], g.dtype)
        for j in range(4):
            if j < 3:
                blk = lambda me, sib, chips, j=j: _block(*chips[j], 1 - me[2])
            else:
                blk = lambda me, sib, chips: _block(*sib)
            carry.copy(("in", i, blk), o, _const_blk(j), _const_blk(j), "sib")
        outs.append(o)
    return outs


def _carry_chip(carry, parts):
    outs = []
    for p in parts:
        i = carry.add_input(p)
        o = carry.add_output((3,) + p.shape[1:], p.dtype)
        for j in range(3):
            carry.copy(("in", i, _const_blk(j)), o, _const_blk(j), _const_blk(j), j)
        outs.append(o)
    return outs


def _pcall(body, *, grid, in_specs, out_specs, out_shape, scratch_shapes, name, args, carry=None):
    sem = ("arbitrary",) * len(grid)
    if carry is None:
        res = pl.pallas_call(body, grid=grid, in_specs=in_specs, out_specs=out_specs, out_shape=out_shape,
                             scratch_shapes=scratch_shapes, compiler_params=_params(sem), name=name)(*args)
        return list(res), []
    n_in, n_out, n_scr = len(in_specs), len(out_specs), len(scratch_shapes)
    c_in, c_out = len(carry.inputs), len(carry.out_shapes)
    nsteps = 1
    for extent in grid:
        nsteps *= extent

    def wrapped(*refs):
        ins = refs[:n_in]
        cins = refs[n_in:n_in + c_in]
        o0 = n_in + c_in
        outs = refs[o0:o0 + n_out]
        couts = refs[o0 + n_out:o0 + n_out + c_out]
        s0 = o0 + n_out + c_out
        scr = refs[s0:s0 + n_scr]
        sems = refs[s0 + n_scr:]
        step = pl.program_id(0)
        for d in range(1, len(grid)):
            step = step * grid[d] + pl.program_id(d)
        carry.starts(step, nsteps, cins, couts, sems)
        body(*ins, *outs, *scr)
        carry.finish(step, nsteps, cins, couts, sems)

    any_spec = pl.BlockSpec(memory_space=pl.ANY)
    res = pl.pallas_call(
        wrapped, grid=grid,
        in_specs=list(in_specs) + [any_spec] * c_in,
        out_specs=list(out_specs) + [any_spec] * c_out,
        out_shape=list(out_shape) + carry.out_shapes,
        scratch_shapes=list(scratch_shapes) + carry.sem_shapes(),
        compiler_params=_params(sem), name=name)(*args, *carry.inputs)
    return list(res[:n_out]), list(res[n_out:])


def _shifted_copies(buf, shifted, tm):
    span = tm + CONV_HALO - 8
    for r in range(1, 8):
        shifted[r - 1, 0:span, :] = buf[r:r + span, :]


def _rows_at(buf, shifted, start):
    aligned, r = (start // 8) * 8, start % 8
    if r == 0:
        return buf[aligned:aligned + ROW_CHUNK, :]
    return shifted[r - 1, aligned:aligned + ROW_CHUNK, :]


def _row_spec(tm, width):
    return pl.BlockSpec((tm, width), lambda i: (i, 0))


def _const_spec(shape):
    return pl.BlockSpec(shape, lambda i: (0,) * len(shape))


def _weight_spec(shape):
    return pl.BlockSpec(shape, lambda i: (0,) * len(shape), pipeline_mode=pl.Buffered(1))


def _fwd_mix(x, w_in_t, w_out, g_mix, conv_w, conv_b, ln_g, ln_b, pool_w, pool_scale, tm, carry=None):
    s = x.shape[0]
    nt = s // tm

    def body(x_ref, win_ref, wout_ref, gmix_ref, cw_ref, cb_ref, lng_ref, lnb_ref, pw_ref, ps_ref,
             z_ref, u1_ref, pooled_ref, x1_ref, mix_ref, ubuf, vbuf, ush):
        i = pl.program_id(0)

        @pl.when(i == 0)
        def _():
            ubuf[0:CONV_HALO, :] = jnp.zeros((CONV_HALO, C_CONV), F32)
            vbuf[0:POOL_HALO, :] = jnp.zeros((POOL_HALO, C_POOL), F32)

        xt = x_ref[...]
        h, _ = _rms_fwd(xt, gmix_ref[...])
        z = _dot_nt(h.astype(BF16), win_ref[...])
        z_ref[...] = z
        a = z[:, :C_CONV]
        b = z[:, C_CONV:2 * C_CONV]
        v = z[:, 2 * C_CONV:]
        ubuf[CONV_HALO:CONV_HALO + tm, :] = a * _sigmoid(b)
        vbuf[POOL_HALO:POOL_HALO + tm, :] = v

        _shifted_copies(ubuf, ush, tm)
        for rc in range(tm // ROW_CHUNK):
            base = rc * ROW_CHUNK + CONV_HALO - (CONV_K - 1)
            acc = jnp.broadcast_to(cb_ref[...], (ROW_CHUNK, C_CONV))
            for k in range(CONV_K):
                acc = acc + cw_ref[k:k + 1, :] * _rows_at(ubuf, ush, base + k)
            u1_ref[rc * ROW_CHUNK:(rc + 1) * ROW_CHUNK, :] = acc

        u1 = u1_ref[...]
        mu = jnp.mean(u1, axis=-1, keepdims=True)
        cen = u1 - mu
        rstd = lax.rsqrt(jnp.mean(cen * cen, axis=-1, keepdims=True) + EPS)
        u2 = cen * rstd * lng_ref[...] + lnb_ref[...]
        u = u2 * _sigmoid(u2)

        pos1 = (i * tm + lax.broadcasted_iota(jnp.int32, (tm, 1), 0) + 1).astype(F32)
        parts = [u]
        for g, w in enumerate(POOL_WINDOWS):
            cols = slice(g * POOL_GROUP, (g + 1) * POOL_GROUP)
            vg = v[:, cols]
            tot = vg
            for j in range(1, w):
                tot = tot + vbuf[POOL_HALO - j:POOL_HALO - j + tm, cols]
            pooled = tot / jnp.minimum(pos1, float(w)) - vg
            pooled_b = pooled.astype(BF16)
            pooled_ref[:, cols] = pooled_b
            parts.append(_dot(pooled_b, pw_ref[g]) * ps_ref[:, cols])
        mix = jnp.concatenate(parts, axis=-1).astype(BF16)
        mix_ref[...] = mix
        x1_ref[...] = xt + _dot(mix, wout_ref[...])

        ubuf[0:CONV_HALO, :] = ubuf[tm:tm + CONV_HALO, :]
        vbuf[0:POOL_HALO, :] = vbuf[tm:tm + POOL_HALO, :]

    return _pcall(
        body,
        grid=(nt,),
        in_specs=[
            _row_spec(tm, D_MODEL),
            _const_spec((Z_WIDTH, D_MODEL)),
            _const_spec((D_MODEL, D_MODEL)),
            _const_spec((1, D_MODEL)),
            _const_spec((CONV_HALO, C_CONV)),
            _const_spec((1, C_CONV)),
            _const_spec((1, C_CONV)),
            _const_spec((1, C_CONV)),
            _const_spec((len(POOL_WINDOWS), POOL_GROUP, POOL_GROUP)),
            _const_spec((1, C_POOL)),
        ],
        out_specs=[
            _row_spec(tm, Z_WIDTH),
            _row_spec(tm, C_CONV),
            _row_spec(tm, C_POOL),
            _row_spec(tm, D_MODEL),
            _row_spec(tm, D_MODEL),
        ],
        out_shape=[
            jax.ShapeDtypeStruct((s, Z_WIDTH), F32),
            jax.ShapeDtypeStruct((s, C_CONV), F32),
            jax.ShapeDtypeStruct((s, C_POOL), BF16),
            jax.ShapeDtypeStruct((s, D_MODEL), F32),
            jax.ShapeDtypeStruct((s, D_MODEL), BF16),
        ],
        scratch_shapes=[
            pltpu.VMEM((tm + CONV_HALO, C_CONV), F32),
            pltpu.VMEM((tm + POOL_HALO, C_POOL), F32),
            pltpu.VMEM((7, tm + CONV_HALO, C_CONV), F32),
        ],
        name="fwd_mix",
        args=(x, w_in_t, w_out, g_mix, conv_w, conv_b, ln_g, ln_b, pool_w, pool_scale),
        carry=carry,
    )


def _ffn_up(x1, w_gu_t, g_ffn, tm, carry=None):
    s = x1.shape[0]

    def body(x1_ref, w_ref, gffn_ref, h2_ref, g_ref, u_ref, act_ref):
        h, _ = _rms_fwd(x1_ref[...], gffn_ref[...])
        h2 = h.astype(BF16)
        h2_ref[...] = h2
        for c in range(D_FF // FF_CHUNK):
            cols = slice(c * FF_CHUNK, (c + 1) * FF_CHUNK)
            g = _dot_nt(h2, w_ref[c * FF_CHUNK:(c + 1) * FF_CHUNK, :])
            u = _dot_nt(h2, w_ref[D_FF + c * FF_CHUNK:D_FF + (c + 1) * FF_CHUNK, :])
            g_ref[:, cols] = g.astype(BF16)
            u_ref[:, cols] = u.astype(BF16)
            act_ref[:, cols] = (g * _sigmoid(g) * u).astype(BF16)

    return _pcall(
        body,
        grid=(s // tm,),
        in_specs=[_row_spec(tm, D_MODEL), _weight_spec((2 * D_FF, D_MODEL)), _const_spec((1, D_MODEL))],
        out_specs=[_row_spec(tm, D_MODEL), _row_spec(tm, D_FF), _row_spec(tm, D_FF), _row_spec(tm, D_FF)],
        out_shape=[
            jax.ShapeDtypeStruct((s, D_MODEL), BF16),
            jax.ShapeDtypeStruct((s, D_FF), BF16),
            jax.ShapeDtypeStruct((s, D_FF), BF16),
            jax.ShapeDtypeStruct((s, D_FF), BF16),
        ],
        scratch_shapes=[],
        name="ffn_up",
        args=(x1, w_gu_t, g_ffn),
        carry=carry,
    )


def _down_ple(x1, act, w_down, p, tgt, w_pg, w_pu_t, g_gate, g_post, g_final, tm):
    s = x1.shape[0]
    nt = s // tm

    def body(x1_ref, act_ref, wd_ref, p_ref, t_ref, wpg_ref, wpu_ref, gg_ref, gp_ref, gf_ref,
             dx2_ref, dx2b_ref, hg_ref, ds_ref, dpe_ref, pb_ref, stats_ref, x2_cur, x2_next):
        i = pl.program_id(0)

        @pl.when(i == 0)
        def _():
            stats_ref[...] = jnp.zeros_like(stats_ref)
            x2_cur[...] = jnp.zeros((tm, D_MODEL), F32)

        def down(c):
            cols = slice(c * 256, (c + 1) * 256)
            x2_next[:, cols] = x1_ref[:, cols] + _dot(act_ref[...], wd_ref[:, cols])

        x2 = x2_cur[...]
        counts = i >= 1

        hg, rg = _rms_fwd(x2, gg_ref[...])
        hg_b = hg.astype(BF16)
        hg_ref[...] = hg_b
        down(0)
        gate = _sigmoid(_dot(hg_b, wpg_ref[...]))
        pb = p_ref[...].astype(BF16)
        pb_ref[...] = pb
        pe = _dot_nt(pb, wpu_ref[...])
        e, rp = _rms_fwd(pe, gp_ref[...])
        x3 = x2 + gate * e
        down(1)
        y, r3 = _rms_fwd(x3, gf_ref[...])
        diff = y - t_ref[...]
        loss = 0.5 * jnp.sum(jnp.sum(diff * diff, axis=-1, keepdims=True), axis=0, keepdims=True) / D_MODEL
        dy = diff * (1.0 / D_MODEL)

        dx3, dgf = _rms_bwd(x3, r3, gf_ref[...], dy)
        down(2)
        dpe, dgp = _rms_bwd(pe, rp, gp_ref[...], dx3 * gate)
        dpe_ref[...] = dpe.astype(BF16)
        ds = (dx3 * e * gate * (1.0 - gate)).astype(BF16)
        ds_ref[...] = ds
        dhg = _dot_nt(ds, wpg_ref[...])
        down(3)
        dxg, dgg = _rms_bwd(x2, rg, gg_ref[...], dhg)
        dx2 = dx3 + dxg
        dx2_ref[...] = dx2
        dx2b_ref[...] = dx2.astype(BF16)
        x2_cur[...] = x2_next[...]

        stats_ref[0:1, :] += jnp.where(counts, dgf, 0.0)
        stats_ref[1:2, :] += jnp.where(counts, dgp, 0.0)
        stats_ref[2:3, :] += jnp.where(counts, dgg, 0.0)
        stats_ref[3:4, :] += jnp.where(counts, jnp.broadcast_to(loss, (1, D_MODEL)), 0.0)

    ahead = lambda width: pl.BlockSpec((tm, width), lambda i: (jnp.minimum(i, nt - 1), 0))
    behind = lambda width: pl.BlockSpec((tm, width), lambda i: (jnp.maximum(i - 1, 0), 0))
    return pl.pallas_call(
        body,
        grid=(nt + 1,),
        in_specs=[
            ahead(D_MODEL),
            ahead(D_FF),
            _weight_spec((D_FF, D_MODEL)),
            behind(D_PLE),
            behind(D_MODEL),
            _weight_spec((D_MODEL, D_MODEL)),
            _weight_spec((D_MODEL, D_PLE)),
            _const_spec((1, D_MODEL)),
            _const_spec((1, D_MODEL)),
            _const_spec((1, D_MODEL)),
        ],
        out_specs=[
            behind(D_MODEL),
            behind(D_MODEL),
            behind(D_MODEL),
            behind(D_MODEL),
            behind(D_MODEL),
            behind(D_PLE),
            _const_spec((8, D_MODEL)),
        ],
        out_shape=[
            jax.ShapeDtypeStruct((s, D_MODEL), F32),
            jax.ShapeDtypeStruct((s, D_MODEL), BF16),
            jax.ShapeDtypeStruct((s, D_MODEL), BF16),
            jax.ShapeDtypeStruct((s, D_MODEL), BF16),
            jax.ShapeDtypeStruct((s, D_MODEL), BF16),
            jax.ShapeDtypeStruct((s, D_PLE), BF16),
            jax.ShapeDtypeStruct((8, D_MODEL), F32),
        ],
        scratch_shapes=[pltpu.VMEM((tm, D_MODEL), F32), pltpu.VMEM((tm, D_MODEL), F32)],
        compiler_params=_params(("arbitrary",)),
        name="down_ple",
    )(x1, act, w_down, p, tgt, w_pg, w_pu_t, g_gate, g_post, g_final)


def _ffn_bwd(dx2, dx2b, x1, g_sav, u_sav, w_gu_t, w_down, g_ffn, tm, carry=None):
    s = x1.shape[0]

    def body(dx2_ref, dx2b_ref, x1_ref, g_ref, u_ref, w_ref, wd_ref, gffn_ref,
             dg_ref, du_ref, dx1_ref, dx1b_ref, stats_ref):
        @pl.when(pl.program_id(0) == 0)
        def _():
            stats_ref[...] = jnp.zeros_like(stats_ref)

        dx2b = dx2b_ref[...]
        dh2 = jnp.zeros((tm, D_MODEL), F32)
        for c in range(D_FF // FF_CHUNK):
            cols = slice(c * FF_CHUNK, (c + 1) * FF_CHUNK)
            dact = _dot_nt(dx2b, wd_ref[c * FF_CHUNK:(c + 1) * FF_CHUNK, :])
            g = g_ref[:, cols].astype(F32)
            u = u_ref[:, cols].astype(F32)
            sg = _sigmoid(g)
            dg = (dact * u * sg * (1.0 + g * (1.0 - sg))).astype(BF16)
            du = (dact * g * sg).astype(BF16)
            dg_ref[:, cols] = dg
            du_ref[:, cols] = du
            dh2 = dh2 + _dot(dg, w_ref[c * FF_CHUNK:(c + 1) * FF_CHUNK, :])
            dh2 = dh2 + _dot(du, w_ref[D_FF + c * FF_CHUNK:D_FF + (c + 1) * FF_CHUNK, :])

        x1 = x1_ref[...]
        r2 = lax.rsqrt(jnp.mean(x1 * x1, axis=-1, keepdims=True) + EPS)
        dxn, dgf = _rms_bwd(x1, r2, gffn_ref[...], dh2)
        dx1 = dx2_ref[...] + dxn
        dx1_ref[...] = dx1
        dx1b_ref[...] = dx1.astype(BF16)
        stats_ref[0:1, :] += dgf

    return _pcall(
        body,
        grid=(s // tm,),
        in_specs=[
            _row_spec(tm, D_MODEL), _row_spec(tm, D_MODEL), _row_spec(tm, D_MODEL),
            _row_spec(tm, D_FF), _row_spec(tm, D_FF),
            _weight_spec((2 * D_FF, D_MODEL)), _weight_spec((D_FF, D_MODEL)), _const_spec((1, D_MODEL)),
        ],
        out_specs=[_row_spec(tm, D_FF), _row_spec(tm, D_FF), _row_spec(tm, D_MODEL), _row_spec(tm, D_MODEL),
                   _const_spec((8, D_MODEL))],
        out_shape=[
            jax.ShapeDtypeStruct((s, D_FF), BF16),
            jax.ShapeDtypeStruct((s, D_FF), BF16),
            jax.ShapeDtypeStruct((s, D_MODEL), F32),
            jax.ShapeDtypeStruct((s, D_MODEL), BF16),
            jax.ShapeDtypeStruct((8, D_MODEL), F32),
        ],
        scratch_shapes=[],
        name="ffn_bwd",
        args=(dx2, dx2b, x1, g_sav, u_sav, w_gu_t, w_down, g_ffn),
        carry=carry,
    )


def _bwd_mix(dx1, dx1b, x, z, u1, pooled, w_in_t, w_out, g_mix, conv_w, ln_g, ln_b, pool_w, pool_scale, tm,
             carry=None):
    s = x.shape[0]
    nt = s // tm

    def body(dx1_ref, dx1b_ref, x_ref, z_ref, u1_ref, pooled_ref, win_ref, wout_ref, gmix_ref, cw_ref,
             lng_ref, lnb_ref, pw_ref, ps_ref,
             dx_ref, dz_ref, h1_ref, vec_ref, dcw_ref, dpw_ref, dubuf, dvbuf, u0buf, du0buf, dush):
        i = pl.program_id(0)
        tile = nt - 1 - i

        @pl.when(i == 0)
        def _():
            vec_ref[...] = jnp.zeros_like(vec_ref)
            dcw_ref[...] = jnp.zeros_like(dcw_ref)
            dpw_ref[...] = jnp.zeros_like(dpw_ref)
            dubuf[tm:tm + CONV_HALO, :] = jnp.zeros((CONV_HALO, C_CONV), F32)
            dvbuf[tm:tm + POOL_HALO, :] = jnp.zeros((POOL_HALO, C_POOL), F32)

        dmix = _dot_nt(dx1b_ref[...], wout_ref[...])
        du = dmix[:, :C_CONV]
        dq = dmix[:, C_CONV:]

        pos1 = (tile * tm + lax.broadcasted_iota(jnp.int32, (tm, 1), 0) + 1).astype(F32)
        dpooled_parts = []
        dps_rows = []
        for g, w in enumerate(POOL_WINDOWS):
            cols = slice(g * POOL_GROUP, (g + 1) * POOL_GROUP)
            pooled_b = pooled_ref[:, cols]
            mixed = _dot(pooled_b, pw_ref[g])
            dqg = dq[:, cols]
            dps_rows.append(jnp.sum(dqg * mixed, axis=0, keepdims=True))
            dmixed = (dqg * ps_ref[:, cols]).astype(BF16)
            dpw_ref[g] += _dot_tn(pooled_b, dmixed)
            dpooled = _dot_nt(dmixed, pw_ref[g])
            dpooled_parts.append(dpooled)
            dvbuf[0:tm, cols] = dpooled / jnp.minimum(pos1, float(w))
        vec_ref[4:5, 0:C_POOL] += jnp.concatenate(dps_rows, axis=-1)
        dv_parts = []
        for g, w in enumerate(POOL_WINDOWS):
            cols = slice(g * POOL_GROUP, (g + 1) * POOL_GROUP)
            tot = dvbuf[0:tm, cols]
            for j in range(1, w):
                tot = tot + dvbuf[j:j + tm, cols]
            dv_parts.append(tot - dpooled_parts[g])

        u1 = u1_ref[...]
        mu = jnp.mean(u1, axis=-1, keepdims=True)
        cen = u1 - mu
        rstd = lax.rsqrt(jnp.mean(cen * cen, axis=-1, keepdims=True) + EPS)
        xhat = cen * rstd
        u2 = xhat * lng_ref[...] + lnb_ref[...]
        sg2 = _sigmoid(u2)
        du2 = du * sg2 * (1.0 + u2 * (1.0 - sg2))
        vec_ref[1:2, 0:C_CONV] += jnp.sum(du2 * xhat, axis=0, keepdims=True)
        vec_ref[2:3, 0:C_CONV] += jnp.sum(du2, axis=0, keepdims=True)
        t1 = du2 * lng_ref[...]
        du1 = rstd * (t1 - jnp.mean(t1, axis=-1, keepdims=True)
                      - xhat * jnp.mean(t1 * xhat, axis=-1, keepdims=True))
        vec_ref[3:4, 0:C_CONV] += jnp.sum(du1, axis=0, keepdims=True)
        dubuf[0:tm, :] = du1

        zt = z_ref[...]
        a = zt[:, :C_CONV]
        sgb = _sigmoid(zt[:, C_CONV:2 * C_CONV])
        u0buf[...] = a * sgb

        _shifted_copies(dubuf, dush, tm)
        for rc in range(tm // ROW_CHUNK):
            r0 = rc * ROW_CHUNK
            acc = jnp.zeros((ROW_CHUNK, C_CONV), F32)
            for k in range(CONV_K):
                acc = acc + cw_ref[k:k + 1, :] * _rows_at(dubuf, dush, r0 + (CONV_K - 1) - k)
            du0buf[r0:r0 + ROW_CHUNK, :] = acc
        for k in range(CONV_K):
            acc = jnp.zeros((ROW_CHUNK, C_CONV), F32)
            for rc in range(tm // ROW_CHUNK):
                r0 = rc * ROW_CHUNK
                acc = acc + u0buf[r0:r0 + ROW_CHUNK, :] * _rows_at(dubuf, dush, r0 + (CONV_K - 1) - k)
            dcw_ref[k:k + 1, :] += jnp.sum(acc, axis=0, keepdims=True)
        du0 = du0buf[...]

        da = du0 * sgb
        db = du0 * a * sgb * (1.0 - sgb)
        dz = jnp.concatenate([da, db] + dv_parts, axis=-1).astype(BF16)
        dz_ref[...] = dz

        xt = x_ref[...]
        h1, r1 = _rms_fwd(xt, gmix_ref[...])
        h1_ref[...] = h1.astype(BF16)
        dh1 = _dot(dz, win_ref[...])
        dxn, dgm = _rms_bwd(xt, r1, gmix_ref[...], dh1)
        dx_ref[...] = dx1_ref[...] + dxn
        vec_ref[0:1, :] += dgm

        dubuf[tm:tm + CONV_HALO, :] = dubuf[0:CONV_HALO, :]
        dvbuf[tm:tm + POOL_HALO, :] = dvbuf[0:POOL_HALO, :]

    rev = lambda width: pl.BlockSpec((tm, width), lambda i: (nt - 1 - i, 0))
    return _pcall(
        body,
        grid=(nt,),
        in_specs=[
            rev(D_MODEL), rev(D_MODEL), rev(D_MODEL), rev(Z_WIDTH), rev(C_CONV), rev(C_POOL),
            _const_spec((Z_WIDTH, D_MODEL)),
            _const_spec((D_MODEL, D_MODEL)),
            _const_spec((1, D_MODEL)),
            _const_spec((CONV_HALO, C_CONV)),
            _const_spec((1, C_CONV)),
            _const_spec((1, C_CONV)),
            _const_spec((len(POOL_WINDOWS), POOL_GROUP, POOL_GROUP)),
            _const_spec((1, C_POOL)),
        ],
        out_specs=[
            rev(D_MODEL), rev(Z_WIDTH), rev(D_MODEL),
            _const_spec((8, D_MODEL)),
            _const_spec((CONV_HALO, C_CONV)),
            _const_spec((len(POOL_WINDOWS), POOL_GROUP, POOL_GROUP)),
        ],
        out_shape=[
            jax.ShapeDtypeStruct((s, D_MODEL), F32),
            jax.ShapeDtypeStruct((s, Z_WIDTH), BF16),
            jax.ShapeDtypeStruct((s, D_MODEL), BF16),
            jax.ShapeDtypeStruct((8, D_MODEL), F32),
            jax.ShapeDtypeStruct((CONV_HALO, C_CONV), F32),
            jax.ShapeDtypeStruct((len(POOL_WINDOWS), POOL_GROUP, POOL_GROUP), F32),
        ],
        scratch_shapes=[
            pltpu.VMEM((tm + CONV_HALO, C_CONV), F32),
            pltpu.VMEM((tm + POOL_HALO, C_POOL), F32),
            pltpu.VMEM((tm, C_CONV), F32),
            pltpu.VMEM((tm, C_CONV), F32),
            pltpu.VMEM((7, tm + CONV_HALO, C_CONV), F32),
        ],
        name="bwd_mix",
        args=(dx1, dx1b, x, z, u1, pooled, w_in_t, w_out, g_mix, conv_w, ln_g, ln_b, pool_w, pool_scale),
        carry=carry,
    )


def _grad_matmul(a, b, bm, name, a2=None, carry=None):
    s, ma = a.shape
    nb = b.shape[1]
    na = ma // bm
    if a2 is None:
        def body(a_ref, b_ref, o_ref):
            o_ref[...] = _dot_tn(a_ref[...], b_ref[...]).astype(BF16)

        lhs_specs = [pl.BlockSpec((s, bm), lambda i: (0, i))]
        lhs = (a,)
        steps = na
    else:
        def body(a_ref, a2_ref, b_ref, o_ref):
            i = pl.program_id(0)

            @pl.when(i < na)
            def _():
                o_ref[...] = _dot_tn(a_ref[...], b_ref[...]).astype(BF16)

            @pl.when(i >= na)
            def _():
                o_ref[...] = _dot_tn(a2_ref[...], b_ref[...]).astype(BF16)

        lhs_specs = [pl.BlockSpec((s, bm), lambda i: (0, jnp.minimum(i, na - 1))),
                     pl.BlockSpec((s, bm), lambda i: (0, jnp.maximum(i - na, 0)))]
        lhs = (a, a2)
        steps = 2 * na

    outs, carried = _pcall(
        body,
        grid=(steps,),
        in_specs=lhs_specs + [pl.BlockSpec((s, nb), lambda i: (0, 0))],
        out_specs=[pl.BlockSpec((bm, nb), lambda i: (i, 0))],
        out_shape=[jax.ShapeDtypeStruct((steps * bm, nb), BF16)],
        scratch_shapes=[],
        name=name,
        args=lhs + (b,),
        carry=carry,
    )
    return outs[0], carried


def _gather_steps(src, dst, send_sems, recv_sems, local_sems):
    n = len(src)
    me, sib, chips = _place()
    c = me[2]
    started = []
    mine = []

    def copy(a, k, block, to, from_src=False):
        rows = dst[a].at[_block(*block)]
        return pltpu.make_async_remote_copy(
            src_ref=src[a] if from_src else rows, dst_ref=rows,
            send_sem=send_sems.at[a, k], recv_sem=recv_sems.at[a, k],
            device_id=to, device_id_type=MESH)

    def start():
        for a in range(n):
            cp = pltpu.make_async_copy(src[a], dst[a].at[_block(*me)], local_sems.at[a])
            cp.start()
            mine.append(cp)
            first = [copy(a, 0, me, sib, True)]
            first += [copy(a, 1 + j, me, (*chip, c), True) for j, chip in enumerate(chips)]
            for cp in first:
                cp.start()
            started.extend(first)

    def forward():
        for a in range(n):
            for j, chip in enumerate(chips):
                copy(a, 1 + j, (*chip, c), me).wait_recv()
                fwd = copy(a, 4 + j, (*chip, c), sib)
                fwd.start()
                started.append(fwd)

    def finish():
        for a in range(n):
            copy(a, 0, sib, me).wait_recv()
            for j, chip in enumerate(chips):
                copy(a, 4 + j, (*chip, 1 - c), me).wait_recv()
        for cp in started:
            cp.wait_send()
        for cp in mine:
            cp.wait()

    return start, forward, finish


def _gather_sems(n):
    return [pltpu.SemaphoreType.DMA((n, 7)), pltpu.SemaphoreType.DMA((n, 7)), pltpu.SemaphoreType.DMA((n,))]


def _all_gather(shards, name):
    n = len(shards)
    carry = _Carry()
    _carry_gather(carry, shards, 1, 2)

    def body(*refs):
        step = jnp.int32(0)
        carry.starts(step, 1, refs[:n], refs[n:2 * n], refs[2 * n:])
        carry.finish(step, 1, refs[:n], refs[n:2 * n], refs[2 * n:])

    any_spec = pl.BlockSpec(memory_space=pl.ANY)
    return pl.pallas_call(
        body,
        in_specs=[any_spec] * n,
        out_specs=[any_spec] * n,
        out_shape=carry.out_shapes,
        scratch_shapes=carry.sem_shapes(),
        name=name,
    )(*shards)


def _reduce_scatter(grads, small, name):
    n, ns = len(grads), len(small)
    shapes = [g.shape[1:] for g in grads]

    def body(*refs):
        g = refs[:n]
        out = refs[n + ns:2 * n + ns]
        scr = refs[2 * (n + ns):]
        own, loc, r1, r2 = scr[:n], scr[n:2 * n], scr[2 * n:3 * n], scr[3 * n:4 * n]
        load_sems, s1, q1, s2, q2 = scr[4 * n:4 * n + 5]
        gather_start = gather_forward = gather_finish = lambda: None
        if ns:
            gather_start, gather_forward, gather_finish = _gather_steps(
                refs[n:n + ns], refs[2 * n + ns:2 * (n + ns)], *scr[4 * n + 5:])
        me, sib, chips = _place()
        c = me[2]

        gather_start()
        loads = []
        sends = []
        for a in range(n):
            ld = [pltpu.make_async_copy(g[a].at[_block(*chip, c)], loc[a].at[j], load_sems.at[a, j])
                  for j, chip in enumerate(chips)]
            ld.append(pltpu.make_async_copy(g[a].at[_block(*me)], own[a], load_sems.at[a, 3]))
            for cp in ld:
                cp.start()
            loads.append(ld)
            blocks = [(*chip, 1 - c) for chip in chips] + [sib]
            for j, blk in enumerate(blocks):
                cp = pltpu.make_async_remote_copy(
                    src_ref=g[a].at[_block(*blk)], dst_ref=r1[a].at[j],
                    send_sem=s1.at[a, j], recv_sem=q1.at[a, j], device_id=sib, device_id_type=MESH)
                cp.start()
                sends.append(cp)

        def from_sibling(a, j):
            return pltpu.make_async_remote_copy(
                src_ref=r1[a].at[j], dst_ref=r1[a].at[j], send_sem=s1.at[a, j], recv_sem=q1.at[a, j],
                device_id=sib, device_id_type=MESH)

        def partial(a, j, chip):
            return pltpu.make_async_remote_copy(
                src_ref=loc[a].at[j], dst_ref=r2[a].at[j], send_sem=s2.at[a, j], recv_sem=q2.at[a, j],
                device_id=(*chip, c), device_id_type=MESH)

        gather_forward()
        for a in range(n):
            for j, chip in enumerate(chips):
                loads[a][j].wait()
                from_sibling(a, j).wait_recv()
                loc[a][j] = (loc[a][j].astype(F32) + r1[a][j].astype(F32)).astype(BF16)
                cp = partial(a, j, chip)
                cp.start()
                sends.append(cp)
        for a in range(n):
            loads[a][3].wait()
            from_sibling(a, 3).wait_recv()
            acc = own[a][...].astype(F32) + r1[a][3].astype(F32)
            for j, chip in enumerate(chips):
                partial(a, j, chip).wait_recv()
                acc = acc + r2[a][j].astype(F32)
            out[a][...] = acc
        for cp in sends:
            cp.wait_send()
        gather_finish()

    any_spec = pl.BlockSpec(memory_space=pl.ANY)
    vmem_spec = pl.BlockSpec(memory_space=pltpu.VMEM)
    res = pl.pallas_call(
        body,
        in_specs=[any_spec] * (n + ns),
        out_specs=[vmem_spec] * n + [any_spec] * ns,
        out_shape=([jax.ShapeDtypeStruct(sh, F32) for sh in shapes]
                   + [jax.ShapeDtypeStruct((N_DEV,) + sm.shape, sm.dtype) for sm in small]),
        scratch_shapes=(
            [pltpu.VMEM(sh, BF16) for sh in shapes]
            + [pltpu.VMEM((3,) + sh, BF16) for sh in shapes]
            + [pltpu.VMEM((4,) + sh, BF16) for sh in shapes]
            + [pltpu.VMEM((3,) + sh, BF16) for sh in shapes]
            + [pltpu.SemaphoreType.DMA((n, 4)),
               pltpu.SemaphoreType.DMA((n, 4)), pltpu.SemaphoreType.DMA((n, 4)),
               pltpu.SemaphoreType.DMA((n, 3)), pltpu.SemaphoreType.DMA((n, 3))]
            + (_gather_sems(ns) if ns else [])
        ),
        compiler_params=pltpu.CompilerParams(vmem_limit_bytes=V7X_VMEM_LIMIT),
        name=name,
    )(*grads, *small)
    return res[:n], res[n:]


def _pair_add(grads, from_sib, blks, name):
    n = len(grads)

    def body(blk_ref, *refs):
        for a in range(n):
            refs[2 * n + a][...] = (refs[a][...].astype(F32) + refs[n + a][...].astype(F32)).astype(BF16)

    mine = [pl.BlockSpec((None,) + g.shape[1:], lambda j, b: (b[j], 0, 0)) for g in grads]
    same = [pl.BlockSpec((None,) + g.shape[1:], lambda j, b: (j, 0, 0)) for g in grads]
    return pl.pallas_call(
        body,
        grid_spec=pltpu.PrefetchScalarGridSpec(
            num_scalar_prefetch=1, grid=(4,), in_specs=mine + same, out_specs=same),
        out_shape=[jax.ShapeDtypeStruct((4,) + g.shape[1:], BF16) for g in grads],
        compiler_params=_params(("arbitrary",)),
        name=name,
    )(blks, *grads, *from_sib)


def _chip_sum(parts, from_chips, name):
    n = len(parts)

    def body(*refs):
        for a in range(n):
            acc = refs[a][...].astype(F32)
            for j in range(3):
                acc = acc + refs[n + a][j].astype(F32)
            refs[2 * n + a][...] = acc

    half = [p.shape[1] // 2 for p in parts]
    return pl.pallas_call(
        body,
        grid=(2,),
        in_specs=([pl.BlockSpec((None, h, p.shape[2]), lambda i: (3, i, 0)) for p, h in zip(parts, half)]
                  + [pl.BlockSpec((3, h, p.shape[2]), lambda i: (0, i, 0)) for p, h in zip(parts, half)]),
        out_specs=[pl.BlockSpec((h, p.shape[2]), lambda i: (i, 0)) for p, h in zip(parts, half)],
        out_shape=[jax.ShapeDtypeStruct(p.shape[1:], F32) for p in parts],
        compiler_params=_params(("arbitrary",)),
        name=name,
    )(*parts, *from_chips)


def _adam_math(w, g, m, v):
    nm = ADAM_B1 * m + (1.0 - ADAM_B1) * g
    nv = ADAM_B2 * v + (1.0 - ADAM_B2) * (g * g)
    m_hat = nm / (1.0 - ADAM_B1 ** ADAM_STEP)
    v_hat = nv / (1.0 - ADAM_B2 ** ADAM_STEP)
    return -ADAM_LR * (m_hat / (jnp.sqrt(v_hat) + ADAM_EPS) + ADAM_WD * w), nm, nv


def _sum_adamw(items, name, carry=None):
    n = len(items)

    def body(*refs):
        for a in range(n):
            p_ref, f_ref, w_ref, m_ref, v_ref = refs[5 * a:5 * a + 5]
            g_ref, d_ref, nm_ref, nv_ref = refs[5 * n + 4 * a:5 * n + 4 * a + 4]
            g = p_ref[...].astype(F32)
            for j in range(3):
                g = g + f_ref[j].astype(F32)
            g_ref[...] = g
            d_ref[...], nm_ref[...], nv_ref[...] = _adam_math(w_ref[...], g, m_ref[...], v_ref[...])

    in_specs, out_specs, out_shape, args = [], [], [], []
    for part, from_chips, w, m, v in items:
        r, c = w.shape
        spec = pl.BlockSpec((r // 2, c), lambda i: (i, 0))
        in_specs += [pl.BlockSpec((None, r // 2, c), lambda i: (3, i, 0)),
                     pl.BlockSpec((3, r // 2, c), lambda i: (0, i, 0)), spec, spec, spec]
        out_specs += [spec] * 4
        out_shape += [jax.ShapeDtypeStruct((r, c), F32)] * 4
        args += [part, from_chips, w, m, v]
    outs, carried = _pcall(body, grid=(2,), in_specs=in_specs, out_specs=out_specs, out_shape=out_shape,
                           scratch_shapes=[], name=name, args=tuple(args), carry=carry)
    return [tuple(outs[4 * a:4 * a + 4]) for a in range(n)], carried


def _small_update(gathered, layout, params, name):
    ng, npar = len(gathered), len(params)

    def body(*refs):
        parts = refs[:ng]
        prm = refs[ng:ng + 3 * npar]
        tot_refs = refs[ng + 3 * npar:2 * ng + 3 * npar]
        out = refs[2 * ng + 3 * npar:]
        tots = []
        for a in range(ng):
            acc = parts[a][0]
            for d in range(1, N_DEV):
                acc = acc + parts[a][d]
            tot_refs[a][...] = acc
            tots.append(acc)
        for i, (a, row, width) in enumerate(layout):
            g = tots[a] if row is None else tots[a][row:row + 1, :width]
            delta, nm, nv = _adam_math(prm[3 * i][...], g, prm[3 * i + 1][...], prm[3 * i + 2][...])
            out[4 * i][...] = g
            out[4 * i + 1][...] = delta
            out[4 * i + 2][...] = nm
            out[4 * i + 3][...] = nv

    flat = [t for prm in params for t in prm]
    res = pl.pallas_call(
        body,
        out_shape=([jax.ShapeDtypeStruct(g.shape[1:], F32) for g in gathered]
                   + [jax.ShapeDtypeStruct(prm[0].shape, F32) for prm in params for _ in range(4)]),
        compiler_params=pltpu.CompilerParams(vmem_limit_bytes=V7X_VMEM_LIMIT),
        name=name,
    )(*gathered, *flat)
    return res[:ng], [tuple(res[ng + 4 * i:ng + 4 * i + 4]) for i in range(npar)]


def _adamw(w, g, m, v, name):
    rows, cols = w.shape
    br = rows
    for cand in (512, 256, 128):
        if rows % cand == 0 and rows > cand:
            br = cand
            break

    def body(w_ref, g_ref, m_ref, v_ref, d_ref, nm_ref, nv_ref):
        d_ref[...], nm_ref[...], nv_ref[...] = _adam_math(w_ref[...], g_ref[...], m_ref[...], v_ref[...])

    spec = pl.BlockSpec((br, cols), lambda i: (i, 0))
    shape = jax.ShapeDtypeStruct((rows, cols), F32)
    return pl.pallas_call(
        body,
        grid=(rows // br,),
        in_specs=[spec] * 4,
        out_specs=[spec] * 3,
        out_shape=[shape] * 3,
        compiler_params=_params(("arbitrary",)),
        name=name,
    )(w, g, m, v)


def _by_device(full):
    return full.reshape(N_DEV, full.shape[0] // N_DEV, full.shape[1])


def kernel(x, p, g_mix, w_in, conv_w, conv_b, ln_g, ln_b, pool_w, pool_scale, w_out, g_ffn, w_gate_up, w_down, g_ple_gate, w_ple_gate, w_ple_up, g_ple_post, g_final, loss_target, m_g_mix, m_w_in, m_conv_w, m_conv_b, m_ln_g, m_ln_b, m_pool_w, m_pool_scale, m_w_out, m_g_ffn, m_w_gate_up, m_w_down, m_g_ple_gate, m_w_ple_gate, m_w_ple_up, m_g_ple_post, m_g_final, v_g_mix, v_w_in, v_conv_w, v_conv_b, v_ln_g, v_ln_b, v_pool_w, v_pool_scale, v_w_out, v_g_ffn, v_w_gate_up, v_w_down, v_g_ple_gate, v_w_ple_gate, v_w_ple_up, v_g_ple_post, v_g_final):
    seq = x.shape[1]
    xs = x[0]
    ps = p[0, 0]
    tgt = loss_target[0]
    ax, ay, ac = lax.axis_index("x"), lax.axis_index("y"), lax.axis_index("c")
    me = _block(ax, ay, ac)
    blks = jnp.stack([_block(1 - ax, ay, ac), _block(ax, 1 - ay, ac), _block(1 - ax, 1 - ay, ac), me]).astype(jnp.int32)
    rows = lambda gth: gth.reshape((-1,) + gth.shape[2:])

    w_in_t, w_out_f, conv_w_t = [rows(gth) for gth in _all_gather([
        w_in[0].T.astype(BF16),
        w_out[0].astype(BF16),
        jnp.pad(conv_w[0].T, ((0, 0), (0, CONV_HALO - CONV_K))),
    ], "gather_first")]
    conv_w_f = conv_w_t.T
    pool_w_b = pool_w[0].astype(BF16)

    carry = _Carry()
    _carry_gather(carry, [w_gate_up[0].T.astype(BF16)], 8, 12)
    (z, u1, pooled, x1, mix), (w_gu_all,) = _fwd_mix(
        xs, w_in_t, w_out_f, g_mix, conv_w_f, conv_b, ln_g, ln_b, pool_w_b, pool_scale, min(256, seq), carry)
    w_gu_t = rows(w_gu_all)

    carry = _Carry()
    _carry_gather(carry, [w_down[0].astype(BF16),
                          w_ple_gate[0].astype(BF16),
                          w_ple_up[0].T.astype(BF16)], 6, 10)
    (h2, g_sav, u_sav, act), late = _ffn_up(x1, w_gu_t, g_ffn, min(256, seq), carry)
    w_down_f, w_pg_f, w_pu_t = [rows(gth) for gth in late]
    dx2, dx2b, hg, ds, dpe, pb, stats_ple = _down_ple(
        x1, act, w_down_f, ps, tgt, w_pg_f, w_pu_t, g_ple_gate, g_ple_post, g_final.reshape(1, D_MODEL),
        min(256, seq))
    dg, du, dx1, dx1b, stats_ffn = _ffn_bwd(dx2, dx2b, x1, g_sav, u_sav, w_gu_t, w_down_f, g_ffn,
                                            min(256, seq))[0]

    d_w_pg, _ = _grad_matmul(hg, ds, 256, "grad_w_ple_gate")
    d_w_pu_t, _ = _grad_matmul(dpe, pb, 256, "grad_w_ple_up")
    d_w_gu_t, _ = _grad_matmul(dg, h2, 256, "grad_w_gate_up", a2=du)
    d_w_out, _ = _grad_matmul(mix, dx1b, 256, "grad_w_out")
    early = [_by_device(d_w_pg), _by_device(d_w_pu_t), _by_device(d_w_gu_t), _by_device(d_w_out)]
    carry = _Carry()
    _carry_pair(carry, early)
    d_w_down, from_sib = _grad_matmul(act, dx2b, 256, "grad_w_down", carry=carry)
    early_parts = _pair_add(early, from_sib, blks, "pair_add_early")

    carry = _Carry()
    _carry_chip(carry, early_parts)
    _carry_pair(carry, [_by_device(d_w_down)])
    (dx, dz, h1, vec_mix, dconv_w_part, dpool_w_part), carried = _bwd_mix(
        dx1, dx1b, xs, z, u1, pooled, w_in_t, w_out_f, g_mix, conv_w_f, ln_g, ln_b, pool_w_b, pool_scale,
        min(256, seq), carry)
    early_chips, down_sib = carried[:4], carried[4:]
    down_parts = _pair_add([_by_device(d_w_down)], down_sib, blks, "pair_add_down")

    carry = _Carry()
    _carry_chip(carry, down_parts)
    d_w_in_t, down_chips = _grad_matmul(dz, h1, 256, "grad_w_in", carry=carry)
    (gr_w_in_t,), small = _reduce_scatter(
        [_by_device(d_w_in_t)], [vec_mix, stats_ple, stats_ffn, dconv_w_part, dpool_w_part], "scatter_last")

    vec_names = ["g_mix", "ln_g", "ln_b", "conv_b", "pool_scale", "g_final", "g_ple_post", "g_ple_gate", "g_ffn",
                 "pool_w"]
    layout = [(0, 0, D_MODEL), (0, 1, C_CONV), (0, 2, C_CONV), (0, 3, C_CONV), (0, 4, C_POOL),
              (1, 0, D_MODEL), (1, 1, D_MODEL), (1, 2, D_MODEL), (2, 0, D_MODEL), (4, None, None)]
    as_row = lambda t: t.reshape(1, D_MODEL)
    params = [(g_mix, m_g_mix, v_g_mix), (ln_g, m_ln_g, v_ln_g), (ln_b, m_ln_b, v_ln_b),
              (conv_b, m_conv_b, v_conv_b), (pool_scale, m_pool_scale, v_pool_scale),
              (as_row(g_final), as_row(m_g_final), as_row(v_g_final)),
              (g_ple_post, m_g_ple_post, v_g_ple_post), (g_ple_gate, m_g_ple_gate, v_g_ple_gate),
              (g_ffn, m_g_ffn, v_g_ffn), (pool_w[0], m_pool_w[0], v_pool_w[0])]
    tots, small_upd = _small_update(small, layout, params, "small_update")
    loss = tots[1][3, 0]
    upd = {}
    for nm, res, prm in zip(vec_names, small_upd, [g_mix, ln_g, ln_b, conv_b, pool_scale, g_final, g_ple_post,
                                                    g_ple_gate, g_ffn, pool_w]):
        upd[nm] = tuple(t.reshape(prm.shape) for t in res)
    gr_conv_w = lax.dynamic_slice_in_dim(tots[3][:CONV_K], me * (C_CONV // N_DEV), C_CONV // N_DEV, axis=1)

    natural = lambda t: t[None]
    turned = lambda t: t.T[None]
    (res_pg, res_gu, res_out, res_down), _ = _sum_adamw([
        (early_parts[0], early_chips[0], w_ple_gate[0], m_w_ple_gate[0], v_w_ple_gate[0]),
        (early_parts[2], early_chips[2], w_gate_up[0].T, m_w_gate_up[0].T, v_w_gate_up[0].T),
        (early_parts[3], early_chips[3], w_out[0], m_w_out[0], v_w_out[0]),
        (down_parts[0], down_chips[0], w_down[0], m_w_down[0], v_w_down[0])], "adamw_big")
    upd["w_ple_gate"] = tuple(natural(t) for t in res_pg)
    upd["w_gate_up"] = tuple(turned(t) for t in res_gu)
    upd["w_out"] = tuple(natural(t) for t in res_out)
    upd["w_down"] = tuple(natural(t) for t in res_down)
    (gr_w_pu_t,) = _chip_sum(early_parts[1:2], early_chips[1:2], "chip_sum")
    plain = [
        ("w_in", w_in[0].T, gr_w_in_t, m_w_in[0].T, v_w_in[0].T, True),
        ("w_ple_up", w_ple_up[0], gr_w_pu_t.T, m_w_ple_up[0], v_w_ple_up[0], False),
        ("conv_w", conv_w[0], gr_conv_w, m_conv_w[0], v_conv_w[0], False),
    ]
    for nm, w_, g_, m_, v_, transposed in plain:
        res = (g_,) + tuple(_adamw(w_, g_, m_, v_, "adamw_" + nm))
        upd[nm] = tuple((t.T if transposed else t)[None] for t in res)

    order = ["g_mix", "w_in", "conv_w", "conv_b", "ln_g", "ln_b", "pool_w", "pool_scale", "w_out", "g_ffn",
             "w_gate_up", "w_down", "g_ple_gate", "w_ple_gate", "w_ple_up", "g_ple_post", "g_final"]
    outs = [loss, dx[None]]
    for k in range(4):
        outs += [upd[nm][k] for nm in order]
    return tuple(outs)
```

```python
import functools

import jax
import jax.numpy as jnp
from jax import lax
from jax.experimental import pallas as pl
from jax.experimental.pallas import tpu as pltpu

D_MODEL = 1024
C_CONV = 512
C_POOL = 512
Z_WIDTH = 2 * C_CONV + C_POOL
POOL_WINDOWS = (2, 4, 8, 16)
POOL_GROUP = 128
CONV_K = 31
D_FF = 2816
D_PLE = 256
EPS = 1e-6
N_DEV = 8

ADAM_LR = 0.001
ADAM_B1 = 0.9
ADAM_B2 = 0.999
ADAM_EPS = 1e-08
ADAM_WD = 0.01
ADAM_STEP = 10

CONV_HALO = 32
POOL_HALO = 16
ROW_CHUNK = 32
V7X_VMEM_LIMIT = 56 * 1024 * 1024
FF_CHUNK = D_FF // 2

BF16 = jnp.bfloat16
F32 = jnp.float32
MESH = pl.DeviceIdType.MESH


def _dot(a, b):
    return lax.dot_general(a, b, (((1,), (0,)), ((), ())), preferred_element_type=F32)


def _dot_nt(a, b):
    return lax.dot_general(a, b, (((1,), (1,)), ((), ())), preferred_element_type=F32)


def _dot_tn(a, b):
    return lax.dot_general(a, b, (((0,), (0,)), ((), ())), preferred_element_type=F32)


def _rms_fwd(x, g):
    r = lax.rsqrt(jnp.mean(x * x, axis=-1, keepdims=True) + EPS)
    return x * r * g, r


def _rms_bwd(x, r, g, dy):
    xr = x * r
    dg = jnp.sum(dy * xr, axis=0, keepdims=True)
    dyg = dy * g
    dx = r * (dyg - xr * jnp.mean(dyg * xr, axis=-1, keepdims=True))
    return dx, dg


def _sigmoid(x):
    return jax.nn.sigmoid(x)


def _params(sem=None):
    return pltpu.CompilerParams(dimension_semantics=sem, vmem_limit_bytes=V7X_VMEM_LIMIT)


def _place():
    x, y, c = lax.axis_index("x"), lax.axis_index("y"), lax.axis_index("c")
    chips = [(1 - x, y), (x, 1 - y), (1 - x, 1 - y)]
    return (x, y, c), (x, y, 1 - c), chips


def _block(px, py, pc):
    return 4 * px + 2 * py + pc


class _Carry:
    def __init__(self):
        self.inputs = []
        self.out_shapes = []
        self.copies = []
        self.locals = []

    def add_input(self, arr):
        self.inputs.append(arr)
        return len(self.inputs) - 1

    def add_output(self, shape, dtype):
        self.out_shapes.append(jax.ShapeDtypeStruct(shape, dtype))
        return len(self.out_shapes) - 1

    def local(self, src_idx, dst_idx, dst_blk):
        self.locals.append((src_idx, dst_idx, dst_blk))

    def copy(self, src, dst_idx, dst_blk, got_blk, peer, step=0, after=()):
        self.copies.append(dict(src=src, dst_idx=dst_idx, dst_blk=dst_blk, got_blk=got_blk, peer=peer, step=step,
                                after=tuple(after)))
        return len(self.copies) - 1

    def sem_shapes(self):
        return [pltpu.SemaphoreType.DMA((max(1, len(self.copies)),)),
                pltpu.SemaphoreType.DMA((max(1, len(self.copies)),)),
                pltpu.SemaphoreType.DMA((max(1, len(self.locals)),))]

    @staticmethod
    def _view(ref, where):
        if isinstance(where, tuple):
            blk, row0, nrows = where
            return ref.at[blk, pl.ds(row0, nrows)]
        return ref.at[where]

    def _desc(self, k, ins, outs, sems, place):
        cp = self.copies[k]
        me, sib, chips = place
        kind, idx, blk = cp["src"]
        src = (ins if kind == "in" else outs)[idx]
        if blk is not None:
            src = self._view(src, blk(*place))
        to = sib if cp["peer"] == "sib" else (*chips[cp["peer"]], me[2])
        return pltpu.make_async_remote_copy(
            src_ref=src, dst_ref=self._view(outs[cp["dst_idx"]], cp["dst_blk"](*place)),
            send_sem=sems[0].at[k], recv_sem=sems[1].at[k], device_id=to, device_id_type=MESH)

    def _arrival(self, k, outs, sems, place):
        cp = self.copies[k]
        got = self._view(outs[cp["dst_idx"]], cp["got_blk"](*place))
        return pltpu.make_async_remote_copy(
            src_ref=got, dst_ref=got, send_sem=sems[0].at[k], recv_sem=sems[1].at[k],
            device_id=place[0], device_id_type=MESH)

    def _local(self, n, ins, outs, sems, place):
        src_idx, dst_idx, blk = self.locals[n]
        return pltpu.make_async_copy(ins[src_idx], outs[dst_idx].at[blk(*place)], sems[2].at[n])

    def starts(self, step, nsteps, ins, outs, sems):
        place = _place()
        self._waited = set()
        for s in sorted({0} | {cp["step"] for cp in self.copies}):
            ks = [k for k, cp in enumerate(self.copies) if cp["step"] == s]

            @pl.when(step == min(s, nsteps - 1))
            def _(s=s, ks=ks):
                if s == 0:
                    for n in range(len(self.locals)):
                        self._local(n, ins, outs, sems, place).start()
                for k in ks:
                    for a in self.copies[k]["after"]:
                        if a not in self._waited:
                            self._arrival(a, outs, sems, place).wait_recv()
                            self._waited.add(a)
                    self._desc(k, ins, outs, sems, place).start()

    def finish(self, step, nsteps, ins, outs, sems):
        place = _place()

        @pl.when(step == nsteps - 1)
        def _():
            for k in range(len(self.copies)):
                if k not in self._waited:
                    self._arrival(k, outs, sems, place).wait_recv()
            for k in range(len(self.copies)):
                self._desc(k, ins, outs, sems, place).wait_send()
            for n in range(len(self.locals)):
                self._local(n, ins, outs, sems, place).wait()


def _const_blk(j):
    return lambda me, sib, chips: j


def _carry_gather(carry, shards, relay_step, last_step):
    outs = []
    for sh in shards:
        i = carry.add_input(sh)
        o = carry.add_output((N_DEV,) + sh.shape, sh.dtype)
        half = sh.shape[0] // 2
        tile = 16 if sh.dtype == BF16 else 8
        split = half % tile == 0
        rows = [(0, half), (half, sh.shape[0] - half)] if split else [(0, sh.shape[0]), None]

        def whole(j, core):
            return lambda me, sib, chips, j=j, core=core: _block(*chips[j], me[2] if core == 0 else 1 - me[2])

        def part(j, core, h, rows=rows):
            return lambda me, sib, chips: (_block(*chips[j], me[2] if core == 0 else 1 - me[2]),) + rows[h]

        mine = lambda me, sib, chips: _block(*me)
        carry.local(i, o, mine)
        carry.copy(("in", i, None), o, mine, lambda me, sib, chips: _block(*sib), "sib")
        near = [carry.copy(("in", i, None), o, mine, whole(j, 0), j) for j in range(2)]
        for j in range(2):
            carry.copy(("out", o, whole(j, 0)), o, whole(j, 0), whole(j, 1), "sib", step=relay_step, after=(near[j],))
        for j in range(2):
            if rows[j] is None:
                continue
            far = carry.copy(("out", o, part(j, 0, j)), o, part(j, 0, j), part(2, 0, j), 1 - j,
                             step=relay_step, after=(near[j],))
            carry.copy(("out", o, part(2, 0, j)), o, part(2, 0, j), part(2, 1, j), "sib", step=last_step, after=(far,))
        outs.append(o)
    return outs


def _carry_pair(carry, grads):
    outs = []
    for g in grads:
        i = carry.add_input(g)
        o = carry.add_output((4,) + g.shape[1:], g.dtype)
        for j in range(4):
            if j < 3:
                blk = lambda me, sib, chips, j=j: _block(*chips[j], 1 - me[2])
            else:
                blk = lambda me, sib, chips: _block(*sib)
            carry.copy(("in", i, blk), o, _const_blk(j), _const_blk(j), "sib")
        outs.append(o)
    return outs


def _carry_chip(carry, parts):
    outs = []
    for p in parts:
        i = carry.add_input(p)
        o = carry.add_output((3,) + p.shape[1:], p.dtype)
        for j in range(3):
            carry.copy(("in", i, _const_blk(j)), o, _const_blk(j), _const_blk(j), j)
        outs.append(o)
    return outs


def _pcall(body, *, grid, in_specs, out_specs, out_shape, scratch_shapes, name, args, carry=None):
    sem = ("arbitrary",) * len(grid)
    if carry is None:
        res = pl.pallas_call(body, grid=grid, in_specs=in_specs, out_specs=out_specs, out_shape=out_shape,
                             scratch_shapes=scratch_shapes, compiler_params=_params(sem), name=name)(*args)
        return list(res), []
    n_in, n_out, n_scr = len(in_specs), len(out_specs), len(scratch_shapes)
    c_in, c_out = len(carry.inputs), len(carry.out_shapes)
    nsteps = 1
    for extent in grid:
        nsteps *= extent

    def wrapped(*refs):
        ins = refs[:n_in]
        cins = refs[n_in:n_in + c_in]
        o0 = n_in + c_in
        outs = refs[o0:o0 + n_out]
        couts = refs[o0 + n_out:o0 + n_out + c_out]
        s0 = o0 + n_out + c_out
        scr = refs[s0:s0 + n_scr]
        sems = refs[s0 + n_scr:]
        step = pl.program_id(0)
        for d in range(1, len(grid)):
            step = step * grid[d] + pl.program_id(d)
        carry.starts(step, nsteps, cins, couts, sems)
        body(*ins, *outs, *scr)
        carry.finish(step, nsteps, cins, couts, sems)

    any_spec = pl.BlockSpec(memory_space=pl.ANY)
    res = pl.pallas_call(
        wrapped, grid=grid,
        in_specs=list(in_specs) + [any_spec] * c_in,
        out_specs=list(out_specs) + [any_spec] * c_out,
        out_shape=list(out_shape) + carry.out_shapes,
        scratch_shapes=list(scratch_shapes) + carry.sem_shapes(),
        compiler_params=_params(sem), name=name)(*args, *carry.inputs)
    return list(res[:n_out]), list(res[n_out:])


def _shifted_copies(buf, shifted, tm):
    span = tm + CONV_HALO - 8
    for r in range(1, 8):
        shifted[r - 1, 0:span, :] = buf[r:r + span, :]


def _rows_at(buf, shifted, start):
    aligned, r = (start // 8) * 8, start % 8
    if r == 0:
        return buf[aligned:aligned + ROW_CHUNK, :]
    return shifted[r - 1, aligned:aligned + ROW_CHUNK, :]


def _row_spec(tm, width):
    return pl.BlockSpec((tm, width), lambda i: (i, 0))


def _const_spec(shape):
    return pl.BlockSpec(shape, lambda i: (0,) * len(shape))


def _weight_spec(shape):
    return pl.BlockSpec(shape, lambda i: (0,) * len(shape), pipeline_mode=pl.Buffered(1))


def _norm_in(x, g_mix, tm, carry=None):
    s = x.shape[0]

    def body(x_ref, gmix_ref, h1_ref):
        h, _ = _rms_fwd(x_ref[...], gmix_ref[...])
        h1_ref[...] = h.astype(BF16)

    return _pcall(
        body,
        grid=(s // tm,),
        in_specs=[_row_spec(tm, D_MODEL), _const_spec((1, D_MODEL))],
        out_specs=[_row_spec(tm, D_MODEL)],
        out_shape=[jax.ShapeDtypeStruct((s, D_MODEL), BF16)],
        scratch_shapes=[],
        name="norm_in",
        args=(x, g_mix),
        carry=carry,
    )


def _fwd_mix(x, h1, w_in_t, w_out, conv_w, conv_b, ln_g, ln_b, pool_w, pool_scale, tm, carry=None):
    s = x.shape[0]
    nt = s // tm

    def body(x_ref, h1_ref, win_ref, wout_ref, cw_ref, cb_ref, lng_ref, lnb_ref, pw_ref, ps_ref,
             z_ref, u1_ref, pooled_ref, x1_ref, mix_ref, ubuf, vbuf, ush):
        i = pl.program_id(0)

        @pl.when(i == 0)
        def _():
            ubuf[0:CONV_HALO, :] = jnp.zeros((CONV_HALO, C_CONV), F32)
            vbuf[0:POOL_HALO, :] = jnp.zeros((POOL_HALO, C_POOL), F32)

        xt = x_ref[...]
        z = _dot_nt(h1_ref[...], win_ref[...])
        z_ref[...] = z
        a = z[:, :C_CONV]
        b = z[:, C_CONV:2 * C_CONV]
        v = z[:, 2 * C_CONV:]
        ubuf[CONV_HALO:CONV_HALO + tm, :] = a * _sigmoid(b)
        vbuf[POOL_HALO:POOL_HALO + tm, :] = v

        _shifted_copies(ubuf, ush, tm)
        for rc in range(tm // ROW_CHUNK):
            base = rc * ROW_CHUNK + CONV_HALO - (CONV_K - 1)
            acc = jnp.broadcast_to(cb_ref[...], (ROW_CHUNK, C_CONV))
            for k in range(CONV_K):
                acc = acc + cw_ref[k:k + 1, :] * _rows_at(ubuf, ush, base + k)
            u1_ref[rc * ROW_CHUNK:(rc + 1) * ROW_CHUNK, :] = acc

        u1 = u1_ref[...]
        mu = jnp.mean(u1, axis=-1, keepdims=True)
        cen = u1 - mu
        rstd = lax.rsqrt(jnp.mean(cen * cen, axis=-1, keepdims=True) + EPS)
        u2 = cen * rstd * lng_ref[...] + lnb_ref[...]
        u = u2 * _sigmoid(u2)

        pos1 = (i * tm + lax.broadcasted_iota(jnp.int32, (tm, 1), 0) + 1).astype(F32)
        parts = [u]
        for g, w in enumerate(POOL_WINDOWS):
            cols = slice(g * POOL_GROUP, (g + 1) * POOL_GROUP)
            vg = v[:, cols]
            tot = vg
            for j in range(1, w):
                tot = tot + vbuf[POOL_HALO - j:POOL_HALO - j + tm, cols]
            pooled = tot / jnp.minimum(pos1, float(w)) - vg
            pooled_b = pooled.astype(BF16)
            pooled_ref[:, cols] = pooled_b
            parts.append(_dot(pooled_b, pw_ref[g]) * ps_ref[:, cols])
        mix = jnp.concatenate(parts, axis=-1).astype(BF16)
        mix_ref[...] = mix
        x1_ref[...] = xt + _dot(mix, wout_ref[...])

        ubuf[0:CONV_HALO, :] = ubuf[tm:tm + CONV_HALO, :]
        vbuf[0:POOL_HALO, :] = vbuf[tm:tm + POOL_HALO, :]

    return _pcall(
        body,
        grid=(nt,),
        in_specs=[
            _row_spec(tm, D_MODEL),
            _row_spec(tm, D_MODEL),
            _const_spec((Z_WIDTH, D_MODEL)),
            _const_spec((D_MODEL, D_MODEL)),
            _const_spec((CONV_HALO, C_CONV)),
            _const_spec((1, C_CONV)),
            _const_spec((1, C_CONV)),
            _const_spec((1, C_CONV)),
            _const_spec((len(POOL_WINDOWS), POOL_GROUP, POOL_GROUP)),
            _const_spec((1, C_POOL)),
        ],
        out_specs=[
            _row_spec(tm, Z_WIDTH),
            _row_spec(tm, C_CONV),
            _row_spec(tm, C_POOL),
            _row_spec(tm, D_MODEL),
            _row_spec(tm, D_MODEL),
        ],
        out_shape=[
            jax.ShapeDtypeStruct((s, Z_WIDTH), F32),
            jax.ShapeDtypeStruct((s, C_CONV), F32),
            jax.ShapeDtypeStruct((s, C_POOL), BF16),
            jax.ShapeDtypeStruct((s, D_MODEL), F32),
            jax.ShapeDtypeStruct((s, D_MODEL), BF16),
        ],
        scratch_shapes=[
            pltpu.VMEM((tm + CONV_HALO, C_CONV), F32),
            pltpu.VMEM((tm + POOL_HALO, C_POOL), F32),
            pltpu.VMEM((7, tm + CONV_HALO, C_CONV), F32),
        ],
        name="fwd_mix",
        args=(x, h1, w_in_t, w_out, conv_w, conv_b, ln_g, ln_b, pool_w, pool_scale),
        carry=carry,
    )


def _ffn_up(x1, w_gu_t, g_ffn, tm, carry=None):
    s = x1.shape[0]

    def body(x1_ref, w_ref, gffn_ref, h2_ref, g_ref, u_ref, act_ref):
        h, _ = _rms_fwd(x1_ref[...], gffn_ref[...])
        h2 = h.astype(BF16)
        h2_ref[...] = h2
        for c in range(D_FF // FF_CHUNK):
            cols = slice(c * FF_CHUNK, (c + 1) * FF_CHUNK)
            g = _dot_nt(h2, w_ref[c * FF_CHUNK:(c + 1) * FF_CHUNK, :])
            u = _dot_nt(h2, w_ref[D_FF + c * FF_CHUNK:D_FF + (c + 1) * FF_CHUNK, :])
            g_ref[:, cols] = g.astype(BF16)
            u_ref[:, cols] = u.astype(BF16)
            act_ref[:, cols] = (g * _sigmoid(g) * u).astype(BF16)

    return _pcall(
        body,
        grid=(s // tm,),
        in_specs=[_row_spec(tm, D_MODEL), _weight_spec((2 * D_FF, D_MODEL)), _const_spec((1, D_MODEL))],
        out_specs=[_row_spec(tm, D_MODEL), _row_spec(tm, D_FF), _row_spec(tm, D_FF), _row_spec(tm, D_FF)],
        out_shape=[
            jax.ShapeDtypeStruct((s, D_MODEL), BF16),
            jax.ShapeDtypeStruct((s, D_FF), BF16),
            jax.ShapeDtypeStruct((s, D_FF), BF16),
            jax.ShapeDtypeStruct((s, D_FF), BF16),
        ],
        scratch_shapes=[],
        name="ffn_up",
        args=(x1, w_gu_t, g_ffn),
        carry=carry,
    )


def _down_ple(x1, act, w_down, p, tgt, w_pg, w_pu_t, g_gate, g_post, g_final, tm):
    s = x1.shape[0]
    nt = s // tm

    def body(x1_ref, act_ref, wd_ref, p_ref, t_ref, wpg_ref, wpu_ref, gg_ref, gp_ref, gf_ref,
             dx2_ref, dx2b_ref, hg_ref, ds_ref, dpe_ref, pb_ref, stats_ref, x2_cur, x2_next):
        i = pl.program_id(0)

        @pl.when(i == 0)
        def _():
            stats_ref[...] = jnp.zeros_like(stats_ref)
            x2_cur[...] = jnp.zeros((tm, D_MODEL), F32)

        def down(c):
            cols = slice(c * 256, (c + 1) * 256)
            x2_next[:, cols] = x1_ref[:, cols] + _dot(act_ref[...], wd_ref[:, cols])

        x2 = x2_cur[...]
        counts = i >= 1

        hg, rg = _rms_fwd(x2, gg_ref[...])
        hg_b = hg.astype(BF16)
        hg_ref[...] = hg_b
        down(0)
        gate = _sigmoid(_dot(hg_b, wpg_ref[...]))
        pb = p_ref[...].astype(BF16)
        pb_ref[...] = pb
        pe = _dot_nt(pb, wpu_ref[...])
        e, rp = _rms_fwd(pe, gp_ref[...])
        x3 = x2 + gate * e
        down(1)
        y, r3 = _rms_fwd(x3, gf_ref[...])
        diff = y - t_ref[...]
        loss = 0.5 * jnp.sum(jnp.sum(diff * diff, axis=-1, keepdims=True), axis=0, keepdims=True) / D_MODEL
        dy = diff * (1.0 / D_MODEL)

        dx3, dgf = _rms_bwd(x3, r3, gf_ref[...], dy)
        down(2)
        dpe, dgp = _rms_bwd(pe, rp, gp_ref[...], dx3 * gate)
        dpe_ref[...] = dpe.astype(BF16)
        ds = (dx3 * e * gate * (1.0 - gate)).astype(BF16)
        ds_ref[...] = ds
        dhg = _dot_nt(ds, wpg_ref[...])
        down(3)
        dxg, dgg = _rms_bwd(x2, rg, gg_ref[...], dhg)
        dx2 = dx3 + dxg
        dx2_ref[...] = dx2
        dx2b_ref[...] = dx2.astype(BF16)
        x2_cur[...] = x2_next[...]

        stats_ref[0:1, :] += jnp.where(counts, dgf, 0.0)
        stats_ref[1:2, :] += jnp.where(counts, dgp, 0.0)
        stats_ref[2:3, :] += jnp.where(counts, dgg, 0.0)
        stats_ref[3:4, :] += jnp.where(counts, jnp.broadcast_to(loss, (1, D_MODEL)), 0.0)

    ahead = lambda width: pl.BlockSpec((tm, width), lambda i: (jnp.minimum(i, nt - 1), 0))
    behind = lambda width: pl.BlockSpec((tm, width), lambda i: (jnp.maximum(i - 1, 0), 0))
    return pl.pallas_call(
        body,
        grid=(nt + 1,),
        in_specs=[
            ahead(D_MODEL),
            ahead(D_FF),
            _weight_spec((D_FF, D_MODEL)),
            behind(D_PLE),
            behind(D_MODEL),
            _weight_spec((D_MODEL, D_MODEL)),
            _weight_spec((D_MODEL, D_PLE)),
            _const_spec((1, D_MODEL)),
            _const_spec((1, D_MODEL)),
            _const_spec((1, D_MODEL)),
        ],
        out_specs=[
            behind(D_MODEL),
            behind(D_MODEL),
            behind(D_MODEL),
            behind(D_MODEL),
            behind(D_MODEL),
            behind(D_PLE),
            _const_spec((8, D_MODEL)),
        ],
        out_shape=[
            jax.ShapeDtypeStruct((s, D_MODEL), F32),
            jax.ShapeDtypeStruct((s, D_MODEL), BF16),
            jax.ShapeDtypeStruct((s, D_MODEL), BF16),
            jax.ShapeDtypeStruct((s, D_MODEL), BF16),
            jax.ShapeDtypeStruct((s, D_MODEL), BF16),
            jax.ShapeDtypeStruct((s, D_PLE), BF16),
            jax.ShapeDtypeStruct((8, D_MODEL), F32),
        ],
        scratch_shapes=[pltpu.VMEM((tm, D_MODEL), F32), pltpu.VMEM((tm, D_MODEL), F32)],
        compiler_params=_params(("arbitrary",)),
        name="down_ple",
    )(x1, act, w_down, p, tgt, w_pg, w_pu_t, g_gate, g_post, g_final)


def _ffn_bwd(dx2, dx2b, x1, g_sav, u_sav, w_gu_t, w_down, g_ffn, tm, carry=None):
    s = x1.shape[0]

    def body(dx2_ref, dx2b_ref, x1_ref, g_ref, u_ref, w_ref, wd_ref, gffn_ref,
             dg_ref, du_ref, dx1_ref, dx1b_ref, stats_ref):
        @pl.when(pl.program_id(0) == 0)
        def _():
            stats_ref[...] = jnp.zeros_like(stats_ref)

        dx2b = dx2b_ref[...]
        nc = D_FF // FF_CHUNK
        dacts = [_dot_nt(dx2b, wd_ref[c * FF_CHUNK:(c + 1) * FF_CHUNK, :]) for c in range(nc)]
        dh2 = jnp.zeros((tm, D_MODEL), F32)
        for c in range(nc):
            cols = slice(c * FF_CHUNK, (c + 1) * FF_CHUNK)
            g = g_ref[:, cols].astype(F32)
            u = u_ref[:, cols].astype(F32)
            sg = _sigmoid(g)
            dg = (dacts[c] * u * sg * (1.0 + g * (1.0 - sg))).astype(BF16)
            du = (dacts[c] * g * sg).astype(BF16)
            dg_ref[:, cols] = dg
            du_ref[:, cols] = du
            dh2 = dh2 + _dot(dg, w_ref[c * FF_CHUNK:(c + 1) * FF_CHUNK, :])
            dh2 = dh2 + _dot(du, w_ref[D_FF + c * FF_CHUNK:D_FF + (c + 1) * FF_CHUNK, :])

        x1 = x1_ref[...]
        r2 = lax.rsqrt(jnp.mean(x1 * x1, axis=-1, keepdims=True) + EPS)
        dxn, dgf = _rms_bwd(x1, r2, gffn_ref[...], dh2)
        dx1 = dx2_ref[...] + dxn
        dx1_ref[...] = dx1
        dx1b_ref[...] = dx1.astype(BF16)
        stats_ref[0:1, :] += dgf

    return _pcall(
        body,
        grid=(s // tm,),
        in_specs=[
            _row_spec(tm, D_MODEL), _row_spec(tm, D_MODEL), _row_spec(tm, D_MODEL),
            _row_spec(tm, D_FF), _row_spec(tm, D_FF),
            _weight_spec((2 * D_FF, D_MODEL)), _weight_spec((D_FF, D_MODEL)), _const_spec((1, D_MODEL)),
        ],
        out_specs=[_row_spec(tm, D_FF), _row_spec(tm, D_FF), _row_spec(tm, D_MODEL), _row_spec(tm, D_MODEL),
                   _const_spec((8, D_MODEL))],
        out_shape=[
            jax.ShapeDtypeStruct((s, D_FF), BF16),
            jax.ShapeDtypeStruct((s, D_FF), BF16),
            jax.ShapeDtypeStruct((s, D_MODEL), F32),
            jax.ShapeDtypeStruct((s, D_MODEL), BF16),
            jax.ShapeDtypeStruct((8, D_MODEL), F32),
        ],
        scratch_shapes=[],
        name="ffn_bwd",
        args=(dx2, dx2b, x1, g_sav, u_sav, w_gu_t, w_down, g_ffn),
        carry=carry,
    )


def _bwd_mix(dx1, dx1b, x, z, u1, pooled, w_in_t, w_out, g_mix, conv_w, ln_g, ln_b, pool_w, pool_scale, tm,
             carry=None):
    s = x.shape[0]
    nt = s // tm

    def body(dx1_ref, dx1b_ref, x_ref, z_ref, u1_ref, pooled_ref, win_ref, wout_ref, gmix_ref, cw_ref,
             lng_ref, lnb_ref, pw_ref, ps_ref,
             dx_ref, dz_ref, vec_ref, dcw_ref, dpw_ref, dubuf, dvbuf, u0buf, du0buf, dush):
        i = pl.program_id(0)
        tile = nt - 1 - i

        @pl.when(i == 0)
        def _():
            vec_ref[...] = jnp.zeros_like(vec_ref)
            dcw_ref[...] = jnp.zeros_like(dcw_ref)
            dpw_ref[...] = jnp.zeros_like(dpw_ref)
            dubuf[tm:tm + CONV_HALO, :] = jnp.zeros((CONV_HALO, C_CONV), F32)
            dvbuf[tm:tm + POOL_HALO, :] = jnp.zeros((POOL_HALO, C_POOL), F32)

        dmix = _dot_nt(dx1b_ref[...], wout_ref[...])
        du = dmix[:, :C_CONV]
        dq = dmix[:, C_CONV:]

        pos1 = (tile * tm + lax.broadcasted_iota(jnp.int32, (tm, 1), 0) + 1).astype(F32)
        dpooled_parts = []
        dps_rows = []
        for g, w in enumerate(POOL_WINDOWS):
            cols = slice(g * POOL_GROUP, (g + 1) * POOL_GROUP)
            pooled_b = pooled_ref[:, cols]
            mixed = _dot(pooled_b, pw_ref[g])
            dqg = dq[:, cols]
            dps_rows.append(jnp.sum(dqg * mixed, axis=0, keepdims=True))
            dmixed = (dqg * ps_ref[:, cols]).astype(BF16)
            dpw_ref[g] += _dot_tn(pooled_b, dmixed)
            dpooled = _dot_nt(dmixed, pw_ref[g])
            dpooled_parts.append(dpooled)
            dvbuf[0:tm, cols] = dpooled / jnp.minimum(pos1, float(w))
        vec_ref[4:5, 0:C_POOL] += jnp.concatenate(dps_rows, axis=-1)
        dv_parts = []
        for g, w in enumerate(POOL_WINDOWS):
            cols = slice(g * POOL_GROUP, (g + 1) * POOL_GROUP)
            tot = dvbuf[0:tm, cols]
            for j in range(1, w):
                tot = tot + dvbuf[j:j + tm, cols]
            dv_parts.append(tot - dpooled_parts[g])

        u1 = u1_ref[...]
        mu = jnp.mean(u1, axis=-1, keepdims=True)
        cen = u1 - mu
        rstd = lax.rsqrt(jnp.mean(cen * cen, axis=-1, keepdims=True) + EPS)
        xhat = cen * rstd
        u2 = xhat * lng_ref[...] + lnb_ref[...]
        sg2 = _sigmoid(u2)
        du2 = du * sg2 * (1.0 + u2 * (1.0 - sg2))
        vec_ref[1:2, 0:C_CONV] += jnp.sum(du2 * xhat, axis=0, keepdims=True)
        vec_ref[2:3, 0:C_CONV] += jnp.sum(du2, axis=0, keepdims=True)
        t1 = du2 * lng_ref[...]
        du1 = rstd * (t1 - jnp.mean(t1, axis=-1, keepdims=True)
                      - xhat * jnp.mean(t1 * xhat, axis=-1, keepdims=True))
        vec_ref[3:4, 0:C_CONV] += jnp.sum(du1, axis=0, keepdims=True)
        dubuf[0:tm, :] = du1

        zt = z_ref[...]
        a = zt[:, :C_CONV]
        sgb = _sigmoid(zt[:, C_CONV:2 * C_CONV])
        u0buf[...] = a * sgb

        _shifted_copies(dubuf, dush, tm)
        for rc in range(tm // ROW_CHUNK):
            r0 = rc * ROW_CHUNK
            acc = jnp.zeros((ROW_CHUNK, C_CONV), F32)
            for k in range(CONV_K):
                acc = acc + cw_ref[k:k + 1, :] * _rows_at(dubuf, dush, r0 + (CONV_K - 1) - k)
            du0buf[r0:r0 + ROW_CHUNK, :] = acc
        for k in range(CONV_K):
            acc = jnp.zeros((ROW_CHUNK, C_CONV), F32)
            for rc in range(tm // ROW_CHUNK):
                r0 = rc * ROW_CHUNK
                acc = acc + u0buf[r0:r0 + ROW_CHUNK, :] * _rows_at(dubuf, dush, r0 + (CONV_K - 1) - k)
            dcw_ref[k:k + 1, :] += jnp.sum(acc, axis=0, keepdims=True)
        du0 = du0buf[...]

        da = du0 * sgb
        db = du0 * a * sgb * (1.0 - sgb)
        dz = jnp.concatenate([da, db] + dv_parts, axis=-1).astype(BF16)
        dz_ref[...] = dz

        xt = x_ref[...]
        r1 = lax.rsqrt(jnp.mean(xt * xt, axis=-1, keepdims=True) + EPS)
        dh1 = _dot(dz, win_ref[...])
        dxn, dgm = _rms_bwd(xt, r1, gmix_ref[...], dh1)
        dx_ref[...] = dx1_ref[...] + dxn
        vec_ref[0:1, :] += dgm

        dubuf[tm:tm + CONV_HALO, :] = dubuf[0:CONV_HALO, :]
        dvbuf[tm:tm + POOL_HALO, :] = dvbuf[0:POOL_HALO, :]

    rev = lambda width: pl.BlockSpec((tm, width), lambda i: (nt - 1 - i, 0))
    return _pcall(
        body,
        grid=(nt,),
        in_specs=[
            rev(D_MODEL), rev(D_MODEL), rev(D_MODEL), rev(Z_WIDTH), rev(C_CONV), rev(C_POOL),
            _const_spec((Z_WIDTH, D_MODEL)),
            _const_spec((D_MODEL, D_MODEL)),
            _const_spec((1, D_MODEL)),
            _const_spec((CONV_HALO, C_CONV)),
            _const_spec((1, C_CONV)),
            _const_spec((1, C_CONV)),
            _const_spec((len(POOL_WINDOWS), POOL_GROUP, POOL_GROUP)),
            _const_spec((1, C_POOL)),
        ],
        out_specs=[
            rev(D_MODEL), rev(Z_WIDTH),
            _const_spec((8, D_MODEL)),
            _const_spec((CONV_HALO, C_CONV)),
            _const_spec((len(POOL_WINDOWS), POOL_GROUP, POOL_GROUP)),
        ],
        out_shape=[
            jax.ShapeDtypeStruct((s, D_MODEL), F32),
            jax.ShapeDtypeStruct((s, Z_WIDTH), BF16),
            jax.ShapeDtypeStruct((8, D_MODEL), F32),
            jax.ShapeDtypeStruct((CONV_HALO, C_CONV), F32),
            jax.ShapeDtypeStruct((len(POOL_WINDOWS), POOL_GROUP, POOL_GROUP), F32),
        ],
        scratch_shapes=[
            pltpu.VMEM((tm + CONV_HALO, C_CONV), F32),
            pltpu.VMEM((tm + POOL_HALO, C_POOL), F32),
            pltpu.VMEM((tm, C_CONV), F32),
            pltpu.VMEM((tm, C_CONV), F32),
            pltpu.VMEM((7, tm + CONV_HALO, C_CONV), F32),
        ],
        name="bwd_mix",
        args=(dx1, dx1b, x, z, u1, pooled, w_in_t, w_out, g_mix, conv_w, ln_g, ln_b, pool_w, pool_scale),
        carry=carry,
    )


def _grad_matmul(a, b, bm, name, a2=None, carry=None):
    s, ma = a.shape
    nb = b.shape[1]
    na = ma // bm
    if a2 is None:
        def body(a_ref, b_ref, o_ref):
            o_ref[...] = _dot_tn(a_ref[...], b_ref[...]).astype(BF16)

        lhs_specs = [pl.BlockSpec((s, bm), lambda i: (0, i))]
        lhs = (a,)
        steps = na
    else:
        def body(a_ref, a2_ref, b_ref, o_ref):
            i = pl.program_id(0)

            @pl.when(i < na)
            def _():
                o_ref[...] = _dot_tn(a_ref[...], b_ref[...]).astype(BF16)

            @pl.when(i >= na)
            def _():
                o_ref[...] = _dot_tn(a2_ref[...], b_ref[...]).astype(BF16)

        lhs_specs = [pl.BlockSpec((s, bm), lambda i: (0, jnp.minimum(i, na - 1))),
                     pl.BlockSpec((s, bm), lambda i: (0, jnp.maximum(i - na, 0)))]
        lhs = (a, a2)
        steps = 2 * na

    outs, carried = _pcall(
        body,
        grid=(steps,),
        in_specs=lhs_specs + [pl.BlockSpec((s, nb), lambda i: (0, 0))],
        out_specs=[pl.BlockSpec((bm, nb), lambda i: (i, 0))],
        out_shape=[jax.ShapeDtypeStruct((steps * bm, nb), BF16)],
        scratch_shapes=[],
        name=name,
        args=lhs + (b,),
        carry=carry,
    )
    return outs[0], carried


def _gather_steps(src, dst, send_sems, recv_sems, local_sems):
    n = len(src)
    me, sib, chips = _place()
    c = me[2]
    started = []
    mine = []

    def copy(a, k, block, to, from_src=False):
        rows = dst[a].at[_block(*block)]
        return pltpu.make_async_remote_copy(
            src_ref=src[a] if from_src else rows, dst_ref=rows,
            send_sem=send_sems.at[a, k], recv_sem=recv_sems.at[a, k],
            device_id=to, device_id_type=MESH)

    def start():
        for a in range(n):
            cp = pltpu.make_async_copy(src[a], dst[a].at[_block(*me)], local_sems.at[a])
            cp.start()
            mine.append(cp)
            first = [copy(a, 0, me, sib, True)]
            first += [copy(a, 1 + j, me, (*chip, c), True) for j, chip in enumerate(chips)]
            for cp in first:
                cp.start()
            started.extend(first)

    def forward():
        for a in range(n):
            for j, chip in enumerate(chips):
                copy(a, 1 + j, (*chip, c), me).wait_recv()
                fwd = copy(a, 4 + j, (*chip, c), sib)
                fwd.start()
                started.append(fwd)

    def finish():
        for a in range(n):
            copy(a, 0, sib, me).wait_recv()
            for j, chip in enumerate(chips):
                copy(a, 4 + j, (*chip, 1 - c), me).wait_recv()
        for cp in started:
            cp.wait_send()
        for cp in mine:
            cp.wait()

    return start, forward, finish


def _gather_sems(n):
    return [pltpu.SemaphoreType.DMA((n, 7)), pltpu.SemaphoreType.DMA((n, 7)), pltpu.SemaphoreType.DMA((n,))]


def _all_gather(shards, name):
    n = len(shards)
    carry = _Carry()
    _carry_gather(carry, shards, 1, 2)

    def body(*refs):
        step = jnp.int32(0)
        carry.starts(step, 1, refs[:n], refs[n:2 * n], refs[2 * n:])
        carry.finish(step, 1, refs[:n], refs[n:2 * n], refs[2 * n:])

    any_spec = pl.BlockSpec(memory_space=pl.ANY)
    return pl.pallas_call(
        body,
        in_specs=[any_spec] * n,
        out_specs=[any_spec] * n,
        out_shape=carry.out_shapes,
        scratch_shapes=carry.sem_shapes(),
        name=name,
    )(*shards)


def _reduce_scatter(grads, small, name):
    n, ns = len(grads), len(small)
    shapes = [g.shape[1:] for g in grads]

    def body(*refs):
        g = refs[:n]
        out = refs[n + ns:2 * n + ns]
        scr = refs[2 * (n + ns):]
        own, loc, r1, r2 = scr[:n], scr[n:2 * n], scr[2 * n:3 * n], scr[3 * n:4 * n]
        load_sems, s1, q1, s2, q2 = scr[4 * n:4 * n + 5]
        gather_start = gather_forward = gather_finish = lambda: None
        if ns:
            gather_start, gather_forward, gather_finish = _gather_steps(
                refs[n:n + ns], refs[2 * n + ns:2 * (n + ns)], *scr[4 * n + 5:])
        me, sib, chips = _place()
        c = me[2]

        gather_start()
        loads = []
        sends = []
        for a in range(n):
            ld = [pltpu.make_async_copy(g[a].at[_block(*chip, c)], loc[a].at[j], load_sems.at[a, j])
                  for j, chip in enumerate(chips)]
            ld.append(pltpu.make_async_copy(g[a].at[_block(*me)], own[a], load_sems.at[a, 3]))
            for cp in ld:
                cp.start()
            loads.append(ld)
            blocks = [(*chip, 1 - c) for chip in chips] + [sib]
            for j, blk in enumerate(blocks):
                cp = pltpu.make_async_remote_copy(
                    src_ref=g[a].at[_block(*blk)], dst_ref=r1[a].at[j],
                    send_sem=s1.at[a, j], recv_sem=q1.at[a, j], device_id=sib, device_id_type=MESH)
                cp.start()
                sends.append(cp)

        def from_sibling(a, j):
            return pltpu.make_async_remote_copy(
                src_ref=r1[a].at[j], dst_ref=r1[a].at[j], send_sem=s1.at[a, j], recv_sem=q1.at[a, j],
                device_id=sib, device_id_type=MESH)

        def partial(a, j, chip):
            return pltpu.make_async_remote_copy(
                src_ref=loc[a].at[j], dst_ref=r2[a].at[j], send_sem=s2.at[a, j], recv_sem=q2.at[a, j],
                device_id=(*chip, c), device_id_type=MESH)

        gather_forward()
        for a in range(n):
            for j, chip in enumerate(chips):
                loads[a][j].wait()
                from_sibling(a, j).wait_recv()
                loc[a][j] = (loc[a][j].astype(F32) + r1[a][j].astype(F32)).astype(BF16)
                cp = partial(a, j, chip)
                cp.start()
                sends.append(cp)
        for a in range(n):
            loads[a][3].wait()
            from_sibling(a, 3).wait_recv()
            acc = own[a][...].astype(F32) + r1[a][3].astype(F32)
            for j, chip in enumerate(chips):
                partial(a, j, chip).wait_recv()
                acc = acc + r2[a][j].astype(F32)
            out[a][...] = acc
        for cp in sends:
            cp.wait_send()
        gather_finish()

    any_spec = pl.BlockSpec(memory_space=pl.ANY)
    vmem_spec = pl.BlockSpec(memory_space=pltpu.VMEM)
    res = pl.pallas_call(
        body,
        in_specs=[any_spec] * (n + ns),
        out_specs=[vmem_spec] * n + [any_spec] * ns,
        out_shape=([jax.ShapeDtypeStruct(sh, F32) for sh in shapes]
                   + [jax.ShapeDtypeStruct((N_DEV,) + sm.shape, sm.dtype) for sm in small]),
        scratch_shapes=(
            [pltpu.VMEM(sh, BF16) for sh in shapes]
            + [pltpu.VMEM((3,) + sh, BF16) for sh in shapes]
            + [pltpu.VMEM((4,) + sh, BF16) for sh in shapes]
            + [pltpu.VMEM((3,) + sh, BF16) for sh in shapes]
            + [pltpu.SemaphoreType.DMA((n, 4)),
               pltpu.SemaphoreType.DMA((n, 4)), pltpu.SemaphoreType.DMA((n, 4)),
               pltpu.SemaphoreType.DMA((n, 3)), pltpu.SemaphoreType.DMA((n, 3))]
            + (_gather_sems(ns) if ns else [])
        ),
        compiler_params=pltpu.CompilerParams(vmem_limit_bytes=V7X_VMEM_LIMIT),
        name=name,
    )(*grads, *small)
    return res[:n], res[n:]


def _pair_add(grads, from_sib, blks, name):
    n = len(grads)

    def body(blk_ref, *refs):
        for a in range(n):
            refs[2 * n + a][...] = (refs[a][...].astype(F32) + refs[n + a][...].astype(F32)).astype(BF16)

    mine = [pl.BlockSpec((None,) + g.shape[1:], lambda j, b: (b[j], 0, 0)) for g in grads]
    same = [pl.BlockSpec((None,) + g.shape[1:], lambda j, b: (j, 0, 0)) for g in grads]
    return pl.pallas_call(
        body,
        grid_spec=pltpu.PrefetchScalarGridSpec(
            num_scalar_prefetch=1, grid=(4,), in_specs=mine + same, out_specs=same),
        out_shape=[jax.ShapeDtypeStruct((4,) + g.shape[1:], BF16) for g in grads],
        compiler_params=_params(("arbitrary",)),
        name=name,
    )(blks, *grads, *from_sib)


def _chip_sum(parts, from_chips, name):
    n = len(parts)

    def body(*refs):
        for a in range(n):
            acc = refs[a][...].astype(F32)
            for j in range(3):
                acc = acc + refs[n + a][j].astype(F32)
            refs[2 * n + a][...] = acc

    half = [p.shape[1] // 2 for p in parts]
    return pl.pallas_call(
        body,
        grid=(2,),
        in_specs=([pl.BlockSpec((None, h, p.shape[2]), lambda i: (3, i, 0)) for p, h in zip(parts, half)]
                  + [pl.BlockSpec((3, h, p.shape[2]), lambda i: (0, i, 0)) for p, h in zip(parts, half)]),
        out_specs=[pl.BlockSpec((h, p.shape[2]), lambda i: (i, 0)) for p, h in zip(parts, half)],
        out_shape=[jax.ShapeDtypeStruct(p.shape[1:], F32) for p in parts],
        compiler_params=_params(("arbitrary",)),
        name=name,
    )(*parts, *from_chips)


def _adam_math(w, g, m, v):
    nm = ADAM_B1 * m + (1.0 - ADAM_B1) * g
    nv = ADAM_B2 * v + (1.0 - ADAM_B2) * (g * g)
    m_hat = nm / (1.0 - ADAM_B1 ** ADAM_STEP)
    v_hat = nv / (1.0 - ADAM_B2 ** ADAM_STEP)
    return -ADAM_LR * (m_hat / (jnp.sqrt(v_hat) + ADAM_EPS) + ADAM_WD * w), nm, nv


def _sum_adamw(items, name, carry=None):
    n = len(items)

    def body(*refs):
        for a in range(n):
            p_ref, f_ref, w_ref, m_ref, v_ref = refs[5 * a:5 * a + 5]
            g_ref, d_ref, nm_ref, nv_ref = refs[5 * n + 4 * a:5 * n + 4 * a + 4]
            g = p_ref[...].astype(F32)
            for j in range(3):
                g = g + f_ref[j].astype(F32)
            g_ref[...] = g
            d_ref[...], nm_ref[...], nv_ref[...] = _adam_math(w_ref[...], g, m_ref[...], v_ref[...])

    in_specs, out_specs, out_shape, args = [], [], [], []
    for part, from_chips, w, m, v in items:
        r, c = w.shape
        spec = pl.BlockSpec((r // 2, c), lambda i: (i, 0))
        in_specs += [pl.BlockSpec((None, r // 2, c), lambda i: (3, i, 0)),
                     pl.BlockSpec((3, r // 2, c), lambda i: (0, i, 0)), spec, spec, spec]
        out_specs += [spec] * 4
        out_shape += [jax.ShapeDtypeStruct((r, c), F32)] * 4
        args += [part, from_chips, w, m, v]
    outs, carried = _pcall(body, grid=(2,), in_specs=in_specs, out_specs=out_specs, out_shape=out_shape,
                           scratch_shapes=[], name=name, args=tuple(args), carry=carry)
    return [tuple(outs[4 * a:4 * a + 4]) for a in range(n)], carried


def _small_update(gathered, layout, params, name):
    ng, npar = len(gathered), len(params)

    def body(*refs):
        parts = refs[:ng]
        prm = refs[ng:ng + 3 * npar]
        tot_refs = refs[ng + 3 * npar:2 * ng + 3 * npar]
        out = refs[2 * ng + 3 * npar:]
        tots = []
        for a in range(ng):
            acc = parts[a][0]
            for d in range(1, N_DEV):
                acc = acc + parts[a][d]
            tot_refs[a][...] = acc
            tots.append(acc)
        for i, (a, row, width) in enumerate(layout):
            g = tots[a] if row is None else tots[a][row:row + 1, :width]
            delta, nm, nv = _adam_math(prm[3 * i][...], g, prm[3 * i + 1][...], prm[3 * i + 2][...])
            out[4 * i][...] = g
            out[4 * i + 1][...] = delta
            out[4 * i + 2][...] = nm
            out[4 * i + 3][...] = nv

    flat = [t for prm in params for t in prm]
    res = pl.pallas_call(
        body,
        out_shape=([jax.ShapeDtypeStruct(g.shape[1:], F32) for g in gathered]
                   + [jax.ShapeDtypeStruct(prm[0].shape, F32) for prm in params for _ in range(4)]),
        compiler_params=pltpu.CompilerParams(vmem_limit_bytes=V7X_VMEM_LIMIT),
        name=name,
    )(*gathered, *flat)
    return res[:ng], [tuple(res[ng + 4 * i:ng + 4 * i + 4]) for i in range(npar)]


def _adamw(w, g, m, v, name):
    rows, cols = w.shape
    br = rows
    for cand in (512, 256, 128):
        if rows % cand == 0 and rows > cand:
            br = cand
            break

    def body(w_ref, g_ref, m_ref, v_ref, d_ref, nm_ref, nv_ref):
        d_ref[...], nm_ref[...], nv_ref[...] = _adam_math(w_ref[...], g_ref[...], m_ref[...], v_ref[...])

    spec = pl.BlockSpec((br, cols), lambda i: (i, 0))
    shape = jax.ShapeDtypeStruct((rows, cols), F32)
    return pl.pallas_call(
        body,
        grid=(rows // br,),
        in_specs=[spec] * 4,
        out_specs=[spec] * 3,
        out_shape=[shape] * 3,
        compiler_params=_params(("arbitrary",)),
        name=name,
    )(w, g, m, v)


def _by_device(full):
    return full.reshape(N_DEV, full.shape[0] // N_DEV, full.shape[1])


def kernel(x, p, g_mix, w_in, conv_w, conv_b, ln_g, ln_b, pool_w, pool_scale, w_out, g_ffn, w_gate_up, w_down, g_ple_gate, w_ple_gate, w_ple_up, g_ple_post, g_final, loss_target, m_g_mix, m_w_in, m_conv_w, m_conv_b, m_ln_g, m_ln_b, m_pool_w, m_pool_scale, m_w_out, m_g_ffn, m_w_gate_up, m_w_down, m_g_ple_gate, m_w_ple_gate, m_w_ple_up, m_g_ple_post, m_g_final, v_g_mix, v_w_in, v_conv_w, v_conv_b, v_ln_g, v_ln_b, v_pool_w, v_pool_scale, v_w_out, v_g_ffn, v_w_gate_up, v_w_down, v_g_ple_gate, v_w_ple_gate, v_w_ple_up, v_g_ple_post, v_g_final):
    seq = x.shape[1]
    xs = x[0]
    ps = p[0, 0]
    tgt = loss_target[0]
    ax, ay, ac = lax.axis_index("x"), lax.axis_index("y"), lax.axis_index("c")
    me = _block(ax, ay, ac)
    blks = jnp.stack([_block(1 - ax, ay, ac), _block(ax, 1 - ay, ac), _block(1 - ax, 1 - ay, ac), me]).astype(jnp.int32)
    rows = lambda gth: gth.reshape((-1,) + gth.shape[2:])

    carry = _Carry()
    _carry_gather(carry, [
        w_in[0].T.astype(BF16),
        w_out[0].astype(BF16),
        jnp.pad(conv_w[0].T, ((0, 0), (0, CONV_HALO - CONV_K))),
    ], 8, 15)
    (h1,), first = _norm_in(xs, g_mix, min(256, seq), carry)
    w_in_t, w_out_f, conv_w_t = [rows(gth) for gth in first]
    conv_w_f = conv_w_t.T
    pool_w_b = pool_w[0].astype(BF16)

    carry = _Carry()
    _carry_gather(carry, [w_gate_up[0].T.astype(BF16)], 8, 12)
    (z, u1, pooled, x1, mix), (w_gu_all,) = _fwd_mix(
        xs, h1, w_in_t, w_out_f, conv_w_f, conv_b, ln_g, ln_b, pool_w_b, pool_scale, min(256, seq), carry)
    w_gu_t = rows(w_gu_all)

    carry = _Carry()
    _carry_gather(carry, [w_down[0].astype(BF16),
                          w_ple_gate[0].astype(BF16),
                          w_ple_up[0].T.astype(BF16)], 6, 10)
    (h2, g_sav, u_sav, act), late = _ffn_up(x1, w_gu_t, g_ffn, min(256, seq), carry)
    w_down_f, w_pg_f, w_pu_t = [rows(gth) for gth in late]
    dx2, dx2b, hg, ds, dpe, pb, stats_ple = _down_ple(
        x1, act, w_down_f, ps, tgt, w_pg_f, w_pu_t, g_ple_gate, g_ple_post, g_final.reshape(1, D_MODEL),
        min(256, seq))
    dg, du, dx1, dx1b, stats_ffn = _ffn_bwd(dx2, dx2b, x1, g_sav, u_sav, w_gu_t, w_down_f, g_ffn,
                                            min(256, seq))[0]

    d_w_pg, _ = _grad_matmul(hg, ds, 256, "grad_w_ple_gate")
    d_w_pu_t, _ = _grad_matmul(dpe, pb, 256, "grad_w_ple_up")
    d_w_gu_t, _ = _grad_matmul(dg, h2, 256, "grad_w_gate_up", a2=du)
    d_w_out, _ = _grad_matmul(mix, dx1b, 256, "grad_w_out")
    early = [_by_device(d_w_pg), _by_device(d_w_pu_t), _by_device(d_w_gu_t), _by_device(d_w_out)]
    carry = _Carry()
    _carry_pair(carry, early)
    d_w_down, from_sib = _grad_matmul(act, dx2b, 256, "grad_w_down", carry=carry)
    early_parts = _pair_add(early, from_sib, blks, "pair_add_early")

    carry = _Carry()
    _carry_chip(carry, early_parts)
    _carry_pair(carry, [_by_device(d_w_down)])
    (dx, dz, vec_mix, dconv_w_part, dpool_w_part), carried = _bwd_mix(
        dx1, dx1b, xs, z, u1, pooled, w_in_t, w_out_f, g_mix, conv_w_f, ln_g, ln_b, pool_w_b, pool_scale,
        min(256, seq), carry)
    early_chips, down_sib = carried[:4], carried[4:]
    down_parts = _pair_add([_by_device(d_w_down)], down_sib, blks, "pair_add_down")

    carry = _Carry()
    _carry_chip(carry, down_parts)
    d_w_in_t, down_chips = _grad_matmul(dz, h1, 256, "grad_w_in", carry=carry)
    (gr_w_in_t,), small = _reduce_scatter(
        [_by_device(d_w_in_t)], [vec_mix, stats_ple, stats_ffn, dconv_w_part, dpool_w_part], "scatter_last")

    vec_names = ["g_mix", "ln_g", "ln_b", "conv_b", "pool_scale", "g_final", "g_ple_post", "g_ple_gate", "g_ffn",
                 "pool_w"]
    layout = [(0, 0, D_MODEL), (0, 1, C_CONV), (0, 2, C_CONV), (0, 3, C_CONV), (0, 4, C_POOL),
              (1, 0, D_MODEL), (1, 1, D_MODEL), (1, 2, D_MODEL), (2, 0, D_MODEL), (4, None, None)]
    as_row = lambda t: t.reshape(1, D_MODEL)
    params = [(g_mix, m_g_mix, v_g_mix), (ln_g, m_ln_g, v_ln_g), (ln_b, m_ln_b, v_ln_b),
              (conv_b, m_conv_b, v_conv_b), (pool_scale, m_pool_scale, v_pool_scale),
              (as_row(g_final), as_row(m_g_final), as_row(v_g_final)),
              (g_ple_post, m_g_ple_post, v_g_ple_post), (g_ple_gate, m_g_ple_gate, v_g_ple_gate),
              (g_ffn, m_g_ffn, v_g_ffn), (pool_w[0], m_pool_w[0], v_pool_w[0])]
    tots, small_upd = _small_update(small, layout, params, "small_update")
    loss = tots[1][3, 0]
    upd = {}
    for nm, res, prm in zip(vec_names, small_upd, [g_mix, ln_g, ln_b, conv_b, pool_scale, g_final, g_ple_post,
                                                    g_ple_gate, g_ffn, pool_w]):
        upd[nm] = tuple(t.reshape(prm.shape) for t in res)
    gr_conv_w = lax.dynamic_slice_in_dim(tots[3][:CONV_K], me * (C_CONV // N_DEV), C_CONV // N_DEV, axis=1)

    natural = lambda t: t[None]
    turned = lambda t: t.T[None]
    (res_pg, res_gu, res_out, res_down), _ = _sum_adamw([
        (early_parts[0], early_chips[0], w_ple_gate[0], m_w_ple_gate[0], v_w_ple_gate[0]),
        (early_parts[2], early_chips[2], w_gate_up[0].T, m_w_gate_up[0].T, v_w_gate_up[0].T),
        (early_parts[3], early_chips[3], w_out[0], m_w_out[0], v_w_out[0]),
        (down_parts[0], down_chips[0], w_down[0], m_w_down[0], v_w_down[0])], "adamw_big")
    upd["w_ple_gate"] = tuple(natural(t) for t in res_pg)
    upd["w_gate_up"] = tuple(turned(t) for t in res_gu)
    upd["w_out"] = tuple(natural(t) for t in res_out)
    upd["w_down"] = tuple(natural(t) for t in res_down)
    (gr_w_pu_t,) = _chip_sum(early_parts[1:2], early_chips[1:2], "chip_sum")
    plain = [
        ("w_in", w_in[0].T, gr_w_in_t, m_w_in[0].T, v_w_in[0].T, True),
        ("w_ple_up", w_ple_up[0], gr_w_pu_t.T, m_w_ple_up[0], v_w_ple_up[0], False),
        ("conv_w", conv_w[0], gr_conv_w, m_conv_w[0], v_conv_w[0], False),
    ]
    for nm, w_, g_, m_, v_, transposed in plain:
        res = (g_,) + tuple(_adamw(w_, g_, m_, v_, "adamw_" + nm))
        upd[nm] = tuple((t.T if transposed else t)[None] for t in res)

    order = ["g_mix", "w_in", "conv_w", "conv_b", "ln_g", "ln_b", "pool_w", "pool_scale", "w_out", "g_ffn",
             "w_gate_up", "w_down", "g_ple_gate", "w_ple_gate", "w_ple_up", "g_ple_post", "g_final"]
    outs = [loss, dx[None]]
    for k in range(4):
        outs += [upd[nm][k] for nm in order]
    return tuple(outs)
```

```python
import functools

import jax
import jax.numpy as jnp
from jax import lax
from jax.experimental import pallas as pl
from jax.experimental.pallas import tpu as pltpu

D_MODEL = 1024
C_CONV = 512
C_POOL = 512
Z_WIDTH = 2 * C_CONV + C_POOL
POOL_WINDOWS = (2, 4, 8, 16)
POOL_GROUP = 128
CONV_K = 31
D_FF = 2816
D_PLE = 256
EPS = 1e-6
N_DEV = 8

ADAM_LR = 0.001
ADAM_B1 = 0.9
ADAM_B2 = 0.999
ADAM_EPS = 1e-08
ADAM_WD = 0.01
ADAM_STEP = 10

CONV_HALO = 32
POOL_HALO = 16
ROW_CHUNK = 32
V7X_VMEM_LIMIT = 56 * 1024 * 1024
FF_CHUNK = D_FF // 2

BF16 = jnp.bfloat16
F32 = jnp.float32
MESH = pl.DeviceIdType.MESH


def _dot(a, b):
    return lax.dot_general(a, b, (((1,), (0,)), ((), ())), preferred_element_type=F32)


def _dot_nt(a, b):
    return lax.dot_general(a, b, (((1,), (1,)), ((), ())), preferred_element_type=F32)


def _dot_tn(a, b):
    return lax.dot_general(a, b, (((0,), (0,)), ((), ())), preferred_element_type=F32)


def _rms_fwd(x, g):
    r = lax.rsqrt(jnp.mean(x * x, axis=-1, keepdims=True) + EPS)
    return x * r * g, r


def _rms_bwd(x, r, g, dy):
    xr = x * r
    dg = jnp.sum(dy * xr, axis=0, keepdims=True)
    dyg = dy * g
    dx = r * (dyg - xr * jnp.mean(dyg * xr, axis=-1, keepdims=True))
    return dx, dg


def _sigmoid(x):
    return jax.nn.sigmoid(x)


def _params(sem=None):
    return pltpu.CompilerParams(dimension_semantics=sem, vmem_limit_bytes=V7X_VMEM_LIMIT)


def _place():
    x, y, c = lax.axis_index("x"), lax.axis_index("y"), lax.axis_index("c")
    chips = [(1 - x, y), (x, 1 - y), (1 - x, 1 - y)]
    return (x, y, c), (x, y, 1 - c), chips


def _block(px, py, pc):
    return 4 * px + 2 * py + pc


class _Carry:
    def __init__(self):
        self.inputs = []
        self.out_shapes = []
        self.copies = []
        self.locals = []

    def add_input(self, arr):
        self.inputs.append(arr)
        return len(self.inputs) - 1

    def add_output(self, shape, dtype):
        self.out_shapes.append(jax.ShapeDtypeStruct(shape, dtype))
        return len(self.out_shapes) - 1

    def local(self, src_idx, dst_idx, dst_blk):
        self.locals.append((src_idx, dst_idx, dst_blk))

    def copy(self, src, dst_idx, dst_blk, got_blk, peer, step=0, after=()):
        self.copies.append(dict(src=src, dst_idx=dst_idx, dst_blk=dst_blk, got_blk=got_blk, peer=peer, step=step,
                                after=tuple(after)))
        return len(self.copies) - 1

    def sem_shapes(self):
        return [pltpu.SemaphoreType.DMA((max(1, len(self.copies)),)),
                pltpu.SemaphoreType.DMA((max(1, len(self.copies)),)),
                pltpu.SemaphoreType.DMA((max(1, len(self.locals)),))]

    @staticmethod
    def _view(ref, where):
        if isinstance(where, tuple):
            blk, row0, nrows = where
            return ref.at[blk, pl.ds(row0, nrows)]
        return ref.at[where]

    def _desc(self, k, ins, outs, sems, place):
        cp = self.copies[k]
        me, sib, chips = place
        kind, idx, blk = cp["src"]
        src = (ins if kind == "in" else outs)[idx]
        if blk is not None:
            src = self._view(src, blk(*place))
        to = sib if cp["peer"] == "sib" else (*chips[cp["peer"]], me[2])
        return pltpu.make_async_remote_copy(
            src_ref=src, dst_ref=self._view(outs[cp["dst_idx"]], cp["dst_blk"](*place)),
            send_sem=sems[0].at[k], recv_sem=sems[1].at[k], device_id=to, device_id_type=MESH)

    def _arrival(self, k, outs, sems, place):
        cp = self.copies[k]
        got = self._view(outs[cp["dst_idx"]], cp["got_blk"](*place))
        return pltpu.make_async_remote_copy(
            src_ref=got, dst_ref=got, send_sem=sems[0].at[k], recv_sem=sems[1].at[k],
            device_id=place[0], device_id_type=MESH)

    def _local(self, n, ins, outs, sems, place):
        src_idx, dst_idx, blk = self.locals[n]
        return pltpu.make_async_copy(ins[src_idx], outs[dst_idx].at[blk(*place)], sems[2].at[n])

    def stages(self):
        return sorted({0} | {cp["step"] for cp in self.copies})

    def stage(self, s, ins, outs, sems):
        place = _place()
        if s == 0:
            self._waited = set()
            for n in range(len(self.locals)):
                self._local(n, ins, outs, sems, place).start()
        for k, cp in enumerate(self.copies):
            if cp["step"] != s:
                continue
            for a in cp["after"]:
                if a not in self._waited:
                    self._arrival(a, outs, sems, place).wait_recv()
                    self._waited.add(a)
            self._desc(k, ins, outs, sems, place).start()

    def drain(self, ins, outs, sems):
        place = _place()
        for k in range(len(self.copies)):
            if k not in self._waited:
                self._arrival(k, outs, sems, place).wait_recv()
        for k in range(len(self.copies)):
            self._desc(k, ins, outs, sems, place).wait_send()
        for n in range(len(self.locals)):
            self._local(n, ins, outs, sems, place).wait()

    def starts(self, step, nsteps, ins, outs, sems):
        for s in self.stages():
            pl.when(step == min(s, nsteps - 1))(functools.partial(self.stage, s, ins, outs, sems))

    def finish(self, step, nsteps, ins, outs, sems):
        pl.when(step == nsteps - 1)(functools.partial(self.drain, ins, outs, sems))


def _const_blk(j):
    return lambda me, sib, chips: j


def _carry_gather(carry, shards, relay_step, last_step):
    outs = []
    for sh in shards:
        i = carry.add_input(sh)
        o = carry.add_output((N_DEV,) + sh.shape, sh.dtype)
        half = sh.shape[0] // 2
        tile = 16 if sh.dtype == BF16 else 8
        split = half % tile == 0
        rows = [(0, half), (half, sh.shape[0] - half)] if split else [(0, sh.shape[0]), None]

        def whole(j, core):
            return lambda me, sib, chips, j=j, core=core: _block(*chips[j], me[2] if core == 0 else 1 - me[2])

        def part(j, core, h, rows=rows):
            return lambda me, sib, chips: (_block(*chips[j], me[2] if core == 0 else 1 - me[2]),) + rows[h]

        mine = lambda me, sib, chips: _block(*me)
        carry.local(i, o, mine)
        carry.copy(("in", i, None), o, mine, lambda me, sib, chips: _block(*sib), "sib")
        near = [carry.copy(("in", i, None), o, mine, whole(j, 0), j) for j in range(2)]
        for j in range(2):
            carry.copy(("out", o, whole(j, 0)), o, whole(j, 0), whole(j, 1), "sib", step=relay_step, after=(near[j],))
        for j in range(2):
            if rows[j] is None:
                continue
            far = carry.copy(("out", o, part(j, 0, j)), o, part(j, 0, j), part(2, 0, j), 1 - j,
                             step=relay_step, after=(near[j],))
            carry.copy(("out", o, part(2, 0, j)), o, part(2, 0, j), part(2, 1, j), "sib", step=last_step, after=(far,))
        outs.append(o)
    return outs


def _carry_pair(carry, grads):
    outs = []
    for g in grads:
        i = carry.add_input(g)
        o = carry.add_output((4,) + g.shape[1:], g.dtype)
        for j in range(4):
            if j < 3:
                blk = lambda me, sib, chips, j=j: _block(*chips[j], 1 - me[2])
            else:
                blk = lambda me, sib, chips: _block(*sib)
            carry.copy(("in", i, blk), o, _const_blk(j), _const_blk(j), "sib")
        outs.append(o)
    return outs


def _carry_chip(carry, parts):
    outs = []
    for p in parts:
        i = carry.add_input(p)
        o = carry.add_output((3,) + p.shape[1:], p.dtype)
        for j in range(3):
            carry.copy(("in", i, _const_blk(j)), o, _const_blk(j), _const_blk(j), j)
        outs.append(o)
    return outs


def _pcall(body, *, grid, in_specs, out_specs, out_shape, scratch_shapes, name, args, carry=None):
    sem = ("arbitrary",) * len(grid)
    if carry is None:
        res = pl.pallas_call(body, grid=grid, in_specs=in_specs, out_specs=out_specs, out_shape=out_shape,
                             scratch_shapes=scratch_shapes, compiler_params=_params(sem), name=name)(*args)
        return list(res), []
    n_in, n_out, n_scr = len(in_specs), len(out_specs), len(scratch_shapes)
    c_in, c_out = len(carry.inputs), len(carry.out_shapes)
    nsteps = 1
    for extent in grid:
        nsteps *= extent

    def wrapped(*refs):
        ins = refs[:n_in]
        cins = refs[n_in:n_in + c_in]
        o0 = n_in + c_in
        outs = refs[o0:o0 + n_out]
        couts = refs[o0 + n_out:o0 + n_out + c_out]
        s0 = o0 + n_out + c_out
        scr = refs[s0:s0 + n_scr]
        sems = refs[s0 + n_scr:]
        step = pl.program_id(0)
        for d in range(1, len(grid)):
            step = step * grid[d] + pl.program_id(d)
        carry.starts(step, nsteps, cins, couts, sems)
        body(*ins, *outs, *scr)
        carry.finish(step, nsteps, cins, couts, sems)

    any_spec = pl.BlockSpec(memory_space=pl.ANY)
    res = pl.pallas_call(
        wrapped, grid=grid,
        in_specs=list(in_specs) + [any_spec] * c_in,
        out_specs=list(out_specs) + [any_spec] * c_out,
        out_shape=list(out_shape) + carry.out_shapes,
        scratch_shapes=list(scratch_shapes) + carry.sem_shapes(),
        compiler_params=_params(sem), name=name)(*args, *carry.inputs)
    return list(res[:n_out]), list(res[n_out:])


def _shifted_copies(buf, shifted, tm):
    span = tm + CONV_HALO - 8
    for r in range(1, 8):
        shifted[r - 1, 0:span, :] = buf[r:r + span, :]


def _rows_at(buf, shifted, start):
    aligned, r = (start // 8) * 8, start % 8
    if r == 0:
        return buf[aligned:aligned + ROW_CHUNK, :]
    return shifted[r - 1, aligned:aligned + ROW_CHUNK, :]


def _row_spec(tm, width):
    return pl.BlockSpec((tm, width), lambda i: (i, 0))


def _const_spec(shape):
    return pl.BlockSpec(shape, lambda i: (0,) * len(shape))


def _weight_spec(shape):
    return pl.BlockSpec(shape, lambda i: (0,) * len(shape), pipeline_mode=pl.Buffered(1))


def _norm_in(x, g_mix, tm, carry=None):
    s = x.shape[0]

    def body(x_ref, gmix_ref, h1_ref):
        h, _ = _rms_fwd(x_ref[...], gmix_ref[...])
        h1_ref[...] = h.astype(BF16)

    return _pcall(
        body,
        grid=(s // tm,),
        in_specs=[_row_spec(tm, D_MODEL), _const_spec((1, D_MODEL))],
        out_specs=[_row_spec(tm, D_MODEL)],
        out_shape=[jax.ShapeDtypeStruct((s, D_MODEL), BF16)],
        scratch_shapes=[],
        name="norm_in",
        args=(x, g_mix),
        carry=carry,
    )


def _fwd_mix(x, h1, w_in_t, w_out, conv_w, conv_b, ln_g, ln_b, pool_w, pool_scale, tm, carry=None):
    s = x.shape[0]
    nt = s // tm

    def body(x_ref, h1_ref, win_ref, wout_ref, cw_ref, cb_ref, lng_ref, lnb_ref, pw_ref, ps_ref,
             z_ref, u1_ref, pooled_ref, x1_ref, mix_ref, ubuf, vbuf, ush):
        i = pl.program_id(0)

        @pl.when(i == 0)
        def _():
            ubuf[0:CONV_HALO, :] = jnp.zeros((CONV_HALO, C_CONV), F32)
            vbuf[0:POOL_HALO, :] = jnp.zeros((POOL_HALO, C_POOL), F32)

        xt = x_ref[...]
        z = _dot_nt(h1_ref[...], win_ref[...])
        z_ref[...] = z
        a = z[:, :C_CONV]
        b = z[:, C_CONV:2 * C_CONV]
        v = z[:, 2 * C_CONV:]
        ubuf[CONV_HALO:CONV_HALO + tm, :] = a * _sigmoid(b)
        vbuf[POOL_HALO:POOL_HALO + tm, :] = v

        _shifted_copies(ubuf, ush, tm)
        for rc in range(tm // ROW_CHUNK):
            base = rc * ROW_CHUNK + CONV_HALO - (CONV_K - 1)
            acc = jnp.broadcast_to(cb_ref[...], (ROW_CHUNK, C_CONV))
            for k in range(CONV_K):
                acc = acc + cw_ref[k:k + 1, :] * _rows_at(ubuf, ush, base + k)
            u1_ref[rc * ROW_CHUNK:(rc + 1) * ROW_CHUNK, :] = acc

        u1 = u1_ref[...]
        mu = jnp.mean(u1, axis=-1, keepdims=True)
        cen = u1 - mu
        rstd = lax.rsqrt(jnp.mean(cen * cen, axis=-1, keepdims=True) + EPS)
        u2 = cen * rstd * lng_ref[...] + lnb_ref[...]
        u = u2 * _sigmoid(u2)

        pos1 = (i * tm + lax.broadcasted_iota(jnp.int32, (tm, 1), 0) + 1).astype(F32)
        parts = [u]
        for g, w in enumerate(POOL_WINDOWS):
            cols = slice(g * POOL_GROUP, (g + 1) * POOL_GROUP)
            vg = v[:, cols]
            tot = vg
            for j in range(1, w):
                tot = tot + vbuf[POOL_HALO - j:POOL_HALO - j + tm, cols]
            pooled = tot / jnp.minimum(pos1, float(w)) - vg
            pooled_b = pooled.astype(BF16)
            pooled_ref[:, cols] = pooled_b
            parts.append(_dot(pooled_b, pw_ref[g]) * ps_ref[:, cols])
        mix = jnp.concatenate(parts, axis=-1).astype(BF16)
        mix_ref[...] = mix
        x1_ref[...] = xt + _dot(mix, wout_ref[...])

        ubuf[0:CONV_HALO, :] = ubuf[tm:tm + CONV_HALO, :]
        vbuf[0:POOL_HALO, :] = vbuf[tm:tm + POOL_HALO, :]

    return _pcall(
        body,
        grid=(nt,),
        in_specs=[
            _row_spec(tm, D_MODEL),
            _row_spec(tm, D_MODEL),
            _const_spec((Z_WIDTH, D_MODEL)),
            _const_spec((D_MODEL, D_MODEL)),
            _const_spec((CONV_HALO, C_CONV)),
            _const_spec((1, C_CONV)),
            _const_spec((1, C_CONV)),
            _const_spec((1, C_CONV)),
            _const_spec((len(POOL_WINDOWS), POOL_GROUP, POOL_GROUP)),
            _const_spec((1, C_POOL)),
        ],
        out_specs=[
            _row_spec(tm, Z_WIDTH),
            _row_spec(tm, C_CONV),
            _row_spec(tm, C_POOL),
            _row_spec(tm, D_MODEL),
            _row_spec(tm, D_MODEL),
        ],
        out_shape=[
            jax.ShapeDtypeStruct((s, Z_WIDTH), F32),
            jax.ShapeDtypeStruct((s, C_CONV), F32),
            jax.ShapeDtypeStruct((s, C_POOL), BF16),
            jax.ShapeDtypeStruct((s, D_MODEL), F32),
            jax.ShapeDtypeStruct((s, D_MODEL), BF16),
        ],
        scratch_shapes=[
            pltpu.VMEM((tm + CONV_HALO, C_CONV), F32),
            pltpu.VMEM((tm + POOL_HALO, C_POOL), F32),
            pltpu.VMEM((7, tm + CONV_HALO, C_CONV), F32),
        ],
        name="fwd_mix",
        args=(x, h1, w_in_t, w_out, conv_w, conv_b, ln_g, ln_b, pool_w, pool_scale),
        carry=carry,
    )


def _ffn_up(x1, w_gu_t, g_ffn, tm, carry=None):
    s = x1.shape[0]

    def body(x1_ref, w_ref, gffn_ref, h2_ref, g_ref, u_ref, act_ref):
        h, _ = _rms_fwd(x1_ref[...], gffn_ref[...])
        h2 = h.astype(BF16)
        h2_ref[...] = h2
        for c in range(D_FF // FF_CHUNK):
            cols = slice(c * FF_CHUNK, (c + 1) * FF_CHUNK)
            g = _dot_nt(h2, w_ref[c * FF_CHUNK:(c + 1) * FF_CHUNK, :])
            u = _dot_nt(h2, w_ref[D_FF + c * FF_CHUNK:D_FF + (c + 1) * FF_CHUNK, :])
            g_ref[:, cols] = g.astype(BF16)
            u_ref[:, cols] = u.astype(BF16)
            act_ref[:, cols] = (g * _sigmoid(g) * u).astype(BF16)

    return _pcall(
        body,
        grid=(s // tm,),
        in_specs=[_row_spec(tm, D_MODEL), _weight_spec((2 * D_FF, D_MODEL)), _const_spec((1, D_MODEL))],
        out_specs=[_row_spec(tm, D_MODEL), _row_spec(tm, D_FF), _row_spec(tm, D_FF), _row_spec(tm, D_FF)],
        out_shape=[
            jax.ShapeDtypeStruct((s, D_MODEL), BF16),
            jax.ShapeDtypeStruct((s, D_FF), BF16),
            jax.ShapeDtypeStruct((s, D_FF), BF16),
            jax.ShapeDtypeStruct((s, D_FF), BF16),
        ],
        scratch_shapes=[],
        name="ffn_up",
        args=(x1, w_gu_t, g_ffn),
        carry=carry,
    )


def _down_ple(x1, act, w_down, p, tgt, w_pg, w_pu_t, g_gate, g_post, g_final, tm):
    s = x1.shape[0]
    nt = s // tm

    def body(x1_ref, act_ref, wd_ref, p_ref, t_ref, wpg_ref, wpu_ref, gg_ref, gp_ref, gf_ref,
             dx2_ref, dx2b_ref, hg_ref, ds_ref, dpe_ref, pb_ref, stats_ref, x2_cur, x2_next):
        i = pl.program_id(0)

        @pl.when(i == 0)
        def _():
            stats_ref[...] = jnp.zeros_like(stats_ref)
            x2_cur[...] = jnp.zeros((tm, D_MODEL), F32)

        def down(c):
            cols = slice(c * 256, (c + 1) * 256)
            x2_next[:, cols] = x1_ref[:, cols] + _dot(act_ref[...], wd_ref[:, cols])

        x2 = x2_cur[...]
        counts = i >= 1

        hg, rg = _rms_fwd(x2, gg_ref[...])
        hg_b = hg.astype(BF16)
        hg_ref[...] = hg_b
        down(0)
        gate = _sigmoid(_dot(hg_b, wpg_ref[...]))
        pb = p_ref[...].astype(BF16)
        pb_ref[...] = pb
        pe = _dot_nt(pb, wpu_ref[...])
        e, rp = _rms_fwd(pe, gp_ref[...])
        x3 = x2 + gate * e
        down(1)
        y, r3 = _rms_fwd(x3, gf_ref[...])
        diff = y - t_ref[...]
        loss = 0.5 * jnp.sum(jnp.sum(diff * diff, axis=-1, keepdims=True), axis=0, keepdims=True) / D_MODEL
        dy = diff * (1.0 / D_MODEL)

        dx3, dgf = _rms_bwd(x3, r3, gf_ref[...], dy)
        down(2)
        dpe, dgp = _rms_bwd(pe, rp, gp_ref[...], dx3 * gate)
        dpe_ref[...] = dpe.astype(BF16)
        ds = (dx3 * e * gate * (1.0 - gate)).astype(BF16)
        ds_ref[...] = ds
        dhg = _dot_nt(ds, wpg_ref[...])
        down(3)
        dxg, dgg = _rms_bwd(x2, rg, gg_ref[...], dhg)
        dx2 = dx3 + dxg
        dx2_ref[...] = dx2
        dx2b_ref[...] = dx2.astype(BF16)
        x2_cur[...] = x2_next[...]

        stats_ref[0:1, :] += jnp.where(counts, dgf, 0.0)
        stats_ref[1:2, :] += jnp.where(counts, dgp, 0.0)
        stats_ref[2:3, :] += jnp.where(counts, dgg, 0.0)
        stats_ref[3:4, :] += jnp.where(counts, jnp.broadcast_to(loss, (1, D_MODEL)), 0.0)

    ahead = lambda width: pl.BlockSpec((tm, width), lambda i: (jnp.minimum(i, nt - 1), 0))
    behind = lambda width: pl.BlockSpec((tm, width), lambda i: (jnp.maximum(i - 1, 0), 0))
    return pl.pallas_call(
        body,
        grid=(nt + 1,),
        in_specs=[
            ahead(D_MODEL),
            ahead(D_FF),
            _weight_spec((D_FF, D_MODEL)),
            behind(D_PLE),
            behind(D_MODEL),
            _weight_spec((D_MODEL, D_MODEL)),
            _weight_spec((D_MODEL, D_PLE)),
            _const_spec((1, D_MODEL)),
            _const_spec((1, D_MODEL)),
            _const_spec((1, D_MODEL)),
        ],
        out_specs=[
            behind(D_MODEL),
            behind(D_MODEL),
            behind(D_MODEL),
            behind(D_MODEL),
            behind(D_MODEL),
            behind(D_PLE),
            _const_spec((8, D_MODEL)),
        ],
        out_shape=[
            jax.ShapeDtypeStruct((s, D_MODEL), F32),
            jax.ShapeDtypeStruct((s, D_MODEL), BF16),
            jax.ShapeDtypeStruct((s, D_MODEL), BF16),
            jax.ShapeDtypeStruct((s, D_MODEL), BF16),
            jax.ShapeDtypeStruct((s, D_MODEL), BF16),
            jax.ShapeDtypeStruct((s, D_PLE), BF16),
            jax.ShapeDtypeStruct((8, D_MODEL), F32),
        ],
        scratch_shapes=[pltpu.VMEM((tm, D_MODEL), F32), pltpu.VMEM((tm, D_MODEL), F32)],
        compiler_params=_params(("arbitrary",)),
        name="down_ple",
    )(x1, act, w_down, p, tgt, w_pg, w_pu_t, g_gate, g_post, g_final)


def _ffn_bwd(dx2, dx2b, x1, g_sav, u_sav, w_gu_t, w_down, g_ffn, tm, carry=None):
    s = x1.shape[0]

    def body(dx2_ref, dx2b_ref, x1_ref, g_ref, u_ref, w_ref, wd_ref, gffn_ref,
             dg_ref, du_ref, dx1_ref, dx1b_ref, stats_ref):
        @pl.when(pl.program_id(0) == 0)
        def _():
            stats_ref[...] = jnp.zeros_like(stats_ref)

        dx2b = dx2b_ref[...]
        nc = D_FF // FF_CHUNK
        dacts = [_dot_nt(dx2b, wd_ref[c * FF_CHUNK:(c + 1) * FF_CHUNK, :]) for c in range(nc)]
        dh2 = jnp.zeros((tm, D_MODEL), F32)
        for c in range(nc):
            cols = slice(c * FF_CHUNK, (c + 1) * FF_CHUNK)
            g = g_ref[:, cols].astype(F32)
            u = u_ref[:, cols].astype(F32)
            sg = _sigmoid(g)
            dg = (dacts[c] * u * sg * (1.0 + g * (1.0 - sg))).astype(BF16)
            du = (dacts[c] * g * sg).astype(BF16)
            dg_ref[:, cols] = dg
            du_ref[:, cols] = du
            dh2 = dh2 + _dot(dg, w_ref[c * FF_CHUNK:(c + 1) * FF_CHUNK, :])
            dh2 = dh2 + _dot(du, w_ref[D_FF + c * FF_CHUNK:D_FF + (c + 1) * FF_CHUNK, :])

        x1 = x1_ref[...]
        r2 = lax.rsqrt(jnp.mean(x1 * x1, axis=-1, keepdims=True) + EPS)
        dxn, dgf = _rms_bwd(x1, r2, gffn_ref[...], dh2)
        dx1 = dx2_ref[...] + dxn
        dx1_ref[...] = dx1
        dx1b_ref[...] = dx1.astype(BF16)
        stats_ref[0:1, :] += dgf

    return _pcall(
        body,
        grid=(s // tm,),
        in_specs=[
            _row_spec(tm, D_MODEL), _row_spec(tm, D_MODEL), _row_spec(tm, D_MODEL),
            _row_spec(tm, D_FF), _row_spec(tm, D_FF),
            _weight_spec((2 * D_FF, D_MODEL)), _weight_spec((D_FF, D_MODEL)), _const_spec((1, D_MODEL)),
        ],
        out_specs=[_row_spec(tm, D_FF), _row_spec(tm, D_FF), _row_spec(tm, D_MODEL), _row_spec(tm, D_MODEL),
                   _const_spec((8, D_MODEL))],
        out_shape=[
            jax.ShapeDtypeStruct((s, D_FF), BF16),
            jax.ShapeDtypeStruct((s, D_FF), BF16),
            jax.ShapeDtypeStruct((s, D_MODEL), F32),
            jax.ShapeDtypeStruct((s, D_MODEL), BF16),
            jax.ShapeDtypeStruct((8, D_MODEL), F32),
        ],
        scratch_shapes=[],
        name="ffn_bwd",
        args=(dx2, dx2b, x1, g_sav, u_sav, w_gu_t, w_down, g_ffn),
        carry=carry,
    )


def _bwd_mix(dx1, dx1b, x, z, u1, pooled, w_in_t, w_out, g_mix, conv_w, ln_g, ln_b, pool_w, pool_scale, tm,
             carry=None):
    s = x.shape[0]
    nt = s // tm

    def body(dx1_ref, dx1b_ref, x_ref, z_ref, u1_ref, pooled_ref, win_ref, wout_ref, gmix_ref, cw_ref,
             lng_ref, lnb_ref, pw_ref, ps_ref,
             dx_ref, dz_ref, vec_ref, dcw_ref, dpw_ref, dubuf, dvbuf, u0buf, du0buf, dush):
        i = pl.program_id(0)
        tile = nt - 1 - i

        @pl.when(i == 0)
        def _():
            vec_ref[...] = jnp.zeros_like(vec_ref)
            dcw_ref[...] = jnp.zeros_like(dcw_ref)
            dpw_ref[...] = jnp.zeros_like(dpw_ref)
            dubuf[tm:tm + CONV_HALO, :] = jnp.zeros((CONV_HALO, C_CONV), F32)
            dvbuf[tm:tm + POOL_HALO, :] = jnp.zeros((POOL_HALO, C_POOL), F32)

        dmix = _dot_nt(dx1b_ref[...], wout_ref[...])
        du = dmix[:, :C_CONV]
        dq = dmix[:, C_CONV:]

        pos1 = (tile * tm + lax.broadcasted_iota(jnp.int32, (tm, 1), 0) + 1).astype(F32)
        dpooled_parts = []
        dps_rows = []
        for g, w in enumerate(POOL_WINDOWS):
            cols = slice(g * POOL_GROUP, (g + 1) * POOL_GROUP)
            pooled_b = pooled_ref[:, cols]
            mixed = _dot(pooled_b, pw_ref[g])
            dqg = dq[:, cols]
            dps_rows.append(jnp.sum(dqg * mixed, axis=0, keepdims=True))
            dmixed = (dqg * ps_ref[:, cols]).astype(BF16)
            dpw_ref[g] += _dot_tn(pooled_b, dmixed)
            dpooled = _dot_nt(dmixed, pw_ref[g])
            dpooled_parts.append(dpooled)
            dvbuf[0:tm, cols] = dpooled / jnp.minimum(pos1, float(w))
        vec_ref[4:5, 0:C_POOL] += jnp.concatenate(dps_rows, axis=-1)
        dv_parts = []
        for g, w in enumerate(POOL_WINDOWS):
            cols = slice(g * POOL_GROUP, (g + 1) * POOL_GROUP)
            tot = dvbuf[0:tm, cols]
            for j in range(1, w):
                tot = tot + dvbuf[j:j + tm, cols]
            dv_parts.append(tot - dpooled_parts[g])

        u1 = u1_ref[...]
        mu = jnp.mean(u1, axis=-1, keepdims=True)
        cen = u1 - mu
        rstd = lax.rsqrt(jnp.mean(cen * cen, axis=-1, keepdims=True) + EPS)
        xhat = cen * rstd
        u2 = xhat * lng_ref[...] + lnb_ref[...]
        sg2 = _sigmoid(u2)
        du2 = du * sg2 * (1.0 + u2 * (1.0 - sg2))
        vec_ref[1:2, 0:C_CONV] += jnp.sum(du2 * xhat, axis=0, keepdims=True)
        vec_ref[2:3, 0:C_CONV] += jnp.sum(du2, axis=0, keepdims=True)
        t1 = du2 * lng_ref[...]
        du1 = rstd * (t1 - jnp.mean(t1, axis=-1, keepdims=True)
                      - xhat * jnp.mean(t1 * xhat, axis=-1, keepdims=True))
        vec_ref[3:4, 0:C_CONV] += jnp.sum(du1, axis=0, keepdims=True)
        dubuf[0:tm, :] = du1

        zt = z_ref[...]
        a = zt[:, :C_CONV]
        sgb = _sigmoid(zt[:, C_CONV:2 * C_CONV])
        u0buf[...] = a * sgb

        _shifted_copies(dubuf, dush, tm)
        for rc in range(tm // ROW_CHUNK):
            r0 = rc * ROW_CHUNK
            acc = jnp.zeros((ROW_CHUNK, C_CONV), F32)
            for k in range(CONV_K):
                acc = acc + cw_ref[k:k + 1, :] * _rows_at(dubuf, dush, r0 + (CONV_K - 1) - k)
            du0buf[r0:r0 + ROW_CHUNK, :] = acc
        for k in range(CONV_K):
            acc = jnp.zeros((ROW_CHUNK, C_CONV), F32)
            for rc in range(tm // ROW_CHUNK):
                r0 = rc * ROW_CHUNK
                acc = acc + u0buf[r0:r0 + ROW_CHUNK, :] * _rows_at(dubuf, dush, r0 + (CONV_K - 1) - k)
            dcw_ref[k:k + 1, :] += jnp.sum(acc, axis=0, keepdims=True)
        du0 = du0buf[...]

        da = du0 * sgb
        db = du0 * a * sgb * (1.0 - sgb)
        dz = jnp.concatenate([da, db] + dv_parts, axis=-1).astype(BF16)
        dz_ref[...] = dz

        xt = x_ref[...]
        r1 = lax.rsqrt(jnp.mean(xt * xt, axis=-1, keepdims=True) + EPS)
        dh1 = _dot(dz, win_ref[...])
        dxn, dgm = _rms_bwd(xt, r1, gmix_ref[...], dh1)
        dx_ref[...] = dx1_ref[...] + dxn
        vec_ref[0:1, :] += dgm

        dubuf[tm:tm + CONV_HALO, :] = dubuf[0:CONV_HALO, :]
        dvbuf[tm:tm + POOL_HALO, :] = dvbuf[0:POOL_HALO, :]

    rev = lambda width: pl.BlockSpec((tm, width), lambda i: (nt - 1 - i, 0))
    return _pcall(
        body,
        grid=(nt,),
        in_specs=[
            rev(D_MODEL), rev(D_MODEL), rev(D_MODEL), rev(Z_WIDTH), rev(C_CONV), rev(C_POOL),
            _const_spec((Z_WIDTH, D_MODEL)),
            _const_spec((D_MODEL, D_MODEL)),
            _const_spec((1, D_MODEL)),
            _const_spec((CONV_HALO, C_CONV)),
            _const_spec((1, C_CONV)),
            _const_spec((1, C_CONV)),
            _const_spec((len(POOL_WINDOWS), POOL_GROUP, POOL_GROUP)),
            _const_spec((1, C_POOL)),
        ],
        out_specs=[
            rev(D_MODEL), rev(Z_WIDTH),
            _const_spec((8, D_MODEL)),
            _const_spec((CONV_HALO, C_CONV)),
            _const_spec((len(POOL_WINDOWS), POOL_GROUP, POOL_GROUP)),
        ],
        out_shape=[
            jax.ShapeDtypeStruct((s, D_MODEL), F32),
            jax.ShapeDtypeStruct((s, Z_WIDTH), BF16),
            jax.ShapeDtypeStruct((8, D_MODEL), F32),
            jax.ShapeDtypeStruct((CONV_HALO, C_CONV), F32),
            jax.ShapeDtypeStruct((len(POOL_WINDOWS), POOL_GROUP, POOL_GROUP), F32),
        ],
        scratch_shapes=[
            pltpu.VMEM((tm + CONV_HALO, C_CONV), F32),
            pltpu.VMEM((tm + POOL_HALO, C_POOL), F32),
            pltpu.VMEM((tm, C_CONV), F32),
            pltpu.VMEM((tm, C_CONV), F32),
            pltpu.VMEM((7, tm + CONV_HALO, C_CONV), F32),
        ],
        name="bwd_mix",
        args=(dx1, dx1b, x, z, u1, pooled, w_in_t, w_out, g_mix, conv_w, ln_g, ln_b, pool_w, pool_scale),
        carry=carry,
    )


def _grad_matmul(a, b, bm, name, a2=None, carry=None):
    s, ma = a.shape
    nb = b.shape[1]
    na = ma // bm
    if a2 is None:
        def body(a_ref, b_ref, o_ref):
            o_ref[...] = _dot_tn(a_ref[...], b_ref[...]).astype(BF16)

        lhs_specs = [pl.BlockSpec((s, bm), lambda i: (0, i))]
        lhs = (a,)
        steps = na
    else:
        def body(a_ref, a2_ref, b_ref, o_ref):
            i = pl.program_id(0)

            @pl.when(i < na)
            def _():
                o_ref[...] = _dot_tn(a_ref[...], b_ref[...]).astype(BF16)

            @pl.when(i >= na)
            def _():
                o_ref[...] = _dot_tn(a2_ref[...], b_ref[...]).astype(BF16)

        lhs_specs = [pl.BlockSpec((s, bm), lambda i: (0, jnp.minimum(i, na - 1))),
                     pl.BlockSpec((s, bm), lambda i: (0, jnp.maximum(i - na, 0)))]
        lhs = (a, a2)
        steps = 2 * na

    outs, carried = _pcall(
        body,
        grid=(steps,),
        in_specs=lhs_specs + [pl.BlockSpec((s, nb), lambda i: (0, 0))],
        out_specs=[pl.BlockSpec((bm, nb), lambda i: (i, 0))],
        out_shape=[jax.ShapeDtypeStruct((steps * bm, nb), BF16)],
        scratch_shapes=[],
        name=name,
        args=lhs + (b,),
        carry=carry,
    )
    return outs[0], carried


def _reduce_scatter(grads, small, name):
    n, ns = len(grads), len(small)
    shapes = [g.shape[1:] for g in grads]
    gather = _Carry()
    _carry_gather(gather, small, 1, 2)

    def body(*refs):
        g = refs[:n]
        out = refs[n + ns:2 * n + ns]
        scr = refs[2 * (n + ns):]
        own, loc, r1, r2 = scr[:n], scr[n:2 * n], scr[2 * n:3 * n], scr[3 * n:4 * n]
        load_sems, s1, q1, s2, q2 = scr[4 * n:4 * n + 5]
        gather_refs = (refs[n:n + ns], refs[2 * n + ns:2 * (n + ns)], scr[4 * n + 5:])
        me, sib, chips = _place()
        c = me[2]

        gather.stage(0, *gather_refs)
        loads = []
        sends = []
        for a in range(n):
            ld = [pltpu.make_async_copy(g[a].at[_block(*chip, c)], loc[a].at[j], load_sems.at[a, j])
                  for j, chip in enumerate(chips)]
            ld.append(pltpu.make_async_copy(g[a].at[_block(*me)], own[a], load_sems.at[a, 3]))
            for cp in ld:
                cp.start()
            loads.append(ld)
            blocks = [(*chip, 1 - c) for chip in chips] + [sib]
            for j, blk in enumerate(blocks):
                cp = pltpu.make_async_remote_copy(
                    src_ref=g[a].at[_block(*blk)], dst_ref=r1[a].at[j],
                    send_sem=s1.at[a, j], recv_sem=q1.at[a, j], device_id=sib, device_id_type=MESH)
                cp.start()
                sends.append(cp)

        def from_sibling(a, j):
            return pltpu.make_async_remote_copy(
                src_ref=r1[a].at[j], dst_ref=r1[a].at[j], send_sem=s1.at[a, j], recv_sem=q1.at[a, j],
                device_id=sib, device_id_type=MESH)

        def partial(a, j, chip):
            return pltpu.make_async_remote_copy(
                src_ref=loc[a].at[j], dst_ref=r2[a].at[j], send_sem=s2.at[a, j], recv_sem=q2.at[a, j],
                device_id=(*chip, c), device_id_type=MESH)

        gather.stage(1, *gather_refs)
        for a in range(n):
            for j, chip in enumerate(chips):
                loads[a][j].wait()
                from_sibling(a, j).wait_recv()
                loc[a][j] = (loc[a][j].astype(F32) + r1[a][j].astype(F32)).astype(BF16)
                cp = partial(a, j, chip)
                cp.start()
                sends.append(cp)
        gather.stage(2, *gather_refs)
        for a in range(n):
            loads[a][3].wait()
            from_sibling(a, 3).wait_recv()
            acc = own[a][...].astype(F32) + r1[a][3].astype(F32)
            for j, chip in enumerate(chips):
                partial(a, j, chip).wait_recv()
                acc = acc + r2[a][j].astype(F32)
            out[a][...] = acc
        for cp in sends:
            cp.wait_send()
        gather.drain(*gather_refs)

    any_spec = pl.BlockSpec(memory_space=pl.ANY)
    vmem_spec = pl.BlockSpec(memory_space=pltpu.VMEM)
    res = pl.pallas_call(
        body,
        in_specs=[any_spec] * (n + ns),
        out_specs=[vmem_spec] * n + [any_spec] * ns,
        out_shape=[jax.ShapeDtypeStruct(sh, F32) for sh in shapes] + gather.out_shapes,
        scratch_shapes=(
            [pltpu.VMEM(sh, BF16) for sh in shapes]
            + [pltpu.VMEM((3,) + sh, BF16) for sh in shapes]
            + [pltpu.VMEM((4,) + sh, BF16) for sh in shapes]
            + [pltpu.VMEM((3,) + sh, BF16) for sh in shapes]
            + [pltpu.SemaphoreType.DMA((n, 4)),
               pltpu.SemaphoreType.DMA((n, 4)), pltpu.SemaphoreType.DMA((n, 4)),
               pltpu.SemaphoreType.DMA((n, 3)), pltpu.SemaphoreType.DMA((n, 3))]
            + gather.sem_shapes()
        ),
        compiler_params=pltpu.CompilerParams(vmem_limit_bytes=V7X_VMEM_LIMIT),
        name=name,
    )(*grads, *small)
    return res[:n], res[n:]


def _pair_add(grads, from_sib, blks, name):
    n = len(grads)

    def body(blk_ref, *refs):
        for a in range(n):
            refs[2 * n + a][...] = (refs[a][...].astype(F32) + refs[n + a][...].astype(F32)).astype(BF16)

    mine = [pl.BlockSpec((None,) + g.shape[1:], lambda j, b: (b[j], 0, 0)) for g in grads]
    same = [pl.BlockSpec((None,) + g.shape[1:], lambda j, b: (j, 0, 0)) for g in grads]
    return pl.pallas_call(
        body,
        grid_spec=pltpu.PrefetchScalarGridSpec(
            num_scalar_prefetch=1, grid=(4,), in_specs=mine + same, out_specs=same),
        out_shape=[jax.ShapeDtypeStruct((4,) + g.shape[1:], BF16) for g in grads],
        compiler_params=_params(("arbitrary",)),
        name=name,
    )(blks, *grads, *from_sib)


def _chip_sum(parts, from_chips, name):
    n = len(parts)

    def body(*refs):
        for a in range(n):
            acc = refs[a][...].astype(F32)
            for j in range(3):
                acc = acc + refs[n + a][j].astype(F32)
            refs[2 * n + a][...] = acc

    half = [p.shape[1] // 2 for p in parts]
    return pl.pallas_call(
        body,
        grid=(2,),
        in_specs=([pl.BlockSpec((None, h, p.shape[2]), lambda i: (3, i, 0)) for p, h in zip(parts, half)]
                  + [pl.BlockSpec((3, h, p.shape[2]), lambda i: (0, i, 0)) for p, h in zip(parts, half)]),
        out_specs=[pl.BlockSpec((h, p.shape[2]), lambda i: (i, 0)) for p, h in zip(parts, half)],
        out_shape=[jax.ShapeDtypeStruct(p.shape[1:], F32) for p in parts],
        compiler_params=_params(("arbitrary",)),
        name=name,
    )(*parts, *from_chips)


def _adam_math(w, g, m, v):
    nm = ADAM_B1 * m + (1.0 - ADAM_B1) * g
    nv = ADAM_B2 * v + (1.0 - ADAM_B2) * (g * g)
    m_hat = nm / (1.0 - ADAM_B1 ** ADAM_STEP)
    v_hat = nv / (1.0 - ADAM_B2 ** ADAM_STEP)
    return -ADAM_LR * (m_hat / (jnp.sqrt(v_hat) + ADAM_EPS) + ADAM_WD * w), nm, nv


def _sum_adamw(items, name, carry=None):
    n = len(items)

    def body(*refs):
        for a in range(n):
            p_ref, f_ref, w_ref, m_ref, v_ref = refs[5 * a:5 * a + 5]
            g_ref, d_ref, nm_ref, nv_ref = refs[5 * n + 4 * a:5 * n + 4 * a + 4]
            g = p_ref[...].astype(F32)
            for j in range(3):
                g = g + f_ref[j].astype(F32)
            g_ref[...] = g
            d_ref[...], nm_ref[...], nv_ref[...] = _adam_math(w_ref[...], g, m_ref[...], v_ref[...])

    in_specs, out_specs, out_shape, args = [], [], [], []
    for part, from_chips, w, m, v in items:
        r, c = w.shape
        spec = pl.BlockSpec((r // 2, c), lambda i: (i, 0))
        in_specs += [pl.BlockSpec((None, r // 2, c), lambda i: (3, i, 0)),
                     pl.BlockSpec((3, r // 2, c), lambda i: (0, i, 0)), spec, spec, spec]
        out_specs += [spec] * 4
        out_shape += [jax.ShapeDtypeStruct((r, c), F32)] * 4
        args += [part, from_chips, w, m, v]
    outs, carried = _pcall(body, grid=(2,), in_specs=in_specs, out_specs=out_specs, out_shape=out_shape,
                           scratch_shapes=[], name=name, args=tuple(args), carry=carry)
    return [tuple(outs[4 * a:4 * a + 4]) for a in range(n)], carried


def _small_update(gathered, layout, params, name):
    ng, npar = len(gathered), len(params)

    def body(*refs):
        parts = refs[:ng]
        prm = refs[ng:ng + 3 * npar]
        tot_refs = refs[ng + 3 * npar:2 * ng + 3 * npar]
        out = refs[2 * ng + 3 * npar:]
        tots = []
        for a in range(ng):
            acc = parts[a][0]
            for d in range(1, N_DEV):
                acc = acc + parts[a][d]
            tot_refs[a][...] = acc
            tots.append(acc)
        for i, (a, row, width) in enumerate(layout):
            g = tots[a] if row is None else tots[a][row:row + 1, :width]
            delta, nm, nv = _adam_math(prm[3 * i][...], g, prm[3 * i + 1][...], prm[3 * i + 2][...])
            out[4 * i][...] = g
            out[4 * i + 1][...] = delta
            out[4 * i + 2][...] = nm
            out[4 * i + 3][...] = nv

    flat = [t for prm in params for t in prm]
    res = pl.pallas_call(
        body,
        out_shape=([jax.ShapeDtypeStruct(g.shape[1:], F32) for g in gathered]
                   + [jax.ShapeDtypeStruct(prm[0].shape, F32) for prm in params for _ in range(4)]),
        compiler_params=pltpu.CompilerParams(vmem_limit_bytes=V7X_VMEM_LIMIT),
        name=name,
    )(*gathered, *flat)
    return res[:ng], [tuple(res[ng + 4 * i:ng + 4 * i + 4]) for i in range(npar)]


def _adamw(w, g, m, v, name):
    rows, cols = w.shape
    br = rows
    for cand in (512, 256, 128):
        if rows % cand == 0 and rows > cand:
            br = cand
            break

    def body(w_ref, g_ref, m_ref, v_ref, d_ref, nm_ref, nv_ref):
        d_ref[...], nm_ref[...], nv_ref[...] = _adam_math(w_ref[...], g_ref[...], m_ref[...], v_ref[...])

    spec = pl.BlockSpec((br, cols), lambda i: (i, 0))
    shape = jax.ShapeDtypeStruct((rows, cols), F32)
    return pl.pallas_call(
        body,
        grid=(rows // br,),
        in_specs=[spec] * 4,
        out_specs=[spec] * 3,
        out_shape=[shape] * 3,
        compiler_params=_params(("arbitrary",)),
        name=name,
    )(w, g, m, v)


def _by_device(full):
    return full.reshape(N_DEV, full.shape[0] // N_DEV, full.shape[1])


def kernel(x, p, g_mix, w_in, conv_w, conv_b, ln_g, ln_b, pool_w, pool_scale, w_out, g_ffn, w_gate_up, w_down, g_ple_gate, w_ple_gate, w_ple_up, g_ple_post, g_final, loss_target, m_g_mix, m_w_in, m_conv_w, m_conv_b, m_ln_g, m_ln_b, m_pool_w, m_pool_scale, m_w_out, m_g_ffn, m_w_gate_up, m_w_down, m_g_ple_gate, m_w_ple_gate, m_w_ple_up, m_g_ple_post, m_g_final, v_g_mix, v_w_in, v_conv_w, v_conv_b, v_ln_g, v_ln_b, v_pool_w, v_pool_scale, v_w_out, v_g_ffn, v_w_gate_up, v_w_down, v_g_ple_gate, v_w_ple_gate, v_w_ple_up, v_g_ple_post, v_g_final):
    seq = x.shape[1]
    xs = x[0]
    ps = p[0, 0]
    tgt = loss_target[0]
    ax, ay, ac = lax.axis_index("x"), lax.axis_index("y"), lax.axis_index("c")
    me = _block(ax, ay, ac)
    blks = jnp.stack([_block(1 - ax, ay, ac), _block(ax, 1 - ay, ac), _block(1 - ax, 1 - ay, ac), me]).astype(jnp.int32)
    rows = lambda gth: gth.reshape((-1,) + gth.shape[2:])

    carry = _Carry()
    _carry_gather(carry, [
        w_in[0].T.astype(BF16),
        w_out[0].astype(BF16),
        jnp.pad(conv_w[0].T, ((0, 0), (0, CONV_HALO - CONV_K))),
    ], 8, 15)
    (h1,), first = _norm_in(xs, g_mix, min(256, seq), carry)
    w_in_t, w_out_f, conv_w_t = [rows(gth) for gth in first]
    conv_w_f = conv_w_t.T
    pool_w_b = pool_w[0].astype(BF16)

    carry = _Carry()
    _carry_gather(carry, [w_gate_up[0].T.astype(BF16)], 8, 12)
    (z, u1, pooled, x1, mix), (w_gu_all,) = _fwd_mix(
        xs, h1, w_in_t, w_out_f, conv_w_f, conv_b, ln_g, ln_b, pool_w_b, pool_scale, min(256, seq), carry)
    w_gu_t = rows(w_gu_all)

    carry = _Carry()
    _carry_gather(carry, [w_down[0].astype(BF16),
                          w_ple_gate[0].astype(BF16),
                          w_ple_up[0].T.astype(BF16)], 6, 10)
    (h2, g_sav, u_sav, act), late = _ffn_up(x1, w_gu_t, g_ffn, min(256, seq), carry)
    w_down_f, w_pg_f, w_pu_t = [rows(gth) for gth in late]
    dx2, dx2b, hg, ds, dpe, pb, stats_ple = _down_ple(
        x1, act, w_down_f, ps, tgt, w_pg_f, w_pu_t, g_ple_gate, g_ple_post, g_final.reshape(1, D_MODEL),
        min(256, seq))
    dg, du, dx1, dx1b, stats_ffn = _ffn_bwd(dx2, dx2b, x1, g_sav, u_sav, w_gu_t, w_down_f, g_ffn,
                                            min(256, seq))[0]

    d_w_pg, _ = _grad_matmul(hg, ds, 256, "grad_w_ple_gate")
    d_w_pu_t, _ = _grad_matmul(dpe, pb, 256, "grad_w_ple_up")
    d_w_gu_t, _ = _grad_matmul(dg, h2, 256, "grad_w_gate_up", a2=du)
    d_w_out, _ = _grad_matmul(mix, dx1b, 256, "grad_w_out")
    early = [_by_device(d_w_pg), _by_device(d_w_pu_t), _by_device(d_w_gu_t), _by_device(d_w_out)]
    carry = _Carry()
    _carry_pair(carry, early)
    d_w_down, from_sib = _grad_matmul(act, dx2b, 256, "grad_w_down", carry=carry)
    early_parts = _pair_add(early, from_sib, blks, "pair_add_early")

    carry = _Carry()
    _carry_chip(carry, early_parts)
    _carry_pair(carry, [_by_device(d_w_down)])
    (dx, dz, vec_mix, dconv_w_part, dpool_w_part), carried = _bwd_mix(
        dx1, dx1b, xs, z, u1, pooled, w_in_t, w_out_f, g_mix, conv_w_f, ln_g, ln_b, pool_w_b, pool_scale,
        min(256, seq), carry)
    early_chips, down_sib = carried[:4], carried[4:]
    down_parts = _pair_add([_by_device(d_w_down)], down_sib, blks, "pair_add_down")

    carry = _Carry()
    _carry_chip(carry, down_parts)
    d_w_in_t, down_chips = _grad_matmul(dz, h1, 256, "grad_w_in", carry=carry)
    (gr_w_in_t,), small = _reduce_scatter(
        [_by_device(d_w_in_t)],
        [vec_mix, stats_ple, stats_ffn, dconv_w_part, dpool_w_part.reshape(-1, POOL_GROUP)], "scatter_last")
    small = list(small[:4]) + [small[4].reshape((N_DEV,) + dpool_w_part.shape)]

    vec_names = ["g_mix", "ln_g", "ln_b", "conv_b", "pool_scale", "g_final", "g_ple_post", "g_ple_gate", "g_ffn",
                 "pool_w"]
    layout = [(0, 0, D_MODEL), (0, 1, C_CONV), (0, 2, C_CONV), (0, 3, C_CONV), (0, 4, C_POOL),
              (1, 0, D_MODEL), (1, 1, D_MODEL), (1, 2, D_MODEL), (2, 0, D_MODEL), (4, None, None)]
    as_row = lambda t: t.reshape(1, D_MODEL)
    params = [(g_mix, m_g_mix, v_g_mix), (ln_g, m_ln_g, v_ln_g), (ln_b, m_ln_b, v_ln_b),
              (conv_b, m_conv_b, v_conv_b), (pool_scale, m_pool_scale, v_pool_scale),
              (as_row(g_final), as_row(m_g_final), as_row(v_g_final)),
              (g_ple_post, m_g_ple_post, v_g_ple_post), (g_ple_gate, m_g_ple_gate, v_g_ple_gate),
              (g_ffn, m_g_ffn, v_g_ffn), (pool_w[0], m_pool_w[0], v_pool_w[0])]
    tots, small_upd = _small_update(small, layout, params, "small_update")
    loss = tots[1][3, 0]
    upd = {}
    for nm, res, prm in zip(vec_names, small_upd, [g_mix, ln_g, ln_b, conv_b, pool_scale, g_final, g_ple_post,
                                                    g_ple_gate, g_ffn, pool_w]):
        upd[nm] = tuple(t.reshape(prm.shape) for t in res)
    gr_conv_w = lax.dynamic_slice_in_dim(tots[3][:CONV_K], me * (C_CONV // N_DEV), C_CONV // N_DEV, axis=1)

    natural = lambda t: t[None]
    turned = lambda t: t.T[None]
    (res_pg, res_gu, res_out, res_down), _ = _sum_adamw([
        (early_parts[0], early_chips[0], w_ple_gate[0], m_w_ple_gate[0], v_w_ple_gate[0]),
        (early_parts[2], early_chips[2], w_gate_up[0].T, m_w_gate_up[0].T, v_w_gate_up[0].T),
        (early_parts[3], early_chips[3], w_out[0], m_w_out[0], v_w_out[0]),
        (down_parts[0], down_chips[0], w_down[0], m_w_down[0], v_w_down[0])], "adamw_big")
    upd["w_ple_gate"] = tuple(natural(t) for t in res_pg)
    upd["w_gate_up"] = tuple(turned(t) for t in res_gu)
    upd["w_out"] = tuple(natural(t) for t in res_out)
    upd["w_down"] = tuple(natural(t) for t in res_down)
    (gr_w_pu_t,) = _chip_sum(early_parts[1:2], early_chips[1:2], "chip_sum")
    plain = [
        ("w_in", w_in[0].T, gr_w_in_t, m_w_in[0].T, v_w_in[0].T, True),
        ("w_ple_up", w_ple_up[0], gr_w_pu_t.T, m_w_ple_up[0], v_w_ple_up[0], False),
        ("conv_w", conv_w[0], gr_conv_w, m_conv_w[0], v_conv_w[0], False),
    ]
    for nm, w_, g_, m_, v_, transposed in plain:
        res = (g_,) + tuple(_adamw(w_, g_, m_, v_, "adamw_" + nm))
        upd[nm] = tuple((t.T if transposed else t)[None] for t in res)

    order = ["g_mix", "w_in", "conv_w", "conv_b", "ln_g", "ln_b", "pool_w", "pool_scale", "w_out", "g_ffn",
             "w_gate_up", "w_down", "g_ple_gate", "w_ple_gate", "w_ple_up", "g_ple_post", "g_final"]
    outs = [loss, dx[None]]
    for k in range(4):
        outs += [upd[nm][k] for nm in order]
    return tuple(outs)
```

```python
import functools

import jax
import jax.numpy as jnp
from jax import lax
from jax.experimental import pallas as pl
from jax.experimental.pallas import tpu as pltpu

D_MODEL = 1024
C_CONV = 512
C_POOL = 512
Z_WIDTH = 2 * C_CONV + C_POOL
POOL_WINDOWS = (2, 4, 8, 16)
POOL_GROUP = 128
CONV_K = 31
D_FF = 2816
D_PLE = 256
EPS = 1e-6
N_DEV = 8

ADAM_LR = 0.001
ADAM_B1 = 0.9
ADAM_B2 = 0.999
ADAM_EPS = 1e-08
ADAM_WD = 0.01
ADAM_STEP = 10

CONV_HALO = 32
POOL_HALO = 16
ROW_CHUNK = 32
V7X_VMEM_LIMIT = 56 * 1024 * 1024
FF_CHUNK = D_FF // 2

BF16 = jnp.bfloat16
F32 = jnp.float32
MESH = pl.DeviceIdType.MESH


def _dot(a, b):
    return lax.dot_general(a, b, (((1,), (0,)), ((), ())), preferred_element_type=F32)


def _dot_nt(a, b):
    return lax.dot_general(a, b, (((1,), (1,)), ((), ())), preferred_element_type=F32)


def _dot_tn(a, b):
    return lax.dot_general(a, b, (((0,), (0,)), ((), ())), preferred_element_type=F32)


def _rms_fwd(x, g):
    r = lax.rsqrt(jnp.mean(x * x, axis=-1, keepdims=True) + EPS)
    return x * r * g, r


def _rms_bwd(x, r, g, dy):
    xr = x * r
    dg = jnp.sum(dy * xr, axis=0, keepdims=True)
    dyg = dy * g
    dx = r * (dyg - xr * jnp.mean(dyg * xr, axis=-1, keepdims=True))
    return dx, dg


def _sigmoid(x):
    return jax.nn.sigmoid(x)


def _params(sem=None):
    return pltpu.CompilerParams(dimension_semantics=sem, vmem_limit_bytes=V7X_VMEM_LIMIT)


def _place():
    x, y, c = lax.axis_index("x"), lax.axis_index("y"), lax.axis_index("c")
    chips = [(1 - x, y), (x, 1 - y), (1 - x, 1 - y)]
    return (x, y, c), (x, y, 1 - c), chips


def _block(px, py, pc):
    return 4 * px + 2 * py + pc


class _Carry:
    def __init__(self):
        self.inputs = []
        self.out_shapes = []
        self.copies = []
        self.locals = []

    def add_input(self, arr):
        self.inputs.append(arr)
        return len(self.inputs) - 1

    def add_output(self, shape, dtype):
        self.out_shapes.append(jax.ShapeDtypeStruct(shape, dtype))
        return len(self.out_shapes) - 1

    def local(self, src_idx, dst_idx, dst_blk):
        self.locals.append((src_idx, dst_idx, dst_blk))

    def copy(self, src, dst_idx, dst_blk, got_blk, peer, step=0, after=()):
        self.copies.append(dict(src=src, dst_idx=dst_idx, dst_blk=dst_blk, got_blk=got_blk, peer=peer, step=step,
                                after=tuple(after)))
        return len(self.copies) - 1

    def sem_shapes(self):
        return [pltpu.SemaphoreType.DMA((max(1, len(self.copies)),)),
                pltpu.SemaphoreType.DMA((max(1, len(self.copies)),)),
                pltpu.SemaphoreType.DMA((max(1, len(self.locals)),))]

    @staticmethod
    def _view(ref, where):
        if isinstance(where, tuple):
            blk, row0, nrows = where
            return ref.at[blk, pl.ds(row0, nrows)]
        return ref.at[where]

    def _desc(self, k, ins, outs, sems, place):
        cp = self.copies[k]
        me, sib, chips = place
        kind, idx, blk = cp["src"]
        src = (ins if kind == "in" else outs)[idx]
        if blk is not None:
            src = self._view(src, blk(*place))
        to = sib if cp["peer"] == "sib" else (*chips[cp["peer"]], me[2])
        return pltpu.make_async_remote_copy(
            src_ref=src, dst_ref=self._view(outs[cp["dst_idx"]], cp["dst_blk"](*place)),
            send_sem=sems[0].at[k], recv_sem=sems[1].at[k], device_id=to, device_id_type=MESH)

    def _arrival(self, k, outs, sems, place):
        cp = self.copies[k]
        got = self._view(outs[cp["dst_idx"]], cp["got_blk"](*place))
        return pltpu.make_async_remote_copy(
            src_ref=got, dst_ref=got, send_sem=sems[0].at[k], recv_sem=sems[1].at[k],
            device_id=place[0], device_id_type=MESH)

    def _local(self, n, ins, outs, sems, place):
        src_idx, dst_idx, blk = self.locals[n]
        return pltpu.make_async_copy(ins[src_idx], outs[dst_idx].at[blk(*place)], sems[2].at[n])

    def stages(self):
        return sorted({0} | {cp["step"] for cp in self.copies})

    def stage(self, s, ins, outs, sems):
        place = _place()
        if s == 0:
            self._waited = set()
            for n in range(len(self.locals)):
                self._local(n, ins, outs, sems, place).start()
        for k, cp in enumerate(self.copies):
            if cp["step"] != s:
                continue
            for a in cp["after"]:
                if a not in self._waited:
                    self._arrival(a, outs, sems, place).wait_recv()
                    self._waited.add(a)
            self._desc(k, ins, outs, sems, place).start()

    def drain(self, ins, outs, sems):
        place = _place()
        for k in range(len(self.copies)):
            if k not in self._waited:
                self._arrival(k, outs, sems, place).wait_recv()
        for k in range(len(self.copies)):
            self._desc(k, ins, outs, sems, place).wait_send()
        for n in range(len(self.locals)):
            self._local(n, ins, outs, sems, place).wait()

    def starts(self, step, nsteps, ins, outs, sems):
        for s in self.stages():
            pl.when(step == min(s, nsteps - 1))(functools.partial(self.stage, s, ins, outs, sems))

    def finish(self, step, nsteps, ins, outs, sems):
        pl.when(step == nsteps - 1)(functools.partial(self.drain, ins, outs, sems))


def _const_blk(j):
    return lambda me, sib, chips: j


def _carry_gather(carry, shards, relay_step, last_step):
    outs = []
    for sh in shards:
        i = carry.add_input(sh)
        o = carry.add_output((N_DEV,) + sh.shape, sh.dtype)
        half = sh.shape[0] // 2
        tile = 16 if sh.dtype == BF16 else 8
        split = half % tile == 0
        rows = [(0, half), (half, sh.shape[0] - half)] if split else [(0, sh.shape[0]), None]

        def whole(j, core):
            return lambda me, sib, chips, j=j, core=core: _block(*chips[j], me[2] if core == 0 else 1 - me[2])

        def part(j, core, h, rows=rows):
            return lambda me, sib, chips: (_block(*chips[j], me[2] if core == 0 else 1 - me[2]),) + rows[h]

        mine = lambda me, sib, chips: _block(*me)
        carry.local(i, o, mine)
        carry.copy(("in", i, None), o, mine, lambda me, sib, chips: _block(*sib), "sib")
        near = [carry.copy(("in", i, None), o, mine, whole(j, 0), j) for j in range(2)]
        for j in range(2):
            carry.copy(("out", o, whole(j, 0)), o, whole(j, 0), whole(j, 1), "sib", step=relay_step, after=(near[j],))
        for j in range(2):
            if rows[j] is None:
                continue
            far = carry.copy(("out", o, part(j, 0, j)), o, part(j, 0, j), part(2, 0, j), 1 - j,
                             step=relay_step, after=(near[j],))
            carry.copy(("out", o, part(2, 0, j)), o, part(2, 0, j), part(2, 1, j), "sib", step=last_step, after=(far,))
        outs.append(o)
    return outs


def _carry_pair(carry, grads):
    outs = []
    for g in grads:
        i = carry.add_input(g)
        o = carry.add_output((4,) + g.shape[1:], g.dtype)
        for j in range(4):
            if j < 3:
                blk = lambda me, sib, chips, j=j: _block(*chips[j], 1 - me[2])
            else:
                blk = lambda me, sib, chips: _block(*sib)
            carry.copy(("in", i, blk), o, _const_blk(j), _const_blk(j), "sib")
        outs.append(o)
    return outs


def _carry_chip(carry, parts):
    outs = []
    for p in parts:
        i = carry.add_input(p)
        o = carry.add_output((3,) + p.shape[1:], p.dtype)
        for j in range(3):
            carry.copy(("in", i, _const_blk(j)), o, _const_blk(j), _const_blk(j), j)
        outs.append(o)
    return outs


def _pcall(body, *, grid, in_specs, out_specs, out_shape, scratch_shapes, name, args, carry=None):
    sem = ("arbitrary",) * len(grid)
    if carry is None:
        res = pl.pallas_call(body, grid=grid, in_specs=in_specs, out_specs=out_specs, out_shape=out_shape,
                             scratch_shapes=scratch_shapes, compiler_params=_params(sem), name=name)(*args)
        return list(res), []
    n_in, n_out, n_scr = len(in_specs), len(out_specs), len(scratch_shapes)
    c_in, c_out = len(carry.inputs), len(carry.out_shapes)
    nsteps = 1
    for extent in grid:
        nsteps *= extent

    def wrapped(*refs):
        ins = refs[:n_in]
        cins = refs[n_in:n_in + c_in]
        o0 = n_in + c_in
        outs = refs[o0:o0 + n_out]
        couts = refs[o0 + n_out:o0 + n_out + c_out]
        s0 = o0 + n_out + c_out
        scr = refs[s0:s0 + n_scr]
        sems = refs[s0 + n_scr:]
        step = pl.program_id(0)
        for d in range(1, len(grid)):
            step = step * grid[d] + pl.program_id(d)
        carry.starts(step, nsteps, cins, couts, sems)
        body(*ins, *outs, *scr)
        carry.finish(step, nsteps, cins, couts, sems)

    any_spec = pl.BlockSpec(memory_space=pl.ANY)
    res = pl.pallas_call(
        wrapped, grid=grid,
        in_specs=list(in_specs) + [any_spec] * c_in,
        out_specs=list(out_specs) + [any_spec] * c_out,
        out_shape=list(out_shape) + carry.out_shapes,
        scratch_shapes=list(scratch_shapes) + carry.sem_shapes(),
        compiler_params=_params(sem), name=name)(*args, *carry.inputs)
    return list(res[:n_out]), list(res[n_out:])


def _shifted_copies(buf, shifted, tm):
    span = tm + CONV_HALO - 8
    for r in range(1, 8):
        shifted[r - 1, 0:span, :] = buf[r:r + span, :]


def _rows_at(buf, shifted, start):
    aligned, r = (start // 8) * 8, start % 8
    if r == 0:
        return buf[aligned:aligned + ROW_CHUNK, :]
    return shifted[r - 1, aligned:aligned + ROW_CHUNK, :]


def _row_spec(tm, width):
    return pl.BlockSpec((tm, width), lambda i: (i, 0))


def _const_spec(shape):
    return pl.BlockSpec(shape, lambda i: (0,) * len(shape))


def _weight_spec(shape):
    return pl.BlockSpec(shape, lambda i: (0,) * len(shape), pipeline_mode=pl.Buffered(1))


def _to_bf16(arrays):
    n = len(arrays)

    def body(*refs):
        for a in range(n):
            refs[n + a][...] = refs[a][...].astype(BF16)

    return pl.pallas_call(
        body,
        out_shape=[jax.ShapeDtypeStruct(arr.shape, BF16) for arr in arrays],
        compiler_params=pltpu.CompilerParams(vmem_limit_bytes=V7X_VMEM_LIMIT),
        name="to_bf16",
    )(*arrays)


def _norm_in(x, g_mix, tm, carry=None):
    s = x.shape[0]

    def body(x_ref, gmix_ref, h1_ref):
        h, _ = _rms_fwd(x_ref[...], gmix_ref[...])
        h1_ref[...] = h.astype(BF16)

    return _pcall(
        body,
        grid=(s // tm,),
        in_specs=[_row_spec(tm, D_MODEL), _const_spec((1, D_MODEL))],
        out_specs=[_row_spec(tm, D_MODEL)],
        out_shape=[jax.ShapeDtypeStruct((s, D_MODEL), BF16)],
        scratch_shapes=[],
        name="norm_in",
        args=(x, g_mix),
        carry=carry,
    )


def _fwd_mix(x, h1, w_in_t, w_out, conv_w, conv_b, ln_g, ln_b, pool_w, pool_scale, tm, carry=None):
    s = x.shape[0]
    nt = s // tm

    def body(x_ref, h1_ref, win_ref, wout_ref, cw_ref, cb_ref, lng_ref, lnb_ref, pw_ref, ps_ref,
             z_ref, u1_ref, pooled_ref, x1_ref, mix_ref, ubuf, vbuf, ush):
        i = pl.program_id(0)

        @pl.when(i == 0)
        def _():
            ubuf[0:CONV_HALO, :] = jnp.zeros((CONV_HALO, C_CONV), F32)
            vbuf[0:POOL_HALO, :] = jnp.zeros((POOL_HALO, C_POOL), F32)

        xt = x_ref[...]
        z = _dot_nt(h1_ref[...], win_ref[...])
        z_ref[...] = z
        a = z[:, :C_CONV]
        b = z[:, C_CONV:2 * C_CONV]
        v = z[:, 2 * C_CONV:]
        ubuf[CONV_HALO:CONV_HALO + tm, :] = a * _sigmoid(b)
        vbuf[POOL_HALO:POOL_HALO + tm, :] = v

        _shifted_copies(ubuf, ush, tm)
        for rc in range(tm // ROW_CHUNK):
            base = rc * ROW_CHUNK + CONV_HALO - (CONV_K - 1)
            acc = jnp.broadcast_to(cb_ref[...], (ROW_CHUNK, C_CONV))
            for k in range(CONV_K):
                acc = acc + cw_ref[k:k + 1, :] * _rows_at(ubuf, ush, base + k)
            u1_ref[rc * ROW_CHUNK:(rc + 1) * ROW_CHUNK, :] = acc

        u1 = u1_ref[...]
        mu = jnp.mean(u1, axis=-1, keepdims=True)
        cen = u1 - mu
        rstd = lax.rsqrt(jnp.mean(cen * cen, axis=-1, keepdims=True) + EPS)
        u2 = cen * rstd * lng_ref[...] + lnb_ref[...]
        u = u2 * _sigmoid(u2)

        pos1 = (i * tm + lax.broadcasted_iota(jnp.int32, (tm, 1), 0) + 1).astype(F32)
        parts = [u]
        for g, w in enumerate(POOL_WINDOWS):
            cols = slice(g * POOL_GROUP, (g + 1) * POOL_GROUP)
            vg = v[:, cols]
            tot = vg
            for j in range(1, w):
                tot = tot + vbuf[POOL_HALO - j:POOL_HALO - j + tm, cols]
            pooled = tot / jnp.minimum(pos1, float(w)) - vg
            pooled_b = pooled.astype(BF16)
            pooled_ref[:, cols] = pooled_b
            parts.append(_dot(pooled_b, pw_ref[g]) * ps_ref[:, cols])
        mix = jnp.concatenate(parts, axis=-1).astype(BF16)
        mix_ref[...] = mix
        x1_ref[...] = xt + _dot(mix, wout_ref[...])

        ubuf[0:CONV_HALO, :] = ubuf[tm:tm + CONV_HALO, :]
        vbuf[0:POOL_HALO, :] = vbuf[tm:tm + POOL_HALO, :]

    return _pcall(
        body,
        grid=(nt,),
        in_specs=[
            _row_spec(tm, D_MODEL),
            _row_spec(tm, D_MODEL),
            _const_spec((Z_WIDTH, D_MODEL)),
            _const_spec((D_MODEL, D_MODEL)),
            _const_spec((CONV_HALO, C_CONV)),
            _const_spec((1, C_CONV)),
            _const_spec((1, C_CONV)),
            _const_spec((1, C_CONV)),
            _const_spec((len(POOL_WINDOWS), POOL_GROUP, POOL_GROUP)),
            _const_spec((1, C_POOL)),
        ],
        out_specs=[
            _row_spec(tm, Z_WIDTH),
            _row_spec(tm, C_CONV),
            _row_spec(tm, C_POOL),
            _row_spec(tm, D_MODEL),
            _row_spec(tm, D_MODEL),
        ],
        out_shape=[
            jax.ShapeDtypeStruct((s, Z_WIDTH), F32),
            jax.ShapeDtypeStruct((s, C_CONV), F32),
            jax.ShapeDtypeStruct((s, C_POOL), BF16),
            jax.ShapeDtypeStruct((s, D_MODEL), F32),
            jax.ShapeDtypeStruct((s, D_MODEL), BF16),
        ],
        scratch_shapes=[
            pltpu.VMEM((tm + CONV_HALO, C_CONV), F32),
            pltpu.VMEM((tm + POOL_HALO, C_POOL), F32),
            pltpu.VMEM((7, tm + CONV_HALO, C_CONV), F32),
        ],
        name="fwd_mix",
        args=(x, h1, w_in_t, w_out, conv_w, conv_b, ln_g, ln_b, pool_w, pool_scale),
        carry=carry,
    )


def _ffn_up(x1, w_gu_t, g_ffn, tm, carry=None):
    s = x1.shape[0]

    def body(x1_ref, w_ref, gffn_ref, h2_ref, g_ref, u_ref, act_ref):
        h, _ = _rms_fwd(x1_ref[...], gffn_ref[...])
        h2 = h.astype(BF16)
        h2_ref[...] = h2
        for c in range(D_FF // FF_CHUNK):
            cols = slice(c * FF_CHUNK, (c + 1) * FF_CHUNK)
            g = _dot_nt(h2, w_ref[c * FF_CHUNK:(c + 1) * FF_CHUNK, :])
            u = _dot_nt(h2, w_ref[D_FF + c * FF_CHUNK:D_FF + (c + 1) * FF_CHUNK, :])
            g_ref[:, cols] = g.astype(BF16)
            u_ref[:, cols] = u.astype(BF16)
            act_ref[:, cols] = (g * _sigmoid(g) * u).astype(BF16)

    return _pcall(
        body,
        grid=(s // tm,),
        in_specs=[_row_spec(tm, D_MODEL), _weight_spec((2 * D_FF, D_MODEL)), _const_spec((1, D_MODEL))],
        out_specs=[_row_spec(tm, D_MODEL), _row_spec(tm, D_FF), _row_spec(tm, D_FF), _row_spec(tm, D_FF)],
        out_shape=[
            jax.ShapeDtypeStruct((s, D_MODEL), BF16),
            jax.ShapeDtypeStruct((s, D_FF), BF16),
            jax.ShapeDtypeStruct((s, D_FF), BF16),
            jax.ShapeDtypeStruct((s, D_FF), BF16),
        ],
        scratch_shapes=[],
        name="ffn_up",
        args=(x1, w_gu_t, g_ffn),
        carry=carry,
    )


def _down_ple(x1, act, w_down, p, tgt, w_pg, w_pu_t, g_gate, g_post, g_final, tm):
    s = x1.shape[0]
    nt = s // tm

    def body(x1_ref, act_ref, wd_ref, p_ref, t_ref, wpg_ref, wpu_ref, gg_ref, gp_ref, gf_ref,
             dx2_ref, dx2b_ref, hg_ref, ds_ref, dpe_ref, pb_ref, stats_ref, x2_cur, x2_next):
        i = pl.program_id(0)

        @pl.when(i == 0)
        def _():
            stats_ref[...] = jnp.zeros_like(stats_ref)
            x2_cur[...] = jnp.zeros((tm, D_MODEL), F32)

        def down(c):
            cols = slice(c * 256, (c + 1) * 256)
            x2_next[:, cols] = x1_ref[:, cols] + _dot(act_ref[...], wd_ref[:, cols])

        x2 = x2_cur[...]
        counts = i >= 1

        hg, rg = _rms_fwd(x2, gg_ref[...])
        hg_b = hg.astype(BF16)
        hg_ref[...] = hg_b
        down(0)
        gate = _sigmoid(_dot(hg_b, wpg_ref[...]))
        pb = p_ref[...].astype(BF16)
        pb_ref[...] = pb
        pe = _dot_nt(pb, wpu_ref[...])
        e, rp = _rms_fwd(pe, gp_ref[...])
        x3 = x2 + gate * e
        down(1)
        y, r3 = _rms_fwd(x3, gf_ref[...])
        diff = y - t_ref[...]
        loss = 0.5 * jnp.sum(jnp.sum(diff * diff, axis=-1, keepdims=True), axis=0, keepdims=True) / D_MODEL
        dy = diff * (1.0 / D_MODEL)

        dx3, dgf = _rms_bwd(x3, r3, gf_ref[...], dy)
        down(2)
        dpe, dgp = _rms_bwd(pe, rp, gp_ref[...], dx3 * gate)
        dpe_ref[...] = dpe.astype(BF16)
        ds = (dx3 * e * gate * (1.0 - gate)).astype(BF16)
        ds_ref[...] = ds
        dhg = _dot_nt(ds, wpg_ref[...])
        down(3)
        dxg, dgg = _rms_bwd(x2, rg, gg_ref[...], dhg)
        dx2 = dx3 + dxg
        dx2_ref[...] = dx2
        dx2b_ref[...] = dx2.astype(BF16)
        x2_cur[...] = x2_next[...]

        stats_ref[0:1, :] += jnp.where(counts, dgf, 0.0)
        stats_ref[1:2, :] += jnp.where(counts, dgp, 0.0)
        stats_ref[2:3, :] += jnp.where(counts, dgg, 0.0)
        stats_ref[3:4, :] += jnp.where(counts, jnp.broadcast_to(loss, (1, D_MODEL)), 0.0)

    ahead = lambda width: pl.BlockSpec((tm, width), lambda i: (jnp.minimum(i, nt - 1), 0))
    behind = lambda width: pl.BlockSpec((tm, width), lambda i: (jnp.maximum(i - 1, 0), 0))
    return pl.pallas_call(
        body,
        grid=(nt + 1,),
        in_specs=[
            ahead(D_MODEL),
            ahead(D_FF),
            _weight_spec((D_FF, D_MODEL)),
            behind(D_PLE),
            behind(D_MODEL),
            _weight_spec((D_MODEL, D_MODEL)),
            _weight_spec((D_MODEL, D_PLE)),
            _const_spec((1, D_MODEL)),
            _const_spec((1, D_MODEL)),
            _const_spec((1, D_MODEL)),
        ],
        out_specs=[
            behind(D_MODEL),
            behind(D_MODEL),
            behind(D_MODEL),
            behind(D_MODEL),
            behind(D_MODEL),
            behind(D_PLE),
            _const_spec((8, D_MODEL)),
        ],
        out_shape=[
            jax.ShapeDtypeStruct((s, D_MODEL), F32),
            jax.ShapeDtypeStruct((s, D_MODEL), BF16),
            jax.ShapeDtypeStruct((s, D_MODEL), BF16),
            jax.ShapeDtypeStruct((s, D_MODEL), BF16),
            jax.ShapeDtypeStruct((s, D_MODEL), BF16),
            jax.ShapeDtypeStruct((s, D_PLE), BF16),
            jax.ShapeDtypeStruct((8, D_MODEL), F32),
        ],
        scratch_shapes=[pltpu.VMEM((tm, D_MODEL), F32), pltpu.VMEM((tm, D_MODEL), F32)],
        compiler_params=_params(("arbitrary",)),
        name="down_ple",
    )(x1, act, w_down, p, tgt, w_pg, w_pu_t, g_gate, g_post, g_final)


def _ffn_bwd(dx2, dx2b, x1, g_sav, u_sav, w_gu_t, w_down, g_ffn, tm, carry=None):
    s = x1.shape[0]

    def body(dx2_ref, dx2b_ref, x1_ref, g_ref, u_ref, w_ref, wd_ref, gffn_ref,
             dg_ref, du_ref, dx1_ref, dx1b_ref, stats_ref):
        @pl.when(pl.program_id(0) == 0)
        def _():
            stats_ref[...] = jnp.zeros_like(stats_ref)

        dx2b = dx2b_ref[...]
        nc = D_FF // FF_CHUNK
        dacts = [_dot_nt(dx2b, wd_ref[c * FF_CHUNK:(c + 1) * FF_CHUNK, :]) for c in range(nc)]
        dh2 = jnp.zeros((tm, D_MODEL), F32)
        for c in range(nc):
            cols = slice(c * FF_CHUNK, (c + 1) * FF_CHUNK)
            g = g_ref[:, cols].astype(F32)
            u = u_ref[:, cols].astype(F32)
            sg = _sigmoid(g)
            dg = (dacts[c] * u * sg * (1.0 + g * (1.0 - sg))).astype(BF16)
            du = (dacts[c] * g * sg).astype(BF16)
            dg_ref[:, cols] = dg
            du_ref[:, cols] = du
            dh2 = dh2 + _dot(dg, w_ref[c * FF_CHUNK:(c + 1) * FF_CHUNK, :])
            dh2 = dh2 + _dot(du, w_ref[D_FF + c * FF_CHUNK:D_FF + (c + 1) * FF_CHUNK, :])

        x1 = x1_ref[...]
        r2 = lax.rsqrt(jnp.mean(x1 * x1, axis=-1, keepdims=True) + EPS)
        dxn, dgf = _rms_bwd(x1, r2, gffn_ref[...], dh2)
        dx1 = dx2_ref[...] + dxn
        dx1_ref[...] = dx1
        dx1b_ref[...] = dx1.astype(BF16)
        stats_ref[0:1, :] += dgf

    return _pcall(
        body,
        grid=(s // tm,),
        in_specs=[
            _row_spec(tm, D_MODEL), _row_spec(tm, D_MODEL), _row_spec(tm, D_MODEL),
            _row_spec(tm, D_FF), _row_spec(tm, D_FF),
            _weight_spec((2 * D_FF, D_MODEL)), _weight_spec((D_FF, D_MODEL)), _const_spec((1, D_MODEL)),
        ],
        out_specs=[_row_spec(tm, D_FF), _row_spec(tm, D_FF), _row_spec(tm, D_MODEL), _row_spec(tm, D_MODEL),
                   _const_spec((8, D_MODEL))],
        out_shape=[
            jax.ShapeDtypeStruct((s, D_FF), BF16),
            jax.ShapeDtypeStruct((s, D_FF), BF16),
            jax.ShapeDtypeStruct((s, D_MODEL), F32),
            jax.ShapeDtypeStruct((s, D_MODEL), BF16),
            jax.ShapeDtypeStruct((8, D_MODEL), F32),
        ],
        scratch_shapes=[],
        name="ffn_bwd",
        args=(dx2, dx2b, x1, g_sav, u_sav, w_gu_t, w_down, g_ffn),
        carry=carry,
    )


def _bwd_mix(dx1, dx1b, x, z, u1, pooled, w_in_t, w_out, g_mix, conv_w, ln_g, ln_b, pool_w, pool_scale, tm,
             carry=None):
    s = x.shape[0]
    nt = s // tm

    def body(dx1_ref, dx1b_ref, x_ref, z_ref, u1_ref, pooled_ref, win_ref, wout_ref, gmix_ref, cw_ref,
             lng_ref, lnb_ref, pw_ref, ps_ref,
             dx_ref, dz_ref, vec_ref, dcw_ref, dpw_ref, dubuf, dvbuf, u0buf, du0buf, dush):
        i = pl.program_id(0)
        tile = nt - 1 - i

        @pl.when(i == 0)
        def _():
            vec_ref[...] = jnp.zeros_like(vec_ref)
            dcw_ref[...] = jnp.zeros_like(dcw_ref)
            dpw_ref[...] = jnp.zeros_like(dpw_ref)
            dubuf[tm:tm + CONV_HALO, :] = jnp.zeros((CONV_HALO, C_CONV), F32)
            dvbuf[tm:tm + POOL_HALO, :] = jnp.zeros((POOL_HALO, C_POOL), F32)

        dmix = _dot_nt(dx1b_ref[...], wout_ref[...])
        du = dmix[:, :C_CONV]
        dq = dmix[:, C_CONV:]

        pos1 = (tile * tm + lax.broadcasted_iota(jnp.int32, (tm, 1), 0) + 1).astype(F32)
        dpooled_parts = []
        dps_rows = []
        for g, w in enumerate(POOL_WINDOWS):
            cols = slice(g * POOL_GROUP, (g + 1) * POOL_GROUP)
            pooled_b = pooled_ref[:, cols]
            mixed = _dot(pooled_b, pw_ref[g])
            dqg = dq[:, cols]
            dps_rows.append(jnp.sum(dqg * mixed, axis=0, keepdims=True))
            dmixed = (dqg * ps_ref[:, cols]).astype(BF16)
            dpw_ref[g] += _dot_tn(pooled_b, dmixed)
            dpooled = _dot_nt(dmixed, pw_ref[g])
            dpooled_parts.append(dpooled)
            dvbuf[0:tm, cols] = dpooled / jnp.minimum(pos1, float(w))
        vec_ref[4:5, 0:C_POOL] += jnp.concatenate(dps_rows, axis=-1)
        dv_parts = []
        for g, w in enumerate(POOL_WINDOWS):
            cols = slice(g * POOL_GROUP, (g + 1) * POOL_GROUP)
            tot = dvbuf[0:tm, cols]
            for j in range(1, w):
                tot = tot + dvbuf[j:j + tm, cols]
            dv_parts.append(tot - dpooled_parts[g])

        u1 = u1_ref[...]
        mu = jnp.mean(u1, axis=-1, keepdims=True)
        cen = u1 - mu
        rstd = lax.rsqrt(jnp.mean(cen * cen, axis=-1, keepdims=True) + EPS)
        xhat = cen * rstd
        u2 = xhat * lng_ref[...] + lnb_ref[...]
        sg2 = _sigmoid(u2)
        du2 = du * sg2 * (1.0 + u2 * (1.0 - sg2))
        vec_ref[1:2, 0:C_CONV] += jnp.sum(du2 * xhat, axis=0, keepdims=True)
        vec_ref[2:3, 0:C_CONV] += jnp.sum(du2, axis=0, keepdims=True)
        t1 = du2 * lng_ref[...]
        du1 = rstd * (t1 - jnp.mean(t1, axis=-1, keepdims=True)
                      - xhat * jnp.mean(t1 * xhat, axis=-1, keepdims=True))
        vec_ref[3:4, 0:C_CONV] += jnp.sum(du1, axis=0, keepdims=True)
        dubuf[0:tm, :] = du1

        zt = z_ref[...]
        a = zt[:, :C_CONV]
        sgb = _sigmoid(zt[:, C_CONV:2 * C_CONV])
        u0buf[...] = a * sgb

        _shifted_copies(dubuf, dush, tm)
        for rc in range(tm // ROW_CHUNK):
            r0 = rc * ROW_CHUNK
            acc = jnp.zeros((ROW_CHUNK, C_CONV), F32)
            for k in range(CONV_K):
                acc = acc + cw_ref[k:k + 1, :] * _rows_at(dubuf, dush, r0 + (CONV_K - 1) - k)
            du0buf[r0:r0 + ROW_CHUNK, :] = acc
        for k in range(CONV_K):
            acc = jnp.zeros((ROW_CHUNK, C_CONV), F32)
            for rc in range(tm // ROW_CHUNK):
                r0 = rc * ROW_CHUNK
                acc = acc + u0buf[r0:r0 + ROW_CHUNK, :] * _rows_at(dubuf, dush, r0 + (CONV_K - 1) - k)
            dcw_ref[k:k + 1, :] += jnp.sum(acc, axis=0, keepdims=True)
        du0 = du0buf[...]

        da = du0 * sgb
        db = du0 * a * sgb * (1.0 - sgb)
        dz = jnp.concatenate([da, db] + dv_parts, axis=-1).astype(BF16)
        dz_ref[...] = dz

        xt = x_ref[...]
        r1 = lax.rsqrt(jnp.mean(xt * xt, axis=-1, keepdims=True) + EPS)
        dh1 = _dot(dz, win_ref[...])
        dxn, dgm = _rms_bwd(xt, r1, gmix_ref[...], dh1)
        dx_ref[...] = dx1_ref[...] + dxn
        vec_ref[0:1, :] += dgm

        dubuf[tm:tm + CONV_HALO, :] = dubuf[0:CONV_HALO, :]
        dvbuf[tm:tm + POOL_HALO, :] = dvbuf[0:POOL_HALO, :]

    rev = lambda width: pl.BlockSpec((tm, width), lambda i: (nt - 1 - i, 0))
    return _pcall(
        body,
        grid=(nt,),
        in_specs=[
            rev(D_MODEL), rev(D_MODEL), rev(D_MODEL), rev(Z_WIDTH), rev(C_CONV), rev(C_POOL),
            _const_spec((Z_WIDTH, D_MODEL)),
            _const_spec((D_MODEL, D_MODEL)),
            _const_spec((1, D_MODEL)),
            _const_spec((CONV_HALO, C_CONV)),
            _const_spec((1, C_CONV)),
            _const_spec((1, C_CONV)),
            _const_spec((len(POOL_WINDOWS), POOL_GROUP, POOL_GROUP)),
            _const_spec((1, C_POOL)),
        ],
        out_specs=[
            rev(D_MODEL), rev(Z_WIDTH),
            _const_spec((8, D_MODEL)),
            _const_spec((CONV_HALO, C_CONV)),
            _const_spec((len(POOL_WINDOWS), POOL_GROUP, POOL_GROUP)),
        ],
        out_shape=[
            jax.ShapeDtypeStruct((s, D_MODEL), F32),
            jax.ShapeDtypeStruct((s, Z_WIDTH), BF16),
            jax.ShapeDtypeStruct((8, D_MODEL), F32),
            jax.ShapeDtypeStruct((CONV_HALO, C_CONV), F32),
            jax.ShapeDtypeStruct((len(POOL_WINDOWS), POOL_GROUP, POOL_GROUP), F32),
        ],
        scratch_shapes=[
            pltpu.VMEM((tm + CONV_HALO, C_CONV), F32),
            pltpu.VMEM((tm + POOL_HALO, C_POOL), F32),
            pltpu.VMEM((tm, C_CONV), F32),
            pltpu.VMEM((tm, C_CONV), F32),
            pltpu.VMEM((7, tm + CONV_HALO, C_CONV), F32),
        ],
        name="bwd_mix",
        args=(dx1, dx1b, x, z, u1, pooled, w_in_t, w_out, g_mix, conv_w, ln_g, ln_b, pool_w, pool_scale),
        carry=carry,
    )


def _grad_matmul(a, b, bm, name, a2=None, carry=None):
    s, ma = a.shape
    nb = b.shape[1]
    na = ma // bm
    if a2 is None:
        def body(a_ref, b_ref, o_ref):
            o_ref[...] = _dot_tn(a_ref[...], b_ref[...]).astype(BF16)

        lhs_specs = [pl.BlockSpec((s, bm), lambda i: (0, i))]
        lhs = (a,)
        steps = na
    else:
        def body(a_ref, a2_ref, b_ref, o_ref):
            i = pl.program_id(0)

            @pl.when(i < na)
            def _():
                o_ref[...] = _dot_tn(a_ref[...], b_ref[...]).astype(BF16)

            @pl.when(i >= na)
            def _():
                o_ref[...] = _dot_tn(a2_ref[...], b_ref[...]).astype(BF16)

        lhs_specs = [pl.BlockSpec((s, bm), lambda i: (0, jnp.minimum(i, na - 1))),
                     pl.BlockSpec((s, bm), lambda i: (0, jnp.maximum(i - na, 0)))]
        lhs = (a, a2)
        steps = 2 * na

    outs, carried = _pcall(
        body,
        grid=(steps,),
        in_specs=lhs_specs + [pl.BlockSpec((s, nb), lambda i: (0, 0))],
        out_specs=[pl.BlockSpec((bm, nb), lambda i: (i, 0))],
        out_shape=[jax.ShapeDtypeStruct((steps * bm, nb), BF16)],
        scratch_shapes=[],
        name=name,
        args=lhs + (b,),
        carry=carry,
    )
    return outs[0], carried


def _reduce_scatter(grads, small, name):
    n, ns = len(grads), len(small)
    shapes = [g.shape[1:] for g in grads]
    gather = _Carry()
    _carry_gather(gather, small, 1, 2)

    def body(*refs):
        g = refs[:n]
        out = refs[n + ns:2 * n + ns]
        scr = refs[2 * (n + ns):]
        own, loc, r1, r2 = scr[:n], scr[n:2 * n], scr[2 * n:3 * n], scr[3 * n:4 * n]
        load_sems, s1, q1, s2, q2 = scr[4 * n:4 * n + 5]
        gather_refs = (refs[n:n + ns], refs[2 * n + ns:2 * (n + ns)], scr[4 * n + 5:])
        me, sib, chips = _place()
        c = me[2]

        gather.stage(0, *gather_refs)
        loads = []
        sends = []
        for a in range(n):
            ld = [pltpu.make_async_copy(g[a].at[_block(*chip, c)], loc[a].at[j], load_sems.at[a, j])
                  for j, chip in enumerate(chips)]
            ld.append(pltpu.make_async_copy(g[a].at[_block(*me)], own[a], load_sems.at[a, 3]))
            for cp in ld:
                cp.start()
            loads.append(ld)
            blocks = [(*chip, 1 - c) for chip in chips] + [sib]
            for j, blk in enumerate(blocks):
                cp = pltpu.make_async_remote_copy(
                    src_ref=g[a].at[_block(*blk)], dst_ref=r1[a].at[j],
                    send_sem=s1.at[a, j], recv_sem=q1.at[a, j], device_id=sib, device_id_type=MESH)
                cp.start()
                sends.append(cp)

        def from_sibling(a, j):
            return pltpu.make_async_remote_copy(
                src_ref=r1[a].at[j], dst_ref=r1[a].at[j], send_sem=s1.at[a, j], recv_sem=q1.at[a, j],
                device_id=sib, device_id_type=MESH)

        def partial(a, j, chip):
            return pltpu.make_async_remote_copy(
                src_ref=loc[a].at[j], dst_ref=r2[a].at[j], send_sem=s2.at[a, j], recv_sem=q2.at[a, j],
                device_id=(*chip, c), device_id_type=MESH)

        gather.stage(1, *gather_refs)
        for a in range(n):
            for j, chip in enumerate(chips):
                loads[a][j].wait()
                from_sibling(a, j).wait_recv()
                loc[a][j] = (loc[a][j].astype(F32) + r1[a][j].astype(F32)).astype(BF16)
                cp = partial(a, j, chip)
                cp.start()
                sends.append(cp)
        gather.stage(2, *gather_refs)
        for a in range(n):
            loads[a][3].wait()
            from_sibling(a, 3).wait_recv()
            acc = own[a][...].astype(F32) + r1[a][3].astype(F32)
            for j, chip in enumerate(chips):
                partial(a, j, chip).wait_recv()
                acc = acc + r2[a][j].astype(F32)
            out[a][...] = acc
        for cp in sends:
            cp.wait_send()
        gather.drain(*gather_refs)

    any_spec = pl.BlockSpec(memory_space=pl.ANY)
    vmem_spec = pl.BlockSpec(memory_space=pltpu.VMEM)
    res = pl.pallas_call(
        body,
        in_specs=[any_spec] * (n + ns),
        out_specs=[vmem_spec] * n + [any_spec] * ns,
        out_shape=[jax.ShapeDtypeStruct(sh, F32) for sh in shapes] + gather.out_shapes,
        scratch_shapes=(
            [pltpu.VMEM(sh, BF16) for sh in shapes]
            + [pltpu.VMEM((3,) + sh, BF16) for sh in shapes]
            + [pltpu.VMEM((4,) + sh, BF16) for sh in shapes]
            + [pltpu.VMEM((3,) + sh, BF16) for sh in shapes]
            + [pltpu.SemaphoreType.DMA((n, 4)),
               pltpu.SemaphoreType.DMA((n, 4)), pltpu.SemaphoreType.DMA((n, 4)),
               pltpu.SemaphoreType.DMA((n, 3)), pltpu.SemaphoreType.DMA((n, 3))]
            + gather.sem_shapes()
        ),
        compiler_params=pltpu.CompilerParams(vmem_limit_bytes=V7X_VMEM_LIMIT),
        name=name,
    )(*grads, *small)
    return res[:n], res[n:]


def _pair_add(grads, from_sib, blks, name):
    n = len(grads)

    def body(blk_ref, *refs):
        for a in range(n):
            refs[2 * n + a][...] = (refs[a][...].astype(F32) + refs[n + a][...].astype(F32)).astype(BF16)

    mine = [pl.BlockSpec((None,) + g.shape[1:], lambda j, b: (b[j], 0, 0)) for g in grads]
    same = [pl.BlockSpec((None,) + g.shape[1:], lambda j, b: (j, 0, 0)) for g in grads]
    return pl.pallas_call(
        body,
        grid_spec=pltpu.PrefetchScalarGridSpec(
            num_scalar_prefetch=1, grid=(4,), in_specs=mine + same, out_specs=same),
        out_shape=[jax.ShapeDtypeStruct((4,) + g.shape[1:], BF16) for g in grads],
        compiler_params=_params(("arbitrary",)),
        name=name,
    )(blks, *grads, *from_sib)


def _chip_sum(parts, from_chips, name):
    n = len(parts)

    def body(*refs):
        for a in range(n):
            acc = refs[a][...].astype(F32)
            for j in range(3):
                acc = acc + refs[n + a][j].astype(F32)
            refs[2 * n + a][...] = acc

    half = [p.shape[1] // 2 for p in parts]
    return pl.pallas_call(
        body,
        grid=(2,),
        in_specs=([pl.BlockSpec((None, h, p.shape[2]), lambda i: (3, i, 0)) for p, h in zip(parts, half)]
                  + [pl.BlockSpec((3, h, p.shape[2]), lambda i: (0, i, 0)) for p, h in zip(parts, half)]),
        out_specs=[pl.BlockSpec((h, p.shape[2]), lambda i: (i, 0)) for p, h in zip(parts, half)],
        out_shape=[jax.ShapeDtypeStruct(p.shape[1:], F32) for p in parts],
        compiler_params=_params(("arbitrary",)),
        name=name,
    )(*parts, *from_chips)


def _adam_math(w, g, m, v):
    nm = ADAM_B1 * m + (1.0 - ADAM_B1) * g
    nv = ADAM_B2 * v + (1.0 - ADAM_B2) * (g * g)
    m_hat = nm / (1.0 - ADAM_B1 ** ADAM_STEP)
    v_hat = nv / (1.0 - ADAM_B2 ** ADAM_STEP)
    return -ADAM_LR * (m_hat / (jnp.sqrt(v_hat) + ADAM_EPS) + ADAM_WD * w), nm, nv


def _sum_adamw(items, name, carry=None):
    n = len(items)

    def body(*refs):
        for a in range(n):
            p_ref, f_ref, w_ref, m_ref, v_ref = refs[5 * a:5 * a + 5]
            g_ref, d_ref, nm_ref, nv_ref = refs[5 * n + 4 * a:5 * n + 4 * a + 4]
            g = p_ref[...].astype(F32)
            for j in range(3):
                g = g + f_ref[j].astype(F32)
            g_ref[...] = g
            d_ref[...], nm_ref[...], nv_ref[...] = _adam_math(w_ref[...], g, m_ref[...], v_ref[...])

    in_specs, out_specs, out_shape, args = [], [], [], []
    for part, from_chips, w, m, v in items:
        r, c = w.shape
        spec = pl.BlockSpec((r // 2, c), lambda i: (i, 0))
        in_specs += [pl.BlockSpec((None, r // 2, c), lambda i: (3, i, 0)),
                     pl.BlockSpec((3, r // 2, c), lambda i: (0, i, 0)), spec, spec, spec]
        out_specs += [spec] * 4
        out_shape += [jax.ShapeDtypeStruct((r, c), F32)] * 4
        args += [part, from_chips, w, m, v]
    outs, carried = _pcall(body, grid=(2,), in_specs=in_specs, out_specs=out_specs, out_shape=out_shape,
                           scratch_shapes=[], name=name, args=tuple(args), carry=carry)
    return [tuple(outs[4 * a:4 * a + 4]) for a in range(n)], carried


def _small_update(gathered, layout, params, name):
    ng, npar = len(gathered), len(params)

    def body(*refs):
        parts = refs[:ng]
        prm = refs[ng:ng + 3 * npar]
        tot_refs = refs[ng + 3 * npar:2 * ng + 3 * npar]
        out = refs[2 * ng + 3 * npar:]
        tots = []
        for a in range(ng):
            acc = parts[a][0]
            for d in range(1, N_DEV):
                acc = acc + parts[a][d]
            tot_refs[a][...] = acc
            tots.append(acc)
        for i, (a, row, width) in enumerate(layout):
            g = tots[a] if row is None else tots[a][row:row + 1, :width]
            delta, nm, nv = _adam_math(prm[3 * i][...], g, prm[3 * i + 1][...], prm[3 * i + 2][...])
            out[4 * i][...] = g
            out[4 * i + 1][...] = delta
            out[4 * i + 2][...] = nm
            out[4 * i + 3][...] = nv

    flat = [t for prm in params for t in prm]
    res = pl.pallas_call(
        body,
        out_shape=([jax.ShapeDtypeStruct(g.shape[1:], F32) for g in gathered]
                   + [jax.ShapeDtypeStruct(prm[0].shape, F32) for prm in params for _ in range(4)]),
        compiler_params=pltpu.CompilerParams(vmem_limit_bytes=V7X_VMEM_LIMIT),
        name=name,
    )(*gathered, *flat)
    return res[:ng], [tuple(res[ng + 4 * i:ng + 4 * i + 4]) for i in range(npar)]


def _adamw(w, g, m, v, name):
    rows, cols = w.shape
    br = rows
    for cand in (512, 256, 128):
        if rows % cand == 0 and rows > cand:
            br = cand
            break

    def body(w_ref, g_ref, m_ref, v_ref, d_ref, nm_ref, nv_ref):
        d_ref[...], nm_ref[...], nv_ref[...] = _adam_math(w_ref[...], g_ref[...], m_ref[...], v_ref[...])

    spec = pl.BlockSpec((br, cols), lambda i: (i, 0))
    shape = jax.ShapeDtypeStruct((rows, cols), F32)
    return pl.pallas_call(
        body,
        grid=(rows // br,),
        in_specs=[spec] * 4,
        out_specs=[spec] * 3,
        out_shape=[shape] * 3,
        compiler_params=_params(("arbitrary",)),
        name=name,
    )(w, g, m, v)


def _by_device(full):
    return full.reshape(N_DEV, full.shape[0] // N_DEV, full.shape[1])


def kernel(x, p, g_mix, w_in, conv_w, conv_b, ln_g, ln_b, pool_w, pool_scale, w_out, g_ffn, w_gate_up, w_down, g_ple_gate, w_ple_gate, w_ple_up, g_ple_post, g_final, loss_target, m_g_mix, m_w_in, m_conv_w, m_conv_b, m_ln_g, m_ln_b, m_pool_w, m_pool_scale, m_w_out, m_g_ffn, m_w_gate_up, m_w_down, m_g_ple_gate, m_w_ple_gate, m_w_ple_up, m_g_ple_post, m_g_final, v_g_mix, v_w_in, v_conv_w, v_conv_b, v_ln_g, v_ln_b, v_pool_w, v_pool_scale, v_w_out, v_g_ffn, v_w_gate_up, v_w_down, v_g_ple_gate, v_w_ple_gate, v_w_ple_up, v_g_ple_post, v_g_final):
    seq = x.shape[1]
    xs = x[0]
    ps = p[0, 0]
    tgt = loss_target[0]
    ax, ay, ac = lax.axis_index("x"), lax.axis_index("y"), lax.axis_index("c")
    me = _block(ax, ay, ac)
    blks = jnp.stack([_block(1 - ax, ay, ac), _block(ax, 1 - ay, ac), _block(1 - ax, 1 - ay, ac), me]).astype(jnp.int32)
    rows = lambda gth: gth.reshape((-1,) + gth.shape[2:])

    sh_in, sh_out, sh_gu, sh_down, sh_pg, sh_pu, pool_w_b = _to_bf16([
        w_in[0].T,
        w_out[0],
        w_gate_up[0].T,
        w_down[0],
        w_ple_gate[0],
        w_ple_up[0].T,
        pool_w[0]])
    carry = _Carry()
    _carry_gather(carry, [sh_in, sh_out,
                          jnp.pad(conv_w[0].T, ((0, 0), (0, CONV_HALO - CONV_K)))], 8, 15)
    (h1,), first = _norm_in(xs, g_mix, min(256, seq), carry)
    w_in_t, w_out_f, conv_w_t = [rows(gth) for gth in first]
    conv_w_f = conv_w_t.T

    carry = _Carry()
    _carry_gather(carry, [sh_gu], 8, 12)
    (z, u1, pooled, x1, mix), (w_gu_all,) = _fwd_mix(
        xs, h1, w_in_t, w_out_f, conv_w_f, conv_b, ln_g, ln_b, pool_w_b, pool_scale, min(256, seq), carry)
    w_gu_t = rows(w_gu_all)

    carry = _Carry()
    _carry_gather(carry, [sh_down, sh_pg, sh_pu], 6, 10)
    (h2, g_sav, u_sav, act), late = _ffn_up(x1, w_gu_t, g_ffn, min(256, seq), carry)
    w_down_f, w_pg_f, w_pu_t = [rows(gth) for gth in late]
    dx2, dx2b, hg, ds, dpe, pb, stats_ple = _down_ple(
        x1, act, w_down_f, ps, tgt, w_pg_f, w_pu_t, g_ple_gate, g_ple_post, g_final.reshape(1, D_MODEL),
        min(256, seq))
    dg, du, dx1, dx1b, stats_ffn = _ffn_bwd(dx2, dx2b, x1, g_sav, u_sav, w_gu_t, w_down_f, g_ffn,
                                            min(256, seq))[0]

    d_w_pg, _ = _grad_matmul(hg, ds, 256, "grad_w_ple_gate")
    d_w_pu_t, _ = _grad_matmul(dpe, pb, 256, "grad_w_ple_up")
    d_w_gu_t, _ = _grad_matmul(dg, h2, 256, "grad_w_gate_up", a2=du)
    d_w_out, _ = _grad_matmul(mix, dx1b, 256, "grad_w_out")
    early = [_by_device(d_w_pg), _by_device(d_w_pu_t), _by_device(d_w_gu_t), _by_device(d_w_out)]
    carry = _Carry()
    _carry_pair(carry, early)
    d_w_down, from_sib = _grad_matmul(act, dx2b, 256, "grad_w_down", carry=carry)
    early_parts = _pair_add(early, from_sib, blks, "pair_add_early")

    carry = _Carry()
    _carry_chip(carry, early_parts)
    _carry_pair(carry, [_by_device(d_w_down)])
    (dx, dz, vec_mix, dconv_w_part, dpool_w_part), carried = _bwd_mix(
        dx1, dx1b, xs, z, u1, pooled, w_in_t, w_out_f, g_mix, conv_w_f, ln_g, ln_b, pool_w_b, pool_scale,
        min(256, seq), carry)
    early_chips, down_sib = carried[:4], carried[4:]
    down_parts = _pair_add([_by_device(d_w_down)], down_sib, blks, "pair_add_down")

    carry = _Carry()
    _carry_chip(carry, down_parts)
    d_w_in_t, down_chips = _grad_matmul(dz, h1, 256, "grad_w_in", carry=carry)
    (gr_w_in_t,), small = _reduce_scatter(
        [_by_device(d_w_in_t)],
        [vec_mix, stats_ple, stats_ffn, dconv_w_part, dpool_w_part.reshape(-1, POOL_GROUP)], "scatter_last")
    small = list(small[:4]) + [small[4].reshape((N_DEV,) + dpool_w_part.shape)]

    vec_names = ["g_mix", "ln_g", "ln_b", "conv_b", "pool_scale", "g_final", "g_ple_post", "g_ple_gate", "g_ffn",
                 "pool_w"]
    layout = [(0, 0, D_MODEL), (0, 1, C_CONV), (0, 2, C_CONV), (0, 3, C_CONV), (0, 4, C_POOL),
              (1, 0, D_MODEL), (1, 1, D_MODEL), (1, 2, D_MODEL), (2, 0, D_MODEL), (4, None, None)]
    as_row = lambda t: t.reshape(1, D_MODEL)
    params = [(g_mix, m_g_mix, v_g_mix), (ln_g, m_ln_g, v_ln_g), (ln_b, m_ln_b, v_ln_b),
              (conv_b, m_conv_b, v_conv_b), (pool_scale, m_pool_scale, v_pool_scale),
              (as_row(g_final), as_row(m_g_final), as_row(v_g_final)),
              (g_ple_post, m_g_ple_post, v_g_ple_post), (g_ple_gate, m_g_ple_gate, v_g_ple_gate),
              (g_ffn, m_g_ffn, v_g_ffn), (pool_w[0], m_pool_w[0], v_pool_w[0])]
    tots, small_upd = _small_update(small, layout, params, "small_update")
    loss = tots[1][3, 0]
    upd = {}
    for nm, res, prm in zip(vec_names, small_upd, [g_mix, ln_g, ln_b, conv_b, pool_scale, g_final, g_ple_post,
                                                    g_ple_gate, g_ffn, pool_w]):
        upd[nm] = tuple(t.reshape(prm.shape) for t in res)
    gr_conv_w = lax.dynamic_slice_in_dim(tots[3][:CONV_K], me * (C_CONV // N_DEV), C_CONV // N_DEV, axis=1)

    natural = lambda t: t[None]
    turned = lambda t: t.T[None]
    (res_pg, res_gu, res_out, res_down), _ = _sum_adamw([
        (early_parts[0], early_chips[0], w_ple_gate[0], m_w_ple_gate[0], v_w_ple_gate[0]),
        (early_parts[2], early_chips[2], w_gate_up[0].T, m_w_gate_up[0].T, v_w_gate_up[0].T),
        (early_parts[3], early_chips[3], w_out[0], m_w_out[0], v_w_out[0]),
        (down_parts[0], down_chips[0], w_down[0], m_w_down[0], v_w_down[0])], "adamw_big")
    upd["w_ple_gate"] = tuple(natural(t) for t in res_pg)
    upd["w_gate_up"] = tuple(turned(t) for t in res_gu)
    upd["w_out"] = tuple(natural(t) for t in res_out)
    upd["w_down"] = tuple(natural(t) for t in res_down)
    (gr_w_pu_t,) = _chip_sum(early_parts[1:2], early_chips[1:2], "chip_sum")
    plain = [
        ("w_in", w_in[0].T, gr_w_in_t, m_w_in[0].T, v_w_in[0].T, True),
        ("w_ple_up", w_ple_up[0], gr_w_pu_t.T, m_w_ple_up[0], v_w_ple_up[0], False),
        ("conv_w", conv_w[0], gr_conv_w, m_conv_w[0], v_conv_w[0], False),
    ]
    for nm, w_, g_, m_, v_, transposed in plain:
        res = (g_,) + tuple(_adamw(w_, g_, m_, v_, "adamw_" + nm))
        upd[nm] = tuple((t.T if transposed else t)[None] for t in res)

    order = ["g_mix", "w_in", "conv_w", "conv_b", "ln_g", "ln_b", "pool_w", "pool_scale", "w_out", "g_ffn",
             "w_gate_up", "w_down", "g_ple_gate", "w_ple_gate", "w_ple_up", "g_ple_post", "g_final"]
    outs = [loss, dx[None]]
    for k in range(4):
        outs += [upd[nm][k] for nm in order]
    return tuple(outs)
```

```python
import functools

import jax
import jax.numpy as jnp
from jax import lax
from jax.experimental import pallas as pl
from jax.experimental.pallas import tpu as pltpu

D_MODEL = 1024
C_CONV = 512
C_POOL = 512
Z_WIDTH = 2 * C_CONV + C_POOL
POOL_WINDOWS = (2, 4, 8, 16)
POOL_GROUP = 128
CONV_K = 31
D_FF = 2816
D_PLE = 256
EPS = 1e-6
N_DEV = 8

ADAM_LR = 0.001
ADAM_B1 = 0.9
ADAM_B2 = 0.999
ADAM_EPS = 1e-08
ADAM_WD = 0.01
ADAM_STEP = 10

CONV_HALO = 32
POOL_HALO = 32
ROW_CHUNK = 32
V7X_VMEM_LIMIT = 56 * 1024 * 1024
FF_CHUNK = D_FF // 2

BF16 = jnp.bfloat16
F32 = jnp.float32
MESH = pl.DeviceIdType.MESH


def _dot(a, b):
    return lax.dot_general(a, b, (((1,), (0,)), ((), ())), preferred_element_type=F32)


def _dot_nt(a, b):
    return lax.dot_general(a, b, (((1,), (1,)), ((), ())), preferred_element_type=F32)


def _dot_tn(a, b):
    return lax.dot_general(a, b, (((0,), (0,)), ((), ())), preferred_element_type=F32)


def _rms_fwd(x, g):
    r = lax.rsqrt(jnp.mean(x * x, axis=-1, keepdims=True) + EPS)
    return x * r * g, r


def _rms_bwd(x, r, g, dy):
    xr = x * r
    dg = jnp.sum(dy * xr, axis=0, keepdims=True)
    dyg = dy * g
    dx = r * (dyg - xr * jnp.mean(dyg * xr, axis=-1, keepdims=True))
    return dx, dg


def _sigmoid(x):
    return jax.nn.sigmoid(x)


def _params(sem=None):
    return pltpu.CompilerParams(dimension_semantics=sem, vmem_limit_bytes=V7X_VMEM_LIMIT)


def _place():
    x, y, c = lax.axis_index("x"), lax.axis_index("y"), lax.axis_index("c")
    chips = [(1 - x, y), (x, 1 - y), (1 - x, 1 - y)]
    return (x, y, c), (x, y, 1 - c), chips


def _block(px, py, pc):
    return 4 * px + 2 * py + pc


class _Carry:
    def __init__(self):
        self.inputs = []
        self.out_shapes = []
        self.copies = []
        self.locals = []

    def add_input(self, arr):
        self.inputs.append(arr)
        return len(self.inputs) - 1

    def add_output(self, shape, dtype):
        self.out_shapes.append(jax.ShapeDtypeStruct(shape, dtype))
        return len(self.out_shapes) - 1

    def local(self, src_idx, dst_idx, dst_blk):
        self.locals.append((src_idx, dst_idx, dst_blk))

    def copy(self, src, dst_idx, dst_blk, got_blk, peer, step=0, after=()):
        self.copies.append(dict(src=src, dst_idx=dst_idx, dst_blk=dst_blk, got_blk=got_blk, peer=peer, step=step,
                                after=tuple(after)))
        return len(self.copies) - 1

    def sem_shapes(self):
        return [pltpu.SemaphoreType.DMA((max(1, len(self.copies)),)),
                pltpu.SemaphoreType.DMA((max(1, len(self.copies)),)),
                pltpu.SemaphoreType.DMA((max(1, len(self.locals)),))]

    @staticmethod
    def _view(ref, where):
        if isinstance(where, tuple):
            blk, row0, nrows = where
            return ref.at[blk, pl.ds(row0, nrows)]
        return ref.at[where]

    def _desc(self, k, ins, outs, sems, place):
        cp = self.copies[k]
        me, sib, chips = place
        kind, idx, blk = cp["src"]
        src = (ins if kind == "in" else outs)[idx]
        if blk is not None:
            src = self._view(src, blk(*place))
        to = sib if cp["peer"] == "sib" else (*chips[cp["peer"]], me[2])
        return pltpu.make_async_remote_copy(
            src_ref=src, dst_ref=self._view(outs[cp["dst_idx"]], cp["dst_blk"](*place)),
            send_sem=sems[0].at[k], recv_sem=sems[1].at[k], device_id=to, device_id_type=MESH)

    def _arrival(self, k, outs, sems, place):
        cp = self.copies[k]
        got = self._view(outs[cp["dst_idx"]], cp["got_blk"](*place))
        return pltpu.make_async_remote_copy(
            src_ref=got, dst_ref=got, send_sem=sems[0].at[k], recv_sem=sems[1].at[k],
            device_id=place[0], device_id_type=MESH)

    def _local(self, n, ins, outs, sems, place):
        src_idx, dst_idx, blk = self.locals[n]
        return pltpu.make_async_copy(ins[src_idx], outs[dst_idx].at[blk(*place)], sems[2].at[n])

    def stages(self):
        return sorted({0} | {cp["step"] for cp in self.copies})

    def stage(self, s, ins, outs, sems):
        place = _place()
        if s == 0:
            self._waited = set()
            for n in range(len(self.locals)):
                self._local(n, ins, outs, sems, place).start()
        for k, cp in enumerate(self.copies):
            if cp["step"] != s:
                continue
            for a in cp["after"]:
                if a not in self._waited:
                    self._arrival(a, outs, sems, place).wait_recv()
                    self._waited.add(a)
            self._desc(k, ins, outs, sems, place).start()

    def drain(self, ins, outs, sems):
        place = _place()
        for k in range(len(self.copies)):
            if k not in self._waited:
                self._arrival(k, outs, sems, place).wait_recv()
        for k in range(len(self.copies)):
            self._desc(k, ins, outs, sems, place).wait_send()
        for n in range(len(self.locals)):
            self._local(n, ins, outs, sems, place).wait()

    def starts(self, step, nsteps, ins, outs, sems):
        for s in self.stages():
            pl.when(step == min(s, nsteps - 1))(functools.partial(self.stage, s, ins, outs, sems))

    def finish(self, step, nsteps, ins, outs, sems):
        pl.when(step == nsteps - 1)(functools.partial(self.drain, ins, outs, sems))


def _const_blk(j):
    return lambda me, sib, chips: j


def _carry_gather(carry, shards, relay_step, last_step):
    outs = []
    for sh in shards:
        i = carry.add_input(sh)
        o = carry.add_output((N_DEV,) + sh.shape, sh.dtype)
        half = sh.shape[0] // 2
        tile = 16 if sh.dtype == BF16 else 8
        split = half % tile == 0
        rows = [(0, half), (half, sh.shape[0] - half)] if split else [(0, sh.shape[0]), None]

        def whole(j, core):
            return lambda me, sib, chips, j=j, core=core: _block(*chips[j], me[2] if core == 0 else 1 - me[2])

        def part(j, core, h, rows=rows):
            return lambda me, sib, chips: (_block(*chips[j], me[2] if core == 0 else 1 - me[2]),) + rows[h]

        mine = lambda me, sib, chips: _block(*me)
        carry.local(i, o, mine)
        carry.copy(("in", i, None), o, mine, lambda me, sib, chips: _block(*sib), "sib")
        near = [carry.copy(("in", i, None), o, mine, whole(j, 0), j) for j in range(2)]
        for j in range(2):
            carry.copy(("out", o, whole(j, 0)), o, whole(j, 0), whole(j, 1), "sib", step=relay_step, after=(near[j],))
        for j in range(2):
            if rows[j] is None:
                continue
            far = carry.copy(("out", o, part(j, 0, j)), o, part(j, 0, j), part(2, 0, j), 1 - j,
                             step=relay_step, after=(near[j],))
            carry.copy(("out", o, part(2, 0, j)), o, part(2, 0, j), part(2, 1, j), "sib", step=last_step, after=(far,))
        outs.append(o)
    return outs


def _carry_pair(carry, grads):
    outs = []
    for g in grads:
        i = carry.add_input(g)
        o = carry.add_output((4,) + g.shape[1:], g.dtype)
        for j in range(4):
            if j < 3:
                blk = lambda me, sib, chips, j=j: _block(*chips[j], 1 - me[2])
            else:
                blk = lambda me, sib, chips: _block(*sib)
            carry.copy(("in", i, blk), o, _const_blk(j), _const_blk(j), "sib")
        outs.append(o)
    return outs


def _carry_chip(carry, parts):
    outs = []
    for p in parts:
        i = carry.add_input(p)
        o = carry.add_output((3,) + p.shape[1:], p.dtype)
        for j in range(3):
            carry.copy(("in", i, _const_blk(j)), o, _const_blk(j), _const_blk(j), j)
        outs.append(o)
    return outs


def _pcall(body, *, grid, in_specs, out_specs, out_shape, scratch_shapes, name, args, carry=None):
    sem = ("arbitrary",) * len(grid)
    if carry is None:
        res = pl.pallas_call(body, grid=grid, in_specs=in_specs, out_specs=out_specs, out_shape=out_shape,
                             scratch_shapes=scratch_shapes, compiler_params=_params(sem), name=name)(*args)
        return list(res), []
    n_in, n_out, n_scr = len(in_specs), len(out_specs), len(scratch_shapes)
    c_in, c_out = len(carry.inputs), len(carry.out_shapes)
    nsteps = 1
    for extent in grid:
        nsteps *= extent

    def wrapped(*refs):
        ins = refs[:n_in]
        cins = refs[n_in:n_in + c_in]
        o0 = n_in + c_in
        outs = refs[o0:o0 + n_out]
        couts = refs[o0 + n_out:o0 + n_out + c_out]
        s0 = o0 + n_out + c_out
        scr = refs[s0:s0 + n_scr]
        sems = refs[s0 + n_scr:]
        step = pl.program_id(0)
        for d in range(1, len(grid)):
            step = step * grid[d] + pl.program_id(d)
        carry.starts(step, nsteps, cins, couts, sems)
        body(*ins, *outs, *scr)
        carry.finish(step, nsteps, cins, couts, sems)

    any_spec = pl.BlockSpec(memory_space=pl.ANY)
    res = pl.pallas_call(
        wrapped, grid=grid,
        in_specs=list(in_specs) + [any_spec] * c_in,
        out_specs=list(out_specs) + [any_spec] * c_out,
        out_shape=list(out_shape) + carry.out_shapes,
        scratch_shapes=list(scratch_shapes) + carry.sem_shapes(),
        compiler_params=_params(sem), name=name)(*args, *carry.inputs)
    return list(res[:n_out]), list(res[n_out:])


def _shifted_copies(buf, shifted, tm):
    span = tm + CONV_HALO - 8
    for r in range(1, 8):
        shifted[r - 1, 0:span, :] = buf[r:r + span, :]


def _rows_at(buf, shifted, start):
    aligned, r = (start // 8) * 8, start % 8
    if r == 0:
        return buf[aligned:aligned + ROW_CHUNK, :]
    return shifted[r - 1, aligned:aligned + ROW_CHUNK, :]


def _window_sums(buf, cols, work, levels, tm, trailing):
    src, src_cols = buf, cols
    for k in range(levels + 1):
        shift = 1 << k
        dst = work.at[k % 2]
        if trailing:
            lo = 8 * (k + 1)
            dst[lo:tm + POOL_HALO, :] = (src[lo:tm + POOL_HALO, src_cols]
                                         + src[lo - shift:tm + POOL_HALO - shift, src_cols])
        else:
            hi = tm + POOL_HALO - 8 * (k + 1)
            dst[0:hi, :] = src[0:hi, src_cols] + src[shift:hi + shift, src_cols]
        src, src_cols = dst, slice(0, POOL_GROUP)
    return src[POOL_HALO:POOL_HALO + tm, src_cols] if trailing else src[0:tm, src_cols]


def _row_spec(tm, width):
    return pl.BlockSpec((tm, width), lambda i: (i, 0))


def _const_spec(shape):
    return pl.BlockSpec(shape, lambda i: (0,) * len(shape))


def _weight_spec(shape):
    return pl.BlockSpec(shape, lambda i: (0,) * len(shape), pipeline_mode=pl.Buffered(1))


def _to_bf16(arrays):
    n = len(arrays)

    def body(*refs):
        for a in range(n):
            refs[n + a][...] = refs[a][...].astype(BF16)

    return pl.pallas_call(
        body,
        out_shape=[jax.ShapeDtypeStruct(arr.shape, BF16) for arr in arrays],
        compiler_params=pltpu.CompilerParams(vmem_limit_bytes=V7X_VMEM_LIMIT),
        name="to_bf16",
    )(*arrays)


def _norm_in(x, g_mix, tm, carry=None):
    s = x.shape[0]

    def body(x_ref, gmix_ref, h1_ref):
        h, _ = _rms_fwd(x_ref[...], gmix_ref[...])
        h1_ref[...] = h.astype(BF16)

    return _pcall(
        body,
        grid=(s // tm,),
        in_specs=[_row_spec(tm, D_MODEL), _const_spec((1, D_MODEL))],
        out_specs=[_row_spec(tm, D_MODEL)],
        out_shape=[jax.ShapeDtypeStruct((s, D_MODEL), BF16)],
        scratch_shapes=[],
        name="norm_in",
        args=(x, g_mix),
        carry=carry,
    )


def _fwd_mix(x, h1, w_in_t, w_out, conv_w, conv_b, ln_g, ln_b, pool_w, pool_scale, tm, carry=None):
    s = x.shape[0]
    nt = s // tm

    def body(x_ref, h1_ref, win_ref, wout_ref, cw_ref, cb_ref, lng_ref, lnb_ref, pw_ref, ps_ref,
             z_ref, u1_ref, pooled_ref, x1_ref, mix_ref, ubuf, vbuf, ush, pbuf):
        i = pl.program_id(0)

        @pl.when(i == 0)
        def _():
            ubuf[0:CONV_HALO, :] = jnp.zeros((CONV_HALO, C_CONV), F32)
            vbuf[0:POOL_HALO, :] = jnp.zeros((POOL_HALO, C_POOL), F32)

        xt = x_ref[...]
        z = _dot_nt(h1_ref[...], win_ref[...])
        z_ref[...] = z
        a = z[:, :C_CONV]
        b = z[:, C_CONV:2 * C_CONV]
        v = z[:, 2 * C_CONV:]
        ubuf[CONV_HALO:CONV_HALO + tm, :] = a * _sigmoid(b)
        vbuf[POOL_HALO:POOL_HALO + tm, :] = v

        _shifted_copies(ubuf, ush, tm)
        for rc in range(tm // ROW_CHUNK):
            base = rc * ROW_CHUNK + CONV_HALO - (CONV_K - 1)
            acc = jnp.broadcast_to(cb_ref[...], (ROW_CHUNK, C_CONV))
            for k in range(CONV_K):
                acc = acc + cw_ref[k:k + 1, :] * _rows_at(ubuf, ush, base + k)
            u1_ref[rc * ROW_CHUNK:(rc + 1) * ROW_CHUNK, :] = acc

        u1 = u1_ref[...]
        mu = jnp.mean(u1, axis=-1, keepdims=True)
        cen = u1 - mu
        rstd = lax.rsqrt(jnp.mean(cen * cen, axis=-1, keepdims=True) + EPS)
        u2 = cen * rstd * lng_ref[...] + lnb_ref[...]
        u = u2 * _sigmoid(u2)

        pos1 = (i * tm + lax.broadcasted_iota(jnp.int32, (tm, 1), 0) + 1).astype(F32)
        parts = [u]
        for g, w in enumerate(POOL_WINDOWS):
            cols = slice(g * POOL_GROUP, (g + 1) * POOL_GROUP)
            vg = v[:, cols]
            tot = _window_sums(vbuf, cols, pbuf, g, tm, trailing=True)
            pooled = tot / jnp.minimum(pos1, float(w)) - vg
            pooled_b = pooled.astype(BF16)
            pooled_ref[:, cols] = pooled_b
            parts.append(_dot(pooled_b, pw_ref[g]) * ps_ref[:, cols])
        mix = jnp.concatenate(parts, axis=-1).astype(BF16)
        mix_ref[...] = mix
        x1_ref[...] = xt + _dot(mix, wout_ref[...])

        ubuf[0:CONV_HALO, :] = ubuf[tm:tm + CONV_HALO, :]
        vbuf[0:POOL_HALO, :] = vbuf[tm:tm + POOL_HALO, :]

    return _pcall(
        body,
        grid=(nt,),
        in_specs=[
            _row_spec(tm, D_MODEL),
            _row_spec(tm, D_MODEL),
            _const_spec((Z_WIDTH, D_MODEL)),
            _const_spec((D_MODEL, D_MODEL)),
            _const_spec((CONV_HALO, C_CONV)),
            _const_spec((1, C_CONV)),
            _const_spec((1, C_CONV)),
            _const_spec((1, C_CONV)),
            _const_spec((len(POOL_WINDOWS), POOL_GROUP, POOL_GROUP)),
            _const_spec((1, C_POOL)),
        ],
        out_specs=[
            _row_spec(tm, Z_WIDTH),
            _row_spec(tm, C_CONV),
            _row_spec(tm, C_POOL),
            _row_spec(tm, D_MODEL),
            _row_spec(tm, D_MODEL),
        ],
        out_shape=[
            jax.ShapeDtypeStruct((s, Z_WIDTH), F32),
            jax.ShapeDtypeStruct((s, C_CONV), F32),
            jax.ShapeDtypeStruct((s, C_POOL), BF16),
            jax.ShapeDtypeStruct((s, D_MODEL), F32),
            jax.ShapeDtypeStruct((s, D_MODEL), BF16),
        ],
        scratch_shapes=[
            pltpu.VMEM((tm + CONV_HALO, C_CONV), F32),
            pltpu.VMEM((tm + POOL_HALO, C_POOL), F32),
            pltpu.VMEM((7, tm + CONV_HALO, C_CONV), F32),
            pltpu.VMEM((2, tm + POOL_HALO, POOL_GROUP), F32),
        ],
        name="fwd_mix",
        args=(x, h1, w_in_t, w_out, conv_w, conv_b, ln_g, ln_b, pool_w, pool_scale),
        carry=carry,
    )


def _ffn_up(x1, w_gu_t, g_ffn, tm, carry=None):
    s = x1.shape[0]

    def body(x1_ref, w_ref, gffn_ref, h2_ref, g_ref, u_ref, act_ref):
        h, _ = _rms_fwd(x1_ref[...], gffn_ref[...])
        h2 = h.astype(BF16)
        h2_ref[...] = h2
        for c in range(D_FF // FF_CHUNK):
            cols = slice(c * FF_CHUNK, (c + 1) * FF_CHUNK)
            g = _dot_nt(h2, w_ref[c * FF_CHUNK:(c + 1) * FF_CHUNK, :])
            u = _dot_nt(h2, w_ref[D_FF + c * FF_CHUNK:D_FF + (c + 1) * FF_CHUNK, :])
            g_ref[:, cols] = g.astype(BF16)
            u_ref[:, cols] = u.astype(BF16)
            act_ref[:, cols] = (g * _sigmoid(g) * u).astype(BF16)

    return _pcall(
        body,
        grid=(s // tm,),
        in_specs=[_row_spec(tm, D_MODEL), _weight_spec((2 * D_FF, D_MODEL)), _const_spec((1, D_MODEL))],
        out_specs=[_row_spec(tm, D_MODEL), _row_spec(tm, D_FF), _row_spec(tm, D_FF), _row_spec(tm, D_FF)],
        out_shape=[
            jax.ShapeDtypeStruct((s, D_MODEL), BF16),
            jax.ShapeDtypeStruct((s, D_FF), BF16),
            jax.ShapeDtypeStruct((s, D_FF), BF16),
            jax.ShapeDtypeStruct((s, D_FF), BF16),
        ],
        scratch_shapes=[],
        name="ffn_up",
        args=(x1, w_gu_t, g_ffn),
        carry=carry,
    )


def _down_ple(x1, act, w_down, p, tgt, w_pg, w_pu_t, g_gate, g_post, g_final, tm):
    s = x1.shape[0]
    nt = s // tm

    def body(x1_ref, act_ref, wd_ref, p_ref, t_ref, wpg_ref, wpu_ref, gg_ref, gp_ref, gf_ref,
             dx2_ref, dx2b_ref, hg_ref, ds_ref, dpe_ref, pb_ref, stats_ref, x2_cur, x2_next):
        i = pl.program_id(0)

        @pl.when(i == 0)
        def _():
            stats_ref[...] = jnp.zeros_like(stats_ref)
            x2_cur[...] = jnp.zeros((tm, D_MODEL), F32)

        def down(c):
            cols = slice(c * 256, (c + 1) * 256)
            x2_next[:, cols] = x1_ref[:, cols] + _dot(act_ref[...], wd_ref[:, cols])

        x2 = x2_cur[...]
        counts = i >= 1

        hg, rg = _rms_fwd(x2, gg_ref[...])
        hg_b = hg.astype(BF16)
        hg_ref[...] = hg_b
        down(0)
        gate = _sigmoid(_dot(hg_b, wpg_ref[...]))
        pb = p_ref[...].astype(BF16)
        pb_ref[...] = pb
        pe = _dot_nt(pb, wpu_ref[...])
        e, rp = _rms_fwd(pe, gp_ref[...])
        x3 = x2 + gate * e
        down(1)
        y, r3 = _rms_fwd(x3, gf_ref[...])
        diff = y - t_ref[...]
        loss = 0.5 * jnp.sum(jnp.sum(diff * diff, axis=-1, keepdims=True), axis=0, keepdims=True) / D_MODEL
        dy = diff * (1.0 / D_MODEL)

        dx3, dgf = _rms_bwd(x3, r3, gf_ref[...], dy)
        down(2)
        dpe, dgp = _rms_bwd(pe, rp, gp_ref[...], dx3 * gate)
        dpe_ref[...] = dpe.astype(BF16)
        ds = (dx3 * e * gate * (1.0 - gate)).astype(BF16)
        ds_ref[...] = ds
        dhg = _dot_nt(ds, wpg_ref[...])
        down(3)
        dxg, dgg = _rms_bwd(x2, rg, gg_ref[...], dhg)
        dx2 = dx3 + dxg
        dx2_ref[...] = dx2
        dx2b_ref[...] = dx2.astype(BF16)
        x2_cur[...] = x2_next[...]

        stats_ref[0:1, :] += jnp.where(counts, dgf, 0.0)
        stats_ref[1:2, :] += jnp.where(counts, dgp, 0.0)
        stats_ref[2:3, :] += jnp.where(counts, dgg, 0.0)
        stats_ref[3:4, :] += jnp.where(counts, jnp.broadcast_to(loss, (1, D_MODEL)), 0.0)

    ahead = lambda width: pl.BlockSpec((tm, width), lambda i: (jnp.minimum(i, nt - 1), 0))
    behind = lambda width: pl.BlockSpec((tm, width), lambda i: (jnp.maximum(i - 1, 0), 0))
    return pl.pallas_call(
        body,
        grid=(nt + 1,),
        in_specs=[
            ahead(D_MODEL),
            ahead(D_FF),
            _weight_spec((D_FF, D_MODEL)),
            behind(D_PLE),
            behind(D_MODEL),
            _weight_spec((D_MODEL, D_MODEL)),
            _weight_spec((D_MODEL, D_PLE)),
            _const_spec((1, D_MODEL)),
            _const_spec((1, D_MODEL)),
            _const_spec((1, D_MODEL)),
        ],
        out_specs=[
            behind(D_MODEL),
            behind(D_MODEL),
            behind(D_MODEL),
            behind(D_MODEL),
            behind(D_MODEL),
            behind(D_PLE),
            _const_spec((8, D_MODEL)),
        ],
        out_shape=[
            jax.ShapeDtypeStruct((s, D_MODEL), F32),
            jax.ShapeDtypeStruct((s, D_MODEL), BF16),
            jax.ShapeDtypeStruct((s, D_MODEL), BF16),
            jax.ShapeDtypeStruct((s, D_MODEL), BF16),
            jax.ShapeDtypeStruct((s, D_MODEL), BF16),
            jax.ShapeDtypeStruct((s, D_PLE), BF16),
            jax.ShapeDtypeStruct((8, D_MODEL), F32),
        ],
        scratch_shapes=[pltpu.VMEM((tm, D_MODEL), F32), pltpu.VMEM((tm, D_MODEL), F32)],
        compiler_params=_params(("arbitrary",)),
        name="down_ple",
    )(x1, act, w_down, p, tgt, w_pg, w_pu_t, g_gate, g_post, g_final)


def _ffn_bwd(dx2, dx2b, x1, g_sav, u_sav, w_gu_t, w_down, g_ffn, tm, carry=None):
    s = x1.shape[0]

    def body(dx2_ref, dx2b_ref, x1_ref, g_ref, u_ref, w_ref, wd_ref, gffn_ref,
             dg_ref, du_ref, dx1_ref, dx1b_ref, stats_ref):
        @pl.when(pl.program_id(0) == 0)
        def _():
            stats_ref[...] = jnp.zeros_like(stats_ref)

        dx2b = dx2b_ref[...]
        nc = D_FF // FF_CHUNK
        dacts = [_dot_nt(dx2b, wd_ref[c * FF_CHUNK:(c + 1) * FF_CHUNK, :]) for c in range(nc)]
        dh2 = jnp.zeros((tm, D_MODEL), F32)
        for c in range(nc):
            cols = slice(c * FF_CHUNK, (c + 1) * FF_CHUNK)
            g = g_ref[:, cols].astype(F32)
            u = u_ref[:, cols].astype(F32)
            sg = _sigmoid(g)
            dg = (dacts[c] * u * sg * (1.0 + g * (1.0 - sg))).astype(BF16)
            du = (dacts[c] * g * sg).astype(BF16)
            dg_ref[:, cols] = dg
            du_ref[:, cols] = du
            dh2 = dh2 + _dot(dg, w_ref[c * FF_CHUNK:(c + 1) * FF_CHUNK, :])
            dh2 = dh2 + _dot(du, w_ref[D_FF + c * FF_CHUNK:D_FF + (c + 1) * FF_CHUNK, :])

        x1 = x1_ref[...]
        r2 = lax.rsqrt(jnp.mean(x1 * x1, axis=-1, keepdims=True) + EPS)
        dxn, dgf = _rms_bwd(x1, r2, gffn_ref[...], dh2)
        dx1 = dx2_ref[...] + dxn
        dx1_ref[...] = dx1
        dx1b_ref[...] = dx1.astype(BF16)
        stats_ref[0:1, :] += dgf

    return _pcall(
        body,
        grid=(s // tm,),
        in_specs=[
            _row_spec(tm, D_MODEL), _row_spec(tm, D_MODEL), _row_spec(tm, D_MODEL),
            _row_spec(tm, D_FF), _row_spec(tm, D_FF),
            _weight_spec((2 * D_FF, D_MODEL)), _weight_spec((D_FF, D_MODEL)), _const_spec((1, D_MODEL)),
        ],
        out_specs=[_row_spec(tm, D_FF), _row_spec(tm, D_FF), _row_spec(tm, D_MODEL), _row_spec(tm, D_MODEL),
                   _const_spec((8, D_MODEL))],
        out_shape=[
            jax.ShapeDtypeStruct((s, D_FF), BF16),
            jax.ShapeDtypeStruct((s, D_FF), BF16),
            jax.ShapeDtypeStruct((s, D_MODEL), F32),
            jax.ShapeDtypeStruct((s, D_MODEL), BF16),
            jax.ShapeDtypeStruct((8, D_MODEL), F32),
        ],
        scratch_shapes=[],
        name="ffn_bwd",
        args=(dx2, dx2b, x1, g_sav, u_sav, w_gu_t, w_down, g_ffn),
        carry=carry,
    )


def _bwd_mix(dx1, dx1b, x, z, u1, pooled, w_in_t, w_out, g_mix, conv_w, ln_g, ln_b, pool_w, pool_scale, tm,
             carry=None):
    s = x.shape[0]
    nt = s // tm

    def body(dx1_ref, dx1b_ref, x_ref, z_ref, u1_ref, pooled_ref, win_ref, wout_ref, gmix_ref, cw_ref,
             lng_ref, lnb_ref, pw_ref, ps_ref,
             dx_ref, dz_ref, vec_ref, dcw_ref, dpw_ref, dubuf, dvbuf, u0buf, du0buf, dush, pbuf):
        i = pl.program_id(0)
        tile = nt - 1 - i

        @pl.when(i == 0)
        def _():
            vec_ref[...] = jnp.zeros_like(vec_ref)
            dcw_ref[...] = jnp.zeros_like(dcw_ref)
            dpw_ref[...] = jnp.zeros_like(dpw_ref)
            dubuf[tm:tm + CONV_HALO, :] = jnp.zeros((CONV_HALO, C_CONV), F32)
            dvbuf[tm:tm + POOL_HALO, :] = jnp.zeros((POOL_HALO, C_POOL), F32)

        dmix = _dot_nt(dx1b_ref[...], wout_ref[...])
        du = dmix[:, :C_CONV]
        dq = dmix[:, C_CONV:]

        pos1 = (tile * tm + lax.broadcasted_iota(jnp.int32, (tm, 1), 0) + 1).astype(F32)
        dpooled_parts = []
        dps_rows = []
        for g, w in enumerate(POOL_WINDOWS):
            cols = slice(g * POOL_GROUP, (g + 1) * POOL_GROUP)
            pooled_b = pooled_ref[:, cols]
            mixed = _dot(pooled_b, pw_ref[g])
            dqg = dq[:, cols]
            dps_rows.append(jnp.sum(dqg * mixed, axis=0, keepdims=True))
            dmixed = (dqg * ps_ref[:, cols]).astype(BF16)
            dpw_ref[g] += _dot_tn(pooled_b, dmixed)
            dpooled = _dot_nt(dmixed, pw_ref[g])
            dpooled_parts.append(dpooled)
            dvbuf[0:tm, cols] = dpooled / jnp.minimum(pos1, float(w))
        vec_ref[4:5, 0:C_POOL] += jnp.concatenate(dps_rows, axis=-1)
        dv_parts = []
        for g, w in enumerate(POOL_WINDOWS):
            cols = slice(g * POOL_GROUP, (g + 1) * POOL_GROUP)
            tot = _window_sums(dvbuf, cols, pbuf, g, tm, trailing=False)
            dv_parts.append(tot - dpooled_parts[g])

        u1 = u1_ref[...]
        mu = jnp.mean(u1, axis=-1, keepdims=True)
        cen = u1 - mu
        rstd = lax.rsqrt(jnp.mean(cen * cen, axis=-1, keepdims=True) + EPS)
        xhat = cen * rstd
        u2 = xhat * lng_ref[...] + lnb_ref[...]
        sg2 = _sigmoid(u2)
        du2 = du * sg2 * (1.0 + u2 * (1.0 - sg2))
        vec_ref[1:2, 0:C_CONV] += jnp.sum(du2 * xhat, axis=0, keepdims=True)
        vec_ref[2:3, 0:C_CONV] += jnp.sum(du2, axis=0, keepdims=True)
        t1 = du2 * lng_ref[...]
        du1 = rstd * (t1 - jnp.mean(t1, axis=-1, keepdims=True)
                      - xhat * jnp.mean(t1 * xhat, axis=-1, keepdims=True))
        vec_ref[3:4, 0:C_CONV] += jnp.sum(du1, axis=0, keepdims=True)
        dubuf[0:tm, :] = du1

        zt = z_ref[...]
        a = zt[:, :C_CONV]
        sgb = _sigmoid(zt[:, C_CONV:2 * C_CONV])
        u0buf[...] = a * sgb

        _shifted_copies(dubuf, dush, tm)
        for rc in range(tm // ROW_CHUNK):
            r0 = rc * ROW_CHUNK
            acc = jnp.zeros((ROW_CHUNK, C_CONV), F32)
            for k in range(CONV_K):
                acc = acc + cw_ref[k:k + 1, :] * _rows_at(dubuf, dush, r0 + (CONV_K - 1) - k)
            du0buf[r0:r0 + ROW_CHUNK, :] = acc
        for k in range(CONV_K):
            acc = jnp.zeros((ROW_CHUNK, C_CONV), F32)
            for rc in range(tm // ROW_CHUNK):
                r0 = rc * ROW_CHUNK
                acc = acc + u0buf[r0:r0 + ROW_CHUNK, :] * _rows_at(dubuf, dush, r0 + (CONV_K - 1) - k)
            dcw_ref[k:k + 1, :] += jnp.sum(acc, axis=0, keepdims=True)
        du0 = du0buf[...]

        da = du0 * sgb
        db = du0 * a * sgb * (1.0 - sgb)
        dz = jnp.concatenate([da, db] + dv_parts, axis=-1).astype(BF16)
        dz_ref[...] = dz

        xt = x_ref[...]
        r1 = lax.rsqrt(jnp.mean(xt * xt, axis=-1, keepdims=True) + EPS)
        dh1 = _dot(dz, win_ref[...])
        dxn, dgm = _rms_bwd(xt, r1, gmix_ref[...], dh1)
        dx_ref[...] = dx1_ref[...] + dxn
        vec_ref[0:1, :] += dgm

        dubuf[tm:tm + CONV_HALO, :] = dubuf[0:CONV_HALO, :]
        dvbuf[tm:tm + POOL_HALO, :] = dvbuf[0:POOL_HALO, :]

    rev = lambda width: pl.BlockSpec((tm, width), lambda i: (nt - 1 - i, 0))
    return _pcall(
        body,
        grid=(nt,),
        in_specs=[
            rev(D_MODEL), rev(D_MODEL), rev(D_MODEL), rev(Z_WIDTH), rev(C_CONV), rev(C_POOL),
            _const_spec((Z_WIDTH, D_MODEL)),
            _const_spec((D_MODEL, D_MODEL)),
            _const_spec((1, D_MODEL)),
            _const_spec((CONV_HALO, C_CONV)),
            _const_spec((1, C_CONV)),
            _const_spec((1, C_CONV)),
            _const_spec((len(POOL_WINDOWS), POOL_GROUP, POOL_GROUP)),
            _const_spec((1, C_POOL)),
        ],
        out_specs=[
            rev(D_MODEL), rev(Z_WIDTH),
            _const_spec((8, D_MODEL)),
            _const_spec((CONV_HALO, C_CONV)),
            _const_spec((len(POOL_WINDOWS), POOL_GROUP, POOL_GROUP)),
        ],
        out_shape=[
            jax.ShapeDtypeStruct((s, D_MODEL), F32),
            jax.ShapeDtypeStruct((s, Z_WIDTH), BF16),
            jax.ShapeDtypeStruct((8, D_MODEL), F32),
            jax.ShapeDtypeStruct((CONV_HALO, C_CONV), F32),
            jax.ShapeDtypeStruct((len(POOL_WINDOWS), POOL_GROUP, POOL_GROUP), F32),
        ],
        scratch_shapes=[
            pltpu.VMEM((tm + CONV_HALO, C_CONV), F32),
            pltpu.VMEM((tm + POOL_HALO, C_POOL), F32),
            pltpu.VMEM((tm, C_CONV), F32),
            pltpu.VMEM((tm, C_CONV), F32),
            pltpu.VMEM((7, tm + CONV_HALO, C_CONV), F32),
            pltpu.VMEM((2, tm + POOL_HALO, POOL_GROUP), F32),
        ],
        name="bwd_mix",
        args=(dx1, dx1b, x, z, u1, pooled, w_in_t, w_out, g_mix, conv_w, ln_g, ln_b, pool_w, pool_scale),
        carry=carry,
    )


def _grad_matmul(a, b, bm, name, a2=None, carry=None):
    s, ma = a.shape
    nb = b.shape[1]
    na = ma // bm
    if a2 is None:
        def body(a_ref, b_ref, o_ref):
            o_ref[...] = _dot_tn(a_ref[...], b_ref[...]).astype(BF16)

        lhs_specs = [pl.BlockSpec((s, bm), lambda i: (0, i))]
        lhs = (a,)
        steps = na
    else:
        def body(a_ref, a2_ref, b_ref, o_ref):
            i = pl.program_id(0)

            @pl.when(i < na)
            def _():
                o_ref[...] = _dot_tn(a_ref[...], b_ref[...]).astype(BF16)

            @pl.when(i >= na)
            def _():
                o_ref[...] = _dot_tn(a2_ref[...], b_ref[...]).astype(BF16)

        lhs_specs = [pl.BlockSpec((s, bm), lambda i: (0, jnp.minimum(i, na - 1))),
                     pl.BlockSpec((s, bm), lambda i: (0, jnp.maximum(i - na, 0)))]
        lhs = (a, a2)
        steps = 2 * na

    outs, carried = _pcall(
        body,
        grid=(steps,),
        in_specs=lhs_specs + [pl.BlockSpec((s, nb), lambda i: (0, 0))],
        out_specs=[pl.BlockSpec((bm, nb), lambda i: (i, 0))],
        out_shape=[jax.ShapeDtypeStruct((steps * bm, nb), BF16)],
        scratch_shapes=[],
        name=name,
        args=lhs + (b,),
        carry=carry,
    )
    return outs[0], carried


def _reduce_scatter(grads, small, name):
    n, ns = len(grads), len(small)
    shapes = [g.shape[1:] for g in grads]
    gather = _Carry()
    _carry_gather(gather, small, 1, 2)

    def body(*refs):
        g = refs[:n]
        out = refs[n + ns:2 * n + ns]
        scr = refs[2 * (n + ns):]
        own, loc, r1, r2 = scr[:n], scr[n:2 * n], scr[2 * n:3 * n], scr[3 * n:4 * n]
        load_sems, s1, q1, s2, q2 = scr[4 * n:4 * n + 5]
        gather_refs = (refs[n:n + ns], refs[2 * n + ns:2 * (n + ns)], scr[4 * n + 5:])
        me, sib, chips = _place()
        c = me[2]

        gather.stage(0, *gather_refs)
        loads = []
        sends = []
        for a in range(n):
            ld = [pltpu.make_async_copy(g[a].at[_block(*chip, c)], loc[a].at[j], load_sems.at[a, j])
                  for j, chip in enumerate(chips)]
            ld.append(pltpu.make_async_copy(g[a].at[_block(*me)], own[a], load_sems.at[a, 3]))
            for cp in ld:
                cp.start()
            loads.append(ld)
            blocks = [(*chip, 1 - c) for chip in chips] + [sib]
            for j, blk in enumerate(blocks):
                cp = pltpu.make_async_remote_copy(
                    src_ref=g[a].at[_block(*blk)], dst_ref=r1[a].at[j],
                    send_sem=s1.at[a, j], recv_sem=q1.at[a, j], device_id=sib, device_id_type=MESH)
                cp.start()
                sends.append(cp)

        def from_sibling(a, j):
            return pltpu.make_async_remote_copy(
                src_ref=r1[a].at[j], dst_ref=r1[a].at[j], send_sem=s1.at[a, j], recv_sem=q1.at[a, j],
                device_id=sib, device_id_type=MESH)

        def partial(a, j, chip):
            return pltpu.make_async_remote_copy(
                src_ref=loc[a].at[j], dst_ref=r2[a].at[j], send_sem=s2.at[a, j], recv_sem=q2.at[a, j],
                device_id=(*chip, c), device_id_type=MESH)

        gather.stage(1, *gather_refs)
        for a in range(n):
            for j, chip in enumerate(chips):
                loads[a][j].wait()
                from_sibling(a, j).wait_recv()
                loc[a][j] = (loc[a][j].astype(F32) + r1[a][j].astype(F32)).astype(BF16)
                cp = partial(a, j, chip)
                cp.start()
                sends.append(cp)
        gather.stage(2, *gather_refs)
        for a in range(n):
            loads[a][3].wait()
            from_sibling(a, 3).wait_recv()
            acc = own[a][...].astype(F32) + r1[a][3].astype(F32)
            for j, chip in enumerate(chips):
                partial(a, j, chip).wait_recv()
                acc = acc + r2[a][j].astype(F32)
            out[a][...] = acc
        for cp in sends:
            cp.wait_send()
        gather.drain(*gather_refs)

    any_spec = pl.BlockSpec(memory_space=pl.ANY)
    vmem_spec = pl.BlockSpec(memory_space=pltpu.VMEM)
    res = pl.pallas_call(
        body,
        in_specs=[any_spec] * (n + ns),
        out_specs=[vmem_spec] * n + [any_spec] * ns,
        out_shape=[jax.ShapeDtypeStruct(sh, F32) for sh in shapes] + gather.out_shapes,
        scratch_shapes=(
            [pltpu.VMEM(sh, BF16) for sh in shapes]
            + [pltpu.VMEM((3,) + sh, BF16) for sh in shapes]
            + [pltpu.VMEM((4,) + sh, BF16) for sh in shapes]
            + [pltpu.VMEM((3,) + sh, BF16) for sh in shapes]
            + [pltpu.SemaphoreType.DMA((n, 4)),
               pltpu.SemaphoreType.DMA((n, 4)), pltpu.SemaphoreType.DMA((n, 4)),
               pltpu.SemaphoreType.DMA((n, 3)), pltpu.SemaphoreType.DMA((n, 3))]
            + gather.sem_shapes()
        ),
        compiler_params=pltpu.CompilerParams(vmem_limit_bytes=V7X_VMEM_LIMIT),
        name=name,
    )(*grads, *small)
    return res[:n], res[n:]


def _pair_add(grads, from_sib, blks, name):
    n = len(grads)

    def body(blk_ref, *refs):
        for a in range(n):
            refs[2 * n + a][...] = (refs[a][...].astype(F32) + refs[n + a][...].astype(F32)).astype(BF16)

    mine = [pl.BlockSpec((None,) + g.shape[1:], lambda j, b: (b[j], 0, 0)) for g in grads]
    same = [pl.BlockSpec((None,) + g.shape[1:], lambda j, b: (j, 0, 0)) for g in grads]
    return pl.pallas_call(
        body,
        grid_spec=pltpu.PrefetchScalarGridSpec(
            num_scalar_prefetch=1, grid=(4,), in_specs=mine + same, out_specs=same),
        out_shape=[jax.ShapeDtypeStruct((4,) + g.shape[1:], BF16) for g in grads],
        compiler_params=_params(("arbitrary",)),
        name=name,
    )(blks, *grads, *from_sib)


def _chip_sum(parts, from_chips, name):
    n = len(parts)

    def body(*refs):
        for a in range(n):
            acc = refs[a][...].astype(F32)
            for j in range(3):
                acc = acc + refs[n + a][j].astype(F32)
            refs[2 * n + a][...] = acc

    half = [p.shape[1] // 2 for p in parts]
    return pl.pallas_call(
        body,
        grid=(2,),
        in_specs=([pl.BlockSpec((None, h, p.shape[2]), lambda i: (3, i, 0)) for p, h in zip(parts, half)]
                  + [pl.BlockSpec((3, h, p.shape[2]), lambda i: (0, i, 0)) for p, h in zip(parts, half)]),
        out_specs=[pl.BlockSpec((h, p.shape[2]), lambda i: (i, 0)) for p, h in zip(parts, half)],
        out_shape=[jax.ShapeDtypeStruct(p.shape[1:], F32) for p in parts],
        compiler_params=_params(("arbitrary",)),
        name=name,
    )(*parts, *from_chips)


def _adam_math(w, g, m, v):
    nm = ADAM_B1 * m + (1.0 - ADAM_B1) * g
    nv = ADAM_B2 * v + (1.0 - ADAM_B2) * (g * g)
    m_hat = nm / (1.0 - ADAM_B1 ** ADAM_STEP)
    v_hat = nv / (1.0 - ADAM_B2 ** ADAM_STEP)
    return -ADAM_LR * (m_hat / (jnp.sqrt(v_hat) + ADAM_EPS) + ADAM_WD * w), nm, nv


def _sum_adamw(items, name, carry=None):
    n = len(items)

    def body(*refs):
        for a in range(n):
            p_ref, f_ref, w_ref, m_ref, v_ref = refs[5 * a:5 * a + 5]
            g_ref, d_ref, nm_ref, nv_ref = refs[5 * n + 4 * a:5 * n + 4 * a + 4]
            g = p_ref[...].astype(F32)
            for j in range(3):
                g = g + f_ref[j].astype(F32)
            g_ref[...] = g
            d_ref[...], nm_ref[...], nv_ref[...] = _adam_math(w_ref[...], g, m_ref[...], v_ref[...])

    in_specs, out_specs, out_shape, args = [], [], [], []
    for part, from_chips, w, m, v in items:
        r, c = w.shape
        spec = pl.BlockSpec((r // 2, c), lambda i: (i, 0))
        in_specs += [pl.BlockSpec((None, r // 2, c), lambda i: (3, i, 0)),
                     pl.BlockSpec((3, r // 2, c), lambda i: (0, i, 0)), spec, spec, spec]
        out_specs += [spec] * 4
        out_shape += [jax.ShapeDtypeStruct((r, c), F32)] * 4
        args += [part, from_chips, w, m, v]
    outs, carried = _pcall(body, grid=(2,), in_specs=in_specs, out_specs=out_specs, out_shape=out_shape,
                           scratch_shapes=[], name=name, args=tuple(args), carry=carry)
    return [tuple(outs[4 * a:4 * a + 4]) for a in range(n)], carried


def _small_update(gathered, layout, params, name):
    ng, npar = len(gathered), len(params)

    def body(*refs):
        parts = refs[:ng]
        prm = refs[ng:ng + 3 * npar]
        tot_refs = refs[ng + 3 * npar:2 * ng + 3 * npar]
        out = refs[2 * ng + 3 * npar:]
        tots = []
        for a in range(ng):
            acc = parts[a][0]
            for d in range(1, N_DEV):
                acc = acc + parts[a][d]
            tot_refs[a][...] = acc
            tots.append(acc)
        for i, (a, row, width) in enumerate(layout):
            g = tots[a] if row is None else tots[a][row:row + 1, :width]
            delta, nm, nv = _adam_math(prm[3 * i][...], g, prm[3 * i + 1][...], prm[3 * i + 2][...])
            out[4 * i][...] = g
            out[4 * i + 1][...] = delta
            out[4 * i + 2][...] = nm
            out[4 * i + 3][...] = nv

    flat = [t for prm in params for t in prm]
    res = pl.pallas_call(
        body,
        out_shape=([jax.ShapeDtypeStruct(g.shape[1:], F32) for g in gathered]
                   + [jax.ShapeDtypeStruct(prm[0].shape, F32) for prm in params for _ in range(4)]),
        compiler_params=pltpu.CompilerParams(vmem_limit_bytes=V7X_VMEM_LIMIT),
        name=name,
    )(*gathered, *flat)
    return res[:ng], [tuple(res[ng + 4 * i:ng + 4 * i + 4]) for i in range(npar)]


def _adamw(w, g, m, v, name):
    rows, cols = w.shape
    br = rows
    for cand in (512, 256, 128):
        if rows % cand == 0 and rows > cand:
            br = cand
            break

    def body(w_ref, g_ref, m_ref, v_ref, d_ref, nm_ref, nv_ref):
        d_ref[...], nm_ref[...], nv_ref[...] = _adam_math(w_ref[...], g_ref[...], m_ref[...], v_ref[...])

    spec = pl.BlockSpec((br, cols), lambda i: (i, 0))
    shape = jax.ShapeDtypeStruct((rows, cols), F32)
    return pl.pallas_call(
        body,
        grid=(rows // br,),
        in_specs=[spec] * 4,
        out_specs=[spec] * 3,
        out_shape=[shape] * 3,
        compiler_params=_params(("arbitrary",)),
        name=name,
    )(w, g, m, v)


def _by_device(full):
    return full.reshape(N_DEV, full.shape[0] // N_DEV, full.shape[1])


def kernel(x, p, g_mix, w_in, conv_w, conv_b, ln_g, ln_b, pool_w, pool_scale, w_out, g_ffn, w_gate_up, w_down, g_ple_gate, w_ple_gate, w_ple_up, g_ple_post, g_final, loss_target, m_g_mix, m_w_in, m_conv_w, m_conv_b, m_ln_g, m_ln_b, m_pool_w, m_pool_scale, m_w_out, m_g_ffn, m_w_gate_up, m_w_down, m_g_ple_gate, m_w_ple_gate, m_w_ple_up, m_g_ple_post, m_g_final, v_g_mix, v_w_in, v_conv_w, v_conv_b, v_ln_g, v_ln_b, v_pool_w, v_pool_scale, v_w_out, v_g_ffn, v_w_gate_up, v_w_down, v_g_ple_gate, v_w_ple_gate, v_w_ple_up, v_g_ple_post, v_g_final):
    seq = x.shape[1]
    xs = x[0]
    ps = p[0, 0]
    tgt = loss_target[0]
    ax, ay, ac = lax.axis_index("x"), lax.axis_index("y"), lax.axis_index("c")
    me = _block(ax, ay, ac)
    blks = jnp.stack([_block(1 - ax, ay, ac), _block(ax, 1 - ay, ac), _block(1 - ax, 1 - ay, ac), me]).astype(jnp.int32)
    rows = lambda gth: gth.reshape((-1,) + gth.shape[2:])

    sh_in, sh_out, sh_gu, sh_down, sh_pg, sh_pu, pool_w_b = _to_bf16([
        w_in[0].T,
        w_out[0],
        w_gate_up[0].T,
        w_down[0],
        w_ple_gate[0],
        w_ple_up[0].T,
        pool_w[0]])
    carry = _Carry()
    _carry_gather(carry, [sh_in, sh_out,
                          jnp.pad(conv_w[0].T, ((0, 0), (0, CONV_HALO - CONV_K)))], 8, 15)
    (h1,), first = _norm_in(xs, g_mix, min(256, seq), carry)
    w_in_t, w_out_f, conv_w_t = [rows(gth) for gth in first]
    conv_w_f = conv_w_t.T

    carry = _Carry()
    _carry_gather(carry, [sh_gu], 8, 12)
    (z, u1, pooled, x1, mix), (w_gu_all,) = _fwd_mix(
        xs, h1, w_in_t, w_out_f, conv_w_f, conv_b, ln_g, ln_b, pool_w_b, pool_scale, min(256, seq), carry)
    w_gu_t = rows(w_gu_all)

    carry = _Carry()
    _carry_gather(carry, [sh_down, sh_pg, sh_pu], 6, 10)
    (h2, g_sav, u_sav, act), late = _ffn_up(x1, w_gu_t, g_ffn, min(256, seq), carry)
    w_down_f, w_pg_f, w_pu_t = [rows(gth) for gth in late]
    dx2, dx2b, hg, ds, dpe, pb, stats_ple = _down_ple(
        x1, act, w_down_f, ps, tgt, w_pg_f, w_pu_t, g_ple_gate, g_ple_post, g_final.reshape(1, D_MODEL),
        min(256, seq))
    dg, du, dx1, dx1b, stats_ffn = _ffn_bwd(dx2, dx2b, x1, g_sav, u_sav, w_gu_t, w_down_f, g_ffn,
                                            min(256, seq))[0]

    d_w_pg, _ = _grad_matmul(hg, ds, 256, "grad_w_ple_gate")
    d_w_pu_t, _ = _grad_matmul(dpe, pb, 256, "grad_w_ple_up")
    d_w_gu_t, _ = _grad_matmul(dg, h2, 256, "grad_w_gate_up", a2=du)
    d_w_out, _ = _grad_matmul(mix, dx1b, 256, "grad_w_out")
    early = [_by_device(d_w_pg), _by_device(d_w_pu_t), _by_device(d_w_gu_t), _by_device(d_w_out)]
    carry = _Carry()
    _carry_pair(carry, early)
    d_w_down, from_sib = _grad_matmul(act, dx2b, 256, "grad_w_down", carry=carry)
    early_parts = _pair_add(early, from_sib, blks, "pair_add_early")

    carry = _Carry()
    _carry_chip(carry, early_parts)
    _carry_pair(carry, [_by_device(d_w_down)])
    (dx, dz, vec_mix, dconv_w_part, dpool_w_part), carried = _bwd_mix(
        dx1, dx1b, xs, z, u1, pooled, w_in_t, w_out_f, g_mix, conv_w_f, ln_g, ln_b, pool_w_b, pool_scale,
        min(256, seq), carry)
    early_chips, down_sib = carried[:4], carried[4:]
    down_parts = _pair_add([_by_device(d_w_down)], down_sib, blks, "pair_add_down")

    carry = _Carry()
    _carry_chip(carry, down_parts)
    d_w_in_t, down_chips = _grad_matmul(dz, h1, 256, "grad_w_in", carry=carry)
    (gr_w_in_t,), small = _reduce_scatter(
        [_by_device(d_w_in_t)],
        [vec_mix, stats_ple, stats_ffn, dconv_w_part, dpool_w_part.reshape(-1, POOL_GROUP)], "scatter_last")
    small = list(small[:4]) + [small[4].reshape((N_DEV,) + dpool_w_part.shape)]

    vec_names = ["g_mix", "ln_g", "ln_b", "conv_b", "pool_scale", "g_final", "g_ple_post", "g_ple_gate", "g_ffn",
                 "pool_w"]
    layout = [(0, 0, D_MODEL), (0, 1, C_CONV), (0, 2, C_CONV), (0, 3, C_CONV), (0, 4, C_POOL),
              (1, 0, D_MODEL), (1, 1, D_MODEL), (1, 2, D_MODEL), (2, 0, D_MODEL), (4, None, None)]
    as_row = lambda t: t.reshape(1, D_MODEL)
    params = [(g_mix, m_g_mix, v_g_mix), (ln_g, m_ln_g, v_ln_g), (ln_b, m_ln_b, v_ln_b),
              (conv_b, m_conv_b, v_conv_b), (pool_scale, m_pool_scale, v_pool_scale),
              (as_row(g_final), as_row(m_g_final), as_row(v_g_final)),
              (g_ple_post, m_g_ple_post, v_g_ple_post), (g_ple_gate, m_g_ple_gate, v_g_ple_gate),
              (g_ffn, m_g_ffn, v_g_ffn), (pool_w[0], m_pool_w[0], v_pool_w[0])]
    tots, small_upd = _small_update(small, layout, params, "small_update")
    loss = tots[1][3, 0]
    upd = {}
    for nm, res, prm in zip(vec_names, small_upd, [g_mix, ln_g, ln_b, conv_b, pool_scale, g_final, g_ple_post,
                                                    g_ple_gate, g_ffn, pool_w]):
        upd[nm] = tuple(t.reshape(prm.shape) for t in res)
    gr_conv_w = lax.dynamic_slice_in_dim(tots[3][:CONV_K], me * (C_CONV // N_DEV), C_CONV // N_DEV, axis=1)

    natural = lambda t: t[None]
    turned = lambda t: t.T[None]
    (res_pg, res_gu, res_out, res_down), _ = _sum_adamw([
        (early_parts[0], early_chips[0], w_ple_gate[0], m_w_ple_gate[0], v_w_ple_gate[0]),
        (early_parts[2], early_chips[2], w_gate_up[0].T, m_w_gate_up[0].T, v_w_gate_up[0].T),
        (early_parts[3], early_chips[3], w_out[0], m_w_out[0], v_w_out[0]),
        (down_parts[0], down_chips[0], w_down[0], m_w_down[0], v_w_down[0])], "adamw_big")
    upd["w_ple_gate"] = tuple(natural(t) for t in res_pg)
    upd["w_gate_up"] = tuple(turned(t) for t in res_gu)
    upd["w_out"] = tuple(natural(t) for t in res_out)
    upd["w_down"] = tuple(natural(t) for t in res_down)
    (gr_w_pu_t,) = _chip_sum(early_parts[1:2], early_chips[1:2], "chip_sum")
    plain = [
        ("w_in", w_in[0].T, gr_w_in_t, m_w_in[0].T, v_w_in[0].T, True),
        ("w_ple_up", w_ple_up[0], gr_w_pu_t.T, m_w_ple_up[0], v_w_ple_up[0], False),
        ("conv_w", conv_w[0], gr_conv_w, m_conv_w[0], v_conv_w[0], False),
    ]
    for nm, w_, g_, m_, v_, transposed in plain:
        res = (g_,) + tuple(_adamw(w_, g_, m_, v_, "adamw_" + nm))
        upd[nm] = tuple((t.T if transposed else t)[None] for t in res)

    order = ["g_mix", "w_in", "conv_w", "conv_b", "ln_g", "ln_b", "pool_w", "pool_scale", "w_out", "g_ffn",
             "w_gate_up", "w_down", "g_ple_gate", "w_ple_gate", "w_ple_up", "g_ple_post", "g_final"]
    outs = [loss, dx[None]]
    for k in range(4):
        outs += [upd[nm][k] for nm in order]
    return tuple(outs)
```

```python
import functools

import jax
import jax.numpy as jnp
from jax import lax
from jax.experimental import pallas as pl
from jax.experimental.pallas import tpu as pltpu

D_MODEL = 1024
C_CONV = 512
C_POOL = 512
Z_WIDTH = 2 * C_CONV + C_POOL
POOL_WINDOWS = (2, 4, 8, 16)
POOL_GROUP = 128
CONV_K = 31
D_FF = 2816
D_PLE = 256
EPS = 1e-6
N_DEV = 8

ADAM_LR = 0.001
ADAM_B1 = 0.9
ADAM_B2 = 0.999
ADAM_EPS = 1e-08
ADAM_WD = 0.01
ADAM_STEP = 10

CONV_HALO = 32
POOL_HALO = 32
ROW_CHUNK = 32
V7X_VMEM_LIMIT = 56 * 1024 * 1024
FF_CHUNK = D_FF // 2

BF16 = jnp.bfloat16
F32 = jnp.float32
MESH = pl.DeviceIdType.MESH


def _dot(a, b):
    return lax.dot_general(a, b, (((1,), (0,)), ((), ())), preferred_element_type=F32)


def _dot_nt(a, b):
    return lax.dot_general(a, b, (((1,), (1,)), ((), ())), preferred_element_type=F32)


def _dot_tn(a, b):
    return lax.dot_general(a, b, (((0,), (0,)), ((), ())), preferred_element_type=F32)


def _rms_fwd(x, g):
    r = lax.rsqrt(jnp.mean(x * x, axis=-1, keepdims=True) + EPS)
    return x * r * g, r


def _rms_bwd(x, r, g, dy):
    xr = x * r
    dg = jnp.sum(dy * xr, axis=0, keepdims=True)
    dyg = dy * g
    dx = r * (dyg - xr * jnp.mean(dyg * xr, axis=-1, keepdims=True))
    return dx, dg


def _sigmoid(x):
    return jax.nn.sigmoid(x)


def _params(sem=None):
    return pltpu.CompilerParams(dimension_semantics=sem, vmem_limit_bytes=V7X_VMEM_LIMIT)


def _place():
    x, y, c = lax.axis_index("x"), lax.axis_index("y"), lax.axis_index("c")
    chips = [(1 - x, y), (x, 1 - y), (1 - x, 1 - y)]
    return (x, y, c), (x, y, 1 - c), chips


def _block(px, py, pc):
    return 4 * px + 2 * py + pc


class _Carry:
    def __init__(self):
        self.inputs = []
        self.out_shapes = []
        self.copies = []
        self.locals = []

    def add_input(self, arr):
        self.inputs.append(arr)
        return len(self.inputs) - 1

    def add_output(self, shape, dtype):
        self.out_shapes.append(jax.ShapeDtypeStruct(shape, dtype))
        return len(self.out_shapes) - 1

    def local(self, src_idx, dst_idx, dst_blk):
        self.locals.append((src_idx, dst_idx, dst_blk))

    def copy(self, src, dst_idx, dst_blk, got_blk, peer, step=0, after=()):
        self.copies.append(dict(src=src, dst_idx=dst_idx, dst_blk=dst_blk, got_blk=got_blk, peer=peer, step=step,
                                after=tuple(after)))
        return len(self.copies) - 1

    def sem_shapes(self):
        return [pltpu.SemaphoreType.DMA((max(1, len(self.copies)),)),
                pltpu.SemaphoreType.DMA((max(1, len(self.copies)),)),
                pltpu.SemaphoreType.DMA((max(1, len(self.locals)),))]

    @staticmethod
    def _view(ref, where):
        if isinstance(where, tuple):
            blk, row0, nrows = where
            return ref.at[blk, pl.ds(row0, nrows)]
        return ref.at[where]

    def _desc(self, k, ins, outs, sems, place):
        cp = self.copies[k]
        me, sib, chips = place
        kind, idx, blk = cp["src"]
        src = (ins if kind == "in" else outs)[idx]
        if blk is not None:
            src = self._view(src, blk(*place))
        to = sib if cp["peer"] == "sib" else (*chips[cp["peer"]], me[2])
        return pltpu.make_async_remote_copy(
            src_ref=src, dst_ref=self._view(outs[cp["dst_idx"]], cp["dst_blk"](*place)),
            send_sem=sems[0].at[k], recv_sem=sems[1].at[k], device_id=to, device_id_type=MESH)

    def _arrival(self, k, outs, sems, place):
        cp = self.copies[k]
        got = self._view(outs[cp["dst_idx"]], cp["got_blk"](*place))
        return pltpu.make_async_remote_copy(
            src_ref=got, dst_ref=got, send_sem=sems[0].at[k], recv_sem=sems[1].at[k],
            device_id=place[0], device_id_type=MESH)

    def _local(self, n, ins, outs, sems, place):
        src_idx, dst_idx, blk = self.locals[n]
        return pltpu.make_async_copy(ins[src_idx], outs[dst_idx].at[blk(*place)], sems[2].at[n])

    def stages(self):
        return sorted({0} | {cp["step"] for cp in self.copies})

    def stage(self, s, ins, outs, sems):
        place = _place()
        if s == 0:
            self._waited = set()
            for n in range(len(self.locals)):
                self._local(n, ins, outs, sems, place).start()
        for k, cp in enumerate(self.copies):
            if cp["step"] != s:
                continue
            for a in cp["after"]:
                if a not in self._waited:
                    self._arrival(a, outs, sems, place).wait_recv()
                    self._waited.add(a)
            self._desc(k, ins, outs, sems, place).start()

    def drain(self, ins, outs, sems):
        place = _place()
        for k in range(len(self.copies)):
            if k not in self._waited:
                self._arrival(k, outs, sems, place).wait_recv()
        for k in range(len(self.copies)):
            self._desc(k, ins, outs, sems, place).wait_send()
        for n in range(len(self.locals)):
            self._local(n, ins, outs, sems, place).wait()

    def starts(self, step, nsteps, ins, outs, sems):
        for s in self.stages():
            pl.when(step == min(s, nsteps - 1))(functools.partial(self.stage, s, ins, outs, sems))

    def finish(self, step, nsteps, ins, outs, sems):
        pl.when(step == nsteps - 1)(functools.partial(self.drain, ins, outs, sems))


def _const_blk(j):
    return lambda me, sib, chips: j


def _carry_gather(carry, shards, relay_step, last_step):
    outs = []
    for sh in shards:
        i = carry.add_input(sh)
        o = carry.add_output((N_DEV,) + sh.shape, sh.dtype)
        half = sh.shape[0] // 2
        tile = 16 if sh.dtype == BF16 else 8
        split = half % tile == 0
        rows = [(0, half), (half, sh.shape[0] - half)] if split else [(0, sh.shape[0]), None]

        def whole(j, core):
            return lambda me, sib, chips, j=j, core=core: _block(*chips[j], me[2] if core == 0 else 1 - me[2])

        def part(j, core, h, rows=rows):
            return lambda me, sib, chips: (_block(*chips[j], me[2] if core == 0 else 1 - me[2]),) + rows[h]

        mine = lambda me, sib, chips: _block(*me)
        carry.local(i, o, mine)
        carry.copy(("in", i, None), o, mine, lambda me, sib, chips: _block(*sib), "sib")
        near = [carry.copy(("in", i, None), o, mine, whole(j, 0), j) for j in range(2)]
        for j in range(2):
            carry.copy(("out", o, whole(j, 0)), o, whole(j, 0), whole(j, 1), "sib", step=relay_step, after=(near[j],))
        for j in range(2):
            if rows[j] is None:
                continue
            far = carry.copy(("out", o, part(j, 0, j)), o, part(j, 0, j), part(2, 0, j), 1 - j,
                             step=relay_step, after=(near[j],))
            carry.copy(("out", o, part(2, 0, j)), o, part(2, 0, j), part(2, 1, j), "sib", step=last_step, after=(far,))
        outs.append(o)
    return outs


def _carry_pair(carry, grads):
    outs = []
    for g in grads:
        i = carry.add_input(g)
        o = carry.add_output((4,) + g.shape[1:], g.dtype)
        for j in range(4):
            if j < 3:
                blk = lambda me, sib, chips, j=j: _block(*chips[j], 1 - me[2])
            else:
                blk = lambda me, sib, chips: _block(*sib)
            carry.copy(("in", i, blk), o, _const_blk(j), _const_blk(j), "sib")
        outs.append(o)
    return outs


def _carry_chip(carry, parts):
    outs = []
    for p in parts:
        i = carry.add_input(p)
        o = carry.add_output((3,) + p.shape[1:], p.dtype)
        for j in range(3):
            carry.copy(("in", i, _const_blk(j)), o, _const_blk(j), _const_blk(j), j)
        outs.append(o)
    return outs


def _pcall(body, *, grid, in_specs, out_specs, out_shape, scratch_shapes, name, args, carry=None):
    sem = ("arbitrary",) * len(grid)
    if carry is None:
        res = pl.pallas_call(body, grid=grid, in_specs=in_specs, out_specs=out_specs, out_shape=out_shape,
                             scratch_shapes=scratch_shapes, compiler_params=_params(sem), name=name)(*args)
        return list(res), []
    n_in, n_out, n_scr = len(in_specs), len(out_specs), len(scratch_shapes)
    c_in, c_out = len(carry.inputs), len(carry.out_shapes)
    nsteps = 1
    for extent in grid:
        nsteps *= extent

    def wrapped(*refs):
        ins = refs[:n_in]
        cins = refs[n_in:n_in + c_in]
        o0 = n_in + c_in
        outs = refs[o0:o0 + n_out]
        couts = refs[o0 + n_out:o0 + n_out + c_out]
        s0 = o0 + n_out + c_out
        scr = refs[s0:s0 + n_scr]
        sems = refs[s0 + n_scr:]
        step = pl.program_id(0)
        for d in range(1, len(grid)):
            step = step * grid[d] + pl.program_id(d)
        carry.starts(step, nsteps, cins, couts, sems)
        body(*ins, *outs, *scr)
        carry.finish(step, nsteps, cins, couts, sems)

    any_spec = pl.BlockSpec(memory_space=pl.ANY)
    res = pl.pallas_call(
        wrapped, grid=grid,
        in_specs=list(in_specs) + [any_spec] * c_in,
        out_specs=list(out_specs) + [any_spec] * c_out,
        out_shape=list(out_shape) + carry.out_shapes,
        scratch_shapes=list(scratch_shapes) + carry.sem_shapes(),
        compiler_params=_params(sem), name=name)(*args, *carry.inputs)
    return list(res[:n_out]), list(res[n_out:])


def _shifted_copies(buf, shifted, tm):
    span = tm + CONV_HALO - 8
    for r in range(1, 8):
        shifted[r - 1, 0:span, :] = buf[r:r + span, :]


def _rows_at(buf, shifted, start):
    aligned, r = (start // 8) * 8, start % 8
    if r == 0:
        return buf[aligned:aligned + ROW_CHUNK, :]
    return shifted[r - 1, aligned:aligned + ROW_CHUNK, :]


def _window_sums(buf, cols, work, levels, tm, trailing):
    src, src_cols = buf, cols
    for k in range(levels + 1):
        shift = 1 << k
        dst = work.at[k % 2]
        if trailing:
            lo = 8 * (k + 1)
            dst[lo:tm + POOL_HALO, :] = (src[lo:tm + POOL_HALO, src_cols]
                                         + src[lo - shift:tm + POOL_HALO - shift, src_cols])
        else:
            hi = tm + POOL_HALO - 8 * (k + 1)
            dst[0:hi, :] = src[0:hi, src_cols] + src[shift:hi + shift, src_cols]
        src, src_cols = dst, slice(0, POOL_GROUP)
    return src[POOL_HALO:POOL_HALO + tm, src_cols] if trailing else src[0:tm, src_cols]


def _row_spec(tm, width):
    return pl.BlockSpec((tm, width), lambda i: (i, 0))


def _const_spec(shape):
    return pl.BlockSpec(shape, lambda i: (0,) * len(shape))


def _weight_spec(shape):
    return pl.BlockSpec(shape, lambda i: (0,) * len(shape), pipeline_mode=pl.Buffered(1))


def _to_bf16(arrays):
    n = len(arrays)

    def body(*refs):
        for a in range(n):
            refs[n + a][...] = refs[a][...].astype(BF16)

    whole = [pl.BlockSpec(arr.shape, lambda i, nd=arr.ndim: (0,) * nd) for arr in arrays]
    return pl.pallas_call(
        body,
        grid=(1,),
        in_specs=whole,
        out_specs=whole,
        out_shape=[jax.ShapeDtypeStruct(arr.shape, BF16) for arr in arrays],
        compiler_params=_params(("arbitrary",)),
        name="to_bf16",
    )(*arrays)


def _norm_in(x, g_mix, tm, carry=None):
    s = x.shape[0]

    def body(x_ref, gmix_ref, h1_ref):
        h, _ = _rms_fwd(x_ref[...], gmix_ref[...])
        h1_ref[...] = h.astype(BF16)

    return _pcall(
        body,
        grid=(s // tm,),
        in_specs=[_row_spec(tm, D_MODEL), _const_spec((1, D_MODEL))],
        out_specs=[_row_spec(tm, D_MODEL)],
        out_shape=[jax.ShapeDtypeStruct((s, D_MODEL), BF16)],
        scratch_shapes=[],
        name="norm_in",
        args=(x, g_mix),
        carry=carry,
    )


def _fwd_mix(x, h1, w_in_t, w_out, conv_w, conv_b, ln_g, ln_b, pool_w, pool_scale, tm, carry=None):
    s = x.shape[0]
    nt = s // tm

    def body(x_ref, h1_ref, win_ref, wout_ref, cw_ref, cb_ref, lng_ref, lnb_ref, pw_ref, ps_ref,
             z_ref, u1_ref, pooled_ref, x1_ref, mix_ref, ubuf, vbuf, ush, pbuf):
        i = pl.program_id(0)

        @pl.when(i == 0)
        def _():
            ubuf[0:CONV_HALO, :] = jnp.zeros((CONV_HALO, C_CONV), F32)
            vbuf[0:POOL_HALO, :] = jnp.zeros((POOL_HALO, C_POOL), F32)

        xt = x_ref[...]
        z = _dot_nt(h1_ref[...], win_ref[...])
        z_ref[...] = z
        a = z[:, :C_CONV]
        b = z[:, C_CONV:2 * C_CONV]
        v = z[:, 2 * C_CONV:]
        ubuf[CONV_HALO:CONV_HALO + tm, :] = a * _sigmoid(b)
        vbuf[POOL_HALO:POOL_HALO + tm, :] = v

        _shifted_copies(ubuf, ush, tm)
        for rc in range(tm // ROW_CHUNK):
            base = rc * ROW_CHUNK + CONV_HALO - (CONV_K - 1)
            acc = jnp.broadcast_to(cb_ref[...], (ROW_CHUNK, C_CONV))
            for k in range(CONV_K):
                acc = acc + cw_ref[k:k + 1, :] * _rows_at(ubuf, ush, base + k)
            u1_ref[rc * ROW_CHUNK:(rc + 1) * ROW_CHUNK, :] = acc

        u1 = u1_ref[...]
        mu = jnp.mean(u1, axis=-1, keepdims=True)
        cen = u1 - mu
        rstd = lax.rsqrt(jnp.mean(cen * cen, axis=-1, keepdims=True) + EPS)
        u2 = cen * rstd * lng_ref[...] + lnb_ref[...]
        u = u2 * _sigmoid(u2)

        pos1 = (i * tm + lax.broadcasted_iota(jnp.int32, (tm, 1), 0) + 1).astype(F32)
        parts = [u]
        for g, w in enumerate(POOL_WINDOWS):
            cols = slice(g * POOL_GROUP, (g + 1) * POOL_GROUP)
            vg = v[:, cols]
            tot = _window_sums(vbuf, cols, pbuf, g, tm, trailing=True)
            pooled = tot / jnp.minimum(pos1, float(w)) - vg
            pooled_b = pooled.astype(BF16)
            pooled_ref[:, cols] = pooled_b
            parts.append(_dot(pooled_b, pw_ref[g]) * ps_ref[:, cols])
        mix = jnp.concatenate(parts, axis=-1).astype(BF16)
        mix_ref[...] = mix
        x1_ref[...] = xt + _dot(mix, wout_ref[...])

        ubuf[0:CONV_HALO, :] = ubuf[tm:tm + CONV_HALO, :]
        vbuf[0:POOL_HALO, :] = vbuf[tm:tm + POOL_HALO, :]

    return _pcall(
        body,
        grid=(nt,),
        in_specs=[
            _row_spec(tm, D_MODEL),
            _row_spec(tm, D_MODEL),
            _const_spec((Z_WIDTH, D_MODEL)),
            _const_spec((D_MODEL, D_MODEL)),
            _const_spec((CONV_HALO, C_CONV)),
            _const_spec((1, C_CONV)),
            _const_spec((1, C_CONV)),
            _const_spec((1, C_CONV)),
            _const_spec((len(POOL_WINDOWS), POOL_GROUP, POOL_GROUP)),
            _const_spec((1, C_POOL)),
        ],
        out_specs=[
            _row_spec(tm, Z_WIDTH),
            _row_spec(tm, C_CONV),
            _row_spec(tm, C_POOL),
            _row_spec(tm, D_MODEL),
            _row_spec(tm, D_MODEL),
        ],
        out_shape=[
            jax.ShapeDtypeStruct((s, Z_WIDTH), F32),
            jax.ShapeDtypeStruct((s, C_CONV), F32),
            jax.ShapeDtypeStruct((s, C_POOL), BF16),
            jax.ShapeDtypeStruct((s, D_MODEL), F32),
            jax.ShapeDtypeStruct((s, D_MODEL), BF16),
        ],
        scratch_shapes=[
            pltpu.VMEM((tm + CONV_HALO, C_CONV), F32),
            pltpu.VMEM((tm + POOL_HALO, C_POOL), F32),
            pltpu.VMEM((7, tm + CONV_HALO, C_CONV), F32),
            pltpu.VMEM((2, tm + POOL_HALO, POOL_GROUP), F32),
        ],
        name="fwd_mix",
        args=(x, h1, w_in_t, w_out, conv_w, conv_b, ln_g, ln_b, pool_w, pool_scale),
        carry=carry,
    )


def _ffn_up(x1, w_gu_t, g_ffn, tm, carry=None):
    s = x1.shape[0]

    def body(x1_ref, w_ref, gffn_ref, h2_ref, g_ref, u_ref, act_ref):
        h, _ = _rms_fwd(x1_ref[...], gffn_ref[...])
        h2 = h.astype(BF16)
        h2_ref[...] = h2
        for c in range(D_FF // FF_CHUNK):
            cols = slice(c * FF_CHUNK, (c + 1) * FF_CHUNK)
            g = _dot_nt(h2, w_ref[c * FF_CHUNK:(c + 1) * FF_CHUNK, :])
            u = _dot_nt(h2, w_ref[D_FF + c * FF_CHUNK:D_FF + (c + 1) * FF_CHUNK, :])
            g_ref[:, cols] = g.astype(BF16)
            u_ref[:, cols] = u.astype(BF16)
            act_ref[:, cols] = (g * _sigmoid(g) * u).astype(BF16)

    return _pcall(
        body,
        grid=(s // tm,),
        in_specs=[_row_spec(tm, D_MODEL), _weight_spec((2 * D_FF, D_MODEL)), _const_spec((1, D_MODEL))],
        out_specs=[_row_spec(tm, D_MODEL), _row_spec(tm, D_FF), _row_spec(tm, D_FF), _row_spec(tm, D_FF)],
        out_shape=[
            jax.ShapeDtypeStruct((s, D_MODEL), BF16),
            jax.ShapeDtypeStruct((s, D_FF), BF16),
            jax.ShapeDtypeStruct((s, D_FF), BF16),
            jax.ShapeDtypeStruct((s, D_FF), BF16),
        ],
        scratch_shapes=[],
        name="ffn_up",
        args=(x1, w_gu_t, g_ffn),
        carry=carry,
    )


def _down_ple(x1, act, w_down, p, tgt, w_pg, w_pu_t, g_gate, g_post, g_final, tm):
    s = x1.shape[0]
    nt = s // tm

    def body(x1_ref, act_ref, wd_ref, p_ref, t_ref, wpg_ref, wpu_ref, gg_ref, gp_ref, gf_ref,
             dx2_ref, dx2b_ref, hg_ref, ds_ref, dpe_ref, pb_ref, stats_ref, x2_cur, x2_next):
        i = pl.program_id(0)

        @pl.when(i == 0)
        def _():
            stats_ref[...] = jnp.zeros_like(stats_ref)
            x2_cur[...] = jnp.zeros((tm, D_MODEL), F32)

        def down(c):
            cols = slice(c * 256, (c + 1) * 256)
            x2_next[:, cols] = x1_ref[:, cols] + _dot(act_ref[...], wd_ref[:, cols])

        x2 = x2_cur[...]
        counts = i >= 1

        hg, rg = _rms_fwd(x2, gg_ref[...])
        hg_b = hg.astype(BF16)
        hg_ref[...] = hg_b
        down(0)
        gate = _sigmoid(_dot(hg_b, wpg_ref[...]))
        pb = p_ref[...].astype(BF16)
        pb_ref[...] = pb
        pe = _dot_nt(pb, wpu_ref[...])
        e, rp = _rms_fwd(pe, gp_ref[...])
        x3 = x2 + gate * e
        down(1)
        y, r3 = _rms_fwd(x3, gf_ref[...])
        diff = y - t_ref[...]
        loss = 0.5 * jnp.sum(jnp.sum(diff * diff, axis=-1, keepdims=True), axis=0, keepdims=True) / D_MODEL
        dy = diff * (1.0 / D_MODEL)

        dx3, dgf = _rms_bwd(x3, r3, gf_ref[...], dy)
        down(2)
        dpe, dgp = _rms_bwd(pe, rp, gp_ref[...], dx3 * gate)
        dpe_ref[...] = dpe.astype(BF16)
        ds = (dx3 * e * gate * (1.0 - gate)).astype(BF16)
        ds_ref[...] = ds
        dhg = _dot_nt(ds, wpg_ref[...])
        down(3)
        dxg, dgg = _rms_bwd(x2, rg, gg_ref[...], dhg)
        dx2 = dx3 + dxg
        dx2_ref[...] = dx2
        dx2b_ref[...] = dx2.astype(BF16)
        x2_cur[...] = x2_next[...]

        stats_ref[0:1, :] += jnp.where(counts, dgf, 0.0)
        stats_ref[1:2, :] += jnp.where(counts, dgp, 0.0)
        stats_ref[2:3, :] += jnp.where(counts, dgg, 0.0)
        stats_ref[3:4, :] += jnp.where(counts, jnp.broadcast_to(loss, (1, D_MODEL)), 0.0)

    ahead = lambda width: pl.BlockSpec((tm, width), lambda i: (jnp.minimum(i, nt - 1), 0))
    behind = lambda width: pl.BlockSpec((tm, width), lambda i: (jnp.maximum(i - 1, 0), 0))
    return pl.pallas_call(
        body,
        grid=(nt + 1,),
        in_specs=[
            ahead(D_MODEL),
            ahead(D_FF),
            _weight_spec((D_FF, D_MODEL)),
            behind(D_PLE),
            behind(D_MODEL),
            _weight_spec((D_MODEL, D_MODEL)),
            _weight_spec((D_MODEL, D_PLE)),
            _const_spec((1, D_MODEL)),
            _const_spec((1, D_MODEL)),
            _const_spec((1, D_MODEL)),
        ],
        out_specs=[
            behind(D_MODEL),
            behind(D_MODEL),
            behind(D_MODEL),
            behind(D_MODEL),
            behind(D_MODEL),
            behind(D_PLE),
            _const_spec((8, D_MODEL)),
        ],
        out_shape=[
            jax.ShapeDtypeStruct((s, D_MODEL), F32),
            jax.ShapeDtypeStruct((s, D_MODEL), BF16),
            jax.ShapeDtypeStruct((s, D_MODEL), BF16),
            jax.ShapeDtypeStruct((s, D_MODEL), BF16),
            jax.ShapeDtypeStruct((s, D_MODEL), BF16),
            jax.ShapeDtypeStruct((s, D_PLE), BF16),
            jax.ShapeDtypeStruct((8, D_MODEL), F32),
        ],
        scratch_shapes=[pltpu.VMEM((tm, D_MODEL), F32), pltpu.VMEM((tm, D_MODEL), F32)],
        compiler_params=_params(("arbitrary",)),
        name="down_ple",
    )(x1, act, w_down, p, tgt, w_pg, w_pu_t, g_gate, g_post, g_final)


def _ffn_bwd(dx2, dx2b, x1, g_sav, u_sav, w_gu_t, w_down, g_ffn, tm, carry=None):
    s = x1.shape[0]

    def body(dx2_ref, dx2b_ref, x1_ref, g_ref, u_ref, w_ref, wd_ref, gffn_ref,
             dg_ref, du_ref, dx1_ref, dx1b_ref, stats_ref):
        @pl.when(pl.program_id(0) == 0)
        def _():
            stats_ref[...] = jnp.zeros_like(stats_ref)

        dx2b = dx2b_ref[...]
        nc = D_FF // FF_CHUNK
        dacts = [_dot_nt(dx2b, wd_ref[c * FF_CHUNK:(c + 1) * FF_CHUNK, :]) for c in range(nc)]
        dh2 = jnp.zeros((tm, D_MODEL), F32)
        for c in range(nc):
            cols = slice(c * FF_CHUNK, (c + 1) * FF_CHUNK)
            g = g_ref[:, cols].astype(F32)
            u = u_ref[:, cols].astype(F32)
            sg = _sigmoid(g)
            dg = (dacts[c] * u * sg * (1.0 + g * (1.0 - sg))).astype(BF16)
            du = (dacts[c] * g * sg).astype(BF16)
            dg_ref[:, cols] = dg
            du_ref[:, cols] = du
            dh2 = dh2 + _dot(dg, w_ref[c * FF_CHUNK:(c + 1) * FF_CHUNK, :])
            dh2 = dh2 + _dot(du, w_ref[D_FF + c * FF_CHUNK:D_FF + (c + 1) * FF_CHUNK, :])

        x1 = x1_ref[...]
        r2 = lax.rsqrt(jnp.mean(x1 * x1, axis=-1, keepdims=True) + EPS)
        dxn, dgf = _rms_bwd(x1, r2, gffn_ref[...], dh2)
        dx1 = dx2_ref[...] + dxn
        dx1_ref[...] = dx1
        dx1b_ref[...] = dx1.astype(BF16)
        stats_ref[0:1, :] += dgf

    return _pcall(
        body,
        grid=(s // tm,),
        in_specs=[
            _row_spec(tm, D_MODEL), _row_spec(tm, D_MODEL), _row_spec(tm, D_MODEL),
            _row_spec(tm, D_FF), _row_spec(tm, D_FF),
            _weight_spec((2 * D_FF, D_MODEL)), _weight_spec((D_FF, D_MODEL)), _const_spec((1, D_MODEL)),
        ],
        out_specs=[_row_spec(tm, D_FF), _row_spec(tm, D_FF), _row_spec(tm, D_MODEL), _row_spec(tm, D_MODEL),
                   _const_spec((8, D_MODEL))],
        out_shape=[
            jax.ShapeDtypeStruct((s, D_FF), BF16),
            jax.ShapeDtypeStruct((s, D_FF), BF16),
            jax.ShapeDtypeStruct((s, D_MODEL), F32),
            jax.ShapeDtypeStruct((s, D_MODEL), BF16),
            jax.ShapeDtypeStruct((8, D_MODEL), F32),
        ],
        scratch_shapes=[],
        name="ffn_bwd",
        args=(dx2, dx2b, x1, g_sav, u_sav, w_gu_t, w_down, g_ffn),
        carry=carry,
    )


def _bwd_mix(dx1, dx1b, x, z, u1, pooled, w_in_t, w_out, g_mix, conv_w, ln_g, ln_b, pool_w, pool_scale, tm,
             carry=None):
    s = x.shape[0]
    nt = s // tm

    def body(dx1_ref, dx1b_ref, x_ref, z_ref, u1_ref, pooled_ref, win_ref, wout_ref, gmix_ref, cw_ref,
             lng_ref, lnb_ref, pw_ref, ps_ref,
             dx_ref, dz_ref, vec_ref, dcw_ref, dpw_ref, dubuf, dvbuf, u0buf, du0buf, dush, pbuf):
        i = pl.program_id(0)
        tile = nt - 1 - i

        @pl.when(i == 0)
        def _():
            vec_ref[...] = jnp.zeros_like(vec_ref)
            dcw_ref[...] = jnp.zeros_like(dcw_ref)
            dpw_ref[...] = jnp.zeros_like(dpw_ref)
            dubuf[tm:tm + CONV_HALO, :] = jnp.zeros((CONV_HALO, C_CONV), F32)
            dvbuf[tm:tm + POOL_HALO, :] = jnp.zeros((POOL_HALO, C_POOL), F32)

        dmix = _dot_nt(dx1b_ref[...], wout_ref[...])
        du = dmix[:, :C_CONV]
        dq = dmix[:, C_CONV:]

        pos1 = (tile * tm + lax.broadcasted_iota(jnp.int32, (tm, 1), 0) + 1).astype(F32)
        dpooled_parts = []
        dps_rows = []
        for g, w in enumerate(POOL_WINDOWS):
            cols = slice(g * POOL_GROUP, (g + 1) * POOL_GROUP)
            pooled_b = pooled_ref[:, cols]
            mixed = _dot(pooled_b, pw_ref[g])
            dqg = dq[:, cols]
            dps_rows.append(jnp.sum(dqg * mixed, axis=0, keepdims=True))
            dmixed = (dqg * ps_ref[:, cols]).astype(BF16)
            dpw_ref[g] += _dot_tn(pooled_b, dmixed)
            dpooled = _dot_nt(dmixed, pw_ref[g])
            dpooled_parts.append(dpooled)
            dvbuf[0:tm, cols] = dpooled / jnp.minimum(pos1, float(w))
        vec_ref[4:5, 0:C_POOL] += jnp.concatenate(dps_rows, axis=-1)
        dv_parts = []
        for g, w in enumerate(POOL_WINDOWS):
            cols = slice(g * POOL_GROUP, (g + 1) * POOL_GROUP)
            tot = _window_sums(dvbuf, cols, pbuf, g, tm, trailing=False)
            dv_parts.append(tot - dpooled_parts[g])

        u1 = u1_ref[...]
        mu = jnp.mean(u1, axis=-1, keepdims=True)
        cen = u1 - mu
        rstd = lax.rsqrt(jnp.mean(cen * cen, axis=-1, keepdims=True) + EPS)
        xhat = cen * rstd
        u2 = xhat * lng_ref[...] + lnb_ref[...]
        sg2 = _sigmoid(u2)
        du2 = du * sg2 * (1.0 + u2 * (1.0 - sg2))
        vec_ref[1:2, 0:C_CONV] += jnp.sum(du2 * xhat, axis=0, keepdims=True)
        vec_ref[2:3, 0:C_CONV] += jnp.sum(du2, axis=0, keepdims=True)
        t1 = du2 * lng_ref[...]
        du1 = rstd * (t1 - jnp.mean(t1, axis=-1, keepdims=True)
                      - xhat * jnp.mean(t1 * xhat, axis=-1, keepdims=True))
        vec_ref[3:4, 0:C_CONV] += jnp.sum(du1, axis=0, keepdims=True)
        dubuf[0:tm, :] = du1

        zt = z_ref[...]
        a = zt[:, :C_CONV]
        sgb = _sigmoid(zt[:, C_CONV:2 * C_CONV])
        u0buf[...] = a * sgb

        _shifted_copies(dubuf, dush, tm)
        for rc in range(tm // ROW_CHUNK):
            r0 = rc * ROW_CHUNK
            acc = jnp.zeros((ROW_CHUNK, C_CONV), F32)
            for k in range(CONV_K):
                acc = acc + cw_ref[k:k + 1, :] * _rows_at(dubuf, dush, r0 + (CONV_K - 1) - k)
            du0buf[r0:r0 + ROW_CHUNK, :] = acc
        for k in range(CONV_K):
            acc = jnp.zeros((ROW_CHUNK, C_CONV), F32)
            for rc in range(tm // ROW_CHUNK):
                r0 = rc * ROW_CHUNK
                acc = acc + u0buf[r0:r0 + ROW_CHUNK, :] * _rows_at(dubuf, dush, r0 + (CONV_K - 1) - k)
            dcw_ref[k:k + 1, :] += jnp.sum(acc, axis=0, keepdims=True)
        du0 = du0buf[...]

        da = du0 * sgb
        db = du0 * a * sgb * (1.0 - sgb)
        dz = jnp.concatenate([da, db] + dv_parts, axis=-1).astype(BF16)
        dz_ref[...] = dz

        xt = x_ref[...]
        r1 = lax.rsqrt(jnp.mean(xt * xt, axis=-1, keepdims=True) + EPS)
        dh1 = _dot(dz, win_ref[...])
        dxn, dgm = _rms_bwd(xt, r1, gmix_ref[...], dh1)
        dx_ref[...] = dx1_ref[...] + dxn
        vec_ref[0:1, :] += dgm

        dubuf[tm:tm + CONV_HALO, :] = dubuf[0:CONV_HALO, :]
        dvbuf[tm:tm + POOL_HALO, :] = dvbuf[0:POOL_HALO, :]

    rev = lambda width: pl.BlockSpec((tm, width), lambda i: (nt - 1 - i, 0))
    return _pcall(
        body,
        grid=(nt,),
        in_specs=[
            rev(D_MODEL), rev(D_MODEL), rev(D_MODEL), rev(Z_WIDTH), rev(C_CONV), rev(C_POOL),
            _const_spec((Z_WIDTH, D_MODEL)),
            _const_spec((D_MODEL, D_MODEL)),
            _const_spec((1, D_MODEL)),
            _const_spec((CONV_HALO, C_CONV)),
            _const_spec((1, C_CONV)),
            _const_spec((1, C_CONV)),
            _const_spec((len(POOL_WINDOWS), POOL_GROUP, POOL_GROUP)),
            _const_spec((1, C_POOL)),
        ],
        out_specs=[
            rev(D_MODEL), rev(Z_WIDTH),
            _const_spec((8, D_MODEL)),
            _const_spec((CONV_HALO, C_CONV)),
            _const_spec((len(POOL_WINDOWS), POOL_GROUP, POOL_GROUP)),
        ],
        out_shape=[
            jax.ShapeDtypeStruct((s, D_MODEL), F32),
            jax.ShapeDtypeStruct((s, Z_WIDTH), BF16),
            jax.ShapeDtypeStruct((8, D_MODEL), F32),
            jax.ShapeDtypeStruct((CONV_HALO, C_CONV), F32),
            jax.ShapeDtypeStruct((len(POOL_WINDOWS), POOL_GROUP, POOL_GROUP), F32),
        ],
        scratch_shapes=[
            pltpu.VMEM((tm + CONV_HALO, C_CONV), F32),
            pltpu.VMEM((tm + POOL_HALO, C_POOL), F32),
            pltpu.VMEM((tm, C_CONV), F32),
            pltpu.VMEM((tm, C_CONV), F32),
            pltpu.VMEM((7, tm + CONV_HALO, C_CONV), F32),
            pltpu.VMEM((2, tm + POOL_HALO, POOL_GROUP), F32),
        ],
        name="bwd_mix",
        args=(dx1, dx1b, x, z, u1, pooled, w_in_t, w_out, g_mix, conv_w, ln_g, ln_b, pool_w, pool_scale),
        carry=carry,
    )


def _grad_matmul(a, b, bm, name, a2=None, carry=None):
    s, ma = a.shape
    nb = b.shape[1]
    na = ma // bm
    if a2 is None:
        def body(a_ref, b_ref, o_ref):
            o_ref[...] = _dot_tn(a_ref[...], b_ref[...]).astype(BF16)

        lhs_specs = [pl.BlockSpec((s, bm), lambda i: (0, i))]
        lhs = (a,)
        steps = na
    else:
        def body(a_ref, a2_ref, b_ref, o_ref):
            i = pl.program_id(0)

            @pl.when(i < na)
            def _():
                o_ref[...] = _dot_tn(a_ref[...], b_ref[...]).astype(BF16)

            @pl.when(i >= na)
            def _():
                o_ref[...] = _dot_tn(a2_ref[...], b_ref[...]).astype(BF16)

        lhs_specs = [pl.BlockSpec((s, bm), lambda i: (0, jnp.minimum(i, na - 1))),
                     pl.BlockSpec((s, bm), lambda i: (0, jnp.maximum(i - na, 0)))]
        lhs = (a, a2)
        steps = 2 * na

    outs, carried = _pcall(
        body,
        grid=(steps,),
        in_specs=lhs_specs + [pl.BlockSpec((s, nb), lambda i: (0, 0))],
        out_specs=[pl.BlockSpec((bm, nb), lambda i: (i, 0))],
        out_shape=[jax.ShapeDtypeStruct((steps * bm, nb), BF16)],
        scratch_shapes=[],
        name=name,
        args=lhs + (b,),
        carry=carry,
    )
    return outs[0], carried


def _reduce_scatter(grads, small, name):
    n, ns = len(grads), len(small)
    shapes = [g.shape[1:] for g in grads]
    gather = _Carry()
    _carry_gather(gather, small, 1, 2)

    def body(*refs):
        g = refs[:n]
        out = refs[n + ns:2 * n + ns]
        scr = refs[2 * (n + ns):]
        own, loc, r1, r2 = scr[:n], scr[n:2 * n], scr[2 * n:3 * n], scr[3 * n:4 * n]
        load_sems, s1, q1, s2, q2 = scr[4 * n:4 * n + 5]
        gather_refs = (refs[n:n + ns], refs[2 * n + ns:2 * (n + ns)], scr[4 * n + 5:])
        me, sib, chips = _place()
        c = me[2]

        gather.stage(0, *gather_refs)
        loads = []
        sends = []
        for a in range(n):
            ld = [pltpu.make_async_copy(g[a].at[_block(*chip, c)], loc[a].at[j], load_sems.at[a, j])
                  for j, chip in enumerate(chips)]
            ld.append(pltpu.make_async_copy(g[a].at[_block(*me)], own[a], load_sems.at[a, 3]))
            for cp in ld:
                cp.start()
            loads.append(ld)
            blocks = [(*chip, 1 - c) for chip in chips] + [sib]
            for j, blk in enumerate(blocks):
                cp = pltpu.make_async_remote_copy(
                    src_ref=g[a].at[_block(*blk)], dst_ref=r1[a].at[j],
                    send_sem=s1.at[a, j], recv_sem=q1.at[a, j], device_id=sib, device_id_type=MESH)
                cp.start()
                sends.append(cp)

        def from_sibling(a, j):
            return pltpu.make_async_remote_copy(
                src_ref=r1[a].at[j], dst_ref=r1[a].at[j], send_sem=s1.at[a, j], recv_sem=q1.at[a, j],
                device_id=sib, device_id_type=MESH)

        def partial(a, j, chip):
            return pltpu.make_async_remote_copy(
                src_ref=loc[a].at[j], dst_ref=r2[a].at[j], send_sem=s2.at[a, j], recv_sem=q2.at[a, j],
                device_id=(*chip, c), device_id_type=MESH)

        gather.stage(1, *gather_refs)
        for a in range(n):
            for j, chip in enumerate(chips):
                loads[a][j].wait()
                from_sibling(a, j).wait_recv()
                loc[a][j] = (loc[a][j].astype(F32) + r1[a][j].astype(F32)).astype(BF16)
                cp = partial(a, j, chip)
                cp.start()
                sends.append(cp)
        gather.stage(2, *gather_refs)
        for a in range(n):
            loads[a][3].wait()
            from_sibling(a, 3).wait_recv()
            acc = own[a][...].astype(F32) + r1[a][3].astype(F32)
            for j, chip in enumerate(chips):
                partial(a, j, chip).wait_recv()
                acc = acc + r2[a][j].astype(F32)
            out[a][...] = acc
        for cp in sends:
            cp.wait_send()
        gather.drain(*gather_refs)

    any_spec = pl.BlockSpec(memory_space=pl.ANY)
    vmem_spec = pl.BlockSpec(memory_space=pltpu.VMEM)
    res = pl.pallas_call(
        body,
        in_specs=[any_spec] * (n + ns),
        out_specs=[vmem_spec] * n + [any_spec] * ns,
        out_shape=[jax.ShapeDtypeStruct(sh, F32) for sh in shapes] + gather.out_shapes,
        scratch_shapes=(
            [pltpu.VMEM(sh, BF16) for sh in shapes]
            + [pltpu.VMEM((3,) + sh, BF16) for sh in shapes]
            + [pltpu.VMEM((4,) + sh, BF16) for sh in shapes]
            + [pltpu.VMEM((3,) + sh, BF16) for sh in shapes]
            + [pltpu.SemaphoreType.DMA((n, 4)),
               pltpu.SemaphoreType.DMA((n, 4)), pltpu.SemaphoreType.DMA((n, 4)),
               pltpu.SemaphoreType.DMA((n, 3)), pltpu.SemaphoreType.DMA((n, 3))]
            + gather.sem_shapes()
        ),
        compiler_params=pltpu.CompilerParams(vmem_limit_bytes=V7X_VMEM_LIMIT),
        name=name,
    )(*grads, *small)
    return res[:n], res[n:]


def _pair_add(grads, from_sib, blks, name):
    n = len(grads)

    def body(blk_ref, *refs):
        for a in range(n):
            refs[2 * n + a][...] = (refs[a][...].astype(F32) + refs[n + a][...].astype(F32)).astype(BF16)

    mine = [pl.BlockSpec((None,) + g.shape[1:], lambda j, b: (b[j], 0, 0)) for g in grads]
    same = [pl.BlockSpec((None,) + g.shape[1:], lambda j, b: (j, 0, 0)) for g in grads]
    return pl.pallas_call(
        body,
        grid_spec=pltpu.PrefetchScalarGridSpec(
            num_scalar_prefetch=1, grid=(4,), in_specs=mine + same, out_specs=same),
        out_shape=[jax.ShapeDtypeStruct((4,) + g.shape[1:], BF16) for g in grads],
        compiler_params=_params(("arbitrary",)),
        name=name,
    )(blks, *grads, *from_sib)


def _chip_sum(parts, from_chips, name):
    n = len(parts)

    def body(*refs):
        for a in range(n):
            acc = refs[a][...].astype(F32)
            for j in range(3):
                acc = acc + refs[n + a][j].astype(F32)
            refs[2 * n + a][...] = acc

    half = [p.shape[1] // 2 for p in parts]
    return pl.pallas_call(
        body,
        grid=(2,),
        in_specs=([pl.BlockSpec((None, h, p.shape[2]), lambda i: (3, i, 0)) for p, h in zip(parts, half)]
                  + [pl.BlockSpec((3, h, p.shape[2]), lambda i: (0, i, 0)) for p, h in zip(parts, half)]),
        out_specs=[pl.BlockSpec((h, p.shape[2]), lambda i: (i, 0)) for p, h in zip(parts, half)],
        out_shape=[jax.ShapeDtypeStruct(p.shape[1:], F32) for p in parts],
        compiler_params=_params(("arbitrary",)),
        name=name,
    )(*parts, *from_chips)


def _adam_math(w, g, m, v):
    nm = ADAM_B1 * m + (1.0 - ADAM_B1) * g
    nv = ADAM_B2 * v + (1.0 - ADAM_B2) * (g * g)
    m_hat = nm / (1.0 - ADAM_B1 ** ADAM_STEP)
    v_hat = nv / (1.0 - ADAM_B2 ** ADAM_STEP)
    return -ADAM_LR * (m_hat / (jnp.sqrt(v_hat) + ADAM_EPS) + ADAM_WD * w), nm, nv


def _sum_adamw(items, name, carry=None):
    n = len(items)

    def body(*refs):
        for a in range(n):
            p_ref, f_ref, w_ref, m_ref, v_ref = refs[5 * a:5 * a + 5]
            g_ref, d_ref, nm_ref, nv_ref = refs[5 * n + 4 * a:5 * n + 4 * a + 4]
            g = p_ref[...].astype(F32)
            for j in range(3):
                g = g + f_ref[j].astype(F32)
            g_ref[...] = g
            d_ref[...], nm_ref[...], nv_ref[...] = _adam_math(w_ref[...], g, m_ref[...], v_ref[...])

    in_specs, out_specs, out_shape, args = [], [], [], []
    for part, from_chips, w, m, v in items:
        r, c = w.shape
        spec = pl.BlockSpec((r // 2, c), lambda i: (i, 0))
        in_specs += [pl.BlockSpec((None, r // 2, c), lambda i: (3, i, 0)),
                     pl.BlockSpec((3, r // 2, c), lambda i: (0, i, 0)), spec, spec, spec]
        out_specs += [spec] * 4
        out_shape += [jax.ShapeDtypeStruct((r, c), F32)] * 4
        args += [part, from_chips, w, m, v]
    outs, carried = _pcall(body, grid=(2,), in_specs=in_specs, out_specs=out_specs, out_shape=out_shape,
                           scratch_shapes=[], name=name, args=tuple(args), carry=carry)
    return [tuple(outs[4 * a:4 * a + 4]) for a in range(n)], carried


def _small_update(gathered, layout, params, name):
    ng, npar = len(gathered), len(params)

    def body(*refs):
        parts = refs[:ng]
        prm = refs[ng:ng + 3 * npar]
        tot_refs = refs[ng + 3 * npar:2 * ng + 3 * npar]
        out = refs[2 * ng + 3 * npar:]
        tots = []
        for a in range(ng):
            acc = parts[a][0]
            for d in range(1, N_DEV):
                acc = acc + parts[a][d]
            tot_refs[a][...] = acc
            tots.append(acc)
        for i, (a, row, width) in enumerate(layout):
            g = tots[a] if row is None else tots[a][row:row + 1, :width]
            delta, nm, nv = _adam_math(prm[3 * i][...], g, prm[3 * i + 1][...], prm[3 * i + 2][...])
            out[4 * i][...] = g
            out[4 * i + 1][...] = delta
            out[4 * i + 2][...] = nm
            out[4 * i + 3][...] = nv

    flat = [t for prm in params for t in prm]
    res = pl.pallas_call(
        body,
        out_shape=([jax.ShapeDtypeStruct(g.shape[1:], F32) for g in gathered]
                   + [jax.ShapeDtypeStruct(prm[0].shape, F32) for prm in params for _ in range(4)]),
        compiler_params=pltpu.CompilerParams(vmem_limit_bytes=V7X_VMEM_LIMIT),
        name=name,
    )(*gathered, *flat)
    return res[:ng], [tuple(res[ng + 4 * i:ng + 4 * i + 4]) for i in range(npar)]


def _adamw(w, g, m, v, name):
    rows, cols = w.shape
    br = rows
    for cand in (512, 256, 128):
        if rows % cand == 0 and rows > cand:
            br = cand
            break

    def body(w_ref, g_ref, m_ref, v_ref, d_ref, nm_ref, nv_ref):
        d_ref[...], nm_ref[...], nv_ref[...] = _adam_math(w_ref[...], g_ref[...], m_ref[...], v_ref[...])

    spec = pl.BlockSpec((br, cols), lambda i: (i, 0))
    shape = jax.ShapeDtypeStruct((rows, cols), F32)
    return pl.pallas_call(
        body,
        grid=(rows // br,),
        in_specs=[spec] * 4,
        out_specs=[spec] * 3,
        out_shape=[shape] * 3,
        compiler_params=_params(("arbitrary",)),
        name=name,
    )(w, g, m, v)


def _by_device(full):
    return full.reshape(N_DEV, full.shape[0] // N_DEV, full.shape[1])


def kernel(x, p, g_mix, w_in, conv_w, conv_b, ln_g, ln_b, pool_w, pool_scale, w_out, g_ffn, w_gate_up, w_down, g_ple_gate, w_ple_gate, w_ple_up, g_ple_post, g_final, loss_target, m_g_mix, m_w_in, m_conv_w, m_conv_b, m_ln_g, m_ln_b, m_pool_w, m_pool_scale, m_w_out, m_g_ffn, m_w_gate_up, m_w_down, m_g_ple_gate, m_w_ple_gate, m_w_ple_up, m_g_ple_post, m_g_final, v_g_mix, v_w_in, v_conv_w, v_conv_b, v_ln_g, v_ln_b, v_pool_w, v_pool_scale, v_w_out, v_g_ffn, v_w_gate_up, v_w_down, v_g_ple_gate, v_w_ple_gate, v_w_ple_up, v_g_ple_post, v_g_final):
    seq = x.shape[1]
    xs = x[0]
    ps = p[0, 0]
    tgt = loss_target[0]
    ax, ay, ac = lax.axis_index("x"), lax.axis_index("y"), lax.axis_index("c")
    me = _block(ax, ay, ac)
    blks = jnp.stack([_block(1 - ax, ay, ac), _block(ax, 1 - ay, ac), _block(1 - ax, 1 - ay, ac), me]).astype(jnp.int32)
    rows = lambda gth: gth.reshape((-1,) + gth.shape[2:])

    sh_in, sh_out, sh_gu, sh_down, sh_pg, sh_pu, pool_w_b = _to_bf16([
        w_in[0].T,
        w_out[0],
        w_gate_up[0].T,
        w_down[0],
        w_ple_gate[0],
        w_ple_up[0].T,
        pool_w[0]])
    carry = _Carry()
    _carry_gather(carry, [sh_in, sh_out,
                          jnp.pad(conv_w[0].T, ((0, 0), (0, CONV_HALO - CONV_K)))], 8, 15)
    (h1,), first = _norm_in(xs, g_mix, min(256, seq), carry)
    w_in_t, w_out_f, conv_w_t = [rows(gth) for gth in first]
    conv_w_f = conv_w_t.T

    carry = _Carry()
    _carry_gather(carry, [sh_gu], 8, 12)
    (z, u1, pooled, x1, mix), (w_gu_all,) = _fwd_mix(
        xs, h1, w_in_t, w_out_f, conv_w_f, conv_b, ln_g, ln_b, pool_w_b, pool_scale, min(256, seq), carry)
    w_gu_t = rows(w_gu_all)

    carry = _Carry()
    _carry_gather(carry, [sh_down, sh_pg, sh_pu], 6, 10)
    (h2, g_sav, u_sav, act), late = _ffn_up(x1, w_gu_t, g_ffn, min(256, seq), carry)
    w_down_f, w_pg_f, w_pu_t = [rows(gth) for gth in late]
    dx2, dx2b, hg, ds, dpe, pb, stats_ple = _down_ple(
        x1, act, w_down_f, ps, tgt, w_pg_f, w_pu_t, g_ple_gate, g_ple_post, g_final.reshape(1, D_MODEL),
        min(256, seq))
    dg, du, dx1, dx1b, stats_ffn = _ffn_bwd(dx2, dx2b, x1, g_sav, u_sav, w_gu_t, w_down_f, g_ffn,
                                            min(256, seq))[0]

    d_w_pg, _ = _grad_matmul(hg, ds, 256, "grad_w_ple_gate")
    d_w_pu_t, _ = _grad_matmul(dpe, pb, 256, "grad_w_ple_up")
    d_w_gu_t, _ = _grad_matmul(dg, h2, 256, "grad_w_gate_up", a2=du)
    d_w_out, _ = _grad_matmul(mix, dx1b, 256, "grad_w_out")
    early = [_by_device(d_w_pg), _by_device(d_w_pu_t), _by_device(d_w_gu_t), _by_device(d_w_out)]
    carry = _Carry()
    _carry_pair(carry, early)
    d_w_down, from_sib = _grad_matmul(act, dx2b, 256, "grad_w_down", carry=carry)
    early_parts = _pair_add(early, from_sib, blks, "pair_add_early")

    carry = _Carry()
    _carry_chip(carry, early_parts)
    _carry_pair(carry, [_by_device(d_w_down)])
    (dx, dz, vec_mix, dconv_w_part, dpool_w_part), carried = _bwd_mix(
        dx1, dx1b, xs, z, u1, pooled, w_in_t, w_out_f, g_mix, conv_w_f, ln_g, ln_b, pool_w_b, pool_scale,
        min(256, seq), carry)
    early_chips, down_sib = carried[:4], carried[4:]
    down_parts = _pair_add([_by_device(d_w_down)], down_sib, blks, "pair_add_down")

    carry = _Carry()
    _carry_chip(carry, down_parts)
    d_w_in_t, down_chips = _grad_matmul(dz, h1, 256, "grad_w_in", carry=carry)
    (gr_w_in_t,), small = _reduce_scatter(
        [_by_device(d_w_in_t)],
        [vec_mix, stats_ple, stats_ffn, dconv_w_part, dpool_w_part.reshape(-1, POOL_GROUP)], "scatter_last")
    small = list(small[:4]) + [small[4].reshape((N_DEV,) + dpool_w_part.shape)]

    vec_names = ["g_mix", "ln_g", "ln_b", "conv_b", "pool_scale", "g_final", "g_ple_post", "g_ple_gate", "g_ffn",
                 "pool_w"]
    layout = [(0, 0, D_MODEL), (0, 1, C_CONV), (0, 2, C_CONV), (0, 3, C_CONV), (0, 4, C_POOL),
              (1, 0, D_MODEL), (1, 1, D_MODEL), (1, 2, D_MODEL), (2, 0, D_MODEL), (4, None, None)]
    as_row = lambda t: t.reshape(1, D_MODEL)
    params = [(g_mix, m_g_mix, v_g_mix), (ln_g, m_ln_g, v_ln_g), (ln_b, m_ln_b, v_ln_b),
              (conv_b, m_conv_b, v_conv_b), (pool_scale, m_pool_scale, v_pool_scale),
              (as_row(g_final), as_row(m_g_final), as_row(v_g_final)),
              (g_ple_post, m_g_ple_post, v_g_ple_post), (g_ple_gate, m_g_ple_gate, v_g_ple_gate),
              (g_ffn, m_g_ffn, v_g_ffn), (pool_w[0], m_pool_w[0], v_pool_w[0])]
    tots, small_upd = _small_update(small, layout, params, "small_update")
    loss = tots[1][3, 0]
    upd = {}
    for nm, res, prm in zip(vec_names, small_upd, [g_mix, ln_g, ln_b, conv_b, pool_scale, g_final, g_ple_post,
                                                    g_ple_gate, g_ffn, pool_w]):
        upd[nm] = tuple(t.reshape(prm.shape) for t in res)
    gr_conv_w = lax.dynamic_slice_in_dim(tots[3][:CONV_K], me * (C_CONV // N_DEV), C_CONV // N_DEV, axis=1)

    natural = lambda t: t[None]
    turned = lambda t: t.T[None]
    (res_pg, res_gu, res_out, res_down), _ = _sum_adamw([
        (early_parts[0], early_chips[0], w_ple_gate[0], m_w_ple_gate[0], v_w_ple_gate[0]),
        (early_parts[2], early_chips[2], w_gate_up[0].T, m_w_gate_up[0].T, v_w_gate_up[0].T),
        (early_parts[3], early_chips[3], w_out[0], m_w_out[0], v_w_out[0]),
        (down_parts[0], down_chips[0], w_down[0], m_w_down[0], v_w_down[0])], "adamw_big")
    upd["w_ple_gate"] = tuple(natural(t) for t in res_pg)
    upd["w_gate_up"] = tuple(turned(t) for t in res_gu)
    upd["w_out"] = tuple(natural(t) for t in res_out)
    upd["w_down"] = tuple(natural(t) for t in res_down)
    (gr_w_pu_t,) = _chip_sum(early_parts[1:2], early_chips[1:2], "chip_sum")
    plain = [
        ("w_in", w_in[0].T, gr_w_in_t, m_w_in[0].T, v_w_in[0].T, True),
        ("w_ple_up", w_ple_up[0], gr_w_pu_t.T, m_w_ple_up[0], v_w_ple_up[0], False),
        ("conv_w", conv_w[0], gr_conv_w, m_conv_w[0], v_conv_w[0], False),
    ]
    for nm, w_, g_, m_, v_, transposed in plain:
        res = (g_,) + tuple(_adamw(w_, g_, m_, v_, "adamw_" + nm))
        upd[nm] = tuple((t.T if transposed else t)[None] for t in res)

    order = ["g_mix", "w_in", "conv_w", "conv_b", "ln_g", "ln_b", "pool_w", "pool_scale", "w_out", "g_ffn",
             "w_gate_up", "w_down", "g_ple_gate", "w_ple_gate", "w_ple_up", "g_ple_post", "g_final"]
    outs = [loss, dx[None]]
    for k in range(4):
        outs += [upd[nm][k] for nm in order]
    return tuple(outs)
```

```python
import functools

import jax
import jax.numpy as jnp
from jax import lax
from jax.experimental import pallas as pl
from jax.experimental.pallas import tpu as pltpu

D_MODEL = 1024
C_CONV = 512
C_POOL = 512
Z_WIDTH = 2 * C_CONV + C_POOL
POOL_WINDOWS = (2, 4, 8, 16)
POOL_GROUP = 128
CONV_K = 31
D_FF = 2816
D_PLE = 256
EPS = 1e-6
N_DEV = 8

ADAM_LR = 0.001
ADAM_B1 = 0.9
ADAM_B2 = 0.999
ADAM_EPS = 1e-08
ADAM_WD = 0.01
ADAM_STEP = 10

CONV_HALO = 32
POOL_HALO = 32
ROW_CHUNK = 32
TOKEN_TILE = 256
GRAD_ROWS = 256
V7X_VMEM_LIMIT = 56 * 1024 * 1024
FF_CHUNK = D_FF // 2

BF16 = jnp.bfloat16
F32 = jnp.float32
MESH = pl.DeviceIdType.MESH


def _dot(a, b):
    return lax.dot_general(a, b, (((1,), (0,)), ((), ())), preferred_element_type=F32)


def _dot_nt(a, b):
    return lax.dot_general(a, b, (((1,), (1,)), ((), ())), preferred_element_type=F32)


def _dot_tn(a, b):
    return lax.dot_general(a, b, (((0,), (0,)), ((), ())), preferred_element_type=F32)


def _rms_fwd(x, g):
    r = lax.rsqrt(jnp.mean(x * x, axis=-1, keepdims=True) + EPS)
    return x * r * g, r


def _rms_bwd(x, r, g, dy):
    xr = x * r
    dg = jnp.sum(dy * xr, axis=0, keepdims=True)
    dyg = dy * g
    dx = r * (dyg - xr * jnp.mean(dyg * xr, axis=-1, keepdims=True))
    return dx, dg


def _sigmoid(x):
    return jax.nn.sigmoid(x)


def _params(sem=None, collective_id=None):
    return pltpu.CompilerParams(dimension_semantics=sem, vmem_limit_bytes=V7X_VMEM_LIMIT,
                                collective_id=collective_id)


def _place():
    x, y, c = lax.axis_index("x"), lax.axis_index("y"), lax.axis_index("c")
    chips = [(1 - x, y), (x, 1 - y), (1 - x, 1 - y)]
    return (x, y, c), (x, y, 1 - c), chips


def _block(px, py, pc):
    return 4 * px + 2 * py + pc


EVERY_PEER = ("sib", 0, 1, 2)
PEER_SET_IDS = {("sib",): 0, ("0", "1", "sib"): 1, ("0", "1", "2"): 2, ("0", "1", "2", "sib"): 3}


def _collective_id(peers):
    return PEER_SET_IDS[tuple(sorted(str(p) for p in peers))]


def _handshake(peers):
    me, sib, chips = _place()
    barrier = pltpu.get_barrier_semaphore()
    for p in peers:
        to = sib if p == "sib" else (*chips[p], me[2])
        pl.semaphore_signal(barrier, inc=1, device_id=to, device_id_type=MESH)
    pl.semaphore_wait(barrier, len(peers))


class _Carry:
    def __init__(self):
        self.inputs = []
        self.out_shapes = []
        self.copies = []
        self.locals = []

    def add_input(self, arr):
        self.inputs.append(arr)
        return len(self.inputs) - 1

    def add_output(self, shape, dtype):
        self.out_shapes.append(jax.ShapeDtypeStruct(shape, dtype))
        return len(self.out_shapes) - 1

    def local(self, src_idx, dst_idx, dst_blk):
        self.locals.append((src_idx, dst_idx, dst_blk))

    def copy(self, src, dst_idx, dst_blk, got_blk, peer, step=0, after=()):
        self.copies.append(dict(src=src, dst_idx=dst_idx, dst_blk=dst_blk, got_blk=got_blk, peer=peer, step=step,
                                after=tuple(after)))
        return len(self.copies) - 1

    def sem_shapes(self):
        return [pltpu.SemaphoreType.DMA((max(1, len(self.copies)),)),
                pltpu.SemaphoreType.DMA((max(1, len(self.copies)),)),
                pltpu.SemaphoreType.DMA((max(1, len(self.locals)),))]

    @staticmethod
    def _view(ref, where):
        if isinstance(where, tuple):
            blk, row0, nrows = where
            return ref.at[blk, pl.ds(row0, nrows)]
        return ref.at[where]

    def _desc(self, k, ins, outs, sems, place):
        cp = self.copies[k]
        me, sib, chips = place
        kind, idx, blk = cp["src"]
        src = (ins if kind == "in" else outs)[idx]
        if blk is not None:
            src = self._view(src, blk(*place))
        to = sib if cp["peer"] == "sib" else (*chips[cp["peer"]], me[2])
        return pltpu.make_async_remote_copy(
            src_ref=src, dst_ref=self._view(outs[cp["dst_idx"]], cp["dst_blk"](*place)),
            send_sem=sems[0].at[k], recv_sem=sems[1].at[k], device_id=to, device_id_type=MESH)

    def _arrival(self, k, outs, sems, place):
        cp = self.copies[k]
        got = self._view(outs[cp["dst_idx"]], cp["got_blk"](*place))
        return pltpu.make_async_remote_copy(
            src_ref=got, dst_ref=got, send_sem=sems[0].at[k], recv_sem=sems[1].at[k],
            device_id=place[0], device_id_type=MESH)

    def _local(self, n, ins, outs, sems, place):
        src_idx, dst_idx, blk = self.locals[n]
        return pltpu.make_async_copy(ins[src_idx], outs[dst_idx].at[blk(*place)], sems[2].at[n])

    def stages(self):
        return sorted({0} | {cp["step"] for cp in self.copies})

    def peers(self):
        return sorted({cp["peer"] for cp in self.copies}, key=str)

    def stage(self, s, ins, outs, sems, handshake=True):
        place = _place()
        if s == 0:
            if handshake:
                _handshake(self.peers())
            self._waited = set()
            for n in range(len(self.locals)):
                self._local(n, ins, outs, sems, place).start()
        for k, cp in enumerate(self.copies):
            if cp["step"] != s:
                continue
            for a in cp["after"]:
                if a not in self._waited:
                    self._arrival(a, outs, sems, place).wait_recv()
                    self._waited.add(a)
            self._desc(k, ins, outs, sems, place).start()

    def drain(self, ins, outs, sems):
        place = _place()
        for k in range(len(self.copies)):
            if k not in self._waited:
                self._arrival(k, outs, sems, place).wait_recv()
        for k in range(len(self.copies)):
            self._desc(k, ins, outs, sems, place).wait_send()
        for n in range(len(self.locals)):
            self._local(n, ins, outs, sems, place).wait()

    def starts(self, step, nsteps, ins, outs, sems):
        for s in self.stages():
            pl.when(step == min(s, nsteps - 1))(functools.partial(self.stage, s, ins, outs, sems))

    def finish(self, step, nsteps, ins, outs, sems):
        pl.when(step == nsteps - 1)(functools.partial(self.drain, ins, outs, sems))


def _const_blk(j):
    return lambda me, sib, chips: j


def _carry_gather(carry, shards, relay_step, last_step):
    outs = []
    for sh in shards:
        i = carry.add_input(sh)
        o = carry.add_output((N_DEV,) + sh.shape, sh.dtype)
        half = sh.shape[0] // 2
        tile = 16 if sh.dtype == BF16 else 8
        split = half % tile == 0
        rows = [(0, half), (half, sh.shape[0] - half)] if split else [(0, sh.shape[0]), None]

        def whole(j, core):
            return lambda me, sib, chips, j=j, core=core: _block(*chips[j], me[2] if core == 0 else 1 - me[2])

        def part(j, core, h, rows=rows):
            return lambda me, sib, chips: (_block(*chips[j], me[2] if core == 0 else 1 - me[2]),) + rows[h]

        mine = lambda me, sib, chips: _block(*me)
        carry.local(i, o, mine)
        carry.copy(("in", i, None), o, mine, lambda me, sib, chips: _block(*sib), "sib")
        near = [carry.copy(("in", i, None), o, mine, whole(j, 0), j) for j in range(2)]
        for j in range(2):
            carry.copy(("out", o, whole(j, 0)), o, whole(j, 0), whole(j, 1), "sib", step=relay_step, after=(near[j],))
        for j in range(2):
            if rows[j] is None:
                continue
            far = carry.copy(("out", o, part(j, 0, j)), o, part(j, 0, j), part(2, 0, j), 1 - j,
                             step=relay_step, after=(near[j],))
            carry.copy(("out", o, part(2, 0, j)), o, part(2, 0, j), part(2, 1, j), "sib", step=last_step, after=(far,))
        outs.append(o)
    return outs


def _carry_pair(carry, grads):
    outs = []
    for g in grads:
        i = carry.add_input(g)
        o = carry.add_output((4,) + g.shape[1:], g.dtype)
        for j in range(4):
            if j < 3:
                blk = lambda me, sib, chips, j=j: _block(*chips[j], 1 - me[2])
            else:
                blk = lambda me, sib, chips: _block(*sib)
            carry.copy(("in", i, blk), o, _const_blk(j), _const_blk(j), "sib")
        outs.append(o)
    return outs


def _carry_chip(carry, parts):
    outs = []
    for p in parts:
        i = carry.add_input(p)
        o = carry.add_output((3,) + p.shape[1:], p.dtype)
        for j in range(3):
            carry.copy(("in", i, _const_blk(j)), o, _const_blk(j), _const_blk(j), j)
        outs.append(o)
    return outs


def _pcall(body, *, grid, in_specs, out_specs, out_shape, scratch_shapes, name, args, carry=None):
    sem = ("arbitrary",) * len(grid)
    if carry is None:
        res = pl.pallas_call(body, grid=grid, in_specs=in_specs, out_specs=out_specs, out_shape=out_shape,
                             scratch_shapes=scratch_shapes, compiler_params=_params(sem), name=name)(*args)
        return list(res), []
    n_in, n_out, n_scr = len(in_specs), len(out_specs), len(scratch_shapes)
    c_in, c_out = len(carry.inputs), len(carry.out_shapes)
    nsteps = 1
    for extent in grid:
        nsteps *= extent

    def wrapped(*refs):
        ins = refs[:n_in]
        cins = refs[n_in:n_in + c_in]
        o0 = n_in + c_in
        outs = refs[o0:o0 + n_out]
        couts = refs[o0 + n_out:o0 + n_out + c_out]
        s0 = o0 + n_out + c_out
        scr = refs[s0:s0 + n_scr]
        sems = refs[s0 + n_scr:]
        step = pl.program_id(0)
        for d in range(1, len(grid)):
            step = step * grid[d] + pl.program_id(d)
        carry.starts(step, nsteps, cins, couts, sems)
        body(*ins, *outs, *scr)
        carry.finish(step, nsteps, cins, couts, sems)

    any_spec = pl.BlockSpec(memory_space=pl.ANY)
    res = pl.pallas_call(
        wrapped, grid=grid,
        in_specs=list(in_specs) + [any_spec] * c_in,
        out_specs=list(out_specs) + [any_spec] * c_out,
        out_shape=list(out_shape) + carry.out_shapes,
        scratch_shapes=list(scratch_shapes) + carry.sem_shapes(),
        compiler_params=_params(sem, _collective_id(carry.peers())), name=name)(*args, *carry.inputs)
    return list(res[:n_out]), list(res[n_out:])


def _shifted_copies(buf, shifted, tm):
    span = tm + CONV_HALO - 8
    for r in range(1, 8):
        shifted[r - 1, 0:span, :] = buf[r:r + span, :]


def _rows_at(buf, shifted, start):
    aligned, r = (start // 8) * 8, start % 8
    if r == 0:
        return buf[aligned:aligned + ROW_CHUNK, :]
    return shifted[r - 1, aligned:aligned + ROW_CHUNK, :]


def _window_sums(buf, cols, work, levels, tm, trailing):
    src, src_cols = buf, cols
    for k in range(levels + 1):
        shift = 1 << k
        dst = work.at[k % 2]
        if trailing:
            lo = 8 * (k + 1)
            dst[lo:tm + POOL_HALO, :] = (src[lo:tm + POOL_HALO, src_cols]
                                         + src[lo - shift:tm + POOL_HALO - shift, src_cols])
        else:
            hi = tm + POOL_HALO - 8 * (k + 1)
            dst[0:hi, :] = src[0:hi, src_cols] + src[shift:hi + shift, src_cols]
        src, src_cols = dst, slice(0, POOL_GROUP)
    return src[POOL_HALO:POOL_HALO + tm, src_cols] if trailing else src[0:tm, src_cols]


def _row_spec(tm, width):
    return pl.BlockSpec((tm, width), lambda i: (i, 0))


def _const_spec(shape):
    return pl.BlockSpec(shape, lambda i: (0,) * len(shape))


def _weight_spec(shape):
    return pl.BlockSpec(shape, lambda i: (0,) * len(shape), pipeline_mode=pl.Buffered(1))


def _to_bf16(arrays):
    n = len(arrays)

    def body(*refs):
        for a in range(n):
            refs[n + a][...] = refs[a][...].astype(BF16)

    whole = [pl.BlockSpec(arr.shape, lambda i, nd=arr.ndim: (0,) * nd) for arr in arrays]
    return pl.pallas_call(
        body,
        grid=(1,),
        in_specs=whole,
        out_specs=whole,
        out_shape=[jax.ShapeDtypeStruct(arr.shape, BF16) for arr in arrays],
        compiler_params=_params(("arbitrary",)),
        name="to_bf16",
    )(*arrays)


def _norm_in(x, g_mix, tm, carry=None):
    s = x.shape[0]

    def body(x_ref, gmix_ref, h1_ref):
        h, _ = _rms_fwd(x_ref[...], gmix_ref[...])
        h1_ref[...] = h.astype(BF16)

    return _pcall(
        body,
        grid=(s // tm,),
        in_specs=[_row_spec(tm, D_MODEL), _const_spec((1, D_MODEL))],
        out_specs=[_row_spec(tm, D_MODEL)],
        out_shape=[jax.ShapeDtypeStruct((s, D_MODEL), BF16)],
        scratch_shapes=[],
        name="norm_in",
        args=(x, g_mix),
        carry=carry,
    )


def _fwd_mix(x, h1, w_in_t, w_out, conv_w, conv_b, ln_g, ln_b, pool_w, pool_scale, tm, carry=None):
    s = x.shape[0]
    nt = s // tm

    def body(x_ref, h1_ref, win_ref, wout_ref, cw_ref, cb_ref, lng_ref, lnb_ref, pw_ref, ps_ref,
             z_ref, u1_ref, pooled_ref, x1_ref, mix_ref, ubuf, vbuf, ush, pbuf):
        i = pl.program_id(0)

        @pl.when(i == 0)
        def _():
            ubuf[0:CONV_HALO, :] = jnp.zeros((CONV_HALO, C_CONV), F32)
            vbuf[0:POOL_HALO, :] = jnp.zeros((POOL_HALO, C_POOL), F32)

        xt = x_ref[...]
        z = _dot_nt(h1_ref[...], win_ref[...])
        z_ref[...] = z
        a = z[:, :C_CONV]
        b = z[:, C_CONV:2 * C_CONV]
        v = z[:, 2 * C_CONV:]
        ubuf[CONV_HALO:CONV_HALO + tm, :] = a * _sigmoid(b)
        vbuf[POOL_HALO:POOL_HALO + tm, :] = v

        _shifted_copies(ubuf, ush, tm)
        for rc in range(tm // ROW_CHUNK):
            base = rc * ROW_CHUNK + CONV_HALO - (CONV_K - 1)
            acc = jnp.broadcast_to(cb_ref[...], (ROW_CHUNK, C_CONV))
            for k in range(CONV_K):
                acc = acc + cw_ref[k:k + 1, :] * _rows_at(ubuf, ush, base + k)
            u1_ref[rc * ROW_CHUNK:(rc + 1) * ROW_CHUNK, :] = acc

        u1 = u1_ref[...]
        mu = jnp.mean(u1, axis=-1, keepdims=True)
        cen = u1 - mu
        rstd = lax.rsqrt(jnp.mean(cen * cen, axis=-1, keepdims=True) + EPS)
        u2 = cen * rstd * lng_ref[...] + lnb_ref[...]
        u = u2 * _sigmoid(u2)

        pos1 = (i * tm + lax.broadcasted_iota(jnp.int32, (tm, 1), 0) + 1).astype(F32)
        parts = [u]
        for g, w in enumerate(POOL_WINDOWS):
            cols = slice(g * POOL_GROUP, (g + 1) * POOL_GROUP)
            vg = v[:, cols]
            tot = _window_sums(vbuf, cols, pbuf, g, tm, trailing=True)
            pooled = tot / jnp.minimum(pos1, float(w)) - vg
            pooled_b = pooled.astype(BF16)
            pooled_ref[:, cols] = pooled_b
            parts.append(_dot(pooled_b, pw_ref[g]) * ps_ref[:, cols])
        mix = jnp.concatenate(parts, axis=-1).astype(BF16)
        mix_ref[...] = mix
        x1_ref[...] = xt + _dot(mix, wout_ref[...])

        ubuf[0:CONV_HALO, :] = ubuf[tm:tm + CONV_HALO, :]
        vbuf[0:POOL_HALO, :] = vbuf[tm:tm + POOL_HALO, :]

    return _pcall(
        body,
        grid=(nt,),
        in_specs=[
            _row_spec(tm, D_MODEL),
            _row_spec(tm, D_MODEL),
            _const_spec((Z_WIDTH, D_MODEL)),
            _const_spec((D_MODEL, D_MODEL)),
            _const_spec((CONV_HALO, C_CONV)),
            _const_spec((1, C_CONV)),
            _const_spec((1, C_CONV)),
            _const_spec((1, C_CONV)),
            _const_spec((len(POOL_WINDOWS), POOL_GROUP, POOL_GROUP)),
            _const_spec((1, C_POOL)),
        ],
        out_specs=[
            _row_spec(tm, Z_WIDTH),
            _row_spec(tm, C_CONV),
            _row_spec(tm, C_POOL),
            _row_spec(tm, D_MODEL),
            _row_spec(tm, D_MODEL),
        ],
        out_shape=[
            jax.ShapeDtypeStruct((s, Z_WIDTH), F32),
            jax.ShapeDtypeStruct((s, C_CONV), F32),
            jax.ShapeDtypeStruct((s, C_POOL), BF16),
            jax.ShapeDtypeStruct((s, D_MODEL), F32),
            jax.ShapeDtypeStruct((s, D_MODEL), BF16),
        ],
        scratch_shapes=[
            pltpu.VMEM((tm + CONV_HALO, C_CONV), F32),
            pltpu.VMEM((tm + POOL_HALO, C_POOL), F32),
            pltpu.VMEM((7, tm + CONV_HALO, C_CONV), F32),
            pltpu.VMEM((2, tm + POOL_HALO, POOL_GROUP), F32),
        ],
        name="fwd_mix",
        args=(x, h1, w_in_t, w_out, conv_w, conv_b, ln_g, ln_b, pool_w, pool_scale),
        carry=carry,
    )


def _ffn_up(x1, w_gu_t, g_ffn, tm, carry=None):
    s = x1.shape[0]

    def body(x1_ref, w_ref, gffn_ref, h2_ref, g_ref, u_ref, act_ref):
        h, _ = _rms_fwd(x1_ref[...], gffn_ref[...])
        h2 = h.astype(BF16)
        h2_ref[...] = h2
        for c in range(D_FF // FF_CHUNK):
            cols = slice(c * FF_CHUNK, (c + 1) * FF_CHUNK)
            g = _dot_nt(h2, w_ref[c * FF_CHUNK:(c + 1) * FF_CHUNK, :])
            u = _dot_nt(h2, w_ref[D_FF + c * FF_CHUNK:D_FF + (c + 1) * FF_CHUNK, :])
            g_ref[:, cols] = g.astype(BF16)
            u_ref[:, cols] = u.astype(BF16)
            act_ref[:, cols] = (g * _sigmoid(g) * u).astype(BF16)

    return _pcall(
        body,
        grid=(s // tm,),
        in_specs=[_row_spec(tm, D_MODEL), _weight_spec((2 * D_FF, D_MODEL)), _const_spec((1, D_MODEL))],
        out_specs=[_row_spec(tm, D_MODEL), _row_spec(tm, D_FF), _row_spec(tm, D_FF), _row_spec(tm, D_FF)],
        out_shape=[
            jax.ShapeDtypeStruct((s, D_MODEL), BF16),
            jax.ShapeDtypeStruct((s, D_FF), BF16),
            jax.ShapeDtypeStruct((s, D_FF), BF16),
            jax.ShapeDtypeStruct((s, D_FF), BF16),
        ],
        scratch_shapes=[],
        name="ffn_up",
        args=(x1, w_gu_t, g_ffn),
        carry=carry,
    )


def _down_ple(x1, act, w_down, p, tgt, w_pg, w_pu_t, g_gate, g_post, g_final, tm):
    s = x1.shape[0]
    nt = s // tm

    def body(x1_ref, act_ref, wd_ref, p_ref, t_ref, wpg_ref, wpu_ref, gg_ref, gp_ref, gf_ref,
             dx2_ref, dx2b_ref, hg_ref, ds_ref, dpe_ref, pb_ref, stats_ref, x2_cur, x2_next):
        i = pl.program_id(0)

        @pl.when(i == 0)
        def _():
            stats_ref[...] = jnp.zeros_like(stats_ref)
            x2_cur[...] = jnp.zeros((tm, D_MODEL), F32)

        def down(c):
            cols = slice(c * 256, (c + 1) * 256)
            x2_next[:, cols] = x1_ref[:, cols] + _dot(act_ref[...], wd_ref[:, cols])

        x2 = x2_cur[...]
        counts = i >= 1

        hg, rg = _rms_fwd(x2, gg_ref[...])
        hg_b = hg.astype(BF16)
        hg_ref[...] = hg_b
        down(0)
        gate = _sigmoid(_dot(hg_b, wpg_ref[...]))
        pb = p_ref[...].astype(BF16)
        pb_ref[...] = pb
        pe = _dot_nt(pb, wpu_ref[...])
        e, rp = _rms_fwd(pe, gp_ref[...])
        x3 = x2 + gate * e
        down(1)
        y, r3 = _rms_fwd(x3, gf_ref[...])
        diff = y - t_ref[...]
        loss = 0.5 * jnp.sum(jnp.sum(diff * diff, axis=-1, keepdims=True), axis=0, keepdims=True) / D_MODEL
        dy = diff * (1.0 / D_MODEL)

        dx3, dgf = _rms_bwd(x3, r3, gf_ref[...], dy)
        down(2)
        dpe, dgp = _rms_bwd(pe, rp, gp_ref[...], dx3 * gate)
        dpe_ref[...] = dpe.astype(BF16)
        ds = (dx3 * e * gate * (1.0 - gate)).astype(BF16)
        ds_ref[...] = ds
        dhg = _dot_nt(ds, wpg_ref[...])
        down(3)
        dxg, dgg = _rms_bwd(x2, rg, gg_ref[...], dhg)
        dx2 = dx3 + dxg
        dx2_ref[...] = dx2
        dx2b_ref[...] = dx2.astype(BF16)
        x2_cur[...] = x2_next[...]

        stats_ref[0:1, :] += jnp.where(counts, dgf, 0.0)
        stats_ref[1:2, :] += jnp.where(counts, dgp, 0.0)
        stats_ref[2:3, :] += jnp.where(counts, dgg, 0.0)
        stats_ref[3:4, :] += jnp.where(counts, jnp.broadcast_to(loss, (1, D_MODEL)), 0.0)

    ahead = lambda width: pl.BlockSpec((tm, width), lambda i: (jnp.minimum(i, nt - 1), 0))
    behind = lambda width: pl.BlockSpec((tm, width), lambda i: (jnp.maximum(i - 1, 0), 0))
    return pl.pallas_call(
        body,
        grid=(nt + 1,),
        in_specs=[
            ahead(D_MODEL),
            ahead(D_FF),
            _weight_spec((D_FF, D_MODEL)),
            behind(D_PLE),
            behind(D_MODEL),
            _weight_spec((D_MODEL, D_MODEL)),
            _weight_spec((D_MODEL, D_PLE)),
            _const_spec((1, D_MODEL)),
            _const_spec((1, D_MODEL)),
            _const_spec((1, D_MODEL)),
        ],
        out_specs=[
            behind(D_MODEL),
            behind(D_MODEL),
            behind(D_MODEL),
            behind(D_MODEL),
            behind(D_MODEL),
            behind(D_PLE),
            _const_spec((8, D_MODEL)),
        ],
        out_shape=[
            jax.ShapeDtypeStruct((s, D_MODEL), F32),
            jax.ShapeDtypeStruct((s, D_MODEL), BF16),
            jax.ShapeDtypeStruct((s, D_MODEL), BF16),
            jax.ShapeDtypeStruct((s, D_MODEL), BF16),
            jax.ShapeDtypeStruct((s, D_MODEL), BF16),
            jax.ShapeDtypeStruct((s, D_PLE), BF16),
            jax.ShapeDtypeStruct((8, D_MODEL), F32),
        ],
        scratch_shapes=[pltpu.VMEM((tm, D_MODEL), F32), pltpu.VMEM((tm, D_MODEL), F32)],
        compiler_params=_params(("arbitrary",)),
        name="down_ple",
    )(x1, act, w_down, p, tgt, w_pg, w_pu_t, g_gate, g_post, g_final)


def _ffn_bwd(dx2, dx2b, x1, g_sav, u_sav, w_gu_t, w_down, g_ffn, tm, carry=None):
    s = x1.shape[0]

    def body(dx2_ref, dx2b_ref, x1_ref, g_ref, u_ref, w_ref, wd_ref, gffn_ref,
             dg_ref, du_ref, dx1_ref, dx1b_ref, stats_ref):
        @pl.when(pl.program_id(0) == 0)
        def _():
            stats_ref[...] = jnp.zeros_like(stats_ref)

        dx2b = dx2b_ref[...]
        nc = D_FF // FF_CHUNK
        dacts = [_dot_nt(dx2b, wd_ref[c * FF_CHUNK:(c + 1) * FF_CHUNK, :]) for c in range(nc)]
        dh2 = jnp.zeros((tm, D_MODEL), F32)
        for c in range(nc):
            cols = slice(c * FF_CHUNK, (c + 1) * FF_CHUNK)
            g = g_ref[:, cols].astype(F32)
            u = u_ref[:, cols].astype(F32)
            sg = _sigmoid(g)
            dg = (dacts[c] * u * sg * (1.0 + g * (1.0 - sg))).astype(BF16)
            du = (dacts[c] * g * sg).astype(BF16)
            dg_ref[:, cols] = dg
            du_ref[:, cols] = du
            dh2 = dh2 + _dot(dg, w_ref[c * FF_CHUNK:(c + 1) * FF_CHUNK, :])
            dh2 = dh2 + _dot(du, w_ref[D_FF + c * FF_CHUNK:D_FF + (c + 1) * FF_CHUNK, :])

        x1 = x1_ref[...]
        r2 = lax.rsqrt(jnp.mean(x1 * x1, axis=-1, keepdims=True) + EPS)
        dxn, dgf = _rms_bwd(x1, r2, gffn_ref[...], dh2)
        dx1 = dx2_ref[...] + dxn
        dx1_ref[...] = dx1
        dx1b_ref[...] = dx1.astype(BF16)
        stats_ref[0:1, :] += dgf

    return _pcall(
        body,
        grid=(s // tm,),
        in_specs=[
            _row_spec(tm, D_MODEL), _row_spec(tm, D_MODEL), _row_spec(tm, D_MODEL),
            _row_spec(tm, D_FF), _row_spec(tm, D_FF),
            _weight_spec((2 * D_FF, D_MODEL)), _weight_spec((D_FF, D_MODEL)), _const_spec((1, D_MODEL)),
        ],
        out_specs=[_row_spec(tm, D_FF), _row_spec(tm, D_FF), _row_spec(tm, D_MODEL), _row_spec(tm, D_MODEL),
                   _const_spec((8, D_MODEL))],
        out_shape=[
            jax.ShapeDtypeStruct((s, D_FF), BF16),
            jax.ShapeDtypeStruct((s, D_FF), BF16),
            jax.ShapeDtypeStruct((s, D_MODEL), F32),
            jax.ShapeDtypeStruct((s, D_MODEL), BF16),
            jax.ShapeDtypeStruct((8, D_MODEL), F32),
        ],
        scratch_shapes=[],
        name="ffn_bwd",
        args=(dx2, dx2b, x1, g_sav, u_sav, w_gu_t, w_down, g_ffn),
        carry=carry,
    )


def _bwd_mix(dx1, dx1b, x, z, u1, pooled, w_in_t, w_out, g_mix, conv_w, ln_g, ln_b, pool_w, pool_scale, tm,
             carry=None):
    s = x.shape[0]
    nt = s // tm

    def body(dx1_ref, dx1b_ref, x_ref, z_ref, u1_ref, pooled_ref, win_ref, wout_ref, gmix_ref, cw_ref,
             lng_ref, lnb_ref, pw_ref, ps_ref,
             dx_ref, dz_ref, vec_ref, dcw_ref, dpw_ref, dubuf, dvbuf, u0buf, du0buf, dush, pbuf):
        i = pl.program_id(0)
        tile = nt - 1 - i

        @pl.when(i == 0)
        def _():
            vec_ref[...] = jnp.zeros_like(vec_ref)
            dcw_ref[...] = jnp.zeros_like(dcw_ref)
            dpw_ref[...] = jnp.zeros_like(dpw_ref)
            dubuf[tm:tm + CONV_HALO, :] = jnp.zeros((CONV_HALO, C_CONV), F32)
            dvbuf[tm:tm + POOL_HALO, :] = jnp.zeros((POOL_HALO, C_POOL), F32)

        dmix = _dot_nt(dx1b_ref[...], wout_ref[...])
        du = dmix[:, :C_CONV]
        dq = dmix[:, C_CONV:]

        pos1 = (tile * tm + lax.broadcasted_iota(jnp.int32, (tm, 1), 0) + 1).astype(F32)
        dpooled_parts = []
        dps_rows = []
        for g, w in enumerate(POOL_WINDOWS):
            cols = slice(g * POOL_GROUP, (g + 1) * POOL_GROUP)
            pooled_b = pooled_ref[:, cols]
            mixed = _dot(pooled_b, pw_ref[g])
            dqg = dq[:, cols]
            dps_rows.append(jnp.sum(dqg * mixed, axis=0, keepdims=True))
            dmixed = (dqg * ps_ref[:, cols]).astype(BF16)
            dpw_ref[g] += _dot_tn(pooled_b, dmixed)
            dpooled = _dot_nt(dmixed, pw_ref[g])
            dpooled_parts.append(dpooled)
            dvbuf[0:tm, cols] = dpooled / jnp.minimum(pos1, float(w))
        vec_ref[4:5, 0:C_POOL] += jnp.concatenate(dps_rows, axis=-1)
        dv_parts = []
        for g, w in enumerate(POOL_WINDOWS):
            cols = slice(g * POOL_GROUP, (g + 1) * POOL_GROUP)
            tot = _window_sums(dvbuf, cols, pbuf, g, tm, trailing=False)
            dv_parts.append(tot - dpooled_parts[g])

        u1 = u1_ref[...]
        mu = jnp.mean(u1, axis=-1, keepdims=True)
        cen = u1 - mu
        rstd = lax.rsqrt(jnp.mean(cen * cen, axis=-1, keepdims=True) + EPS)
        xhat = cen * rstd
        u2 = xhat * lng_ref[...] + lnb_ref[...]
        sg2 = _sigmoid(u2)
        du2 = du * sg2 * (1.0 + u2 * (1.0 - sg2))
        vec_ref[1:2, 0:C_CONV] += jnp.sum(du2 * xhat, axis=0, keepdims=True)
        vec_ref[2:3, 0:C_CONV] += jnp.sum(du2, axis=0, keepdims=True)
        t1 = du2 * lng_ref[...]
        du1 = rstd * (t1 - jnp.mean(t1, axis=-1, keepdims=True)
                      - xhat * jnp.mean(t1 * xhat, axis=-1, keepdims=True))
        vec_ref[3:4, 0:C_CONV] += jnp.sum(du1, axis=0, keepdims=True)
        dubuf[0:tm, :] = du1

        zt = z_ref[...]
        a = zt[:, :C_CONV]
        sgb = _sigmoid(zt[:, C_CONV:2 * C_CONV])
        u0buf[...] = a * sgb

        _shifted_copies(dubuf, dush, tm)
        for rc in range(tm // ROW_CHUNK):
            r0 = rc * ROW_CHUNK
            acc = jnp.zeros((ROW_CHUNK, C_CONV), F32)
            for k in range(CONV_K):
                acc = acc + cw_ref[k:k + 1, :] * _rows_at(dubuf, dush, r0 + (CONV_K - 1) - k)
            du0buf[r0:r0 + ROW_CHUNK, :] = acc
        for k in range(CONV_K):
            acc = jnp.zeros((ROW_CHUNK, C_CONV), F32)
            for rc in range(tm // ROW_CHUNK):
                r0 = rc * ROW_CHUNK
                acc = acc + u0buf[r0:r0 + ROW_CHUNK, :] * _rows_at(dubuf, dush, r0 + (CONV_K - 1) - k)
            dcw_ref[k:k + 1, :] += jnp.sum(acc, axis=0, keepdims=True)
        du0 = du0buf[...]

        da = du0 * sgb
        db = du0 * a * sgb * (1.0 - sgb)
        dz = jnp.concatenate([da, db] + dv_parts, axis=-1).astype(BF16)
        dz_ref[...] = dz

        xt = x_ref[...]
        r1 = lax.rsqrt(jnp.mean(xt * xt, axis=-1, keepdims=True) + EPS)
        dh1 = _dot(dz, win_ref[...])
        dxn, dgm = _rms_bwd(xt, r1, gmix_ref[...], dh1)
        dx_ref[...] = dx1_ref[...] + dxn
        vec_ref[0:1, :] += dgm

        dubuf[tm:tm + CONV_HALO, :] = dubuf[0:CONV_HALO, :]
        dvbuf[tm:tm + POOL_HALO, :] = dvbuf[0:POOL_HALO, :]

    rev = lambda width: pl.BlockSpec((tm, width), lambda i: (nt - 1 - i, 0))
    return _pcall(
        body,
        grid=(nt,),
        in_specs=[
            rev(D_MODEL), rev(D_MODEL), rev(D_MODEL), rev(Z_WIDTH), rev(C_CONV), rev(C_POOL),
            _const_spec((Z_WIDTH, D_MODEL)),
            _const_spec((D_MODEL, D_MODEL)),
            _const_spec((1, D_MODEL)),
            _const_spec((CONV_HALO, C_CONV)),
            _const_spec((1, C_CONV)),
            _const_spec((1, C_CONV)),
            _const_spec((len(POOL_WINDOWS), POOL_GROUP, POOL_GROUP)),
            _const_spec((1, C_POOL)),
        ],
        out_specs=[
            rev(D_MODEL), rev(Z_WIDTH),
            _const_spec((8, D_MODEL)),
            _const_spec((CONV_HALO, C_CONV)),
            _const_spec((len(POOL_WINDOWS), POOL_GROUP, POOL_GROUP)),
        ],
        out_shape=[
            jax.ShapeDtypeStruct((s, D_MODEL), F32),
            jax.ShapeDtypeStruct((s, Z_WIDTH), BF16),
            jax.ShapeDtypeStruct((8, D_MODEL), F32),
            jax.ShapeDtypeStruct((CONV_HALO, C_CONV), F32),
            jax.ShapeDtypeStruct((len(POOL_WINDOWS), POOL_GROUP, POOL_GROUP), F32),
        ],
        scratch_shapes=[
            pltpu.VMEM((tm + CONV_HALO, C_CONV), F32),
            pltpu.VMEM((tm + POOL_HALO, C_POOL), F32),
            pltpu.VMEM((tm, C_CONV), F32),
            pltpu.VMEM((tm, C_CONV), F32),
            pltpu.VMEM((7, tm + CONV_HALO, C_CONV), F32),
            pltpu.VMEM((2, tm + POOL_HALO, POOL_GROUP), F32),
        ],
        name="bwd_mix",
        args=(dx1, dx1b, x, z, u1, pooled, w_in_t, w_out, g_mix, conv_w, ln_g, ln_b, pool_w, pool_scale),
        carry=carry,
    )


def _grad_matmul(a, b, bm, name, a2=None, carry=None):
    s, ma = a.shape
    nb = b.shape[1]
    na = ma // bm
    if a2 is None:
        def body(a_ref, b_ref, o_ref):
            o_ref[...] = _dot_tn(a_ref[...], b_ref[...]).astype(BF16)

        lhs_specs = [pl.BlockSpec((s, bm), lambda i: (0, i))]
        lhs = (a,)
        steps = na
    else:
        def body(a_ref, a2_ref, b_ref, o_ref):
            i = pl.program_id(0)

            @pl.when(i < na)
            def _():
                o_ref[...] = _dot_tn(a_ref[...], b_ref[...]).astype(BF16)

            @pl.when(i >= na)
            def _():
                o_ref[...] = _dot_tn(a2_ref[...], b_ref[...]).astype(BF16)

        lhs_specs = [pl.BlockSpec((s, bm), lambda i: (0, jnp.minimum(i, na - 1))),
                     pl.BlockSpec((s, bm), lambda i: (0, jnp.maximum(i - na, 0)))]
        lhs = (a, a2)
        steps = 2 * na

    outs, carried = _pcall(
        body,
        grid=(steps,),
        in_specs=lhs_specs + [pl.BlockSpec((s, nb), lambda i: (0, 0))],
        out_specs=[pl.BlockSpec((bm, nb), lambda i: (i, 0))],
        out_shape=[jax.ShapeDtypeStruct((steps * bm, nb), BF16)],
        scratch_shapes=[],
        name=name,
        args=lhs + (b,),
        carry=carry,
    )
    return outs[0], carried


def _reduce_scatter(grads, small, name):
    n, ns = len(grads), len(small)
    shapes = [g.shape[1:] for g in grads]
    gather = _Carry()
    _carry_gather(gather, small, 1, 2)

    def body(*refs):
        g = refs[:n]
        out = refs[n + ns:2 * n + ns]
        scr = refs[2 * (n + ns):]
        own, loc, r1, r2 = scr[:n], scr[n:2 * n], scr[2 * n:3 * n], scr[3 * n:4 * n]
        load_sems, s1, q1, s2, q2 = scr[4 * n:4 * n + 5]
        gather_refs = (refs[n:n + ns], refs[2 * n + ns:2 * (n + ns)], scr[4 * n + 5:])
        me, sib, chips = _place()
        c = me[2]

        _handshake(EVERY_PEER)
        gather.stage(0, *gather_refs, handshake=False)
        loads = []
        sends = []
        for a in range(n):
            ld = [pltpu.make_async_copy(g[a].at[_block(*chip, c)], loc[a].at[j], load_sems.at[a, j])
                  for j, chip in enumerate(chips)]
            ld.append(pltpu.make_async_copy(g[a].at[_block(*me)], own[a], load_sems.at[a, 3]))
            for cp in ld:
                cp.start()
            loads.append(ld)
            blocks = [(*chip, 1 - c) for chip in chips] + [sib]
            for j, blk in enumerate(blocks):
                cp = pltpu.make_async_remote_copy(
                    src_ref=g[a].at[_block(*blk)], dst_ref=r1[a].at[j],
                    send_sem=s1.at[a, j], recv_sem=q1.at[a, j], device_id=sib, device_id_type=MESH)
                cp.start()
                sends.append(cp)

        def from_sibling(a, j):
            return pltpu.make_async_remote_copy(
                src_ref=r1[a].at[j], dst_ref=r1[a].at[j], send_sem=s1.at[a, j], recv_sem=q1.at[a, j],
                device_id=sib, device_id_type=MESH)

        def partial(a, j, chip):
            return pltpu.make_async_remote_copy(
                src_ref=loc[a].at[j], dst_ref=r2[a].at[j], send_sem=s2.at[a, j], recv_sem=q2.at[a, j],
                device_id=(*chip, c), device_id_type=MESH)

        gather.stage(1, *gather_refs)
        for a in range(n):
            for j, chip in enumerate(chips):
                loads[a][j].wait()
                from_sibling(a, j).wait_recv()
                loc[a][j] = (loc[a][j].astype(F32) + r1[a][j].astype(F32)).astype(BF16)
                cp = partial(a, j, chip)
                cp.start()
                sends.append(cp)
        gather.stage(2, *gather_refs)
        for a in range(n):
            loads[a][3].wait()
            from_sibling(a, 3).wait_recv()
            acc = own[a][...].astype(F32) + r1[a][3].astype(F32)
            for j, chip in enumerate(chips):
                partial(a, j, chip).wait_recv()
                acc = acc + r2[a][j].astype(F32)
            out[a][...] = acc
        for cp in sends:
            cp.wait_send()
        gather.drain(*gather_refs)

    any_spec = pl.BlockSpec(memory_space=pl.ANY)
    vmem_spec = pl.BlockSpec(memory_space=pltpu.VMEM)
    res = pl.pallas_call(
        body,
        in_specs=[any_spec] * (n + ns),
        out_specs=[vmem_spec] * n + [any_spec] * ns,
        out_shape=[jax.ShapeDtypeStruct(sh, F32) for sh in shapes] + gather.out_shapes,
        scratch_shapes=(
            [pltpu.VMEM(sh, BF16) for sh in shapes]
            + [pltpu.VMEM((3,) + sh, BF16) for sh in shapes]
            + [pltpu.VMEM((4,) + sh, BF16) for sh in shapes]
            + [pltpu.VMEM((3,) + sh, BF16) for sh in shapes]
            + [pltpu.SemaphoreType.DMA((n, 4)),
               pltpu.SemaphoreType.DMA((n, 4)), pltpu.SemaphoreType.DMA((n, 4)),
               pltpu.SemaphoreType.DMA((n, 3)), pltpu.SemaphoreType.DMA((n, 3))]
            + gather.sem_shapes()
        ),
        compiler_params=_params(collective_id=_collective_id(EVERY_PEER)),
        name=name,
    )(*grads, *small)
    return res[:n], res[n:]


def _pair_add(grads, from_sib, blks, name):
    n = len(grads)

    def body(blk_ref, *refs):
        for a in range(n):
            refs[2 * n + a][...] = (refs[a][...].astype(F32) + refs[n + a][...].astype(F32)).astype(BF16)

    mine = [pl.BlockSpec((None,) + g.shape[1:], lambda j, b: (b[j], 0, 0)) for g in grads]
    same = [pl.BlockSpec((None,) + g.shape[1:], lambda j, b: (j, 0, 0)) for g in grads]
    return pl.pallas_call(
        body,
        grid_spec=pltpu.PrefetchScalarGridSpec(
            num_scalar_prefetch=1, grid=(4,), in_specs=mine + same, out_specs=same),
        out_shape=[jax.ShapeDtypeStruct((4,) + g.shape[1:], BF16) for g in grads],
        compiler_params=_params(("arbitrary",)),
        name=name,
    )(blks, *grads, *from_sib)


def _chip_sum(parts, from_chips, name):
    n = len(parts)

    def body(*refs):
        for a in range(n):
            acc = refs[a][...].astype(F32)
            for j in range(3):
                acc = acc + refs[n + a][j].astype(F32)
            refs[2 * n + a][...] = acc

    half = [p.shape[1] // 2 for p in parts]
    return pl.pallas_call(
        body,
        grid=(2,),
        in_specs=([pl.BlockSpec((None, h, p.shape[2]), lambda i: (3, i, 0)) for p, h in zip(parts, half)]
                  + [pl.BlockSpec((3, h, p.shape[2]), lambda i: (0, i, 0)) for p, h in zip(parts, half)]),
        out_specs=[pl.BlockSpec((h, p.shape[2]), lambda i: (i, 0)) for p, h in zip(parts, half)],
        out_shape=[jax.ShapeDtypeStruct(p.shape[1:], F32) for p in parts],
        compiler_params=_params(("arbitrary",)),
        name=name,
    )(*parts, *from_chips)


def _adam_math(w, g, m, v):
    nm = ADAM_B1 * m + (1.0 - ADAM_B1) * g
    nv = ADAM_B2 * v + (1.0 - ADAM_B2) * (g * g)
    m_hat = nm / (1.0 - ADAM_B1 ** ADAM_STEP)
    v_hat = nv / (1.0 - ADAM_B2 ** ADAM_STEP)
    return -ADAM_LR * (m_hat / (jnp.sqrt(v_hat) + ADAM_EPS) + ADAM_WD * w), nm, nv


def _sum_adamw(items, name, carry=None):
    n = len(items)

    def body(*refs):
        for a in range(n):
            p_ref, f_ref, w_ref, m_ref, v_ref = refs[5 * a:5 * a + 5]
            g_ref, d_ref, nm_ref, nv_ref = refs[5 * n + 4 * a:5 * n + 4 * a + 4]
            g = p_ref[...].astype(F32)
            for j in range(3):
                g = g + f_ref[j].astype(F32)
            g_ref[...] = g
            d_ref[...], nm_ref[...], nv_ref[...] = _adam_math(w_ref[...], g, m_ref[...], v_ref[...])

    in_specs, out_specs, out_shape, args = [], [], [], []
    for part, from_chips, w, m, v in items:
        r, c = w.shape
        spec = pl.BlockSpec((r // 2, c), lambda i: (i, 0))
        in_specs += [pl.BlockSpec((None, r // 2, c), lambda i: (3, i, 0)),
                     pl.BlockSpec((3, r // 2, c), lambda i: (0, i, 0)), spec, spec, spec]
        out_specs += [spec] * 4
        out_shape += [jax.ShapeDtypeStruct((r, c), F32)] * 4
        args += [part, from_chips, w, m, v]
    outs, carried = _pcall(body, grid=(2,), in_specs=in_specs, out_specs=out_specs, out_shape=out_shape,
                           scratch_shapes=[], name=name, args=tuple(args), carry=carry)
    return [tuple(outs[4 * a:4 * a + 4]) for a in range(n)], carried


def _small_update(gathered, layout, params, name):
    ng, npar = len(gathered), len(params)

    def body(*refs):
        parts = refs[:ng]
        prm = refs[ng:ng + 3 * npar]
        tot_refs = refs[ng + 3 * npar:2 * ng + 3 * npar]
        out = refs[2 * ng + 3 * npar:]
        tots = []
        for a in range(ng):
            acc = parts[a][0]
            for d in range(1, N_DEV):
                acc = acc + parts[a][d]
            tot_refs[a][...] = acc
            tots.append(acc)
        for i, (a, row, width) in enumerate(layout):
            g = tots[a] if row is None else tots[a][row:row + 1, :width]
            delta, nm, nv = _adam_math(prm[3 * i][...], g, prm[3 * i + 1][...], prm[3 * i + 2][...])
            out[4 * i][...] = g
            out[4 * i + 1][...] = delta
            out[4 * i + 2][...] = nm
            out[4 * i + 3][...] = nv

    flat = [t for prm in params for t in prm]
    res = pl.pallas_call(
        body,
        out_shape=([jax.ShapeDtypeStruct(g.shape[1:], F32) for g in gathered]
                   + [jax.ShapeDtypeStruct(prm[0].shape, F32) for prm in params for _ in range(4)]),
        compiler_params=pltpu.CompilerParams(vmem_limit_bytes=V7X_VMEM_LIMIT),
        name=name,
    )(*gathered, *flat)
    return res[:ng], [tuple(res[ng + 4 * i:ng + 4 * i + 4]) for i in range(npar)]


def _adamw(w, g, m, v, name):
    rows, cols = w.shape
    br = rows
    for cand in (512, 256, 128):
        if rows % cand == 0 and rows > cand:
            br = cand
            break

    def body(w_ref, g_ref, m_ref, v_ref, d_ref, nm_ref, nv_ref):
        d_ref[...], nm_ref[...], nv_ref[...] = _adam_math(w_ref[...], g_ref[...], m_ref[...], v_ref[...])

    spec = pl.BlockSpec((br, cols), lambda i: (i, 0))
    shape = jax.ShapeDtypeStruct((rows, cols), F32)
    return pl.pallas_call(
        body,
        grid=(rows // br,),
        in_specs=[spec] * 4,
        out_specs=[spec] * 3,
        out_shape=[shape] * 3,
        compiler_params=_params(("arbitrary",)),
        name=name,
    )(w, g, m, v)


def _by_device(full):
    return full.reshape(N_DEV, full.shape[0] // N_DEV, full.shape[1])


def kernel(x, p, g_mix, w_in, conv_w, conv_b, ln_g, ln_b, pool_w, pool_scale, w_out, g_ffn, w_gate_up, w_down, g_ple_gate, w_ple_gate, w_ple_up, g_ple_post, g_final, loss_target, m_g_mix, m_w_in, m_conv_w, m_conv_b, m_ln_g, m_ln_b, m_pool_w, m_pool_scale, m_w_out, m_g_ffn, m_w_gate_up, m_w_down, m_g_ple_gate, m_w_ple_gate, m_w_ple_up, m_g_ple_post, m_g_final, v_g_mix, v_w_in, v_conv_w, v_conv_b, v_ln_g, v_ln_b, v_pool_w, v_pool_scale, v_w_out, v_g_ffn, v_w_gate_up, v_w_down, v_g_ple_gate, v_w_ple_gate, v_w_ple_up, v_g_ple_post, v_g_final):
    seq = x.shape[1]
    tile = min(TOKEN_TILE, seq)
    xs = x[0]
    ps = p[0, 0]
    tgt = loss_target[0]
    ax, ay, ac = lax.axis_index("x"), lax.axis_index("y"), lax.axis_index("c")
    me = _block(ax, ay, ac)
    blks = jnp.stack([_block(1 - ax, ay, ac), _block(ax, 1 - ay, ac), _block(1 - ax, 1 - ay, ac), me]).astype(jnp.int32)
    rows = lambda gth: gth.reshape((-1,) + gth.shape[2:])

    sh_in, sh_out, sh_gu, sh_down, sh_pg, sh_pu, pool_w_b = _to_bf16([
        w_in[0].T,
        w_out[0],
        w_gate_up[0].T,
        w_down[0],
        w_ple_gate[0],
        w_ple_up[0].T,
        pool_w[0]])
    carry = _Carry()
    _carry_gather(carry, [sh_in, sh_out,
                          jnp.pad(conv_w[0].T, ((0, 0), (0, CONV_HALO - CONV_K)))], 8, 15)
    (h1,), first = _norm_in(xs, g_mix, tile, carry)
    w_in_t, w_out_f, conv_w_t = [rows(gth) for gth in first]
    conv_w_f = conv_w_t.T

    carry = _Carry()
    _carry_gather(carry, [sh_gu], 8, 12)
    (z, u1, pooled, x1, mix), (w_gu_all,) = _fwd_mix(
        xs, h1, w_in_t, w_out_f, conv_w_f, conv_b, ln_g, ln_b, pool_w_b, pool_scale, tile, carry)
    w_gu_t = rows(w_gu_all)

    carry = _Carry()
    _carry_gather(carry, [sh_down, sh_pg, sh_pu], 6, 10)
    (h2, g_sav, u_sav, act), late = _ffn_up(x1, w_gu_t, g_ffn, tile, carry)
    w_down_f, w_pg_f, w_pu_t = [rows(gth) for gth in late]
    dx2, dx2b, hg, ds, dpe, pb, stats_ple = _down_ple(
        x1, act, w_down_f, ps, tgt, w_pg_f, w_pu_t, g_ple_gate, g_ple_post, g_final.reshape(1, D_MODEL),
        tile)
    dg, du, dx1, dx1b, stats_ffn = _ffn_bwd(dx2, dx2b, x1, g_sav, u_sav, w_gu_t, w_down_f, g_ffn,
                                            tile)[0]

    d_w_pg, _ = _grad_matmul(hg, ds, GRAD_ROWS, "grad_w_ple_gate")
    d_w_pu_t, _ = _grad_matmul(dpe, pb, GRAD_ROWS, "grad_w_ple_up")
    d_w_gu_t, _ = _grad_matmul(dg, h2, GRAD_ROWS, "grad_w_gate_up", a2=du)
    d_w_out, _ = _grad_matmul(mix, dx1b, GRAD_ROWS, "grad_w_out")
    early = [_by_device(d_w_pg), _by_device(d_w_pu_t), _by_device(d_w_gu_t), _by_device(d_w_out)]
    carry = _Carry()
    _carry_pair(carry, early)
    d_w_down, from_sib = _grad_matmul(act, dx2b, GRAD_ROWS, "grad_w_down", carry=carry)
    early_parts = _pair_add(early, from_sib, blks, "pair_add_early")

    carry = _Carry()
    _carry_chip(carry, early_parts)
    _carry_pair(carry, [_by_device(d_w_down)])
    (dx, dz, vec_mix, dconv_w_part, dpool_w_part), carried = _bwd_mix(
        dx1, dx1b, xs, z, u1, pooled, w_in_t, w_out_f, g_mix, conv_w_f, ln_g, ln_b, pool_w_b, pool_scale,
        tile, carry)
    early_chips, down_sib = carried[:4], carried[4:]
    down_parts = _pair_add([_by_device(d_w_down)], down_sib, blks, "pair_add_down")

    carry = _Carry()
    _carry_chip(carry, down_parts)
    d_w_in_t, down_chips = _grad_matmul(dz, h1, GRAD_ROWS, "grad_w_in", carry=carry)
    (gr_w_in_t,), small = _reduce_scatter(
        [_by_device(d_w_in_t)],
        [vec_mix, stats_ple, stats_ffn, dconv_w_part, dpool_w_part.reshape(-1, POOL_GROUP)], "scatter_last")
    small = list(small[:4]) + [small[4].reshape((N_DEV,) + dpool_w_part.shape)]

    vec_names = ["g_mix", "ln_g", "ln_b", "conv_b", "pool_scale", "g_final", "g_ple_post", "g_ple_gate", "g_ffn",
                 "pool_w"]
    layout = [(0, 0, D_MODEL), (0, 1, C_CONV), (0, 2, C_CONV), (0, 3, C_CONV), (0, 4, C_POOL),
              (1, 0, D_MODEL), (1, 1, D_MODEL), (1, 2, D_MODEL), (2, 0, D_MODEL), (4, None, None)]
    as_row = lambda t: t.reshape(1, D_MODEL)
    params = [(g_mix, m_g_mix, v_g_mix), (ln_g, m_ln_g, v_ln_g), (ln_b, m_ln_b, v_ln_b),
              (conv_b, m_conv_b, v_conv_b), (pool_scale, m_pool_scale, v_pool_scale),
              (as_row(g_final), as_row(m_g_final), as_row(v_g_final)),
              (g_ple_post, m_g_ple_post, v_g_ple_post), (g_ple_gate, m_g_ple_gate, v_g_ple_gate),
              (g_ffn, m_g_ffn, v_g_ffn), (pool_w[0], m_pool_w[0], v_pool_w[0])]
    tots, small_upd = _small_update(small, layout, params, "small_update")
    loss = tots[1][3, 0]
    upd = {}
    for nm, res, prm in zip(vec_names, small_upd, [g_mix, ln_g, ln_b, conv_b, pool_scale, g_final, g_ple_post,
                                                    g_ple_gate, g_ffn, pool_w]):
        upd[nm] = tuple(t.reshape(prm.shape) for t in res)
    gr_conv_w = lax.dynamic_slice_in_dim(tots[3][:CONV_K], me * (C_CONV // N_DEV), C_CONV // N_DEV, axis=1)

    natural = lambda t: t[None]
    turned = lambda t: t.T[None]
    (res_pg, res_gu, res_out, res_down), _ = _sum_adamw([
        (early_parts[0], early_chips[0], w_ple_gate[0], m_w_ple_gate[0], v_w_ple_gate[0]),
        (early_parts[2], early_chips[2], w_gate_up[0].T, m_w_gate_up[0].T, v_w_gate_up[0].T),
        (early_parts[3], early_chips[3], w_out[0], m_w_out[0], v_w_out[0]),
        (down_parts[0], down_chips[0], w_down[0], m_w_down[0], v_w_down[0])], "adamw_big")
    upd["w_ple_gate"] = tuple(natural(t) for t in res_pg)
    upd["w_gate_up"] = tuple(turned(t) for t in res_gu)
    upd["w_out"] = tuple(natural(t) for t in res_out)
    upd["w_down"] = tuple(natural(t) for t in res_down)
    (gr_w_pu_t,) = _chip_sum(early_parts[1:2], early_chips[1:2], "chip_sum")
    plain = [
        ("w_in", w_in[0].T, gr_w_in_t, m_w_in[0].T, v_w_in[0].T, True),
        ("w_ple_up", w_ple_up[0], gr_w_pu_t.T, m_w_ple_up[0], v_w_ple_up[0], False),
        ("conv_w", conv_w[0], gr_conv_w, m_conv_w[0], v_conv_w[0], False),
    ]
    for nm, w_, g_, m_, v_, transposed in plain:
        res = (g_,) + tuple(_adamw(w_, g_, m_, v_, "adamw_" + nm))
        upd[nm] = tuple((t.T if transposed else t)[None] for t in res)

    order = ["g_mix", "w_in", "conv_w", "conv_b", "ln_g", "ln_b", "pool_w", "pool_scale", "w_out", "g_ffn",
             "w_gate_up", "w_down", "g_ple_gate", "w_ple_gate", "w_ple_up", "g_ple_post", "g_final"]
    outs = [loss, dx[None]]
    for k in range(4):
        outs += [upd[nm][k] for nm in order]
    return tuple(outs)
```

```python
import functools

import jax
import jax.numpy as jnp
from jax import lax
from jax.experimental import pallas as pl
from jax.experimental.pallas import tpu as pltpu

D_MODEL = 1024
C_CONV = 512
C_POOL = 512
Z_WIDTH = 2 * C_CONV + C_POOL
POOL_WINDOWS = (2, 4, 8, 16)
POOL_GROUP = 128
CONV_K = 31
D_FF = 2816
D_PLE = 256
EPS = 1e-6
N_DEV = 8

ADAM_LR = 0.001
ADAM_B1 = 0.9
ADAM_B2 = 0.999
ADAM_EPS = 1e-08
ADAM_WD = 0.01
ADAM_STEP = 10

CONV_HALO = 32
POOL_HALO = 32
ROW_CHUNK = 32
TOKEN_TILE = 256
GRAD_ROWS = 256
V7X_VMEM_LIMIT = 56 * 1024 * 1024
FF_CHUNK = D_FF // 2

BF16 = jnp.bfloat16
F32 = jnp.float32
MESH = pl.DeviceIdType.MESH


def _dot(a, b):
    return lax.dot_general(a, b, (((1,), (0,)), ((), ())), preferred_element_type=F32)


def _dot_nt(a, b):
    return lax.dot_general(a, b, (((1,), (1,)), ((), ())), preferred_element_type=F32)


def _dot_tn(a, b):
    return lax.dot_general(a, b, (((0,), (0,)), ((), ())), preferred_element_type=F32)


def _rms_fwd(x, g):
    r = lax.rsqrt(jnp.mean(x * x, axis=-1, keepdims=True) + EPS)
    return x * r * g, r


def _rms_bwd(x, r, g, dy):
    xr = x * r
    dg = jnp.sum(dy * xr, axis=0, keepdims=True)
    dyg = dy * g
    dx = r * (dyg - xr * jnp.mean(dyg * xr, axis=-1, keepdims=True))
    return dx, dg


def _sigmoid(x):
    return jax.nn.sigmoid(x)


def _params(sem=None, collective_id=None):
    return pltpu.CompilerParams(dimension_semantics=sem, vmem_limit_bytes=V7X_VMEM_LIMIT,
                                collective_id=collective_id)


def _place():
    x, y, c = lax.axis_index("x"), lax.axis_index("y"), lax.axis_index("c")
    chips = [(1 - x, y), (x, 1 - y), (1 - x, 1 - y)]
    return (x, y, c), (x, y, 1 - c), chips


def _block(px, py, pc):
    return 4 * px + 2 * py + pc


EVERY_PEER = ("sib", 0, 1, 2)
PEER_SET_IDS = {("sib",): 0, ("0", "1", "sib"): 1, ("0", "1", "2"): 2, ("0", "1", "2", "sib"): 3}


def _collective_id(peers):
    return PEER_SET_IDS[tuple(sorted(str(p) for p in peers))]


def _handshake(peers):
    me, sib, chips = _place()
    barrier = pltpu.get_barrier_semaphore()
    for p in peers:
        to = sib if p == "sib" else (*chips[p], me[2])
        pl.semaphore_signal(barrier, inc=1, device_id=to, device_id_type=MESH)
    pl.semaphore_wait(barrier, len(peers))


class _Carry:
    def __init__(self):
        self.inputs = []
        self.out_shapes = []
        self.copies = []
        self.locals = []

    def add_input(self, arr):
        self.inputs.append(arr)
        return len(self.inputs) - 1

    def add_output(self, shape, dtype):
        self.out_shapes.append(jax.ShapeDtypeStruct(shape, dtype))
        return len(self.out_shapes) - 1

    def local(self, src_idx, dst_idx, dst_blk):
        self.locals.append((src_idx, dst_idx, dst_blk))

    def copy(self, src, dst_idx, dst_blk, got_blk, peer, step=0, after=()):
        self.copies.append(dict(src=src, dst_idx=dst_idx, dst_blk=dst_blk, got_blk=got_blk, peer=peer, step=step,
                                after=tuple(after)))
        return len(self.copies) - 1

    def sem_shapes(self):
        return [pltpu.SemaphoreType.DMA((max(1, len(self.copies)),)),
                pltpu.SemaphoreType.DMA((max(1, len(self.copies)),)),
                pltpu.SemaphoreType.DMA((max(1, len(self.locals)),))]

    @staticmethod
    def _view(ref, where):
        if isinstance(where, tuple):
            blk, row0, nrows = where
            return ref.at[blk, pl.ds(row0, nrows)]
        return ref.at[where]

    def _desc(self, k, ins, outs, sems, place):
        cp = self.copies[k]
        me, sib, chips = place
        kind, idx, blk = cp["src"]
        src = (ins if kind == "in" else outs)[idx]
        if blk is not None:
            src = self._view(src, blk(*place))
        to = sib if cp["peer"] == "sib" else (*chips[cp["peer"]], me[2])
        return pltpu.make_async_remote_copy(
            src_ref=src, dst_ref=self._view(outs[cp["dst_idx"]], cp["dst_blk"](*place)),
            send_sem=sems[0].at[k], recv_sem=sems[1].at[k], device_id=to, device_id_type=MESH)

    def _arrival(self, k, outs, sems, place):
        cp = self.copies[k]
        got = self._view(outs[cp["dst_idx"]], cp["got_blk"](*place))
        return pltpu.make_async_remote_copy(
            src_ref=got, dst_ref=got, send_sem=sems[0].at[k], recv_sem=sems[1].at[k],
            device_id=place[0], device_id_type=MESH)

    def _local(self, n, ins, outs, sems, place):
        src_idx, dst_idx, blk = self.locals[n]
        return pltpu.make_async_copy(ins[src_idx], outs[dst_idx].at[blk(*place)], sems[2].at[n])

    def stages(self):
        return sorted({0} | {cp["step"] for cp in self.copies})

    def peers(self):
        return sorted({cp["peer"] for cp in self.copies}, key=str)

    def stage(self, s, ins, outs, sems, handshake=True):
        place = _place()
        if s == 0:
            if handshake:
                _handshake(self.peers())
            self._waited = set()
            for n in range(len(self.locals)):
                self._local(n, ins, outs, sems, place).start()
        for k, cp in enumerate(self.copies):
            if cp["step"] != s:
                continue
            for a in cp["after"]:
                if a not in self._waited:
                    self._arrival(a, outs, sems, place).wait_recv()
                    self._waited.add(a)
            self._desc(k, ins, outs, sems, place).start()

    def drain(self, ins, outs, sems):
        place = _place()
        for k in range(len(self.copies)):
            if k not in self._waited:
                self._arrival(k, outs, sems, place).wait_recv()
        for k in range(len(self.copies)):
            self._desc(k, ins, outs, sems, place).wait_send()
        for n in range(len(self.locals)):
            self._local(n, ins, outs, sems, place).wait()

    def starts(self, step, nsteps, ins, outs, sems):
        for s in self.stages():
            pl.when(step == min(s, nsteps - 1))(functools.partial(self.stage, s, ins, outs, sems))

    def finish(self, step, nsteps, ins, outs, sems):
        pl.when(step == nsteps - 1)(functools.partial(self.drain, ins, outs, sems))


def _const_blk(j):
    return lambda me, sib, chips: j


def _carry_gather(carry, shards, relay_step, last_step):
    outs = []
    for sh in shards:
        i = carry.add_input(sh)
        o = carry.add_output((N_DEV,) + sh.shape, sh.dtype)
        half = sh.shape[0] // 2
        tile = 16 if sh.dtype == BF16 else 8
        split = half % tile == 0
        rows = [(0, half), (half, sh.shape[0] - half)] if split else [(0, sh.shape[0]), None]

        def whole(j, core):
            return lambda me, sib, chips, j=j, core=core: _block(*chips[j], me[2] if core == 0 else 1 - me[2])

        def part(j, core, h, rows=rows):
            return lambda me, sib, chips: (_block(*chips[j], me[2] if core == 0 else 1 - me[2]),) + rows[h]

        mine = lambda me, sib, chips: _block(*me)
        carry.local(i, o, mine)
        carry.copy(("in", i, None), o, mine, lambda me, sib, chips: _block(*sib), "sib")
        near = [carry.copy(("in", i, None), o, mine, whole(j, 0), j) for j in range(2)]
        for j in range(2):
            carry.copy(("out", o, whole(j, 0)), o, whole(j, 0), whole(j, 1), "sib", step=relay_step, after=(near[j],))
        for j in range(2):
            if rows[j] is None:
                continue
            far = carry.copy(("out", o, part(j, 0, j)), o, part(j, 0, j), part(2, 0, j), 1 - j,
                             step=relay_step, after=(near[j],))
            carry.copy(("out", o, part(2, 0, j)), o, part(2, 0, j), part(2, 1, j), "sib", step=last_step, after=(far,))
        outs.append(o)
    return outs


def _carry_pair(carry, grads):
    outs = []
    for g in grads:
        i = carry.add_input(g)
        o = carry.add_output((4,) + g.shape[1:], g.dtype)
        for j in range(4):
            if j < 3:
                blk = lambda me, sib, chips, j=j: _block(*chips[j], 1 - me[2])
            else:
                blk = lambda me, sib, chips: _block(*sib)
            carry.copy(("in", i, blk), o, _const_blk(j), _const_blk(j), "sib")
        outs.append(o)
    return outs


def _carry_chip(carry, parts):
    outs = []
    for p in parts:
        i = carry.add_input(p)
        o = carry.add_output((3,) + p.shape[1:], p.dtype)
        for j in range(3):
            carry.copy(("in", i, _const_blk(j)), o, _const_blk(j), _const_blk(j), j)
        outs.append(o)
    return outs


def _pcall(body, *, grid, in_specs, out_specs, out_shape, scratch_shapes, name, args, carry=None):
    sem = ("arbitrary",) * len(grid)
    if carry is None:
        res = pl.pallas_call(body, grid=grid, in_specs=in_specs, out_specs=out_specs, out_shape=out_shape,
                             scratch_shapes=scratch_shapes, compiler_params=_params(sem), name=name)(*args)
        return list(res), []
    n_in, n_out, n_scr = len(in_specs), len(out_specs), len(scratch_shapes)
    c_in, c_out = len(carry.inputs), len(carry.out_shapes)
    nsteps = 1
    for extent in grid:
        nsteps *= extent

    def wrapped(*refs):
        ins = refs[:n_in]
        cins = refs[n_in:n_in + c_in]
        o0 = n_in + c_in
        outs = refs[o0:o0 + n_out]
        couts = refs[o0 + n_out:o0 + n_out + c_out]
        s0 = o0 + n_out + c_out
        scr = refs[s0:s0 + n_scr]
        sems = refs[s0 + n_scr:]
        step = pl.program_id(0)
        for d in range(1, len(grid)):
            step = step * grid[d] + pl.program_id(d)
        carry.starts(step, nsteps, cins, couts, sems)
        body(*ins, *outs, *scr)
        carry.finish(step, nsteps, cins, couts, sems)

    any_spec = pl.BlockSpec(memory_space=pl.ANY)
    res = pl.pallas_call(
        wrapped, grid=grid,
        in_specs=list(in_specs) + [any_spec] * c_in,
        out_specs=list(out_specs) + [any_spec] * c_out,
        out_shape=list(out_shape) + carry.out_shapes,
        scratch_shapes=list(scratch_shapes) + carry.sem_shapes(),
        compiler_params=_params(sem, _collective_id(carry.peers())), name=name)(*args, *carry.inputs)
    return list(res[:n_out]), list(res[n_out:])


def _shifted_copies(buf, shifted, tm):
    span = tm + CONV_HALO - 8
    for r in range(1, 8):
        shifted[r - 1, 0:span, :] = buf[r:r + span, :]


def _rows_at(buf, shifted, start):
    aligned, r = (start // 8) * 8, start % 8
    if r == 0:
        return buf[aligned:aligned + ROW_CHUNK, :]
    return shifted[r - 1, aligned:aligned + ROW_CHUNK, :]


def _window_sums(buf, cols, work, levels, tm, trailing):
    src, src_cols = buf, cols
    for k in range(levels + 1):
        shift = 1 << k
        dst = work.at[k % 2]
        if trailing:
            lo = 8 * (k + 1)
            dst[lo:tm + POOL_HALO, :] = (src[lo:tm + POOL_HALO, src_cols]
                                         + src[lo - shift:tm + POOL_HALO - shift, src_cols])
        else:
            hi = tm + POOL_HALO - 8 * (k + 1)
            dst[0:hi, :] = src[0:hi, src_cols] + src[shift:hi + shift, src_cols]
        src, src_cols = dst, slice(0, POOL_GROUP)
    return src[POOL_HALO:POOL_HALO + tm, src_cols] if trailing else src[0:tm, src_cols]


def _row_spec(tm, width):
    return pl.BlockSpec((tm, width), lambda i: (i, 0))


def _const_spec(shape):
    return pl.BlockSpec(shape, lambda i: (0,) * len(shape))


def _weight_spec(shape):
    return pl.BlockSpec(shape, lambda i: (0,) * len(shape), pipeline_mode=pl.Buffered(1))


def _to_bf16(arrays):
    n = len(arrays)

    def body(*refs):
        for a in range(n):
            refs[n + a][...] = refs[a][...].astype(BF16)

    whole = [pl.BlockSpec(arr.shape, lambda i, nd=arr.ndim: (0,) * nd) for arr in arrays]
    return pl.pallas_call(
        body,
        grid=(1,),
        in_specs=whole,
        out_specs=whole,
        out_shape=[jax.ShapeDtypeStruct(arr.shape, BF16) for arr in arrays],
        compiler_params=_params(("arbitrary",)),
        name="to_bf16",
    )(*arrays)


def _norm_in(x, g_mix, tm, carry=None):
    s = x.shape[0]

    def body(x_ref, gmix_ref, h1_ref):
        h, _ = _rms_fwd(x_ref[...], gmix_ref[...])
        h1_ref[...] = h.astype(BF16)

    return _pcall(
        body,
        grid=(s // tm,),
        in_specs=[_row_spec(tm, D_MODEL), _const_spec((1, D_MODEL))],
        out_specs=[_row_spec(tm, D_MODEL)],
        out_shape=[jax.ShapeDtypeStruct((s, D_MODEL), BF16)],
        scratch_shapes=[],
        name="norm_in",
        args=(x, g_mix),
        carry=carry,
    )


def _fwd_mix(x, h1, w_in_t, w_out, conv_w, conv_b, ln_g, ln_b, pool_w, pool_scale, tm, carry=None):
    s = x.shape[0]
    nt = s // tm

    def body(x_ref, h1_ref, win_ref, wout_ref, cw_ref, cb_ref, lng_ref, lnb_ref, pw_ref, ps_ref,
             z_ref, u1_ref, pooled_ref, x1_ref, mix_ref, ubuf, vbuf, ush, pbuf):
        i = pl.program_id(0)

        @pl.when(i == 0)
        def _():
            ubuf[0:CONV_HALO, :] = jnp.zeros((CONV_HALO, C_CONV), F32)
            vbuf[0:POOL_HALO, :] = jnp.zeros((POOL_HALO, C_POOL), F32)

        xt = x_ref[...]
        z = _dot_nt(h1_ref[...], win_ref[...])
        z_ref[...] = z
        a = z[:, :C_CONV]
        b = z[:, C_CONV:2 * C_CONV]
        v = z[:, 2 * C_CONV:]
        ubuf[CONV_HALO:CONV_HALO + tm, :] = a * _sigmoid(b)
        vbuf[POOL_HALO:POOL_HALO + tm, :] = v

        _shifted_copies(ubuf, ush, tm)
        for rc in range(tm // ROW_CHUNK):
            base = rc * ROW_CHUNK + CONV_HALO - (CONV_K - 1)
            acc = jnp.broadcast_to(cb_ref[...], (ROW_CHUNK, C_CONV))
            for k in range(CONV_K):
                acc = acc + cw_ref[k:k + 1, :] * _rows_at(ubuf, ush, base + k)
            u1_ref[rc * ROW_CHUNK:(rc + 1) * ROW_CHUNK, :] = acc

        u1 = u1_ref[...]
        mu = jnp.mean(u1, axis=-1, keepdims=True)
        cen = u1 - mu
        rstd = lax.rsqrt(jnp.mean(cen * cen, axis=-1, keepdims=True) + EPS)
        u2 = cen * rstd * lng_ref[...] + lnb_ref[...]
        u = u2 * _sigmoid(u2)

        pos1 = (i * tm + lax.broadcasted_iota(jnp.int32, (tm, 1), 0) + 1).astype(F32)
        parts = [u]
        for g, w in enumerate(POOL_WINDOWS):
            cols = slice(g * POOL_GROUP, (g + 1) * POOL_GROUP)
            vg = v[:, cols]
            tot = _window_sums(vbuf, cols, pbuf, g, tm, trailing=True)
            pooled = tot / jnp.minimum(pos1, float(w)) - vg
            pooled_b = pooled.astype(BF16)
            pooled_ref[:, cols] = pooled_b
            parts.append(_dot(pooled_b, pw_ref[g]) * ps_ref[:, cols])
        mix = jnp.concatenate(parts, axis=-1).astype(BF16)
        mix_ref[...] = mix
        x1_ref[...] = xt + _dot(mix, wout_ref[...])

        ubuf[0:CONV_HALO, :] = ubuf[tm:tm + CONV_HALO, :]
        vbuf[0:POOL_HALO, :] = vbuf[tm:tm + POOL_HALO, :]

    return _pcall(
        body,
        grid=(nt,),
        in_specs=[
            _row_spec(tm, D_MODEL),
            _row_spec(tm, D_MODEL),
            _const_spec((Z_WIDTH, D_MODEL)),
            _const_spec((D_MODEL, D_MODEL)),
            _const_spec((CONV_HALO, C_CONV)),
            _const_spec((1, C_CONV)),
            _const_spec((1, C_CONV)),
            _const_spec((1, C_CONV)),
            _const_spec((len(POOL_WINDOWS), POOL_GROUP, POOL_GROUP)),
            _const_spec((1, C_POOL)),
        ],
        out_specs=[
            _row_spec(tm, Z_WIDTH),
            _row_spec(tm, C_CONV),
            _row_spec(tm, C_POOL),
            _row_spec(tm, D_MODEL),
            _row_spec(tm, D_MODEL),
        ],
        out_shape=[
            jax.ShapeDtypeStruct((s, Z_WIDTH), F32),
            jax.ShapeDtypeStruct((s, C_CONV), F32),
            jax.ShapeDtypeStruct((s, C_POOL), BF16),
            jax.ShapeDtypeStruct((s, D_MODEL), F32),
            jax.ShapeDtypeStruct((s, D_MODEL), BF16),
        ],
        scratch_shapes=[
            pltpu.VMEM((tm + CONV_HALO, C_CONV), F32),
            pltpu.VMEM((tm + POOL_HALO, C_POOL), F32),
            pltpu.VMEM((7, tm + CONV_HALO, C_CONV), F32),
            pltpu.VMEM((2, tm + POOL_HALO, POOL_GROUP), F32),
        ],
        name="fwd_mix",
        args=(x, h1, w_in_t, w_out, conv_w, conv_b, ln_g, ln_b, pool_w, pool_scale),
        carry=carry,
    )


def _ffn_up(x1, w_gu_t, g_ffn, tm, carry=None):
    s = x1.shape[0]

    def body(x1_ref, w_ref, gffn_ref, h2_ref, g_ref, u_ref, act_ref):
        h, _ = _rms_fwd(x1_ref[...], gffn_ref[...])
        h2 = h.astype(BF16)
        h2_ref[...] = h2
        for c in range(D_FF // FF_CHUNK):
            cols = slice(c * FF_CHUNK, (c + 1) * FF_CHUNK)
            g = _dot_nt(h2, w_ref[c * FF_CHUNK:(c + 1) * FF_CHUNK, :])
            u = _dot_nt(h2, w_ref[D_FF + c * FF_CHUNK:D_FF + (c + 1) * FF_CHUNK, :])
            g_ref[:, cols] = g.astype(BF16)
            u_ref[:, cols] = u.astype(BF16)
            act_ref[:, cols] = (g * _sigmoid(g) * u).astype(BF16)

    return _pcall(
        body,
        grid=(s // tm,),
        in_specs=[_row_spec(tm, D_MODEL), _weight_spec((2 * D_FF, D_MODEL)), _const_spec((1, D_MODEL))],
        out_specs=[_row_spec(tm, D_MODEL), _row_spec(tm, D_FF), _row_spec(tm, D_FF), _row_spec(tm, D_FF)],
        out_shape=[
            jax.ShapeDtypeStruct((s, D_MODEL), BF16),
            jax.ShapeDtypeStruct((s, D_FF), BF16),
            jax.ShapeDtypeStruct((s, D_FF), BF16),
            jax.ShapeDtypeStruct((s, D_FF), BF16),
        ],
        scratch_shapes=[],
        name="ffn_up",
        args=(x1, w_gu_t, g_ffn),
        carry=carry,
    )


def _down_ple(x1, act, w_down, p, tgt, w_pg, w_pu_t, g_gate, g_post, g_final, tm):
    s = x1.shape[0]
    nt = s // tm

    def body(x1_ref, act_ref, wd_ref, p_ref, t_ref, wpg_ref, wpu_ref, gg_ref, gp_ref, gf_ref,
             dx2_ref, dx2b_ref, hg_ref, ds_ref, dpe_ref, pb_ref, stats_ref, x2_cur, x2_next):
        i = pl.program_id(0)

        @pl.when(i == 0)
        def _():
            stats_ref[...] = jnp.zeros_like(stats_ref)
            x2_cur[...] = jnp.zeros((tm, D_MODEL), F32)

        def down(c):
            cols = slice(c * 256, (c + 1) * 256)
            x2_next[:, cols] = x1_ref[:, cols] + _dot(act_ref[...], wd_ref[:, cols])

        x2 = x2_cur[...]
        counts = i >= 1

        hg, rg = _rms_fwd(x2, gg_ref[...])
        hg_b = hg.astype(BF16)
        hg_ref[...] = hg_b
        down(0)
        gate = _sigmoid(_dot(hg_b, wpg_ref[...]))
        pb = p_ref[...].astype(BF16)
        pb_ref[...] = pb
        pe = _dot_nt(pb, wpu_ref[...])
        e, rp = _rms_fwd(pe, gp_ref[...])
        x3 = x2 + gate * e
        down(1)
        y, r3 = _rms_fwd(x3, gf_ref[...])
        diff = y - t_ref[...]
        loss = 0.5 * jnp.sum(jnp.sum(diff * diff, axis=-1, keepdims=True), axis=0, keepdims=True) / D_MODEL
        dy = diff * (1.0 / D_MODEL)

        dx3, dgf = _rms_bwd(x3, r3, gf_ref[...], dy)
        down(2)
        dpe, dgp = _rms_bwd(pe, rp, gp_ref[...], dx3 * gate)
        dpe_ref[...] = dpe.astype(BF16)
        ds = (dx3 * e * gate * (1.0 - gate)).astype(BF16)
        ds_ref[...] = ds
        dhg = _dot_nt(ds, wpg_ref[...])
        down(3)
        dxg, dgg = _rms_bwd(x2, rg, gg_ref[...], dhg)
        dx2 = dx3 + dxg
        dx2_ref[...] = dx2
        dx2b_ref[...] = dx2.astype(BF16)
        x2_cur[...] = x2_next[...]

        stats_ref[0:1, :] += jnp.where(counts, dgf, 0.0)
        stats_ref[1:2, :] += jnp.where(counts, dgp, 0.0)
        stats_ref[2:3, :] += jnp.where(counts, dgg, 0.0)
        stats_ref[3:4, :] += jnp.where(counts, jnp.broadcast_to(loss, (1, D_MODEL)), 0.0)

    ahead = lambda width: pl.BlockSpec((tm, width), lambda i: (jnp.minimum(i, nt - 1), 0))
    behind = lambda width: pl.BlockSpec((tm, width), lambda i: (jnp.maximum(i - 1, 0), 0))
    return pl.pallas_call(
        body,
        grid=(nt + 1,),
        in_specs=[
            ahead(D_MODEL),
            ahead(D_FF),
            _weight_spec((D_FF, D_MODEL)),
            behind(D_PLE),
            behind(D_MODEL),
            _weight_spec((D_MODEL, D_MODEL)),
            _weight_spec((D_MODEL, D_PLE)),
            _const_spec((1, D_MODEL)),
            _const_spec((1, D_MODEL)),
            _const_spec((1, D_MODEL)),
        ],
        out_specs=[
            behind(D_MODEL),
            behind(D_MODEL),
            behind(D_MODEL),
            behind(D_MODEL),
            behind(D_MODEL),
            behind(D_PLE),
            _const_spec((8, D_MODEL)),
        ],
        out_shape=[
            jax.ShapeDtypeStruct((s, D_MODEL), F32),
            jax.ShapeDtypeStruct((s, D_MODEL), BF16),
            jax.ShapeDtypeStruct((s, D_MODEL), BF16),
            jax.ShapeDtypeStruct((s, D_MODEL), BF16),
            jax.ShapeDtypeStruct((s, D_MODEL), BF16),
            jax.ShapeDtypeStruct((s, D_PLE), BF16),
            jax.ShapeDtypeStruct((8, D_MODEL), F32),
        ],
        scratch_shapes=[pltpu.VMEM((tm, D_MODEL), F32), pltpu.VMEM((tm, D_MODEL), F32)],
        compiler_params=_params(("arbitrary",)),
        name="down_ple",
    )(x1, act, w_down, p, tgt, w_pg, w_pu_t, g_gate, g_post, g_final)


def _ffn_bwd(dx2, dx2b, x1, g_sav, u_sav, w_gu_t, w_down, g_ffn, tm, carry=None):
    s = x1.shape[0]

    def body(dx2_ref, dx2b_ref, x1_ref, g_ref, u_ref, w_ref, wd_ref, gffn_ref,
             dg_ref, du_ref, dx1_ref, dx1b_ref, stats_ref):
        @pl.when(pl.program_id(0) == 0)
        def _():
            stats_ref[...] = jnp.zeros_like(stats_ref)

        dx2b = dx2b_ref[...]
        nc = D_FF // FF_CHUNK
        dacts = [_dot_nt(dx2b, wd_ref[c * FF_CHUNK:(c + 1) * FF_CHUNK, :]) for c in range(nc)]
        dh2 = jnp.zeros((tm, D_MODEL), F32)
        for c in range(nc):
            cols = slice(c * FF_CHUNK, (c + 1) * FF_CHUNK)
            g = g_ref[:, cols].astype(F32)
            u = u_ref[:, cols].astype(F32)
            sg = _sigmoid(g)
            dg = (dacts[c] * u * sg * (1.0 + g * (1.0 - sg))).astype(BF16)
            du = (dacts[c] * g * sg).astype(BF16)
            dg_ref[:, cols] = dg
            du_ref[:, cols] = du
            dh2 = dh2 + _dot(dg, w_ref[c * FF_CHUNK:(c + 1) * FF_CHUNK, :])
            dh2 = dh2 + _dot(du, w_ref[D_FF + c * FF_CHUNK:D_FF + (c + 1) * FF_CHUNK, :])

        x1 = x1_ref[...]
        r2 = lax.rsqrt(jnp.mean(x1 * x1, axis=-1, keepdims=True) + EPS)
        dxn, dgf = _rms_bwd(x1, r2, gffn_ref[...], dh2)
        dx1 = dx2_ref[...] + dxn
        dx1_ref[...] = dx1
        dx1b_ref[...] = dx1.astype(BF16)
        stats_ref[0:1, :] += dgf

    return _pcall(
        body,
        grid=(s // tm,),
        in_specs=[
            _row_spec(tm, D_MODEL), _row_spec(tm, D_MODEL), _row_spec(tm, D_MODEL),
            _row_spec(tm, D_FF), _row_spec(tm, D_FF),
            _weight_spec((2 * D_FF, D_MODEL)), _weight_spec((D_FF, D_MODEL)), _const_spec((1, D_MODEL)),
        ],
        out_specs=[_row_spec(tm, D_FF), _row_spec(tm, D_FF), _row_spec(tm, D_MODEL), _row_spec(tm, D_MODEL),
                   _const_spec((8, D_MODEL))],
        out_shape=[
            jax.ShapeDtypeStruct((s, D_FF), BF16),
            jax.ShapeDtypeStruct((s, D_FF), BF16),
            jax.ShapeDtypeStruct((s, D_MODEL), F32),
            jax.ShapeDtypeStruct((s, D_MODEL), BF16),
            jax.ShapeDtypeStruct((8, D_MODEL), F32),
        ],
        scratch_shapes=[],
        name="ffn_bwd",
        args=(dx2, dx2b, x1, g_sav, u_sav, w_gu_t, w_down, g_ffn),
        carry=carry,
    )


def _bwd_mix(dx1, dx1b, x, z, u1, pooled, w_in_t, w_out, g_mix, conv_w, ln_g, ln_b, pool_w, pool_scale, tm,
             carry=None):
    s = x.shape[0]
    nt = s // tm

    def body(dx1_ref, dx1b_ref, x_ref, z_ref, u1_ref, pooled_ref, win_ref, wout_ref, gmix_ref, cw_ref,
             lng_ref, lnb_ref, pw_ref, ps_ref,
             dx_ref, dz_ref, vec_ref, dcw_ref, dpw_ref, dubuf, dvbuf, u0buf, du0buf, dush, pbuf):
        i = pl.program_id(0)
        tile = nt - 1 - i

        @pl.when(i == 0)
        def _():
            vec_ref[...] = jnp.zeros_like(vec_ref)
            dcw_ref[...] = jnp.zeros_like(dcw_ref)
            dpw_ref[...] = jnp.zeros_like(dpw_ref)
            dubuf[tm:tm + CONV_HALO, :] = jnp.zeros((CONV_HALO, C_CONV), F32)
            dvbuf[tm:tm + POOL_HALO, :] = jnp.zeros((POOL_HALO, C_POOL), F32)

        dmix = _dot_nt(dx1b_ref[...], wout_ref[...])
        du = dmix[:, :C_CONV]
        dq = dmix[:, C_CONV:]

        pos1 = (tile * tm + lax.broadcasted_iota(jnp.int32, (tm, 1), 0) + 1).astype(F32)
        dpooled_parts = []
        dps_rows = []
        for g, w in enumerate(POOL_WINDOWS):
            cols = slice(g * POOL_GROUP, (g + 1) * POOL_GROUP)
            pooled_b = pooled_ref[:, cols]
            mixed = _dot(pooled_b, pw_ref[g])
            dqg = dq[:, cols]
            dps_rows.append(jnp.sum(dqg * mixed, axis=0, keepdims=True))
            dmixed = (dqg * ps_ref[:, cols]).astype(BF16)
            dpw_ref[g] += _dot_tn(pooled_b, dmixed)
            dpooled = _dot_nt(dmixed, pw_ref[g])
            dpooled_parts.append(dpooled)
            dvbuf[0:tm, cols] = dpooled / jnp.minimum(pos1, float(w))
        vec_ref[4:5, 0:C_POOL] += jnp.concatenate(dps_rows, axis=-1)
        dv_parts = []
        for g, w in enumerate(POOL_WINDOWS):
            cols = slice(g * POOL_GROUP, (g + 1) * POOL_GROUP)
            tot = _window_sums(dvbuf, cols, pbuf, g, tm, trailing=False)
            dv_parts.append(tot - dpooled_parts[g])

        u1 = u1_ref[...]
        mu = jnp.mean(u1, axis=-1, keepdims=True)
        cen = u1 - mu
        rstd = lax.rsqrt(jnp.mean(cen * cen, axis=-1, keepdims=True) + EPS)
        xhat = cen * rstd
        u2 = xhat * lng_ref[...] + lnb_ref[...]
        sg2 = _sigmoid(u2)
        du2 = du * sg2 * (1.0 + u2 * (1.0 - sg2))
        vec_ref[1:2, 0:C_CONV] += jnp.sum(du2 * xhat, axis=0, keepdims=True)
        vec_ref[2:3, 0:C_CONV] += jnp.sum(du2, axis=0, keepdims=True)
        t1 = du2 * lng_ref[...]
        du1 = rstd * (t1 - jnp.mean(t1, axis=-1, keepdims=True)
                      - xhat * jnp.mean(t1 * xhat, axis=-1, keepdims=True))
        vec_ref[3:4, 0:C_CONV] += jnp.sum(du1, axis=0, keepdims=True)
        dubuf[0:tm, :] = du1

        zt = z_ref[...]
        a = zt[:, :C_CONV]
        sgb = _sigmoid(zt[:, C_CONV:2 * C_CONV])
        u0buf[...] = a * sgb

        _shifted_copies(dubuf, dush, tm)
        for rc in range(tm // ROW_CHUNK):
            r0 = rc * ROW_CHUNK
            acc = jnp.zeros((ROW_CHUNK, C_CONV), F32)
            for k in range(CONV_K):
                acc = acc + cw_ref[k:k + 1, :] * _rows_at(dubuf, dush, r0 + (CONV_K - 1) - k)
            du0buf[r0:r0 + ROW_CHUNK, :] = acc
        for k in range(CONV_K):
            acc = jnp.zeros((ROW_CHUNK, C_CONV), F32)
            for rc in range(tm // ROW_CHUNK):
                r0 = rc * ROW_CHUNK
                acc = acc + u0buf[r0:r0 + ROW_CHUNK, :] * _rows_at(dubuf, dush, r0 + (CONV_K - 1) - k)
            dcw_ref[k:k + 1, :] += jnp.sum(acc, axis=0, keepdims=True)
        du0 = du0buf[...]

        da = du0 * sgb
        db = du0 * a * sgb * (1.0 - sgb)
        dz = jnp.concatenate([da, db] + dv_parts, axis=-1).astype(BF16)
        dz_ref[...] = dz

        xt = x_ref[...]
        r1 = lax.rsqrt(jnp.mean(xt * xt, axis=-1, keepdims=True) + EPS)
        dh1 = _dot(dz, win_ref[...])
        dxn, dgm = _rms_bwd(xt, r1, gmix_ref[...], dh1)
        dx_ref[...] = dx1_ref[...] + dxn
        vec_ref[0:1, :] += dgm

        dubuf[tm:tm + CONV_HALO, :] = dubuf[0:CONV_HALO, :]
        dvbuf[tm:tm + POOL_HALO, :] = dvbuf[0:POOL_HALO, :]

    rev = lambda width: pl.BlockSpec((tm, width), lambda i: (nt - 1 - i, 0))
    return _pcall(
        body,
        grid=(nt,),
        in_specs=[
            rev(D_MODEL), rev(D_MODEL), rev(D_MODEL), rev(Z_WIDTH), rev(C_CONV), rev(C_POOL),
            _const_spec((Z_WIDTH, D_MODEL)),
            _const_spec((D_MODEL, D_MODEL)),
            _const_spec((1, D_MODEL)),
            _const_spec((CONV_HALO, C_CONV)),
            _const_spec((1, C_CONV)),
            _const_spec((1, C_CONV)),
            _const_spec((len(POOL_WINDOWS), POOL_GROUP, POOL_GROUP)),
            _const_spec((1, C_POOL)),
        ],
        out_specs=[
            rev(D_MODEL), rev(Z_WIDTH),
            _const_spec((8, D_MODEL)),
            _const_spec((CONV_HALO, C_CONV)),
            _const_spec((len(POOL_WINDOWS), POOL_GROUP, POOL_GROUP)),
        ],
        out_shape=[
            jax.ShapeDtypeStruct((s, D_MODEL), F32),
            jax.ShapeDtypeStruct((s, Z_WIDTH), BF16),
            jax.ShapeDtypeStruct((8, D_MODEL), F32),
            jax.ShapeDtypeStruct((CONV_HALO, C_CONV), F32),
            jax.ShapeDtypeStruct((len(POOL_WINDOWS), POOL_GROUP, POOL_GROUP), F32),
        ],
        scratch_shapes=[
            pltpu.VMEM((tm + CONV_HALO, C_CONV), F32),
            pltpu.VMEM((tm + POOL_HALO, C_POOL), F32),
            pltpu.VMEM((tm, C_CONV), F32),
            pltpu.VMEM((tm, C_CONV), F32),
            pltpu.VMEM((7, tm + CONV_HALO, C_CONV), F32),
            pltpu.VMEM((2, tm + POOL_HALO, POOL_GROUP), F32),
        ],
        name="bwd_mix",
        args=(dx1, dx1b, x, z, u1, pooled, w_in_t, w_out, g_mix, conv_w, ln_g, ln_b, pool_w, pool_scale),
        carry=carry,
    )


def _grad_matmul(a, b, bm, name, a2=None, carry=None):
    s, ma = a.shape
    nb = b.shape[1]
    na = ma // bm
    if a2 is None:
        def body(a_ref, b_ref, o_ref):
            o_ref[...] = _dot_tn(a_ref[...], b_ref[...]).astype(BF16)

        lhs_specs = [pl.BlockSpec((s, bm), lambda i: (0, i))]
        lhs = (a,)
        steps = na
    else:
        def body(a_ref, a2_ref, b_ref, o_ref):
            i = pl.program_id(0)

            @pl.when(i < na)
            def _():
                o_ref[...] = _dot_tn(a_ref[...], b_ref[...]).astype(BF16)

            @pl.when(i >= na)
            def _():
                o_ref[...] = _dot_tn(a2_ref[...], b_ref[...]).astype(BF16)

        lhs_specs = [pl.BlockSpec((s, bm), lambda i: (0, jnp.minimum(i, na - 1))),
                     pl.BlockSpec((s, bm), lambda i: (0, jnp.maximum(i - na, 0)))]
        lhs = (a, a2)
        steps = 2 * na

    outs, carried = _pcall(
        body,
        grid=(steps,),
        in_specs=lhs_specs + [pl.BlockSpec((s, nb), lambda i: (0, 0))],
        out_specs=[pl.BlockSpec((bm, nb), lambda i: (i, 0))],
        out_shape=[jax.ShapeDtypeStruct((steps * bm, nb), BF16)],
        scratch_shapes=[],
        name=name,
        args=lhs + (b,),
        carry=carry,
    )
    return outs[0], carried


def _reduce_scatter(grads, small, name):
    n, ns = len(grads), len(small)
    shapes = [g.shape[1:] for g in grads]
    gather = _Carry()
    _carry_gather(gather, small, 1, 2)

    def body(*refs):
        g = refs[:n]
        out = refs[n + ns:2 * n + ns]
        scr = refs[2 * (n + ns):]
        own, loc, r1, r2 = scr[:n], scr[n:2 * n], scr[2 * n:3 * n], scr[3 * n:4 * n]
        load_sems, s1, q1, s2, q2 = scr[4 * n:4 * n + 5]
        gather_refs = (refs[n:n + ns], refs[2 * n + ns:2 * (n + ns)], scr[4 * n + 5:])
        me, sib, chips = _place()
        c = me[2]

        _handshake(EVERY_PEER)
        gather.stage(0, *gather_refs, handshake=False)
        loads = []
        sends = []
        for a in range(n):
            ld = [pltpu.make_async_copy(g[a].at[_block(*chip, c)], loc[a].at[j], load_sems.at[a, j])
                  for j, chip in enumerate(chips)]
            ld.append(pltpu.make_async_copy(g[a].at[_block(*me)], own[a], load_sems.at[a, 3]))
            for cp in ld:
                cp.start()
            loads.append(ld)
            blocks = [(*chip, 1 - c) for chip in chips] + [sib]
            for j, blk in enumerate(blocks):
                cp = pltpu.make_async_remote_copy(
                    src_ref=g[a].at[_block(*blk)], dst_ref=r1[a].at[j],
                    send_sem=s1.at[a, j], recv_sem=q1.at[a, j], device_id=sib, device_id_type=MESH)
                cp.start()
                sends.append(cp)

        def from_sibling(a, j):
            return pltpu.make_async_remote_copy(
                src_ref=r1[a].at[j], dst_ref=r1[a].at[j], send_sem=s1.at[a, j], recv_sem=q1.at[a, j],
                device_id=sib, device_id_type=MESH)

        def partial(a, j, chip):
            return pltpu.make_async_remote_copy(
                src_ref=loc[a].at[j], dst_ref=r2[a].at[j], send_sem=s2.at[a, j], recv_sem=q2.at[a, j],
                device_id=(*chip, c), device_id_type=MESH)

        gather.stage(1, *gather_refs)
        for a in range(n):
            for j, chip in enumerate(chips):
                loads[a][j].wait()
                from_sibling(a, j).wait_recv()
                loc[a][j] = (loc[a][j].astype(F32) + r1[a][j].astype(F32)).astype(BF16)
                cp = partial(a, j, chip)
                cp.start()
                sends.append(cp)
        gather.stage(2, *gather_refs)
        for a in range(n):
            loads[a][3].wait()
            from_sibling(a, 3).wait_recv()
            acc = own[a][...].astype(F32) + r1[a][3].astype(F32)
            for j, chip in enumerate(chips):
                partial(a, j, chip).wait_recv()
                acc = acc + r2[a][j].astype(F32)
            out[a][...] = acc
        for cp in sends:
            cp.wait_send()
        gather.drain(*gather_refs)

    any_spec = pl.BlockSpec(memory_space=pl.ANY)
    vmem_spec = pl.BlockSpec(memory_space=pltpu.VMEM)
    res = pl.pallas_call(
        body,
        in_specs=[any_spec] * (n + ns),
        out_specs=[vmem_spec] * n + [any_spec] * ns,
        out_shape=[jax.ShapeDtypeStruct(sh, F32) for sh in shapes] + gather.out_shapes,
        scratch_shapes=(
            [pltpu.VMEM(sh, BF16) for sh in shapes]
            + [pltpu.VMEM((3,) + sh, BF16) for sh in shapes]
            + [pltpu.VMEM((4,) + sh, BF16) for sh in shapes]
            + [pltpu.VMEM((3,) + sh, BF16) for sh in shapes]
            + [pltpu.SemaphoreType.DMA((n, 4)),
               pltpu.SemaphoreType.DMA((n, 4)), pltpu.SemaphoreType.DMA((n, 4)),
               pltpu.SemaphoreType.DMA((n, 3)), pltpu.SemaphoreType.DMA((n, 3))]
            + gather.sem_shapes()
        ),
        compiler_params=_params(collective_id=_collective_id(EVERY_PEER)),
        name=name,
    )(*grads, *small)
    return res[:n], res[n:]


def _pair_add(grads, from_sib, blks, name):
    n = len(grads)

    def body(blk_ref, *refs):
        for a in range(n):
            refs[2 * n + a][...] = (refs[a][...].astype(F32) + refs[n + a][...].astype(F32)).astype(BF16)

    mine = [pl.BlockSpec((None,) + g.shape[1:], lambda j, b: (b[j], 0, 0)) for g in grads]
    same = [pl.BlockSpec((None,) + g.shape[1:], lambda j, b: (j, 0, 0)) for g in grads]
    return pl.pallas_call(
        body,
        grid_spec=pltpu.PrefetchScalarGridSpec(
            num_scalar_prefetch=1, grid=(4,), in_specs=mine + same, out_specs=same),
        out_shape=[jax.ShapeDtypeStruct((4,) + g.shape[1:], BF16) for g in grads],
        compiler_params=_params(("arbitrary",)),
        name=name,
    )(blks, *grads, *from_sib)


def _chip_sum(parts, from_chips, name):
    n = len(parts)

    def body(*refs):
        for a in range(n):
            acc = refs[a][...].astype(F32)
            for j in range(3):
                acc = acc + refs[n + a][j].astype(F32)
            refs[2 * n + a][...] = acc

    half = [p.shape[1] // 2 for p in parts]
    return pl.pallas_call(
        body,
        grid=(2,),
        in_specs=([pl.BlockSpec((None, h, p.shape[2]), lambda i: (3, i, 0)) for p, h in zip(parts, half)]
                  + [pl.BlockSpec((3, h, p.shape[2]), lambda i: (0, i, 0)) for p, h in zip(parts, half)]),
        out_specs=[pl.BlockSpec((h, p.shape[2]), lambda i: (i, 0)) for p, h in zip(parts, half)],
        out_shape=[jax.ShapeDtypeStruct(p.shape[1:], F32) for p in parts],
        compiler_params=_params(("arbitrary",)),
        name=name,
    )(*parts, *from_chips)


def _adam_math(w, g, m, v):
    nm = ADAM_B1 * m + (1.0 - ADAM_B1) * g
    nv = ADAM_B2 * v + (1.0 - ADAM_B2) * (g * g)
    m_hat = nm / (1.0 - ADAM_B1 ** ADAM_STEP)
    v_hat = nv / (1.0 - ADAM_B2 ** ADAM_STEP)
    return -ADAM_LR * (m_hat / (jnp.sqrt(v_hat) + ADAM_EPS) + ADAM_WD * w), nm, nv


def _sum_adamw(items, name, carry=None):
    n = len(items)

    def body(*refs):
        for a in range(n):
            p_ref, f_ref, w_ref, m_ref, v_ref = refs[5 * a:5 * a + 5]
            g_ref, d_ref, nm_ref, nv_ref = refs[5 * n + 4 * a:5 * n + 4 * a + 4]
            g = p_ref[...].astype(F32)
            for j in range(3):
                g = g + f_ref[j].astype(F32)
            g_ref[...] = g
            d_ref[...], nm_ref[...], nv_ref[...] = _adam_math(w_ref[...], g, m_ref[...], v_ref[...])

    in_specs, out_specs, out_shape, args = [], [], [], []
    for part, from_chips, w, m, v in items:
        r, c = w.shape
        spec = pl.BlockSpec((r // 2, c), lambda i: (i, 0))
        in_specs += [pl.BlockSpec((None, r // 2, c), lambda i: (3, i, 0)),
                     pl.BlockSpec((3, r // 2, c), lambda i: (0, i, 0)), spec, spec, spec]
        out_specs += [spec] * 4
        out_shape += [jax.ShapeDtypeStruct((r, c), F32)] * 4
        args += [part, from_chips, w, m, v]
    outs, carried = _pcall(body, grid=(2,), in_specs=in_specs, out_specs=out_specs, out_shape=out_shape,
                           scratch_shapes=[], name=name, args=tuple(args), carry=carry)
    return [tuple(outs[4 * a:4 * a + 4]) for a in range(n)], carried


def _small_update(gathered, layout, params, name):
    ng, npar = len(gathered), len(params)

    def body(*refs):
        parts = refs[:ng]
        prm = refs[ng:ng + 3 * npar]
        tot_refs = refs[ng + 3 * npar:2 * ng + 3 * npar]
        out = refs[2 * ng + 3 * npar:]
        tots = []
        for a in range(ng):
            acc = parts[a][0]
            for d in range(1, N_DEV):
                acc = acc + parts[a][d]
            tot_refs[a][...] = acc
            tots.append(acc)
        for i, (a, row, width) in enumerate(layout):
            g = tots[a] if row is None else tots[a][row:row + 1, :width]
            delta, nm, nv = _adam_math(prm[3 * i][...], g, prm[3 * i + 1][...], prm[3 * i + 2][...])
            out[4 * i][...] = g
            out[4 * i + 1][...] = delta
            out[4 * i + 2][...] = nm
            out[4 * i + 3][...] = nv

    flat = [t for prm in params for t in prm]
    res = pl.pallas_call(
        body,
        out_shape=([jax.ShapeDtypeStruct(g.shape[1:], F32) for g in gathered]
                   + [jax.ShapeDtypeStruct(prm[0].shape, F32) for prm in params for _ in range(4)]),
        compiler_params=pltpu.CompilerParams(vmem_limit_bytes=V7X_VMEM_LIMIT),
        name=name,
    )(*gathered, *flat)
    return res[:ng], [tuple(res[ng + 4 * i:ng + 4 * i + 4]) for i in range(npar)]


def _adamw(w, g, m, v, name):
    rows, cols = w.shape
    br = rows
    for cand in (512, 256, 128):
        if rows % cand == 0 and rows > cand:
            br = cand
            break

    def body(w_ref, g_ref, m_ref, v_ref, d_ref, nm_ref, nv_ref):
        d_ref[...], nm_ref[...], nv_ref[...] = _adam_math(w_ref[...], g_ref[...], m_ref[...], v_ref[...])

    spec = pl.BlockSpec((br, cols), lambda i: (i, 0))
    shape = jax.ShapeDtypeStruct((rows, cols), F32)
    return pl.pallas_call(
        body,
        grid=(rows // br,),
        in_specs=[spec] * 4,
        out_specs=[spec] * 3,
        out_shape=[shape] * 3,
        compiler_params=_params(("arbitrary",)),
        name=name,
    )(w, g, m, v)


def _by_device(full):
    return full.reshape(N_DEV, full.shape[0] // N_DEV, full.shape[1])


def kernel(x, p, g_mix, w_in, conv_w, conv_b, ln_g, ln_b, pool_w, pool_scale, w_out, g_ffn, w_gate_up, w_down, g_ple_gate, w_ple_gate, w_ple_up, g_ple_post, g_final, loss_target, m_g_mix, m_w_in, m_conv_w, m_conv_b, m_ln_g, m_ln_b, m_pool_w, m_pool_scale, m_w_out, m_g_ffn, m_w_gate_up, m_w_down, m_g_ple_gate, m_w_ple_gate, m_w_ple_up, m_g_ple_post, m_g_final, v_g_mix, v_w_in, v_conv_w, v_conv_b, v_ln_g, v_ln_b, v_pool_w, v_pool_scale, v_w_out, v_g_ffn, v_w_gate_up, v_w_down, v_g_ple_gate, v_w_ple_gate, v_w_ple_up, v_g_ple_post, v_g_final):
    seq = x.shape[1]
    tile = min(TOKEN_TILE, seq)
    xs = x[0]
    ps = p[0, 0]
    tgt = loss_target[0]
    ax, ay, ac = lax.axis_index("x"), lax.axis_index("y"), lax.axis_index("c")
    me = _block(ax, ay, ac)
    blks = jnp.stack([_block(1 - ax, ay, ac), _block(ax, 1 - ay, ac), _block(1 - ax, 1 - ay, ac), me]).astype(jnp.int32)
    rows = lambda gth: gth.reshape((-1,) + gth.shape[2:])

    sh_in, sh_out, sh_gu, sh_down, sh_pg, sh_pu, pool_w_b = _to_bf16([
        w_in[0].T,
        w_out[0],
        w_gate_up[0].T,
        w_down[0],
        w_ple_gate[0],
        w_ple_up[0].T,
        pool_w[0]])
    carry = _Carry()
    _carry_gather(carry, [sh_in, sh_out,
                          jnp.pad(conv_w[0].T, ((0, 0), (0, CONV_HALO - CONV_K)))], 8, 15)
    (h1,), first = _norm_in(xs, g_mix, tile, carry)
    w_in_t, w_out_f, conv_w_t = [rows(gth) for gth in first]
    conv_w_f = conv_w_t.T

    carry = _Carry()
    _carry_gather(carry, [sh_gu], 8, 12)
    (z, u1, pooled, x1, mix), (w_gu_all,) = _fwd_mix(
        xs, h1, w_in_t, w_out_f, conv_w_f, conv_b, ln_g, ln_b, pool_w_b, pool_scale, tile, carry)
    w_gu_t = rows(w_gu_all)

    carry = _Carry()
    _carry_gather(carry, [sh_down, sh_pg, sh_pu], 6, 10)
    (h2, g_sav, u_sav, act), late = _ffn_up(x1, w_gu_t, g_ffn, tile, carry)
    w_down_f, w_pg_f, w_pu_t = [rows(gth) for gth in late]
    dx2, dx2b, hg, ds, dpe, pb, stats_ple = _down_ple(
        x1, act, w_down_f, ps, tgt, w_pg_f, w_pu_t, g_ple_gate, g_ple_post, g_final.reshape(1, D_MODEL),
        tile)

    d_w_down, _ = _grad_matmul(act, dx2b, GRAD_ROWS, "grad_w_down")
    d_w_pg, _ = _grad_matmul(hg, ds, GRAD_ROWS, "grad_w_ple_gate")
    d_w_pu_t, _ = _grad_matmul(dpe, pb, GRAD_ROWS, "grad_w_ple_up")
    early = [_by_device(d_w_pg), _by_device(d_w_pu_t), _by_device(d_w_down)]
    carry = _Carry()
    _carry_pair(carry, early)
    (dg, du, dx1, dx1b, stats_ffn), from_sib = _ffn_bwd(dx2, dx2b, x1, g_sav, u_sav, w_gu_t, w_down_f, g_ffn,
                                                        tile, carry)
    early_parts = _pair_add(early, from_sib, blks, "pair_add_early")

    carry = _Carry()
    _carry_chip(carry, early_parts)
    d_w_gu_t, early_chips = _grad_matmul(dg, h2, GRAD_ROWS, "grad_w_gate_up", a2=du, carry=carry)
    carry = _Carry()
    _carry_pair(carry, [_by_device(d_w_gu_t)])
    d_w_out, gu_sib = _grad_matmul(mix, dx1b, GRAD_ROWS, "grad_w_out", carry=carry)
    gu_parts = _pair_add([_by_device(d_w_gu_t)], gu_sib, blks, "pair_add_gate_up")

    carry = _Carry()
    _carry_chip(carry, gu_parts)
    _carry_pair(carry, [_by_device(d_w_out)])
    (dx, dz, vec_mix, dconv_w_part, dpool_w_part), carried = _bwd_mix(
        dx1, dx1b, xs, z, u1, pooled, w_in_t, w_out_f, g_mix, conv_w_f, ln_g, ln_b, pool_w_b, pool_scale,
        tile, carry)
    gu_chips, out_sib = carried[:1], carried[1:]
    out_parts = _pair_add([_by_device(d_w_out)], out_sib, blks, "pair_add_out")

    carry = _Carry()
    _carry_chip(carry, out_parts)
    d_w_in_t, out_chips = _grad_matmul(dz, h1, GRAD_ROWS, "grad_w_in", carry=carry)
    (gr_w_in_t,), small = _reduce_scatter(
        [_by_device(d_w_in_t)],
        [vec_mix, stats_ple, stats_ffn, dconv_w_part, dpool_w_part.reshape(-1, POOL_GROUP)], "scatter_last")
    small = list(small[:4]) + [small[4].reshape((N_DEV,) + dpool_w_part.shape)]

    vec_names = ["g_mix", "ln_g", "ln_b", "conv_b", "pool_scale", "g_final", "g_ple_post", "g_ple_gate", "g_ffn",
                 "pool_w"]
    layout = [(0, 0, D_MODEL), (0, 1, C_CONV), (0, 2, C_CONV), (0, 3, C_CONV), (0, 4, C_POOL),
              (1, 0, D_MODEL), (1, 1, D_MODEL), (1, 2, D_MODEL), (2, 0, D_MODEL), (4, None, None)]
    as_row = lambda t: t.reshape(1, D_MODEL)
    params = [(g_mix, m_g_mix, v_g_mix), (ln_g, m_ln_g, v_ln_g), (ln_b, m_ln_b, v_ln_b),
              (conv_b, m_conv_b, v_conv_b), (pool_scale, m_pool_scale, v_pool_scale),
              (as_row(g_final), as_row(m_g_final), as_row(v_g_final)),
              (g_ple_post, m_g_ple_post, v_g_ple_post), (g_ple_gate, m_g_ple_gate, v_g_ple_gate),
              (g_ffn, m_g_ffn, v_g_ffn), (pool_w[0], m_pool_w[0], v_pool_w[0])]
    tots, small_upd = _small_update(small, layout, params, "small_update")
    loss = tots[1][3, 0]
    upd = {}
    for nm, res, prm in zip(vec_names, small_upd, [g_mix, ln_g, ln_b, conv_b, pool_scale, g_final, g_ple_post,
                                                    g_ple_gate, g_ffn, pool_w]):
        upd[nm] = tuple(t.reshape(prm.shape) for t in res)
    gr_conv_w = lax.dynamic_slice_in_dim(tots[3][:CONV_K], me * (C_CONV // N_DEV), C_CONV // N_DEV, axis=1)

    natural = lambda t: t[None]
    turned = lambda t: t.T[None]
    (res_pg, res_gu, res_out, res_down), _ = _sum_adamw([
        (early_parts[0], early_chips[0], w_ple_gate[0], m_w_ple_gate[0], v_w_ple_gate[0]),
        (gu_parts[0], gu_chips[0], w_gate_up[0].T, m_w_gate_up[0].T, v_w_gate_up[0].T),
        (out_parts[0], out_chips[0], w_out[0], m_w_out[0], v_w_out[0]),
        (early_parts[2], early_chips[2], w_down[0], m_w_down[0], v_w_down[0])], "adamw_big")
    upd["w_ple_gate"] = tuple(natural(t) for t in res_pg)
    upd["w_gate_up"] = tuple(turned(t) for t in res_gu)
    upd["w_out"] = tuple(natural(t) for t in res_out)
    upd["w_down"] = tuple(natural(t) for t in res_down)
    (gr_w_pu_t,) = _chip_sum(early_parts[1:2], early_chips[1:2], "chip_sum")
    plain = [
        ("w_in", w_in[0].T, gr_w_in_t, m_w_in[0].T, v_w_in[0].T, True),
        ("w_ple_up", w_ple_up[0], gr_w_pu_t.T, m_w_ple_up[0], v_w_ple_up[0], False),
        ("conv_w", conv_w[0], gr_conv_w, m_conv_w[0], v_conv_w[0], False),
    ]
    for nm, w_, g_, m_, v_, transposed in plain:
        res = (g_,) + tuple(_adamw(w_, g_, m_, v_, "adamw_" + nm))
        upd[nm] = tuple((t.T if transposed else t)[None] for t in res)

    order = ["g_mix", "w_in", "conv_w", "conv_b", "ln_g", "ln_b", "pool_w", "pool_scale", "w_out", "g_ffn",
             "w_gate_up", "w_down", "g_ple_gate", "w_ple_gate", "w_ple_up", "g_ple_post", "g_final"]
    outs = [loss, dx[None]]
    for k in range(4):
        outs += [upd[nm][k] for nm in order]
    return tuple(outs)
```

```python
import functools

import jax
import jax.numpy as jnp
from jax import lax
from jax.experimental import pallas as pl
from jax.experimental.pallas import tpu as pltpu

D_MODEL = 1024
C_CONV = 512
C_POOL = 512
Z_WIDTH = 2 * C_CONV + C_POOL
POOL_WINDOWS = (2, 4, 8, 16)
POOL_GROUP = 128
CONV_K = 31
D_FF = 2816
D_PLE = 256
EPS = 1e-6
N_DEV = 8

ADAM_LR = 0.001
ADAM_B1 = 0.9
ADAM_B2 = 0.999
ADAM_EPS = 1e-08
ADAM_WD = 0.01
ADAM_STEP = 10

CONV_HALO = 32
POOL_HALO = 32
ROW_CHUNK = 32
TOKEN_TILE = 256
GRAD_ROWS = 256
V7X_VMEM_LIMIT = 56 * 1024 * 1024
FF_CHUNK = D_FF // 2

BF16 = jnp.bfloat16
F32 = jnp.float32
MESH = pl.DeviceIdType.MESH


def _dot(a, b):
    return lax.dot_general(a, b, (((1,), (0,)), ((), ())), preferred_element_type=F32)


def _dot_nt(a, b):
    return lax.dot_general(a, b, (((1,), (1,)), ((), ())), preferred_element_type=F32)


def _dot_tn(a, b):
    return lax.dot_general(a, b, (((0,), (0,)), ((), ())), preferred_element_type=F32)


def _rms_fwd(x, g):
    r = lax.rsqrt(jnp.mean(x * x, axis=-1, keepdims=True) + EPS)
    return x * r * g, r


def _rms_bwd(x, r, g, dy):
    xr = x * r
    dg = jnp.sum(dy * xr, axis=0, keepdims=True)
    dyg = dy * g
    dx = r * (dyg - xr * jnp.mean(dyg * xr, axis=-1, keepdims=True))
    return dx, dg


def _sigmoid(x):
    return jax.nn.sigmoid(x)


def _params(sem=None, collective_id=None):
    return pltpu.CompilerParams(dimension_semantics=sem, vmem_limit_bytes=V7X_VMEM_LIMIT,
                                collective_id=collective_id)


def _place():
    x, y, c = lax.axis_index("x"), lax.axis_index("y"), lax.axis_index("c")
    chips = [(1 - x, y), (x, 1 - y), (1 - x, 1 - y)]
    return (x, y, c), (x, y, 1 - c), chips


def _block(px, py, pc):
    return 4 * px + 2 * py + pc


EVERY_PEER = ("sib", 0, 1, 2)
PEER_SET_IDS = {("sib",): 0, ("0", "1", "sib"): 1, ("0", "1", "2"): 2, ("0", "1", "2", "sib"): 3}


def _collective_id(peers):
    return PEER_SET_IDS[tuple(sorted(str(p) for p in peers))]


def _handshake(peers):
    me, sib, chips = _place()
    barrier = pltpu.get_barrier_semaphore()
    for p in peers:
        to = sib if p == "sib" else (*chips[p], me[2])
        pl.semaphore_signal(barrier, inc=1, device_id=to, device_id_type=MESH)
    pl.semaphore_wait(barrier, len(peers))


class _Carry:
    def __init__(self):
        self.inputs = []
        self.out_shapes = []
        self.copies = []
        self.locals = []

    def add_input(self, arr):
        self.inputs.append(arr)
        return len(self.inputs) - 1

    def add_output(self, shape, dtype):
        self.out_shapes.append(jax.ShapeDtypeStruct(shape, dtype))
        return len(self.out_shapes) - 1

    def local(self, src_idx, dst_idx, dst_blk):
        self.locals.append((src_idx, dst_idx, dst_blk))

    def copy(self, src, dst_idx, dst_blk, got_blk, peer, step=0, after=()):
        self.copies.append(dict(src=src, dst_idx=dst_idx, dst_blk=dst_blk, got_blk=got_blk, peer=peer, step=step,
                                after=tuple(after)))
        return len(self.copies) - 1

    def sem_shapes(self):
        return [pltpu.SemaphoreType.DMA((max(1, len(self.copies)),)),
                pltpu.SemaphoreType.DMA((max(1, len(self.copies)),)),
                pltpu.SemaphoreType.DMA((max(1, len(self.locals)),))]

    @staticmethod
    def _view(ref, where):
        if isinstance(where, tuple):
            blk, row0, nrows = where
            return ref.at[blk, pl.ds(row0, nrows)]
        return ref.at[where]

    def _desc(self, k, ins, outs, sems, place):
        cp = self.copies[k]
        me, sib, chips = place
        kind, idx, blk = cp["src"]
        src = (ins if kind == "in" else outs)[idx]
        if blk is not None:
            src = self._view(src, blk(*place))
        to = sib if cp["peer"] == "sib" else (*chips[cp["peer"]], me[2])
        return pltpu.make_async_remote_copy(
            src_ref=src, dst_ref=self._view(outs[cp["dst_idx"]], cp["dst_blk"](*place)),
            send_sem=sems[0].at[k], recv_sem=sems[1].at[k], device_id=to, device_id_type=MESH)

    def _arrival(self, k, outs, sems, place):
        cp = self.copies[k]
        got = self._view(outs[cp["dst_idx"]], cp["got_blk"](*place))
        return pltpu.make_async_remote_copy(
            src_ref=got, dst_ref=got, send_sem=sems[0].at[k], recv_sem=sems[1].at[k],
            device_id=place[0], device_id_type=MESH)

    def _local(self, n, ins, outs, sems, place):
        src_idx, dst_idx, blk = self.locals[n]
        return pltpu.make_async_copy(ins[src_idx], outs[dst_idx].at[blk(*place)], sems[2].at[n])

    def stages(self):
        return sorted({0} | {cp["step"] for cp in self.copies})

    def peers(self):
        return sorted({cp["peer"] for cp in self.copies}, key=str)

    def stage(self, s, ins, outs, sems, handshake=True):
        place = _place()
        if s == 0:
            if handshake:
                _handshake(self.peers())
            self._waited = set()
            for n in range(len(self.locals)):
                self._local(n, ins, outs, sems, place).start()
        for k, cp in enumerate(self.copies):
            if cp["step"] != s:
                continue
            for a in cp["after"]:
                if a not in self._waited:
                    self._arrival(a, outs, sems, place).wait_recv()
                    self._waited.add(a)
            self._desc(k, ins, outs, sems, place).start()

    def drain(self, ins, outs, sems):
        place = _place()
        for k in range(len(self.copies)):
            if k not in self._waited:
                self._arrival(k, outs, sems, place).wait_recv()
        for k in range(len(self.copies)):
            self._desc(k, ins, outs, sems, place).wait_send()
        for n in range(len(self.locals)):
            self._local(n, ins, outs, sems, place).wait()

    def starts(self, step, nsteps, ins, outs, sems):
        for s in self.stages():
            pl.when(step == min(s, nsteps - 1))(functools.partial(self.stage, s, ins, outs, sems))

    def finish(self, step, nsteps, ins, outs, sems):
        pl.when(step == nsteps - 1)(functools.partial(self.drain, ins, outs, sems))


def _const_blk(j):
    return lambda me, sib, chips: j


def _carry_gather(carry, shards, relay_step, last_step):
    outs = []
    for sh in shards:
        i = carry.add_input(sh)
        o = carry.add_output((N_DEV,) + sh.shape, sh.dtype)
        half = sh.shape[0] // 2
        tile = 16 if sh.dtype == BF16 else 8
        split = half % tile == 0
        rows = [(0, half), (half, sh.shape[0] - half)] if split else [(0, sh.shape[0]), None]

        def whole(j, core):
            return lambda me, sib, chips, j=j, core=core: _block(*chips[j], me[2] if core == 0 else 1 - me[2])

        def part(j, core, h, rows=rows):
            return lambda me, sib, chips: (_block(*chips[j], me[2] if core == 0 else 1 - me[2]),) + rows[h]

        mine = lambda me, sib, chips: _block(*me)
        carry.local(i, o, mine)
        carry.copy(("in", i, None), o, mine, lambda me, sib, chips: _block(*sib), "sib")
        near = [carry.copy(("in", i, None), o, mine, whole(j, 0), j) for j in range(2)]
        for j in range(2):
            carry.copy(("out", o, whole(j, 0)), o, whole(j, 0), whole(j, 1), "sib", step=relay_step, after=(near[j],))
        for j in range(2):
            if rows[j] is None:
                continue
            far = carry.copy(("out", o, part(j, 0, j)), o, part(j, 0, j), part(2, 0, j), 1 - j,
                             step=relay_step, after=(near[j],))
            carry.copy(("out", o, part(2, 0, j)), o, part(2, 0, j), part(2, 1, j), "sib", step=last_step, after=(far,))
        outs.append(o)
    return outs


def _carry_pair(carry, grads):
    outs = []
    for g in grads:
        i = carry.add_input(g)
        o = carry.add_output((4,) + g.shape[1:], g.dtype)
        for j in range(4):
            if j < 3:
                blk = lambda me, sib, chips, j=j: _block(*chips[j], 1 - me[2])
            else:
                blk = lambda me, sib, chips: _block(*sib)
            carry.copy(("in", i, blk), o, _const_blk(j), _const_blk(j), "sib")
        outs.append(o)
    return outs


def _carry_chip(carry, parts):
    outs = []
    for p in parts:
        i = carry.add_input(p)
        o = carry.add_output((3,) + p.shape[1:], p.dtype)
        for j in range(3):
            carry.copy(("in", i, _const_blk(j)), o, _const_blk(j), _const_blk(j), j)
        outs.append(o)
    return outs


def _pcall(body, *, grid, in_specs, out_specs, out_shape, scratch_shapes, name, args, carry=None):
    sem = ("arbitrary",) * len(grid)
    if carry is None:
        res = pl.pallas_call(body, grid=grid, in_specs=in_specs, out_specs=out_specs, out_shape=out_shape,
                             scratch_shapes=scratch_shapes, compiler_params=_params(sem), name=name)(*args)
        return list(res), []
    n_in, n_out, n_scr = len(in_specs), len(out_specs), len(scratch_shapes)
    c_in, c_out = len(carry.inputs), len(carry.out_shapes)
    nsteps = 1
    for extent in grid:
        nsteps *= extent

    def wrapped(*refs):
        ins = refs[:n_in]
        cins = refs[n_in:n_in + c_in]
        o0 = n_in + c_in
        outs = refs[o0:o0 + n_out]
        couts = refs[o0 + n_out:o0 + n_out + c_out]
        s0 = o0 + n_out + c_out
        scr = refs[s0:s0 + n_scr]
        sems = refs[s0 + n_scr:]
        step = pl.program_id(0)
        for d in range(1, len(grid)):
            step = step * grid[d] + pl.program_id(d)
        carry.starts(step, nsteps, cins, couts, sems)
        body(*ins, *outs, *scr)
        carry.finish(step, nsteps, cins, couts, sems)

    any_spec = pl.BlockSpec(memory_space=pl.ANY)
    res = pl.pallas_call(
        wrapped, grid=grid,
        in_specs=list(in_specs) + [any_spec] * c_in,
        out_specs=list(out_specs) + [any_spec] * c_out,
        out_shape=list(out_shape) + carry.out_shapes,
        scratch_shapes=list(scratch_shapes) + carry.sem_shapes(),
        compiler_params=_params(sem, _collective_id(carry.peers())), name=name)(*args, *carry.inputs)
    return list(res[:n_out]), list(res[n_out:])


def _shifted_copies(buf, shifted, tm):
    span = tm + CONV_HALO - 8
    for r in range(1, 8):
        shifted[r - 1, 0:span, :] = buf[r:r + span, :]


def _rows_at(buf, shifted, start):
    aligned, r = (start // 8) * 8, start % 8
    if r == 0:
        return buf[aligned:aligned + ROW_CHUNK, :]
    return shifted[r - 1, aligned:aligned + ROW_CHUNK, :]


def _window_sums(buf, cols, work, levels, tm, trailing):
    src, src_cols = buf, cols
    for k in range(levels + 1):
        shift = 1 << k
        dst = work.at[k % 2]
        if trailing:
            lo = 8 * (k + 1)
            dst[lo:tm + POOL_HALO, :] = (src[lo:tm + POOL_HALO, src_cols]
                                         + src[lo - shift:tm + POOL_HALO - shift, src_cols])
        else:
            hi = tm + POOL_HALO - 8 * (k + 1)
            dst[0:hi, :] = src[0:hi, src_cols] + src[shift:hi + shift, src_cols]
        src, src_cols = dst, slice(0, POOL_GROUP)
    return src[POOL_HALO:POOL_HALO + tm, src_cols] if trailing else src[0:tm, src_cols]


def _row_spec(tm, width):
    return pl.BlockSpec((tm, width), lambda i: (i, 0))


def _const_spec(shape):
    return pl.BlockSpec(shape, lambda i: (0,) * len(shape))


def _weight_spec(shape):
    return pl.BlockSpec(shape, lambda i: (0,) * len(shape), pipeline_mode=pl.Buffered(1))


def _to_bf16(arrays):
    n = len(arrays)

    def body(*refs):
        for a in range(n):
            refs[n + a][...] = refs[a][...].astype(BF16)

    whole = [pl.BlockSpec(arr.shape, lambda i, nd=arr.ndim: (0,) * nd) for arr in arrays]
    return pl.pallas_call(
        body,
        grid=(1,),
        in_specs=whole,
        out_specs=whole,
        out_shape=[jax.ShapeDtypeStruct(arr.shape, BF16) for arr in arrays],
        compiler_params=_params(("arbitrary",)),
        name="to_bf16",
    )(*arrays)


def _norm_in(x, g_mix, tm, carry=None):
    s = x.shape[0]

    def body(x_ref, gmix_ref, h1_ref):
        h, _ = _rms_fwd(x_ref[...], gmix_ref[...])
        h1_ref[...] = h.astype(BF16)

    return _pcall(
        body,
        grid=(s // tm,),
        in_specs=[_row_spec(tm, D_MODEL), _const_spec((1, D_MODEL))],
        out_specs=[_row_spec(tm, D_MODEL)],
        out_shape=[jax.ShapeDtypeStruct((s, D_MODEL), BF16)],
        scratch_shapes=[],
        name="norm_in",
        args=(x, g_mix),
        carry=carry,
    )


def _fwd_mix(x, h1, w_in_t, w_out, conv_w, conv_b, ln_g, ln_b, pool_w, pool_scale, tm, carry=None):
    s = x.shape[0]
    nt = s // tm

    def body(x_ref, h1_ref, win_ref, wout_ref, cw_ref, cb_ref, lng_ref, lnb_ref, pw_ref, ps_ref,
             z_ref, u1_ref, pooled_ref, x1_ref, mix_ref, ubuf, vbuf, ush, pbuf):
        i = pl.program_id(0)

        @pl.when(i == 0)
        def _():
            ubuf[0:CONV_HALO, :] = jnp.zeros((CONV_HALO, C_CONV), F32)
            vbuf[0:POOL_HALO, :] = jnp.zeros((POOL_HALO, C_POOL), F32)

        xt = x_ref[...]
        z = _dot_nt(h1_ref[...], win_ref[...])
        z_ref[...] = z
        a = z[:, :C_CONV]
        b = z[:, C_CONV:2 * C_CONV]
        v = z[:, 2 * C_CONV:]
        ubuf[CONV_HALO:CONV_HALO + tm, :] = a * _sigmoid(b)
        vbuf[POOL_HALO:POOL_HALO + tm, :] = v

        _shifted_copies(ubuf, ush, tm)
        for rc in range(tm // ROW_CHUNK):
            base = rc * ROW_CHUNK + CONV_HALO - (CONV_K - 1)
            acc = jnp.broadcast_to(cb_ref[...], (ROW_CHUNK, C_CONV))
            for k in range(CONV_K):
                acc = acc + cw_ref[k:k + 1, :] * _rows_at(ubuf, ush, base + k)
            u1_ref[rc * ROW_CHUNK:(rc + 1) * ROW_CHUNK, :] = acc

        u1 = u1_ref[...]
        mu = jnp.mean(u1, axis=-1, keepdims=True)
        cen = u1 - mu
        rstd = lax.rsqrt(jnp.mean(cen * cen, axis=-1, keepdims=True) + EPS)
        u2 = cen * rstd * lng_ref[...] + lnb_ref[...]
        u = u2 * _sigmoid(u2)

        pos1 = (i * tm + lax.broadcasted_iota(jnp.int32, (tm, 1), 0) + 1).astype(F32)
        parts = [u]
        for g, w in enumerate(POOL_WINDOWS):
            cols = slice(g * POOL_GROUP, (g + 1) * POOL_GROUP)
            vg = v[:, cols]
            tot = _window_sums(vbuf, cols, pbuf, g, tm, trailing=True)
            pooled = tot / jnp.minimum(pos1, float(w)) - vg
            pooled_b = pooled.astype(BF16)
            pooled_ref[:, cols] = pooled_b
            parts.append(_dot(pooled_b, pw_ref[g]) * ps_ref[:, cols])
        mix = jnp.concatenate(parts, axis=-1).astype(BF16)
        mix_ref[...] = mix
        x1_ref[...] = xt + _dot(mix, wout_ref[...])

        ubuf[0:CONV_HALO, :] = ubuf[tm:tm + CONV_HALO, :]
        vbuf[0:POOL_HALO, :] = vbuf[tm:tm + POOL_HALO, :]

    return _pcall(
        body,
        grid=(nt,),
        in_specs=[
            _row_spec(tm, D_MODEL),
            _row_spec(tm, D_MODEL),
            _const_spec((Z_WIDTH, D_MODEL)),
            _const_spec((D_MODEL, D_MODEL)),
            _const_spec((CONV_HALO, C_CONV)),
            _const_spec((1, C_CONV)),
            _const_spec((1, C_CONV)),
            _const_spec((1, C_CONV)),
            _const_spec((len(POOL_WINDOWS), POOL_GROUP, POOL_GROUP)),
            _const_spec((1, C_POOL)),
        ],
        out_specs=[
            _row_spec(tm, Z_WIDTH),
            _row_spec(tm, C_CONV),
            _row_spec(tm, C_POOL),
            _row_spec(tm, D_MODEL),
            _row_spec(tm, D_MODEL),
        ],
        out_shape=[
            jax.ShapeDtypeStruct((s, Z_WIDTH), F32),
            jax.ShapeDtypeStruct((s, C_CONV), F32),
            jax.ShapeDtypeStruct((s, C_POOL), BF16),
            jax.ShapeDtypeStruct((s, D_MODEL), F32),
            jax.ShapeDtypeStruct((s, D_MODEL), BF16),
        ],
        scratch_shapes=[
            pltpu.VMEM((tm + CONV_HALO, C_CONV), F32),
            pltpu.VMEM((tm + POOL_HALO, C_POOL), F32),
            pltpu.VMEM((7, tm + CONV_HALO, C_CONV), F32),
            pltpu.VMEM((2, tm + POOL_HALO, POOL_GROUP), F32),
        ],
        name="fwd_mix",
        args=(x, h1, w_in_t, w_out, conv_w, conv_b, ln_g, ln_b, pool_w, pool_scale),
        carry=carry,
    )


def _ffn_up(x1, w_gu_t, g_ffn, tm, carry=None):
    s = x1.shape[0]

    def body(x1_ref, w_ref, gffn_ref, h2_ref, g_ref, u_ref, act_ref):
        h, _ = _rms_fwd(x1_ref[...], gffn_ref[...])
        h2 = h.astype(BF16)
        h2_ref[...] = h2
        for c in range(D_FF // FF_CHUNK):
            cols = slice(c * FF_CHUNK, (c + 1) * FF_CHUNK)
            g = _dot_nt(h2, w_ref[c * FF_CHUNK:(c + 1) * FF_CHUNK, :])
            u = _dot_nt(h2, w_ref[D_FF + c * FF_CHUNK:D_FF + (c + 1) * FF_CHUNK, :])
            g_ref[:, cols] = g.astype(BF16)
            u_ref[:, cols] = u.astype(BF16)
            act_ref[:, cols] = (g * _sigmoid(g) * u).astype(BF16)

    return _pcall(
        body,
        grid=(s // tm,),
        in_specs=[_row_spec(tm, D_MODEL), _weight_spec((2 * D_FF, D_MODEL)), _const_spec((1, D_MODEL))],
        out_specs=[_row_spec(tm, D_MODEL), _row_spec(tm, D_FF), _row_spec(tm, D_FF), _row_spec(tm, D_FF)],
        out_shape=[
            jax.ShapeDtypeStruct((s, D_MODEL), BF16),
            jax.ShapeDtypeStruct((s, D_FF), BF16),
            jax.ShapeDtypeStruct((s, D_FF), BF16),
            jax.ShapeDtypeStruct((s, D_FF), BF16),
        ],
        scratch_shapes=[],
        name="ffn_up",
        args=(x1, w_gu_t, g_ffn),
        carry=carry,
    )


def _down_ple(x1, act, w_down, p, tgt, w_pg, w_pu_t, g_gate, g_post, g_final, tm):
    s = x1.shape[0]
    nt = s // tm

    def body(x1_ref, act_ref, wd_ref, p_ref, t_ref, wpg_ref, wpu_ref, gg_ref, gp_ref, gf_ref,
             dx2_ref, dx2b_ref, hg_ref, ds_ref, dpe_ref, pb_ref, stats_ref, x2_cur, x2_next):
        i = pl.program_id(0)

        @pl.when(i == 0)
        def _():
            stats_ref[...] = jnp.zeros_like(stats_ref)
            x2_cur[...] = jnp.zeros((tm, D_MODEL), F32)

        def down(c):
            cols = slice(c * 256, (c + 1) * 256)
            x2_next[:, cols] = x1_ref[:, cols] + _dot(act_ref[...], wd_ref[:, cols])

        x2 = x2_cur[...]
        counts = i >= 1

        hg, rg = _rms_fwd(x2, gg_ref[...])
        hg_b = hg.astype(BF16)
        hg_ref[...] = hg_b
        down(0)
        gate = _sigmoid(_dot(hg_b, wpg_ref[...]))
        pb = p_ref[...].astype(BF16)
        pb_ref[...] = pb
        pe = _dot_nt(pb, wpu_ref[...])
        e, rp = _rms_fwd(pe, gp_ref[...])
        x3 = x2 + gate * e
        down(1)
        y, r3 = _rms_fwd(x3, gf_ref[...])
        diff = y - t_ref[...]
        loss = 0.5 * jnp.sum(jnp.sum(diff * diff, axis=-1, keepdims=True), axis=0, keepdims=True) / D_MODEL
        dy = diff * (1.0 / D_MODEL)

        dx3, dgf = _rms_bwd(x3, r3, gf_ref[...], dy)
        down(2)
        dpe, dgp = _rms_bwd(pe, rp, gp_ref[...], dx3 * gate)
        dpe_ref[...] = dpe.astype(BF16)
        ds = (dx3 * e * gate * (1.0 - gate)).astype(BF16)
        ds_ref[...] = ds
        dhg = _dot_nt(ds, wpg_ref[...])
        down(3)
        dxg, dgg = _rms_bwd(x2, rg, gg_ref[...], dhg)
        dx2 = dx3 + dxg
        dx2_ref[...] = dx2
        dx2b_ref[...] = dx2.astype(BF16)
        x2_cur[...] = x2_next[...]

        stats_ref[0:1, :] += jnp.where(counts, dgf, 0.0)
        stats_ref[1:2, :] += jnp.where(counts, dgp, 0.0)
        stats_ref[2:3, :] += jnp.where(counts, dgg, 0.0)
        stats_ref[3:4, :] += jnp.where(counts, jnp.broadcast_to(loss, (1, D_MODEL)), 0.0)

    ahead = lambda width: pl.BlockSpec((tm, width), lambda i: (jnp.minimum(i, nt - 1), 0))
    behind = lambda width: pl.BlockSpec((tm, width), lambda i: (jnp.maximum(i - 1, 0), 0))
    return pl.pallas_call(
        body,
        grid=(nt + 1,),
        in_specs=[
            ahead(D_MODEL),
            ahead(D_FF),
            _weight_spec((D_FF, D_MODEL)),
            behind(D_PLE),
            behind(D_MODEL),
            _weight_spec((D_MODEL, D_MODEL)),
            _weight_spec((D_MODEL, D_PLE)),
            _const_spec((1, D_MODEL)),
            _const_spec((1, D_MODEL)),
            _const_spec((1, D_MODEL)),
        ],
        out_specs=[
            behind(D_MODEL),
            behind(D_MODEL),
            behind(D_MODEL),
            behind(D_MODEL),
            behind(D_MODEL),
            behind(D_PLE),
            _const_spec((8, D_MODEL)),
        ],
        out_shape=[
            jax.ShapeDtypeStruct((s, D_MODEL), F32),
            jax.ShapeDtypeStruct((s, D_MODEL), BF16),
            jax.ShapeDtypeStruct((s, D_MODEL), BF16),
            jax.ShapeDtypeStruct((s, D_MODEL), BF16),
            jax.ShapeDtypeStruct((s, D_MODEL), BF16),
            jax.ShapeDtypeStruct((s, D_PLE), BF16),
            jax.ShapeDtypeStruct((8, D_MODEL), F32),
        ],
        scratch_shapes=[pltpu.VMEM((tm, D_MODEL), F32), pltpu.VMEM((tm, D_MODEL), F32)],
        compiler_params=_params(("arbitrary",)),
        name="down_ple",
    )(x1, act, w_down, p, tgt, w_pg, w_pu_t, g_gate, g_post, g_final)


def _ffn_bwd(dx2, dx2b, x1, g_sav, u_sav, w_gu_t, w_down, g_ffn, tm, carry=None):
    s = x1.shape[0]

    def body(dx2_ref, dx2b_ref, x1_ref, g_ref, u_ref, w_ref, wd_ref, gffn_ref,
             dg_ref, du_ref, dx1_ref, dx1b_ref, stats_ref):
        @pl.when(pl.program_id(0) == 0)
        def _():
            stats_ref[...] = jnp.zeros_like(stats_ref)

        dx2b = dx2b_ref[...]
        nc = D_FF // FF_CHUNK
        dacts = [_dot_nt(dx2b, wd_ref[c * FF_CHUNK:(c + 1) * FF_CHUNK, :]) for c in range(nc)]
        dh2 = jnp.zeros((tm, D_MODEL), F32)
        for c in range(nc):
            cols = slice(c * FF_CHUNK, (c + 1) * FF_CHUNK)
            g = g_ref[:, cols].astype(F32)
            u = u_ref[:, cols].astype(F32)
            sg = _sigmoid(g)
            dg = (dacts[c] * u * sg * (1.0 + g * (1.0 - sg))).astype(BF16)
            du = (dacts[c] * g * sg).astype(BF16)
            dg_ref[:, cols] = dg
            du_ref[:, cols] = du
            dh2 = dh2 + _dot(dg, w_ref[c * FF_CHUNK:(c + 1) * FF_CHUNK, :])
            dh2 = dh2 + _dot(du, w_ref[D_FF + c * FF_CHUNK:D_FF + (c + 1) * FF_CHUNK, :])

        x1 = x1_ref[...]
        r2 = lax.rsqrt(jnp.mean(x1 * x1, axis=-1, keepdims=True) + EPS)
        dxn, dgf = _rms_bwd(x1, r2, gffn_ref[...], dh2)
        dx1 = dx2_ref[...] + dxn
        dx1_ref[...] = dx1
        dx1b_ref[...] = dx1.astype(BF16)
        stats_ref[0:1, :] += dgf

    return _pcall(
        body,
        grid=(s // tm,),
        in_specs=[
            _row_spec(tm, D_MODEL), _row_spec(tm, D_MODEL), _row_spec(tm, D_MODEL),
            _row_spec(tm, D_FF), _row_spec(tm, D_FF),
            _weight_spec((2 * D_FF, D_MODEL)), _weight_spec((D_FF, D_MODEL)), _const_spec((1, D_MODEL)),
        ],
        out_specs=[_row_spec(tm, D_FF), _row_spec(tm, D_FF), _row_spec(tm, D_MODEL), _row_spec(tm, D_MODEL),
                   _const_spec((8, D_MODEL))],
        out_shape=[
            jax.ShapeDtypeStruct((s, D_FF), BF16),
            jax.ShapeDtypeStruct((s, D_FF), BF16),
            jax.ShapeDtypeStruct((s, D_MODEL), F32),
            jax.ShapeDtypeStruct((s, D_MODEL), BF16),
            jax.ShapeDtypeStruct((8, D_MODEL), F32),
        ],
        scratch_shapes=[],
        name="ffn_bwd",
        args=(dx2, dx2b, x1, g_sav, u_sav, w_gu_t, w_down, g_ffn),
        carry=carry,
    )


def _bwd_mix(dx1, dx1b, x, z, u1, pooled, w_in_t, w_out, g_mix, conv_w, ln_g, ln_b, pool_w, pool_scale, tm,
             carry=None):
    s = x.shape[0]
    nt = s // tm

    def body(dx1_ref, dx1b_ref, x_ref, z_ref, u1_ref, pooled_ref, win_ref, wout_ref, gmix_ref, cw_ref,
             lng_ref, lnb_ref, pw_ref, ps_ref,
             dx_ref, dz_ref, vec_ref, dcw_ref, dpw_ref, dubuf, dvbuf, u0buf, du0buf, dush, pbuf):
        i = pl.program_id(0)
        tile = nt - 1 - i

        @pl.when(i == 0)
        def _():
            vec_ref[...] = jnp.zeros_like(vec_ref)
            dcw_ref[...] = jnp.zeros_like(dcw_ref)
            dpw_ref[...] = jnp.zeros_like(dpw_ref)
            dubuf[tm:tm + CONV_HALO, :] = jnp.zeros((CONV_HALO, C_CONV), F32)
            dvbuf[tm:tm + POOL_HALO, :] = jnp.zeros((POOL_HALO, C_POOL), F32)

        dmix = _dot_nt(dx1b_ref[...], wout_ref[...])
        du = dmix[:, :C_CONV]
        dq = dmix[:, C_CONV:]

        pos1 = (tile * tm + lax.broadcasted_iota(jnp.int32, (tm, 1), 0) + 1).astype(F32)
        dpooled_parts = []
        dps_rows = []
        for g, w in enumerate(POOL_WINDOWS):
            cols = slice(g * POOL_GROUP, (g + 1) * POOL_GROUP)
            pooled_b = pooled_ref[:, cols]
            mixed = _dot(pooled_b, pw_ref[g])
            dqg = dq[:, cols]
            dps_rows.append(jnp.sum(dqg * mixed, axis=0, keepdims=True))
            dmixed = (dqg * ps_ref[:, cols]).astype(BF16)
            dpw_ref[g] += _dot_tn(pooled_b, dmixed)
            dpooled = _dot_nt(dmixed, pw_ref[g])
            dpooled_parts.append(dpooled)
            dvbuf[0:tm, cols] = dpooled / jnp.minimum(pos1, float(w))
        vec_ref[4:5, 0:C_POOL] += jnp.concatenate(dps_rows, axis=-1)
        dv_parts = []
        for g, w in enumerate(POOL_WINDOWS):
            cols = slice(g * POOL_GROUP, (g + 1) * POOL_GROUP)
            tot = _window_sums(dvbuf, cols, pbuf, g, tm, trailing=False)
            dv_parts.append(tot - dpooled_parts[g])

        u1 = u1_ref[...]
        mu = jnp.mean(u1, axis=-1, keepdims=True)
        cen = u1 - mu
        rstd = lax.rsqrt(jnp.mean(cen * cen, axis=-1, keepdims=True) + EPS)
        xhat = cen * rstd
        u2 = xhat * lng_ref[...] + lnb_ref[...]
        sg2 = _sigmoid(u2)
        du2 = du * sg2 * (1.0 + u2 * (1.0 - sg2))
        vec_ref[1:2, 0:C_CONV] += jnp.sum(du2 * xhat, axis=0, keepdims=True)
        vec_ref[2:3, 0:C_CONV] += jnp.sum(du2, axis=0, keepdims=True)
        t1 = du2 * lng_ref[...]
        du1 = rstd * (t1 - jnp.mean(t1, axis=-1, keepdims=True)
                      - xhat * jnp.mean(t1 * xhat, axis=-1, keepdims=True))
        vec_ref[3:4, 0:C_CONV] += jnp.sum(du1, axis=0, keepdims=True)
        dubuf[0:tm, :] = du1

        zt = z_ref[...]
        a = zt[:, :C_CONV]
        sgb = _sigmoid(zt[:, C_CONV:2 * C_CONV])
        u0buf[...] = a * sgb

        _shifted_copies(dubuf, dush, tm)
        for rc in range(tm // ROW_CHUNK):
            r0 = rc * ROW_CHUNK
            acc = jnp.zeros((ROW_CHUNK, C_CONV), F32)
            for k in range(CONV_K):
                acc = acc + cw_ref[k:k + 1, :] * _rows_at(dubuf, dush, r0 + (CONV_K - 1) - k)
            du0buf[r0:r0 + ROW_CHUNK, :] = acc
        for k in range(CONV_K):
            acc = jnp.zeros((ROW_CHUNK, C_CONV), F32)
            for rc in range(tm // ROW_CHUNK):
                r0 = rc * ROW_CHUNK
                acc = acc + u0buf[r0:r0 + ROW_CHUNK, :] * _rows_at(dubuf, dush, r0 + (CONV_K - 1) - k)
            dcw_ref[k:k + 1, :] += jnp.sum(acc, axis=0, keepdims=True)
        du0 = du0buf[...]

        da = du0 * sgb
        db = du0 * a * sgb * (1.0 - sgb)
        dz = jnp.concatenate([da, db] + dv_parts, axis=-1).astype(BF16)
        dz_ref[...] = dz

        xt = x_ref[...]
        r1 = lax.rsqrt(jnp.mean(xt * xt, axis=-1, keepdims=True) + EPS)
        dh1 = _dot(dz, win_ref[...])
        dxn, dgm = _rms_bwd(xt, r1, gmix_ref[...], dh1)
        dx_ref[...] = dx1_ref[...] + dxn
        vec_ref[0:1, :] += dgm

        dubuf[tm:tm + CONV_HALO, :] = dubuf[0:CONV_HALO, :]
        dvbuf[tm:tm + POOL_HALO, :] = dvbuf[0:POOL_HALO, :]

    rev = lambda width: pl.BlockSpec((tm, width), lambda i: (nt - 1 - i, 0))
    return _pcall(
        body,
        grid=(nt,),
        in_specs=[
            rev(D_MODEL), rev(D_MODEL), rev(D_MODEL), rev(Z_WIDTH), rev(C_CONV), rev(C_POOL),
            _const_spec((Z_WIDTH, D_MODEL)),
            _const_spec((D_MODEL, D_MODEL)),
            _const_spec((1, D_MODEL)),
            _const_spec((CONV_HALO, C_CONV)),
            _const_spec((1, C_CONV)),
            _const_spec((1, C_CONV)),
            _const_spec((len(POOL_WINDOWS), POOL_GROUP, POOL_GROUP)),
            _const_spec((1, C_POOL)),
        ],
        out_specs=[
            rev(D_MODEL), rev(Z_WIDTH),
            _const_spec((8, D_MODEL)),
            _const_spec((CONV_HALO, C_CONV)),
            _const_spec((len(POOL_WINDOWS), POOL_GROUP, POOL_GROUP)),
        ],
        out_shape=[
            jax.ShapeDtypeStruct((s, D_MODEL), F32),
            jax.ShapeDtypeStruct((s, Z_WIDTH), BF16),
            jax.ShapeDtypeStruct((8, D_MODEL), F32),
            jax.ShapeDtypeStruct((CONV_HALO, C_CONV), F32),
            jax.ShapeDtypeStruct((len(POOL_WINDOWS), POOL_GROUP, POOL_GROUP), F32),
        ],
        scratch_shapes=[
            pltpu.VMEM((tm + CONV_HALO, C_CONV), F32),
            pltpu.VMEM((tm + POOL_HALO, C_POOL), F32),
            pltpu.VMEM((tm, C_CONV), F32),
            pltpu.VMEM((tm, C_CONV), F32),
            pltpu.VMEM((7, tm + CONV_HALO, C_CONV), F32),
            pltpu.VMEM((2, tm + POOL_HALO, POOL_GROUP), F32),
        ],
        name="bwd_mix",
        args=(dx1, dx1b, x, z, u1, pooled, w_in_t, w_out, g_mix, conv_w, ln_g, ln_b, pool_w, pool_scale),
        carry=carry,
    )


def _grad_matmul(a, b, bm, name, a2=None, carry=None):
    s, ma = a.shape
    nb = b.shape[1]
    na = ma // bm
    if a2 is None:
        def body(a_ref, b_ref, o_ref):
            o_ref[...] = _dot_tn(a_ref[...], b_ref[...]).astype(BF16)

        lhs_specs = [pl.BlockSpec((s, bm), lambda i: (0, i))]
        lhs = (a,)
        steps = na
    else:
        def body(a_ref, a2_ref, b_ref, o_ref):
            i = pl.program_id(0)

            @pl.when(i < na)
            def _():
                o_ref[...] = _dot_tn(a_ref[...], b_ref[...]).astype(BF16)

            @pl.when(i >= na)
            def _():
                o_ref[...] = _dot_tn(a2_ref[...], b_ref[...]).astype(BF16)

        lhs_specs = [pl.BlockSpec((s, bm), lambda i: (0, jnp.minimum(i, na - 1))),
                     pl.BlockSpec((s, bm), lambda i: (0, jnp.maximum(i - na, 0)))]
        lhs = (a, a2)
        steps = 2 * na

    outs, carried = _pcall(
        body,
        grid=(steps,),
        in_specs=lhs_specs + [pl.BlockSpec((s, nb), lambda i: (0, 0))],
        out_specs=[pl.BlockSpec((bm, nb), lambda i: (i, 0))],
        out_shape=[jax.ShapeDtypeStruct((steps * bm, nb), BF16)],
        scratch_shapes=[],
        name=name,
        args=lhs + (b,),
        carry=carry,
    )
    return outs[0], carried


def _reduce_scatter(grads, small, name):
    n, ns = len(grads), len(small)
    shapes = [g.shape[1:] for g in grads]
    gather = _Carry()
    _carry_gather(gather, small, 1, 2)

    def body(*refs):
        g = refs[:n]
        out = refs[n + ns:2 * n + ns]
        scr = refs[2 * (n + ns):]
        own, loc, r1, r2 = scr[:n], scr[n:2 * n], scr[2 * n:3 * n], scr[3 * n:4 * n]
        load_sems, s1, q1, s2, q2 = scr[4 * n:4 * n + 5]
        gather_refs = (refs[n:n + ns], refs[2 * n + ns:2 * (n + ns)], scr[4 * n + 5:])
        me, sib, chips = _place()
        c = me[2]

        _handshake(EVERY_PEER)
        gather.stage(0, *gather_refs, handshake=False)
        loads = []
        sends = []
        for a in range(n):
            ld = [pltpu.make_async_copy(g[a].at[_block(*chip, c)], loc[a].at[j], load_sems.at[a, j])
                  for j, chip in enumerate(chips)]
            ld.append(pltpu.make_async_copy(g[a].at[_block(*me)], own[a], load_sems.at[a, 3]))
            for cp in ld:
                cp.start()
            loads.append(ld)
            blocks = [(*chip, 1 - c) for chip in chips] + [sib]
            for j, blk in enumerate(blocks):
                cp = pltpu.make_async_remote_copy(
                    src_ref=g[a].at[_block(*blk)], dst_ref=r1[a].at[j],
                    send_sem=s1.at[a, j], recv_sem=q1.at[a, j], device_id=sib, device_id_type=MESH)
                cp.start()
                sends.append(cp)

        def from_sibling(a, j):
            return pltpu.make_async_remote_copy(
                src_ref=r1[a].at[j], dst_ref=r1[a].at[j], send_sem=s1.at[a, j], recv_sem=q1.at[a, j],
                device_id=sib, device_id_type=MESH)

        def partial(a, j, chip):
            return pltpu.make_async_remote_copy(
                src_ref=loc[a].at[j], dst_ref=r2[a].at[j], send_sem=s2.at[a, j], recv_sem=q2.at[a, j],
                device_id=(*chip, c), device_id_type=MESH)

        gather.stage(1, *gather_refs)
        for a in range(n):
            for j, chip in enumerate(chips):
                loads[a][j].wait()
                from_sibling(a, j).wait_recv()
                loc[a][j] = (loc[a][j].astype(F32) + r1[a][j].astype(F32)).astype(BF16)
                cp = partial(a, j, chip)
                cp.start()
                sends.append(cp)
        gather.stage(2, *gather_refs)
        for a in range(n):
            loads[a][3].wait()
            from_sibling(a, 3).wait_recv()
            acc = own[a][...].astype(F32) + r1[a][3].astype(F32)
            for j, chip in enumerate(chips):
                partial(a, j, chip).wait_recv()
                acc = acc + r2[a][j].astype(F32)
            out[a][...] = acc
        for cp in sends:
            cp.wait_send()
        gather.drain(*gather_refs)

    any_spec = pl.BlockSpec(memory_space=pl.ANY)
    vmem_spec = pl.BlockSpec(memory_space=pltpu.VMEM)
    res = pl.pallas_call(
        body,
        in_specs=[any_spec] * (n + ns),
        out_specs=[vmem_spec] * n + [any_spec] * ns,
        out_shape=[jax.ShapeDtypeStruct(sh, F32) for sh in shapes] + gather.out_shapes,
        scratch_shapes=(
            [pltpu.VMEM(sh, BF16) for sh in shapes]
            + [pltpu.VMEM((3,) + sh, BF16) for sh in shapes]
            + [pltpu.VMEM((4,) + sh, BF16) for sh in shapes]
            + [pltpu.VMEM((3,) + sh, BF16) for sh in shapes]
            + [pltpu.SemaphoreType.DMA((n, 4)),
               pltpu.SemaphoreType.DMA((n, 4)), pltpu.SemaphoreType.DMA((n, 4)),
               pltpu.SemaphoreType.DMA((n, 3)), pltpu.SemaphoreType.DMA((n, 3))]
            + gather.sem_shapes()
        ),
        compiler_params=_params(collective_id=_collective_id(EVERY_PEER)),
        name=name,
    )(*grads, *small)
    return res[:n], res[n:]


def _pair_add(grads, from_sib, blks, name):
    n = len(grads)

    def body(blk_ref, *refs):
        for a in range(n):
            refs[2 * n + a][...] = (refs[a][...].astype(F32) + refs[n + a][...].astype(F32)).astype(BF16)

    mine = [pl.BlockSpec((None,) + g.shape[1:], lambda j, b: (b[j], 0, 0)) for g in grads]
    same = [pl.BlockSpec((None,) + g.shape[1:], lambda j, b: (j, 0, 0)) for g in grads]
    return pl.pallas_call(
        body,
        grid_spec=pltpu.PrefetchScalarGridSpec(
            num_scalar_prefetch=1, grid=(4,), in_specs=mine + same, out_specs=same),
        out_shape=[jax.ShapeDtypeStruct((4,) + g.shape[1:], BF16) for g in grads],
        compiler_params=_params(("arbitrary",)),
        name=name,
    )(blks, *grads, *from_sib)


def _chip_sum(parts, from_chips, name):
    n = len(parts)

    def body(*refs):
        for a in range(n):
            acc = refs[a][...].astype(F32)
            for j in range(3):
                acc = acc + refs[n + a][j].astype(F32)
            refs[2 * n + a][...] = acc

    half = [p.shape[1] // 2 for p in parts]
    return pl.pallas_call(
        body,
        grid=(2,),
        in_specs=([pl.BlockSpec((None, h, p.shape[2]), lambda i: (3, i, 0)) for p, h in zip(parts, half)]
                  + [pl.BlockSpec((3, h, p.shape[2]), lambda i: (0, i, 0)) for p, h in zip(parts, half)]),
        out_specs=[pl.BlockSpec((h, p.shape[2]), lambda i: (i, 0)) for p, h in zip(parts, half)],
        out_shape=[jax.ShapeDtypeStruct(p.shape[1:], F32) for p in parts],
        compiler_params=_params(("arbitrary",)),
        name=name,
    )(*parts, *from_chips)


def _adam_math(w, g, m, v):
    nm = ADAM_B1 * m + (1.0 - ADAM_B1) * g
    nv = ADAM_B2 * v + (1.0 - ADAM_B2) * (g * g)
    m_hat = nm / (1.0 - ADAM_B1 ** ADAM_STEP)
    v_hat = nv / (1.0 - ADAM_B2 ** ADAM_STEP)
    return -ADAM_LR * (m_hat / (jnp.sqrt(v_hat) + ADAM_EPS) + ADAM_WD * w), nm, nv


def _sum_adamw(items, name, carry=None):
    n = len(items)

    def body(*refs):
        for a in range(n):
            p_ref, f_ref, w_ref, m_ref, v_ref = refs[5 * a:5 * a + 5]
            g_ref, d_ref, nm_ref, nv_ref = refs[5 * n + 4 * a:5 * n + 4 * a + 4]
            g = p_ref[...].astype(F32)
            for j in range(3):
                g = g + f_ref[j].astype(F32)
            g_ref[...] = g
            d_ref[...], nm_ref[...], nv_ref[...] = _adam_math(w_ref[...], g, m_ref[...], v_ref[...])

    in_specs, out_specs, out_shape, args = [], [], [], []
    for part, from_chips, w, m, v in items:
        r, c = w.shape
        spec = pl.BlockSpec((r // 2, c), lambda i: (i, 0))
        in_specs += [pl.BlockSpec((None, r // 2, c), lambda i: (3, i, 0)),
                     pl.BlockSpec((3, r // 2, c), lambda i: (0, i, 0)), spec, spec, spec]
        out_specs += [spec] * 4
        out_shape += [jax.ShapeDtypeStruct((r, c), F32)] * 4
        args += [part, from_chips, w, m, v]
    outs, carried = _pcall(body, grid=(2,), in_specs=in_specs, out_specs=out_specs, out_shape=out_shape,
                           scratch_shapes=[], name=name, args=tuple(args), carry=carry)
    return [tuple(outs[4 * a:4 * a + 4]) for a in range(n)], carried


def _small_update(gathered, layout, params, name):
    ng, npar = len(gathered), len(params)

    def body(*refs):
        parts = refs[:ng]
        prm = refs[ng:ng + 3 * npar]
        tot_refs = refs[ng + 3 * npar:2 * ng + 3 * npar]
        out = refs[2 * ng + 3 * npar:]
        tots = []
        for a in range(ng):
            acc = parts[a][0]
            for d in range(1, N_DEV):
                acc = acc + parts[a][d]
            tot_refs[a][...] = acc
            tots.append(acc)
        for i, (a, row, width) in enumerate(layout):
            g = tots[a] if row is None else tots[a][row:row + 1, :width]
            delta, nm, nv = _adam_math(prm[3 * i][...], g, prm[3 * i + 1][...], prm[3 * i + 2][...])
            out[4 * i][...] = g
            out[4 * i + 1][...] = delta
            out[4 * i + 2][...] = nm
            out[4 * i + 3][...] = nv

    flat = [t for prm in params for t in prm]
    res = pl.pallas_call(
        body,
        out_shape=([jax.ShapeDtypeStruct(g.shape[1:], F32) for g in gathered]
                   + [jax.ShapeDtypeStruct(prm[0].shape, F32) for prm in params for _ in range(4)]),
        compiler_params=pltpu.CompilerParams(vmem_limit_bytes=V7X_VMEM_LIMIT),
        name=name,
    )(*gathered, *flat)
    return res[:ng], [tuple(res[ng + 4 * i:ng + 4 * i + 4]) for i in range(npar)]


def _adamw(w, g, m, v, name):
    rows, cols = w.shape
    br = rows
    for cand in (512, 256, 128):
        if rows % cand == 0 and rows > cand:
            br = cand
            break

    def body(w_ref, g_ref, m_ref, v_ref, d_ref, nm_ref, nv_ref):
        d_ref[...], nm_ref[...], nv_ref[...] = _adam_math(w_ref[...], g_ref[...], m_ref[...], v_ref[...])

    spec = pl.BlockSpec((br, cols), lambda i: (i, 0))
    shape = jax.ShapeDtypeStruct((rows, cols), F32)
    return pl.pallas_call(
        body,
        grid=(rows // br,),
        in_specs=[spec] * 4,
        out_specs=[spec] * 3,
        out_shape=[shape] * 3,
        compiler_params=_params(("arbitrary",)),
        name=name,
    )(w, g, m, v)


def _by_device(full):
    return full.reshape(N_DEV, full.shape[0] // N_DEV, full.shape[1])


def kernel(x, p, g_mix, w_in, conv_w, conv_b, ln_g, ln_b, pool_w, pool_scale, w_out, g_ffn, w_gate_up, w_down, g_ple_gate, w_ple_gate, w_ple_up, g_ple_post, g_final, loss_target, m_g_mix, m_w_in, m_conv_w, m_conv_b, m_ln_g, m_ln_b, m_pool_w, m_pool_scale, m_w_out, m_g_ffn, m_w_gate_up, m_w_down, m_g_ple_gate, m_w_ple_gate, m_w_ple_up, m_g_ple_post, m_g_final, v_g_mix, v_w_in, v_conv_w, v_conv_b, v_ln_g, v_ln_b, v_pool_w, v_pool_scale, v_w_out, v_g_ffn, v_w_gate_up, v_w_down, v_g_ple_gate, v_w_ple_gate, v_w_ple_up, v_g_ple_post, v_g_final):
    seq = x.shape[1]
    tile = min(TOKEN_TILE, seq)
    xs = x[0]
    ps = p[0, 0]
    tgt = loss_target[0]
    ax, ay, ac = lax.axis_index("x"), lax.axis_index("y"), lax.axis_index("c")
    me = _block(ax, ay, ac)
    blks = jnp.stack([_block(1 - ax, ay, ac), _block(ax, 1 - ay, ac), _block(1 - ax, 1 - ay, ac), me]).astype(jnp.int32)
    rows = lambda gth: gth.reshape((-1,) + gth.shape[2:])

    sh_in, sh_out, sh_gu, sh_down, sh_pg, sh_pu, pool_w_b = _to_bf16([
        w_in[0].T,
        w_out[0],
        w_gate_up[0].T,
        w_down[0],
        w_ple_gate[0],
        w_ple_up[0].T,
        pool_w[0]])
    carry = _Carry()
    _carry_gather(carry, [sh_in, sh_out,
                          jnp.pad(conv_w[0].T, ((0, 0), (0, CONV_HALO - CONV_K)))], 8, 15)
    (h1,), first = _norm_in(xs, g_mix, tile, carry)
    w_in_t, w_out_f, conv_w_t = [rows(gth) for gth in first]
    conv_w_f = conv_w_t.T

    carry = _Carry()
    _carry_gather(carry, [sh_gu], 8, 12)
    (z, u1, pooled, x1, mix), (w_gu_all,) = _fwd_mix(
        xs, h1, w_in_t, w_out_f, conv_w_f, conv_b, ln_g, ln_b, pool_w_b, pool_scale, tile, carry)
    w_gu_t = rows(w_gu_all)

    carry = _Carry()
    _carry_gather(carry, [sh_down, sh_pg, sh_pu], 6, 10)
    (h2, g_sav, u_sav, act), late = _ffn_up(x1, w_gu_t, g_ffn, tile, carry)
    w_down_f, w_pg_f, w_pu_t = [rows(gth) for gth in late]
    dx2, dx2b, hg, ds, dpe, pb, stats_ple = _down_ple(
        x1, act, w_down_f, ps, tgt, w_pg_f, w_pu_t, g_ple_gate, g_ple_post, g_final.reshape(1, D_MODEL),
        tile)

    d_w_down, _ = _grad_matmul(act, dx2b, GRAD_ROWS, "grad_w_down")
    d_w_pg, _ = _grad_matmul(hg, ds, GRAD_ROWS, "grad_w_ple_gate")
    d_w_pu_t, _ = _grad_matmul(dpe, pb, GRAD_ROWS, "grad_w_ple_up")
    early = [_by_device(d_w_pg), _by_device(d_w_pu_t), _by_device(d_w_down)]
    carry = _Carry()
    _carry_pair(carry, early)
    (dg, du, dx1, dx1b, stats_ffn), from_sib = _ffn_bwd(dx2, dx2b, x1, g_sav, u_sav, w_gu_t, w_down_f, g_ffn,
                                                        tile, carry)
    early_parts = _pair_add(early, from_sib, blks, "pair_add_early")

    carry = _Carry()
    _carry_chip(carry, early_parts)
    d_w_gu_t, early_chips = _grad_matmul(dg, h2, GRAD_ROWS, "grad_w_gate_up", a2=du, carry=carry)
    carry = _Carry()
    _carry_pair(carry, [_by_device(d_w_gu_t)])
    d_w_out, gu_sib = _grad_matmul(mix, dx1b, GRAD_ROWS, "grad_w_out", carry=carry)
    gu_parts = _pair_add([_by_device(d_w_gu_t)], gu_sib, blks, "pair_add_gate_up")

    carry = _Carry()
    _carry_chip(carry, gu_parts)
    _carry_pair(carry, [_by_device(d_w_out)])
    (dx, dz, vec_mix, dconv_w_part, dpool_w_part), carried = _bwd_mix(
        dx1, dx1b, xs, z, u1, pooled, w_in_t, w_out_f, g_mix, conv_w_f, ln_g, ln_b, pool_w_b, pool_scale,
        tile, carry)
    gu_chips, out_sib = carried[:1], carried[1:]
    out_parts = _pair_add([_by_device(d_w_out)], out_sib, blks, "pair_add_out")

    carry = _Carry()
    _carry_chip(carry, out_parts)
    _carry_gather(carry, [vec_mix, stats_ple, stats_ffn, dconv_w_part,
                          dpool_w_part.reshape(-1, POOL_GROUP)], 2, 4)
    d_w_in_t, carried = _grad_matmul(dz, h1, GRAD_ROWS, "grad_w_in", carry=carry)
    out_chips, small = carried[:1], carried[1:]
    small = list(small[:4]) + [small[4].reshape((N_DEV,) + dpool_w_part.shape)]
    (gr_w_in_t,), _ = _reduce_scatter([_by_device(d_w_in_t)], [], "scatter_last")

    vec_names = ["g_mix", "ln_g", "ln_b", "conv_b", "pool_scale", "g_final", "g_ple_post", "g_ple_gate", "g_ffn",
                 "pool_w"]
    layout = [(0, 0, D_MODEL), (0, 1, C_CONV), (0, 2, C_CONV), (0, 3, C_CONV), (0, 4, C_POOL),
              (1, 0, D_MODEL), (1, 1, D_MODEL), (1, 2, D_MODEL), (2, 0, D_MODEL), (4, None, None)]
    as_row = lambda t: t.reshape(1, D_MODEL)
    params = [(g_mix, m_g_mix, v_g_mix), (ln_g, m_ln_g, v_ln_g), (ln_b, m_ln_b, v_ln_b),
              (conv_b, m_conv_b, v_conv_b), (pool_scale, m_pool_scale, v_pool_scale),
              (as_row(g_final), as_row(m_g_final), as_row(v_g_final)),
              (g_ple_post, m_g_ple_post, v_g_ple_post), (g_ple_gate, m_g_ple_gate, v_g_ple_gate),
              (g_ffn, m_g_ffn, v_g_ffn), (pool_w[0], m_pool_w[0], v_pool_w[0])]
    tots, small_upd = _small_update(small, layout, params, "small_update")
    loss = tots[1][3, 0]
    upd = {}
    for nm, res, prm in zip(vec_names, small_upd, [g_mix, ln_g, ln_b, conv_b, pool_scale, g_final, g_ple_post,
                                                    g_ple_gate, g_ffn, pool_w]):
        upd[nm] = tuple(t.reshape(prm.shape) for t in res)
    gr_conv_w = lax.dynamic_slice_in_dim(tots[3][:CONV_K], me * (C_CONV // N_DEV), C_CONV // N_DEV, axis=1)

    natural = lambda t: t[None]
    turned = lambda t: t.T[None]
    (res_pg, res_gu, res_out, res_down), _ = _sum_adamw([
        (early_parts[0], early_chips[0], w_ple_gate[0], m_w_ple_gate[0], v_w_ple_gate[0]),
        (gu_parts[0], gu_chips[0], w_gate_up[0].T, m_w_gate_up[0].T, v_w_gate_up[0].T),
        (out_parts[0], out_chips[0], w_out[0], m_w_out[0], v_w_out[0]),
        (early_parts[2], early_chips[2], w_down[0], m_w_down[0], v_w_down[0])], "adamw_big")
    upd["w_ple_gate"] = tuple(natural(t) for t in res_pg)
    upd["w_gate_up"] = tuple(turned(t) for t in res_gu)
    upd["w_out"] = tuple(natural(t) for t in res_out)
    upd["w_down"] = tuple(natural(t) for t in res_down)
    (gr_w_pu_t,) = _chip_sum(early_parts[1:2], early_chips[1:2], "chip_sum")
    plain = [
        ("w_in", w_in[0].T, gr_w_in_t, m_w_in[0].T, v_w_in[0].T, True),
        ("w_ple_up", w_ple_up[0], gr_w_pu_t.T, m_w_ple_up[0], v_w_ple_up[0], False),
        ("conv_w", conv_w[0], gr_conv_w, m_conv_w[0], v_conv_w[0], False),
    ]
    for nm, w_, g_, m_, v_, transposed in plain:
        res = (g_,) + tuple(_adamw(w_, g_, m_, v_, "adamw_" + nm))
        upd[nm] = tuple((t.T if transposed else t)[None] for t in res)

    order = ["g_mix", "w_in", "conv_w", "conv_b", "ln_g", "ln_b", "pool_w", "pool_scale", "w_out", "g_ffn",
             "w_gate_up", "w_down", "g_ple_gate", "w_ple_gate", "w_ple_up", "g_ple_post", "g_final"]
    outs = [loss, dx[None]]
    for k in range(4):
        outs += [upd[nm][k] for nm in order]
    return tuple(outs)
```

```python
import functools

import jax
import jax.numpy as jnp
from jax import lax
from jax.experimental import pallas as pl
from jax.experimental.pallas import tpu as pltpu

D_MODEL = 1024
C_CONV = 512
C_POOL = 512
Z_WIDTH = 2 * C_CONV + C_POOL
POOL_WINDOWS = (2, 4, 8, 16)
POOL_GROUP = 128
CONV_K = 31
D_FF = 2816
D_PLE = 256
EPS = 1e-6
N_DEV = 8

ADAM_LR = 0.001
ADAM_B1 = 0.9
ADAM_B2 = 0.999
ADAM_EPS = 1e-08
ADAM_WD = 0.01
ADAM_STEP = 10

CONV_HALO = 32
POOL_HALO = 32
ROW_CHUNK = 32
TOKEN_TILE = 256
GRAD_ROWS = 256
V7X_VMEM_LIMIT = 56 * 1024 * 1024
FF_CHUNK = D_FF // 2

BF16 = jnp.bfloat16
F32 = jnp.float32
MESH = pl.DeviceIdType.MESH


def _dot(a, b):
    return lax.dot_general(a, b, (((1,), (0,)), ((), ())), preferred_element_type=F32)


def _dot_nt(a, b):
    return lax.dot_general(a, b, (((1,), (1,)), ((), ())), preferred_element_type=F32)


def _dot_tn(a, b):
    return lax.dot_general(a, b, (((0,), (0,)), ((), ())), preferred_element_type=F32)


def _rms_fwd(x, g):
    r = lax.rsqrt(jnp.mean(x * x, axis=-1, keepdims=True) + EPS)
    return x * r * g, r


def _rms_bwd(x, r, g, dy):
    xr = x * r
    dg = jnp.sum(dy * xr, axis=0, keepdims=True)
    dyg = dy * g
    dx = r * (dyg - xr * jnp.mean(dyg * xr, axis=-1, keepdims=True))
    return dx, dg


def _sigmoid(x):
    return jax.nn.sigmoid(x)


def _params(sem=None, collective_id=None):
    return pltpu.CompilerParams(dimension_semantics=sem, vmem_limit_bytes=V7X_VMEM_LIMIT,
                                collective_id=collective_id)


def _place():
    x, y, c = lax.axis_index("x"), lax.axis_index("y"), lax.axis_index("c")
    chips = [(1 - x, y), (x, 1 - y), (1 - x, 1 - y)]
    return (x, y, c), (x, y, 1 - c), chips


def _block(px, py, pc):
    return 4 * px + 2 * py + pc


EVERY_PEER = ("sib", 0, 1, 2)
PEER_SET_IDS = {("sib",): 0, ("0", "1", "sib"): 1, ("0", "1", "2"): 2, ("0", "1", "2", "sib"): 3}


def _collective_id(peers):
    return PEER_SET_IDS[tuple(sorted(str(p) for p in peers))]


def _handshake(peers):
    me, sib, chips = _place()
    barrier = pltpu.get_barrier_semaphore()
    for p in peers:
        to = sib if p == "sib" else (*chips[p], me[2])
        pl.semaphore_signal(barrier, inc=1, device_id=to, device_id_type=MESH)
    pl.semaphore_wait(barrier, len(peers))


class _Carry:
    def __init__(self):
        self.inputs = []
        self.out_shapes = []
        self.copies = []
        self.locals = []

    def add_input(self, arr):
        self.inputs.append(arr)
        return len(self.inputs) - 1

    def add_output(self, shape, dtype):
        self.out_shapes.append(jax.ShapeDtypeStruct(shape, dtype))
        return len(self.out_shapes) - 1

    def local(self, src_idx, dst_idx, dst_blk):
        self.locals.append((src_idx, dst_idx, dst_blk))

    def copy(self, src, dst_idx, dst_blk, got_blk, peer, step=0, after=()):
        self.copies.append(dict(src=src, dst_idx=dst_idx, dst_blk=dst_blk, got_blk=got_blk, peer=peer, step=step,
                                after=tuple(after)))
        return len(self.copies) - 1

    def sem_shapes(self):
        return [pltpu.SemaphoreType.DMA((max(1, len(self.copies)),)),
                pltpu.SemaphoreType.DMA((max(1, len(self.copies)),)),
                pltpu.SemaphoreType.DMA((max(1, len(self.locals)),))]

    @staticmethod
    def _view(ref, where):
        if isinstance(where, tuple):
            blk, row0, nrows = where
            return ref.at[blk, pl.ds(row0, nrows)]
        return ref.at[where]

    def _desc(self, k, ins, outs, sems, place):
        cp = self.copies[k]
        me, sib, chips = place
        kind, idx, blk = cp["src"]
        src = (ins if kind == "in" else outs)[idx]
        if blk is not None:
            src = self._view(src, blk(*place))
        to = sib if cp["peer"] == "sib" else (*chips[cp["peer"]], me[2])
        return pltpu.make_async_remote_copy(
            src_ref=src, dst_ref=self._view(outs[cp["dst_idx"]], cp["dst_blk"](*place)),
            send_sem=sems[0].at[k], recv_sem=sems[1].at[k], device_id=to, device_id_type=MESH)

    def _arrival(self, k, outs, sems, place):
        cp = self.copies[k]
        got = self._view(outs[cp["dst_idx"]], cp["got_blk"](*place))
        return pltpu.make_async_remote_copy(
            src_ref=got, dst_ref=got, send_sem=sems[0].at[k], recv_sem=sems[1].at[k],
            device_id=place[0], device_id_type=MESH)

    def _local(self, n, ins, outs, sems, place):
        src_idx, dst_idx, blk = self.locals[n]
        return pltpu.make_async_copy(ins[src_idx], outs[dst_idx].at[blk(*place)], sems[2].at[n])

    def stages(self):
        return sorted({0} | {cp["step"] for cp in self.copies})

    def peers(self):
        return sorted({cp["peer"] for cp in self.copies}, key=str)

    def stage(self, s, ins, outs, sems, handshake=True):
        place = _place()
        if s == 0:
            if handshake:
                _handshake(self.peers())
            self._waited = set()
            for n in range(len(self.locals)):
                self._local(n, ins, outs, sems, place).start()
        for k, cp in enumerate(self.copies):
            if cp["step"] != s:
                continue
            for a in cp["after"]:
                if a not in self._waited:
                    self._arrival(a, outs, sems, place).wait_recv()
                    self._waited.add(a)
            self._desc(k, ins, outs, sems, place).start()

    def drain(self, ins, outs, sems):
        place = _place()
        for k in range(len(self.copies)):
            if k not in self._waited:
                self._arrival(k, outs, sems, place).wait_recv()
        for k in range(len(self.copies)):
            self._desc(k, ins, outs, sems, place).wait_send()
        for n in range(len(self.locals)):
            self._local(n, ins, outs, sems, place).wait()

    def starts(self, step, nsteps, ins, outs, sems):
        for s in self.stages():
            pl.when(step == min(s, nsteps - 1))(functools.partial(self.stage, s, ins, outs, sems))

    def finish(self, step, nsteps, ins, outs, sems):
        pl.when(step == nsteps - 1)(functools.partial(self.drain, ins, outs, sems))


def _const_blk(j):
    return lambda me, sib, chips: j


def _carry_gather(carry, shards, relay_step, last_step):
    outs = []
    for sh in shards:
        i = carry.add_input(sh)
        o = carry.add_output((N_DEV,) + sh.shape, sh.dtype)
        half = sh.shape[0] // 2
        tile = 16 if sh.dtype == BF16 else 8
        split = half % tile == 0
        rows = [(0, half), (half, sh.shape[0] - half)] if split else [(0, sh.shape[0]), None]

        def whole(j, core):
            return lambda me, sib, chips, j=j, core=core: _block(*chips[j], me[2] if core == 0 else 1 - me[2])

        def part(j, core, h, rows=rows):
            return lambda me, sib, chips: (_block(*chips[j], me[2] if core == 0 else 1 - me[2]),) + rows[h]

        mine = lambda me, sib, chips: _block(*me)
        carry.local(i, o, mine)
        carry.copy(("in", i, None), o, mine, lambda me, sib, chips: _block(*sib), "sib")
        near = [carry.copy(("in", i, None), o, mine, whole(j, 0), j) for j in range(2)]
        for j in range(2):
            carry.copy(("out", o, whole(j, 0)), o, whole(j, 0), whole(j, 1), "sib", step=relay_step, after=(near[j],))
        for j in range(2):
            if rows[j] is None:
                continue
            far = carry.copy(("out", o, part(j, 0, j)), o, part(j, 0, j), part(2, 0, j), 1 - j,
                             step=relay_step, after=(near[j],))
            carry.copy(("out", o, part(2, 0, j)), o, part(2, 0, j), part(2, 1, j), "sib", step=last_step, after=(far,))
        outs.append(o)
    return outs


def _carry_pair(carry, grads):
    outs = []
    for g in grads:
        i = carry.add_input(g)
        o = carry.add_output((4,) + g.shape[1:], g.dtype)
        for j in range(4):
            if j < 3:
                blk = lambda me, sib, chips, j=j: _block(*chips[j], 1 - me[2])
            else:
                blk = lambda me, sib, chips: _block(*sib)
            carry.copy(("in", i, blk), o, _const_blk(j), _const_blk(j), "sib")
        outs.append(o)
    return outs


def _carry_chip(carry, parts):
    outs = []
    for p in parts:
        i = carry.add_input(p)
        o = carry.add_output((3,) + p.shape[1:], p.dtype)
        for j in range(3):
            carry.copy(("in", i, _const_blk(j)), o, _const_blk(j), _const_blk(j), j)
        outs.append(o)
    return outs


def _pcall(body, *, grid, in_specs, out_specs, out_shape, scratch_shapes, name, args, carry=None):
    sem = ("arbitrary",) * len(grid)
    if carry is None:
        res = pl.pallas_call(body, grid=grid, in_specs=in_specs, out_specs=out_specs, out_shape=out_shape,
                             scratch_shapes=scratch_shapes, compiler_params=_params(sem), name=name)(*args)
        return list(res), []
    n_in, n_out, n_scr = len(in_specs), len(out_specs), len(scratch_shapes)
    c_in, c_out = len(carry.inputs), len(carry.out_shapes)
    nsteps = 1
    for extent in grid:
        nsteps *= extent

    def wrapped(*refs):
        ins = refs[:n_in]
        cins = refs[n_in:n_in + c_in]
        o0 = n_in + c_in
        outs = refs[o0:o0 + n_out]
        couts = refs[o0 + n_out:o0 + n_out + c_out]
        s0 = o0 + n_out + c_out
        scr = refs[s0:s0 + n_scr]
        sems = refs[s0 + n_scr:]
        step = pl.program_id(0)
        for d in range(1, len(grid)):
            step = step * grid[d] + pl.program_id(d)
        carry.starts(step, nsteps, cins, couts, sems)
        body(*ins, *outs, *scr)
        carry.finish(step, nsteps, cins, couts, sems)

    any_spec = pl.BlockSpec(memory_space=pl.ANY)
    res = pl.pallas_call(
        wrapped, grid=grid,
        in_specs=list(in_specs) + [any_spec] * c_in,
        out_specs=list(out_specs) + [any_spec] * c_out,
        out_shape=list(out_shape) + carry.out_shapes,
        scratch_shapes=list(scratch_shapes) + carry.sem_shapes(),
        compiler_params=_params(sem, _collective_id(carry.peers())), name=name)(*args, *carry.inputs)
    return list(res[:n_out]), list(res[n_out:])


def _shifted_copies(buf, shifted, tm):
    span = tm + CONV_HALO - 8
    for r in range(1, 8):
        shifted[r - 1, 0:span, :] = buf[r:r + span, :]


def _rows_at(buf, shifted, start):
    aligned, r = (start // 8) * 8, start % 8
    if r == 0:
        return buf[aligned:aligned + ROW_CHUNK, :]
    return shifted[r - 1, aligned:aligned + ROW_CHUNK, :]


def _window_sums(buf, cols, work, levels, tm, trailing):
    src, src_cols = buf, cols
    for k in range(levels + 1):
        shift = 1 << k
        dst = work.at[k % 2]
        if trailing:
            lo = 8 * (k + 1)
            dst[lo:tm + POOL_HALO, :] = (src[lo:tm + POOL_HALO, src_cols]
                                         + src[lo - shift:tm + POOL_HALO - shift, src_cols])
        else:
            hi = tm + POOL_HALO - 8 * (k + 1)
            dst[0:hi, :] = src[0:hi, src_cols] + src[shift:hi + shift, src_cols]
        src, src_cols = dst, slice(0, POOL_GROUP)
    return src[POOL_HALO:POOL_HALO + tm, src_cols] if trailing else src[0:tm, src_cols]


def _row_spec(tm, width):
    return pl.BlockSpec((tm, width), lambda i: (i, 0))


def _const_spec(shape):
    return pl.BlockSpec(shape, lambda i: (0,) * len(shape))


def _weight_spec(shape):
    return pl.BlockSpec(shape, lambda i: (0,) * len(shape), pipeline_mode=pl.Buffered(1))


def _to_bf16(arrays):
    n = len(arrays)

    def body(*refs):
        for a in range(n):
            refs[n + a][...] = refs[a][...].astype(BF16)

    whole = [pl.BlockSpec(arr.shape, lambda i, nd=arr.ndim: (0,) * nd) for arr in arrays]
    return pl.pallas_call(
        body,
        grid=(1,),
        in_specs=whole,
        out_specs=whole,
        out_shape=[jax.ShapeDtypeStruct(arr.shape, BF16) for arr in arrays],
        compiler_params=_params(("arbitrary",)),
        name="to_bf16",
    )(*arrays)


def _norm_in(x, g_mix, tm, carry=None):
    s = x.shape[0]

    def body(x_ref, gmix_ref, h1_ref):
        h, _ = _rms_fwd(x_ref[...], gmix_ref[...])
        h1_ref[...] = h.astype(BF16)

    return _pcall(
        body,
        grid=(s // tm,),
        in_specs=[_row_spec(tm, D_MODEL), _const_spec((1, D_MODEL))],
        out_specs=[_row_spec(tm, D_MODEL)],
        out_shape=[jax.ShapeDtypeStruct((s, D_MODEL), BF16)],
        scratch_shapes=[],
        name="norm_in",
        args=(x, g_mix),
        carry=carry,
    )


def _fwd_mix(x, h1, w_in_t, w_out, conv_w, conv_b, ln_g, ln_b, pool_w, pool_scale, tm, carry=None):
    s = x.shape[0]
    nt = s // tm

    def body(x_ref, h1_ref, win_ref, wout_ref, cw_ref, cb_ref, lng_ref, lnb_ref, pw_ref, ps_ref,
             z_ref, u1_ref, pooled_ref, x1_ref, mix_ref, ubuf, vbuf, ush, pbuf):
        i = pl.program_id(0)

        @pl.when(i == 0)
        def _():
            ubuf[0:CONV_HALO, :] = jnp.zeros((CONV_HALO, C_CONV), F32)
            vbuf[0:POOL_HALO, :] = jnp.zeros((POOL_HALO, C_POOL), F32)

        xt = x_ref[...]
        z = _dot_nt(h1_ref[...], win_ref[...])
        z_ref[...] = z
        a = z[:, :C_CONV]
        b = z[:, C_CONV:2 * C_CONV]
        v = z[:, 2 * C_CONV:]
        ubuf[CONV_HALO:CONV_HALO + tm, :] = a * _sigmoid(b)
        vbuf[POOL_HALO:POOL_HALO + tm, :] = v

        _shifted_copies(ubuf, ush, tm)
        for rc in range(tm // ROW_CHUNK):
            base = rc * ROW_CHUNK + CONV_HALO - (CONV_K - 1)
            acc = jnp.broadcast_to(cb_ref[...], (ROW_CHUNK, C_CONV))
            for k in range(CONV_K):
                acc = acc + cw_ref[k:k + 1, :] * _rows_at(ubuf, ush, base + k)
            u1_ref[rc * ROW_CHUNK:(rc + 1) * ROW_CHUNK, :] = acc

        u1 = u1_ref[...]
        mu = jnp.mean(u1, axis=-1, keepdims=True)
        cen = u1 - mu
        rstd = lax.rsqrt(jnp.mean(cen * cen, axis=-1, keepdims=True) + EPS)
        u2 = cen * rstd * lng_ref[...] + lnb_ref[...]
        u = u2 * _sigmoid(u2)

        pos1 = (i * tm + lax.broadcasted_iota(jnp.int32, (tm, 1), 0) + 1).astype(F32)
        parts = [u]
        for g, w in enumerate(POOL_WINDOWS):
            cols = slice(g * POOL_GROUP, (g + 1) * POOL_GROUP)
            vg = v[:, cols]
            tot = _window_sums(vbuf, cols, pbuf, g, tm, trailing=True)
            pooled = tot / jnp.minimum(pos1, float(w)) - vg
            pooled_b = pooled.astype(BF16)
            pooled_ref[:, cols] = pooled_b
            parts.append(_dot(pooled_b, pw_ref[g]) * ps_ref[:, cols])
        mix = jnp.concatenate(parts, axis=-1).astype(BF16)
        mix_ref[...] = mix
        x1_ref[...] = xt + _dot(mix, wout_ref[...])

        ubuf[0:CONV_HALO, :] = ubuf[tm:tm + CONV_HALO, :]
        vbuf[0:POOL_HALO, :] = vbuf[tm:tm + POOL_HALO, :]

    return _pcall(
        body,
        grid=(nt,),
        in_specs=[
            _row_spec(tm, D_MODEL),
            _row_spec(tm, D_MODEL),
            _const_spec((Z_WIDTH, D_MODEL)),
            _const_spec((D_MODEL, D_MODEL)),
            _const_spec((CONV_HALO, C_CONV)),
            _const_spec((1, C_CONV)),
            _const_spec((1, C_CONV)),
            _const_spec((1, C_CONV)),
            _const_spec((len(POOL_WINDOWS), POOL_GROUP, POOL_GROUP)),
            _const_spec((1, C_POOL)),
        ],
        out_specs=[
            _row_spec(tm, Z_WIDTH),
            _row_spec(tm, C_CONV),
            _row_spec(tm, C_POOL),
            _row_spec(tm, D_MODEL),
            _row_spec(tm, D_MODEL),
        ],
        out_shape=[
            jax.ShapeDtypeStruct((s, Z_WIDTH), F32),
            jax.ShapeDtypeStruct((s, C_CONV), F32),
            jax.ShapeDtypeStruct((s, C_POOL), BF16),
            jax.ShapeDtypeStruct((s, D_MODEL), F32),
            jax.ShapeDtypeStruct((s, D_MODEL), BF16),
        ],
        scratch_shapes=[
            pltpu.VMEM((tm + CONV_HALO, C_CONV), F32),
            pltpu.VMEM((tm + POOL_HALO, C_POOL), F32),
            pltpu.VMEM((7, tm + CONV_HALO, C_CONV), F32),
            pltpu.VMEM((2, tm + POOL_HALO, POOL_GROUP), F32),
        ],
        name="fwd_mix",
        args=(x, h1, w_in_t, w_out, conv_w, conv_b, ln_g, ln_b, pool_w, pool_scale),
        carry=carry,
    )


def _ffn_up(x1, w_gu_t, g_ffn, tm, carry=None):
    s = x1.shape[0]

    def body(x1_ref, w_ref, gffn_ref, h2_ref, g_ref, u_ref, act_ref):
        h, _ = _rms_fwd(x1_ref[...], gffn_ref[...])
        h2 = h.astype(BF16)
        h2_ref[...] = h2
        for c in range(D_FF // FF_CHUNK):
            cols = slice(c * FF_CHUNK, (c + 1) * FF_CHUNK)
            g = _dot_nt(h2, w_ref[c * FF_CHUNK:(c + 1) * FF_CHUNK, :])
            u = _dot_nt(h2, w_ref[D_FF + c * FF_CHUNK:D_FF + (c + 1) * FF_CHUNK, :])
            g_ref[:, cols] = g.astype(BF16)
            u_ref[:, cols] = u.astype(BF16)
            act_ref[:, cols] = (g * _sigmoid(g) * u).astype(BF16)

    return _pcall(
        body,
        grid=(s // tm,),
        in_specs=[_row_spec(tm, D_MODEL), _weight_spec((2 * D_FF, D_MODEL)), _const_spec((1, D_MODEL))],
        out_specs=[_row_spec(tm, D_MODEL), _row_spec(tm, D_FF), _row_spec(tm, D_FF), _row_spec(tm, D_FF)],
        out_shape=[
            jax.ShapeDtypeStruct((s, D_MODEL), BF16),
            jax.ShapeDtypeStruct((s, D_FF), BF16),
            jax.ShapeDtypeStruct((s, D_FF), BF16),
            jax.ShapeDtypeStruct((s, D_FF), BF16),
        ],
        scratch_shapes=[],
        name="ffn_up",
        args=(x1, w_gu_t, g_ffn),
        carry=carry,
    )


def _down_ple(x1, act, w_down, p, tgt, w_pg, w_pu_t, g_gate, g_post, g_final, tm):
    s = x1.shape[0]
    nt = s // tm

    def body(x1_ref, act_ref, wd_ref, p_ref, t_ref, wpg_ref, wpu_ref, gg_ref, gp_ref, gf_ref,
             dx2_ref, dx2b_ref, hg_ref, ds_ref, dpe_ref, pb_ref, stats_ref, x2_cur, x2_next):
        i = pl.program_id(0)

        @pl.when(i == 0)
        def _():
            stats_ref[...] = jnp.zeros_like(stats_ref)
            x2_cur[...] = jnp.zeros((tm, D_MODEL), F32)

        def down(c):
            cols = slice(c * 256, (c + 1) * 256)
            x2_next[:, cols] = x1_ref[:, cols] + _dot(act_ref[...], wd_ref[:, cols])

        x2 = x2_cur[...]
        counts = i >= 1

        hg, rg = _rms_fwd(x2, gg_ref[...])
        hg_b = hg.astype(BF16)
        hg_ref[...] = hg_b
        down(0)
        gate = _sigmoid(_dot(hg_b, wpg_ref[...]))
        pb = p_ref[...].astype(BF16)
        pb_ref[...] = pb
        pe = _dot_nt(pb, wpu_ref[...])
        e, rp = _rms_fwd(pe, gp_ref[...])
        x3 = x2 + gate * e
        down(1)
        y, r3 = _rms_fwd(x3, gf_ref[...])
        diff = y - t_ref[...]
        loss = 0.5 * jnp.sum(jnp.sum(diff * diff, axis=-1, keepdims=True), axis=0, keepdims=True) / D_MODEL
        dy = diff * (1.0 / D_MODEL)

        dx3, dgf = _rms_bwd(x3, r3, gf_ref[...], dy)
        down(2)
        dpe, dgp = _rms_bwd(pe, rp, gp_ref[...], dx3 * gate)
        dpe_ref[...] = dpe.astype(BF16)
        ds = (dx3 * e * gate * (1.0 - gate)).astype(BF16)
        ds_ref[...] = ds
        dhg = _dot_nt(ds, wpg_ref[...])
        down(3)
        dxg, dgg = _rms_bwd(x2, rg, gg_ref[...], dhg)
        dx2 = dx3 + dxg
        dx2_ref[...] = dx2
        dx2b_ref[...] = dx2.astype(BF16)
        x2_cur[...] = x2_next[...]

        stats_ref[0:1, :] += jnp.where(counts, dgf, 0.0)
        stats_ref[1:2, :] += jnp.where(counts, dgp, 0.0)
        stats_ref[2:3, :] += jnp.where(counts, dgg, 0.0)
        stats_ref[3:4, :] += jnp.where(counts, jnp.broadcast_to(loss, (1, D_MODEL)), 0.0)

    ahead = lambda width: pl.BlockSpec((tm, width), lambda i: (jnp.minimum(i, nt - 1), 0))
    behind = lambda width: pl.BlockSpec((tm, width), lambda i: (jnp.maximum(i - 1, 0), 0))
    return pl.pallas_call(
        body,
        grid=(nt + 1,),
        in_specs=[
            ahead(D_MODEL),
            ahead(D_FF),
            _weight_spec((D_FF, D_MODEL)),
            behind(D_PLE),
            behind(D_MODEL),
            _weight_spec((D_MODEL, D_MODEL)),
            _weight_spec((D_MODEL, D_PLE)),
            _const_spec((1, D_MODEL)),
            _const_spec((1, D_MODEL)),
            _const_spec((1, D_MODEL)),
        ],
        out_specs=[
            behind(D_MODEL),
            behind(D_MODEL),
            behind(D_MODEL),
            behind(D_MODEL),
            behind(D_MODEL),
            behind(D_PLE),
            _const_spec((8, D_MODEL)),
        ],
        out_shape=[
            jax.ShapeDtypeStruct((s, D_MODEL), F32),
            jax.ShapeDtypeStruct((s, D_MODEL), BF16),
            jax.ShapeDtypeStruct((s, D_MODEL), BF16),
            jax.ShapeDtypeStruct((s, D_MODEL), BF16),
            jax.ShapeDtypeStruct((s, D_MODEL), BF16),
            jax.ShapeDtypeStruct((s, D_PLE), BF16),
            jax.ShapeDtypeStruct((8, D_MODEL), F32),
        ],
        scratch_shapes=[pltpu.VMEM((tm, D_MODEL), F32), pltpu.VMEM((tm, D_MODEL), F32)],
        compiler_params=_params(("arbitrary",)),
        name="down_ple",
    )(x1, act, w_down, p, tgt, w_pg, w_pu_t, g_gate, g_post, g_final)


def _ffn_bwd(dx2, dx2b, x1, g_sav, u_sav, w_gu_t, w_down, g_ffn, tm, carry=None):
    s = x1.shape[0]

    def body(dx2_ref, dx2b_ref, x1_ref, g_ref, u_ref, w_ref, wd_ref, gffn_ref,
             dg_ref, du_ref, dx1_ref, dx1b_ref, stats_ref):
        @pl.when(pl.program_id(0) == 0)
        def _():
            stats_ref[...] = jnp.zeros_like(stats_ref)

        dx2b = dx2b_ref[...]
        nc = D_FF // FF_CHUNK
        dacts = [_dot_nt(dx2b, wd_ref[c * FF_CHUNK:(c + 1) * FF_CHUNK, :]) for c in range(nc)]
        dh2 = jnp.zeros((tm, D_MODEL), F32)
        for c in range(nc):
            cols = slice(c * FF_CHUNK, (c + 1) * FF_CHUNK)
            g = g_ref[:, cols].astype(F32)
            u = u_ref[:, cols].astype(F32)
            sg = _sigmoid(g)
            dg = (dacts[c] * u * sg * (1.0 + g * (1.0 - sg))).astype(BF16)
            du = (dacts[c] * g * sg).astype(BF16)
            dg_ref[:, cols] = dg
            du_ref[:, cols] = du
            dh2 = dh2 + _dot(dg, w_ref[c * FF_CHUNK:(c + 1) * FF_CHUNK, :])
            dh2 = dh2 + _dot(du, w_ref[D_FF + c * FF_CHUNK:D_FF + (c + 1) * FF_CHUNK, :])

        x1 = x1_ref[...]
        r2 = lax.rsqrt(jnp.mean(x1 * x1, axis=-1, keepdims=True) + EPS)
        dxn, dgf = _rms_bwd(x1, r2, gffn_ref[...], dh2)
        dx1 = dx2_ref[...] + dxn
        dx1_ref[...] = dx1
        dx1b_ref[...] = dx1.astype(BF16)
        stats_ref[0:1, :] += dgf

    return _pcall(
        body,
        grid=(s // tm,),
        in_specs=[
            _row_spec(tm, D_MODEL), _row_spec(tm, D_MODEL), _row_spec(tm, D_MODEL),
            _row_spec(tm, D_FF), _row_spec(tm, D_FF),
            _weight_spec((2 * D_FF, D_MODEL)), _weight_spec((D_FF, D_MODEL)), _const_spec((1, D_MODEL)),
        ],
        out_specs=[_row_spec(tm, D_FF), _row_spec(tm, D_FF), _row_spec(tm, D_MODEL), _row_spec(tm, D_MODEL),
                   _const_spec((8, D_MODEL))],
        out_shape=[
            jax.ShapeDtypeStruct((s, D_FF), BF16),
            jax.ShapeDtypeStruct((s, D_FF), BF16),
            jax.ShapeDtypeStruct((s, D_MODEL), F32),
            jax.ShapeDtypeStruct((s, D_MODEL), BF16),
            jax.ShapeDtypeStruct((8, D_MODEL), F32),
        ],
        scratch_shapes=[],
        name="ffn_bwd",
        args=(dx2, dx2b, x1, g_sav, u_sav, w_gu_t, w_down, g_ffn),
        carry=carry,
    )


def _bwd_mix(dx1, dx1b, x, z, u1, pooled, w_in_t, w_out, g_mix, conv_w, ln_g, ln_b, pool_w, pool_scale, tm,
             carry=None):
    s = x.shape[0]
    nt = s // tm

    def body(dx1_ref, dx1b_ref, x_ref, z_ref, u1_ref, pooled_ref, win_ref, wout_ref, gmix_ref, cw_ref,
             lng_ref, lnb_ref, pw_ref, ps_ref,
             dx_ref, dz_ref, vec_ref, dcw_ref, dpw_ref, dubuf, dvbuf, u0buf, du0buf, dush, pbuf):
        i = pl.program_id(0)
        tile = nt - 1 - i

        @pl.when(i == 0)
        def _():
            vec_ref[...] = jnp.zeros_like(vec_ref)
            dcw_ref[...] = jnp.zeros_like(dcw_ref)
            dpw_ref[...] = jnp.zeros_like(dpw_ref)
            dubuf[tm:tm + CONV_HALO, :] = jnp.zeros((CONV_HALO, C_CONV), F32)
            dvbuf[tm:tm + POOL_HALO, :] = jnp.zeros((POOL_HALO, C_POOL), F32)

        dmix = _dot_nt(dx1b_ref[...], wout_ref[...])
        du = dmix[:, :C_CONV]
        dq = dmix[:, C_CONV:]

        pos1 = (tile * tm + lax.broadcasted_iota(jnp.int32, (tm, 1), 0) + 1).astype(F32)
        dpooled_parts = []
        dps_rows = []
        for g, w in enumerate(POOL_WINDOWS):
            cols = slice(g * POOL_GROUP, (g + 1) * POOL_GROUP)
            pooled_b = pooled_ref[:, cols]
            mixed = _dot(pooled_b, pw_ref[g])
            dqg = dq[:, cols]
            dps_rows.append(jnp.sum(dqg * mixed, axis=0, keepdims=True))
            dmixed = (dqg * ps_ref[:, cols]).astype(BF16)
            dpw_ref[g] += _dot_tn(pooled_b, dmixed)
            dpooled = _dot_nt(dmixed, pw_ref[g])
            dpooled_parts.append(dpooled)
            dvbuf[0:tm, cols] = dpooled / jnp.minimum(pos1, float(w))
        vec_ref[4:5, 0:C_POOL] += jnp.concatenate(dps_rows, axis=-1)
        dv_parts = []
        for g, w in enumerate(POOL_WINDOWS):
            cols = slice(g * POOL_GROUP, (g + 1) * POOL_GROUP)
            tot = _window_sums(dvbuf, cols, pbuf, g, tm, trailing=False)
            dv_parts.append(tot - dpooled_parts[g])

        u1 = u1_ref[...]
        mu = jnp.mean(u1, axis=-1, keepdims=True)
        cen = u1 - mu
        rstd = lax.rsqrt(jnp.mean(cen * cen, axis=-1, keepdims=True) + EPS)
        xhat = cen * rstd
        u2 = xhat * lng_ref[...] + lnb_ref[...]
        sg2 = _sigmoid(u2)
        du2 = du * sg2 * (1.0 + u2 * (1.0 - sg2))
        vec_ref[1:2, 0:C_CONV] += jnp.sum(du2 * xhat, axis=0, keepdims=True)
        vec_ref[2:3, 0:C_CONV] += jnp.sum(du2, axis=0, keepdims=True)
        t1 = du2 * lng_ref[...]
        du1 = rstd * (t1 - jnp.mean(t1, axis=-1, keepdims=True)
                      - xhat * jnp.mean(t1 * xhat, axis=-1, keepdims=True))
        vec_ref[3:4, 0:C_CONV] += jnp.sum(du1, axis=0, keepdims=True)
        dubuf[0:tm, :] = du1

        zt = z_ref[...]
        a = zt[:, :C_CONV]
        sgb = _sigmoid(zt[:, C_CONV:2 * C_CONV])
        u0buf[...] = a * sgb

        _shifted_copies(dubuf, dush, tm)
        for rc in range(tm // ROW_CHUNK):
            r0 = rc * ROW_CHUNK
            acc = jnp.zeros((ROW_CHUNK, C_CONV), F32)
            for k in range(CONV_K):
                acc = acc + cw_ref[k:k + 1, :] * _rows_at(dubuf, dush, r0 + (CONV_K - 1) - k)
            du0buf[r0:r0 + ROW_CHUNK, :] = acc
        for k in range(CONV_K):
            acc = jnp.zeros((ROW_CHUNK, C_CONV), F32)
            for rc in range(tm // ROW_CHUNK):
                r0 = rc * ROW_CHUNK
                acc = acc + u0buf[r0:r0 + ROW_CHUNK, :] * _rows_at(dubuf, dush, r0 + (CONV_K - 1) - k)
            dcw_ref[k:k + 1, :] += jnp.sum(acc, axis=0, keepdims=True)
        du0 = du0buf[...]

        da = du0 * sgb
        db = du0 * a * sgb * (1.0 - sgb)
        dz = jnp.concatenate([da, db] + dv_parts, axis=-1).astype(BF16)
        dz_ref[...] = dz

        xt = x_ref[...]
        r1 = lax.rsqrt(jnp.mean(xt * xt, axis=-1, keepdims=True) + EPS)
        dh1 = _dot(dz, win_ref[...])
        dxn, dgm = _rms_bwd(xt, r1, gmix_ref[...], dh1)
        dx_ref[...] = dx1_ref[...] + dxn
        vec_ref[0:1, :] += dgm

        dubuf[tm:tm + CONV_HALO, :] = dubuf[0:CONV_HALO, :]
        dvbuf[tm:tm + POOL_HALO, :] = dvbuf[0:POOL_HALO, :]

    rev = lambda width: pl.BlockSpec((tm, width), lambda i: (nt - 1 - i, 0))
    return _pcall(
        body,
        grid=(nt,),
        in_specs=[
            rev(D_MODEL), rev(D_MODEL), rev(D_MODEL), rev(Z_WIDTH), rev(C_CONV), rev(C_POOL),
            _const_spec((Z_WIDTH, D_MODEL)),
            _const_spec((D_MODEL, D_MODEL)),
            _const_spec((1, D_MODEL)),
            _const_spec((CONV_HALO, C_CONV)),
            _const_spec((1, C_CONV)),
            _const_spec((1, C_CONV)),
            _const_spec((len(POOL_WINDOWS), POOL_GROUP, POOL_GROUP)),
            _const_spec((1, C_POOL)),
        ],
        out_specs=[
            rev(D_MODEL), rev(Z_WIDTH),
            _const_spec((8, D_MODEL)),
            _const_spec((CONV_HALO, C_CONV)),
            _const_spec((len(POOL_WINDOWS), POOL_GROUP, POOL_GROUP)),
        ],
        out_shape=[
            jax.ShapeDtypeStruct((s, D_MODEL), F32),
            jax.ShapeDtypeStruct((s, Z_WIDTH), BF16),
            jax.ShapeDtypeStruct((8, D_MODEL), F32),
            jax.ShapeDtypeStruct((CONV_HALO, C_CONV), F32),
            jax.ShapeDtypeStruct((len(POOL_WINDOWS), POOL_GROUP, POOL_GROUP), F32),
        ],
        scratch_shapes=[
            pltpu.VMEM((tm + CONV_HALO, C_CONV), F32),
            pltpu.VMEM((tm + POOL_HALO, C_POOL), F32),
            pltpu.VMEM((tm, C_CONV), F32),
            pltpu.VMEM((tm, C_CONV), F32),
            pltpu.VMEM((7, tm + CONV_HALO, C_CONV), F32),
            pltpu.VMEM((2, tm + POOL_HALO, POOL_GROUP), F32),
        ],
        name="bwd_mix",
        args=(dx1, dx1b, x, z, u1, pooled, w_in_t, w_out, g_mix, conv_w, ln_g, ln_b, pool_w, pool_scale),
        carry=carry,
    )


def _grad_matmul(a, b, bm, name, a2=None, carry=None):
    s, ma = a.shape
    nb = b.shape[1]
    na = ma // bm
    if a2 is None:
        def body(a_ref, b_ref, o_ref):
            o_ref[...] = _dot_tn(a_ref[...], b_ref[...]).astype(BF16)

        lhs_specs = [pl.BlockSpec((s, bm), lambda i: (0, i))]
        lhs = (a,)
        steps = na
    else:
        def body(a_ref, a2_ref, b_ref, o_ref):
            i = pl.program_id(0)

            @pl.when(i < na)
            def _():
                o_ref[...] = _dot_tn(a_ref[...], b_ref[...]).astype(BF16)

            @pl.when(i >= na)
            def _():
                o_ref[...] = _dot_tn(a2_ref[...], b_ref[...]).astype(BF16)

        lhs_specs = [pl.BlockSpec((s, bm), lambda i: (0, jnp.minimum(i, na - 1))),
                     pl.BlockSpec((s, bm), lambda i: (0, jnp.maximum(i - na, 0)))]
        lhs = (a, a2)
        steps = 2 * na

    outs, carried = _pcall(
        body,
        grid=(steps,),
        in_specs=lhs_specs + [pl.BlockSpec((s, nb), lambda i: (0, 0))],
        out_specs=[pl.BlockSpec((bm, nb), lambda i: (i, 0))],
        out_shape=[jax.ShapeDtypeStruct((steps * bm, nb), BF16)],
        scratch_shapes=[],
        name=name,
        args=lhs + (b,),
        carry=carry,
    )
    return outs[0], carried


def _reduce_scatter(grads, small, name):
    n, ns = len(grads), len(small)
    shapes = [g.shape[1:] for g in grads]
    gather = _Carry()
    _carry_gather(gather, small, 1, 2)

    def body(*refs):
        g = refs[:n]
        out = refs[n + ns:2 * n + ns]
        scr = refs[2 * (n + ns):]
        own, loc, r1, r2 = scr[:n], scr[n:2 * n], scr[2 * n:3 * n], scr[3 * n:4 * n]
        load_sems, s1, q1, s2, q2 = scr[4 * n:4 * n + 5]
        gather_refs = (refs[n:n + ns], refs[2 * n + ns:2 * (n + ns)], scr[4 * n + 5:])
        me, sib, chips = _place()
        c = me[2]

        _handshake(EVERY_PEER)
        gather.stage(0, *gather_refs, handshake=False)
        loads = []
        sends = []
        for a in range(n):
            ld = [pltpu.make_async_copy(g[a].at[_block(*chip, c)], loc[a].at[j], load_sems.at[a, j])
                  for j, chip in enumerate(chips)]
            ld.append(pltpu.make_async_copy(g[a].at[_block(*me)], own[a], load_sems.at[a, 3]))
            for cp in ld:
                cp.start()
            loads.append(ld)
            blocks = [(*chip, 1 - c) for chip in chips] + [sib]
            for j, blk in enumerate(blocks):
                cp = pltpu.make_async_remote_copy(
                    src_ref=g[a].at[_block(*blk)], dst_ref=r1[a].at[j],
                    send_sem=s1.at[a, j], recv_sem=q1.at[a, j], device_id=sib, device_id_type=MESH)
                cp.start()
                sends.append(cp)

        def from_sibling(a, j):
            return pltpu.make_async_remote_copy(
                src_ref=r1[a].at[j], dst_ref=r1[a].at[j], send_sem=s1.at[a, j], recv_sem=q1.at[a, j],
                device_id=sib, device_id_type=MESH)

        def partial(a, j, chip):
            return pltpu.make_async_remote_copy(
                src_ref=loc[a].at[j], dst_ref=r2[a].at[j], send_sem=s2.at[a, j], recv_sem=q2.at[a, j],
                device_id=(*chip, c), device_id_type=MESH)

        gather.stage(1, *gather_refs)
        for a in range(n):
            for j, chip in enumerate(chips):
                loads[a][j].wait()
                from_sibling(a, j).wait_recv()
                loc[a][j] = (loc[a][j].astype(F32) + r1[a][j].astype(F32)).astype(BF16)
                cp = partial(a, j, chip)
                cp.start()
                sends.append(cp)
        gather.stage(2, *gather_refs)
        for a in range(n):
            loads[a][3].wait()
            from_sibling(a, 3).wait_recv()
            acc = own[a][...].astype(F32) + r1[a][3].astype(F32)
            for j, chip in enumerate(chips):
                partial(a, j, chip).wait_recv()
                acc = acc + r2[a][j].astype(F32)
            out[a][...] = acc
        for cp in sends:
            cp.wait_send()
        gather.drain(*gather_refs)

    any_spec = pl.BlockSpec(memory_space=pl.ANY)
    vmem_spec = pl.BlockSpec(memory_space=pltpu.VMEM)
    res = pl.pallas_call(
        body,
        in_specs=[any_spec] * (n + ns),
        out_specs=[vmem_spec] * n + [any_spec] * ns,
        out_shape=[jax.ShapeDtypeStruct(sh, F32) for sh in shapes] + gather.out_shapes,
        scratch_shapes=(
            [pltpu.VMEM(sh, BF16) for sh in shapes]
            + [pltpu.VMEM((3,) + sh, BF16) for sh in shapes]
            + [pltpu.VMEM((4,) + sh, BF16) for sh in shapes]
            + [pltpu.VMEM((3,) + sh, BF16) for sh in shapes]
            + [pltpu.SemaphoreType.DMA((n, 4)),
               pltpu.SemaphoreType.DMA((n, 4)), pltpu.SemaphoreType.DMA((n, 4)),
               pltpu.SemaphoreType.DMA((n, 3)), pltpu.SemaphoreType.DMA((n, 3))]
            + gather.sem_shapes()
        ),
        compiler_params=_params(collective_id=_collective_id(EVERY_PEER)),
        name=name,
    )(*grads, *small)
    return res[:n], res[n:]


def _pair_add(grads, from_sib, blks, name):
    n = len(grads)

    def body(blk_ref, *refs):
        for a in range(n):
            refs[2 * n + a][...] = (refs[a][...].astype(F32) + refs[n + a][...].astype(F32)).astype(BF16)

    mine = [pl.BlockSpec((None,) + g.shape[1:], lambda j, b: (b[j], 0, 0)) for g in grads]
    same = [pl.BlockSpec((None,) + g.shape[1:], lambda j, b: (j, 0, 0)) for g in grads]
    return pl.pallas_call(
        body,
        grid_spec=pltpu.PrefetchScalarGridSpec(
            num_scalar_prefetch=1, grid=(4,), in_specs=mine + same, out_specs=same),
        out_shape=[jax.ShapeDtypeStruct((4,) + g.shape[1:], BF16) for g in grads],
        compiler_params=_params(("arbitrary",)),
        name=name,
    )(blks, *grads, *from_sib)


def _chip_sum(parts, from_chips, name):
    n = len(parts)

    def body(*refs):
        for a in range(n):
            acc = refs[a][...].astype(F32)
            for j in range(3):
                acc = acc + refs[n + a][j].astype(F32)
            refs[2 * n + a][...] = acc

    half = [p.shape[1] // 2 for p in parts]
    return pl.pallas_call(
        body,
        grid=(2,),
        in_specs=([pl.BlockSpec((None, h, p.shape[2]), lambda i: (3, i, 0)) for p, h in zip(parts, half)]
                  + [pl.BlockSpec((3, h, p.shape[2]), lambda i: (0, i, 0)) for p, h in zip(parts, half)]),
        out_specs=[pl.BlockSpec((h, p.shape[2]), lambda i: (i, 0)) for p, h in zip(parts, half)],
        out_shape=[jax.ShapeDtypeStruct(p.shape[1:], F32) for p in parts],
        compiler_params=_params(("arbitrary",)),
        name=name,
    )(*parts, *from_chips)


def _adam_math(w, g, m, v):
    nm = ADAM_B1 * m + (1.0 - ADAM_B1) * g
    nv = ADAM_B2 * v + (1.0 - ADAM_B2) * (g * g)
    m_hat = nm / (1.0 - ADAM_B1 ** ADAM_STEP)
    v_hat = nv / (1.0 - ADAM_B2 ** ADAM_STEP)
    return -ADAM_LR * (m_hat / (jnp.sqrt(v_hat) + ADAM_EPS) + ADAM_WD * w), nm, nv


def _sum_adamw(items, name, carry=None):
    n = len(items)

    def body(*refs):
        for a in range(n):
            p_ref, f_ref, w_ref, m_ref, v_ref = refs[5 * a:5 * a + 5]
            g_ref, d_ref, nm_ref, nv_ref = refs[5 * n + 4 * a:5 * n + 4 * a + 4]
            g = p_ref[...].astype(F32)
            for j in range(3):
                g = g + f_ref[j].astype(F32)
            g_ref[...] = g
            d_ref[...], nm_ref[...], nv_ref[...] = _adam_math(w_ref[...], g, m_ref[...], v_ref[...])

    in_specs, out_specs, out_shape, args = [], [], [], []
    for part, from_chips, w, m, v in items:
        r, c = w.shape
        spec = pl.BlockSpec((r // 2, c), lambda i: (i, 0))
        in_specs += [pl.BlockSpec((None, r // 2, c), lambda i: (3, i, 0)),
                     pl.BlockSpec((3, r // 2, c), lambda i: (0, i, 0)), spec, spec, spec]
        out_specs += [spec] * 4
        out_shape += [jax.ShapeDtypeStruct((r, c), F32)] * 4
        args += [part, from_chips, w, m, v]
    outs, carried = _pcall(body, grid=(2,), in_specs=in_specs, out_specs=out_specs, out_shape=out_shape,
                           scratch_shapes=[], name=name, args=tuple(args), carry=carry)
    return [tuple(outs[4 * a:4 * a + 4]) for a in range(n)], carried


def _small_update(gathered, layout, params, name):
    ng, npar = len(gathered), len(params)

    def body(*refs):
        parts = refs[:ng]
        prm = refs[ng:ng + 3 * npar]
        tot_refs = refs[ng + 3 * npar:2 * ng + 3 * npar]
        out = refs[2 * ng + 3 * npar:]
        tots = []
        for a in range(ng):
            acc = parts[a][0].astype(F32)
            for d in range(1, N_DEV):
                acc = acc + parts[a][d].astype(F32)
            tot_refs[a][...] = acc
            tots.append(acc)
        for i, (a, row, width) in enumerate(layout):
            g = tots[a] if row is None else tots[a][row:row + 1, :width]
            delta, nm, nv = _adam_math(prm[3 * i][...], g, prm[3 * i + 1][...], prm[3 * i + 2][...])
            out[4 * i][...] = g
            out[4 * i + 1][...] = delta
            out[4 * i + 2][...] = nm
            out[4 * i + 3][...] = nv

    flat = [t for prm in params for t in prm]
    res = pl.pallas_call(
        body,
        out_shape=([jax.ShapeDtypeStruct(g.shape[1:], F32) for g in gathered]
                   + [jax.ShapeDtypeStruct(prm[0].shape, F32) for prm in params for _ in range(4)]),
        compiler_params=pltpu.CompilerParams(vmem_limit_bytes=V7X_VMEM_LIMIT),
        name=name,
    )(*gathered, *flat)
    return res[:ng], [tuple(res[ng + 4 * i:ng + 4 * i + 4]) for i in range(npar)]


def _adamw(w, g, m, v, name):
    rows, cols = w.shape
    br = rows
    for cand in (512, 256, 128):
        if rows % cand == 0 and rows > cand:
            br = cand
            break

    def body(w_ref, g_ref, m_ref, v_ref, d_ref, nm_ref, nv_ref):
        d_ref[...], nm_ref[...], nv_ref[...] = _adam_math(w_ref[...], g_ref[...], m_ref[...], v_ref[...])

    spec = pl.BlockSpec((br, cols), lambda i: (i, 0))
    shape = jax.ShapeDtypeStruct((rows, cols), F32)
    return pl.pallas_call(
        body,
        grid=(rows // br,),
        in_specs=[spec] * 4,
        out_specs=[spec] * 3,
        out_shape=[shape] * 3,
        compiler_params=_params(("arbitrary",)),
        name=name,
    )(w, g, m, v)


def _by_device(full):
    return full.reshape(N_DEV, full.shape[0] // N_DEV, full.shape[1])


def kernel(x, p, g_mix, w_in, conv_w, conv_b, ln_g, ln_b, pool_w, pool_scale, w_out, g_ffn, w_gate_up, w_down, g_ple_gate, w_ple_gate, w_ple_up, g_ple_post, g_final, loss_target, m_g_mix, m_w_in, m_conv_w, m_conv_b, m_ln_g, m_ln_b, m_pool_w, m_pool_scale, m_w_out, m_g_ffn, m_w_gate_up, m_w_down, m_g_ple_gate, m_w_ple_gate, m_w_ple_up, m_g_ple_post, m_g_final, v_g_mix, v_w_in, v_conv_w, v_conv_b, v_ln_g, v_ln_b, v_pool_w, v_pool_scale, v_w_out, v_g_ffn, v_w_gate_up, v_w_down, v_g_ple_gate, v_w_ple_gate, v_w_ple_up, v_g_ple_post, v_g_final):
    seq = x.shape[1]
    tile = min(TOKEN_TILE, seq)
    xs = x[0]
    ps = p[0, 0]
    tgt = loss_target[0]
    ax, ay, ac = lax.axis_index("x"), lax.axis_index("y"), lax.axis_index("c")
    me = _block(ax, ay, ac)
    blks = jnp.stack([_block(1 - ax, ay, ac), _block(ax, 1 - ay, ac), _block(1 - ax, 1 - ay, ac), me]).astype(jnp.int32)
    rows = lambda gth: gth.reshape((-1,) + gth.shape[2:])

    sh_in, sh_out, sh_gu, sh_down, sh_pg, sh_pu, pool_w_b = _to_bf16([
        w_in[0].T,
        w_out[0],
        w_gate_up[0].T,
        w_down[0],
        w_ple_gate[0],
        w_ple_up[0].T,
        pool_w[0]])
    carry = _Carry()
    _carry_gather(carry, [sh_in, sh_out,
                          jnp.pad(conv_w[0].T, ((0, 0), (0, CONV_HALO - CONV_K)))], 8, 15)
    (h1,), first = _norm_in(xs, g_mix, tile, carry)
    w_in_t, w_out_f, conv_w_t = [rows(gth) for gth in first]
    conv_w_f = conv_w_t.T

    carry = _Carry()
    _carry_gather(carry, [sh_gu], 8, 12)
    (z, u1, pooled, x1, mix), (w_gu_all,) = _fwd_mix(
        xs, h1, w_in_t, w_out_f, conv_w_f, conv_b, ln_g, ln_b, pool_w_b, pool_scale, tile, carry)
    w_gu_t = rows(w_gu_all)

    carry = _Carry()
    _carry_gather(carry, [sh_down, sh_pg, sh_pu], 6, 10)
    (h2, g_sav, u_sav, act), late = _ffn_up(x1, w_gu_t, g_ffn, tile, carry)
    w_down_f, w_pg_f, w_pu_t = [rows(gth) for gth in late]
    dx2, dx2b, hg, ds, dpe, pb, stats_ple = _down_ple(
        x1, act, w_down_f, ps, tgt, w_pg_f, w_pu_t, g_ple_gate, g_ple_post, g_final.reshape(1, D_MODEL),
        tile)

    d_w_down, _ = _grad_matmul(act, dx2b, GRAD_ROWS, "grad_w_down")
    d_w_pg, _ = _grad_matmul(hg, ds, GRAD_ROWS, "grad_w_ple_gate")
    d_w_pu_t, _ = _grad_matmul(dpe, pb, GRAD_ROWS, "grad_w_ple_up")
    early = [_by_device(d_w_pg), _by_device(d_w_pu_t), _by_device(d_w_down)]
    carry = _Carry()
    _carry_pair(carry, early)
    (dg, du, dx1, dx1b, stats_ffn), from_sib = _ffn_bwd(dx2, dx2b, x1, g_sav, u_sav, w_gu_t, w_down_f, g_ffn,
                                                        tile, carry)
    early_parts = _pair_add(early, from_sib, blks, "pair_add_early")

    carry = _Carry()
    _carry_chip(carry, early_parts)
    d_w_gu_t, early_chips = _grad_matmul(dg, h2, GRAD_ROWS, "grad_w_gate_up", a2=du, carry=carry)
    carry = _Carry()
    _carry_pair(carry, [_by_device(d_w_gu_t)])
    d_w_out, gu_sib = _grad_matmul(mix, dx1b, GRAD_ROWS, "grad_w_out", carry=carry)
    gu_parts = _pair_add([_by_device(d_w_gu_t)], gu_sib, blks, "pair_add_gate_up")

    carry = _Carry()
    _carry_chip(carry, gu_parts)
    _carry_pair(carry, [_by_device(d_w_out)])
    (dx, dz, vec_mix, dconv_w_part, dpool_w_part), carried = _bwd_mix(
        dx1, dx1b, xs, z, u1, pooled, w_in_t, w_out_f, g_mix, conv_w_f, ln_g, ln_b, pool_w_b, pool_scale,
        tile, carry)
    gu_chips, out_sib = carried[:1], carried[1:]
    out_parts = _pair_add([_by_device(d_w_out)], out_sib, blks, "pair_add_out")

    carry = _Carry()
    _carry_chip(carry, out_parts)
    d_w_in_t, out_chips = _grad_matmul(dz, h1, GRAD_ROWS, "grad_w_in", carry=carry)
    (gr_w_in_t,), small = _reduce_scatter(
        [_by_device(d_w_in_t)],
        [vec_mix, stats_ple, stats_ffn, dconv_w_part,
         dpool_w_part.astype(BF16).reshape(-1, POOL_GROUP)],
        "scatter_last")
    small = list(small[:4]) + [small[4].reshape((N_DEV,) + dpool_w_part.shape)]

    vec_names = ["g_mix", "ln_g", "ln_b", "conv_b", "pool_scale", "g_final", "g_ple_post", "g_ple_gate", "g_ffn",
                 "pool_w"]
    layout = [(0, 0, D_MODEL), (0, 1, C_CONV), (0, 2, C_CONV), (0, 3, C_CONV), (0, 4, C_POOL),
              (1, 0, D_MODEL), (1, 1, D_MODEL), (1, 2, D_MODEL), (2, 0, D_MODEL), (4, None, None)]
    as_row = lambda t: t.reshape(1, D_MODEL)
    params = [(g_mix, m_g_mix, v_g_mix), (ln_g, m_ln_g, v_ln_g), (ln_b, m_ln_b, v_ln_b),
              (conv_b, m_conv_b, v_conv_b), (pool_scale, m_pool_scale, v_pool_scale),
              (as_row(g_final), as_row(m_g_final), as_row(v_g_final)),
              (g_ple_post, m_g_ple_post, v_g_ple_post), (g_ple_gate, m_g_ple_gate, v_g_ple_gate),
              (g_ffn, m_g_ffn, v_g_ffn), (pool_w[0], m_pool_w[0], v_pool_w[0])]
    tots, small_upd = _small_update(small, layout, params, "small_update")
    loss = tots[1][3, 0]
    upd = {}
    for nm, res, prm in zip(vec_names, small_upd, [g_mix, ln_g, ln_b, conv_b, pool_scale, g_final, g_ple_post,
                                                    g_ple_gate, g_ffn, pool_w]):
        upd[nm] = tuple(t.reshape(prm.shape) for t in res)
    gr_conv_w = lax.dynamic_slice_in_dim(tots[3][:CONV_K], me * (C_CONV // N_DEV), C_CONV // N_DEV, axis=1)

    natural = lambda t: t[None]
    turned = lambda t: t.T[None]
    (res_pg, res_gu, res_out, res_down), _ = _sum_adamw([
        (early_parts[0], early_chips[0], w_ple_gate[0], m_w_ple_gate[0], v_w_ple_gate[0]),
        (gu_parts[0], gu_chips[0], w_gate_up[0].T, m_w_gate_up[0].T, v_w_gate_up[0].T),
        (out_parts[0], out_chips[0], w_out[0], m_w_out[0], v_w_out[0]),
        (early_parts[2], early_chips[2], w_down[0], m_w_down[0], v_w_down[0])], "adamw_big")
    upd["w_ple_gate"] = tuple(natural(t) for t in res_pg)
    upd["w_gate_up"] = tuple(turned(t) for t in res_gu)
    upd["w_out"] = tuple(natural(t) for t in res_out)
    upd["w_down"] = tuple(natural(t) for t in res_down)
    (gr_w_pu_t,) = _chip_sum(early_parts[1:2], early_chips[1:2], "chip_sum")
    plain = [
        ("w_in", w_in[0].T, gr_w_in_t, m_w_in[0].T, v_w_in[0].T, True),
        ("w_ple_up", w_ple_up[0], gr_w_pu_t.T, m_w_ple_up[0], v_w_ple_up[0], False),
        ("conv_w", conv_w[0], gr_conv_w, m_conv_w[0], v_conv_w[0], False),
    ]
    for nm, w_, g_, m_, v_, transposed in plain:
        res = (g_,) + tuple(_adamw(w_, g_, m_, v_, "adamw_" + nm))
        upd[nm] = tuple((t.T if transposed else t)[None] for t in res)

    order = ["g_mix", "w_in", "conv_w", "conv_b", "ln_g", "ln_b", "pool_w", "pool_scale", "w_out", "g_ffn",
             "w_gate_up", "w_down", "g_ple_gate", "w_ple_gate", "w_ple_up", "g_ple_post", "g_final"]
    outs = [loss, dx[None]]
    for k in range(4):
        outs += [upd[nm][k] for nm in order]
    return tuple(outs)
```

```python
import functools

import jax
import jax.numpy as jnp
from jax import lax
from jax.experimental import pallas as pl
from jax.experimental.pallas import tpu as pltpu

D_MODEL = 1024
C_CONV = 512
C_POOL = 512
Z_WIDTH = 2 * C_CONV + C_POOL
POOL_WINDOWS = (2, 4, 8, 16)
POOL_GROUP = 128
CONV_K = 31
D_FF = 2816
D_PLE = 256
EPS = 1e-6
N_DEV = 8

ADAM_LR = 0.001
ADAM_B1 = 0.9
ADAM_B2 = 0.999
ADAM_EPS = 1e-08
ADAM_WD = 0.01
ADAM_STEP = 10

CONV_HALO = 32
POOL_HALO = 32
ROW_CHUNK = 32
TOKEN_TILE = 256
GRAD_ROWS = 256
V7X_VMEM_LIMIT = 56 * 1024 * 1024
FF_CHUNK = D_FF // 2

BF16 = jnp.bfloat16
F32 = jnp.float32
MESH = pl.DeviceIdType.MESH


def _dot(a, b):
    return lax.dot_general(a, b, (((1,), (0,)), ((), ())), preferred_element_type=F32)


def _dot_nt(a, b):
    return lax.dot_general(a, b, (((1,), (1,)), ((), ())), preferred_element_type=F32)


def _dot_tn(a, b):
    return lax.dot_general(a, b, (((0,), (0,)), ((), ())), preferred_element_type=F32)


def _rms_fwd(x, g):
    r = lax.rsqrt(jnp.mean(x * x, axis=-1, keepdims=True) + EPS)
    return x * r * g, r


def _rms_bwd(x, r, g, dy):
    xr = x * r
    dg = jnp.sum(dy * xr, axis=0, keepdims=True)
    dyg = dy * g
    dx = r * (dyg - xr * jnp.mean(dyg * xr, axis=-1, keepdims=True))
    return dx, dg


def _sigmoid(x):
    return jax.nn.sigmoid(x)


def _params(sem=None, collective_id=None):
    return pltpu.CompilerParams(dimension_semantics=sem, vmem_limit_bytes=V7X_VMEM_LIMIT,
                                collective_id=collective_id)


def _place():
    x, y, c = lax.axis_index("x"), lax.axis_index("y"), lax.axis_index("c")
    chips = [(1 - x, y), (x, 1 - y), (1 - x, 1 - y)]
    return (x, y, c), (x, y, 1 - c), chips


def _block(px, py, pc):
    return 4 * px + 2 * py + pc


EVERY_PEER = ("sib", 0, 1, 2)
PEER_SET_IDS = {("sib",): 0, ("0", "1", "sib"): 1, ("0", "1", "2"): 2, ("0", "1", "2", "sib"): 3}


def _collective_id(peers):
    return PEER_SET_IDS[tuple(sorted(str(p) for p in peers))]


def _handshake(peers):
    me, sib, chips = _place()
    barrier = pltpu.get_barrier_semaphore()
    for p in peers:
        to = sib if p == "sib" else (*chips[p], me[2])
        pl.semaphore_signal(barrier, inc=1, device_id=to, device_id_type=MESH)
    pl.semaphore_wait(barrier, len(peers))


class _Carry:
    def __init__(self):
        self.inputs = []
        self.out_shapes = []
        self.copies = []
        self.locals = []

    def add_input(self, arr):
        self.inputs.append(arr)
        return len(self.inputs) - 1

    def add_output(self, shape, dtype):
        self.out_shapes.append(jax.ShapeDtypeStruct(shape, dtype))
        return len(self.out_shapes) - 1

    def local(self, src_idx, dst_idx, dst_blk):
        self.locals.append((src_idx, dst_idx, dst_blk))

    def copy(self, src, dst_idx, dst_blk, got_blk, peer, step=0, after=()):
        self.copies.append(dict(src=src, dst_idx=dst_idx, dst_blk=dst_blk, got_blk=got_blk, peer=peer, step=step,
                                after=tuple(after)))
        return len(self.copies) - 1

    def sem_shapes(self):
        return [pltpu.SemaphoreType.DMA((max(1, len(self.copies)),)),
                pltpu.SemaphoreType.DMA((max(1, len(self.copies)),)),
                pltpu.SemaphoreType.DMA((max(1, len(self.locals)),))]

    @staticmethod
    def _view(ref, where):
        if isinstance(where, tuple):
            blk, row0, nrows = where
            return ref.at[blk, pl.ds(row0, nrows)]
        return ref.at[where]

    def _desc(self, k, ins, outs, sems, place):
        cp = self.copies[k]
        me, sib, chips = place
        kind, idx, blk = cp["src"]
        src = (ins if kind == "in" else outs)[idx]
        if blk is not None:
            src = self._view(src, blk(*place))
        to = sib if cp["peer"] == "sib" else (*chips[cp["peer"]], me[2])
        return pltpu.make_async_remote_copy(
            src_ref=src, dst_ref=self._view(outs[cp["dst_idx"]], cp["dst_blk"](*place)),
            send_sem=sems[0].at[k], recv_sem=sems[1].at[k], device_id=to, device_id_type=MESH)

    def _arrival(self, k, outs, sems, place):
        cp = self.copies[k]
        got = self._view(outs[cp["dst_idx"]], cp["got_blk"](*place))
        return pltpu.make_async_remote_copy(
            src_ref=got, dst_ref=got, send_sem=sems[0].at[k], recv_sem=sems[1].at[k],
            device_id=place[0], device_id_type=MESH)

    def _local(self, n, ins, outs, sems, place):
        src_idx, dst_idx, blk = self.locals[n]
        return pltpu.make_async_copy(ins[src_idx], outs[dst_idx].at[blk(*place)], sems[2].at[n])

    def stages(self):
        return sorted({0} | {cp["step"] for cp in self.copies})

    def peers(self):
        return sorted({cp["peer"] for cp in self.copies}, key=str)

    def stage(self, s, ins, outs, sems, handshake=True):
        place = _place()
        if s == 0:
            if handshake:
                _handshake(self.peers())
            self._waited = set()
            for n in range(len(self.locals)):
                self._local(n, ins, outs, sems, place).start()
        for k, cp in enumerate(self.copies):
            if cp["step"] != s:
                continue
            for a in cp["after"]:
                if a not in self._waited:
                    self._arrival(a, outs, sems, place).wait_recv()
                    self._waited.add(a)
            self._desc(k, ins, outs, sems, place).start()

    def drain(self, ins, outs, sems):
        place = _place()
        for k in range(len(self.copies)):
            if k not in self._waited:
                self._arrival(k, outs, sems, place).wait_recv()
        for k in range(len(self.copies)):
            self._desc(k, ins, outs, sems, place).wait_send()
        for n in range(len(self.locals)):
            self._local(n, ins, outs, sems, place).wait()

    def starts(self, step, nsteps, ins, outs, sems):
        for s in self.stages():
            pl.when(step == min(s, nsteps - 1))(functools.partial(self.stage, s, ins, outs, sems))

    def finish(self, step, nsteps, ins, outs, sems):
        pl.when(step == nsteps - 1)(functools.partial(self.drain, ins, outs, sems))


def _const_blk(j):
    return lambda me, sib, chips: j


def _carry_gather(carry, shards, relay_step, last_step):
    outs = []
    for sh in shards:
        i = carry.add_input(sh)
        o = carry.add_output((N_DEV,) + sh.shape, sh.dtype)
        half = sh.shape[0] // 2
        tile = 16 if sh.dtype == BF16 else 8
        split = half % tile == 0
        rows = [(0, half), (half, sh.shape[0] - half)] if split else [(0, sh.shape[0]), None]

        def whole(j, core):
            return lambda me, sib, chips, j=j, core=core: _block(*chips[j], me[2] if core == 0 else 1 - me[2])

        def part(j, core, h, rows=rows):
            return lambda me, sib, chips: (_block(*chips[j], me[2] if core == 0 else 1 - me[2]),) + rows[h]

        mine = lambda me, sib, chips: _block(*me)
        carry.local(i, o, mine)
        carry.copy(("in", i, None), o, mine, lambda me, sib, chips: _block(*sib), "sib")
        near = [carry.copy(("in", i, None), o, mine, whole(j, 0), j) for j in range(2)]
        for j in range(2):
            carry.copy(("out", o, whole(j, 0)), o, whole(j, 0), whole(j, 1), "sib", step=relay_step, after=(near[j],))
        for j in range(2):
            if rows[j] is None:
                continue
            far = carry.copy(("out", o, part(j, 0, j)), o, part(j, 0, j), part(2, 0, j), 1 - j,
                             step=relay_step, after=(near[j],))
            carry.copy(("out", o, part(2, 0, j)), o, part(2, 0, j), part(2, 1, j), "sib", step=last_step, after=(far,))
        outs.append(o)
    return outs


def _carry_pair(carry, grads):
    outs = []
    for g in grads:
        i = carry.add_input(g)
        o = carry.add_output((4,) + g.shape[1:], g.dtype)
        for j in range(4):
            if j < 3:
                blk = lambda me, sib, chips, j=j: _block(*chips[j], 1 - me[2])
            else:
                blk = lambda me, sib, chips: _block(*sib)
            carry.copy(("in", i, blk), o, _const_blk(j), _const_blk(j), "sib")
        outs.append(o)
    return outs


def _carry_chip(carry, parts):
    outs = []
    for p in parts:
        i = carry.add_input(p)
        o = carry.add_output((3,) + p.shape[1:], p.dtype)
        for j in range(3):
            carry.copy(("in", i, _const_blk(j)), o, _const_blk(j), _const_blk(j), j)
        outs.append(o)
    return outs


def _pcall(body, *, grid, in_specs, out_specs, out_shape, scratch_shapes, name, args, carry=None):
    sem = ("arbitrary",) * len(grid)
    if carry is None:
        res = pl.pallas_call(body, grid=grid, in_specs=in_specs, out_specs=out_specs, out_shape=out_shape,
                             scratch_shapes=scratch_shapes, compiler_params=_params(sem), name=name)(*args)
        return list(res), []
    n_in, n_out, n_scr = len(in_specs), len(out_specs), len(scratch_shapes)
    c_in, c_out = len(carry.inputs), len(carry.out_shapes)
    nsteps = 1
    for extent in grid:
        nsteps *= extent

    def wrapped(*refs):
        ins = refs[:n_in]
        cins = refs[n_in:n_in + c_in]
        o0 = n_in + c_in
        outs = refs[o0:o0 + n_out]
        couts = refs[o0 + n_out:o0 + n_out + c_out]
        s0 = o0 + n_out + c_out
        scr = refs[s0:s0 + n_scr]
        sems = refs[s0 + n_scr:]
        step = pl.program_id(0)
        for d in range(1, len(grid)):
            step = step * grid[d] + pl.program_id(d)
        carry.starts(step, nsteps, cins, couts, sems)
        body(*ins, *outs, *scr)
        carry.finish(step, nsteps, cins, couts, sems)

    any_spec = pl.BlockSpec(memory_space=pl.ANY)
    res = pl.pallas_call(
        wrapped, grid=grid,
        in_specs=list(in_specs) + [any_spec] * c_in,
        out_specs=list(out_specs) + [any_spec] * c_out,
        out_shape=list(out_shape) + carry.out_shapes,
        scratch_shapes=list(scratch_shapes) + carry.sem_shapes(),
        compiler_params=_params(sem, _collective_id(carry.peers())), name=name)(*args, *carry.inputs)
    return list(res[:n_out]), list(res[n_out:])


def _shifted_copies(buf, shifted, tm):
    span = tm + CONV_HALO - 8
    for r in range(1, 8):
        shifted[r - 1, 0:span, :] = buf[r:r + span, :]


def _rows_at(buf, shifted, start):
    aligned, r = (start // 8) * 8, start % 8
    if r == 0:
        return buf[aligned:aligned + ROW_CHUNK, :]
    return shifted[r - 1, aligned:aligned + ROW_CHUNK, :]


def _window_sums(buf, cols, work, levels, tm, trailing):
    src, src_cols = buf, cols
    for k in range(levels + 1):
        shift = 1 << k
        dst = work.at[k % 2]
        if trailing:
            lo = 8 * (k + 1)
            dst[lo:tm + POOL_HALO, :] = (src[lo:tm + POOL_HALO, src_cols]
                                         + src[lo - shift:tm + POOL_HALO - shift, src_cols])
        else:
            hi = tm + POOL_HALO - 8 * (k + 1)
            dst[0:hi, :] = src[0:hi, src_cols] + src[shift:hi + shift, src_cols]
        src, src_cols = dst, slice(0, POOL_GROUP)
    return src[POOL_HALO:POOL_HALO + tm, src_cols] if trailing else src[0:tm, src_cols]


def _row_spec(tm, width):
    return pl.BlockSpec((tm, width), lambda i: (i, 0))


def _const_spec(shape):
    return pl.BlockSpec(shape, lambda i: (0,) * len(shape))


def _weight_spec(shape):
    return pl.BlockSpec(shape, lambda i: (0,) * len(shape), pipeline_mode=pl.Buffered(1))


def _to_bf16(arrays):
    n = len(arrays)

    def body(*refs):
        for a in range(n):
            refs[n + a][...] = refs[a][...].astype(BF16)

    whole = [pl.BlockSpec(arr.shape, lambda i, nd=arr.ndim: (0,) * nd) for arr in arrays]
    return pl.pallas_call(
        body,
        grid=(1,),
        in_specs=whole,
        out_specs=whole,
        out_shape=[jax.ShapeDtypeStruct(arr.shape, BF16) for arr in arrays],
        compiler_params=_params(("arbitrary",)),
        name="to_bf16",
    )(*arrays)


def _norm_in(x, g_mix, tm, carry=None):
    s = x.shape[0]

    def body(x_ref, gmix_ref, h1_ref):
        h, _ = _rms_fwd(x_ref[...], gmix_ref[...])
        h1_ref[...] = h.astype(BF16)

    return _pcall(
        body,
        grid=(s // tm,),
        in_specs=[_row_spec(tm, D_MODEL), _const_spec((1, D_MODEL))],
        out_specs=[_row_spec(tm, D_MODEL)],
        out_shape=[jax.ShapeDtypeStruct((s, D_MODEL), BF16)],
        scratch_shapes=[],
        name="norm_in",
        args=(x, g_mix),
        carry=carry,
    )


def _fwd_mix(x, h1, w_in_t, w_out, conv_w, conv_b, ln_g, ln_b, pool_w, pool_scale, tm, carry=None):
    s = x.shape[0]
    nt = s // tm

    def body(x_ref, h1_ref, win_ref, wout_ref, cw_ref, cb_ref, lng_ref, lnb_ref, pw_ref, ps_ref,
             z_ref, u1_ref, pooled_ref, x1_ref, mix_ref, ubuf, vbuf, ush, pbuf):
        i = pl.program_id(0)

        @pl.when(i == 0)
        def _():
            ubuf[0:CONV_HALO, :] = jnp.zeros((CONV_HALO, C_CONV), F32)
            vbuf[0:POOL_HALO, :] = jnp.zeros((POOL_HALO, C_POOL), F32)

        xt = x_ref[...]
        z = _dot_nt(h1_ref[...], win_ref[...])
        z_ref[...] = z
        a = z[:, :C_CONV]
        b = z[:, C_CONV:2 * C_CONV]
        v = z[:, 2 * C_CONV:]
        ubuf[CONV_HALO:CONV_HALO + tm, :] = a * _sigmoid(b)
        vbuf[POOL_HALO:POOL_HALO + tm, :] = v

        _shifted_copies(ubuf, ush, tm)
        for rc in range(tm // ROW_CHUNK):
            base = rc * ROW_CHUNK + CONV_HALO - (CONV_K - 1)
            acc = jnp.broadcast_to(cb_ref[...], (ROW_CHUNK, C_CONV))
            for k in range(CONV_K):
                acc = acc + cw_ref[k:k + 1, :] * _rows_at(ubuf, ush, base + k)
            u1_ref[rc * ROW_CHUNK:(rc + 1) * ROW_CHUNK, :] = acc

        u1 = u1_ref[...]
        mu = jnp.mean(u1, axis=-1, keepdims=True)
        cen = u1 - mu
        rstd = lax.rsqrt(jnp.mean(cen * cen, axis=-1, keepdims=True) + EPS)
        u2 = cen * rstd * lng_ref[...] + lnb_ref[...]
        u = u2 * _sigmoid(u2)

        pos1 = (i * tm + lax.broadcasted_iota(jnp.int32, (tm, 1), 0) + 1).astype(F32)
        parts = [u]
        for g, w in enumerate(POOL_WINDOWS):
            cols = slice(g * POOL_GROUP, (g + 1) * POOL_GROUP)
            vg = v[:, cols]
            tot = _window_sums(vbuf, cols, pbuf, g, tm, trailing=True)
            pooled = tot / jnp.minimum(pos1, float(w)) - vg
            pooled_b = pooled.astype(BF16)
            pooled_ref[:, cols] = pooled_b
            parts.append(_dot(pooled_b, pw_ref[g]) * ps_ref[:, cols])
        mix = jnp.concatenate(parts, axis=-1).astype(BF16)
        mix_ref[...] = mix
        x1_ref[...] = xt + _dot(mix, wout_ref[...])

        ubuf[0:CONV_HALO, :] = ubuf[tm:tm + CONV_HALO, :]
        vbuf[0:POOL_HALO, :] = vbuf[tm:tm + POOL_HALO, :]

    return _pcall(
        body,
        grid=(nt,),
        in_specs=[
            _row_spec(tm, D_MODEL),
            _row_spec(tm, D_MODEL),
            _const_spec((Z_WIDTH, D_MODEL)),
            _const_spec((D_MODEL, D_MODEL)),
            _const_spec((CONV_HALO, C_CONV)),
            _const_spec((1, C_CONV)),
            _const_spec((1, C_CONV)),
            _const_spec((1, C_CONV)),
            _const_spec((len(POOL_WINDOWS), POOL_GROUP, POOL_GROUP)),
            _const_spec((1, C_POOL)),
        ],
        out_specs=[
            _row_spec(tm, Z_WIDTH),
            _row_spec(tm, C_CONV),
            _row_spec(tm, C_POOL),
            _row_spec(tm, D_MODEL),
            _row_spec(tm, D_MODEL),
        ],
        out_shape=[
            jax.ShapeDtypeStruct((s, Z_WIDTH), F32),
            jax.ShapeDtypeStruct((s, C_CONV), F32),
            jax.ShapeDtypeStruct((s, C_POOL), BF16),
            jax.ShapeDtypeStruct((s, D_MODEL), F32),
            jax.ShapeDtypeStruct((s, D_MODEL), BF16),
        ],
        scratch_shapes=[
            pltpu.VMEM((tm + CONV_HALO, C_CONV), F32),
            pltpu.VMEM((tm + POOL_HALO, C_POOL), F32),
            pltpu.VMEM((7, tm + CONV_HALO, C_CONV), F32),
            pltpu.VMEM((2, tm + POOL_HALO, POOL_GROUP), F32),
        ],
        name="fwd_mix",
        args=(x, h1, w_in_t, w_out, conv_w, conv_b, ln_g, ln_b, pool_w, pool_scale),
        carry=carry,
    )


def _ffn_up(x1, w_gu_t, g_ffn, tm, carry=None):
    s = x1.shape[0]

    def body(x1_ref, w_ref, gffn_ref, h2_ref, g_ref, u_ref, act_ref):
        h, _ = _rms_fwd(x1_ref[...], gffn_ref[...])
        h2 = h.astype(BF16)
        h2_ref[...] = h2
        for c in range(D_FF // FF_CHUNK):
            cols = slice(c * FF_CHUNK, (c + 1) * FF_CHUNK)
            g = _dot_nt(h2, w_ref[c * FF_CHUNK:(c + 1) * FF_CHUNK, :])
            u = _dot_nt(h2, w_ref[D_FF + c * FF_CHUNK:D_FF + (c + 1) * FF_CHUNK, :])
            g_ref[:, cols] = g.astype(BF16)
            u_ref[:, cols] = u.astype(BF16)
            act_ref[:, cols] = (g * _sigmoid(g) * u).astype(BF16)

    return _pcall(
        body,
        grid=(s // tm,),
        in_specs=[_row_spec(tm, D_MODEL), _weight_spec((2 * D_FF, D_MODEL)), _const_spec((1, D_MODEL))],
        out_specs=[_row_spec(tm, D_MODEL), _row_spec(tm, D_FF), _row_spec(tm, D_FF), _row_spec(tm, D_FF)],
        out_shape=[
            jax.ShapeDtypeStruct((s, D_MODEL), BF16),
            jax.ShapeDtypeStruct((s, D_FF), BF16),
            jax.ShapeDtypeStruct((s, D_FF), BF16),
            jax.ShapeDtypeStruct((s, D_FF), BF16),
        ],
        scratch_shapes=[],
        name="ffn_up",
        args=(x1, w_gu_t, g_ffn),
        carry=carry,
    )


def _down_ple(x1, act, w_down, p, tgt, w_pg, w_pu_t, g_gate, g_post, g_final, tm):
    s = x1.shape[0]
    nt = s // tm

    def body(x1_ref, act_ref, wd_ref, p_ref, t_ref, wpg_ref, wpu_ref, gg_ref, gp_ref, gf_ref,
             dx2_ref, dx2b_ref, hg_ref, ds_ref, dpe_ref, pb_ref, stats_ref, x2_cur, x2_next):
        i = pl.program_id(0)

        @pl.when(i == 0)
        def _():
            stats_ref[...] = jnp.zeros_like(stats_ref)
            x2_cur[...] = jnp.zeros((tm, D_MODEL), F32)

        def down(c):
            cols = slice(c * 256, (c + 1) * 256)
            x2_next[:, cols] = x1_ref[:, cols] + _dot(act_ref[...], wd_ref[:, cols])

        x2 = x2_cur[...]
        counts = i >= 1

        hg, rg = _rms_fwd(x2, gg_ref[...])
        hg_b = hg.astype(BF16)
        hg_ref[...] = hg_b
        down(0)
        gate = _sigmoid(_dot(hg_b, wpg_ref[...]))
        pb = p_ref[...].astype(BF16)
        pb_ref[...] = pb
        pe = _dot_nt(pb, wpu_ref[...])
        e, rp = _rms_fwd(pe, gp_ref[...])
        x3 = x2 + gate * e
        down(1)
        y, r3 = _rms_fwd(x3, gf_ref[...])
        diff = y - t_ref[...]
        loss = 0.5 * jnp.sum(jnp.sum(diff * diff, axis=-1, keepdims=True), axis=0, keepdims=True) / D_MODEL
        dy = diff * (1.0 / D_MODEL)

        dx3, dgf = _rms_bwd(x3, r3, gf_ref[...], dy)
        down(2)
        dpe, dgp = _rms_bwd(pe, rp, gp_ref[...], dx3 * gate)
        dpe_ref[...] = dpe.astype(BF16)
        ds = (dx3 * e * gate * (1.0 - gate)).astype(BF16)
        ds_ref[...] = ds
        dhg = _dot_nt(ds, wpg_ref[...])
        down(3)
        dxg, dgg = _rms_bwd(x2, rg, gg_ref[...], dhg)
        dx2 = dx3 + dxg
        dx2_ref[...] = dx2
        dx2b_ref[...] = dx2.astype(BF16)
        x2_cur[...] = x2_next[...]

        stats_ref[0:1, :] += jnp.where(counts, dgf, 0.0)
        stats_ref[1:2, :] += jnp.where(counts, dgp, 0.0)
        stats_ref[2:3, :] += jnp.where(counts, dgg, 0.0)
        stats_ref[3:4, :] += jnp.where(counts, jnp.broadcast_to(loss, (1, D_MODEL)), 0.0)

    ahead = lambda width: pl.BlockSpec((tm, width), lambda i: (jnp.minimum(i, nt - 1), 0))
    behind = lambda width: pl.BlockSpec((tm, width), lambda i: (jnp.maximum(i - 1, 0), 0))
    return pl.pallas_call(
        body,
        grid=(nt + 1,),
        in_specs=[
            ahead(D_MODEL),
            ahead(D_FF),
            _weight_spec((D_FF, D_MODEL)),
            behind(D_PLE),
            behind(D_MODEL),
            _weight_spec((D_MODEL, D_MODEL)),
            _weight_spec((D_MODEL, D_PLE)),
            _const_spec((1, D_MODEL)),
            _const_spec((1, D_MODEL)),
            _const_spec((1, D_MODEL)),
        ],
        out_specs=[
            behind(D_MODEL),
            behind(D_MODEL),
            behind(D_MODEL),
            behind(D_MODEL),
            behind(D_MODEL),
            behind(D_PLE),
            _const_spec((8, D_MODEL)),
        ],
        out_shape=[
            jax.ShapeDtypeStruct((s, D_MODEL), F32),
            jax.ShapeDtypeStruct((s, D_MODEL), BF16),
            jax.ShapeDtypeStruct((s, D_MODEL), BF16),
            jax.ShapeDtypeStruct((s, D_MODEL), BF16),
            jax.ShapeDtypeStruct((s, D_MODEL), BF16),
            jax.ShapeDtypeStruct((s, D_PLE), BF16),
            jax.ShapeDtypeStruct((8, D_MODEL), F32),
        ],
        scratch_shapes=[pltpu.VMEM((tm, D_MODEL), F32), pltpu.VMEM((tm, D_MODEL), F32)],
        compiler_params=_params(("arbitrary",)),
        name="down_ple",
    )(x1, act, w_down, p, tgt, w_pg, w_pu_t, g_gate, g_post, g_final)


def _ffn_bwd(dx2, dx2b, x1, g_sav, u_sav, w_gu_t, w_down, g_ffn, tm, carry=None):
    s = x1.shape[0]

    def body(dx2_ref, dx2b_ref, x1_ref, g_ref, u_ref, w_ref, wd_ref, gffn_ref,
             dg_ref, du_ref, dx1_ref, dx1b_ref, stats_ref):
        @pl.when(pl.program_id(0) == 0)
        def _():
            stats_ref[...] = jnp.zeros_like(stats_ref)

        dx2b = dx2b_ref[...]
        nc = D_FF // FF_CHUNK
        dacts = [_dot_nt(dx2b, wd_ref[c * FF_CHUNK:(c + 1) * FF_CHUNK, :]) for c in range(nc)]
        dh2 = jnp.zeros((tm, D_MODEL), F32)
        for c in range(nc):
            cols = slice(c * FF_CHUNK, (c + 1) * FF_CHUNK)
            g = g_ref[:, cols].astype(F32)
            u = u_ref[:, cols].astype(F32)
            sg = _sigmoid(g)
            dg = (dacts[c] * u * sg * (1.0 + g * (1.0 - sg))).astype(BF16)
            du = (dacts[c] * g * sg).astype(BF16)
            dg_ref[:, cols] = dg
            du_ref[:, cols] = du
            dh2 = dh2 + _dot(dg, w_ref[c * FF_CHUNK:(c + 1) * FF_CHUNK, :])
            dh2 = dh2 + _dot(du, w_ref[D_FF + c * FF_CHUNK:D_FF + (c + 1) * FF_CHUNK, :])

        x1 = x1_ref[...]
        r2 = lax.rsqrt(jnp.mean(x1 * x1, axis=-1, keepdims=True) + EPS)
        dxn, dgf = _rms_bwd(x1, r2, gffn_ref[...], dh2)
        dx1 = dx2_ref[...] + dxn
        dx1_ref[...] = dx1
        dx1b_ref[...] = dx1.astype(BF16)
        stats_ref[0:1, :] += dgf

    return _pcall(
        body,
        grid=(s // tm,),
        in_specs=[
            _row_spec(tm, D_MODEL), _row_spec(tm, D_MODEL), _row_spec(tm, D_MODEL),
            _row_spec(tm, D_FF), _row_spec(tm, D_FF),
            _weight_spec((2 * D_FF, D_MODEL)), _weight_spec((D_FF, D_MODEL)), _const_spec((1, D_MODEL)),
        ],
        out_specs=[_row_spec(tm, D_FF), _row_spec(tm, D_FF), _row_spec(tm, D_MODEL), _row_spec(tm, D_MODEL),
                   _const_spec((8, D_MODEL))],
        out_shape=[
            jax.ShapeDtypeStruct((s, D_FF), BF16),
            jax.ShapeDtypeStruct((s, D_FF), BF16),
            jax.ShapeDtypeStruct((s, D_MODEL), F32),
            jax.ShapeDtypeStruct((s, D_MODEL), BF16),
            jax.ShapeDtypeStruct((8, D_MODEL), F32),
        ],
        scratch_shapes=[],
        name="ffn_bwd",
        args=(dx2, dx2b, x1, g_sav, u_sav, w_gu_t, w_down, g_ffn),
        carry=carry,
    )


def _bwd_mix(dx1, dx1b, x, z, u1, pooled, w_in_t, w_out, g_mix, conv_w, ln_g, ln_b, pool_w, pool_scale, tm,
             carry=None):
    s = x.shape[0]
    nt = s // tm

    def body(dx1_ref, dx1b_ref, x_ref, z_ref, u1_ref, pooled_ref, win_ref, wout_ref, gmix_ref, cw_ref,
             lng_ref, lnb_ref, pw_ref, ps_ref,
             dx_ref, dz_ref, vec_ref, dcw_ref, dpw_ref, dubuf, dvbuf, u0buf, du0buf, dush, pbuf):
        i = pl.program_id(0)
        tile = nt - 1 - i

        @pl.when(i == 0)
        def _():
            vec_ref[...] = jnp.zeros_like(vec_ref)
            dcw_ref[...] = jnp.zeros_like(dcw_ref)
            dpw_ref[...] = jnp.zeros_like(dpw_ref)
            dubuf[tm:tm + CONV_HALO, :] = jnp.zeros((CONV_HALO, C_CONV), F32)
            dvbuf[tm:tm + POOL_HALO, :] = jnp.zeros((POOL_HALO, C_POOL), F32)

        dmix = _dot_nt(dx1b_ref[...], wout_ref[...])
        du = dmix[:, :C_CONV]
        dq = dmix[:, C_CONV:]

        pos1 = (tile * tm + lax.broadcasted_iota(jnp.int32, (tm, 1), 0) + 1).astype(F32)
        dpooled_parts = []
        dps_rows = []
        for g, w in enumerate(POOL_WINDOWS):
            cols = slice(g * POOL_GROUP, (g + 1) * POOL_GROUP)
            pooled_b = pooled_ref[:, cols]
            mixed = _dot(pooled_b, pw_ref[g])
            dqg = dq[:, cols]
            dps_rows.append(jnp.sum(dqg * mixed, axis=0, keepdims=True))
            dmixed = (dqg * ps_ref[:, cols]).astype(BF16)
            dpw_ref[g] += _dot_tn(pooled_b, dmixed)
            dpooled = _dot_nt(dmixed, pw_ref[g])
            dpooled_parts.append(dpooled)
            dvbuf[0:tm, cols] = dpooled / jnp.minimum(pos1, float(w))
        vec_ref[4:5, 0:C_POOL] += jnp.concatenate(dps_rows, axis=-1)
        dv_parts = []
        for g, w in enumerate(POOL_WINDOWS):
            cols = slice(g * POOL_GROUP, (g + 1) * POOL_GROUP)
            tot = _window_sums(dvbuf, cols, pbuf, g, tm, trailing=False)
            dv_parts.append(tot - dpooled_parts[g])

        u1 = u1_ref[...]
        mu = jnp.mean(u1, axis=-1, keepdims=True)
        cen = u1 - mu
        rstd = lax.rsqrt(jnp.mean(cen * cen, axis=-1, keepdims=True) + EPS)
        xhat = cen * rstd
        u2 = xhat * lng_ref[...] + lnb_ref[...]
        sg2 = _sigmoid(u2)
        du2 = du * sg2 * (1.0 + u2 * (1.0 - sg2))
        vec_ref[1:2, 0:C_CONV] += jnp.sum(du2 * xhat, axis=0, keepdims=True)
        vec_ref[2:3, 0:C_CONV] += jnp.sum(du2, axis=0, keepdims=True)
        t1 = du2 * lng_ref[...]
        du1 = rstd * (t1 - jnp.mean(t1, axis=-1, keepdims=True)
                      - xhat * jnp.mean(t1 * xhat, axis=-1, keepdims=True))
        vec_ref[3:4, 0:C_CONV] += jnp.sum(du1, axis=0, keepdims=True)
        dubuf[0:tm, :] = du1

        zt = z_ref[...]
        a = zt[:, :C_CONV]
        sgb = _sigmoid(zt[:, C_CONV:2 * C_CONV])
        u0buf[...] = a * sgb

        _shifted_copies(dubuf, dush, tm)
        for rc in range(tm // ROW_CHUNK):
            r0 = rc * ROW_CHUNK
            acc = jnp.zeros((ROW_CHUNK, C_CONV), F32)
            for k in range(CONV_K):
                acc = acc + cw_ref[k:k + 1, :] * _rows_at(dubuf, dush, r0 + (CONV_K - 1) - k)
            du0buf[r0:r0 + ROW_CHUNK, :] = acc
        for k in range(CONV_K):
            acc = jnp.zeros((ROW_CHUNK, C_CONV), F32)
            for rc in range(tm // ROW_CHUNK):
                r0 = rc * ROW_CHUNK
                acc = acc + u0buf[r0:r0 + ROW_CHUNK, :] * _rows_at(dubuf, dush, r0 + (CONV_K - 1) - k)
            dcw_ref[k:k + 1, :] += jnp.sum(acc, axis=0, keepdims=True)
        du0 = du0buf[...]

        da = du0 * sgb
        db = du0 * a * sgb * (1.0 - sgb)
        dz = jnp.concatenate([da, db] + dv_parts, axis=-1).astype(BF16)
        dz_ref[...] = dz

        xt = x_ref[...]
        r1 = lax.rsqrt(jnp.mean(xt * xt, axis=-1, keepdims=True) + EPS)
        dh1 = _dot(dz, win_ref[...])
        dxn, dgm = _rms_bwd(xt, r1, gmix_ref[...], dh1)
        dx_ref[...] = dx1_ref[...] + dxn
        vec_ref[0:1, :] += dgm

        dubuf[tm:tm + CONV_HALO, :] = dubuf[0:CONV_HALO, :]
        dvbuf[tm:tm + POOL_HALO, :] = dvbuf[0:POOL_HALO, :]

    rev = lambda width: pl.BlockSpec((tm, width), lambda i: (nt - 1 - i, 0))
    return _pcall(
        body,
        grid=(nt,),
        in_specs=[
            rev(D_MODEL), rev(D_MODEL), rev(D_MODEL), rev(Z_WIDTH), rev(C_CONV), rev(C_POOL),
            _const_spec((Z_WIDTH, D_MODEL)),
            _const_spec((D_MODEL, D_MODEL)),
            _const_spec((1, D_MODEL)),
            _const_spec((CONV_HALO, C_CONV)),
            _const_spec((1, C_CONV)),
            _const_spec((1, C_CONV)),
            _const_spec((len(POOL_WINDOWS), POOL_GROUP, POOL_GROUP)),
            _const_spec((1, C_POOL)),
        ],
        out_specs=[
            rev(D_MODEL), rev(Z_WIDTH),
            _const_spec((8, D_MODEL)),
            _const_spec((CONV_HALO, C_CONV)),
            _const_spec((len(POOL_WINDOWS), POOL_GROUP, POOL_GROUP)),
        ],
        out_shape=[
            jax.ShapeDtypeStruct((s, D_MODEL), F32),
            jax.ShapeDtypeStruct((s, Z_WIDTH), BF16),
            jax.ShapeDtypeStruct((8, D_MODEL), F32),
            jax.ShapeDtypeStruct((CONV_HALO, C_CONV), F32),
            jax.ShapeDtypeStruct((len(POOL_WINDOWS), POOL_GROUP, POOL_GROUP), F32),
        ],
        scratch_shapes=[
            pltpu.VMEM((tm + CONV_HALO, C_CONV), F32),
            pltpu.VMEM((tm + POOL_HALO, C_POOL), F32),
            pltpu.VMEM((tm, C_CONV), F32),
            pltpu.VMEM((tm, C_CONV), F32),
            pltpu.VMEM((7, tm + CONV_HALO, C_CONV), F32),
            pltpu.VMEM((2, tm + POOL_HALO, POOL_GROUP), F32),
        ],
        name="bwd_mix",
        args=(dx1, dx1b, x, z, u1, pooled, w_in_t, w_out, g_mix, conv_w, ln_g, ln_b, pool_w, pool_scale),
        carry=carry,
    )


def _grad_matmul(a, b, bm, name, a2=None, carry=None):
    s, ma = a.shape
    nb = b.shape[1]
    na = ma // bm
    if a2 is None:
        def body(a_ref, b_ref, o_ref):
            o_ref[...] = _dot_tn(a_ref[...], b_ref[...]).astype(BF16)

        lhs_specs = [pl.BlockSpec((s, bm), lambda i: (0, i))]
        lhs = (a,)
        steps = na
    else:
        def body(a_ref, a2_ref, b_ref, o_ref):
            i = pl.program_id(0)

            @pl.when(i < na)
            def _():
                o_ref[...] = _dot_tn(a_ref[...], b_ref[...]).astype(BF16)

            @pl.when(i >= na)
            def _():
                o_ref[...] = _dot_tn(a2_ref[...], b_ref[...]).astype(BF16)

        lhs_specs = [pl.BlockSpec((s, bm), lambda i: (0, jnp.minimum(i, na - 1))),
                     pl.BlockSpec((s, bm), lambda i: (0, jnp.maximum(i - na, 0)))]
        lhs = (a, a2)
        steps = 2 * na

    outs, carried = _pcall(
        body,
        grid=(steps,),
        in_specs=lhs_specs + [pl.BlockSpec((s, nb), lambda i: (0, 0))],
        out_specs=[pl.BlockSpec((bm, nb), lambda i: (i, 0))],
        out_shape=[jax.ShapeDtypeStruct((steps * bm, nb), BF16)],
        scratch_shapes=[],
        name=name,
        args=lhs + (b,),
        carry=carry,
    )
    return outs[0], carried


def _reduce_scatter(grads, small, name):
    n, ns = len(grads), len(small)
    shapes = [g.shape[1:] for g in grads]
    gather = _Carry()
    _carry_gather(gather, small, 1, 2)

    def body(*refs):
        g = refs[:n]
        out = refs[n + ns:2 * n + ns]
        scr = refs[2 * (n + ns):]
        own, loc, r1, r2 = scr[:n], scr[n:2 * n], scr[2 * n:3 * n], scr[3 * n:4 * n]
        load_sems, s1, q1, s2, q2 = scr[4 * n:4 * n + 5]
        gather_refs = (refs[n:n + ns], refs[2 * n + ns:2 * (n + ns)], scr[4 * n + 5:])
        me, sib, chips = _place()
        c = me[2]

        _handshake(EVERY_PEER)
        gather.stage(0, *gather_refs, handshake=False)
        loads = []
        sends = []
        for a in range(n):
            ld = [pltpu.make_async_copy(g[a].at[_block(*chip, c)], loc[a].at[j], load_sems.at[a, j])
                  for j, chip in enumerate(chips)]
            ld.append(pltpu.make_async_copy(g[a].at[_block(*me)], own[a], load_sems.at[a, 3]))
            for cp in ld:
                cp.start()
            loads.append(ld)
            blocks = [(*chip, 1 - c) for chip in chips] + [sib]
            for j, blk in enumerate(blocks):
                cp = pltpu.make_async_remote_copy(
                    src_ref=g[a].at[_block(*blk)], dst_ref=r1[a].at[j],
                    send_sem=s1.at[a, j], recv_sem=q1.at[a, j], device_id=sib, device_id_type=MESH)
                cp.start()
                sends.append(cp)

        def from_sibling(a, j):
            return pltpu.make_async_remote_copy(
                src_ref=r1[a].at[j], dst_ref=r1[a].at[j], send_sem=s1.at[a, j], recv_sem=q1.at[a, j],
                device_id=sib, device_id_type=MESH)

        def partial(a, j, chip):
            return pltpu.make_async_remote_copy(
                src_ref=loc[a].at[j], dst_ref=r2[a].at[j], send_sem=s2.at[a, j], recv_sem=q2.at[a, j],
                device_id=(*chip, c), device_id_type=MESH)

        gather.stage(1, *gather_refs)
        for a in range(n):
            for j, chip in enumerate(chips):
                loads[a][j].wait()
                from_sibling(a, j).wait_recv()
                loc[a][j] = (loc[a][j].astype(F32) + r1[a][j].astype(F32)).astype(BF16)
                cp = partial(a, j, chip)
                cp.start()
                sends.append(cp)
        gather.stage(2, *gather_refs)
        for a in range(n):
            loads[a][3].wait()
            from_sibling(a, 3).wait_recv()
            acc = own[a][...].astype(F32) + r1[a][3].astype(F32)
            for j, chip in enumerate(chips):
                partial(a, j, chip).wait_recv()
                acc = acc + r2[a][j].astype(F32)
            out[a][...] = acc
        for cp in sends:
            cp.wait_send()
        gather.drain(*gather_refs)

    any_spec = pl.BlockSpec(memory_space=pl.ANY)
    vmem_spec = pl.BlockSpec(memory_space=pltpu.VMEM)
    res = pl.pallas_call(
        body,
        in_specs=[any_spec] * (n + ns),
        out_specs=[vmem_spec] * n + [any_spec] * ns,
        out_shape=[jax.ShapeDtypeStruct(sh, F32) for sh in shapes] + gather.out_shapes,
        scratch_shapes=(
            [pltpu.VMEM(sh, BF16) for sh in shapes]
            + [pltpu.VMEM((3,) + sh, BF16) for sh in shapes]
            + [pltpu.VMEM((4,) + sh, BF16) for sh in shapes]
            + [pltpu.VMEM((3,) + sh, BF16) for sh in shapes]
            + [pltpu.SemaphoreType.DMA((n, 4)),
               pltpu.SemaphoreType.DMA((n, 4)), pltpu.SemaphoreType.DMA((n, 4)),
               pltpu.SemaphoreType.DMA((n, 3)), pltpu.SemaphoreType.DMA((n, 3))]
            + gather.sem_shapes()
        ),
        compiler_params=_params(collective_id=_collective_id(EVERY_PEER)),
        name=name,
    )(*grads, *small)
    return res[:n], res[n:]


def _pair_add(grads, from_sib, blks, name):
    n = len(grads)

    def body(blk_ref, *refs):
        for a in range(n):
            refs[2 * n + a][...] = (refs[a][...].astype(F32) + refs[n + a][...].astype(F32)).astype(BF16)

    mine = [pl.BlockSpec((None,) + g.shape[1:], lambda j, b: (b[j], 0, 0)) for g in grads]
    same = [pl.BlockSpec((None,) + g.shape[1:], lambda j, b: (j, 0, 0)) for g in grads]
    return pl.pallas_call(
        body,
        grid_spec=pltpu.PrefetchScalarGridSpec(
            num_scalar_prefetch=1, grid=(4,), in_specs=mine + same, out_specs=same),
        out_shape=[jax.ShapeDtypeStruct((4,) + g.shape[1:], BF16) for g in grads],
        compiler_params=_params(("arbitrary",)),
        name=name,
    )(blks, *grads, *from_sib)


def _chip_sum(parts, from_chips, name):
    n = len(parts)

    def body(*refs):
        for a in range(n):
            acc = refs[a][...].astype(F32)
            for j in range(3):
                acc = acc + refs[n + a][j].astype(F32)
            refs[2 * n + a][...] = acc

    half = [p.shape[1] // 2 for p in parts]
    return pl.pallas_call(
        body,
        grid=(2,),
        in_specs=([pl.BlockSpec((None, h, p.shape[2]), lambda i: (3, i, 0)) for p, h in zip(parts, half)]
                  + [pl.BlockSpec((3, h, p.shape[2]), lambda i: (0, i, 0)) for p, h in zip(parts, half)]),
        out_specs=[pl.BlockSpec((h, p.shape[2]), lambda i: (i, 0)) for p, h in zip(parts, half)],
        out_shape=[jax.ShapeDtypeStruct(p.shape[1:], F32) for p in parts],
        compiler_params=_params(("arbitrary",)),
        name=name,
    )(*parts, *from_chips)


def _adam_math(w, g, m, v):
    nm = ADAM_B1 * m + (1.0 - ADAM_B1) * g
    nv = ADAM_B2 * v + (1.0 - ADAM_B2) * (g * g)
    m_hat = nm / (1.0 - ADAM_B1 ** ADAM_STEP)
    v_hat = nv / (1.0 - ADAM_B2 ** ADAM_STEP)
    return -ADAM_LR * (m_hat / (jnp.sqrt(v_hat) + ADAM_EPS) + ADAM_WD * w), nm, nv


def _sum_adamw(items, name, carry=None):
    n = len(items)

    def body(*refs):
        for a in range(n):
            p_ref, f_ref, w_ref, m_ref, v_ref = refs[5 * a:5 * a + 5]
            g_ref, d_ref, nm_ref, nv_ref = refs[5 * n + 4 * a:5 * n + 4 * a + 4]
            g = p_ref[...].astype(F32)
            for j in range(3):
                g = g + f_ref[j].astype(F32)
            g_ref[...] = g
            d_ref[...], nm_ref[...], nv_ref[...] = _adam_math(w_ref[...], g, m_ref[...], v_ref[...])

    in_specs, out_specs, out_shape, args = [], [], [], []
    for part, from_chips, w, m, v in items:
        r, c = w.shape
        spec = pl.BlockSpec((r // 2, c), lambda i: (i, 0))
        in_specs += [pl.BlockSpec((None, r // 2, c), lambda i: (3, i, 0)),
                     pl.BlockSpec((3, r // 2, c), lambda i: (0, i, 0)), spec, spec, spec]
        out_specs += [spec] * 4
        out_shape += [jax.ShapeDtypeStruct((r, c), F32)] * 4
        args += [part, from_chips, w, m, v]
    outs, carried = _pcall(body, grid=(2,), in_specs=in_specs, out_specs=out_specs, out_shape=out_shape,
                           scratch_shapes=[], name=name, args=tuple(args), carry=carry)
    return [tuple(outs[4 * a:4 * a + 4]) for a in range(n)], carried


def _small_update(gathered, layout, params, name):
    ng, npar = len(gathered), len(params)

    def body(*refs):
        parts = refs[:ng]
        prm = refs[ng:ng + 3 * npar]
        tot_refs = refs[ng + 3 * npar:2 * ng + 3 * npar]
        out = refs[2 * ng + 3 * npar:]
        tots = []
        for a in range(ng):
            acc = parts[a][0].astype(F32)
            for d in range(1, N_DEV):
                acc = acc + parts[a][d].astype(F32)
            tot_refs[a][...] = acc
            tots.append(acc)
        for i, (a, row, width) in enumerate(layout):
            g = tots[a] if row is None else tots[a][row:row + 1, :width]
            delta, nm, nv = _adam_math(prm[3 * i][...], g, prm[3 * i + 1][...], prm[3 * i + 2][...])
            out[4 * i][...] = g
            out[4 * i + 1][...] = delta
            out[4 * i + 2][...] = nm
            out[4 * i + 3][...] = nv

    flat = [t for prm in params for t in prm]
    res = pl.pallas_call(
        body,
        out_shape=([jax.ShapeDtypeStruct(g.shape[1:], F32) for g in gathered]
                   + [jax.ShapeDtypeStruct(prm[0].shape, F32) for prm in params for _ in range(4)]),
        compiler_params=pltpu.CompilerParams(vmem_limit_bytes=V7X_VMEM_LIMIT),
        name=name,
    )(*gathered, *flat)
    return res[:ng], [tuple(res[ng + 4 * i:ng + 4 * i + 4]) for i in range(npar)]


def _adamw(w, g, m, v, name):
    rows, cols = w.shape
    br = rows
    for cand in (512, 256, 128):
        if rows % cand == 0 and rows > cand:
            br = cand
            break

    def body(w_ref, g_ref, m_ref, v_ref, d_ref, nm_ref, nv_ref):
        d_ref[...], nm_ref[...], nv_ref[...] = _adam_math(w_ref[...], g_ref[...], m_ref[...], v_ref[...])

    spec = pl.BlockSpec((br, cols), lambda i: (i, 0))
    shape = jax.ShapeDtypeStruct((rows, cols), F32)
    return pl.pallas_call(
        body,
        grid=(rows // br,),
        in_specs=[spec] * 4,
        out_specs=[spec] * 3,
        out_shape=[shape] * 3,
        compiler_params=_params(("arbitrary",)),
        name=name,
    )(w, g, m, v)


def _adamw_whole(w, g, m, v, name):
    def body(w_ref, g_ref, m_ref, v_ref, d_ref, nm_ref, nv_ref):
        d_ref[...], nm_ref[...], nv_ref[...] = _adam_math(w_ref[...], g_ref[...], m_ref[...], v_ref[...])

    return pl.pallas_call(
        body,
        out_shape=[jax.ShapeDtypeStruct(w.shape, F32)] * 3,
        name=name,
    )(w, g, m, v)


def _by_device(full):
    return full.reshape(N_DEV, full.shape[0] // N_DEV, full.shape[1])


def kernel(x, p, g_mix, w_in, conv_w, conv_b, ln_g, ln_b, pool_w, pool_scale, w_out, g_ffn, w_gate_up, w_down, g_ple_gate, w_ple_gate, w_ple_up, g_ple_post, g_final, loss_target, m_g_mix, m_w_in, m_conv_w, m_conv_b, m_ln_g, m_ln_b, m_pool_w, m_pool_scale, m_w_out, m_g_ffn, m_w_gate_up, m_w_down, m_g_ple_gate, m_w_ple_gate, m_w_ple_up, m_g_ple_post, m_g_final, v_g_mix, v_w_in, v_conv_w, v_conv_b, v_ln_g, v_ln_b, v_pool_w, v_pool_scale, v_w_out, v_g_ffn, v_w_gate_up, v_w_down, v_g_ple_gate, v_w_ple_gate, v_w_ple_up, v_g_ple_post, v_g_final):
    seq = x.shape[1]
    tile = min(TOKEN_TILE, seq)
    xs = x[0]
    ps = p[0, 0]
    tgt = loss_target[0]
    ax, ay, ac = lax.axis_index("x"), lax.axis_index("y"), lax.axis_index("c")
    me = _block(ax, ay, ac)
    blks = jnp.stack([_block(1 - ax, ay, ac), _block(ax, 1 - ay, ac), _block(1 - ax, 1 - ay, ac), me]).astype(jnp.int32)
    rows = lambda gth: gth.reshape((-1,) + gth.shape[2:])

    sh_in, sh_out, sh_gu, sh_down, sh_pg, sh_pu, pool_w_b = _to_bf16([
        w_in[0].T,
        w_out[0],
        w_gate_up[0].T,
        w_down[0],
        w_ple_gate[0],
        w_ple_up[0].T,
        pool_w[0]])
    carry = _Carry()
    _carry_gather(carry, [sh_in, sh_out,
                          jnp.pad(conv_w[0].T, ((0, 0), (0, CONV_HALO - CONV_K)))], 8, 15)
    (h1,), first = _norm_in(xs, g_mix, tile, carry)
    w_in_t, w_out_f, conv_w_t = [rows(gth) for gth in first]
    conv_w_f = conv_w_t.T

    carry = _Carry()
    _carry_gather(carry, [sh_gu], 8, 12)
    (z, u1, pooled, x1, mix), (w_gu_all,) = _fwd_mix(
        xs, h1, w_in_t, w_out_f, conv_w_f, conv_b, ln_g, ln_b, pool_w_b, pool_scale, tile, carry)
    w_gu_t = rows(w_gu_all)

    carry = _Carry()
    _carry_gather(carry, [sh_down, sh_pg, sh_pu], 6, 10)
    (h2, g_sav, u_sav, act), late = _ffn_up(x1, w_gu_t, g_ffn, tile, carry)
    w_down_f, w_pg_f, w_pu_t = [rows(gth) for gth in late]
    dx2, dx2b, hg, ds, dpe, pb, stats_ple = _down_ple(
        x1, act, w_down_f, ps, tgt, w_pg_f, w_pu_t, g_ple_gate, g_ple_post, g_final.reshape(1, D_MODEL),
        tile)

    d_w_down, _ = _grad_matmul(act, dx2b, GRAD_ROWS, "grad_w_down")
    d_w_pg, _ = _grad_matmul(hg, ds, GRAD_ROWS, "grad_w_ple_gate")
    d_w_pu_t, _ = _grad_matmul(dpe, pb, GRAD_ROWS, "grad_w_ple_up")
    early = [_by_device(d_w_pg), _by_device(d_w_pu_t), _by_device(d_w_down)]
    carry = _Carry()
    _carry_pair(carry, early)
    (dg, du, dx1, dx1b, stats_ffn), from_sib = _ffn_bwd(dx2, dx2b, x1, g_sav, u_sav, w_gu_t, w_down_f, g_ffn,
                                                        tile, carry)
    early_parts = _pair_add(early, from_sib, blks, "pair_add_early")

    carry = _Carry()
    _carry_chip(carry, early_parts)
    d_w_gu_t, early_chips = _grad_matmul(dg, h2, GRAD_ROWS, "grad_w_gate_up", a2=du, carry=carry)
    carry = _Carry()
    _carry_pair(carry, [_by_device(d_w_gu_t)])
    d_w_out, gu_sib = _grad_matmul(mix, dx1b, GRAD_ROWS, "grad_w_out", carry=carry)
    gu_parts = _pair_add([_by_device(d_w_gu_t)], gu_sib, blks, "pair_add_gate_up")

    carry = _Carry()
    _carry_chip(carry, gu_parts)
    _carry_pair(carry, [_by_device(d_w_out)])
    (dx, dz, vec_mix, dconv_w_part, dpool_w_part), carried = _bwd_mix(
        dx1, dx1b, xs, z, u1, pooled, w_in_t, w_out_f, g_mix, conv_w_f, ln_g, ln_b, pool_w_b, pool_scale,
        tile, carry)
    gu_chips, out_sib = carried[:1], carried[1:]
    out_parts = _pair_add([_by_device(d_w_out)], out_sib, blks, "pair_add_out")

    carry = _Carry()
    _carry_chip(carry, out_parts)
    d_w_in_t, out_chips = _grad_matmul(dz, h1, GRAD_ROWS, "grad_w_in", carry=carry)
    (gr_w_in_t,), small = _reduce_scatter(
        [_by_device(d_w_in_t)],
        [vec_mix, stats_ple, stats_ffn, dconv_w_part,
         dpool_w_part.astype(BF16).reshape(-1, POOL_GROUP)],
        "scatter_last")
    small = list(small[:4]) + [small[4].reshape((N_DEV,) + dpool_w_part.shape)]

    vec_names = ["g_mix", "ln_g", "ln_b", "conv_b", "pool_scale", "g_final", "g_ple_post", "g_ple_gate", "g_ffn",
                 "pool_w"]
    layout = [(0, 0, D_MODEL), (0, 1, C_CONV), (0, 2, C_CONV), (0, 3, C_CONV), (0, 4, C_POOL),
              (1, 0, D_MODEL), (1, 1, D_MODEL), (1, 2, D_MODEL), (2, 0, D_MODEL), (4, None, None)]
    as_row = lambda t: t.reshape(1, D_MODEL)
    params = [(g_mix, m_g_mix, v_g_mix), (ln_g, m_ln_g, v_ln_g), (ln_b, m_ln_b, v_ln_b),
              (conv_b, m_conv_b, v_conv_b), (pool_scale, m_pool_scale, v_pool_scale),
              (as_row(g_final), as_row(m_g_final), as_row(v_g_final)),
              (g_ple_post, m_g_ple_post, v_g_ple_post), (g_ple_gate, m_g_ple_gate, v_g_ple_gate),
              (g_ffn, m_g_ffn, v_g_ffn), (pool_w[0], m_pool_w[0], v_pool_w[0])]
    tots, small_upd = _small_update(small, layout, params, "small_update")
    loss = tots[1][3, 0]
    upd = {}
    for nm, res, prm in zip(vec_names, small_upd, [g_mix, ln_g, ln_b, conv_b, pool_scale, g_final, g_ple_post,
                                                    g_ple_gate, g_ffn, pool_w]):
        upd[nm] = tuple(t.reshape(prm.shape) for t in res)
    gr_conv_w = lax.dynamic_slice_in_dim(tots[3][:CONV_K], me * (C_CONV // N_DEV), C_CONV // N_DEV, axis=1)

    natural = lambda t: t[None]
    turned = lambda t: t.T[None]
    (res_pg, res_gu, res_out, res_down), _ = _sum_adamw([
        (early_parts[0], early_chips[0], w_ple_gate[0], m_w_ple_gate[0], v_w_ple_gate[0]),
        (gu_parts[0], gu_chips[0], w_gate_up[0].T, m_w_gate_up[0].T, v_w_gate_up[0].T),
        (out_parts[0], out_chips[0], w_out[0], m_w_out[0], v_w_out[0]),
        (early_parts[2], early_chips[2], w_down[0], m_w_down[0], v_w_down[0])], "adamw_big")
    upd["w_ple_gate"] = tuple(natural(t) for t in res_pg)
    upd["w_gate_up"] = tuple(turned(t) for t in res_gu)
    upd["w_out"] = tuple(natural(t) for t in res_out)
    upd["w_down"] = tuple(natural(t) for t in res_down)
    (gr_w_pu_t,) = _chip_sum(early_parts[1:2], early_chips[1:2], "chip_sum")
    plain = [
        ("w_in", w_in[0].T, gr_w_in_t, m_w_in[0].T, v_w_in[0].T, True),
        ("w_ple_up", w_ple_up[0], gr_w_pu_t.T, m_w_ple_up[0], v_w_ple_up[0], False),
    ]
    for nm, w_, g_, m_, v_, transposed in plain:
        res = (g_,) + tuple(_adamw(w_, g_, m_, v_, "adamw_" + nm))
        upd[nm] = tuple((t.T if transposed else t)[None] for t in res)
    taps = lambda t: t.transpose(1, 0, 2)
    g_taps = gr_conv_w[:, None, :]
    res = (g_taps,) + tuple(_adamw_whole(taps(conv_w), g_taps, taps(m_conv_w), taps(v_conv_w), "adamw_conv_w"))
    upd["conv_w"] = tuple(taps(t) for t in res)

    order = ["g_mix", "w_in", "conv_w", "conv_b", "ln_g", "ln_b", "pool_w", "pool_scale", "w_out", "g_ffn",
             "w_gate_up", "w_down", "g_ple_gate", "w_ple_gate", "w_ple_up", "g_ple_post", "g_final"]
    outs = [loss, dx[None]]
    for k in range(4):
        outs += [upd[nm][k] for nm in order]
    return tuple(outs)
```

```python
import functools

import jax
import jax.numpy as jnp
from jax import lax
from jax.experimental import pallas as pl
from jax.experimental.pallas import tpu as pltpu

D_MODEL = 1024
C_CONV = 512
C_POOL = 512
Z_WIDTH = 2 * C_CONV + C_POOL
POOL_WINDOWS = (2, 4, 8, 16)
POOL_GROUP = 128
CONV_K = 31
D_FF = 2816
D_PLE = 256
EPS = 1e-6
N_DEV = 8

ADAM_LR = 0.001
ADAM_B1 = 0.9
ADAM_B2 = 0.999
ADAM_EPS = 1e-08
ADAM_WD = 0.01
ADAM_STEP = 10

CONV_HALO = 32
POOL_HALO = 32
ROW_CHUNK = 32
TOKEN_TILE = 256
GRAD_ROWS = 256
V7X_VMEM_LIMIT = 56 * 1024 * 1024
FF_CHUNK = D_FF // 2

BF16 = jnp.bfloat16
F32 = jnp.float32
MESH = pl.DeviceIdType.MESH


def _dot(a, b):
    return lax.dot_general(a, b, (((1,), (0,)), ((), ())), preferred_element_type=F32)


def _dot_nt(a, b):
    return lax.dot_general(a, b, (((1,), (1,)), ((), ())), preferred_element_type=F32)


def _dot_tn(a, b):
    return lax.dot_general(a, b, (((0,), (0,)), ((), ())), preferred_element_type=F32)


def _rms_fwd(x, g):
    r = lax.rsqrt(jnp.mean(x * x, axis=-1, keepdims=True) + EPS)
    return x * r * g, r


def _rms_bwd(x, r, g, dy):
    xr = x * r
    dg = jnp.sum(dy * xr, axis=0, keepdims=True)
    dyg = dy * g
    dx = r * (dyg - xr * jnp.mean(dyg * xr, axis=-1, keepdims=True))
    return dx, dg


def _sigmoid(x):
    return jax.nn.sigmoid(x)


def _params(sem=None, collective_id=None):
    return pltpu.CompilerParams(dimension_semantics=sem, vmem_limit_bytes=V7X_VMEM_LIMIT,
                                collective_id=collective_id)


def _place():
    x, y, c = lax.axis_index("x"), lax.axis_index("y"), lax.axis_index("c")
    chips = [(1 - x, y), (x, 1 - y), (1 - x, 1 - y)]
    return (x, y, c), (x, y, 1 - c), chips


def _block(px, py, pc):
    return 4 * px + 2 * py + pc


EVERY_PEER = ("sib", 0, 1, 2)
PEER_SET_IDS = {("sib",): 0, ("0", "1", "sib"): 1, ("0", "1", "2"): 2, ("0", "1", "2", "sib"): 3}


def _collective_id(peers):
    return PEER_SET_IDS[tuple(sorted(str(p) for p in peers))]


def _handshake(peers):
    me, sib, chips = _place()
    barrier = pltpu.get_barrier_semaphore()
    for p in peers:
        to = sib if p == "sib" else (*chips[p], me[2])
        pl.semaphore_signal(barrier, inc=1, device_id=to, device_id_type=MESH)
    pl.semaphore_wait(barrier, len(peers))


class _Carry:
    def __init__(self):
        self.inputs = []
        self.out_shapes = []
        self.copies = []
        self.locals = []

    def add_input(self, arr):
        self.inputs.append(arr)
        return len(self.inputs) - 1

    def add_output(self, shape, dtype):
        self.out_shapes.append(jax.ShapeDtypeStruct(shape, dtype))
        return len(self.out_shapes) - 1

    def local(self, src_idx, dst_idx, dst_blk):
        self.locals.append((src_idx, dst_idx, dst_blk))

    def copy(self, src, dst_idx, dst_blk, got_blk, peer, step=0, after=()):
        self.copies.append(dict(src=src, dst_idx=dst_idx, dst_blk=dst_blk, got_blk=got_blk, peer=peer, step=step,
                                after=tuple(after)))
        return len(self.copies) - 1

    def sem_shapes(self):
        return [pltpu.SemaphoreType.DMA((max(1, len(self.copies)),)),
                pltpu.SemaphoreType.DMA((max(1, len(self.copies)),)),
                pltpu.SemaphoreType.DMA((max(1, len(self.locals)),))]

    @staticmethod
    def _view(ref, where):
        if isinstance(where, tuple):
            blk, row0, nrows = where
            return ref.at[blk, pl.ds(row0, nrows)]
        return ref.at[where]

    def _desc(self, k, ins, outs, sems, place):
        cp = self.copies[k]
        me, sib, chips = place
        kind, idx, blk = cp["src"]
        src = (ins if kind == "in" else outs)[idx]
        if blk is not None:
            src = self._view(src, blk(*place))
        to = sib if cp["peer"] == "sib" else (*chips[cp["peer"]], me[2])
        return pltpu.make_async_remote_copy(
            src_ref=src, dst_ref=self._view(outs[cp["dst_idx"]], cp["dst_blk"](*place)),
            send_sem=sems[0].at[k], recv_sem=sems[1].at[k], device_id=to, device_id_type=MESH)

    def _arrival(self, k, outs, sems, place):
        cp = self.copies[k]
        got = self._view(outs[cp["dst_idx"]], cp["got_blk"](*place))
        return pltpu.make_async_remote_copy(
            src_ref=got, dst_ref=got, send_sem=sems[0].at[k], recv_sem=sems[1].at[k],
            device_id=place[0], device_id_type=MESH)

    def _local(self, n, ins, outs, sems, place):
        src_idx, dst_idx, blk = self.locals[n]
        return pltpu.make_async_copy(ins[src_idx], outs[dst_idx].at[blk(*place)], sems[2].at[n])

    def stages(self):
        return sorted({0} | {cp["step"] for cp in self.copies})

    def peers(self):
        return sorted({cp["peer"] for cp in self.copies}, key=str)

    def stage(self, s, ins, outs, sems, handshake=True):
        place = _place()
        if s == 0:
            if handshake:
                _handshake(self.peers())
            self._waited = set()
            for n in range(len(self.locals)):
                self._local(n, ins, outs, sems, place).start()
        for k, cp in enumerate(self.copies):
            if cp["step"] != s:
                continue
            for a in cp["after"]:
                if a not in self._waited:
                    self._arrival(a, outs, sems, place).wait_recv()
                    self._waited.add(a)
            self._desc(k, ins, outs, sems, place).start()

    def drain(self, ins, outs, sems):
        place = _place()
        for k in range(len(self.copies)):
            if k not in self._waited:
                self._arrival(k, outs, sems, place).wait_recv()
        for k in range(len(self.copies)):
            self._desc(k, ins, outs, sems, place).wait_send()
        for n in range(len(self.locals)):
            self._local(n, ins, outs, sems, place).wait()

    def starts(self, step, nsteps, ins, outs, sems):
        for s in self.stages():
            pl.when(step == min(s, nsteps - 1))(functools.partial(self.stage, s, ins, outs, sems))

    def finish(self, step, nsteps, ins, outs, sems):
        pl.when(step == nsteps - 1)(functools.partial(self.drain, ins, outs, sems))


def _const_blk(j):
    return lambda me, sib, chips: j


def _carry_gather(carry, shards, relay_step, last_step):
    outs = []
    for sh in shards:
        i = carry.add_input(sh)
        o = carry.add_output((N_DEV,) + sh.shape, sh.dtype)
        half = sh.shape[0] // 2
        tile = 16 if sh.dtype == BF16 else 8
        split = half % tile == 0
        rows = [(0, half), (half, sh.shape[0] - half)] if split else [(0, sh.shape[0]), None]

        def whole(j, core):
            return lambda me, sib, chips, j=j, core=core: _block(*chips[j], me[2] if core == 0 else 1 - me[2])

        def part(j, core, h, rows=rows):
            return lambda me, sib, chips: (_block(*chips[j], me[2] if core == 0 else 1 - me[2]),) + rows[h]

        mine = lambda me, sib, chips: _block(*me)
        carry.local(i, o, mine)
        carry.copy(("in", i, None), o, mine, lambda me, sib, chips: _block(*sib), "sib")
        near = [carry.copy(("in", i, None), o, mine, whole(j, 0), j) for j in range(2)]
        for j in range(2):
            carry.copy(("out", o, whole(j, 0)), o, whole(j, 0), whole(j, 1), "sib", step=relay_step, after=(near[j],))
        for j in range(2):
            if rows[j] is None:
                continue
            far = carry.copy(("out", o, part(j, 0, j)), o, part(j, 0, j), part(2, 0, j), 1 - j,
                             step=relay_step, after=(near[j],))
            carry.copy(("out", o, part(2, 0, j)), o, part(2, 0, j), part(2, 1, j), "sib", step=last_step, after=(far,))
        outs.append(o)
    return outs


def _carry_pair(carry, grads):
    outs = []
    for g in grads:
        i = carry.add_input(g)
        o = carry.add_output((4,) + g.shape[1:], g.dtype)
        for j in range(4):
            if j < 3:
                blk = lambda me, sib, chips, j=j: _block(*chips[j], 1 - me[2])
            else:
                blk = lambda me, sib, chips: _block(*sib)
            carry.copy(("in", i, blk), o, _const_blk(j), _const_blk(j), "sib")
        outs.append(o)
    return outs


def _carry_chip(carry, parts):
    outs = []
    for p in parts:
        i = carry.add_input(p)
        o = carry.add_output((3,) + p.shape[1:], p.dtype)
        for j in range(3):
            carry.copy(("in", i, _const_blk(j)), o, _const_blk(j), _const_blk(j), j)
        outs.append(o)
    return outs


def _pcall(body, *, grid, in_specs, out_specs, out_shape, scratch_shapes, name, args, carry=None):
    sem = ("arbitrary",) * len(grid)
    if carry is None:
        res = pl.pallas_call(body, grid=grid, in_specs=in_specs, out_specs=out_specs, out_shape=out_shape,
                             scratch_shapes=scratch_shapes, compiler_params=_params(sem), name=name)(*args)
        return list(res), []
    n_in, n_out, n_scr = len(in_specs), len(out_specs), len(scratch_shapes)
    c_in, c_out = len(carry.inputs), len(carry.out_shapes)
    nsteps = 1
    for extent in grid:
        nsteps *= extent

    def wrapped(*refs):
        ins = refs[:n_in]
        cins = refs[n_in:n_in + c_in]
        o0 = n_in + c_in
        outs = refs[o0:o0 + n_out]
        couts = refs[o0 + n_out:o0 + n_out + c_out]
        s0 = o0 + n_out + c_out
        scr = refs[s0:s0 + n_scr]
        sems = refs[s0 + n_scr:]
        step = pl.program_id(0)
        for d in range(1, len(grid)):
            step = step * grid[d] + pl.program_id(d)
        carry.starts(step, nsteps, cins, couts, sems)
        body(*ins, *outs, *scr)
        carry.finish(step, nsteps, cins, couts, sems)

    any_spec = pl.BlockSpec(memory_space=pl.ANY)
    res = pl.pallas_call(
        wrapped, grid=grid,
        in_specs=list(in_specs) + [any_spec] * c_in,
        out_specs=list(out_specs) + [any_spec] * c_out,
        out_shape=list(out_shape) + carry.out_shapes,
        scratch_shapes=list(scratch_shapes) + carry.sem_shapes(),
        compiler_params=_params(sem, _collective_id(carry.peers())), name=name)(*args, *carry.inputs)
    return list(res[:n_out]), list(res[n_out:])


def _shifted_copies(buf, shifted, tm):
    span = tm + CONV_HALO - 8
    for r in range(1, 8):
        shifted[r - 1, 0:span, :] = buf[r:r + span, :]


def _rows_at(buf, shifted, start):
    aligned, r = (start // 8) * 8, start % 8
    if r == 0:
        return buf[aligned:aligned + ROW_CHUNK, :]
    return shifted[r - 1, aligned:aligned + ROW_CHUNK, :]


def _window_sums(buf, cols, work, levels, tm, trailing):
    src, src_cols = buf, cols
    for k in range(levels + 1):
        shift = 1 << k
        dst = work.at[k % 2]
        if trailing:
            lo = 8 * (k + 1)
            dst[lo:tm + POOL_HALO, :] = (src[lo:tm + POOL_HALO, src_cols]
                                         + src[lo - shift:tm + POOL_HALO - shift, src_cols])
        else:
            hi = tm + POOL_HALO - 8 * (k + 1)
            dst[0:hi, :] = src[0:hi, src_cols] + src[shift:hi + shift, src_cols]
        src, src_cols = dst, slice(0, POOL_GROUP)
    return src[POOL_HALO:POOL_HALO + tm, src_cols] if trailing else src[0:tm, src_cols]


def _row_spec(tm, width):
    return pl.BlockSpec((tm, width), lambda i: (i, 0))


def _const_spec(shape):
    return pl.BlockSpec(shape, lambda i: (0,) * len(shape))


def _weight_spec(shape):
    return pl.BlockSpec(shape, lambda i: (0,) * len(shape), pipeline_mode=pl.Buffered(1))


def _to_bf16(arrays):
    n = len(arrays)

    def body(*refs):
        for a in range(n):
            refs[n + a][...] = refs[a][...].astype(BF16)

    whole = [pl.BlockSpec(arr.shape, lambda i, nd=arr.ndim: (0,) * nd) for arr in arrays]
    return pl.pallas_call(
        body,
        grid=(1,),
        in_specs=whole,
        out_specs=whole,
        out_shape=[jax.ShapeDtypeStruct(arr.shape, BF16) for arr in arrays],
        compiler_params=_params(("arbitrary",)),
        name="to_bf16",
    )(*arrays)


def _norm_in(x, g_mix, later, tm, carry=None):
    s = x.shape[0]
    n = len(later)

    def body(x_ref, gmix_ref, *refs):
        h1_ref = refs[2 * n]

        @pl.when(pl.program_id(0) == 0)
        def _():
            for a in range(n):
                refs[n + a][...] = refs[a][...].astype(BF16)

        h, _ = _rms_fwd(x_ref[...], gmix_ref[...])
        h1_ref[...] = h.astype(BF16)

    return _pcall(
        body,
        grid=(s // tm,),
        in_specs=[_row_spec(tm, D_MODEL), _const_spec((1, D_MODEL))] + [_weight_spec(w.shape) for w in later],
        out_specs=[_const_spec(w.shape) for w in later] + [_row_spec(tm, D_MODEL)],
        out_shape=[jax.ShapeDtypeStruct(w.shape, BF16) for w in later] + [jax.ShapeDtypeStruct((s, D_MODEL), BF16)],
        scratch_shapes=[],
        name="norm_in",
        args=(x, g_mix) + tuple(later),
        carry=carry,
    )


def _fwd_mix(x, h1, w_in_t, w_out, conv_w, conv_b, ln_g, ln_b, pool_w, pool_scale, tm, carry=None):
    s = x.shape[0]
    nt = s // tm

    def body(x_ref, h1_ref, win_ref, wout_ref, cw_ref, cb_ref, lng_ref, lnb_ref, pw_ref, ps_ref,
             z_ref, u1_ref, pooled_ref, x1_ref, mix_ref, ubuf, vbuf, ush, pbuf):
        i = pl.program_id(0)

        @pl.when(i == 0)
        def _():
            ubuf[0:CONV_HALO, :] = jnp.zeros((CONV_HALO, C_CONV), F32)
            vbuf[0:POOL_HALO, :] = jnp.zeros((POOL_HALO, C_POOL), F32)

        xt = x_ref[...]
        z = _dot_nt(h1_ref[...], win_ref[...])
        z_ref[...] = z
        a = z[:, :C_CONV]
        b = z[:, C_CONV:2 * C_CONV]
        v = z[:, 2 * C_CONV:]
        ubuf[CONV_HALO:CONV_HALO + tm, :] = a * _sigmoid(b)
        vbuf[POOL_HALO:POOL_HALO + tm, :] = v

        _shifted_copies(ubuf, ush, tm)
        for rc in range(tm // ROW_CHUNK):
            base = rc * ROW_CHUNK + CONV_HALO - (CONV_K - 1)
            acc = jnp.broadcast_to(cb_ref[...], (ROW_CHUNK, C_CONV))
            for k in range(CONV_K):
                acc = acc + cw_ref[k:k + 1, :] * _rows_at(ubuf, ush, base + k)
            u1_ref[rc * ROW_CHUNK:(rc + 1) * ROW_CHUNK, :] = acc

        u1 = u1_ref[...]
        mu = jnp.mean(u1, axis=-1, keepdims=True)
        cen = u1 - mu
        rstd = lax.rsqrt(jnp.mean(cen * cen, axis=-1, keepdims=True) + EPS)
        u2 = cen * rstd * lng_ref[...] + lnb_ref[...]
        u = u2 * _sigmoid(u2)

        pos1 = (i * tm + lax.broadcasted_iota(jnp.int32, (tm, 1), 0) + 1).astype(F32)
        parts = [u]
        for g, w in enumerate(POOL_WINDOWS):
            cols = slice(g * POOL_GROUP, (g + 1) * POOL_GROUP)
            vg = v[:, cols]
            tot = _window_sums(vbuf, cols, pbuf, g, tm, trailing=True)
            pooled = tot / jnp.minimum(pos1, float(w)) - vg
            pooled_b = pooled.astype(BF16)
            pooled_ref[:, cols] = pooled_b
            parts.append(_dot(pooled_b, pw_ref[g]) * ps_ref[:, cols])
        mix = jnp.concatenate(parts, axis=-1).astype(BF16)
        mix_ref[...] = mix
        x1_ref[...] = xt + _dot(mix, wout_ref[...])

        ubuf[0:CONV_HALO, :] = ubuf[tm:tm + CONV_HALO, :]
        vbuf[0:POOL_HALO, :] = vbuf[tm:tm + POOL_HALO, :]

    return _pcall(
        body,
        grid=(nt,),
        in_specs=[
            _row_spec(tm, D_MODEL),
            _row_spec(tm, D_MODEL),
            _const_spec((Z_WIDTH, D_MODEL)),
            _const_spec((D_MODEL, D_MODEL)),
            _const_spec((CONV_HALO, C_CONV)),
            _const_spec((1, C_CONV)),
            _const_spec((1, C_CONV)),
            _const_spec((1, C_CONV)),
            _const_spec((len(POOL_WINDOWS), POOL_GROUP, POOL_GROUP)),
            _const_spec((1, C_POOL)),
        ],
        out_specs=[
            _row_spec(tm, Z_WIDTH),
            _row_spec(tm, C_CONV),
            _row_spec(tm, C_POOL),
            _row_spec(tm, D_MODEL),
            _row_spec(tm, D_MODEL),
        ],
        out_shape=[
            jax.ShapeDtypeStruct((s, Z_WIDTH), F32),
            jax.ShapeDtypeStruct((s, C_CONV), F32),
            jax.ShapeDtypeStruct((s, C_POOL), BF16),
            jax.ShapeDtypeStruct((s, D_MODEL), F32),
            jax.ShapeDtypeStruct((s, D_MODEL), BF16),
        ],
        scratch_shapes=[
            pltpu.VMEM((tm + CONV_HALO, C_CONV), F32),
            pltpu.VMEM((tm + POOL_HALO, C_POOL), F32),
            pltpu.VMEM((7, tm + CONV_HALO, C_CONV), F32),
            pltpu.VMEM((2, tm + POOL_HALO, POOL_GROUP), F32),
        ],
        name="fwd_mix",
        args=(x, h1, w_in_t, w_out, conv_w, conv_b, ln_g, ln_b, pool_w, pool_scale),
        carry=carry,
    )


def _ffn_up(x1, w_gu_t, g_ffn, tm, carry=None):
    s = x1.shape[0]

    def body(x1_ref, w_ref, gffn_ref, h2_ref, g_ref, u_ref, act_ref):
        h, _ = _rms_fwd(x1_ref[...], gffn_ref[...])
        h2 = h.astype(BF16)
        h2_ref[...] = h2
        for c in range(D_FF // FF_CHUNK):
            cols = slice(c * FF_CHUNK, (c + 1) * FF_CHUNK)
            g = _dot_nt(h2, w_ref[c * FF_CHUNK:(c + 1) * FF_CHUNK, :])
            u = _dot_nt(h2, w_ref[D_FF + c * FF_CHUNK:D_FF + (c + 1) * FF_CHUNK, :])
            g_ref[:, cols] = g.astype(BF16)
            u_ref[:, cols] = u.astype(BF16)
            act_ref[:, cols] = (g * _sigmoid(g) * u).astype(BF16)

    return _pcall(
        body,
        grid=(s // tm,),
        in_specs=[_row_spec(tm, D_MODEL), _weight_spec((2 * D_FF, D_MODEL)), _const_spec((1, D_MODEL))],
        out_specs=[_row_spec(tm, D_MODEL), _row_spec(tm, D_FF), _row_spec(tm, D_FF), _row_spec(tm, D_FF)],
        out_shape=[
            jax.ShapeDtypeStruct((s, D_MODEL), BF16),
            jax.ShapeDtypeStruct((s, D_FF), BF16),
            jax.ShapeDtypeStruct((s, D_FF), BF16),
            jax.ShapeDtypeStruct((s, D_FF), BF16),
        ],
        scratch_shapes=[],
        name="ffn_up",
        args=(x1, w_gu_t, g_ffn),
        carry=carry,
    )


def _down_ple(x1, act, w_down, p, tgt, w_pg, w_pu_t, g_gate, g_post, g_final, tm):
    s = x1.shape[0]
    nt = s // tm

    def body(x1_ref, act_ref, wd_ref, p_ref, t_ref, wpg_ref, wpu_ref, gg_ref, gp_ref, gf_ref,
             dx2_ref, dx2b_ref, hg_ref, ds_ref, dpe_ref, pb_ref, stats_ref, x2_cur, x2_next):
        i = pl.program_id(0)

        @pl.when(i == 0)
        def _():
            stats_ref[...] = jnp.zeros_like(stats_ref)
            x2_cur[...] = jnp.zeros((tm, D_MODEL), F32)

        def down(c):
            cols = slice(c * 256, (c + 1) * 256)
            x2_next[:, cols] = x1_ref[:, cols] + _dot(act_ref[...], wd_ref[:, cols])

        x2 = x2_cur[...]
        counts = i >= 1

        hg, rg = _rms_fwd(x2, gg_ref[...])
        hg_b = hg.astype(BF16)
        hg_ref[...] = hg_b
        down(0)
        gate = _sigmoid(_dot(hg_b, wpg_ref[...]))
        pb = p_ref[...].astype(BF16)
        pb_ref[...] = pb
        pe = _dot_nt(pb, wpu_ref[...])
        e, rp = _rms_fwd(pe, gp_ref[...])
        x3 = x2 + gate * e
        down(1)
        y, r3 = _rms_fwd(x3, gf_ref[...])
        diff = y - t_ref[...]
        loss = 0.5 * jnp.sum(jnp.sum(diff * diff, axis=-1, keepdims=True), axis=0, keepdims=True) / D_MODEL
        dy = diff * (1.0 / D_MODEL)

        dx3, dgf = _rms_bwd(x3, r3, gf_ref[...], dy)
        down(2)
        dpe, dgp = _rms_bwd(pe, rp, gp_ref[...], dx3 * gate)
        dpe_ref[...] = dpe.astype(BF16)
        ds = (dx3 * e * gate * (1.0 - gate)).astype(BF16)
        ds_ref[...] = ds
        dhg = _dot_nt(ds, wpg_ref[...])
        down(3)
        dxg, dgg = _rms_bwd(x2, rg, gg_ref[...], dhg)
        dx2 = dx3 + dxg
        dx2_ref[...] = dx2
        dx2b_ref[...] = dx2.astype(BF16)
        x2_cur[...] = x2_next[...]

        stats_ref[0:1, :] += jnp.where(counts, dgf, 0.0)
        stats_ref[1:2, :] += jnp.where(counts, dgp, 0.0)
        stats_ref[2:3, :] += jnp.where(counts, dgg, 0.0)
        stats_ref[3:4, :] += jnp.where(counts, jnp.broadcast_to(loss, (1, D_MODEL)), 0.0)

    ahead = lambda width: pl.BlockSpec((tm, width), lambda i: (jnp.minimum(i, nt - 1), 0))
    behind = lambda width: pl.BlockSpec((tm, width), lambda i: (jnp.maximum(i - 1, 0), 0))
    return pl.pallas_call(
        body,
        grid=(nt + 1,),
        in_specs=[
            ahead(D_MODEL),
            ahead(D_FF),
            _weight_spec((D_FF, D_MODEL)),
            behind(D_PLE),
            behind(D_MODEL),
            _weight_spec((D_MODEL, D_MODEL)),
            _weight_spec((D_MODEL, D_PLE)),
            _const_spec((1, D_MODEL)),
            _const_spec((1, D_MODEL)),
            _const_spec((1, D_MODEL)),
        ],
        out_specs=[
            behind(D_MODEL),
            behind(D_MODEL),
            behind(D_MODEL),
            behind(D_MODEL),
            behind(D_MODEL),
            behind(D_PLE),
            _const_spec((8, D_MODEL)),
        ],
        out_shape=[
            jax.ShapeDtypeStruct((s, D_MODEL), F32),
            jax.ShapeDtypeStruct((s, D_MODEL), BF16),
            jax.ShapeDtypeStruct((s, D_MODEL), BF16),
            jax.ShapeDtypeStruct((s, D_MODEL), BF16),
            jax.ShapeDtypeStruct((s, D_MODEL), BF16),
            jax.ShapeDtypeStruct((s, D_PLE), BF16),
            jax.ShapeDtypeStruct((8, D_MODEL), F32),
        ],
        scratch_shapes=[pltpu.VMEM((tm, D_MODEL), F32), pltpu.VMEM((tm, D_MODEL), F32)],
        compiler_params=_params(("arbitrary",)),
        name="down_ple",
    )(x1, act, w_down, p, tgt, w_pg, w_pu_t, g_gate, g_post, g_final)


def _ffn_bwd(dx2, dx2b, x1, g_sav, u_sav, w_gu_t, w_down, g_ffn, tm, carry=None):
    s = x1.shape[0]

    def body(dx2_ref, dx2b_ref, x1_ref, g_ref, u_ref, w_ref, wd_ref, gffn_ref,
             dg_ref, du_ref, dx1_ref, dx1b_ref, stats_ref):
        @pl.when(pl.program_id(0) == 0)
        def _():
            stats_ref[...] = jnp.zeros_like(stats_ref)

        dx2b = dx2b_ref[...]
        nc = D_FF // FF_CHUNK
        dacts = [_dot_nt(dx2b, wd_ref[c * FF_CHUNK:(c + 1) * FF_CHUNK, :]) for c in range(nc)]
        dh2 = jnp.zeros((tm, D_MODEL), F32)
        for c in range(nc):
            cols = slice(c * FF_CHUNK, (c + 1) * FF_CHUNK)
            g = g_ref[:, cols].astype(F32)
            u = u_ref[:, cols].astype(F32)
            sg = _sigmoid(g)
            dg = (dacts[c] * u * sg * (1.0 + g * (1.0 - sg))).astype(BF16)
            du = (dacts[c] * g * sg).astype(BF16)
            dg_ref[:, cols] = dg
            du_ref[:, cols] = du
            dh2 = dh2 + _dot(dg, w_ref[c * FF_CHUNK:(c + 1) * FF_CHUNK, :])
            dh2 = dh2 + _dot(du, w_ref[D_FF + c * FF_CHUNK:D_FF + (c + 1) * FF_CHUNK, :])

        x1 = x1_ref[...]
        r2 = lax.rsqrt(jnp.mean(x1 * x1, axis=-1, keepdims=True) + EPS)
        dxn, dgf = _rms_bwd(x1, r2, gffn_ref[...], dh2)
        dx1 = dx2_ref[...] + dxn
        dx1_ref[...] = dx1
        dx1b_ref[...] = dx1.astype(BF16)
        stats_ref[0:1, :] += dgf

    return _pcall(
        body,
        grid=(s // tm,),
        in_specs=[
            _row_spec(tm, D_MODEL), _row_spec(tm, D_MODEL), _row_spec(tm, D_MODEL),
            _row_spec(tm, D_FF), _row_spec(tm, D_FF),
            _weight_spec((2 * D_FF, D_MODEL)), _weight_spec((D_FF, D_MODEL)), _const_spec((1, D_MODEL)),
        ],
        out_specs=[_row_spec(tm, D_FF), _row_spec(tm, D_FF), _row_spec(tm, D_MODEL), _row_spec(tm, D_MODEL),
                   _const_spec((8, D_MODEL))],
        out_shape=[
            jax.ShapeDtypeStruct((s, D_FF), BF16),
            jax.ShapeDtypeStruct((s, D_FF), BF16),
            jax.ShapeDtypeStruct((s, D_MODEL), F32),
            jax.ShapeDtypeStruct((s, D_MODEL), BF16),
            jax.ShapeDtypeStruct((8, D_MODEL), F32),
        ],
        scratch_shapes=[],
        name="ffn_bwd",
        args=(dx2, dx2b, x1, g_sav, u_sav, w_gu_t, w_down, g_ffn),
        carry=carry,
    )


def _bwd_mix(dx1, dx1b, x, z, u1, pooled, w_in_t, w_out, g_mix, conv_w, ln_g, ln_b, pool_w, pool_scale, tm,
             carry=None):
    s = x.shape[0]
    nt = s // tm

    def body(dx1_ref, dx1b_ref, x_ref, z_ref, u1_ref, pooled_ref, win_ref, wout_ref, gmix_ref, cw_ref,
             lng_ref, lnb_ref, pw_ref, ps_ref,
             dx_ref, dz_ref, vec_ref, dcw_ref, dpw_ref, dubuf, dvbuf, u0buf, du0buf, dush, pbuf):
        i = pl.program_id(0)
        tile = nt - 1 - i

        @pl.when(i == 0)
        def _():
            vec_ref[...] = jnp.zeros_like(vec_ref)
            dcw_ref[...] = jnp.zeros_like(dcw_ref)
            dpw_ref[...] = jnp.zeros_like(dpw_ref)
            dubuf[tm:tm + CONV_HALO, :] = jnp.zeros((CONV_HALO, C_CONV), F32)
            dvbuf[tm:tm + POOL_HALO, :] = jnp.zeros((POOL_HALO, C_POOL), F32)

        dmix = _dot_nt(dx1b_ref[...], wout_ref[...])
        du = dmix[:, :C_CONV]
        dq = dmix[:, C_CONV:]

        pos1 = (tile * tm + lax.broadcasted_iota(jnp.int32, (tm, 1), 0) + 1).astype(F32)
        dpooled_parts = []
        dps_rows = []
        for g, w in enumerate(POOL_WINDOWS):
            cols = slice(g * POOL_GROUP, (g + 1) * POOL_GROUP)
            pooled_b = pooled_ref[:, cols]
            mixed = _dot(pooled_b, pw_ref[g])
            dqg = dq[:, cols]
            dps_rows.append(jnp.sum(dqg * mixed, axis=0, keepdims=True))
            dmixed = (dqg * ps_ref[:, cols]).astype(BF16)
            dpw_ref[g] += _dot_tn(pooled_b, dmixed)
            dpooled = _dot_nt(dmixed, pw_ref[g])
            dpooled_parts.append(dpooled)
            dvbuf[0:tm, cols] = dpooled / jnp.minimum(pos1, float(w))
        vec_ref[4:5, 0:C_POOL] += jnp.concatenate(dps_rows, axis=-1)
        dv_parts = []
        for g, w in enumerate(POOL_WINDOWS):
            cols = slice(g * POOL_GROUP, (g + 1) * POOL_GROUP)
            tot = _window_sums(dvbuf, cols, pbuf, g, tm, trailing=False)
            dv_parts.append(tot - dpooled_parts[g])

        u1 = u1_ref[...]
        mu = jnp.mean(u1, axis=-1, keepdims=True)
        cen = u1 - mu
        rstd = lax.rsqrt(jnp.mean(cen * cen, axis=-1, keepdims=True) + EPS)
        xhat = cen * rstd
        u2 = xhat * lng_ref[...] + lnb_ref[...]
        sg2 = _sigmoid(u2)
        du2 = du * sg2 * (1.0 + u2 * (1.0 - sg2))
        vec_ref[1:2, 0:C_CONV] += jnp.sum(du2 * xhat, axis=0, keepdims=True)
        vec_ref[2:3, 0:C_CONV] += jnp.sum(du2, axis=0, keepdims=True)
        t1 = du2 * lng_ref[...]
        du1 = rstd * (t1 - jnp.mean(t1, axis=-1, keepdims=True)
                      - xhat * jnp.mean(t1 * xhat, axis=-1, keepdims=True))
        vec_ref[3:4, 0:C_CONV] += jnp.sum(du1, axis=0, keepdims=True)
        dubuf[0:tm, :] = du1

        zt = z_ref[...]
        a = zt[:, :C_CONV]
        sgb = _sigmoid(zt[:, C_CONV:2 * C_CONV])
        u0buf[...] = a * sgb

        _shifted_copies(dubuf, dush, tm)
        for rc in range(tm // ROW_CHUNK):
            r0 = rc * ROW_CHUNK
            acc = jnp.zeros((ROW_CHUNK, C_CONV), F32)
            for k in range(CONV_K):
                acc = acc + cw_ref[k:k + 1, :] * _rows_at(dubuf, dush, r0 + (CONV_K - 1) - k)
            du0buf[r0:r0 + ROW_CHUNK, :] = acc
        for k in range(CONV_K):
            acc = jnp.zeros((ROW_CHUNK, C_CONV), F32)
            for rc in range(tm // ROW_CHUNK):
                r0 = rc * ROW_CHUNK
                acc = acc + u0buf[r0:r0 + ROW_CHUNK, :] * _rows_at(dubuf, dush, r0 + (CONV_K - 1) - k)
            dcw_ref[k:k + 1, :] += jnp.sum(acc, axis=0, keepdims=True)
        du0 = du0buf[...]

        da = du0 * sgb
        db = du0 * a * sgb * (1.0 - sgb)
        dz = jnp.concatenate([da, db] + dv_parts, axis=-1).astype(BF16)
        dz_ref[...] = dz

        xt = x_ref[...]
        r1 = lax.rsqrt(jnp.mean(xt * xt, axis=-1, keepdims=True) + EPS)
        dh1 = _dot(dz, win_ref[...])
        dxn, dgm = _rms_bwd(xt, r1, gmix_ref[...], dh1)
        dx_ref[...] = dx1_ref[...] + dxn
        vec_ref[0:1, :] += dgm

        dubuf[tm:tm + CONV_HALO, :] = dubuf[0:CONV_HALO, :]
        dvbuf[tm:tm + POOL_HALO, :] = dvbuf[0:POOL_HALO, :]

    rev = lambda width: pl.BlockSpec((tm, width), lambda i: (nt - 1 - i, 0))
    return _pcall(
        body,
        grid=(nt,),
        in_specs=[
            rev(D_MODEL), rev(D_MODEL), rev(D_MODEL), rev(Z_WIDTH), rev(C_CONV), rev(C_POOL),
            _const_spec((Z_WIDTH, D_MODEL)),
            _const_spec((D_MODEL, D_MODEL)),
            _const_spec((1, D_MODEL)),
            _const_spec((CONV_HALO, C_CONV)),
            _const_spec((1, C_CONV)),
            _const_spec((1, C_CONV)),
            _const_spec((len(POOL_WINDOWS), POOL_GROUP, POOL_GROUP)),
            _const_spec((1, C_POOL)),
        ],
        out_specs=[
            rev(D_MODEL), rev(Z_WIDTH),
            _const_spec((8, D_MODEL)),
            _const_spec((CONV_HALO, C_CONV)),
            _const_spec((len(POOL_WINDOWS), POOL_GROUP, POOL_GROUP)),
        ],
        out_shape=[
            jax.ShapeDtypeStruct((s, D_MODEL), F32),
            jax.ShapeDtypeStruct((s, Z_WIDTH), BF16),
            jax.ShapeDtypeStruct((8, D_MODEL), F32),
            jax.ShapeDtypeStruct((CONV_HALO, C_CONV), F32),
            jax.ShapeDtypeStruct((len(POOL_WINDOWS), POOL_GROUP, POOL_GROUP), F32),
        ],
        scratch_shapes=[
            pltpu.VMEM((tm + CONV_HALO, C_CONV), F32),
            pltpu.VMEM((tm + POOL_HALO, C_POOL), F32),
            pltpu.VMEM((tm, C_CONV), F32),
            pltpu.VMEM((tm, C_CONV), F32),
            pltpu.VMEM((7, tm + CONV_HALO, C_CONV), F32),
            pltpu.VMEM((2, tm + POOL_HALO, POOL_GROUP), F32),
        ],
        name="bwd_mix",
        args=(dx1, dx1b, x, z, u1, pooled, w_in_t, w_out, g_mix, conv_w, ln_g, ln_b, pool_w, pool_scale),
        carry=carry,
    )


def _grad_matmul(a, b, bm, name, a2=None, carry=None):
    s, ma = a.shape
    nb = b.shape[1]
    na = ma // bm
    if a2 is None:
        def body(a_ref, b_ref, o_ref):
            o_ref[...] = _dot_tn(a_ref[...], b_ref[...]).astype(BF16)

        lhs_specs = [pl.BlockSpec((s, bm), lambda i: (0, i))]
        lhs = (a,)
        steps = na
    else:
        def body(a_ref, a2_ref, b_ref, o_ref):
            i = pl.program_id(0)

            @pl.when(i < na)
            def _():
                o_ref[...] = _dot_tn(a_ref[...], b_ref[...]).astype(BF16)

            @pl.when(i >= na)
            def _():
                o_ref[...] = _dot_tn(a2_ref[...], b_ref[...]).astype(BF16)

        lhs_specs = [pl.BlockSpec((s, bm), lambda i: (0, jnp.minimum(i, na - 1))),
                     pl.BlockSpec((s, bm), lambda i: (0, jnp.maximum(i - na, 0)))]
        lhs = (a, a2)
        steps = 2 * na

    outs, carried = _pcall(
        body,
        grid=(steps,),
        in_specs=lhs_specs + [pl.BlockSpec((s, nb), lambda i: (0, 0))],
        out_specs=[pl.BlockSpec((bm, nb), lambda i: (i, 0))],
        out_shape=[jax.ShapeDtypeStruct((steps * bm, nb), BF16)],
        scratch_shapes=[],
        name=name,
        args=lhs + (b,),
        carry=carry,
    )
    return outs[0], carried


def _reduce_scatter(grads, small, name):
    n, ns = len(grads), len(small)
    shapes = [g.shape[1:] for g in grads]
    gather = _Carry()
    _carry_gather(gather, small, 1, 2)

    def body(*refs):
        g = refs[:n]
        out = refs[n + ns:2 * n + ns]
        scr = refs[2 * (n + ns):]
        own, loc, r1, r2 = scr[:n], scr[n:2 * n], scr[2 * n:3 * n], scr[3 * n:4 * n]
        load_sems, s1, q1, s2, q2 = scr[4 * n:4 * n + 5]
        gather_refs = (refs[n:n + ns], refs[2 * n + ns:2 * (n + ns)], scr[4 * n + 5:])
        me, sib, chips = _place()
        c = me[2]

        _handshake(EVERY_PEER)
        gather.stage(0, *gather_refs, handshake=False)
        loads = []
        sends = []
        for a in range(n):
            ld = [pltpu.make_async_copy(g[a].at[_block(*chip, c)], loc[a].at[j], load_sems.at[a, j])
                  for j, chip in enumerate(chips)]
            ld.append(pltpu.make_async_copy(g[a].at[_block(*me)], own[a], load_sems.at[a, 3]))
            for cp in ld:
                cp.start()
            loads.append(ld)
            blocks = [(*chip, 1 - c) for chip in chips] + [sib]
            for j, blk in enumerate(blocks):
                cp = pltpu.make_async_remote_copy(
                    src_ref=g[a].at[_block(*blk)], dst_ref=r1[a].at[j],
                    send_sem=s1.at[a, j], recv_sem=q1.at[a, j], device_id=sib, device_id_type=MESH)
                cp.start()
                sends.append(cp)

        def from_sibling(a, j):
            return pltpu.make_async_remote_copy(
                src_ref=r1[a].at[j], dst_ref=r1[a].at[j], send_sem=s1.at[a, j], recv_sem=q1.at[a, j],
                device_id=sib, device_id_type=MESH)

        def partial(a, j, chip):
            return pltpu.make_async_remote_copy(
                src_ref=loc[a].at[j], dst_ref=r2[a].at[j], send_sem=s2.at[a, j], recv_sem=q2.at[a, j],
                device_id=(*chip, c), device_id_type=MESH)

        gather.stage(1, *gather_refs)
        for a in range(n):
            for j, chip in enumerate(chips):
                loads[a][j].wait()
                from_sibling(a, j).wait_recv()
                loc[a][j] = (loc[a][j].astype(F32) + r1[a][j].astype(F32)).astype(BF16)
                cp = partial(a, j, chip)
                cp.start()
                sends.append(cp)
        gather.stage(2, *gather_refs)
        for a in range(n):
            loads[a][3].wait()
            from_sibling(a, 3).wait_recv()
            acc = own[a][...].astype(F32) + r1[a][3].astype(F32)
            for j, chip in enumerate(chips):
                partial(a, j, chip).wait_recv()
                acc = acc + r2[a][j].astype(F32)
            out[a][...] = acc
        for cp in sends:
            cp.wait_send()
        gather.drain(*gather_refs)

    any_spec = pl.BlockSpec(memory_space=pl.ANY)
    vmem_spec = pl.BlockSpec(memory_space=pltpu.VMEM)
    res = pl.pallas_call(
        body,
        in_specs=[any_spec] * (n + ns),
        out_specs=[vmem_spec] * n + [any_spec] * ns,
        out_shape=[jax.ShapeDtypeStruct(sh, F32) for sh in shapes] + gather.out_shapes,
        scratch_shapes=(
            [pltpu.VMEM(sh, BF16) for sh in shapes]
            + [pltpu.VMEM((3,) + sh, BF16) for sh in shapes]
            + [pltpu.VMEM((4,) + sh, BF16) for sh in shapes]
            + [pltpu.VMEM((3,) + sh, BF16) for sh in shapes]
            + [pltpu.SemaphoreType.DMA((n, 4)),
               pltpu.SemaphoreType.DMA((n, 4)), pltpu.SemaphoreType.DMA((n, 4)),
               pltpu.SemaphoreType.DMA((n, 3)), pltpu.SemaphoreType.DMA((n, 3))]
            + gather.sem_shapes()
        ),
        compiler_params=_params(collective_id=_collective_id(EVERY_PEER)),
        name=name,
    )(*grads, *small)
    return res[:n], res[n:]


def _pair_add(grads, from_sib, blks, name):
    n = len(grads)

    def body(blk_ref, *refs):
        for a in range(n):
            refs[2 * n + a][...] = (refs[a][...].astype(F32) + refs[n + a][...].astype(F32)).astype(BF16)

    mine = [pl.BlockSpec((None,) + g.shape[1:], lambda j, b: (b[j], 0, 0)) for g in grads]
    same = [pl.BlockSpec((None,) + g.shape[1:], lambda j, b: (j, 0, 0)) for g in grads]
    return pl.pallas_call(
        body,
        grid_spec=pltpu.PrefetchScalarGridSpec(
            num_scalar_prefetch=1, grid=(4,), in_specs=mine + same, out_specs=same),
        out_shape=[jax.ShapeDtypeStruct((4,) + g.shape[1:], BF16) for g in grads],
        compiler_params=_params(("arbitrary",)),
        name=name,
    )(blks, *grads, *from_sib)


def _chip_sum(parts, from_chips, name):
    n = len(parts)

    def body(*refs):
        for a in range(n):
            acc = refs[a][...].astype(F32)
            for j in range(3):
                acc = acc + refs[n + a][j].astype(F32)
            refs[2 * n + a][...] = acc

    half = [p.shape[1] // 2 for p in parts]
    return pl.pallas_call(
        body,
        grid=(2,),
        in_specs=([pl.BlockSpec((None, h, p.shape[2]), lambda i: (3, i, 0)) for p, h in zip(parts, half)]
                  + [pl.BlockSpec((3, h, p.shape[2]), lambda i: (0, i, 0)) for p, h in zip(parts, half)]),
        out_specs=[pl.BlockSpec((h, p.shape[2]), lambda i: (i, 0)) for p, h in zip(parts, half)],
        out_shape=[jax.ShapeDtypeStruct(p.shape[1:], F32) for p in parts],
        compiler_params=_params(("arbitrary",)),
        name=name,
    )(*parts, *from_chips)


def _adam_math(w, g, m, v):
    nm = ADAM_B1 * m + (1.0 - ADAM_B1) * g
    nv = ADAM_B2 * v + (1.0 - ADAM_B2) * (g * g)
    m_hat = nm / (1.0 - ADAM_B1 ** ADAM_STEP)
    v_hat = nv / (1.0 - ADAM_B2 ** ADAM_STEP)
    return -ADAM_LR * (m_hat / (jnp.sqrt(v_hat) + ADAM_EPS) + ADAM_WD * w), nm, nv


def _sum_adamw(items, name, carry=None):
    n = len(items)

    def body(*refs):
        for a in range(n):
            p_ref, f_ref, w_ref, m_ref, v_ref = refs[5 * a:5 * a + 5]
            g_ref, d_ref, nm_ref, nv_ref = refs[5 * n + 4 * a:5 * n + 4 * a + 4]
            g = p_ref[...].astype(F32)
            for j in range(3):
                g = g + f_ref[j].astype(F32)
            g_ref[...] = g
            d_ref[...], nm_ref[...], nv_ref[...] = _adam_math(w_ref[...], g, m_ref[...], v_ref[...])

    in_specs, out_specs, out_shape, args = [], [], [], []
    for part, from_chips, w, m, v in items:
        r, c = w.shape
        spec = pl.BlockSpec((r // 2, c), lambda i: (i, 0))
        in_specs += [pl.BlockSpec((None, r // 2, c), lambda i: (3, i, 0)),
                     pl.BlockSpec((3, r // 2, c), lambda i: (0, i, 0)), spec, spec, spec]
        out_specs += [spec] * 4
        out_shape += [jax.ShapeDtypeStruct((r, c), F32)] * 4
        args += [part, from_chips, w, m, v]
    outs, carried = _pcall(body, grid=(2,), in_specs=in_specs, out_specs=out_specs, out_shape=out_shape,
                           scratch_shapes=[], name=name, args=tuple(args), carry=carry)
    return [tuple(outs[4 * a:4 * a + 4]) for a in range(n)], carried


def _small_update(gathered, layout, params, name):
    ng, npar = len(gathered), len(params)

    def body(*refs):
        parts = refs[:ng]
        prm = refs[ng:ng + 3 * npar]
        tot_refs = refs[ng + 3 * npar:2 * ng + 3 * npar]
        out = refs[2 * ng + 3 * npar:]
        tots = []
        for a in range(ng):
            acc = parts[a][0].astype(F32)
            for d in range(1, N_DEV):
                acc = acc + parts[a][d].astype(F32)
            tot_refs[a][...] = acc
            tots.append(acc)
        for i, (a, row, width) in enumerate(layout):
            g = tots[a] if row is None else tots[a][row:row + 1, :width]
            delta, nm, nv = _adam_math(prm[3 * i][...], g, prm[3 * i + 1][...], prm[3 * i + 2][...])
            out[4 * i][...] = g
            out[4 * i + 1][...] = delta
            out[4 * i + 2][...] = nm
            out[4 * i + 3][...] = nv

    flat = [t for prm in params for t in prm]
    res = pl.pallas_call(
        body,
        out_shape=([jax.ShapeDtypeStruct(g.shape[1:], F32) for g in gathered]
                   + [jax.ShapeDtypeStruct(prm[0].shape, F32) for prm in params for _ in range(4)]),
        compiler_params=pltpu.CompilerParams(vmem_limit_bytes=V7X_VMEM_LIMIT),
        name=name,
    )(*gathered, *flat)
    return res[:ng], [tuple(res[ng + 4 * i:ng + 4 * i + 4]) for i in range(npar)]


def _adamw(w, g, m, v, name):
    rows, cols = w.shape
    br = rows
    for cand in (512, 256, 128):
        if rows % cand == 0 and rows > cand:
            br = cand
            break

    def body(w_ref, g_ref, m_ref, v_ref, d_ref, nm_ref, nv_ref):
        d_ref[...], nm_ref[...], nv_ref[...] = _adam_math(w_ref[...], g_ref[...], m_ref[...], v_ref[...])

    spec = pl.BlockSpec((br, cols), lambda i: (i, 0))
    shape = jax.ShapeDtypeStruct((rows, cols), F32)
    return pl.pallas_call(
        body,
        grid=(rows // br,),
        in_specs=[spec] * 4,
        out_specs=[spec] * 3,
        out_shape=[shape] * 3,
        compiler_params=_params(("arbitrary",)),
        name=name,
    )(w, g, m, v)


def _adamw_whole(w, g, m, v, name):
    def body(w_ref, g_ref, m_ref, v_ref, d_ref, nm_ref, nv_ref):
        d_ref[...], nm_ref[...], nv_ref[...] = _adam_math(w_ref[...], g_ref[...], m_ref[...], v_ref[...])

    return pl.pallas_call(
        body,
        out_shape=[jax.ShapeDtypeStruct(w.shape, F32)] * 3,
        name=name,
    )(w, g, m, v)


def _by_device(full):
    return full.reshape(N_DEV, full.shape[0] // N_DEV, full.shape[1])


def kernel(x, p, g_mix, w_in, conv_w, conv_b, ln_g, ln_b, pool_w, pool_scale, w_out, g_ffn, w_gate_up, w_down, g_ple_gate, w_ple_gate, w_ple_up, g_ple_post, g_final, loss_target, m_g_mix, m_w_in, m_conv_w, m_conv_b, m_ln_g, m_ln_b, m_pool_w, m_pool_scale, m_w_out, m_g_ffn, m_w_gate_up, m_w_down, m_g_ple_gate, m_w_ple_gate, m_w_ple_up, m_g_ple_post, m_g_final, v_g_mix, v_w_in, v_conv_w, v_conv_b, v_ln_g, v_ln_b, v_pool_w, v_pool_scale, v_w_out, v_g_ffn, v_w_gate_up, v_w_down, v_g_ple_gate, v_w_ple_gate, v_w_ple_up, v_g_ple_post, v_g_final):
    seq = x.shape[1]
    tile = min(TOKEN_TILE, seq)
    xs = x[0]
    ps = p[0, 0]
    tgt = loss_target[0]
    ax, ay, ac = lax.axis_index("x"), lax.axis_index("y"), lax.axis_index("c")
    me = _block(ax, ay, ac)
    blks = jnp.stack([_block(1 - ax, ay, ac), _block(ax, 1 - ay, ac), _block(1 - ax, 1 - ay, ac), me]).astype(jnp.int32)
    rows = lambda gth: gth.reshape((-1,) + gth.shape[2:])

    sh_in, sh_out, pool_w_b = _to_bf16([
        w_in[0].T,
        w_out[0],
        pool_w[0]])
    carry = _Carry()
    _carry_gather(carry, [sh_in, sh_out,
                          jnp.pad(conv_w[0].T, ((0, 0), (0, CONV_HALO - CONV_K)))], 8, 15)
    (sh_gu, sh_down, sh_pg, sh_pu, h1), first = _norm_in(xs, g_mix, [
        w_gate_up[0].T,
        w_down[0],
        w_ple_gate[0],
        w_ple_up[0].T,
    ], tile, carry)
    w_in_t, w_out_f, conv_w_t = [rows(gth) for gth in first]
    conv_w_f = conv_w_t.T

    carry = _Carry()
    _carry_gather(carry, [sh_gu], 8, 12)
    (z, u1, pooled, x1, mix), (w_gu_all,) = _fwd_mix(
        xs, h1, w_in_t, w_out_f, conv_w_f, conv_b, ln_g, ln_b, pool_w_b, pool_scale, tile, carry)
    w_gu_t = rows(w_gu_all)

    carry = _Carry()
    _carry_gather(carry, [sh_down, sh_pg, sh_pu], 6, 10)
    (h2, g_sav, u_sav, act), late = _ffn_up(x1, w_gu_t, g_ffn, tile, carry)
    w_down_f, w_pg_f, w_pu_t = [rows(gth) for gth in late]
    dx2, dx2b, hg, ds, dpe, pb, stats_ple = _down_ple(
        x1, act, w_down_f, ps, tgt, w_pg_f, w_pu_t, g_ple_gate, g_ple_post, g_final.reshape(1, D_MODEL),
        tile)

    d_w_down, _ = _grad_matmul(act, dx2b, GRAD_ROWS, "grad_w_down")
    d_w_pg, _ = _grad_matmul(hg, ds, GRAD_ROWS, "grad_w_ple_gate")
    d_w_pu_t, _ = _grad_matmul(dpe, pb, GRAD_ROWS, "grad_w_ple_up")
    early = [_by_device(d_w_pg), _by_device(d_w_pu_t), _by_device(d_w_down)]
    carry = _Carry()
    _carry_pair(carry, early)
    (dg, du, dx1, dx1b, stats_ffn), from_sib = _ffn_bwd(dx2, dx2b, x1, g_sav, u_sav, w_gu_t, w_down_f, g_ffn,
                                                        tile, carry)
    early_parts = _pair_add(early, from_sib, blks, "pair_add_early")

    carry = _Carry()
    _carry_chip(carry, early_parts)
    d_w_gu_t, early_chips = _grad_matmul(dg, h2, GRAD_ROWS, "grad_w_gate_up", a2=du, carry=carry)
    carry = _Carry()
    _carry_pair(carry, [_by_device(d_w_gu_t)])
    d_w_out, gu_sib = _grad_matmul(mix, dx1b, GRAD_ROWS, "grad_w_out", carry=carry)
    gu_parts = _pair_add([_by_device(d_w_gu_t)], gu_sib, blks, "pair_add_gate_up")

    carry = _Carry()
    _carry_chip(carry, gu_parts)
    _carry_pair(carry, [_by_device(d_w_out)])
    (dx, dz, vec_mix, dconv_w_part, dpool_w_part), carried = _bwd_mix(
        dx1, dx1b, xs, z, u1, pooled, w_in_t, w_out_f, g_mix, conv_w_f, ln_g, ln_b, pool_w_b, pool_scale,
        tile, carry)
    gu_chips, out_sib = carried[:1], carried[1:]
    out_parts = _pair_add([_by_device(d_w_out)], out_sib, blks, "pair_add_out")

    carry = _Carry()
    _carry_chip(carry, out_parts)
    d_w_in_t, out_chips = _grad_matmul(dz, h1, GRAD_ROWS, "grad_w_in", carry=carry)
    (gr_w_in_t,), small = _reduce_scatter(
        [_by_device(d_w_in_t)],
        [vec_mix, stats_ple, stats_ffn, dconv_w_part,
         dpool_w_part.astype(BF16).reshape(-1, POOL_GROUP)],
        "scatter_last")
    small = list(small[:4]) + [small[4].reshape((N_DEV,) + dpool_w_part.shape)]

    vec_names = ["g_mix", "ln_g", "ln_b", "conv_b", "pool_scale", "g_final", "g_ple_post", "g_ple_gate", "g_ffn",
                 "pool_w"]
    layout = [(0, 0, D_MODEL), (0, 1, C_CONV), (0, 2, C_CONV), (0, 3, C_CONV), (0, 4, C_POOL),
              (1, 0, D_MODEL), (1, 1, D_MODEL), (1, 2, D_MODEL), (2, 0, D_MODEL), (4, None, None)]
    as_row = lambda t: t.reshape(1, D_MODEL)
    params = [(g_mix, m_g_mix, v_g_mix), (ln_g, m_ln_g, v_ln_g), (ln_b, m_ln_b, v_ln_b),
              (conv_b, m_conv_b, v_conv_b), (pool_scale, m_pool_scale, v_pool_scale),
              (as_row(g_final), as_row(m_g_final), as_row(v_g_final)),
              (g_ple_post, m_g_ple_post, v_g_ple_post), (g_ple_gate, m_g_ple_gate, v_g_ple_gate),
              (g_ffn, m_g_ffn, v_g_ffn), (pool_w[0], m_pool_w[0], v_pool_w[0])]
    tots, small_upd = _small_update(small, layout, params, "small_update")
    loss = tots[1][3, 0]
    upd = {}
    for nm, res, prm in zip(vec_names, small_upd, [g_mix, ln_g, ln_b, conv_b, pool_scale, g_final, g_ple_post,
                                                    g_ple_gate, g_ffn, pool_w]):
        upd[nm] = tuple(t.reshape(prm.shape) for t in res)
    gr_conv_w = lax.dynamic_slice_in_dim(tots[3][:CONV_K], me * (C_CONV // N_DEV), C_CONV // N_DEV, axis=1)

    natural = lambda t: t[None]
    turned = lambda t: t.T[None]
    (res_pg, res_gu, res_out, res_down), _ = _sum_adamw([
        (early_parts[0], early_chips[0], w_ple_gate[0], m_w_ple_gate[0], v_w_ple_gate[0]),
        (gu_parts[0], gu_chips[0], w_gate_up[0].T, m_w_gate_up[0].T, v_w_gate_up[0].T),
        (out_parts[0], out_chips[0], w_out[0], m_w_out[0], v_w_out[0]),
        (early_parts[2], early_chips[2], w_down[0], m_w_down[0], v_w_down[0])], "adamw_big")
    upd["w_ple_gate"] = tuple(natural(t) for t in res_pg)
    upd["w_gate_up"] = tuple(turned(t) for t in res_gu)
    upd["w_out"] = tuple(natural(t) for t in res_out)
    upd["w_down"] = tuple(natural(t) for t in res_down)
    (gr_w_pu_t,) = _chip_sum(early_parts[1:2], early_chips[1:2], "chip_sum")
    plain = [
        ("w_in", w_in[0].T, gr_w_in_t, m_w_in[0].T, v_w_in[0].T, True),
        ("w_ple_up", w_ple_up[0], gr_w_pu_t.T, m_w_ple_up[0], v_w_ple_up[0], False),
    ]
    for nm, w_, g_, m_, v_, transposed in plain:
        res = (g_,) + tuple(_adamw(w_, g_, m_, v_, "adamw_" + nm))
        upd[nm] = tuple((t.T if transposed else t)[None] for t in res)
    taps = lambda t: t.transpose(1, 0, 2)
    g_taps = gr_conv_w[:, None, :]
    res = (g_taps,) + tuple(_adamw_whole(taps(conv_w), g_taps, taps(m_conv_w), taps(v_conv_w), "adamw_conv_w"))
    upd["conv_w"] = tuple(taps(t) for t in res)

    order = ["g_mix", "w_in", "conv_w", "conv_b", "ln_g", "ln_b", "pool_w", "pool_scale", "w_out", "g_ffn",
             "w_gate_up", "w_down", "g_ple_gate", "w_ple_gate", "w_ple_up", "g_ple_post", "g_final"]
    outs = [loss, dx[None]]
    for k in range(4):
        outs += [upd[nm][k] for nm in order]
    return tuple(outs)
```

```python
import functools

import jax
import jax.numpy as jnp
from jax import lax
from jax.experimental import pallas as pl
from jax.experimental.pallas import tpu as pltpu

D_MODEL = 1024
C_CONV = 512
C_POOL = 512
Z_WIDTH = 2 * C_CONV + C_POOL
POOL_WINDOWS = (2, 4, 8, 16)
POOL_GROUP = 128
CONV_K = 31
D_FF = 2816
D_PLE = 256
EPS = 1e-6
N_DEV = 8

ADAM_LR = 0.001
ADAM_B1 = 0.9
ADAM_B2 = 0.999
ADAM_EPS = 1e-08
ADAM_WD = 0.01
ADAM_STEP = 10

CONV_HALO = 32
POOL_HALO = 32
ROW_CHUNK = 32
TOKEN_TILE = 256
GRAD_ROWS = 256
V7X_VMEM_LIMIT = 56 * 1024 * 1024
FF_CHUNK = D_FF // 2

BF16 = jnp.bfloat16
F32 = jnp.float32
MESH = pl.DeviceIdType.MESH


def _dot(a, b):
    return lax.dot_general(a, b, (((1,), (0,)), ((), ())), preferred_element_type=F32)


def _dot_nt(a, b):
    return lax.dot_general(a, b, (((1,), (1,)), ((), ())), preferred_element_type=F32)


def _dot_tn(a, b):
    return lax.dot_general(a, b, (((0,), (0,)), ((), ())), preferred_element_type=F32)


def _rms_fwd(x, g):
    r = lax.rsqrt(jnp.mean(x * x, axis=-1, keepdims=True) + EPS)
    return x * r * g, r


def _rms_bwd(x, r, g, dy):
    xr = x * r
    dg = jnp.sum(dy * xr, axis=0, keepdims=True)
    dyg = dy * g
    dx = r * (dyg - xr * jnp.mean(dyg * xr, axis=-1, keepdims=True))
    return dx, dg


def _sigmoid(x):
    return jax.nn.sigmoid(x)


def _params(sem=None, collective_id=None):
    return pltpu.CompilerParams(dimension_semantics=sem, vmem_limit_bytes=V7X_VMEM_LIMIT,
                                collective_id=collective_id)


def _place():
    x, y, c = lax.axis_index("x"), lax.axis_index("y"), lax.axis_index("c")
    chips = [(1 - x, y), (x, 1 - y), (1 - x, 1 - y)]
    return (x, y, c), (x, y, 1 - c), chips


def _block(px, py, pc):
    return 4 * px + 2 * py + pc


LOCAL_DMA_PRIORITY = 1
EVERY_PEER = ("sib", 0, 1, 2)
PEER_SET_IDS = {("sib",): 0, ("0", "1", "sib"): 1, ("0", "1", "2"): 2, ("0", "1", "2", "sib"): 3}


def _collective_id(peers):
    return PEER_SET_IDS[tuple(sorted(str(p) for p in peers))]


def _handshake(peers):
    me, sib, chips = _place()
    barrier = pltpu.get_barrier_semaphore()
    for p in peers:
        to = sib if p == "sib" else (*chips[p], me[2])
        pl.semaphore_signal(barrier, inc=1, device_id=to, device_id_type=MESH)
    pl.semaphore_wait(barrier, len(peers))


class _Carry:
    def __init__(self):
        self.inputs = []
        self.out_shapes = []
        self.copies = []
        self.locals = []

    def add_input(self, arr):
        self.inputs.append(arr)
        return len(self.inputs) - 1

    def add_output(self, shape, dtype):
        self.out_shapes.append(jax.ShapeDtypeStruct(shape, dtype))
        return len(self.out_shapes) - 1

    def local(self, src_idx, dst_idx, dst_blk):
        self.locals.append((src_idx, dst_idx, dst_blk))

    def copy(self, src, dst_idx, dst_blk, got_blk, peer, step=0, after=()):
        self.copies.append(dict(src=src, dst_idx=dst_idx, dst_blk=dst_blk, got_blk=got_blk, peer=peer, step=step,
                                after=tuple(after)))
        return len(self.copies) - 1

    def sem_shapes(self):
        return [pltpu.SemaphoreType.DMA((max(1, len(self.copies)),)),
                pltpu.SemaphoreType.DMA((max(1, len(self.copies)),)),
                pltpu.SemaphoreType.DMA((max(1, len(self.locals)),))]

    @staticmethod
    def _view(ref, where):
        if isinstance(where, tuple):
            blk, row0, nrows = where
            return ref.at[blk, pl.ds(row0, nrows)]
        return ref.at[where]

    def _desc(self, k, ins, outs, sems, place):
        cp = self.copies[k]
        me, sib, chips = place
        kind, idx, blk = cp["src"]
        src = (ins if kind == "in" else outs)[idx]
        if blk is not None:
            src = self._view(src, blk(*place))
        to = sib if cp["peer"] == "sib" else (*chips[cp["peer"]], me[2])
        return pltpu.make_async_remote_copy(
            src_ref=src, dst_ref=self._view(outs[cp["dst_idx"]], cp["dst_blk"](*place)),
            send_sem=sems[0].at[k], recv_sem=sems[1].at[k], device_id=to, device_id_type=MESH)

    def _arrival(self, k, outs, sems, place):
        cp = self.copies[k]
        got = self._view(outs[cp["dst_idx"]], cp["got_blk"](*place))
        return pltpu.make_async_remote_copy(
            src_ref=got, dst_ref=got, send_sem=sems[0].at[k], recv_sem=sems[1].at[k],
            device_id=place[0], device_id_type=MESH)

    def _local(self, n, ins, outs, sems, place):
        src_idx, dst_idx, blk = self.locals[n]
        return pltpu.make_async_copy(ins[src_idx], outs[dst_idx].at[blk(*place)], sems[2].at[n])

    def stages(self):
        return sorted({0} | {cp["step"] for cp in self.copies})

    def peers(self):
        return sorted({cp["peer"] for cp in self.copies}, key=str)

    def stage(self, s, ins, outs, sems, handshake=True):
        place = _place()
        if s == 0:
            if handshake:
                _handshake(self.peers())
            self._waited = set()
        for k, cp in enumerate(self.copies):
            if cp["step"] != s:
                continue
            for a in cp["after"]:
                if a not in self._waited:
                    self._arrival(a, outs, sems, place).wait_recv()
                    self._waited.add(a)
            self._desc(k, ins, outs, sems, place).start()
        if s == 0:
            for n in range(len(self.locals)):
                self._local(n, ins, outs, sems, place).start(priority=LOCAL_DMA_PRIORITY)

    def drain(self, ins, outs, sems):
        place = _place()
        for k in range(len(self.copies)):
            if k not in self._waited:
                self._arrival(k, outs, sems, place).wait_recv()
        for k in range(len(self.copies)):
            self._desc(k, ins, outs, sems, place).wait_send()
        for n in range(len(self.locals)):
            self._local(n, ins, outs, sems, place).wait()

    def starts(self, step, nsteps, ins, outs, sems):
        for s in self.stages():
            pl.when(step == min(s, nsteps - 1))(functools.partial(self.stage, s, ins, outs, sems))

    def finish(self, step, nsteps, ins, outs, sems):
        pl.when(step == nsteps - 1)(functools.partial(self.drain, ins, outs, sems))


def _const_blk(j):
    return lambda me, sib, chips: j


def _carry_gather(carry, shards, relay_step, last_step):
    outs = []
    for sh in shards:
        i = carry.add_input(sh)
        o = carry.add_output((N_DEV,) + sh.shape, sh.dtype)
        half = sh.shape[0] // 2
        tile = 16 if sh.dtype == BF16 else 8
        split = half % tile == 0
        rows = [(0, half), (half, sh.shape[0] - half)] if split else [(0, sh.shape[0]), None]

        def whole(j, core):
            return lambda me, sib, chips, j=j, core=core: _block(*chips[j], me[2] if core == 0 else 1 - me[2])

        def part(j, core, h, rows=rows):
            return lambda me, sib, chips: (_block(*chips[j], me[2] if core == 0 else 1 - me[2]),) + rows[h]

        mine = lambda me, sib, chips: _block(*me)
        carry.local(i, o, mine)
        carry.copy(("in", i, None), o, mine, lambda me, sib, chips: _block(*sib), "sib")
        near = [carry.copy(("in", i, None), o, mine, whole(j, 0), j) for j in range(2)]
        for j in range(2):
            carry.copy(("out", o, whole(j, 0)), o, whole(j, 0), whole(j, 1), "sib", step=relay_step, after=(near[j],))
        for j in range(2):
            if rows[j] is None:
                continue
            far = carry.copy(("out", o, part(j, 0, j)), o, part(j, 0, j), part(2, 0, j), 1 - j,
                             step=relay_step, after=(near[j],))
            carry.copy(("out", o, part(2, 0, j)), o, part(2, 0, j), part(2, 1, j), "sib", step=last_step, after=(far,))
        outs.append(o)
    return outs


def _carry_pair(carry, grads):
    outs = []
    for g in grads:
        i = carry.add_input(g)
        o = carry.add_output((4,) + g.shape[1:], g.dtype)
        for j in range(4):
            if j < 3:
                blk = lambda me, sib, chips, j=j: _block(*chips[j], 1 - me[2])
            else:
                blk = lambda me, sib, chips: _block(*sib)
            carry.copy(("in", i, blk), o, _const_blk(j), _const_blk(j), "sib")
        outs.append(o)
    return outs


def _carry_chip(carry, parts):
    outs = []
    for p in parts:
        i = carry.add_input(p)
        o = carry.add_output((3,) + p.shape[1:], p.dtype)
        for j in range(3):
            carry.copy(("in", i, _const_blk(j)), o, _const_blk(j), _const_blk(j), j)
        outs.append(o)
    return outs


def _pcall(body, *, grid, in_specs, out_specs, out_shape, scratch_shapes, name, args, carry=None):
    sem = ("arbitrary",) * len(grid)
    if carry is None:
        res = pl.pallas_call(body, grid=grid, in_specs=in_specs, out_specs=out_specs, out_shape=out_shape,
                             scratch_shapes=scratch_shapes, compiler_params=_params(sem), name=name)(*args)
        return list(res), []
    n_in, n_out, n_scr = len(in_specs), len(out_specs), len(scratch_shapes)
    c_in, c_out = len(carry.inputs), len(carry.out_shapes)
    nsteps = 1
    for extent in grid:
        nsteps *= extent

    def wrapped(*refs):
        ins = refs[:n_in]
        cins = refs[n_in:n_in + c_in]
        o0 = n_in + c_in
        outs = refs[o0:o0 + n_out]
        couts = refs[o0 + n_out:o0 + n_out + c_out]
        s0 = o0 + n_out + c_out
        scr = refs[s0:s0 + n_scr]
        sems = refs[s0 + n_scr:]
        step = pl.program_id(0)
        for d in range(1, len(grid)):
            step = step * grid[d] + pl.program_id(d)
        carry.starts(step, nsteps, cins, couts, sems)
        body(*ins, *outs, *scr)
        carry.finish(step, nsteps, cins, couts, sems)

    any_spec = pl.BlockSpec(memory_space=pl.ANY)
    res = pl.pallas_call(
        wrapped, grid=grid,
        in_specs=list(in_specs) + [any_spec] * c_in,
        out_specs=list(out_specs) + [any_spec] * c_out,
        out_shape=list(out_shape) + carry.out_shapes,
        scratch_shapes=list(scratch_shapes) + carry.sem_shapes(),
        compiler_params=_params(sem, _collective_id(carry.peers())), name=name)(*args, *carry.inputs)
    return list(res[:n_out]), list(res[n_out:])


def _shifted_copies(buf, shifted, tm):
    span = tm + CONV_HALO - 8
    for r in range(1, 8):
        shifted[r - 1, 0:span, :] = buf[r:r + span, :]


def _rows_at(buf, shifted, start):
    aligned, r = (start // 8) * 8, start % 8
    if r == 0:
        return buf[aligned:aligned + ROW_CHUNK, :]
    return shifted[r - 1, aligned:aligned + ROW_CHUNK, :]


def _window_sums(buf, cols, work, levels, tm, trailing):
    src, src_cols = buf, cols
    for k in range(levels + 1):
        shift = 1 << k
        dst = work.at[k % 2]
        if trailing:
            lo = 8 * (k + 1)
            dst[lo:tm + POOL_HALO, :] = (src[lo:tm + POOL_HALO, src_cols]
                                         + src[lo - shift:tm + POOL_HALO - shift, src_cols])
        else:
            hi = tm + POOL_HALO - 8 * (k + 1)
            dst[0:hi, :] = src[0:hi, src_cols] + src[shift:hi + shift, src_cols]
        src, src_cols = dst, slice(0, POOL_GROUP)
    return src[POOL_HALO:POOL_HALO + tm, src_cols] if trailing else src[0:tm, src_cols]


def _row_spec(tm, width):
    return pl.BlockSpec((tm, width), lambda i: (i, 0))


def _const_spec(shape):
    return pl.BlockSpec(shape, lambda i: (0,) * len(shape))


def _weight_spec(shape):
    return pl.BlockSpec(shape, lambda i: (0,) * len(shape), pipeline_mode=pl.Buffered(1))


def _to_bf16(arrays):
    n = len(arrays)

    def body(*refs):
        for a in range(n):
            refs[n + a][...] = refs[a][...].astype(BF16)

    whole = [pl.BlockSpec(arr.shape, lambda i, nd=arr.ndim: (0,) * nd) for arr in arrays]
    return pl.pallas_call(
        body,
        grid=(1,),
        in_specs=whole,
        out_specs=whole,
        out_shape=[jax.ShapeDtypeStruct(arr.shape, BF16) for arr in arrays],
        compiler_params=_params(("arbitrary",)),
        name="to_bf16",
    )(*arrays)


def _norm_in(x, g_mix, tm, carry=None):
    s = x.shape[0]

    def body(x_ref, gmix_ref, h1_ref):
        h, _ = _rms_fwd(x_ref[...], gmix_ref[...])
        h1_ref[...] = h.astype(BF16)

    return _pcall(
        body,
        grid=(s // tm,),
        in_specs=[_row_spec(tm, D_MODEL), _const_spec((1, D_MODEL))],
        out_specs=[_row_spec(tm, D_MODEL)],
        out_shape=[jax.ShapeDtypeStruct((s, D_MODEL), BF16)],
        scratch_shapes=[],
        name="norm_in",
        args=(x, g_mix),
        carry=carry,
    )


def _fwd_mix(x, h1, w_in_t, w_out, conv_w, conv_b, ln_g, ln_b, pool_w, pool_scale, tm, carry=None):
    s = x.shape[0]
    nt = s // tm

    def body(x_ref, h1_ref, win_ref, wout_ref, cw_ref, cb_ref, lng_ref, lnb_ref, pw_ref, ps_ref,
             z_ref, u1_ref, pooled_ref, x1_ref, mix_ref, ubuf, vbuf, ush, pbuf):
        i = pl.program_id(0)

        @pl.when(i == 0)
        def _():
            ubuf[0:CONV_HALO, :] = jnp.zeros((CONV_HALO, C_CONV), F32)
            vbuf[0:POOL_HALO, :] = jnp.zeros((POOL_HALO, C_POOL), F32)

        xt = x_ref[...]
        z = _dot_nt(h1_ref[...], win_ref[...])
        z_ref[...] = z
        a = z[:, :C_CONV]
        b = z[:, C_CONV:2 * C_CONV]
        v = z[:, 2 * C_CONV:]
        ubuf[CONV_HALO:CONV_HALO + tm, :] = a * _sigmoid(b)
        vbuf[POOL_HALO:POOL_HALO + tm, :] = v

        _shifted_copies(ubuf, ush, tm)
        for rc in range(tm // ROW_CHUNK):
            base = rc * ROW_CHUNK + CONV_HALO - (CONV_K - 1)
            acc = jnp.broadcast_to(cb_ref[...], (ROW_CHUNK, C_CONV))
            for k in range(CONV_K):
                acc = acc + cw_ref[k:k + 1, :] * _rows_at(ubuf, ush, base + k)
            u1_ref[rc * ROW_CHUNK:(rc + 1) * ROW_CHUNK, :] = acc

        u1 = u1_ref[...]
        mu = jnp.mean(u1, axis=-1, keepdims=True)
        cen = u1 - mu
        rstd = lax.rsqrt(jnp.mean(cen * cen, axis=-1, keepdims=True) + EPS)
        u2 = cen * rstd * lng_ref[...] + lnb_ref[...]
        u = u2 * _sigmoid(u2)

        pos1 = (i * tm + lax.broadcasted_iota(jnp.int32, (tm, 1), 0) + 1).astype(F32)
        parts = [u]
        for g, w in enumerate(POOL_WINDOWS):
            cols = slice(g * POOL_GROUP, (g + 1) * POOL_GROUP)
            vg = v[:, cols]
            tot = _window_sums(vbuf, cols, pbuf, g, tm, trailing=True)
            pooled = tot / jnp.minimum(pos1, float(w)) - vg
            pooled_b = pooled.astype(BF16)
            pooled_ref[:, cols] = pooled_b
            parts.append(_dot(pooled_b, pw_ref[g]) * ps_ref[:, cols])
        mix = jnp.concatenate(parts, axis=-1).astype(BF16)
        mix_ref[...] = mix
        x1_ref[...] = xt + _dot(mix, wout_ref[...])

        ubuf[0:CONV_HALO, :] = ubuf[tm:tm + CONV_HALO, :]
        vbuf[0:POOL_HALO, :] = vbuf[tm:tm + POOL_HALO, :]

    return _pcall(
        body,
        grid=(nt,),
        in_specs=[
            _row_spec(tm, D_MODEL),
            _row_spec(tm, D_MODEL),
            _const_spec((Z_WIDTH, D_MODEL)),
            _const_spec((D_MODEL, D_MODEL)),
            _const_spec((CONV_HALO, C_CONV)),
            _const_spec((1, C_CONV)),
            _const_spec((1, C_CONV)),
            _const_spec((1, C_CONV)),
            _const_spec((len(POOL_WINDOWS), POOL_GROUP, POOL_GROUP)),
            _const_spec((1, C_POOL)),
        ],
        out_specs=[
            _row_spec(tm, Z_WIDTH),
            _row_spec(tm, C_CONV),
            _row_spec(tm, C_POOL),
            _row_spec(tm, D_MODEL),
            _row_spec(tm, D_MODEL),
        ],
        out_shape=[
            jax.ShapeDtypeStruct((s, Z_WIDTH), F32),
            jax.ShapeDtypeStruct((s, C_CONV), F32),
            jax.ShapeDtypeStruct((s, C_POOL), BF16),
            jax.ShapeDtypeStruct((s, D_MODEL), F32),
            jax.ShapeDtypeStruct((s, D_MODEL), BF16),
        ],
        scratch_shapes=[
            pltpu.VMEM((tm + CONV_HALO, C_CONV), F32),
            pltpu.VMEM((tm + POOL_HALO, C_POOL), F32),
            pltpu.VMEM((7, tm + CONV_HALO, C_CONV), F32),
            pltpu.VMEM((2, tm + POOL_HALO, POOL_GROUP), F32),
        ],
        name="fwd_mix",
        args=(x, h1, w_in_t, w_out, conv_w, conv_b, ln_g, ln_b, pool_w, pool_scale),
        carry=carry,
    )


def _ffn_up(x1, w_gu_t, g_ffn, tm, carry=None):
    s = x1.shape[0]

    def body(x1_ref, w_ref, gffn_ref, h2_ref, g_ref, u_ref, act_ref):
        h, _ = _rms_fwd(x1_ref[...], gffn_ref[...])
        h2 = h.astype(BF16)
        h2_ref[...] = h2
        for c in range(D_FF // FF_CHUNK):
            cols = slice(c * FF_CHUNK, (c + 1) * FF_CHUNK)
            g = _dot_nt(h2, w_ref[c * FF_CHUNK:(c + 1) * FF_CHUNK, :])
            u = _dot_nt(h2, w_ref[D_FF + c * FF_CHUNK:D_FF + (c + 1) * FF_CHUNK, :])
            g_ref[:, cols] = g.astype(BF16)
            u_ref[:, cols] = u.astype(BF16)
            act_ref[:, cols] = (g * _sigmoid(g) * u).astype(BF16)

    return _pcall(
        body,
        grid=(s // tm,),
        in_specs=[_row_spec(tm, D_MODEL), _weight_spec((2 * D_FF, D_MODEL)), _const_spec((1, D_MODEL))],
        out_specs=[_row_spec(tm, D_MODEL), _row_spec(tm, D_FF), _row_spec(tm, D_FF), _row_spec(tm, D_FF)],
        out_shape=[
            jax.ShapeDtypeStruct((s, D_MODEL), BF16),
            jax.ShapeDtypeStruct((s, D_FF), BF16),
            jax.ShapeDtypeStruct((s, D_FF), BF16),
            jax.ShapeDtypeStruct((s, D_FF), BF16),
        ],
        scratch_shapes=[],
        name="ffn_up",
        args=(x1, w_gu_t, g_ffn),
        carry=carry,
    )


def _down_ple(x1, act, w_down, p, tgt, w_pg, w_pu_t, g_gate, g_post, g_final, tm):
    s = x1.shape[0]
    nt = s // tm

    def body(x1_ref, act_ref, wd_ref, p_ref, t_ref, wpg_ref, wpu_ref, gg_ref, gp_ref, gf_ref,
             dx2_ref, dx2b_ref, hg_ref, ds_ref, dpe_ref, pb_ref, stats_ref, x2_cur, x2_next):
        i = pl.program_id(0)

        @pl.when(i == 0)
        def _():
            stats_ref[...] = jnp.zeros_like(stats_ref)
            x2_cur[...] = jnp.zeros((tm, D_MODEL), F32)

        def down(c):
            cols = slice(c * 256, (c + 1) * 256)
            x2_next[:, cols] = x1_ref[:, cols] + _dot(act_ref[...], wd_ref[:, cols])

        x2 = x2_cur[...]
        counts = i >= 1

        hg, rg = _rms_fwd(x2, gg_ref[...])
        hg_b = hg.astype(BF16)
        hg_ref[...] = hg_b
        down(0)
        gate = _sigmoid(_dot(hg_b, wpg_ref[...]))
        pb = p_ref[...].astype(BF16)
        pb_ref[...] = pb
        pe = _dot_nt(pb, wpu_ref[...])
        e, rp = _rms_fwd(pe, gp_ref[...])
        x3 = x2 + gate * e
        down(1)
        y, r3 = _rms_fwd(x3, gf_ref[...])
        diff = y - t_ref[...]
        loss = 0.5 * jnp.sum(jnp.sum(diff * diff, axis=-1, keepdims=True), axis=0, keepdims=True) / D_MODEL
        dy = diff * (1.0 / D_MODEL)

        dx3, dgf = _rms_bwd(x3, r3, gf_ref[...], dy)
        down(2)
        dpe, dgp = _rms_bwd(pe, rp, gp_ref[...], dx3 * gate)
        dpe_ref[...] = dpe.astype(BF16)
        ds = (dx3 * e * gate * (1.0 - gate)).astype(BF16)
        ds_ref[...] = ds
        dhg = _dot_nt(ds, wpg_ref[...])
        down(3)
        dxg, dgg = _rms_bwd(x2, rg, gg_ref[...], dhg)
        dx2 = dx3 + dxg
        dx2_ref[...] = dx2
        dx2b_ref[...] = dx2.astype(BF16)
        x2_cur[...] = x2_next[...]

        stats_ref[0:1, :] += jnp.where(counts, dgf, 0.0)
        stats_ref[1:2, :] += jnp.where(counts, dgp, 0.0)
        stats_ref[2:3, :] += jnp.where(counts, dgg, 0.0)
        stats_ref[3:4, :] += jnp.where(counts, jnp.broadcast_to(loss, (1, D_MODEL)), 0.0)

    ahead = lambda width: pl.BlockSpec((tm, width), lambda i: (jnp.minimum(i, nt - 1), 0))
    behind = lambda width: pl.BlockSpec((tm, width), lambda i: (jnp.maximum(i - 1, 0), 0))
    return pl.pallas_call(
        body,
        grid=(nt + 1,),
        in_specs=[
            ahead(D_MODEL),
            ahead(D_FF),
            _weight_spec((D_FF, D_MODEL)),
            behind(D_PLE),
            behind(D_MODEL),
            _weight_spec((D_MODEL, D_MODEL)),
            _weight_spec((D_MODEL, D_PLE)),
            _const_spec((1, D_MODEL)),
            _const_spec((1, D_MODEL)),
            _const_spec((1, D_MODEL)),
        ],
        out_specs=[
            behind(D_MODEL),
            behind(D_MODEL),
            behind(D_MODEL),
            behind(D_MODEL),
            behind(D_MODEL),
            behind(D_PLE),
            _const_spec((8, D_MODEL)),
        ],
        out_shape=[
            jax.ShapeDtypeStruct((s, D_MODEL), F32),
            jax.ShapeDtypeStruct((s, D_MODEL), BF16),
            jax.ShapeDtypeStruct((s, D_MODEL), BF16),
            jax.ShapeDtypeStruct((s, D_MODEL), BF16),
            jax.ShapeDtypeStruct((s, D_MODEL), BF16),
            jax.ShapeDtypeStruct((s, D_PLE), BF16),
            jax.ShapeDtypeStruct((8, D_MODEL), F32),
        ],
        scratch_shapes=[pltpu.VMEM((tm, D_MODEL), F32), pltpu.VMEM((tm, D_MODEL), F32)],
        compiler_params=_params(("arbitrary",)),
        name="down_ple",
    )(x1, act, w_down, p, tgt, w_pg, w_pu_t, g_gate, g_post, g_final)


def _ffn_bwd(dx2, dx2b, x1, g_sav, u_sav, w_gu_t, w_down, g_ffn, tm, carry=None):
    s = x1.shape[0]

    def body(dx2_ref, dx2b_ref, x1_ref, g_ref, u_ref, w_ref, wd_ref, gffn_ref,
             dg_ref, du_ref, dx1_ref, dx1b_ref, stats_ref):
        @pl.when(pl.program_id(0) == 0)
        def _():
            stats_ref[...] = jnp.zeros_like(stats_ref)

        dx2b = dx2b_ref[...]
        nc = D_FF // FF_CHUNK
        dacts = [_dot_nt(dx2b, wd_ref[c * FF_CHUNK:(c + 1) * FF_CHUNK, :]) for c in range(nc)]
        dh2 = jnp.zeros((tm, D_MODEL), F32)
        for c in range(nc):
            cols = slice(c * FF_CHUNK, (c + 1) * FF_CHUNK)
            g = g_ref[:, cols].astype(F32)
            u = u_ref[:, cols].astype(F32)
            sg = _sigmoid(g)
            dg = (dacts[c] * u * sg * (1.0 + g * (1.0 - sg))).astype(BF16)
            du = (dacts[c] * g * sg).astype(BF16)
            dg_ref[:, cols] = dg
            du_ref[:, cols] = du
            dh2 = dh2 + _dot(dg, w_ref[c * FF_CHUNK:(c + 1) * FF_CHUNK, :])
            dh2 = dh2 + _dot(du, w_ref[D_FF + c * FF_CHUNK:D_FF + (c + 1) * FF_CHUNK, :])

        x1 = x1_ref[...]
        r2 = lax.rsqrt(jnp.mean(x1 * x1, axis=-1, keepdims=True) + EPS)
        dxn, dgf = _rms_bwd(x1, r2, gffn_ref[...], dh2)
        dx1 = dx2_ref[...] + dxn
        dx1_ref[...] = dx1
        dx1b_ref[...] = dx1.astype(BF16)
        stats_ref[0:1, :] += dgf

    return _pcall(
        body,
        grid=(s // tm,),
        in_specs=[
            _row_spec(tm, D_MODEL), _row_spec(tm, D_MODEL), _row_spec(tm, D_MODEL),
            _row_spec(tm, D_FF), _row_spec(tm, D_FF),
            _weight_spec((2 * D_FF, D_MODEL)), _weight_spec((D_FF, D_MODEL)), _const_spec((1, D_MODEL)),
        ],
        out_specs=[_row_spec(tm, D_FF), _row_spec(tm, D_FF), _row_spec(tm, D_MODEL), _row_spec(tm, D_MODEL),
                   _const_spec((8, D_MODEL))],
        out_shape=[
            jax.ShapeDtypeStruct((s, D_FF), BF16),
            jax.ShapeDtypeStruct((s, D_FF), BF16),
            jax.ShapeDtypeStruct((s, D_MODEL), F32),
            jax.ShapeDtypeStruct((s, D_MODEL), BF16),
            jax.ShapeDtypeStruct((8, D_MODEL), F32),
        ],
        scratch_shapes=[],
        name="ffn_bwd",
        args=(dx2, dx2b, x1, g_sav, u_sav, w_gu_t, w_down, g_ffn),
        carry=carry,
    )


def _bwd_mix(dx1, dx1b, x, z, u1, pooled, w_in_t, w_out, g_mix, conv_w, ln_g, ln_b, pool_w, pool_scale, tm,
             carry=None):
    s = x.shape[0]
    nt = s // tm

    def body(dx1_ref, dx1b_ref, x_ref, z_ref, u1_ref, pooled_ref, win_ref, wout_ref, gmix_ref, cw_ref,
             lng_ref, lnb_ref, pw_ref, ps_ref,
             dx_ref, dz_ref, vec_ref, dcw_ref, dpw_ref, dubuf, dvbuf, u0buf, du0buf, dush, pbuf):
        i = pl.program_id(0)
        tile = nt - 1 - i

        @pl.when(i == 0)
        def _():
            vec_ref[...] = jnp.zeros_like(vec_ref)
            dcw_ref[...] = jnp.zeros_like(dcw_ref)
            dpw_ref[...] = jnp.zeros_like(dpw_ref)
            dubuf[tm:tm + CONV_HALO, :] = jnp.zeros((CONV_HALO, C_CONV), F32)
            dvbuf[tm:tm + POOL_HALO, :] = jnp.zeros((POOL_HALO, C_POOL), F32)

        dmix = _dot_nt(dx1b_ref[...], wout_ref[...])
        du = dmix[:, :C_CONV]
        dq = dmix[:, C_CONV:]

        pos1 = (tile * tm + lax.broadcasted_iota(jnp.int32, (tm, 1), 0) + 1).astype(F32)
        dpooled_parts = []
        dps_rows = []
        for g, w in enumerate(POOL_WINDOWS):
            cols = slice(g * POOL_GROUP, (g + 1) * POOL_GROUP)
            pooled_b = pooled_ref[:, cols]
            mixed = _dot(pooled_b, pw_ref[g])
            dqg = dq[:, cols]
            dps_rows.append(jnp.sum(dqg * mixed, axis=0, keepdims=True))
            dmixed = (dqg * ps_ref[:, cols]).astype(BF16)
            dpw_ref[g] += _dot_tn(pooled_b, dmixed)
            dpooled = _dot_nt(dmixed, pw_ref[g])
            dpooled_parts.append(dpooled)
            dvbuf[0:tm, cols] = dpooled / jnp.minimum(pos1, float(w))
        vec_ref[4:5, 0:C_POOL] += jnp.concatenate(dps_rows, axis=-1)
        dv_parts = []
        for g, w in enumerate(POOL_WINDOWS):
            cols = slice(g * POOL_GROUP, (g + 1) * POOL_GROUP)
            tot = _window_sums(dvbuf, cols, pbuf, g, tm, trailing=False)
            dv_parts.append(tot - dpooled_parts[g])

        u1 = u1_ref[...]
        mu = jnp.mean(u1, axis=-1, keepdims=True)
        cen = u1 - mu
        rstd = lax.rsqrt(jnp.mean(cen * cen, axis=-1, keepdims=True) + EPS)
        xhat = cen * rstd
        u2 = xhat * lng_ref[...] + lnb_ref[...]
        sg2 = _sigmoid(u2)
        du2 = du * sg2 * (1.0 + u2 * (1.0 - sg2))
        vec_ref[1:2, 0:C_CONV] += jnp.sum(du2 * xhat, axis=0, keepdims=True)
        vec_ref[2:3, 0:C_CONV] += jnp.sum(du2, axis=0, keepdims=True)
        t1 = du2 * lng_ref[...]
        du1 = rstd * (t1 - jnp.mean(t1, axis=-1, keepdims=True)
                      - xhat * jnp.mean(t1 * xhat, axis=-1, keepdims=True))
        vec_ref[3:4, 0:C_CONV] += jnp.sum(du1, axis=0, keepdims=True)
        dubuf[0:tm, :] = du1

        zt = z_ref[...]
        a = zt[:, :C_CONV]
        sgb = _sigmoid(zt[:, C_CONV:2 * C_CONV])
        u0buf[...] = a * sgb

        _shifted_copies(dubuf, dush, tm)
        for rc in range(tm // ROW_CHUNK):
            r0 = rc * ROW_CHUNK
            acc = jnp.zeros((ROW_CHUNK, C_CONV), F32)
            for k in range(CONV_K):
                acc = acc + cw_ref[k:k + 1, :] * _rows_at(dubuf, dush, r0 + (CONV_K - 1) - k)
            du0buf[r0:r0 + ROW_CHUNK, :] = acc
        for k in range(CONV_K):
            acc = jnp.zeros((ROW_CHUNK, C_CONV), F32)
            for rc in range(tm // ROW_CHUNK):
                r0 = rc * ROW_CHUNK
                acc = acc + u0buf[r0:r0 + ROW_CHUNK, :] * _rows_at(dubuf, dush, r0 + (CONV_K - 1) - k)
            dcw_ref[k:k + 1, :] += jnp.sum(acc, axis=0, keepdims=True)
        du0 = du0buf[...]

        da = du0 * sgb
        db = du0 * a * sgb * (1.0 - sgb)
        dz = jnp.concatenate([da, db] + dv_parts, axis=-1).astype(BF16)
        dz_ref[...] = dz

        xt = x_ref[...]
        r1 = lax.rsqrt(jnp.mean(xt * xt, axis=-1, keepdims=True) + EPS)
        dh1 = _dot(dz, win_ref[...])
        dxn, dgm = _rms_bwd(xt, r1, gmix_ref[...], dh1)
        dx_ref[...] = dx1_ref[...] + dxn
        vec_ref[0:1, :] += dgm

        dubuf[tm:tm + CONV_HALO, :] = dubuf[0:CONV_HALO, :]
        dvbuf[tm:tm + POOL_HALO, :] = dvbuf[0:POOL_HALO, :]

    rev = lambda width: pl.BlockSpec((tm, width), lambda i: (nt - 1 - i, 0))
    return _pcall(
        body,
        grid=(nt,),
        in_specs=[
            rev(D_MODEL), rev(D_MODEL), rev(D_MODEL), rev(Z_WIDTH), rev(C_CONV), rev(C_POOL),
            _const_spec((Z_WIDTH, D_MODEL)),
            _const_spec((D_MODEL, D_MODEL)),
            _const_spec((1, D_MODEL)),
            _const_spec((CONV_HALO, C_CONV)),
            _const_spec((1, C_CONV)),
            _const_spec((1, C_CONV)),
            _const_spec((len(POOL_WINDOWS), POOL_GROUP, POOL_GROUP)),
            _const_spec((1, C_POOL)),
        ],
        out_specs=[
            rev(D_MODEL), rev(Z_WIDTH),
            _const_spec((8, D_MODEL)),
            _const_spec((CONV_HALO, C_CONV)),
            _const_spec((len(POOL_WINDOWS), POOL_GROUP, POOL_GROUP)),
        ],
        out_shape=[
            jax.ShapeDtypeStruct((s, D_MODEL), F32),
            jax.ShapeDtypeStruct((s, Z_WIDTH), BF16),
            jax.ShapeDtypeStruct((8, D_MODEL), F32),
            jax.ShapeDtypeStruct((CONV_HALO, C_CONV), F32),
            jax.ShapeDtypeStruct((len(POOL_WINDOWS), POOL_GROUP, POOL_GROUP), F32),
        ],
        scratch_shapes=[
            pltpu.VMEM((tm + CONV_HALO, C_CONV), F32),
            pltpu.VMEM((tm + POOL_HALO, C_POOL), F32),
            pltpu.VMEM((tm, C_CONV), F32),
            pltpu.VMEM((tm, C_CONV), F32),
            pltpu.VMEM((7, tm + CONV_HALO, C_CONV), F32),
            pltpu.VMEM((2, tm + POOL_HALO, POOL_GROUP), F32),
        ],
        name="bwd_mix",
        args=(dx1, dx1b, x, z, u1, pooled, w_in_t, w_out, g_mix, conv_w, ln_g, ln_b, pool_w, pool_scale),
        carry=carry,
    )


def _grad_matmul(a, b, bm, name, a2=None, carry=None):
    s, ma = a.shape
    nb = b.shape[1]
    na = ma // bm
    if a2 is None:
        def body(a_ref, b_ref, o_ref):
            o_ref[...] = _dot_tn(a_ref[...], b_ref[...]).astype(BF16)

        lhs_specs = [pl.BlockSpec((s, bm), lambda i: (0, i))]
        lhs = (a,)
        steps = na
    else:
        def body(a_ref, a2_ref, b_ref, o_ref):
            i = pl.program_id(0)

            @pl.when(i < na)
            def _():
                o_ref[...] = _dot_tn(a_ref[...], b_ref[...]).astype(BF16)

            @pl.when(i >= na)
            def _():
                o_ref[...] = _dot_tn(a2_ref[...], b_ref[...]).astype(BF16)

        lhs_specs = [pl.BlockSpec((s, bm), lambda i: (0, jnp.minimum(i, na - 1))),
                     pl.BlockSpec((s, bm), lambda i: (0, jnp.maximum(i - na, 0)))]
        lhs = (a, a2)
        steps = 2 * na

    outs, carried = _pcall(
        body,
        grid=(steps,),
        in_specs=lhs_specs + [pl.BlockSpec((s, nb), lambda i: (0, 0))],
        out_specs=[pl.BlockSpec((bm, nb), lambda i: (i, 0))],
        out_shape=[jax.ShapeDtypeStruct((steps * bm, nb), BF16)],
        scratch_shapes=[],
        name=name,
        args=lhs + (b,),
        carry=carry,
    )
    return outs[0], carried


def _reduce_scatter(grads, small, name):
    n, ns = len(grads), len(small)
    shapes = [g.shape[1:] for g in grads]
    gather = _Carry()
    _carry_gather(gather, small, 1, 2)

    def body(*refs):
        g = refs[:n]
        out = refs[n + ns:2 * n + ns]
        scr = refs[2 * (n + ns):]
        own, loc, r1, r2 = scr[:n], scr[n:2 * n], scr[2 * n:3 * n], scr[3 * n:4 * n]
        load_sems, s1, q1, s2, q2 = scr[4 * n:4 * n + 5]
        gather_refs = (refs[n:n + ns], refs[2 * n + ns:2 * (n + ns)], scr[4 * n + 5:])
        me, sib, chips = _place()
        c = me[2]

        _handshake(EVERY_PEER)
        gather.stage(0, *gather_refs, handshake=False)
        loads = []
        sends = []
        for a in range(n):
            ld = [pltpu.make_async_copy(g[a].at[_block(*chip, c)], loc[a].at[j], load_sems.at[a, j])
                  for j, chip in enumerate(chips)]
            ld.append(pltpu.make_async_copy(g[a].at[_block(*me)], own[a], load_sems.at[a, 3]))
            for cp in ld:
                cp.start()
            loads.append(ld)
            blocks = [(*chip, 1 - c) for chip in chips] + [sib]
            for j, blk in enumerate(blocks):
                cp = pltpu.make_async_remote_copy(
                    src_ref=g[a].at[_block(*blk)], dst_ref=r1[a].at[j],
                    send_sem=s1.at[a, j], recv_sem=q1.at[a, j], device_id=sib, device_id_type=MESH)
                cp.start()
                sends.append(cp)

        def from_sibling(a, j):
            return pltpu.make_async_remote_copy(
                src_ref=r1[a].at[j], dst_ref=r1[a].at[j], send_sem=s1.at[a, j], recv_sem=q1.at[a, j],
                device_id=sib, device_id_type=MESH)

        def partial(a, j, chip):
            return pltpu.make_async_remote_copy(
                src_ref=loc[a].at[j], dst_ref=r2[a].at[j], send_sem=s2.at[a, j], recv_sem=q2.at[a, j],
                device_id=(*chip, c), device_id_type=MESH)

        gather.stage(1, *gather_refs)
        for a in range(n):
            for j, chip in enumerate(chips):
                loads[a][j].wait()
                from_sibling(a, j).wait_recv()
                loc[a][j] = (loc[a][j].astype(F32) + r1[a][j].astype(F32)).astype(BF16)
                cp = partial(a, j, chip)
                cp.start()
                sends.append(cp)
        gather.stage(2, *gather_refs)
        for a in range(n):
            loads[a][3].wait()
            from_sibling(a, 3).wait_recv()
            acc = own[a][...].astype(F32) + r1[a][3].astype(F32)
            for j, chip in enumerate(chips):
                partial(a, j, chip).wait_recv()
                acc = acc + r2[a][j].astype(F32)
            out[a][...] = acc
        for cp in sends:
            cp.wait_send()
        gather.drain(*gather_refs)

    any_spec = pl.BlockSpec(memory_space=pl.ANY)
    vmem_spec = pl.BlockSpec(memory_space=pltpu.VMEM)
    res = pl.pallas_call(
        body,
        in_specs=[any_spec] * (n + ns),
        out_specs=[vmem_spec] * n + [any_spec] * ns,
        out_shape=[jax.ShapeDtypeStruct(sh, F32) for sh in shapes] + gather.out_shapes,
        scratch_shapes=(
            [pltpu.VMEM(sh, BF16) for sh in shapes]
            + [pltpu.VMEM((3,) + sh, BF16) for sh in shapes]
            + [pltpu.VMEM((4,) + sh, BF16) for sh in shapes]
            + [pltpu.VMEM((3,) + sh, BF16) for sh in shapes]
            + [pltpu.SemaphoreType.DMA((n, 4)),
               pltpu.SemaphoreType.DMA((n, 4)), pltpu.SemaphoreType.DMA((n, 4)),
               pltpu.SemaphoreType.DMA((n, 3)), pltpu.SemaphoreType.DMA((n, 3))]
            + gather.sem_shapes()
        ),
        compiler_params=_params(collective_id=_collective_id(EVERY_PEER)),
        name=name,
    )(*grads, *small)
    return res[:n], res[n:]


def _pair_add(grads, from_sib, blks, name):
    n = len(grads)

    def body(blk_ref, *refs):
        for a in range(n):
            refs[2 * n + a][...] = (refs[a][...].astype(F32) + refs[n + a][...].astype(F32)).astype(BF16)

    mine = [pl.BlockSpec((None,) + g.shape[1:], lambda j, b: (b[j], 0, 0)) for g in grads]
    same = [pl.BlockSpec((None,) + g.shape[1:], lambda j, b: (j, 0, 0)) for g in grads]
    return pl.pallas_call(
        body,
        grid_spec=pltpu.PrefetchScalarGridSpec(
            num_scalar_prefetch=1, grid=(4,), in_specs=mine + same, out_specs=same),
        out_shape=[jax.ShapeDtypeStruct((4,) + g.shape[1:], BF16) for g in grads],
        compiler_params=_params(("arbitrary",)),
        name=name,
    )(blks, *grads, *from_sib)


def _chip_sum(parts, from_chips, name):
    n = len(parts)

    def body(*refs):
        for a in range(n):
            acc = refs[a][...].astype(F32)
            for j in range(3):
                acc = acc + refs[n + a][j].astype(F32)
            refs[2 * n + a][...] = acc

    half = [p.shape[1] // 2 for p in parts]
    return pl.pallas_call(
        body,
        grid=(2,),
        in_specs=([pl.BlockSpec((None, h, p.shape[2]), lambda i: (3, i, 0)) for p, h in zip(parts, half)]
                  + [pl.BlockSpec((3, h, p.shape[2]), lambda i: (0, i, 0)) for p, h in zip(parts, half)]),
        out_specs=[pl.BlockSpec((h, p.shape[2]), lambda i: (i, 0)) for p, h in zip(parts, half)],
        out_shape=[jax.ShapeDtypeStruct(p.shape[1:], F32) for p in parts],
        compiler_params=_params(("arbitrary",)),
        name=name,
    )(*parts, *from_chips)


def _adam_math(w, g, m, v):
    nm = ADAM_B1 * m + (1.0 - ADAM_B1) * g
    nv = ADAM_B2 * v + (1.0 - ADAM_B2) * (g * g)
    m_hat = nm / (1.0 - ADAM_B1 ** ADAM_STEP)
    v_hat = nv / (1.0 - ADAM_B2 ** ADAM_STEP)
    return -ADAM_LR * (m_hat / (jnp.sqrt(v_hat) + ADAM_EPS) + ADAM_WD * w), nm, nv


def _sum_adamw(items, name, carry=None):
    n = len(items)

    def body(*refs):
        for a in range(n):
            p_ref, f_ref, w_ref, m_ref, v_ref = refs[5 * a:5 * a + 5]
            g_ref, d_ref, nm_ref, nv_ref = refs[5 * n + 4 * a:5 * n + 4 * a + 4]
            g = p_ref[...].astype(F32)
            for j in range(3):
                g = g + f_ref[j].astype(F32)
            g_ref[...] = g
            d_ref[...], nm_ref[...], nv_ref[...] = _adam_math(w_ref[...], g, m_ref[...], v_ref[...])

    in_specs, out_specs, out_shape, args = [], [], [], []
    for part, from_chips, w, m, v in items:
        r, c = w.shape
        spec = pl.BlockSpec((r // 2, c), lambda i: (i, 0))
        in_specs += [pl.BlockSpec((None, r // 2, c), lambda i: (3, i, 0)),
                     pl.BlockSpec((3, r // 2, c), lambda i: (0, i, 0)), spec, spec, spec]
        out_specs += [spec] * 4
        out_shape += [jax.ShapeDtypeStruct((r, c), F32)] * 4
        args += [part, from_chips, w, m, v]
    outs, carried = _pcall(body, grid=(2,), in_specs=in_specs, out_specs=out_specs, out_shape=out_shape,
                           scratch_shapes=[], name=name, args=tuple(args), carry=carry)
    return [tuple(outs[4 * a:4 * a + 4]) for a in range(n)], carried


def _small_update(gathered, layout, params, name):
    ng, npar = len(gathered), len(params)

    def body(*refs):
        parts = refs[:ng]
        prm = refs[ng:ng + 3 * npar]
        tot_refs = refs[ng + 3 * npar:2 * ng + 3 * npar]
        out = refs[2 * ng + 3 * npar:]
        tots = []
        for a in range(ng):
            acc = parts[a][0].astype(F32)
            for d in range(1, N_DEV):
                acc = acc + parts[a][d].astype(F32)
            tot_refs[a][...] = acc
            tots.append(acc)
        for i, (a, row, width) in enumerate(layout):
            g = tots[a] if row is None else tots[a][row:row + 1, :width]
            delta, nm, nv = _adam_math(prm[3 * i][...], g, prm[3 * i + 1][...], prm[3 * i + 2][...])
            out[4 * i][...] = g
            out[4 * i + 1][...] = delta
            out[4 * i + 2][...] = nm
            out[4 * i + 3][...] = nv

    flat = [t for prm in params for t in prm]
    res = pl.pallas_call(
        body,
        out_shape=([jax.ShapeDtypeStruct(g.shape[1:], F32) for g in gathered]
                   + [jax.ShapeDtypeStruct(prm[0].shape, F32) for prm in params for _ in range(4)]),
        compiler_params=pltpu.CompilerParams(vmem_limit_bytes=V7X_VMEM_LIMIT),
        name=name,
    )(*gathered, *flat)
    return res[:ng], [tuple(res[ng + 4 * i:ng + 4 * i + 4]) for i in range(npar)]


def _adamw(w, g, m, v, name):
    rows, cols = w.shape
    br = rows
    for cand in (512, 256, 128):
        if rows % cand == 0 and rows > cand:
            br = cand
            break

    def body(w_ref, g_ref, m_ref, v_ref, d_ref, nm_ref, nv_ref):
        d_ref[...], nm_ref[...], nv_ref[...] = _adam_math(w_ref[...], g_ref[...], m_ref[...], v_ref[...])

    spec = pl.BlockSpec((br, cols), lambda i: (i, 0))
    shape = jax.ShapeDtypeStruct((rows, cols), F32)
    return pl.pallas_call(
        body,
        grid=(rows // br,),
        in_specs=[spec] * 4,
        out_specs=[spec] * 3,
        out_shape=[shape] * 3,
        compiler_params=_params(("arbitrary",)),
        name=name,
    )(w, g, m, v)


def _adamw_whole(w, g, m, v, name):
    def body(w_ref, g_ref, m_ref, v_ref, d_ref, nm_ref, nv_ref):
        d_ref[...], nm_ref[...], nv_ref[...] = _adam_math(w_ref[...], g_ref[...], m_ref[...], v_ref[...])

    return pl.pallas_call(
        body,
        out_shape=[jax.ShapeDtypeStruct(w.shape, F32)] * 3,
        name=name,
    )(w, g, m, v)


def _by_device(full):
    return full.reshape(N_DEV, full.shape[0] // N_DEV, full.shape[1])


def kernel(x, p, g_mix, w_in, conv_w, conv_b, ln_g, ln_b, pool_w, pool_scale, w_out, g_ffn, w_gate_up, w_down, g_ple_gate, w_ple_gate, w_ple_up, g_ple_post, g_final, loss_target, m_g_mix, m_w_in, m_conv_w, m_conv_b, m_ln_g, m_ln_b, m_pool_w, m_pool_scale, m_w_out, m_g_ffn, m_w_gate_up, m_w_down, m_g_ple_gate, m_w_ple_gate, m_w_ple_up, m_g_ple_post, m_g_final, v_g_mix, v_w_in, v_conv_w, v_conv_b, v_ln_g, v_ln_b, v_pool_w, v_pool_scale, v_w_out, v_g_ffn, v_w_gate_up, v_w_down, v_g_ple_gate, v_w_ple_gate, v_w_ple_up, v_g_ple_post, v_g_final):
    seq = x.shape[1]
    tile = min(TOKEN_TILE, seq)
    xs = x[0]
    ps = p[0, 0]
    tgt = loss_target[0]
    ax, ay, ac = lax.axis_index("x"), lax.axis_index("y"), lax.axis_index("c")
    me = _block(ax, ay, ac)
    blks = jnp.stack([_block(1 - ax, ay, ac), _block(ax, 1 - ay, ac), _block(1 - ax, 1 - ay, ac), me]).astype(jnp.int32)
    rows = lambda gth: gth.reshape((-1,) + gth.shape[2:])

    sh_in, sh_out, sh_gu, sh_down, sh_pg, sh_pu, pool_w_b = _to_bf16([
        w_in[0].T,
        w_out[0],
        w_gate_up[0].T,
        w_down[0],
        w_ple_gate[0],
        w_ple_up[0].T,
        pool_w[0]])
    carry = _Carry()
    _carry_gather(carry, [sh_in, sh_out,
                          jnp.pad(conv_w[0].T, ((0, 0), (0, CONV_HALO - CONV_K)))], 8, 15)
    (h1,), first = _norm_in(xs, g_mix, tile, carry)
    w_in_t, w_out_f, conv_w_t = [rows(gth) for gth in first]
    conv_w_f = conv_w_t.T

    carry = _Carry()
    _carry_gather(carry, [sh_gu], 8, 12)
    (z, u1, pooled, x1, mix), (w_gu_all,) = _fwd_mix(
        xs, h1, w_in_t, w_out_f, conv_w_f, conv_b, ln_g, ln_b, pool_w_b, pool_scale, tile, carry)
    w_gu_t = rows(w_gu_all)

    carry = _Carry()
    _carry_gather(carry, [sh_down, sh_pg, sh_pu], 6, 10)
    (h2, g_sav, u_sav, act), late = _ffn_up(x1, w_gu_t, g_ffn, tile, carry)
    w_down_f, w_pg_f, w_pu_t = [rows(gth) for gth in late]
    dx2, dx2b, hg, ds, dpe, pb, stats_ple = _down_ple(
        x1, act, w_down_f, ps, tgt, w_pg_f, w_pu_t, g_ple_gate, g_ple_post, g_final.reshape(1, D_MODEL),
        tile)

    d_w_down, _ = _grad_matmul(act, dx2b, GRAD_ROWS, "grad_w_down")
    d_w_pg, _ = _grad_matmul(hg, ds, GRAD_ROWS, "grad_w_ple_gate")
    d_w_pu_t, _ = _grad_matmul(dpe, pb, GRAD_ROWS, "grad_w_ple_up")
    early = [_by_device(d_w_pg), _by_device(d_w_pu_t), _by_device(d_w_down)]
    carry = _Carry()
    _carry_pair(carry, early)
    (dg, du, dx1, dx1b, stats_ffn), from_sib = _ffn_bwd(dx2, dx2b, x1, g_sav, u_sav, w_gu_t, w_down_f, g_ffn,
                                                        tile, carry)
    early_parts = _pair_add(early, from_sib, blks, "pair_add_early")

    carry = _Carry()
    _carry_chip(carry, early_parts)
    d_w_gu_t, early_chips = _grad_matmul(dg, h2, GRAD_ROWS, "grad_w_gate_up", a2=du, carry=carry)
    carry = _Carry()
    _carry_pair(carry, [_by_device(d_w_gu_t)])
    d_w_out, gu_sib = _grad_matmul(mix, dx1b, GRAD_ROWS, "grad_w_out", carry=carry)
    gu_parts = _pair_add([_by_device(d_w_gu_t)], gu_sib, blks, "pair_add_gate_up")

    carry = _Carry()
    _carry_chip(carry, gu_parts)
    _carry_pair(carry, [_by_device(d_w_out)])
    (dx, dz, vec_mix, dconv_w_part, dpool_w_part), carried = _bwd_mix(
        dx1, dx1b, xs, z, u1, pooled, w_in_t, w_out_f, g_mix, conv_w_f, ln_g, ln_b, pool_w_b, pool_scale,
        tile, carry)
    gu_chips, out_sib = carried[:1], carried[1:]
    out_parts = _pair_add([_by_device(d_w_out)], out_sib, blks, "pair_add_out")

    carry = _Carry()
    _carry_chip(carry, out_parts)
    d_w_in_t, out_chips = _grad_matmul(dz, h1, GRAD_ROWS, "grad_w_in", carry=carry)
    (gr_w_in_t,), small = _reduce_scatter(
        [_by_device(d_w_in_t)],
        [vec_mix, stats_ple, stats_ffn, dconv_w_part,
         dpool_w_part.astype(BF16).reshape(-1, POOL_GROUP)],
        "scatter_last")
    small = list(small[:4]) + [small[4].reshape((N_DEV,) + dpool_w_part.shape)]

    vec_names = ["g_mix", "ln_g", "ln_b", "conv_b", "pool_scale", "g_final", "g_ple_post", "g_ple_gate", "g_ffn",
                 "pool_w"]
    layout = [(0, 0, D_MODEL), (0, 1, C_CONV), (0, 2, C_CONV), (0, 3, C_CONV), (0, 4, C_POOL),
              (1, 0, D_MODEL), (1, 1, D_MODEL), (1, 2, D_MODEL), (2, 0, D_MODEL), (4, None, None)]
    as_row = lambda t: t.reshape(1, D_MODEL)
    params = [(g_mix, m_g_mix, v_g_mix), (ln_g, m_ln_g, v_ln_g), (ln_b, m_ln_b, v_ln_b),
              (conv_b, m_conv_b, v_conv_b), (pool_scale, m_pool_scale, v_pool_scale),
              (as_row(g_final), as_row(m_g_final), as_row(v_g_final)),
              (g_ple_post, m_g_ple_post, v_g_ple_post), (g_ple_gate, m_g_ple_gate, v_g_ple_gate),
              (g_ffn, m_g_ffn, v_g_ffn), (pool_w[0], m_pool_w[0], v_pool_w[0])]
    tots, small_upd = _small_update(small, layout, params, "small_update")
    loss = tots[1][3, 0]
    upd = {}
    for nm, res, prm in zip(vec_names, small_upd, [g_mix, ln_g, ln_b, conv_b, pool_scale, g_final, g_ple_post,
                                                    g_ple_gate, g_ffn, pool_w]):
        upd[nm] = tuple(t.reshape(prm.shape) for t in res)
    gr_conv_w = lax.dynamic_slice_in_dim(tots[3][:CONV_K], me * (C_CONV // N_DEV), C_CONV // N_DEV, axis=1)

    natural = lambda t: t[None]
    turned = lambda t: t.T[None]
    (res_pg, res_gu, res_out, res_down), _ = _sum_adamw([
        (early_parts[0], early_chips[0], w_ple_gate[0], m_w_ple_gate[0], v_w_ple_gate[0]),
        (gu_parts[0], gu_chips[0], w_gate_up[0].T, m_w_gate_up[0].T, v_w_gate_up[0].T),
        (out_parts[0], out_chips[0], w_out[0], m_w_out[0], v_w_out[0]),
        (early_parts[2], early_chips[2], w_down[0], m_w_down[0], v_w_down[0])], "adamw_big")
    upd["w_ple_gate"] = tuple(natural(t) for t in res_pg)
    upd["w_gate_up"] = tuple(turned(t) for t in res_gu)
    upd["w_out"] = tuple(natural(t) for t in res_out)
    upd["w_down"] = tuple(natural(t) for t in res_down)
    (gr_w_pu_t,) = _chip_sum(early_parts[1:2], early_chips[1:2], "chip_sum")
    plain = [
        ("w_in", w_in[0].T, gr_w_in_t, m_w_in[0].T, v_w_in[0].T, True),
        ("w_ple_up", w_ple_up[0], gr_w_pu_t.T, m_w_ple_up[0], v_w_ple_up[0], False),
    ]
    for nm, w_, g_, m_, v_, transposed in plain:
        res = (g_,) + tuple(_adamw(w_, g_, m_, v_, "adamw_" + nm))
        upd[nm] = tuple((t.T if transposed else t)[None] for t in res)
    taps = lambda t: t.transpose(1, 0, 2)
    g_taps = gr_conv_w[:, None, :]
    res = (g_taps,) + tuple(_adamw_whole(taps(conv_w), g_taps, taps(m_conv_w), taps(v_conv_w), "adamw_conv_w"))
    upd["conv_w"] = tuple(taps(t) for t in res)

    order = ["g_mix", "w_in", "conv_w", "conv_b", "ln_g", "ln_b", "pool_w", "pool_scale", "w_out", "g_ffn",
             "w_gate_up", "w_down", "g_ple_gate", "w_ple_gate", "w_ple_up", "g_ple_post", "g_final"]
    outs = [loss, dx[None]]
    for k in range(4):
        outs += [upd[nm][k] for nm in order]
    return tuple(outs)
```
